```python
import math
import jax
import jax.numpy as jnp
from jax import lax
import numpy as np

D_MODEL = 1024
BATCH = 8
SEQ = 2048
DEPTH = 2
DEC_BATCH = 128
DEC_SEQ = 8
PAST_LEN = 16384
PAGE_SIZE = 128

A_HEADS = 16
A_KV_HEADS = 2
A_HEAD_DIM = 64
A_GROUP = A_HEADS // A_KV_HEADS
WINDOW = 128
ROT_DIM = A_HEAD_DIM // 4
ROPE_THETA = 500000.0
NEG_INF = -1e30
B_HEADS = 4
B_DK = D_MODEL // 2 // B_HEADS
B_DV = D_MODEL // B_HEADS
B_GATE_RANK = 16
B_TAU = 16.0
GLA_CHUNK = 64
POOL_WINDOWS = (2, 4, 8, 16)
C_GROUPS = len(POOL_WINDOWS)
C_GROUP_W = D_MODEL // C_GROUPS
C_WIDTH = C_GROUPS * C_GROUP_W
POOL_STATE = max(POOL_WINDOWS) - 1
PEER_HEADS = 8
N_KEYS = 128
N_EXPERTS = N_KEYS * N_KEYS
PEER_TOPK = 16
PEER_DKEY = 256
PEER_DHALF = PEER_DKEY // 2
PEER_BLOCK = 128
DN_ALPHA = (2 * DEPTH) ** 0.25
DN_BETA = (8 * DEPTH) ** -0.25
LN_EPS = 1e-5
RMS_EPS = 1e-6

A_Q = A_HEADS * A_HEAD_DIM
A_KV = A_KV_HEADS * A_HEAD_DIM
B_QK = B_HEADS * B_DK
B_V = B_HEADS * B_DV
SPLITS = (A_Q, A_KV, A_KV, B_QK, B_QK, B_V, B_GATE_RANK, B_V, C_WIDTH, 3 * D_MODEL)
IN_WIDTH = sum(SPLITS)

kernel_name = 'hybrid_swa_gla_pool_peer_step'


def _layer_norm(x, g, b):
    xf = x.astype(jnp.float32)
    mu = jnp.mean(xf, -1, keepdims=True)
    var = jnp.mean(jnp.square(xf - mu), -1, keepdims=True)
    y = (xf - mu) * lax.rsqrt(var + LN_EPS) * g.astype(jnp.float32) + b.astype(jnp.float32)
    return y.astype(x.dtype)


def _partial_rope(x, pos):
    half = ROT_DIM // 2
    inv = ROPE_THETA ** (-jnp.arange(half, dtype=jnp.float32) / half)
    ang = pos.astype(jnp.float32)[:, None] * inv[None, :]
    cos = jnp.cos(ang)[:, None, :]
    sin = jnp.sin(ang)[:, None, :]
    xr = x[..., :ROT_DIM].astype(jnp.float32)
    x1, x2 = xr[..., :half], xr[..., half:]
    rot = jnp.concatenate([x1 * cos - x2 * sin, x2 * cos + x1 * sin], -1).astype(x.dtype)
    return jnp.concatenate([rot, x[..., ROT_DIM:]], -1)


def _sink_attention(q, k, v, q_pos, k_pos, sinks):
    s = jnp.einsum('bnqhgd,bnkhd->bnhgqk', q, k, preferred_element_type=jnp.float32) * (A_HEAD_DIM ** -0.5)
    kp = k_pos[:, None, :]
    qp = q_pos[:, :, None]
    ok = (kp <= qp) & (kp > qp - WINDOW) & (kp >= 0)
    s = jnp.where(ok[None, :, None, None], s, NEG_INF)
    sink = jnp.broadcast_to(sinks.astype(jnp.float32).reshape(1, 1, A_KV_HEADS, A_GROUP, 1, 1), s.shape[:-1] + (1,))
    p = jax.nn.softmax(jnp.concatenate([s, sink], -1), axis=-1)[..., :-1]
    return jnp.einsum('bnhgqk,bnkhd->bnqhgd', p.astype(v.dtype), v)


def _attn_prompt(q, k, v, sinks, buf_len):
    Bx, T = q.shape[:2]
    nb = T // WINDOW
    qb = q.reshape(Bx, nb, WINDOW, A_KV_HEADS, A_GROUP, A_HEAD_DIM)

    def banded(t):
        tb = t.reshape(Bx, nb, WINDOW, A_KV_HEADS, A_HEAD_DIM)
        prev = jnp.concatenate([jnp.zeros_like(tb[:, :1]), tb[:, :-1]], 1)
        return jnp.concatenate([prev, tb], 2)

    qpos = jnp.arange(T).reshape(nb, WINDOW)
    kpos = jnp.concatenate([qpos - WINDOW, qpos], 1)
    o = _sink_attention(qb, banded(k), banded(v), qpos, kpos, sinks)
    return o.reshape(Bx, T, A_Q), k[:, T - buf_len:], v[:, T - buf_len:]


def _attn_sample(q, k, v, sinks, k_buf, v_buf, pos0):
    Bx, T = q.shape[:2]
    L = k_buf.shape[1]
    kk = jnp.concatenate([k_buf, k], 1)
    vv = jnp.concatenate([v_buf, v], 1)
    qpos = (pos0 + jnp.arange(T))[None]
    kpos = (pos0 - L + jnp.arange(L + T))[None]
    qb = q.reshape(Bx, 1, T, A_KV_HEADS, A_GROUP, A_HEAD_DIM)
    o = _sink_attention(qb, kk[:, None], vv[:, None], qpos, kpos, sinks)
    return o.reshape(Bx, T, A_Q), kk[:, -L:], vv[:, -L:]


def _gla_chunked(q, k, v, log_a, s0):
    Bx, T, H, _ = q.shape
    c = math.gcd(T, GLA_CHUNK)
    n = T // c

    def to_chunks(t):
        return jnp.moveaxis(t.astype(jnp.float32).reshape(Bx, n, c, H, t.shape[-1]), 1, 0)

    causal = jnp.tril(jnp.ones((c, c), dtype=bool))

    def step(S, inp):
        qi, ki, vi, ai = inp
        b = jnp.cumsum(ai, axis=1)
        qd = qi * jnp.exp(b)
        kd = ki * jnp.exp(-b)
        o = jnp.einsum('bchk,bhkv->bchv', qd, S)
        att = jnp.where(causal, jnp.einsum('bihk,bjhk->bhij', qd, kd), 0.0)
        o = o + jnp.einsum('bhij,bjhv->bihv', att, vi)
        bl = b[:, -1]
        kl = ki * jnp.exp(bl[:, None] - b)
        S = jnp.exp(bl)[..., None] * S + jnp.einsum('bchk,bchv->bhkv', kl, vi)
        return S, o

    S, o = lax.scan(step, s0.astype(jnp.float32), (to_chunks(q), to_chunks(k), to_chunks(v), to_chunks(log_a)))
    o = jnp.moveaxis(o, 0, 1).reshape(Bx, T, H, v.shape[-1])
    return o, S.astype(s0.dtype)


def _pool_mix(u, prev, pos0, w_pool, pool_scale):
    Bx, T, _ = u.shape
    full = jnp.concatenate([prev.astype(u.dtype), u], 1)
    cs = jnp.cumsum(full.astype(jnp.float32), axis=1)
    cs = jnp.concatenate([jnp.zeros((Bx, 1, C_WIDTH), jnp.float32), cs], 1)
    end = POOL_STATE + jnp.arange(T) + 1
    pos = pos0 + jnp.arange(T)
    means = []
    for g, w in enumerate(POOL_WINDOWS):
        sl = slice(g * C_GROUP_W, (g + 1) * C_GROUP_W)
        wsum = cs[:, end, sl] - cs[:, end - w, sl]
        cnt = jnp.minimum(pos + 1, w).astype(jnp.float32)
        means.append(wsum / cnt[None, :, None])
    d = (jnp.concatenate(means, -1) - u.astype(jnp.float32)).astype(u.dtype)
    d = d.reshape(Bx, T, C_GROUPS, C_GROUP_W)
    y = jnp.einsum('btgc,gce->btge', d, w_pool).reshape(Bx, T, C_WIDTH) * pool_scale
    return y, full[:, -POOL_STATE:]


def _peer_ffn(x, w_query, sub_keys, u_tab, v_tab):
    Bx, T, D = x.shape
    n_tok = Bx * T
    pad = (-n_tok) % PEER_BLOCK
    xb = jnp.pad(x.reshape(n_tok, D), ((0, pad), (0, 0))).reshape(-1, PEER_BLOCK, D)

    def block(xi):
        q = jnp.einsum('td,dhk->thk', xi, w_query).reshape(PEER_BLOCK, PEER_HEADS, 2, PEER_DHALF)
        s = jnp.einsum('thpk,hpnk->thpn', q, sub_keys, preferred_element_type=jnp.float32)
        sv, si = lax.top_k(s, PEER_TOPK)
        cand = sv[:, :, 0, :, None] + sv[:, :, 1, None, :]
        cidx = si[:, :, 0, :, None] * N_KEYS + si[:, :, 1, None, :]
        cscore, ci = lax.top_k(cand.reshape(PEER_BLOCK, PEER_HEADS, PEER_TOPK * PEER_TOPK), PEER_TOPK)
        eidx = jnp.take_along_axis(cidx.reshape(PEER_BLOCK, PEER_HEADS, PEER_TOPK * PEER_TOPK), ci, axis=-1)
        g = jax.nn.softmax(cscore, axis=-1)
        u = jnp.take(u_tab, eidx, axis=0)
        h = jax.nn.gelu(jnp.einsum('thkd,td->thk', u, xi, preferred_element_type=jnp.float32), approximate=False)
        v = jnp.take(v_tab, eidx, axis=0)
        return jnp.einsum('thk,thkd->td', (g * h).astype(v.dtype), v)

    y = lax.map(block, xb).reshape(-1, D)[:n_tok]
    return y.reshape(Bx, T, D)


def _trunk_layer(x, win_k, win_v, gla_s, pool_prev, pos0, buf_len,
                 w_in, b_in, attn_sinks, w_alpha, b_alpha, gla_norm_g, w_pool, pool_scale,
                 w_branch_a, w_branch_b, w_branch_c, w_out, ln1_g, ln1_b,
                 peer_query, peer_subkeys, peer_u, peer_v, ln2_g, ln2_b):
    Bx, T, _ = x.shape
    pos = pos0 + jnp.arange(T)
    proj = jnp.einsum('btd,de->bte', x, w_in) + b_in
    cuts = [int(c) for c in np.cumsum(SPLITS)[:-1]]
    qa, ka, va, qb, kb, vb, lr, gb, uc, gates = jnp.split(proj, cuts, axis=-1)

    qa = _partial_rope(qa.reshape(Bx, T, A_HEADS, A_HEAD_DIM), pos)
    ka = _partial_rope(ka.reshape(Bx, T, A_KV_HEADS, A_HEAD_DIM), pos)
    va = va.reshape(Bx, T, A_KV_HEADS, A_HEAD_DIM)
    if win_k is None:
        oa, nk, nv = _attn_prompt(qa, ka, va, attn_sinks, buf_len)
    else:
        oa, nk, nv = _attn_sample(qa, ka, va, attn_sinks, win_k, win_v, pos0)

    log_a = jax.nn.log_sigmoid(jnp.einsum('btr,rk->btk', lr, w_alpha).astype(jnp.float32) + b_alpha.astype(jnp.float32)) / B_TAU
    s0 = jnp.zeros((Bx, B_HEADS, B_DK, B_DV), x.dtype) if gla_s is None else gla_s
    ob, ns = _gla_chunked(qb.reshape(Bx, T, B_HEADS, B_DK) * (B_DK ** -0.5), kb.reshape(Bx, T, B_HEADS, B_DK),
                          vb.reshape(Bx, T, B_HEADS, B_DV), log_a.reshape(Bx, T, B_HEADS, B_DK), s0)
    ob = ob * lax.rsqrt(jnp.mean(jnp.square(ob), -1, keepdims=True) + RMS_EPS) * gla_norm_g.astype(jnp.float32)
    ob = (ob.astype(x.dtype) * jax.nn.silu(gb.reshape(Bx, T, B_HEADS, B_DV))).reshape(Bx, T, B_V)

    prev = jnp.zeros((Bx, POOL_STATE, C_WIDTH), x.dtype) if pool_prev is None else pool_prev
    oc, npv = _pool_mix(uc, prev, pos0, w_pool, pool_scale)

    ga, gbr, gc = jnp.split(jax.nn.sigmoid(gates), 3, axis=-1)
    merged = (ga * jnp.einsum('bti,id->btd', oa, w_branch_a)
              + gbr * jnp.einsum('bti,id->btd', ob, w_branch_b)
              + gc * jnp.einsum('bti,id->btd', oc, w_branch_c))
    mix = jnp.einsum('btd,de->bte', merged, w_out)
    x = _layer_norm(DN_ALPHA * x + mix, ln1_g, ln1_b)

    x = _layer_norm(DN_ALPHA * x + _peer_ffn(x, peer_query, peer_subkeys, peer_u, peer_v), ln2_g, ln2_b)
    return x, nk, nv, ns, npv


def setup_inputs(seed: int = 0) -> dict:
    key = jax.random.key(seed)
    ks = jax.random.split(key, 26)
    f32 = jnp.float32

    def nrm(i, shape, scale):
        return jax.random.normal(ks[i], shape, f32) * scale

    wb = min(WINDOW, PAST_LEN)
    return {
        'x_prompt': nrm(0, (BATCH, SEQ, D_MODEL), 1.0),
        'x_sample': nrm(1, (DEC_BATCH, DEC_SEQ, D_MODEL), 1.0),
        'state_win_k': nrm(2, (DEPTH, DEC_BATCH, wb, A_KV_HEADS, A_HEAD_DIM), 1.0),
        'state_win_v': nrm(3, (DEPTH, DEC_BATCH, wb, A_KV_HEADS, A_HEAD_DIM), 1.0),
        'state_gla': nrm(4, (DEPTH, DEC_BATCH, B_HEADS, B_DK, B_DV), B_DK ** -0.5),
        'state_pool': nrm(5, (DEPTH, DEC_BATCH, POOL_STATE, C_WIDTH), 1.0),
        'w_in': nrm(6, (DEPTH, D_MODEL, IN_WIDTH), D_MODEL ** -0.5),
        'b_in': nrm(7, (DEPTH, IN_WIDTH), 0.02),
        'attn_sinks': nrm(8, (DEPTH, A_HEADS), 0.5),
        'w_alpha': nrm(9, (DEPTH, B_GATE_RANK, B_QK), B_GATE_RANK ** -0.5),
        'b_alpha': nrm(10, (DEPTH, B_QK), 0.1),
        'gla_norm_g': 1.0 + nrm(11, (DEPTH, B_HEADS, B_DV), 0.02),
        'w_pool': nrm(12, (DEPTH, C_GROUPS, C_GROUP_W, C_GROUP_W), C_GROUP_W ** -0.5),
        'pool_scale': 1.0 + nrm(13, (DEPTH, C_WIDTH), 0.05),
        'w_branch_a': nrm(14, (DEPTH, A_Q, D_MODEL), A_Q ** -0.5),
        'w_branch_b': nrm(15, (DEPTH, B_V, D_MODEL), B_V ** -0.5),
        'w_branch_c': nrm(16, (DEPTH, C_WIDTH, D_MODEL), C_WIDTH ** -0.5),
        'w_out': nrm(17, (DEPTH, D_MODEL, D_MODEL), DN_BETA * D_MODEL ** -0.5),
        'ln1_g': 1.0 + nrm(18, (DEPTH, D_MODEL), 0.01),
        'ln1_b': nrm(19, (DEPTH, D_MODEL), 0.01),
        'peer_query': nrm(20, (DEPTH, D_MODEL, PEER_HEADS, PEER_DKEY), D_MODEL ** -0.5),
        'peer_subkeys': nrm(21, (DEPTH, PEER_HEADS, 2, N_KEYS, PEER_DHALF), PEER_DHALF ** -0.5),
        'peer_u': nrm(22, (DEPTH, N_EXPERTS, D_MODEL), D_MODEL ** -0.5),
        'peer_v': nrm(23, (DEPTH, N_EXPERTS, D_MODEL), DN_BETA * PEER_HEADS ** -0.5),
        'ln2_g': 1.0 + nrm(24, (DEPTH, D_MODEL), 0.01),
        'ln2_b': nrm(25, (DEPTH, D_MODEL), 0.01),
    }


def reference(x_prompt, x_sample, state_win_k, state_win_v, state_gla, state_pool,
              w_in, b_in, attn_sinks, w_alpha, b_alpha, gla_norm_g, w_pool, pool_scale,
              w_branch_a, w_branch_b, w_branch_c, w_out, ln1_g, ln1_b,
              peer_query, peer_subkeys, peer_u, peer_v, ln2_g, ln2_b):
    buf_len = state_win_k.shape[2]
    yp, ys = x_prompt, x_sample
    pk, pv, pg, pp = [], [], [], []
    sk, sv, sg, sp = [], [], [], []
    for l in range(DEPTH):
        w = (w_in[l], b_in[l], attn_sinks[l], w_alpha[l], b_alpha[l], gla_norm_g[l], w_pool[l], pool_scale[l],
             w_branch_a[l], w_branch_b[l], w_branch_c[l], w_out[l], ln1_g[l], ln1_b[l],
             peer_query[l], peer_subkeys[l], peer_u[l], peer_v[l], ln2_g[l], ln2_b[l])
        yp, k1, v1, g1, q1 = _trunk_layer(yp, None, None, None, None, 0, buf_len, *w)
        ys, k2, v2, g2, q2 = _trunk_layer(ys, state_win_k[l], state_win_v[l], state_gla[l], state_pool[l],
                                          PAST_LEN, buf_len, *w)
        pk.append(k1); pv.append(v1); pg.append(g1); pp.append(q1)
        sk.append(k2); sv.append(v2); sg.append(g2); sp.append(q2)
    return (yp, ys,
            jnp.stack(pk), jnp.stack(pv), jnp.stack(pg), jnp.stack(pp),
            jnp.stack(sk), jnp.stack(sv), jnp.stack(sg), jnp.stack(sp))
```

```python
import functools
import math

import jax
import jax.numpy as jnp
import numpy as np
from jax import lax
from jax.experimental import pallas as pl
from jax.experimental.pallas import tpu as pltpu

F32 = jnp.float32
BF16 = jnp.bfloat16

D_MODEL = 1024
BATCH = 8
SEQ = 2048
DEPTH = 2
DEC_BATCH = 128
DEC_SEQ = 8
PAST_LEN = 16384

A_HEADS = 16
A_KV_HEADS = 2
A_HEAD_DIM = 64
A_GROUP = A_HEADS // A_KV_HEADS
WINDOW = 128
ROT_DIM = A_HEAD_DIM // 4
ROPE_THETA = 500000.0
NEG_INF = -1e30
B_HEADS = 4
B_DK = D_MODEL // 2 // B_HEADS
B_DV = D_MODEL // B_HEADS
B_GATE_RANK = 16
B_TAU = 16.0
GLA_CHUNK = 64
POOL_WINDOWS = (2, 4, 8, 16)
C_GROUPS = len(POOL_WINDOWS)
C_GROUP_W = D_MODEL // C_GROUPS
C_WIDTH = C_GROUPS * C_GROUP_W
POOL_STATE = max(POOL_WINDOWS) - 1
PEER_HEADS = 8
N_KEYS = 128
N_EXPERTS = N_KEYS * N_KEYS
PEER_TOPK = 16
PEER_DKEY = 256
PEER_DHALF = PEER_DKEY // 2
DN_ALPHA = (2 * DEPTH) ** 0.25
LN_EPS = 1e-5
RMS_EPS = 1e-6

A_Q = A_HEADS * A_HEAD_DIM
A_KV = A_KV_HEADS * A_HEAD_DIM
B_QK = B_HEADS * B_DK
B_V = B_HEADS * B_DV
SPLITS = (A_Q, A_KV, A_KV, B_QK, B_QK, B_V, B_GATE_RANK, B_V, C_WIDTH, 3 * D_MODEL)

LANES = 128
N_PROMPT = BATCH * SEQ
N_SAMPLE = DEC_BATCH * DEC_SEQ
N_TOK = N_PROMPT + N_SAMPLE

R_GATES = 0
R_VB = 3 * D_MODEL
R_GB = R_VB + B_V
R_UC = R_GB + B_V
R_QB = R_UC + C_WIDTH
R_KB = R_QB + B_QK
R_VA = R_KB + B_QK
R_LR = R_VA + A_KV
R_WIDTH = R_LR + LANES
QK_WIDTH = A_Q + A_KV

VMEM_LIMIT = 48 * 1024 * 1024

TM_QK = 512
TM_REST = 1024
TN_REST = 256
TM_MERGE = 256
TP_POOL = 512
TT_TOPK = 256
TT_PEER = 512
IB_PEER = 8
N_CAND = PEER_TOPK + 1


def _params(sem):
    return pltpu.CompilerParams(dimension_semantics=sem, vmem_limit_bytes=VMEM_LIMIT)


def _qk_kernel(x_ref, w_ref, b_ref, c_ref, s1_ref, s2_ref, o_ref):
    y = jnp.dot(x_ref[...], w_ref[...], preferred_element_type=F32) + b_ref[...]
    c = c_ref[...]
    s1 = s1_ref[...]
    s2 = s2_ref[...]
    for j in range(QK_WIDTH // LANES):
        yj = y[:, j * LANES:(j + 1) * LANES]
        up = pltpu.roll(yj, LANES - ROT_DIM // 2, axis=1)
        dn = pltpu.roll(yj, ROT_DIM // 2, axis=1)
        o_ref[:, j * LANES:(j + 1) * LANES] = yj * c + up * s1 + dn * s2


def _proj_qk(xb, w, b, rope_c, rope_s1, rope_s2):
    n_prompt_blocks = SEQ // TM_QK

    def tab_map(i):
        return (jnp.where(i < N_PROMPT // TM_QK, i % n_prompt_blocks, n_prompt_blocks), 0)

    tab_spec = pl.BlockSpec((TM_QK, LANES), tab_map)
    return pl.pallas_call(
        _qk_kernel,
        grid=(N_TOK // TM_QK,),
        in_specs=[
            pl.BlockSpec((TM_QK, D_MODEL), lambda i: (i, 0)),
            pl.BlockSpec((D_MODEL, QK_WIDTH), lambda i: (0, 0)),
            pl.BlockSpec((1, QK_WIDTH), lambda i: (0, 0)),
            tab_spec, tab_spec, tab_spec,
        ],
        out_specs=pl.BlockSpec((TM_QK, QK_WIDTH), lambda i: (i, 0)),
        out_shape=jax.ShapeDtypeStruct((N_TOK, QK_WIDTH), F32),
        compiler_params=_params(("parallel",)),
        name="proj_qk",
    )(xb, w, b, rope_c, rope_s1, rope_s2)


def _mm_bias_kernel(x_ref, w_ref, b_ref, o_ref):
    o_ref[...] = jnp.dot(x_ref[...], w_ref[...], preferred_element_type=F32) + b_ref[...]


def _proj_rest(xb, w, b):
    return pl.pallas_call(
        _mm_bias_kernel,
        grid=(N_TOK // TM_REST, R_WIDTH // TN_REST),
        in_specs=[
            pl.BlockSpec((TM_REST, D_MODEL), lambda i, j: (i, 0)),
            pl.BlockSpec((D_MODEL, TN_REST), lambda i, j: (0, j)),
            pl.BlockSpec((1, TN_REST), lambda i, j: (0, j)),
        ],
        out_specs=pl.BlockSpec((TM_REST, TN_REST), lambda i, j: (i, j)),
        out_shape=jax.ShapeDtypeStruct((N_TOK, R_WIDTH), F32),
        compiler_params=_params(("parallel", "arbitrary")),
        name="proj_rest",
    )(xb, w, b)


def _attend(q, kk, vv, sink_ref, c_min, o_ref, row0):
    tq = q.shape[0]
    r = lax.broadcasted_iota(jnp.int32, (tq, 2 * WINDOW), 0)
    c = lax.broadcasted_iota(jnp.int32, (tq, 2 * WINDOW), 1)
    ok = (c > r) & (c <= r + WINDOW) & (c >= c_min)
    qb = (q * (A_HEAD_DIM ** -0.5)).astype(BF16)
    for h in range(A_HEADS):
        g = h // A_GROUP
        qh = qb[:, h * A_HEAD_DIM:(h + 1) * A_HEAD_DIM]
        kg = kk[:, g * A_HEAD_DIM:(g + 1) * A_HEAD_DIM]
        vg = vv[:, g * A_HEAD_DIM:(g + 1) * A_HEAD_DIM]
        s = lax.dot_general(qh, kg, (((1,), (1,)), ((), ())), preferred_element_type=F32)
        s = jnp.where(ok, s, NEG_INF)
        sink = sink_ref[h]
        m = jnp.maximum(jnp.max(s, axis=1, keepdims=True), sink)
        p = jnp.exp(s - m)
        denom = jnp.sum(p, axis=1, keepdims=True) + jnp.exp(sink - m)
        o = jnp.dot(p.astype(BF16), vg, preferred_element_type=F32) / denom
        o_ref[pl.ds(row0, tq), h * A_HEAD_DIM:(h + 1) * A_HEAD_DIM] = o.astype(o_ref.dtype)


def _attn_prompt_kernel(sink_ref, q_ref, kc_ref, kp_ref, vc_ref, vp_ref, o_ref):
    n = pl.program_id(1)
    kk = jnp.concatenate([kp_ref[...], kc_ref[...]], axis=0).astype(BF16)
    vv = jnp.concatenate([vp_ref[...], vc_ref[...]], axis=0).astype(BF16)
    _attend(q_ref[...], kk, vv, sink_ref, jnp.where(n > 0, 0, WINDOW), o_ref, 0)


def _attn_prompt(qk, rest, sinks):
    nb = SEQ // WINDOW
    kcol = A_Q // A_KV
    vcol = R_VA // A_KV

    def cur(b, n):
        return b * nb + n

    def prev(b, n):
        return b * nb + jnp.maximum(n - 1, 0)

    return pl.pallas_call(
        _attn_prompt_kernel,
        grid=(BATCH, nb),
        in_specs=[
            pl.BlockSpec(memory_space=pltpu.SMEM),
            pl.BlockSpec((WINDOW, A_Q), lambda b, n: (cur(b, n), 0)),
            pl.BlockSpec((WINDOW, A_KV), lambda b, n: (cur(b, n), kcol)),
            pl.BlockSpec((WINDOW, A_KV), lambda b, n: (prev(b, n), kcol)),
            pl.BlockSpec((WINDOW, A_KV), lambda b, n: (cur(b, n), vcol)),
            pl.BlockSpec((WINDOW, A_KV), lambda b, n: (prev(b, n), vcol)),
        ],
        out_specs=pl.BlockSpec((WINDOW, A_Q), lambda b, n: (cur(b, n), 0)),
        out_shape=jax.ShapeDtypeStruct((N_PROMPT, A_Q), BF16),
        compiler_params=_params(("parallel", "arbitrary")),
        name="attn_prompt",
    )(sinks, qk, qk, qk, rest, rest)


BB_ATTN = 8


def _attn_sample_kernel(sink_ref, q_ref, kn_ref, vn_ref, ks_ref, vs_ref, o_ref):
    pad = jnp.zeros((WINDOW - DEC_SEQ, A_KV), F32)

    def body(bb, carry):
        row0 = pl.multiple_of(bb * DEC_SEQ, DEC_SEQ)
        q = q_ref[pl.ds(row0, DEC_SEQ), :]
        kk = jnp.concatenate([ks_ref[bb], kn_ref[pl.ds(row0, DEC_SEQ), :], pad], axis=0).astype(BF16)
        vv = jnp.concatenate([vs_ref[bb], vn_ref[pl.ds(row0, DEC_SEQ), :], pad], axis=0).astype(BF16)
        _attend(q, kk, vv, sink_ref, 0, o_ref, row0)
        return carry

    lax.fori_loop(0, BB_ATTN, body, 0)


def _attn_sample(qk, rest, sinks, k_state, v_state):
    rows = BB_ATTN * DEC_SEQ
    base = N_PROMPT // rows
    kcol = A_Q // A_KV
    vcol = R_VA // A_KV
    return pl.pallas_call(
        _attn_sample_kernel,
        grid=(DEC_BATCH // BB_ATTN,),
        in_specs=[
            pl.BlockSpec(memory_space=pltpu.SMEM),
            pl.BlockSpec((rows, A_Q), lambda i: (base + i, 0)),
            pl.BlockSpec((rows, A_KV), lambda i: (base + i, kcol)),
            pl.BlockSpec((rows, A_KV), lambda i: (base + i, vcol)),
            pl.BlockSpec((BB_ATTN, WINDOW, A_KV), lambda i: (i, 0, 0)),
            pl.BlockSpec((BB_ATTN, WINDOW, A_KV), lambda i: (i, 0, 0)),
        ],
        out_specs=pl.BlockSpec((rows, A_Q), lambda i: (i, 0)),
        out_shape=jax.ShapeDtypeStruct((N_SAMPLE, A_Q), BF16),
        compiler_params=_params(("parallel",)),
        name="attn_sample",
    )(sinks, qk, qk, rest, k_state, v_state)


def _split3(x):
    hi = x.astype(BF16)
    r1 = x - hi.astype(F32)
    mid = r1.astype(BF16)
    lo = (r1 - mid.astype(F32)).astype(BF16)
    return hi, mid, lo


def _gla_kernel(lr_ref, q_ref, k_ref, v_ref, gb_ref, s0_ref, wa_ref, ba_ref, g_ref,
                o_ref, sout_ref, st_ref, *, n_chunks):
    ci = pl.program_id(1)
    c = q_ref.shape[0]

    @pl.when(ci == 0)
    def _():
        for h in range(B_HEADS):
            st_ref[h] = s0_ref[0, h].T

    z = jnp.dot(lr_ref[...].astype(BF16), wa_ref[...], preferred_element_type=F32) + ba_ref[...]
    log_a = -(jnp.maximum(-z, 0.0) + jnp.log1p(jnp.exp(-jnp.abs(z)))) / B_TAU
    ri = lax.broadcasted_iota(jnp.int32, (c, c), 0)
    cj = lax.broadcasted_iota(jnp.int32, (c, c), 1)
    causal = cj <= ri
    tri = jnp.where(causal, 1.0, 0.0).astype(BF16)
    hi, mid, lo = _split3(log_a)
    b = (jnp.dot(tri, hi, preferred_element_type=F32)
         + jnp.dot(tri, mid, preferred_element_type=F32)
         + jnp.dot(tri, lo, preferred_element_type=F32))
    bl = b[c - 1:c, :]
    q = q_ref[...] * (B_DK ** -0.5)
    k = k_ref[...]
    qd = (q * jnp.exp(b)).astype(BF16)
    kd = (k * jnp.exp(-b)).astype(BF16)
    kl = (k * jnp.exp(bl - b)).astype(BF16)
    ebl = jnp.exp(bl)
    nt = (((1,), (1,)), ((), ()))
    for h in range(B_HEADS):
        ks = slice(h * B_DK, (h + 1) * B_DK)
        vs = slice(h * B_DV, (h + 1) * B_DV)
        vh = v_ref[:, vs]
        st = st_ref[h]
        o = lax.dot_general(qd[:, ks], st.astype(BF16), nt, preferred_element_type=F32)
        att = lax.dot_general(qd[:, ks], kd[:, ks], nt, preferred_element_type=F32)
        att = jnp.where(causal, att, 0.0)
        o = o + jnp.dot(att.astype(BF16), vh.astype(BF16), preferred_element_type=F32)
        st_ref[h] = st * ebl[:, ks] + jnp.dot(vh.T.astype(BF16), kl[:, ks], preferred_element_type=F32)
        o = o * lax.rsqrt(jnp.mean(o * o, axis=1, keepdims=True) + RMS_EPS) * g_ref[h:h + 1, :]
        gate = gb_ref[:, vs]
        o_ref[:, vs] = (o * (gate / (1.0 + jnp.exp(-gate)))).astype(o_ref.dtype)

    @pl.when(ci == n_chunks - 1)
    def _():
        for h in range(B_HEADS):
            sout_ref[0, h] = st_ref[h].T


def _gla(rest, s0, wa, ba, g, *, n_batch, seq, chunk, row_base, n_rows):
    n_chunks = seq // chunk
    base = row_base // chunk

    def row(b, ci):
        return base + b * n_chunks + ci

    return pl.pallas_call(
        functools.partial(_gla_kernel, n_chunks=n_chunks),
        grid=(n_batch, n_chunks),
        in_specs=[
            pl.BlockSpec((chunk, LANES), lambda b, ci: (row(b, ci), R_LR // LANES)),
            pl.BlockSpec((chunk, B_QK), lambda b, ci: (row(b, ci), R_QB // B_QK)),
            pl.BlockSpec((chunk, B_QK), lambda b, ci: (row(b, ci), R_KB // B_QK)),
            pl.BlockSpec((chunk, B_V), lambda b, ci: (row(b, ci), R_VB // B_V)),
            pl.BlockSpec((chunk, B_V), lambda b, ci: (row(b, ci), R_GB // B_V)),
            pl.BlockSpec((1, B_HEADS, B_DK, B_DV), lambda b, ci: (b, 0, 0, 0)),
            pl.BlockSpec((LANES, B_QK), lambda b, ci: (0, 0)),
            pl.BlockSpec((1, B_QK), lambda b, ci: (0, 0)),
            pl.BlockSpec((B_HEADS, B_DV), lambda b, ci: (0, 0)),
        ],
        out_specs=[
            pl.BlockSpec((chunk, B_V), lambda b, ci: (b * n_chunks + ci, 0)),
            pl.BlockSpec((1, B_HEADS, B_DK, B_DV), lambda b, ci: (b, 0, 0, 0)),
        ],
        out_shape=[
            jax.ShapeDtypeStruct((n_rows, B_V), BF16),
            jax.ShapeDtypeStruct((n_batch, B_HEADS, B_DK, B_DV), F32),
        ],
        scratch_shapes=[pltpu.VMEM((B_HEADS, B_DV, B_DK), F32)],
        compiler_params=_params(("parallel", "arbitrary")),
        name="gla",
    )(rest, rest, rest, rest, rest, s0, wa, ba, g)


HALO = 16


def _pool_kernel(u_ref, prev_ref, w_ref, scale_ref, o_ref, *, from_start):
    tp = u_ref.shape[0]
    u = u_ref[...]
    prev = prev_ref[...]
    if from_start:
        ti = pl.program_id(1)
        prev = jnp.where(ti > 0, prev, 0.0)
        t0 = ti * tp
    full = jnp.concatenate([prev, u], axis=0)
    for g, w in enumerate(POOL_WINDOWS):
        cs = slice(g * C_GROUP_W, (g + 1) * C_GROUP_W)
        acc = full[:, cs]
        span = 1
        while span < w:
            acc = acc + pltpu.roll(acc, span, axis=0)
            span *= 2
        wsum = acc[HALO:, :]
        if from_start:
            t = t0 + lax.broadcasted_iota(jnp.int32, (tp, C_GROUP_W), 0)
            cnt = jnp.minimum(t + 1, w).astype(F32)
        else:
            cnt = float(w)
        d = wsum / cnt - u[:, cs]
        y = jnp.dot(d.astype(BF16), w_ref[g], preferred_element_type=F32) * scale_ref[:, cs]
        o_ref[:, cs] = y.astype(o_ref.dtype)


def _pool_prompt(rest, w, scale):
    nt = SEQ // TP_POOL
    ucol = R_UC // C_WIDTH

    def halo(b, i):
        return (jnp.maximum((b * SEQ + i * TP_POOL) // HALO - 1, 0), ucol)

    return pl.pallas_call(
        functools.partial(_pool_kernel, from_start=True),
        grid=(BATCH, nt),
        in_specs=[
            pl.BlockSpec((TP_POOL, C_WIDTH), lambda b, i: (b * nt + i, ucol)),
            pl.BlockSpec((HALO, C_WIDTH), halo),
            pl.BlockSpec((C_GROUPS, C_GROUP_W, C_GROUP_W), lambda b, i: (0, 0, 0)),
            pl.BlockSpec((1, C_WIDTH), lambda b, i: (0, 0)),
        ],
        out_specs=pl.BlockSpec((TP_POOL, C_WIDTH), lambda b, i: (b * nt + i, 0)),
        out_shape=jax.ShapeDtypeStruct((N_PROMPT, C_WIDTH), BF16),
        compiler_params=_params(("parallel", "arbitrary")),
        name="pool_prompt",
    )(rest, rest, w, scale)


def _pool_sample(rest, prev, w, scale):
    ucol = R_UC // C_WIDTH
    base = N_PROMPT // DEC_SEQ
    return pl.pallas_call(
        functools.partial(_pool_kernel, from_start=False),
        grid=(DEC_BATCH,),
        in_specs=[
            pl.BlockSpec((DEC_SEQ, C_WIDTH), lambda b: (base + b, ucol)),
            pl.BlockSpec((HALO, C_WIDTH), lambda b: (b, 0)),
            pl.BlockSpec((C_GROUPS, C_GROUP_W, C_GROUP_W), lambda b: (0, 0, 0)),
            pl.BlockSpec((1, C_WIDTH), lambda b: (0, 0)),
        ],
        out_specs=pl.BlockSpec((DEC_SEQ, C_WIDTH), lambda b: (b, 0)),
        out_shape=jax.ShapeDtypeStruct((N_SAMPLE, C_WIDTH), BF16),
        compiler_params=_params(("parallel",)),
        name="pool_sample",
    )(rest, prev, w, scale)


def _layer_norm(x, g, b):
    mu = jnp.mean(x, axis=1, keepdims=True)
    xc = x - mu
    var = jnp.mean(xc * xc, axis=1, keepdims=True)
    return xc * lax.rsqrt(var + LN_EPS) * g + b


def _merge_kernel(x_ref, gates_ref, oa_ref, ob_ref, oc_ref, wa_ref, wb_ref, wc_ref, wo_ref,
                  g_ref, b_ref, o_ref):
    def gate(i):
        z = gates_ref[:, i * D_MODEL:(i + 1) * D_MODEL]
        return 1.0 / (1.0 + jnp.exp(-z))

    merged = (gate(0) * jnp.dot(oa_ref[...], wa_ref[...], preferred_element_type=F32)
              + gate(1) * jnp.dot(ob_ref[...], wb_ref[...], preferred_element_type=F32)
              + gate(2) * jnp.dot(oc_ref[...], wc_ref[...], preferred_element_type=F32))
    mix = jnp.dot(merged.astype(BF16), wo_ref[...], preferred_element_type=F32)
    o_ref[...] = _layer_norm(DN_ALPHA * x_ref[...] + mix, g_ref[...], b_ref[...])


def _merge(x, rest, oa, ob, oc, wa, wb, wc, wo, g, b):
    row = pl.BlockSpec((TM_MERGE, D_MODEL), lambda i: (i, 0))
    wspec = pl.BlockSpec((D_MODEL, D_MODEL), lambda i: (0, 0))
    vec = pl.BlockSpec((1, D_MODEL), lambda i: (0, 0))
    return pl.pallas_call(
        _merge_kernel,
        grid=(N_TOK // TM_MERGE,),
        in_specs=[row, pl.BlockSpec((TM_MERGE, 3 * D_MODEL), lambda i: (i, R_GATES)),
                  row, row, row, wspec, wspec, wspec, wspec, vec, vec],
        out_specs=row,
        out_shape=jax.ShapeDtypeStruct((N_TOK, D_MODEL), F32),
        compiler_params=_params(("parallel",)),
        name="merge",
    )(x, rest, oa, ob, oc, wa, wb, wc, wo, g, b)


def _extract_desc(s, n):
    vals = []
    for _ in range(n):
        m = jnp.max(s, axis=0, keepdims=True)
        vals.append(m)
        s = jnp.where(s == m, -jnp.inf, s)
    return vals


def _peer_topk_kernel(xT_ref, wq_ref, sk_ref, s2_ref, thr_ref, e1_ref, m2_ref, s_scr, sv_scr):
    tt = xT_ref.shape[1]
    xT = xT_ref[...]
    fill = jnp.full((24 - N_CAND, tt), -jnp.inf, F32)
    for hp in range(2 * PEER_HEADS):
        qT = jnp.dot(wq_ref[hp * PEER_DHALF:(hp + 1) * PEER_DHALF, :], xT, preferred_element_type=F32)
        s = jnp.dot(sk_ref[hp], qT.astype(BF16), preferred_element_type=F32)
        s_scr[hp] = s
        sv_scr[hp] = jnp.concatenate(_extract_desc(s, N_CAND) + [fill], axis=0)
    row8 = lax.broadcasted_iota(jnp.int32, (8, tt), 0)
    for h in range(PEER_HEADS):
        sv1 = sv_scr[2 * h]
        sv2 = sv_scr[2 * h + 1]
        pieces = [sv1[0:1, :] + sv2]
        for a in range(1, 8):
            nb = N_CAND // (a + 1)
            pieces.append(jnp.where(row8 < nb, sv1[a:a + 1, :] + sv2[0:8, :], -jnp.inf))
        pieces.append(sv2[0:1, :] + sv1[8:24, :])
        cand = _extract_desc(jnp.concatenate(pieces, axis=0), N_CAND)
        top = cand[0]
        z = jnp.zeros_like(top)
        for r in range(PEER_TOPK):
            z = z + jnp.exp(cand[r] - top)
        tau = 0.5 * (cand[PEER_TOPK - 1] + cand[PEER_TOPK])
        s1 = s_scr[2 * h]
        thr_ref[h] = tau - s1
        e1_ref[h] = jnp.exp(s1 - (sv1[0:1, :] + jnp.log(z)))
        s2_ref[h] = s_scr[2 * h + 1]
        m2_ref[h:h + 1, :] = sv2[0:1, :]


def _peer_topk(xT, wqT, sk):
    big = jax.ShapeDtypeStruct((PEER_HEADS, N_KEYS, N_TOK), F32)
    big_spec = pl.BlockSpec((PEER_HEADS, N_KEYS, TT_TOPK), lambda t: (0, 0, t))
    return pl.pallas_call(
        _peer_topk_kernel,
        grid=(N_TOK // TT_TOPK,),
        in_specs=[
            pl.BlockSpec((D_MODEL, TT_TOPK), lambda t: (0, t)),
            pl.BlockSpec((PEER_HEADS * PEER_DKEY, D_MODEL), lambda t: (0, 0)),
            pl.BlockSpec((2 * PEER_HEADS, N_KEYS, PEER_DHALF), lambda t: (0, 0, 0)),
        ],
        out_specs=[big_spec, big_spec, big_spec, pl.BlockSpec((PEER_HEADS, TT_TOPK), lambda t: (0, t))],
        out_shape=[big, big, big, jax.ShapeDtypeStruct((PEER_HEADS, N_TOK), F32)],
        scratch_shapes=[pltpu.VMEM((2 * PEER_HEADS, N_KEYS, TT_TOPK), F32),
                        pltpu.VMEM((2 * PEER_HEADS, 24, TT_TOPK), F32)],
        compiler_params=_params(("parallel",)),
        name="peer_topk",
    )(xT, wqT, sk)


def _gelu(x):
    return 0.5 * x * (1.0 + lax.erf(x * (2.0 ** -0.5)))


def _peer_main_kernel(xT_ref, u_ref, vt_ref, thr_ref, e1_ref, s2_ref, m2_ref, yT_ref, e2_scr, wh_scr):
    i = pl.program_id(1)
    tt = xT_ref.shape[1]

    @pl.when(i == 0)
    def _():
        yT_ref[...] = jnp.zeros_like(yT_ref)
        for h in range(PEER_HEADS):
            e2_scr[h] = jnp.exp(s2_ref[h] - m2_ref[h:h + 1, :])

    xT = xT_ref[...]
    for ii in range(IB_PEER):
        rows = slice(ii * N_KEYS, (ii + 1) * N_KEYS)
        s = jnp.dot(u_ref[rows, :], xT, preferred_element_type=F32)
        w = jnp.zeros((N_KEYS, tt), F32)
        for h in range(PEER_HEADS):
            thr = jnp.broadcast_to(thr_ref[h, ii:ii + 1, :], (N_KEYS, tt))
            e1 = jnp.broadcast_to(e1_ref[h, ii:ii + 1, :], (N_KEYS, tt))
            w = w + jnp.where(s2_ref[h] >= thr, e2_scr[h] * e1, 0.0)
        wh_scr[rows, :] = (w * _gelu(s)).astype(BF16)
    yT_ref[...] += jnp.dot(vt_ref[...], wh_scr[...], preferred_element_type=F32)


def _peer_main(xT, u, vt, thr, e1, s2, m2):
    eb = IB_PEER * N_KEYS
    return pl.pallas_call(
        _peer_main_kernel,
        grid=(N_TOK // TT_PEER, N_KEYS // IB_PEER),
        in_specs=[
            pl.BlockSpec((D_MODEL, TT_PEER), lambda t, i: (0, t)),
            pl.BlockSpec((eb, D_MODEL), lambda t, i: (i, 0)),
            pl.BlockSpec((D_MODEL, eb), lambda t, i: (0, i)),
            pl.BlockSpec((PEER_HEADS, IB_PEER, TT_PEER), lambda t, i: (0, i, t)),
            pl.BlockSpec((PEER_HEADS, IB_PEER, TT_PEER), lambda t, i: (0, i, t)),
            pl.BlockSpec((PEER_HEADS, N_KEYS, TT_PEER), lambda t, i: (0, 0, t)),
            pl.BlockSpec((PEER_HEADS, TT_PEER), lambda t, i: (0, t)),
        ],
        out_specs=pl.BlockSpec((D_MODEL, TT_PEER), lambda t, i: (0, t)),
        out_shape=jax.ShapeDtypeStruct((D_MODEL, N_TOK), F32),
        scratch_shapes=[pltpu.VMEM((PEER_HEADS, N_KEYS, TT_PEER), F32),
                        pltpu.VMEM((eb, TT_PEER), BF16)],
        compiler_params=_params(("parallel", "arbitrary")),
        name="peer_main",
    )(xT, u, vt, thr, e1, s2, m2)


def _peer_out_kernel(x_ref, yT_ref, g_ref, b_ref, o_ref):
    o_ref[...] = _layer_norm(DN_ALPHA * x_ref[...] + yT_ref[...].T, g_ref[...], b_ref[...])


def _peer_out(x1, yT, g, b):
    row = pl.BlockSpec((TM_MERGE, D_MODEL), lambda i: (i, 0))
    vec = pl.BlockSpec((1, D_MODEL), lambda i: (0, 0))
    return pl.pallas_call(
        _peer_out_kernel,
        grid=(N_TOK // TM_MERGE,),
        in_specs=[row, pl.BlockSpec((D_MODEL, TM_MERGE), lambda i: (0, i)), vec, vec],
        out_specs=row,
        out_shape=jax.ShapeDtypeStruct((N_TOK, D_MODEL), F32),
        compiler_params=_params(("parallel",)),
        name="peer_out",
    )(x1, yT, g, b)


def _rope_tables():
    half = ROT_DIM // 2
    pos = jnp.concatenate([jnp.arange(SEQ), PAST_LEN + (jnp.arange(TM_QK) % DEC_SEQ)])
    inv = ROPE_THETA ** (-jnp.arange(half, dtype=F32) / half)
    ang = pos.astype(F32)[:, None] * inv[None, :]
    cos, sin = jnp.cos(ang), jnp.sin(ang)
    n = pos.shape[0]
    one = jnp.ones((n, A_HEAD_DIM - ROT_DIM), F32)
    zero = jnp.zeros((n, A_HEAD_DIM - ROT_DIM), F32)
    zh = jnp.zeros((n, half), F32)
    reps = LANES // A_HEAD_DIM
    c = jnp.tile(jnp.concatenate([cos, cos, one], 1), (1, reps))
    s1 = jnp.tile(jnp.concatenate([-sin, zh, zero], 1), (1, reps))
    s2 = jnp.tile(jnp.concatenate([zh, sin, zero], 1), (1, reps))
    return c, s1, s2


def _split_cols(w):
    cuts = [int(c) for c in np.cumsum(SPLITS)[:-1]]
    return jnp.split(w, cuts, axis=-1)


def _layer(x, k_state, v_state, gla_state, pool_state, rope, w_in, b_in, sinks, w_alpha, b_alpha,
           gla_g, w_pool, pool_scale, w_a, w_b, w_c, w_out, ln1_g, ln1_b,
           peer_query, peer_subkeys, peer_u, peer_v, ln2_g, ln2_b):
    qa_w, ka_w, va_w, qb_w, kb_w, vb_w, lr_w, gb_w, uc_w, gates_w = _split_cols(w_in)
    qa_b, ka_b, va_b, qb_b, kb_b, vb_b, lr_b, gb_b, uc_b, gates_b = _split_cols(b_in[None, :])
    lr_pad = LANES - B_GATE_RANK
    w_qk = jnp.concatenate([qa_w, ka_w], 1).astype(BF16)
    b_qk = jnp.concatenate([qa_b, ka_b], 1)
    w_rest = jnp.concatenate([gates_w, vb_w, gb_w, uc_w, qb_w, kb_w, va_w,
                              jnp.pad(lr_w, ((0, 0), (0, lr_pad)))], 1).astype(BF16)
    b_rest = jnp.concatenate([gates_b, vb_b, gb_b, uc_b, qb_b, kb_b, va_b,
                              jnp.pad(lr_b, ((0, 0), (0, lr_pad)))], 1)

    xb = x.astype(BF16)
    qk = _proj_qk(xb, w_qk, b_qk, *rope)
    rest = _proj_rest(xb, w_rest, b_rest)

    ks = k_state.reshape(DEC_BATCH, WINDOW, A_KV)
    vs = v_state.reshape(DEC_BATCH, WINDOW, A_KV)
    oa = jnp.concatenate([_attn_prompt(qk, rest, sinks), _attn_sample(qk, rest, sinks, ks, vs)], 0)

    wa = jnp.pad(w_alpha, ((0, lr_pad), (0, 0))).astype(BF16)
    ba = b_alpha[None, :]
    ob_p, gla_p = _gla(rest, jnp.zeros((BATCH, B_HEADS, B_DK, B_DV), F32), wa, ba, gla_g,
                       n_batch=BATCH, seq=SEQ, chunk=GLA_CHUNK, row_base=0, n_rows=N_PROMPT)
    ob_s, gla_s = _gla(rest, gla_state, wa, ba, gla_g,
                       n_batch=DEC_BATCH, seq=DEC_SEQ, chunk=math.gcd(DEC_SEQ, GLA_CHUNK),
                       row_base=N_PROMPT, n_rows=N_SAMPLE)
    ob = jnp.concatenate([ob_p, ob_s], 0)

    wp = w_pool.astype(BF16)
    ps = pool_scale[None, :]
    prev = jnp.pad(pool_state, ((0, 0), (HALO - POOL_STATE, 0), (0, 0))).reshape(DEC_BATCH * HALO, C_WIDTH)
    oc = jnp.concatenate([_pool_prompt(rest, wp, ps), _pool_sample(rest, prev, wp, ps)], 0)

    x1 = _merge(x, rest, oa, ob, oc, w_a.astype(BF16), w_b.astype(BF16), w_c.astype(BF16),
                w_out.astype(BF16), ln1_g[None, :], ln1_b[None, :])

    x1T = x1.T.astype(BF16)
    wqT = peer_query.reshape(D_MODEL, PEER_HEADS * PEER_DKEY).T.astype(BF16)
    sk = peer_subkeys.reshape(2 * PEER_HEADS, N_KEYS, PEER_DHALF).astype(BF16)
    s2, thr, e1, m2 = _peer_topk(x1T, wqT, sk)
    yT = _peer_main(x1T, peer_u.astype(BF16), peer_v.T.astype(BF16), thr, e1, s2, m2)
    x2 = _peer_out(x1, yT, ln2_g[None, :], ln2_b[None, :])

    k_new = qk[:, A_Q:]
    v_new = rest[:, R_VA:R_VA + A_KV]
    u_new = rest[:, R_UC:R_UC + C_WIDTH]

    def prompt_tail(t, n):
        return t[:N_PROMPT].reshape(BATCH, SEQ, -1)[:, SEQ - n:]

    def sample_tail(state, t, n):
        new = t[N_PROMPT:].reshape(DEC_BATCH, DEC_SEQ, -1)
        return jnp.concatenate([state, new], 1)[:, -n:]

    kv_shape = (-1, WINDOW, A_KV_HEADS, A_HEAD_DIM)
    states = (prompt_tail(k_new, WINDOW).reshape(kv_shape), prompt_tail(v_new, WINDOW).reshape(kv_shape),
              gla_p, prompt_tail(u_new, POOL_STATE),
              sample_tail(ks, k_new, WINDOW).reshape(kv_shape), sample_tail(vs, v_new, WINDOW).reshape(kv_shape),
              gla_s, sample_tail(pool_state, u_new, POOL_STATE))
    return x2, states


def kernel(x_prompt, x_sample, state_win_k, state_win_v, state_gla, state_pool, w_in, b_in, attn_sinks,
           w_alpha, b_alpha, gla_norm_g, w_pool, pool_scale, w_branch_a, w_branch_b, w_branch_c, w_out,
           ln1_g, ln1_b, peer_query, peer_subkeys, peer_u, peer_v, ln2_g, ln2_b):
    x = jnp.concatenate([x_prompt.reshape(N_PROMPT, D_MODEL), x_sample.reshape(N_SAMPLE, D_MODEL)], 0)
    rope = _rope_tables()
    per_layer = []
    for l in range(DEPTH):
        x, states = _layer(x, state_win_k[l], state_win_v[l], state_gla[l], state_pool[l], rope,
                           w_in[l], b_in[l], attn_sinks[l], w_alpha[l], b_alpha[l], gla_norm_g[l],
                           w_pool[l], pool_scale[l], w_branch_a[l], w_branch_b[l], w_branch_c[l], w_out[l],
                           ln1_g[l], ln1_b[l], peer_query[l], peer_subkeys[l], peer_u[l], peer_v[l],
                           ln2_g[l], ln2_b[l])
        per_layer.append(states)
    stacked = [jnp.stack([per_layer[l][i] for l in range(DEPTH)]) for i in range(8)]
    return (x[:N_PROMPT].reshape(BATCH, SEQ, D_MODEL), x[N_PROMPT:].reshape(DEC_BATCH, DEC_SEQ, D_MODEL),
            *stacked)
```

```python
import functools
import math

import jax
import jax.numpy as jnp
import numpy as np
from jax import lax
from jax.experimental import pallas as pl
from jax.experimental.pallas import tpu as pltpu

F32 = jnp.float32
BF16 = jnp.bfloat16

D_MODEL = 1024
BATCH = 8
SEQ = 2048
DEPTH = 2
DEC_BATCH = 128
DEC_SEQ = 8
PAST_LEN = 16384

A_HEADS = 16
A_KV_HEADS = 2
A_HEAD_DIM = 64
A_GROUP = A_HEADS // A_KV_HEADS
WINDOW = 128
ROT_DIM = A_HEAD_DIM // 4
ROPE_THETA = 500000.0
NEG_INF = -1e30
B_HEADS = 4
B_DK = D_MODEL // 2 // B_HEADS
B_DV = D_MODEL // B_HEADS
B_GATE_RANK = 16
B_TAU = 16.0
GLA_CHUNK = 64
POOL_WINDOWS = (2, 4, 8, 16)
C_GROUPS = len(POOL_WINDOWS)
C_GROUP_W = D_MODEL // C_GROUPS
C_WIDTH = C_GROUPS * C_GROUP_W
POOL_STATE = max(POOL_WINDOWS) - 1
PEER_HEADS = 8
N_KEYS = 128
N_EXPERTS = N_KEYS * N_KEYS
PEER_TOPK = 16
PEER_DKEY = 256
PEER_DHALF = PEER_DKEY // 2
DN_ALPHA = (2 * DEPTH) ** 0.25
LN_EPS = 1e-5
RMS_EPS = 1e-6

A_Q = A_HEADS * A_HEAD_DIM
A_KV = A_KV_HEADS * A_HEAD_DIM
B_QK = B_HEADS * B_DK
B_V = B_HEADS * B_DV
SPLITS = (A_Q, A_KV, A_KV, B_QK, B_QK, B_V, B_GATE_RANK, B_V, C_WIDTH, 3 * D_MODEL)

LANES = 128
N_PROMPT = BATCH * SEQ
N_SAMPLE = DEC_BATCH * DEC_SEQ
N_TOK = N_PROMPT + N_SAMPLE

R_GATES = 0
R_VB = 3 * D_MODEL
R_GB = R_VB + B_V
R_UC = R_GB + B_V
R_QB = R_UC + C_WIDTH
R_KB = R_QB + B_QK
R_VA = R_KB + B_QK
R_LR = R_VA + A_KV
R_WIDTH = R_LR + LANES
QK_WIDTH = A_Q + A_KV

VMEM_LIMIT = 48 * 1024 * 1024

TM_QK = 512
TM_REST = 512
TN_REST = R_WIDTH // 2
TM_MERGE = 256
TP_POOL = 512
TT_TOPK = 256
TT_PEER = 512
IB_PEER = 8


def _params(sem):
    return pltpu.CompilerParams(dimension_semantics=sem, vmem_limit_bytes=VMEM_LIMIT)


def _qk_kernel(x_ref, w_ref, b_ref, c_ref, s1_ref, s2_ref, o_ref):
    y = jnp.dot(x_ref[...], w_ref[...], preferred_element_type=F32) + b_ref[...]
    c = c_ref[...]
    s1 = s1_ref[...]
    s2 = s2_ref[...]
    for j in range(QK_WIDTH // LANES):
        yj = y[:, j * LANES:(j + 1) * LANES]
        up = pltpu.roll(yj, LANES - ROT_DIM // 2, axis=1)
        dn = pltpu.roll(yj, ROT_DIM // 2, axis=1)
        o_ref[:, j * LANES:(j + 1) * LANES] = yj * c + up * s1 + dn * s2


def _proj_qk(xb, w, b, rope_c, rope_s1, rope_s2):
    n_prompt_blocks = SEQ // TM_QK

    def tab_map(i):
        return (jnp.where(i < N_PROMPT // TM_QK, i % n_prompt_blocks, n_prompt_blocks), 0)

    tab_spec = pl.BlockSpec((TM_QK, LANES), tab_map)
    return pl.pallas_call(
        _qk_kernel,
        grid=(N_TOK // TM_QK,),
        in_specs=[
            pl.BlockSpec((TM_QK, D_MODEL), lambda i: (i, 0)),
            pl.BlockSpec((D_MODEL, QK_WIDTH), lambda i: (0, 0)),
            pl.BlockSpec((1, QK_WIDTH), lambda i: (0, 0)),
            tab_spec, tab_spec, tab_spec,
        ],
        out_specs=pl.BlockSpec((TM_QK, QK_WIDTH), lambda i: (i, 0)),
        out_shape=jax.ShapeDtypeStruct((N_TOK, QK_WIDTH), F32),
        compiler_params=_params(("parallel",)),
        name="proj_qk",
    )(xb, w, b, rope_c, rope_s1, rope_s2)


def _mm_bias_kernel(x_ref, w_ref, b_ref, o_ref):
    o_ref[...] = jnp.dot(x_ref[...], w_ref[...], preferred_element_type=F32) + b_ref[...]


def _proj_rest(xb, w, b):
    return pl.pallas_call(
        _mm_bias_kernel,
        grid=(R_WIDTH // TN_REST, N_TOK // TM_REST),
        in_specs=[
            pl.BlockSpec((TM_REST, D_MODEL), lambda j, i: (i, 0)),
            pl.BlockSpec((D_MODEL, TN_REST), lambda j, i: (0, j)),
            pl.BlockSpec((1, TN_REST), lambda j, i: (0, j)),
        ],
        out_specs=pl.BlockSpec((TM_REST, TN_REST), lambda j, i: (i, j)),
        out_shape=jax.ShapeDtypeStruct((N_TOK, R_WIDTH), F32),
        compiler_params=_params(("parallel", "arbitrary")),
        name="proj_rest",
    )(xb, w, b)


def _attend(q, kk, vv, sink_ref, c_min, o_ref, row0):
    tq = q.shape[0]
    r = lax.broadcasted_iota(jnp.int32, (tq, 2 * WINDOW), 0)
    c = lax.broadcasted_iota(jnp.int32, (tq, 2 * WINDOW), 1)
    ok = (c > r) & (c <= r + WINDOW) & (c >= c_min)
    qb = (q * (A_HEAD_DIM ** -0.5)).astype(BF16)
    for h in range(A_HEADS):
        g = h // A_GROUP
        qh = qb[:, h * A_HEAD_DIM:(h + 1) * A_HEAD_DIM]
        kg = kk[:, g * A_HEAD_DIM:(g + 1) * A_HEAD_DIM]
        vg = vv[:, g * A_HEAD_DIM:(g + 1) * A_HEAD_DIM]
        s = lax.dot_general(qh, kg, (((1,), (1,)), ((), ())), preferred_element_type=F32)
        s = jnp.where(ok, s, NEG_INF)
        sink = sink_ref[h]
        m = jnp.maximum(jnp.max(s, axis=1, keepdims=True), sink)
        p = jnp.exp(s - m)
        denom = jnp.sum(p, axis=1, keepdims=True) + jnp.exp(sink - m)
        o = jnp.dot(p.astype(BF16), vg, preferred_element_type=F32) / denom
        o_ref[pl.ds(row0, tq), h * A_HEAD_DIM:(h + 1) * A_HEAD_DIM] = o.astype(o_ref.dtype)


def _attn_prompt_kernel(sink_ref, q_ref, kc_ref, kp_ref, vc_ref, vp_ref, o_ref):
    n = pl.program_id(1)
    kk = jnp.concatenate([kp_ref[...], kc_ref[...]], axis=0).astype(BF16)
    vv = jnp.concatenate([vp_ref[...], vc_ref[...]], axis=0).astype(BF16)
    _attend(q_ref[...], kk, vv, sink_ref, jnp.where(n > 0, 0, WINDOW), o_ref, 0)


def _attn_prompt(qk, rest, sinks):
    nb = SEQ // WINDOW
    kcol = A_Q // A_KV
    vcol = R_VA // A_KV

    def cur(b, n):
        return b * nb + n

    def prev(b, n):
        return b * nb + jnp.maximum(n - 1, 0)

    return pl.pallas_call(
        _attn_prompt_kernel,
        grid=(BATCH, nb),
        in_specs=[
            pl.BlockSpec(memory_space=pltpu.SMEM),
            pl.BlockSpec((WINDOW, A_Q), lambda b, n: (cur(b, n), 0)),
            pl.BlockSpec((WINDOW, A_KV), lambda b, n: (cur(b, n), kcol)),
            pl.BlockSpec((WINDOW, A_KV), lambda b, n: (prev(b, n), kcol)),
            pl.BlockSpec((WINDOW, A_KV), lambda b, n: (cur(b, n), vcol)),
            pl.BlockSpec((WINDOW, A_KV), lambda b, n: (prev(b, n), vcol)),
        ],
        out_specs=pl.BlockSpec((WINDOW, A_Q), lambda b, n: (cur(b, n), 0)),
        out_shape=jax.ShapeDtypeStruct((N_PROMPT, A_Q), BF16),
        compiler_params=_params(("parallel", "arbitrary")),
        name="attn_prompt",
    )(sinks, qk, qk, qk, rest, rest)


BB_ATTN = 8


def _attn_sample_kernel(sink_ref, q_ref, kn_ref, vn_ref, ks_ref, vs_ref, o_ref):
    pad = jnp.zeros((WINDOW - DEC_SEQ, A_KV), F32)
    rows = A_GROUP * DEC_SEQ
    t = lax.broadcasted_iota(jnp.int32, (rows, 2 * WINDOW), 0) % DEC_SEQ
    c = lax.broadcasted_iota(jnp.int32, (rows, 2 * WINDOW), 1)
    ok = (c > t) & (c <= t + WINDOW)
    nt = (((1,), (1,)), ((), ()))

    def body(bb, carry):
        row0 = pl.multiple_of(bb * DEC_SEQ, DEC_SEQ)
        q = q_ref[pl.ds(row0, DEC_SEQ), :] * (A_HEAD_DIM ** -0.5)
        kk = jnp.concatenate([ks_ref[bb], kn_ref[pl.ds(row0, DEC_SEQ), :], pad], axis=0).astype(BF16)
        vv = jnp.concatenate([vs_ref[bb], vn_ref[pl.ds(row0, DEC_SEQ), :], pad], axis=0).astype(BF16)
        for g in range(A_KV_HEADS):
            heads = range(g * A_GROUP, (g + 1) * A_GROUP)
            qg = jnp.concatenate([q[:, h * A_HEAD_DIM:(h + 1) * A_HEAD_DIM] for h in heads], axis=0)
            ds = slice(g * A_HEAD_DIM, (g + 1) * A_HEAD_DIM)
            s = lax.dot_general(qg.astype(BF16), kk[:, ds], nt, preferred_element_type=F32)
            s = jnp.where(ok, s, NEG_INF)
            sink = sink_ref[g * rows:(g + 1) * rows, 0:1]
            m = jnp.maximum(jnp.max(s, axis=1, keepdims=True), sink)
            p = jnp.exp(s - m)
            denom = jnp.sum(p, axis=1, keepdims=True) + jnp.exp(sink - m)
            o = jnp.dot(p.astype(BF16), vv[:, ds], preferred_element_type=F32) / denom
            for k, h in enumerate(heads):
                o_ref[pl.ds(row0, DEC_SEQ), h * A_HEAD_DIM:(h + 1) * A_HEAD_DIM] = (
                    o[k * DEC_SEQ:(k + 1) * DEC_SEQ, :].astype(o_ref.dtype))
        return carry

    lax.fori_loop(0, BB_ATTN, body, 0)


def _attn_sample(qk, rest, sinks, k_state, v_state):
    rows = BB_ATTN * DEC_SEQ
    base = N_PROMPT // rows
    kcol = A_Q // A_KV
    vcol = R_VA // A_KV
    sink_rows = jnp.broadcast_to(jnp.repeat(sinks, DEC_SEQ)[:, None], (A_HEADS * DEC_SEQ, LANES))
    return pl.pallas_call(
        _attn_sample_kernel,
        grid=(DEC_BATCH // BB_ATTN,),
        in_specs=[
            pl.BlockSpec((A_HEADS * DEC_SEQ, LANES), lambda i: (0, 0)),
            pl.BlockSpec((rows, A_Q), lambda i: (base + i, 0)),
            pl.BlockSpec((rows, A_KV), lambda i: (base + i, kcol)),
            pl.BlockSpec((rows, A_KV), lambda i: (base + i, vcol)),
            pl.BlockSpec((BB_ATTN, WINDOW, A_KV), lambda i: (i, 0, 0)),
            pl.BlockSpec((BB_ATTN, WINDOW, A_KV), lambda i: (i, 0, 0)),
        ],
        out_specs=pl.BlockSpec((rows, A_Q), lambda i: (i, 0)),
        out_shape=jax.ShapeDtypeStruct((N_SAMPLE, A_Q), BF16),
        compiler_params=_params(("parallel",)),
        name="attn_sample",
    )(sink_rows, qk, qk, rest, k_state, v_state)


def _split3(x):
    hi = x.astype(BF16)
    r1 = x - hi.astype(F32)
    mid = r1.astype(BF16)
    lo = (r1 - mid.astype(F32)).astype(BF16)
    return hi, mid, lo


def _gla_kernel(lr_ref, q_ref, k_ref, v_ref, gb_ref, s0_ref, wa_ref, ba_ref, g_ref,
                o_ref, sout_ref, st_ref, *, n_chunks):
    ci = pl.program_id(1)
    c = q_ref.shape[0]

    @pl.when(ci == 0)
    def _():
        for h in range(B_HEADS):
            st_ref[h] = s0_ref[0, h].T

    z = jnp.dot(lr_ref[...].astype(BF16), wa_ref[...], preferred_element_type=F32) + ba_ref[...]
    log_a = -(jnp.maximum(-z, 0.0) + jnp.log1p(jnp.exp(-jnp.abs(z)))) / B_TAU
    ri = lax.broadcasted_iota(jnp.int32, (c, c), 0)
    cj = lax.broadcasted_iota(jnp.int32, (c, c), 1)
    causal = cj <= ri
    tri = jnp.where(causal, 1.0, 0.0).astype(BF16)
    hi, mid, lo = _split3(log_a)
    b = (jnp.dot(tri, hi, preferred_element_type=F32)
         + jnp.dot(tri, mid, preferred_element_type=F32)
         + jnp.dot(tri, lo, preferred_element_type=F32))
    bl = b[c - 1:c, :]
    q = q_ref[...] * (B_DK ** -0.5)
    k = k_ref[...]
    qd = (q * jnp.exp(b)).astype(BF16)
    kd = (k * jnp.exp(-b)).astype(BF16)
    kl = (k * jnp.exp(bl - b)).astype(BF16)
    ebl = jnp.exp(bl)
    nt = (((1,), (1,)), ((), ()))
    for h in range(B_HEADS):
        ks = slice(h * B_DK, (h + 1) * B_DK)
        vs = slice(h * B_DV, (h + 1) * B_DV)
        vh = v_ref[:, vs]
        st = st_ref[h]
        o = lax.dot_general(qd[:, ks], st.astype(BF16), nt, preferred_element_type=F32)
        att = lax.dot_general(qd[:, ks], kd[:, ks], nt, preferred_element_type=F32)
        att = jnp.where(causal, att, 0.0)
        o = o + jnp.dot(att.astype(BF16), vh.astype(BF16), preferred_element_type=F32)
        st_ref[h] = st * ebl[:, ks] + jnp.dot(vh.T.astype(BF16), kl[:, ks], preferred_element_type=F32)
        o = o * lax.rsqrt(jnp.mean(o * o, axis=1, keepdims=True) + RMS_EPS) * g_ref[h:h + 1, :]
        gate = gb_ref[:, vs]
        o_ref[:, vs] = (o * (gate / (1.0 + jnp.exp(-gate)))).astype(o_ref.dtype)

    @pl.when(ci == n_chunks - 1)
    def _():
        for h in range(B_HEADS):
            sout_ref[0, h] = st_ref[h].T


def _gla(rest, s0, wa, ba, g, *, n_batch, seq, chunk, row_base, n_rows):
    n_chunks = seq // chunk
    base = row_base // chunk

    def row(b, ci):
        return base + b * n_chunks + ci

    return pl.pallas_call(
        functools.partial(_gla_kernel, n_chunks=n_chunks),
        grid=(n_batch, n_chunks),
        in_specs=[
            pl.BlockSpec((chunk, LANES), lambda b, ci: (row(b, ci), R_LR // LANES)),
            pl.BlockSpec((chunk, B_QK), lambda b, ci: (row(b, ci), R_QB // B_QK)),
            pl.BlockSpec((chunk, B_QK), lambda b, ci: (row(b, ci), R_KB // B_QK)),
            pl.BlockSpec((chunk, B_V), lambda b, ci: (row(b, ci), R_VB // B_V)),
            pl.BlockSpec((chunk, B_V), lambda b, ci: (row(b, ci), R_GB // B_V)),
            pl.BlockSpec((1, B_HEADS, B_DK, B_DV), lambda b, ci: (b, 0, 0, 0)),
            pl.BlockSpec((LANES, B_QK), lambda b, ci: (0, 0)),
            pl.BlockSpec((1, B_QK), lambda b, ci: (0, 0)),
            pl.BlockSpec((B_HEADS, B_DV), lambda b, ci: (0, 0)),
        ],
        out_specs=[
            pl.BlockSpec((chunk, B_V), lambda b, ci: (b * n_chunks + ci, 0)),
            pl.BlockSpec((1, B_HEADS, B_DK, B_DV), lambda b, ci: (b, 0, 0, 0)),
        ],
        out_shape=[
            jax.ShapeDtypeStruct((n_rows, B_V), BF16),
            jax.ShapeDtypeStruct((n_batch, B_HEADS, B_DK, B_DV), F32),
        ],
        scratch_shapes=[pltpu.VMEM((B_HEADS, B_DV, B_DK), F32)],
        compiler_params=_params(("parallel", "arbitrary")),
        name="gla",
    )(rest, rest, rest, rest, rest, s0, wa, ba, g)


HALO = 16


def _pool_kernel(u_ref, prev_ref, w_ref, scale_ref, o_ref, *, from_start):
    tp = u_ref.shape[0]
    u = u_ref[...]
    prev = prev_ref[...]
    if from_start:
        ti = pl.program_id(1)
        prev = jnp.where(ti > 0, prev, 0.0)
        t0 = ti * tp
    full = jnp.concatenate([prev, u], axis=0)
    for g, w in enumerate(POOL_WINDOWS):
        cs = slice(g * C_GROUP_W, (g + 1) * C_GROUP_W)
        acc = full[:, cs]
        span = 1
        while span < w:
            acc = acc + pltpu.roll(acc, span, axis=0)
            span *= 2
        wsum = acc[HALO:, :]
        if from_start:
            t = t0 + lax.broadcasted_iota(jnp.int32, (tp, C_GROUP_W), 0)
            cnt = jnp.minimum(t + 1, w).astype(F32)
        else:
            cnt = float(w)
        d = wsum / cnt - u[:, cs]
        y = jnp.dot(d.astype(BF16), w_ref[g], preferred_element_type=F32) * scale_ref[:, cs]
        o_ref[:, cs] = y.astype(o_ref.dtype)


def _pool_prompt(rest, w, scale):
    nt = SEQ // TP_POOL
    ucol = R_UC // C_WIDTH

    def halo(b, i):
        return (jnp.maximum((b * SEQ + i * TP_POOL) // HALO - 1, 0), ucol)

    return pl.pallas_call(
        functools.partial(_pool_kernel, from_start=True),
        grid=(BATCH, nt),
        in_specs=[
            pl.BlockSpec((TP_POOL, C_WIDTH), lambda b, i: (b * nt + i, ucol)),
            pl.BlockSpec((HALO, C_WIDTH), halo),
            pl.BlockSpec((C_GROUPS, C_GROUP_W, C_GROUP_W), lambda b, i: (0, 0, 0)),
            pl.BlockSpec((1, C_WIDTH), lambda b, i: (0, 0)),
        ],
        out_specs=pl.BlockSpec((TP_POOL, C_WIDTH), lambda b, i: (b * nt + i, 0)),
        out_shape=jax.ShapeDtypeStruct((N_PROMPT, C_WIDTH), BF16),
        compiler_params=_params(("parallel", "arbitrary")),
        name="pool_prompt",
    )(rest, rest, w, scale)


def _pool_sample(rest, prev, w, scale):
    ucol = R_UC // C_WIDTH
    base = N_PROMPT // DEC_SEQ
    return pl.pallas_call(
        functools.partial(_pool_kernel, from_start=False),
        grid=(DEC_BATCH,),
        in_specs=[
            pl.BlockSpec((DEC_SEQ, C_WIDTH), lambda b: (base + b, ucol)),
            pl.BlockSpec((HALO, C_WIDTH), lambda b: (b, 0)),
            pl.BlockSpec((C_GROUPS, C_GROUP_W, C_GROUP_W), lambda b: (0, 0, 0)),
            pl.BlockSpec((1, C_WIDTH), lambda b: (0, 0)),
        ],
        out_specs=pl.BlockSpec((DEC_SEQ, C_WIDTH), lambda b: (b, 0)),
        out_shape=jax.ShapeDtypeStruct((N_SAMPLE, C_WIDTH), BF16),
        compiler_params=_params(("parallel",)),
        name="pool_sample",
    )(rest, prev, w, scale)


def _layer_norm(x, g, b):
    mu = jnp.mean(x, axis=1, keepdims=True)
    xc = x - mu
    var = jnp.mean(xc * xc, axis=1, keepdims=True)
    return xc * lax.rsqrt(var + LN_EPS) * g + b


def _merge_kernel(x_ref, gates_ref, oa_ref, ob_ref, oc_ref, wa_ref, wb_ref, wc_ref, wo_ref,
                  g_ref, b_ref, o_ref):
    def gate(i):
        z = gates_ref[:, i * D_MODEL:(i + 1) * D_MODEL]
        return 1.0 / (1.0 + jnp.exp(-z))

    merged = (gate(0) * jnp.dot(oa_ref[...], wa_ref[...], preferred_element_type=F32)
              + gate(1) * jnp.dot(ob_ref[...], wb_ref[...], preferred_element_type=F32)
              + gate(2) * jnp.dot(oc_ref[...], wc_ref[...], preferred_element_type=F32))
    mix = jnp.dot(merged.astype(BF16), wo_ref[...], preferred_element_type=F32)
    o_ref[...] = _layer_norm(DN_ALPHA * x_ref[...] + mix, g_ref[...], b_ref[...])


def _merge(x, rest, oa, ob, oc, wa, wb, wc, wo, g, b):
    row = pl.BlockSpec((TM_MERGE, D_MODEL), lambda i: (i, 0))
    wspec = pl.BlockSpec((D_MODEL, D_MODEL), lambda i: (0, 0))
    vec = pl.BlockSpec((1, D_MODEL), lambda i: (0, 0))
    return pl.pallas_call(
        _merge_kernel,
        grid=(N_TOK // TM_MERGE,),
        in_specs=[row, pl.BlockSpec((TM_MERGE, 3 * D_MODEL), lambda i: (i, R_GATES)),
                  row, row, row, wspec, wspec, wspec, wspec, vec, vec],
        out_specs=row,
        out_shape=jax.ShapeDtypeStruct((N_TOK, D_MODEL), F32),
        compiler_params=_params(("parallel",)),
        name="merge",
    )(x, rest, oa, ob, oc, wa, wb, wc, wo, g, b)


def _extract_desc(s, n, track_rank=False):
    vals = []
    rank = jnp.full(s.shape, float(n), F32) if track_rank else None
    for r in range(n):
        m = jnp.max(s, axis=0, keepdims=True)
        vals.append(m)
        hit = s == m
        if track_rank:
            rank = jnp.where(hit, float(r), rank)
        s = jnp.where(hit, -jnp.inf, s)
    return vals, rank


def _peer_topk_kernel(xT_ref, wq_ref, sk_ref, cnt_ref, e1_ref, rank_ref, e2_ref, s1_scr, sv_scr):
    tt = xT_ref.shape[1]
    xT = xT_ref[...]
    for hp in range(2 * PEER_HEADS):
        h, second = divmod(hp, 2)
        qT = jnp.dot(wq_ref[hp * PEER_DHALF:(hp + 1) * PEER_DHALF, :], xT, preferred_element_type=F32)
        s = jnp.dot(sk_ref[hp], qT.astype(BF16), preferred_element_type=F32)
        vals, rank = _extract_desc(s, PEER_TOPK, track_rank=bool(second))
        sv_scr[hp] = jnp.concatenate(vals, axis=0)
        if second:
            rank_ref[h] = rank.astype(BF16)
            e2_ref[h] = jnp.exp(s - vals[0]).astype(BF16)
        else:
            s1_scr[h] = s
    row8 = lax.broadcasted_iota(jnp.int32, (8, tt), 0)
    for h in range(PEER_HEADS):
        sv1 = sv_scr[2 * h]
        sv2 = sv_scr[2 * h + 1]
        pieces = [sv1[0:1, :] + sv2]
        for a in range(1, 8):
            nb = PEER_TOPK // (a + 1)
            pieces.append(jnp.where(row8 < nb, sv1[a:a + 1, :] + sv2[0:8, :], -jnp.inf))
        pieces.append(sv2[0:1, :] + sv1[8:16, :])
        cand, _ = _extract_desc(jnp.concatenate(pieces, axis=0), PEER_TOPK)
        top = cand[0]
        z = jnp.zeros_like(top)
        for r in range(PEER_TOPK):
            z = z + jnp.exp(cand[r] - top)
        kth = cand[PEER_TOPK - 1]
        counts = [jnp.sum(jnp.where(p >= kth, 1.0, 0.0), axis=0, keepdims=True) for p in pieces[:8]]
        tail = jnp.where(pieces[8] >= kth, 1.0, 0.0)
        counts += [tail[a:a + 1, :] for a in range(8)]
        s1 = s1_scr[h]
        cnt = jnp.zeros_like(s1)
        for a in range(PEER_TOPK):
            cnt = jnp.where(s1 == sv1[a:a + 1, :], counts[a], cnt)
        cnt_ref[h] = cnt
        e1_ref[h] = jnp.exp(s1 - (sv1[0:1, :] + jnp.log(z)))


def _peer_topk(xT, wqT, sk):
    spec = pl.BlockSpec((PEER_HEADS, N_KEYS, TT_TOPK), lambda t: (0, 0, t))
    wide = jax.ShapeDtypeStruct((PEER_HEADS, N_KEYS, N_TOK), F32)
    narrow = jax.ShapeDtypeStruct((PEER_HEADS, N_KEYS, N_TOK), BF16)
    return pl.pallas_call(
        _peer_topk_kernel,
        grid=(N_TOK // TT_TOPK,),
        in_specs=[
            pl.BlockSpec((D_MODEL, TT_TOPK), lambda t: (0, t)),
            pl.BlockSpec((PEER_HEADS * PEER_DKEY, D_MODEL), lambda t: (0, 0)),
            pl.BlockSpec((2 * PEER_HEADS, N_KEYS, PEER_DHALF), lambda t: (0, 0, 0)),
        ],
        out_specs=[spec, spec, spec, spec],
        out_shape=[wide, wide, narrow, narrow],
        scratch_shapes=[pltpu.VMEM((PEER_HEADS, N_KEYS, TT_TOPK), F32),
                        pltpu.VMEM((2 * PEER_HEADS, PEER_TOPK, TT_TOPK), F32)],
        compiler_params=_params(("parallel",)),
        name="peer_topk",
    )(xT, wqT, sk)


def _gelu(x):
    return 0.5 * x * (1.0 + lax.erf(x * (2.0 ** -0.5)))


LC_PEER = 256
BF16_SUBLANES = 16


def _bf16_rows(row, n_rows):
    packed = jnp.broadcast_to(row, (BF16_SUBLANES, row.shape[1])).astype(BF16)
    return jnp.concatenate([packed] * (n_rows // BF16_SUBLANES), axis=0)


def _peer_main_kernel(xT_ref, u_ref, vt_ref, cnt_ref, e1_ref, rank_ref, e2_ref, yT_ref, s_scr, wh_scr):
    i = pl.program_id(1)
    tt = xT_ref.shape[1]

    @pl.when(i == 0)
    def _():
        yT_ref[...] = jnp.zeros_like(yT_ref)

    s_scr[...] = jnp.dot(u_ref[...], xT_ref[...], preferred_element_type=F32)
    zero = jnp.zeros((N_KEYS, LC_PEER), BF16)
    for ii in range(IB_PEER):
        rows = slice(ii * N_KEYS, (ii + 1) * N_KEYS)
        for lc in range(tt // LC_PEER):
            cols = slice(lc * LC_PEER, (lc + 1) * LC_PEER)
            w = zero
            for h in range(PEER_HEADS):
                cnt = _bf16_rows(cnt_ref[h, ii:ii + 1, cols], N_KEYS)
                e1 = _bf16_rows(e1_ref[h, ii:ii + 1, cols], N_KEYS)
                w = w + jnp.where(rank_ref[h, :, cols] < cnt, e2_ref[h, :, cols] * e1, zero)
            wh_scr[rows, cols] = w * _gelu(s_scr[rows, cols]).astype(BF16)
    yT_ref[...] += jnp.dot(vt_ref[...], wh_scr[...], preferred_element_type=F32)


def _peer_main(xT, u, vt, cnt, e1, rank, e2):
    eb = IB_PEER * N_KEYS
    row_spec = pl.BlockSpec((PEER_HEADS, IB_PEER, TT_PEER), lambda t, i: (0, i, t))
    tab_spec = pl.BlockSpec((PEER_HEADS, N_KEYS, TT_PEER), lambda t, i: (0, 0, t))
    return pl.pallas_call(
        _peer_main_kernel,
        grid=(N_TOK // TT_PEER, N_KEYS // IB_PEER),
        in_specs=[
            pl.BlockSpec((D_MODEL, TT_PEER), lambda t, i: (0, t)),
            pl.BlockSpec((eb, D_MODEL), lambda t, i: (i, 0)),
            pl.BlockSpec((D_MODEL, eb), lambda t, i: (0, i)),
            row_spec, row_spec, tab_spec, tab_spec,
        ],
        out_specs=pl.BlockSpec((D_MODEL, TT_PEER), lambda t, i: (0, t)),
        out_shape=jax.ShapeDtypeStruct((D_MODEL, N_TOK), F32),
        scratch_shapes=[pltpu.VMEM((eb, TT_PEER), F32),
                        pltpu.VMEM((eb, TT_PEER), BF16)],
        compiler_params=_params(("parallel", "arbitrary")),
        name="peer_main",
    )(xT, u, vt, cnt, e1, rank, e2)


def _peer_out_kernel(x_ref, yT_ref, g_ref, b_ref, o_ref):
    o_ref[...] = _layer_norm(DN_ALPHA * x_ref[...] + yT_ref[...].T, g_ref[...], b_ref[...])


def _peer_out(x1, yT, g, b):
    row = pl.BlockSpec((TM_MERGE, D_MODEL), lambda i: (i, 0))
    vec = pl.BlockSpec((1, D_MODEL), lambda i: (0, 0))
    return pl.pallas_call(
        _peer_out_kernel,
        grid=(N_TOK // TM_MERGE,),
        in_specs=[row, pl.BlockSpec((D_MODEL, TM_MERGE), lambda i: (0, i)), vec, vec],
        out_specs=row,
        out_shape=jax.ShapeDtypeStruct((N_TOK, D_MODEL), F32),
        compiler_params=_params(("parallel",)),
        name="peer_out",
    )(x1, yT, g, b)


def _rope_tables():
    half = ROT_DIM // 2
    pos = jnp.concatenate([jnp.arange(SEQ), PAST_LEN + (jnp.arange(TM_QK) % DEC_SEQ)])
    inv = ROPE_THETA ** (-jnp.arange(half, dtype=F32) / half)
    ang = pos.astype(F32)[:, None] * inv[None, :]
    cos, sin = jnp.cos(ang), jnp.sin(ang)
    n = pos.shape[0]
    one = jnp.ones((n, A_HEAD_DIM - ROT_DIM), F32)
    zero = jnp.zeros((n, A_HEAD_DIM - ROT_DIM), F32)
    zh = jnp.zeros((n, half), F32)
    reps = LANES // A_HEAD_DIM
    c = jnp.tile(jnp.concatenate([cos, cos, one], 1), (1, reps))
    s1 = jnp.tile(jnp.concatenate([-sin, zh, zero], 1), (1, reps))
    s2 = jnp.tile(jnp.concatenate([zh, sin, zero], 1), (1, reps))
    return c, s1, s2


def _split_cols(w):
    cuts = [int(c) for c in np.cumsum(SPLITS)[:-1]]
    return jnp.split(w, cuts, axis=-1)


def _layer(x, k_state, v_state, gla_state, pool_state, rope, w_in, b_in, sinks, w_alpha, b_alpha,
           gla_g, w_pool, pool_scale, w_a, w_b, w_c, w_out, ln1_g, ln1_b,
           peer_query, peer_subkeys, peer_u, peer_v, ln2_g, ln2_b):
    qa_w, ka_w, va_w, qb_w, kb_w, vb_w, lr_w, gb_w, uc_w, gates_w = _split_cols(w_in)
    qa_b, ka_b, va_b, qb_b, kb_b, vb_b, lr_b, gb_b, uc_b, gates_b = _split_cols(b_in[None, :])
    lr_pad = LANES - B_GATE_RANK
    w_qk = jnp.concatenate([qa_w, ka_w], 1).astype(BF16)
    b_qk = jnp.concatenate([qa_b, ka_b], 1)
    w_rest = jnp.concatenate([gates_w, vb_w, gb_w, uc_w, qb_w, kb_w, va_w,
                              jnp.pad(lr_w, ((0, 0), (0, lr_pad)))], 1).astype(BF16)
    b_rest = jnp.concatenate([gates_b, vb_b, gb_b, uc_b, qb_b, kb_b, va_b,
                              jnp.pad(lr_b, ((0, 0), (0, lr_pad)))], 1)

    xb = x.astype(BF16)
    qk = _proj_qk(xb, w_qk, b_qk, *rope)
    rest = _proj_rest(xb, w_rest, b_rest)

    ks = k_state.reshape(DEC_BATCH, WINDOW, A_KV)
    vs = v_state.reshape(DEC_BATCH, WINDOW, A_KV)
    oa = jnp.concatenate([_attn_prompt(qk, rest, sinks), _attn_sample(qk, rest, sinks, ks, vs)], 0)

    wa = jnp.pad(w_alpha, ((0, lr_pad), (0, 0))).astype(BF16)
    ba = b_alpha[None, :]
    ob_p, gla_p = _gla(rest, jnp.zeros((BATCH, B_HEADS, B_DK, B_DV), F32), wa, ba, gla_g,
                       n_batch=BATCH, seq=SEQ, chunk=GLA_CHUNK, row_base=0, n_rows=N_PROMPT)
    ob_s, gla_s = _gla(rest, gla_state, wa, ba, gla_g,
                       n_batch=DEC_BATCH, seq=DEC_SEQ, chunk=math.gcd(DEC_SEQ, GLA_CHUNK),
                       row_base=N_PROMPT, n_rows=N_SAMPLE)
    ob = jnp.concatenate([ob_p, ob_s], 0)

    wp = w_pool.astype(BF16)
    ps = pool_scale[None, :]
    prev = jnp.pad(pool_state, ((0, 0), (HALO - POOL_STATE, 0), (0, 0))).reshape(DEC_BATCH * HALO, C_WIDTH)
    oc = jnp.concatenate([_pool_prompt(rest, wp, ps), _pool_sample(rest, prev, wp, ps)], 0)

    x1 = _merge(x, rest, oa, ob, oc, w_a.astype(BF16), w_b.astype(BF16), w_c.astype(BF16),
                w_out.astype(BF16), ln1_g[None, :], ln1_b[None, :])

    x1T = x1.T.astype(BF16)
    wqT = peer_query.reshape(D_MODEL, PEER_HEADS * PEER_DKEY).T.astype(BF16)
    sk = peer_subkeys.reshape(2 * PEER_HEADS, N_KEYS, PEER_DHALF).astype(BF16)
    cnt, e1, rank, e2 = _peer_topk(x1T, wqT, sk)
    yT = _peer_main(x1T, peer_u.astype(BF16), peer_v.T.astype(BF16), cnt, e1, rank, e2)
    x2 = _peer_out(x1, yT, ln2_g[None, :], ln2_b[None, :])

    k_new = qk[:, A_Q:]
    v_new = rest[:, R_VA:R_VA + A_KV]
    u_new = rest[:, R_UC:R_UC + C_WIDTH]

    def prompt_tail(t, n):
        return t[:N_PROMPT].reshape(BATCH, SEQ, -1)[:, SEQ - n:]

    def sample_tail(state, t, n):
        new = t[N_PROMPT:].reshape(DEC_BATCH, DEC_SEQ, -1)
        return jnp.concatenate([state, new], 1)[:, -n:]

    kv_shape = (-1, WINDOW, A_KV_HEADS, A_HEAD_DIM)
    states = (prompt_tail(k_new, WINDOW).reshape(kv_shape), prompt_tail(v_new, WINDOW).reshape(kv_shape),
              gla_p, prompt_tail(u_new, POOL_STATE),
              sample_tail(ks, k_new, WINDOW).reshape(kv_shape), sample_tail(vs, v_new, WINDOW).reshape(kv_shape),
              gla_s, sample_tail(pool_state, u_new, POOL_STATE))
    return x2, states


def kernel(x_prompt, x_sample, state_win_k, state_win_v, state_gla, state_pool, w_in, b_in, attn_sinks,
           w_alpha, b_alpha, gla_norm_g, w_pool, pool_scale, w_branch_a, w_branch_b, w_branch_c, w_out,
           ln1_g, ln1_b, peer_query, peer_subkeys, peer_u, peer_v, ln2_g, ln2_b):
    x = jnp.concatenate([x_prompt.reshape(N_PROMPT, D_MODEL), x_sample.reshape(N_SAMPLE, D_MODEL)], 0)
    rope = _rope_tables()
    per_layer = []
    for l in range(DEPTH):
        x, states = _layer(x, state_win_k[l], state_win_v[l], state_gla[l], state_pool[l], rope,
                           w_in[l], b_in[l], attn_sinks[l], w_alpha[l], b_alpha[l], gla_norm_g[l],
                           w_pool[l], pool_scale[l], w_branch_a[l], w_branch_b[l], w_branch_c[l], w_out[l],
                           ln1_g[l], ln1_b[l], peer_query[l], peer_subkeys[l], peer_u[l], peer_v[l],
                           ln2_g[l], ln2_b[l])
        per_layer.append(states)
    stacked = [jnp.stack([per_layer[l][i] for l in range(DEPTH)]) for i in range(8)]
    return (x[:N_PROMPT].reshape(BATCH, SEQ, D_MODEL), x[N_PROMPT:].reshape(DEC_BATCH, DEC_SEQ, D_MODEL),
            *stacked)
```

```python
import functools
import math

import jax
import jax.numpy as jnp
import numpy as np
from jax import lax
from jax.experimental import pallas as pl
from jax.experimental.pallas import tpu as pltpu

F32 = jnp.float32
BF16 = jnp.bfloat16

D_MODEL = 1024
BATCH = 8
SEQ = 2048
DEPTH = 2
DEC_BATCH = 128
DEC_SEQ = 8
PAST_LEN = 16384

A_HEADS = 16
A_KV_HEADS = 2
A_HEAD_DIM = 64
A_GROUP = A_HEADS // A_KV_HEADS
WINDOW = 128
ROT_DIM = A_HEAD_DIM // 4
ROPE_THETA = 500000.0
NEG_INF = -1e30
B_HEADS = 4
B_DK = D_MODEL // 2 // B_HEADS
B_DV = D_MODEL // B_HEADS
B_GATE_RANK = 16
B_TAU = 16.0
GLA_CHUNK = 64
POOL_WINDOWS = (2, 4, 8, 16)
C_GROUPS = len(POOL_WINDOWS)
C_GROUP_W = D_MODEL // C_GROUPS
C_WIDTH = C_GROUPS * C_GROUP_W
POOL_STATE = max(POOL_WINDOWS) - 1
PEER_HEADS = 8
N_KEYS = 128
N_EXPERTS = N_KEYS * N_KEYS
PEER_TOPK = 16
PEER_DKEY = 256
PEER_DHALF = PEER_DKEY // 2
DN_ALPHA = (2 * DEPTH) ** 0.25
LN_EPS = 1e-5
RMS_EPS = 1e-6

A_Q = A_HEADS * A_HEAD_DIM
A_KV = A_KV_HEADS * A_HEAD_DIM
B_QK = B_HEADS * B_DK
B_V = B_HEADS * B_DV
SPLITS = (A_Q, A_KV, A_KV, B_QK, B_QK, B_V, B_GATE_RANK, B_V, C_WIDTH, 3 * D_MODEL)

LANES = 128
N_PROMPT = BATCH * SEQ
N_SAMPLE = DEC_BATCH * DEC_SEQ
N_TOK = N_PROMPT + N_SAMPLE

R_GATES = 0
R_VB = 3 * D_MODEL
R_GB = R_VB + B_V
R_UC = R_GB + B_V
R_QB = R_UC + C_WIDTH
R_KB = R_QB + B_QK
R_VA = R_KB + B_QK
R_LR = R_VA + A_KV
R_WIDTH = R_LR + LANES
QK_WIDTH = A_Q + A_KV

VMEM_LIMIT = 48 * 1024 * 1024
PEER_VMEM_LIMIT = 56 * 1024 * 1024

TM_QK = 512
TM_REST = 512
TN_REST = R_WIDTH // 2
TM_MERGE = 256
TP_POOL = 512
TT_TOPK = 256
TT_PEER = 512
IB_PEER = 8


def _params(sem):
    return pltpu.CompilerParams(dimension_semantics=sem, vmem_limit_bytes=VMEM_LIMIT)


def _qk_kernel(x_ref, w_ref, b_ref, c_ref, s1_ref, s2_ref, o_ref):
    y = jnp.dot(x_ref[...], w_ref[...], preferred_element_type=F32) + b_ref[...]
    c = c_ref[...]
    s1 = s1_ref[...]
    s2 = s2_ref[...]
    for j in range(QK_WIDTH // LANES):
        yj = y[:, j * LANES:(j + 1) * LANES]
        up = pltpu.roll(yj, LANES - ROT_DIM // 2, axis=1)
        dn = pltpu.roll(yj, ROT_DIM // 2, axis=1)
        o_ref[:, j * LANES:(j + 1) * LANES] = yj * c + up * s1 + dn * s2


def _proj_qk(xb, w, b, rope_c, rope_s1, rope_s2):
    n_prompt_blocks = SEQ // TM_QK

    def tab_map(i):
        return (jnp.where(i < N_PROMPT // TM_QK, i % n_prompt_blocks, n_prompt_blocks), 0)

    tab_spec = pl.BlockSpec((TM_QK, LANES), tab_map)
    return pl.pallas_call(
        _qk_kernel,
        grid=(N_TOK // TM_QK,),
        in_specs=[
            pl.BlockSpec((TM_QK, D_MODEL), lambda i: (i, 0)),
            pl.BlockSpec((D_MODEL, QK_WIDTH), lambda i: (0, 0)),
            pl.BlockSpec((1, QK_WIDTH), lambda i: (0, 0)),
            tab_spec, tab_spec, tab_spec,
        ],
        out_specs=pl.BlockSpec((TM_QK, QK_WIDTH), lambda i: (i, 0)),
        out_shape=jax.ShapeDtypeStruct((N_TOK, QK_WIDTH), F32),
        compiler_params=_params(("parallel",)),
        name="proj_qk",
    )(xb, w, b, rope_c, rope_s1, rope_s2)


def _mm_bias_kernel(x_ref, w_ref, b_ref, o_ref):
    o_ref[...] = jnp.dot(x_ref[...], w_ref[...], preferred_element_type=F32) + b_ref[...]


def _proj_rest(xb, w, b):
    return pl.pallas_call(
        _mm_bias_kernel,
        grid=(R_WIDTH // TN_REST, N_TOK // TM_REST),
        in_specs=[
            pl.BlockSpec((TM_REST, D_MODEL), lambda j, i: (i, 0)),
            pl.BlockSpec((D_MODEL, TN_REST), lambda j, i: (0, j)),
            pl.BlockSpec((1, TN_REST), lambda j, i: (0, j)),
        ],
        out_specs=pl.BlockSpec((TM_REST, TN_REST), lambda j, i: (i, j)),
        out_shape=jax.ShapeDtypeStruct((N_TOK, R_WIDTH), F32),
        compiler_params=_params(("parallel", "arbitrary")),
        name="proj_rest",
    )(xb, w, b)


def _attend(q, kk, vv, sink_ref, c_min, o_ref, row0):
    tq = q.shape[0]
    r = lax.broadcasted_iota(jnp.int32, (tq, 2 * WINDOW), 0)
    c = lax.broadcasted_iota(jnp.int32, (tq, 2 * WINDOW), 1)
    ok = (c > r) & (c <= r + WINDOW) & (c >= c_min)
    qb = (q * (A_HEAD_DIM ** -0.5)).astype(BF16)
    for h in range(A_HEADS):
        g = h // A_GROUP
        qh = qb[:, h * A_HEAD_DIM:(h + 1) * A_HEAD_DIM]
        kg = kk[:, g * A_HEAD_DIM:(g + 1) * A_HEAD_DIM]
        vg = vv[:, g * A_HEAD_DIM:(g + 1) * A_HEAD_DIM]
        s = lax.dot_general(qh, kg, (((1,), (1,)), ((), ())), preferred_element_type=F32)
        s = jnp.where(ok, s, NEG_INF)
        sink = sink_ref[h]
        m = jnp.maximum(jnp.max(s, axis=1, keepdims=True), sink)
        p = jnp.exp(s - m)
        denom = jnp.sum(p, axis=1, keepdims=True) + jnp.exp(sink - m)
        o = jnp.dot(p.astype(BF16), vg, preferred_element_type=F32) / denom
        o_ref[pl.ds(row0, tq), h * A_HEAD_DIM:(h + 1) * A_HEAD_DIM] = o.astype(o_ref.dtype)


def _attn_prompt_kernel(sink_ref, q_ref, kc_ref, kp_ref, vc_ref, vp_ref, o_ref):
    n = pl.program_id(1)
    kk = jnp.concatenate([kp_ref[...], kc_ref[...]], axis=0).astype(BF16)
    vv = jnp.concatenate([vp_ref[...], vc_ref[...]], axis=0).astype(BF16)
    _attend(q_ref[...], kk, vv, sink_ref, jnp.where(n > 0, 0, WINDOW), o_ref, 0)


def _attn_prompt(qk, rest, sinks):
    nb = SEQ // WINDOW
    kcol = A_Q // A_KV
    vcol = R_VA // A_KV

    def cur(b, n):
        return b * nb + n

    def prev(b, n):
        return b * nb + jnp.maximum(n - 1, 0)

    return pl.pallas_call(
        _attn_prompt_kernel,
        grid=(BATCH, nb),
        in_specs=[
            pl.BlockSpec(memory_space=pltpu.SMEM),
            pl.BlockSpec((WINDOW, A_Q), lambda b, n: (cur(b, n), 0)),
            pl.BlockSpec((WINDOW, A_KV), lambda b, n: (cur(b, n), kcol)),
            pl.BlockSpec((WINDOW, A_KV), lambda b, n: (prev(b, n), kcol)),
            pl.BlockSpec((WINDOW, A_KV), lambda b, n: (cur(b, n), vcol)),
            pl.BlockSpec((WINDOW, A_KV), lambda b, n: (prev(b, n), vcol)),
        ],
        out_specs=pl.BlockSpec((WINDOW, A_Q), lambda b, n: (cur(b, n), 0)),
        out_shape=jax.ShapeDtypeStruct((N_TOK, A_Q), BF16),
        compiler_params=_params(("parallel", "arbitrary")),
        name="attn_prompt",
    )(sinks, qk, qk, qk, rest, rest)


BB_ATTN = 8


def _attn_sample_kernel(sink_ref, q_ref, kn_ref, vn_ref, ks_ref, vs_ref, prompt_out_ref, o_ref):
    del prompt_out_ref
    pad = jnp.zeros((WINDOW - DEC_SEQ, A_KV), F32)
    rows = A_GROUP * DEC_SEQ
    t = lax.broadcasted_iota(jnp.int32, (rows, 2 * WINDOW), 0) % DEC_SEQ
    c = lax.broadcasted_iota(jnp.int32, (rows, 2 * WINDOW), 1)
    ok = (c > t) & (c <= t + WINDOW)
    nt = (((1,), (1,)), ((), ()))

    def body(bb, carry):
        row0 = pl.multiple_of(bb * DEC_SEQ, DEC_SEQ)
        q = q_ref[pl.ds(row0, DEC_SEQ), :] * (A_HEAD_DIM ** -0.5)
        kk = jnp.concatenate([ks_ref[bb], kn_ref[pl.ds(row0, DEC_SEQ), :], pad], axis=0).astype(BF16)
        vv = jnp.concatenate([vs_ref[bb], vn_ref[pl.ds(row0, DEC_SEQ), :], pad], axis=0).astype(BF16)
        for g in range(A_KV_HEADS):
            heads = range(g * A_GROUP, (g + 1) * A_GROUP)
            qg = jnp.concatenate([q[:, h * A_HEAD_DIM:(h + 1) * A_HEAD_DIM] for h in heads], axis=0)
            ds = slice(g * A_HEAD_DIM, (g + 1) * A_HEAD_DIM)
            s = lax.dot_general(qg.astype(BF16), kk[:, ds], nt, preferred_element_type=F32)
            s = jnp.where(ok, s, NEG_INF)
            sink = sink_ref[g * rows:(g + 1) * rows, 0:1]
            m = jnp.maximum(jnp.max(s, axis=1, keepdims=True), sink)
            p = jnp.exp(s - m)
            denom = jnp.sum(p, axis=1, keepdims=True) + jnp.exp(sink - m)
            o = jnp.dot(p.astype(BF16), vv[:, ds], preferred_element_type=F32) / denom
            for k, h in enumerate(heads):
                o_ref[pl.ds(row0, DEC_SEQ), h * A_HEAD_DIM:(h + 1) * A_HEAD_DIM] = (
                    o[k * DEC_SEQ:(k + 1) * DEC_SEQ, :].astype(o_ref.dtype))
        return carry

    lax.fori_loop(0, BB_ATTN, body, 0)


def _attn_sample(qk, rest, sinks, k_state, v_state, prompt_out):
    rows = BB_ATTN * DEC_SEQ
    base = N_PROMPT // rows
    kcol = A_Q // A_KV
    vcol = R_VA // A_KV
    sink_rows = jnp.broadcast_to(jnp.repeat(sinks, DEC_SEQ)[:, None], (A_HEADS * DEC_SEQ, LANES))
    return pl.pallas_call(
        _attn_sample_kernel,
        grid=(DEC_BATCH // BB_ATTN,),
        in_specs=[
            pl.BlockSpec((A_HEADS * DEC_SEQ, LANES), lambda i: (0, 0)),
            pl.BlockSpec((rows, A_Q), lambda i: (base + i, 0)),
            pl.BlockSpec((rows, A_KV), lambda i: (base + i, kcol)),
            pl.BlockSpec((rows, A_KV), lambda i: (base + i, vcol)),
            pl.BlockSpec((BB_ATTN, WINDOW, A_KV), lambda i: (i, 0, 0)),
            pl.BlockSpec((BB_ATTN, WINDOW, A_KV), lambda i: (i, 0, 0)),
            pl.BlockSpec(memory_space=pl.ANY),
        ],
        out_specs=pl.BlockSpec((rows, A_Q), lambda i: (base + i, 0)),
        out_shape=jax.ShapeDtypeStruct((N_TOK, A_Q), BF16),
        input_output_aliases={6: 0},
        compiler_params=_params(("parallel",)),
        name="attn_sample",
    )(sink_rows, qk, qk, rest, k_state, v_state, prompt_out)


def _split3(x):
    hi = x.astype(BF16)
    r1 = x - hi.astype(F32)
    mid = r1.astype(BF16)
    lo = (r1 - mid.astype(F32)).astype(BF16)
    return hi, mid, lo


def _gla_kernel(lr_ref, q_ref, k_ref, v_ref, gb_ref, s0_ref, wa_ref, ba_ref, g_ref, *refs, n_chunks):
    o_ref, sout_ref, st_ref = refs[-3:]
    ci = pl.program_id(1)
    c = q_ref.shape[0]

    @pl.when(ci == 0)
    def _():
        for h in range(B_HEADS):
            st_ref[h] = s0_ref[0, h].T

    z = jnp.dot(lr_ref[...].astype(BF16), wa_ref[...], preferred_element_type=F32) + ba_ref[...]
    log_a = -(jnp.maximum(-z, 0.0) + jnp.log1p(jnp.exp(-jnp.abs(z)))) / B_TAU
    ri = lax.broadcasted_iota(jnp.int32, (c, c), 0)
    cj = lax.broadcasted_iota(jnp.int32, (c, c), 1)
    causal = cj <= ri
    tri = jnp.where(causal, 1.0, 0.0).astype(BF16)
    hi, mid, lo = _split3(log_a)
    b = (jnp.dot(tri, hi, preferred_element_type=F32)
         + jnp.dot(tri, mid, preferred_element_type=F32)
         + jnp.dot(tri, lo, preferred_element_type=F32))
    bl = b[c - 1:c, :]
    q = q_ref[...] * (B_DK ** -0.5)
    k = k_ref[...]
    qd = (q * jnp.exp(b)).astype(BF16)
    kd = (k * jnp.exp(-b)).astype(BF16)
    kl = (k * jnp.exp(bl - b)).astype(BF16)
    ebl = jnp.exp(bl)
    nt = (((1,), (1,)), ((), ()))
    for h in range(B_HEADS):
        ks = slice(h * B_DK, (h + 1) * B_DK)
        vs = slice(h * B_DV, (h + 1) * B_DV)
        vh = v_ref[:, vs]
        st = st_ref[h]
        o = lax.dot_general(qd[:, ks], st.astype(BF16), nt, preferred_element_type=F32)
        att = lax.dot_general(qd[:, ks], kd[:, ks], nt, preferred_element_type=F32)
        att = jnp.where(causal, att, 0.0)
        o = o + jnp.dot(att.astype(BF16), vh.astype(BF16), preferred_element_type=F32)
        st_ref[h] = st * ebl[:, ks] + jnp.dot(vh.T.astype(BF16), kl[:, ks], preferred_element_type=F32)
        o = o * lax.rsqrt(jnp.mean(o * o, axis=1, keepdims=True) + RMS_EPS) * g_ref[h:h + 1, :]
        gate = gb_ref[:, vs]
        o_ref[:, vs] = (o * (gate / (1.0 + jnp.exp(-gate)))).astype(o_ref.dtype)

    @pl.when(ci == n_chunks - 1)
    def _():
        for h in range(B_HEADS):
            sout_ref[0, h] = st_ref[h].T


def _gla(rest, s0, wa, ba, g, *, n_batch, seq, chunk, row_base, fill_into=None):
    n_chunks = seq // chunk
    base = row_base // chunk

    def row(b, ci):
        return base + b * n_chunks + ci

    extra_specs = [] if fill_into is None else [pl.BlockSpec(memory_space=pl.ANY)]
    extra_args = [] if fill_into is None else [fill_into]
    return pl.pallas_call(
        functools.partial(_gla_kernel, n_chunks=n_chunks),
        grid=(n_batch, n_chunks),
        input_output_aliases={} if fill_into is None else {9: 0},
        in_specs=[
            pl.BlockSpec((chunk, LANES), lambda b, ci: (row(b, ci), R_LR // LANES)),
            pl.BlockSpec((chunk, B_QK), lambda b, ci: (row(b, ci), R_QB // B_QK)),
            pl.BlockSpec((chunk, B_QK), lambda b, ci: (row(b, ci), R_KB // B_QK)),
            pl.BlockSpec((chunk, B_V), lambda b, ci: (row(b, ci), R_VB // B_V)),
            pl.BlockSpec((chunk, B_V), lambda b, ci: (row(b, ci), R_GB // B_V)),
            pl.BlockSpec((1, B_HEADS, B_DK, B_DV), lambda b, ci: (b, 0, 0, 0)),
            pl.BlockSpec((LANES, B_QK), lambda b, ci: (0, 0)),
            pl.BlockSpec((1, B_QK), lambda b, ci: (0, 0)),
            pl.BlockSpec((B_HEADS, B_DV), lambda b, ci: (0, 0)),
        ] + extra_specs,
        out_specs=[
            pl.BlockSpec((chunk, B_V), lambda b, ci: (row(b, ci), 0)),
            pl.BlockSpec((1, B_HEADS, B_DK, B_DV), lambda b, ci: (b, 0, 0, 0)),
        ],
        out_shape=[
            jax.ShapeDtypeStruct((N_TOK, B_V), BF16),
            jax.ShapeDtypeStruct((n_batch, B_HEADS, B_DK, B_DV), F32),
        ],
        scratch_shapes=[pltpu.VMEM((B_HEADS, B_DV, B_DK), F32)],
        compiler_params=_params(("parallel", "arbitrary")),
        name="gla",
    )(rest, rest, rest, rest, rest, s0, wa, ba, g, *extra_args)


HALO = 16


def _pool_kernel(u_ref, prev_ref, w_ref, scale_ref, *refs, from_start):
    o_ref = refs[-1]
    tp = u_ref.shape[0]
    u = u_ref[...]
    prev = prev_ref[...]
    if from_start:
        ti = pl.program_id(1)
        prev = jnp.where(ti > 0, prev, 0.0)
        t0 = ti * tp
    full = jnp.concatenate([prev, u], axis=0)
    for g, w in enumerate(POOL_WINDOWS):
        cs = slice(g * C_GROUP_W, (g + 1) * C_GROUP_W)
        acc = full[:, cs]
        span = 1
        while span < w:
            acc = acc + pltpu.roll(acc, span, axis=0)
            span *= 2
        wsum = acc[HALO:, :]
        if from_start:
            t = t0 + lax.broadcasted_iota(jnp.int32, (tp, C_GROUP_W), 0)
            cnt = jnp.minimum(t + 1, w).astype(F32)
        else:
            cnt = float(w)
        d = wsum / cnt - u[:, cs]
        y = jnp.dot(d.astype(BF16), w_ref[g], preferred_element_type=F32) * scale_ref[:, cs]
        o_ref[:, cs] = y.astype(o_ref.dtype)


def _pool_prompt(rest, w, scale):
    nt = SEQ // TP_POOL
    ucol = R_UC // C_WIDTH

    def halo(b, i):
        return (jnp.maximum((b * SEQ + i * TP_POOL) // HALO - 1, 0), ucol)

    return pl.pallas_call(
        functools.partial(_pool_kernel, from_start=True),
        grid=(BATCH, nt),
        in_specs=[
            pl.BlockSpec((TP_POOL, C_WIDTH), lambda b, i: (b * nt + i, ucol)),
            pl.BlockSpec((HALO, C_WIDTH), halo),
            pl.BlockSpec((C_GROUPS, C_GROUP_W, C_GROUP_W), lambda b, i: (0, 0, 0)),
            pl.BlockSpec((1, C_WIDTH), lambda b, i: (0, 0)),
        ],
        out_specs=pl.BlockSpec((TP_POOL, C_WIDTH), lambda b, i: (b * nt + i, 0)),
        out_shape=jax.ShapeDtypeStruct((N_TOK, C_WIDTH), BF16),
        compiler_params=_params(("parallel", "arbitrary")),
        name="pool_prompt",
    )(rest, rest, w, scale)


def _pool_sample(rest, prev, w, scale, prompt_out):
    ucol = R_UC // C_WIDTH
    base = N_PROMPT // DEC_SEQ
    return pl.pallas_call(
        functools.partial(_pool_kernel, from_start=False),
        grid=(DEC_BATCH,),
        in_specs=[
            pl.BlockSpec((DEC_SEQ, C_WIDTH), lambda b: (base + b, ucol)),
            pl.BlockSpec((HALO, C_WIDTH), lambda b: (b, 0)),
            pl.BlockSpec((C_GROUPS, C_GROUP_W, C_GROUP_W), lambda b: (0, 0, 0)),
            pl.BlockSpec((1, C_WIDTH), lambda b: (0, 0)),
            pl.BlockSpec(memory_space=pl.ANY),
        ],
        out_specs=pl.BlockSpec((DEC_SEQ, C_WIDTH), lambda b: (base + b, 0)),
        out_shape=jax.ShapeDtypeStruct((N_TOK, C_WIDTH), BF16),
        input_output_aliases={4: 0},
        compiler_params=_params(("parallel",)),
        name="pool_sample",
    )(rest, prev, w, scale, prompt_out)


def _layer_norm(x, g, b):
    mu = jnp.mean(x, axis=1, keepdims=True)
    xc = x - mu
    var = jnp.mean(xc * xc, axis=1, keepdims=True)
    return xc * lax.rsqrt(var + LN_EPS) * g + b


def _merge_kernel(x_ref, gates_ref, oa_ref, ob_ref, oc_ref, wa_ref, wb_ref, wc_ref, wo_ref,
                  g_ref, b_ref, o_ref, oT_ref):
    def gate(i):
        z = gates_ref[:, i * D_MODEL:(i + 1) * D_MODEL]
        return 1.0 / (1.0 + jnp.exp(-z))

    merged = (gate(0) * jnp.dot(oa_ref[...], wa_ref[...], preferred_element_type=F32)
              + gate(1) * jnp.dot(ob_ref[...], wb_ref[...], preferred_element_type=F32)
              + gate(2) * jnp.dot(oc_ref[...], wc_ref[...], preferred_element_type=F32))
    mix = jnp.dot(merged.astype(BF16), wo_ref[...], preferred_element_type=F32)
    y = _layer_norm(DN_ALPHA * x_ref[...] + mix, g_ref[...], b_ref[...])
    o_ref[...] = y
    oT_ref[...] = y.T.astype(BF16)


def _merge(x, rest, oa, ob, oc, wa, wb, wc, wo, g, b):
    row = pl.BlockSpec((TM_MERGE, D_MODEL), lambda i: (i, 0))
    wspec = pl.BlockSpec((D_MODEL, D_MODEL), lambda i: (0, 0))
    vec = pl.BlockSpec((1, D_MODEL), lambda i: (0, 0))
    return pl.pallas_call(
        _merge_kernel,
        grid=(N_TOK // TM_MERGE,),
        in_specs=[row, pl.BlockSpec((TM_MERGE, 3 * D_MODEL), lambda i: (i, R_GATES)),
                  row, row, row, wspec, wspec, wspec, wspec, vec, vec],
        out_specs=[row, pl.BlockSpec((D_MODEL, TM_MERGE), lambda i: (0, i))],
        out_shape=[jax.ShapeDtypeStruct((N_TOK, D_MODEL), F32),
                   jax.ShapeDtypeStruct((D_MODEL, N_TOK), BF16)],
        compiler_params=_params(("parallel",)),
        name="merge",
    )(x, rest, oa, ob, oc, wa, wb, wc, wo, g, b)


def _extract_desc(s, n, track_rank=False):
    vals = []
    rank = jnp.full(s.shape, float(n), F32) if track_rank else None
    for r in range(n):
        m = jnp.max(s, axis=0, keepdims=True)
        vals.append(m)
        hit = s == m
        if track_rank:
            rank = jnp.where(hit, float(r), rank)
        s = jnp.where(hit, -jnp.inf, s)
    return vals, rank


def _peer_topk_kernel(xT_ref, wq_ref, sk_ref, cnt_ref, e1_ref, rank_ref, e2_ref, s1_scr, sv_scr):
    tt = xT_ref.shape[1]
    xT = xT_ref[...]
    for hp in range(2 * PEER_HEADS):
        h, second = divmod(hp, 2)
        qT = jnp.dot(wq_ref[hp * PEER_DHALF:(hp + 1) * PEER_DHALF, :], xT, preferred_element_type=F32)
        s = jnp.dot(sk_ref[hp], qT.astype(BF16), preferred_element_type=F32)
        vals, rank = _extract_desc(s, PEER_TOPK, track_rank=bool(second))
        sv_scr[hp] = jnp.concatenate(vals, axis=0)
        if second:
            rank_ref[h] = rank.astype(BF16)
            e2_ref[h] = jnp.exp(s - vals[0]).astype(BF16)
        else:
            s1_scr[h] = s
    row8 = lax.broadcasted_iota(jnp.int32, (8, tt), 0)
    for h in range(PEER_HEADS):
        sv1 = sv_scr[2 * h]
        sv2 = sv_scr[2 * h + 1]
        pieces = [sv1[0:1, :] + sv2]
        for a in range(1, 8):
            nb = PEER_TOPK // (a + 1)
            pieces.append(jnp.where(row8 < nb, sv1[a:a + 1, :] + sv2[0:8, :], -jnp.inf))
        pieces.append(sv2[0:1, :] + sv1[8:16, :])
        cand, _ = _extract_desc(jnp.concatenate(pieces, axis=0), PEER_TOPK)
        top = cand[0]
        z = jnp.zeros_like(top)
        for r in range(PEER_TOPK):
            z = z + jnp.exp(cand[r] - top)
        kth = cand[PEER_TOPK - 1]
        counts = [jnp.sum(jnp.where(p >= kth, 1.0, 0.0), axis=0, keepdims=True) for p in pieces[:8]]
        tail = jnp.where(pieces[8] >= kth, 1.0, 0.0)
        counts += [tail[a:a + 1, :] for a in range(8)]
        s1 = s1_scr[h]
        cnt = jnp.zeros_like(s1)
        for a in range(PEER_TOPK):
            cnt = jnp.where(s1 == sv1[a:a + 1, :], counts[a], cnt)
        e1 = jnp.exp(s1 - (sv1[0:1, :] + jnp.log(z))) * SQRT_HALF
        for c in range(tt // LANES):
            cnt_ref[h, c] = cnt[:, c * LANES:(c + 1) * LANES]
            e1_ref[h, c] = e1[:, c * LANES:(c + 1) * LANES]


def _peer_topk(xT, wqT, sk):
    spec = pl.BlockSpec((PEER_HEADS, N_KEYS, TT_TOPK), lambda t: (0, 0, t))
    row_spec = pl.BlockSpec((PEER_HEADS, TT_TOPK // LANES, N_KEYS, LANES), lambda t: (0, t, 0, 0))
    wide = jax.ShapeDtypeStruct((PEER_HEADS, N_TOK // LANES, N_KEYS, LANES), F32)
    narrow = jax.ShapeDtypeStruct((PEER_HEADS, N_KEYS, N_TOK), BF16)
    return pl.pallas_call(
        _peer_topk_kernel,
        grid=(N_TOK // TT_TOPK,),
        in_specs=[
            pl.BlockSpec((D_MODEL, TT_TOPK), lambda t: (0, t)),
            pl.BlockSpec((PEER_HEADS * PEER_DKEY, D_MODEL), lambda t: (0, 0)),
            pl.BlockSpec((2 * PEER_HEADS, N_KEYS, PEER_DHALF), lambda t: (0, 0, 0)),
        ],
        out_specs=[row_spec, row_spec, spec, spec],
        out_shape=[wide, wide, narrow, narrow],
        scratch_shapes=[pltpu.VMEM((PEER_HEADS, N_KEYS, TT_TOPK), F32),
                        pltpu.VMEM((2 * PEER_HEADS, PEER_TOPK, TT_TOPK), F32)],
        compiler_params=_params(("parallel",)),
        name="peer_topk",
    )(xT, wqT, sk)


SQRT_HALF = 2.0 ** -0.5


def _gelu_unscaled(x):
    t = x * SQRT_HALF
    return t * (1.0 + lax.erf(t))


LC_PEER = 256
MC_PEER = 512


def _bf16_rows(ref, h, r, chunks, n_rows):
    x = jnp.concatenate([ref[h, c, pl.ds(r, 16, stride=0), :] for c in chunks], axis=1)
    packed = x.astype(BF16)
    return jnp.concatenate([packed] * (n_rows // packed.shape[0]), axis=0)


def _peer_main_kernel(xT_ref, u_ref, vt_ref, cnt_ref, e1_ref, rank_ref, e2_ref, yT_ref, *scratch):
    i = pl.program_id(1)
    n_mc = len(scratch) // 2
    s_scrs, wh_scrs = scratch[:n_mc], scratch[n_mc:]

    @pl.when(i == 0)
    def _():
        yT_ref[...] = jnp.zeros_like(yT_ref)

    zero = jnp.zeros((N_KEYS, LC_PEER), BF16)
    for mc in range(n_mc):
        mcols = slice(mc * MC_PEER, (mc + 1) * MC_PEER)
        s_scr, wh_scr = s_scrs[mc], wh_scrs[mc]
        s_scr[...] = jnp.dot(u_ref[...], xT_ref[:, mcols], preferred_element_type=F32)
        for ii in range(IB_PEER):
            rows = slice(ii * N_KEYS, (ii + 1) * N_KEYS)
            for lc in range(MC_PEER // LC_PEER):
                lcols = slice(lc * LC_PEER, (lc + 1) * LC_PEER)
                lane0 = mc * MC_PEER + lc * LC_PEER
                cols = slice(lane0, lane0 + LC_PEER)
                chunks = range(lane0 // LANES, (lane0 + LC_PEER) // LANES)
                w = zero
                for h in range(PEER_HEADS):
                    cnt = _bf16_rows(cnt_ref, h, ii, chunks, N_KEYS)
                    e1 = _bf16_rows(e1_ref, h, ii, chunks, N_KEYS)
                    w = w + jnp.where(rank_ref[h, :, cols] < cnt, e2_ref[h, :, cols] * e1, zero)
                wh_scr[rows, lcols] = w * _gelu_unscaled(s_scr[rows, lcols]).astype(BF16)
        yT_ref[:, mcols] += jnp.dot(vt_ref[...], wh_scr[...], preferred_element_type=F32)


def _peer_main(xT, u, vt, cnt, e1, rank, e2):
    eb = IB_PEER * N_KEYS
    n_mc = TT_PEER // MC_PEER
    row_spec = pl.BlockSpec((PEER_HEADS, TT_PEER // LANES, IB_PEER, LANES), lambda t, i: (0, t, i, 0))
    tab_spec = pl.BlockSpec((PEER_HEADS, N_KEYS, TT_PEER), lambda t, i: (0, 0, t))
    return pl.pallas_call(
        _peer_main_kernel,
        grid=(N_TOK // TT_PEER, N_EXPERTS // eb),
        in_specs=[
            pl.BlockSpec((D_MODEL, TT_PEER), lambda t, i: (0, t)),
            pl.BlockSpec((eb, D_MODEL), lambda t, i: (i, 0)),
            pl.BlockSpec((D_MODEL, eb), lambda t, i: (0, i)),
            row_spec, row_spec, tab_spec, tab_spec,
        ],
        out_specs=pl.BlockSpec((D_MODEL, TT_PEER), lambda t, i: (0, t)),
        out_shape=jax.ShapeDtypeStruct((D_MODEL, N_TOK), F32),
        scratch_shapes=([pltpu.VMEM((eb, MC_PEER), F32)] * n_mc + [pltpu.VMEM((eb, MC_PEER), BF16)] * n_mc),
        compiler_params=_params(("parallel", "arbitrary")),
        name="peer_main",
    )(xT, u, vt, cnt, e1, rank, e2)


def _peer_out_kernel(x_ref, yT_ref, g_ref, b_ref, o_ref, ob_ref):
    y = _layer_norm(DN_ALPHA * x_ref[...] + yT_ref[...].T, g_ref[...], b_ref[...])
    o_ref[...] = y
    ob_ref[...] = y.astype(BF16)


def _peer_out(x1, yT, g, b):
    row = pl.BlockSpec((TM_MERGE, D_MODEL), lambda i: (i, 0))
    vec = pl.BlockSpec((1, D_MODEL), lambda i: (0, 0))
    return pl.pallas_call(
        _peer_out_kernel,
        grid=(N_TOK // TM_MERGE,),
        in_specs=[row, pl.BlockSpec((D_MODEL, TM_MERGE), lambda i: (0, i)), vec, vec],
        out_specs=[row, row],
        out_shape=[jax.ShapeDtypeStruct((N_TOK, D_MODEL), F32),
                   jax.ShapeDtypeStruct((N_TOK, D_MODEL), BF16)],
        compiler_params=_params(("parallel",)),
        name="peer_out",
    )(x1, yT, g, b)


def _rope_tables():
    half = ROT_DIM // 2
    pos = jnp.concatenate([jnp.arange(SEQ), PAST_LEN + (jnp.arange(TM_QK) % DEC_SEQ)])
    inv = ROPE_THETA ** (-jnp.arange(half, dtype=F32) / half)
    ang = pos.astype(F32)[:, None] * inv[None, :]
    cos, sin = jnp.cos(ang), jnp.sin(ang)
    n = pos.shape[0]
    one = jnp.ones((n, A_HEAD_DIM - ROT_DIM), F32)
    zero = jnp.zeros((n, A_HEAD_DIM - ROT_DIM), F32)
    zh = jnp.zeros((n, half), F32)
    reps = LANES // A_HEAD_DIM
    c = jnp.tile(jnp.concatenate([cos, cos, one], 1), (1, reps))
    s1 = jnp.tile(jnp.concatenate([-sin, zh, zero], 1), (1, reps))
    s2 = jnp.tile(jnp.concatenate([zh, sin, zero], 1), (1, reps))
    return c, s1, s2


def _split_cols(w):
    cuts = [int(c) for c in np.cumsum(SPLITS)[:-1]]
    return jnp.split(w, cuts, axis=-1)


def _layer(x, xb, k_state, v_state, gla_state, pool_state, rope, w_in, b_in, sinks, w_alpha, b_alpha,
           gla_g, w_pool, pool_scale, w_a, w_b, w_c, w_out, ln1_g, ln1_b,
           peer_query, peer_subkeys, peer_u, peer_v, ln2_g, ln2_b):
    qa_w, ka_w, va_w, qb_w, kb_w, vb_w, lr_w, gb_w, uc_w, gates_w = _split_cols(w_in)
    qa_b, ka_b, va_b, qb_b, kb_b, vb_b, lr_b, gb_b, uc_b, gates_b = _split_cols(b_in[None, :])
    lr_pad = LANES - B_GATE_RANK
    w_qk = jnp.concatenate([qa_w, ka_w], 1).astype(BF16)
    b_qk = jnp.concatenate([qa_b, ka_b], 1)
    w_rest = jnp.concatenate([gates_w, vb_w, gb_w, uc_w, qb_w, kb_w, va_w,
                              jnp.pad(lr_w, ((0, 0), (0, lr_pad)))], 1).astype(BF16)
    b_rest = jnp.concatenate([gates_b, vb_b, gb_b, uc_b, qb_b, kb_b, va_b,
                              jnp.pad(lr_b, ((0, 0), (0, lr_pad)))], 1)

    qk = _proj_qk(xb, w_qk, b_qk, *rope)
    rest = _proj_rest(xb, w_rest, b_rest)

    ks = k_state.reshape(DEC_BATCH, WINDOW, A_KV)
    vs = v_state.reshape(DEC_BATCH, WINDOW, A_KV)
    oa = _attn_sample(qk, rest, sinks, ks, vs, _attn_prompt(qk, rest, sinks))

    wa = jnp.pad(w_alpha, ((0, lr_pad), (0, 0))).astype(BF16)
    ba = b_alpha[None, :]
    ob, gla_p = _gla(rest, jnp.zeros((BATCH, B_HEADS, B_DK, B_DV), F32), wa, ba, gla_g,
                     n_batch=BATCH, seq=SEQ, chunk=GLA_CHUNK, row_base=0)
    ob, gla_s = _gla(rest, gla_state, wa, ba, gla_g, n_batch=DEC_BATCH, seq=DEC_SEQ,
                     chunk=math.gcd(DEC_SEQ, GLA_CHUNK), row_base=N_PROMPT, fill_into=ob)

    wp = w_pool.astype(BF16)
    ps = pool_scale[None, :]
    prev = jnp.pad(pool_state, ((0, 0), (HALO - POOL_STATE, 0), (0, 0))).reshape(DEC_BATCH * HALO, C_WIDTH)
    oc = _pool_sample(rest, prev, wp, ps, _pool_prompt(rest, wp, ps))

    x1, x1T = _merge(x, rest, oa, ob, oc, w_a.astype(BF16), w_b.astype(BF16), w_c.astype(BF16),
                     w_out.astype(BF16), ln1_g[None, :], ln1_b[None, :])

    wqT = peer_query.reshape(D_MODEL, PEER_HEADS * PEER_DKEY).T.astype(BF16)
    sk = peer_subkeys.reshape(2 * PEER_HEADS, N_KEYS, PEER_DHALF).astype(BF16)
    cnt, e1, rank, e2 = _peer_topk(x1T, wqT, sk)
    yT = _peer_main(x1T, peer_u.astype(BF16), peer_v.T.astype(BF16), cnt, e1, rank, e2)
    x2, x2b = _peer_out(x1, yT, ln2_g[None, :], ln2_b[None, :])

    def prompt_tail(t, col0, width, n):
        return t[:N_PROMPT].reshape(BATCH, SEQ, t.shape[1])[:, SEQ - n:, col0:col0 + width]

    def sample_tail(state, t, col0, width, n):
        new = t[N_PROMPT:, col0:col0 + width].reshape(DEC_BATCH, DEC_SEQ, width)
        return jnp.concatenate([state, new], 1)[:, -n:]

    kv_shape = (-1, WINDOW, A_KV_HEADS, A_HEAD_DIM)
    states = (prompt_tail(qk, A_Q, A_KV, WINDOW).reshape(kv_shape),
              prompt_tail(rest, R_VA, A_KV, WINDOW).reshape(kv_shape),
              gla_p, prompt_tail(rest, R_UC, C_WIDTH, POOL_STATE),
              sample_tail(ks, qk, A_Q, A_KV, WINDOW).reshape(kv_shape),
              sample_tail(vs, rest, R_VA, A_KV, WINDOW).reshape(kv_shape),
              gla_s, sample_tail(pool_state, rest, R_UC, C_WIDTH, POOL_STATE))
    return x2, x2b, states


def kernel(x_prompt, x_sample, state_win_k, state_win_v, state_gla, state_pool, w_in, b_in, attn_sinks,
           w_alpha, b_alpha, gla_norm_g, w_pool, pool_scale, w_branch_a, w_branch_b, w_branch_c, w_out,
           ln1_g, ln1_b, peer_query, peer_subkeys, peer_u, peer_v, ln2_g, ln2_b):
    x = jnp.concatenate([x_prompt.reshape(N_PROMPT, D_MODEL), x_sample.reshape(N_SAMPLE, D_MODEL)], 0)
    xb = x.astype(BF16)
    rope = _rope_tables()
    per_layer = []
    for l in range(DEPTH):
        x, xb, states = _layer(x, xb, state_win_k[l], state_win_v[l], state_gla[l], state_pool[l], rope,
                           w_in[l], b_in[l], attn_sinks[l], w_alpha[l], b_alpha[l], gla_norm_g[l],
                           w_pool[l], pool_scale[l], w_branch_a[l], w_branch_b[l], w_branch_c[l], w_out[l],
                           ln1_g[l], ln1_b[l], peer_query[l], peer_subkeys[l], peer_u[l], peer_v[l],
                           ln2_g[l], ln2_b[l])
        per_layer.append(states)
    stacked = [jnp.stack([per_layer[l][i] for l in range(DEPTH)]) for i in range(8)]
    return (x[:N_PROMPT].reshape(BATCH, SEQ, D_MODEL), x[N_PROMPT:].reshape(DEC_BATCH, DEC_SEQ, D_MODEL),
            *stacked)
```

```python
import functools
import math

import jax
import jax.numpy as jnp
import numpy as np
from jax import lax
from jax.experimental import pallas as pl
from jax.experimental.pallas import tpu as pltpu

F32 = jnp.float32
BF16 = jnp.bfloat16

D_MODEL = 1024
BATCH = 8
SEQ = 2048
DEPTH = 2
DEC_BATCH = 128
DEC_SEQ = 8
PAST_LEN = 16384

A_HEADS = 16
A_KV_HEADS = 2
A_HEAD_DIM = 64
A_GROUP = A_HEADS // A_KV_HEADS
WINDOW = 128
ROT_DIM = A_HEAD_DIM // 4
ROPE_THETA = 500000.0
NEG_INF = -1e30
B_HEADS = 4
B_DK = D_MODEL // 2 // B_HEADS
B_DV = D_MODEL // B_HEADS
B_GATE_RANK = 16
B_TAU = 16.0
GLA_CHUNK = 64
POOL_WINDOWS = (2, 4, 8, 16)
C_GROUPS = len(POOL_WINDOWS)
C_GROUP_W = D_MODEL // C_GROUPS
C_WIDTH = C_GROUPS * C_GROUP_W
POOL_STATE = max(POOL_WINDOWS) - 1
PEER_HEADS = 8
N_KEYS = 128
N_EXPERTS = N_KEYS * N_KEYS
PEER_TOPK = 16
PEER_DKEY = 256
PEER_DHALF = PEER_DKEY // 2
DN_ALPHA = (2 * DEPTH) ** 0.25
LN_EPS = 1e-5
RMS_EPS = 1e-6

A_Q = A_HEADS * A_HEAD_DIM
A_KV = A_KV_HEADS * A_HEAD_DIM
B_QK = B_HEADS * B_DK
B_V = B_HEADS * B_DV
SPLITS = (A_Q, A_KV, A_KV, B_QK, B_QK, B_V, B_GATE_RANK, B_V, C_WIDTH, 3 * D_MODEL)

LANES = 128
N_PROMPT = BATCH * SEQ
N_SAMPLE = DEC_BATCH * DEC_SEQ
N_TOK = N_PROMPT + N_SAMPLE

R_GATES = 0
R_VB = 3 * D_MODEL
R_GB = R_VB + B_V
R_UC = R_GB + B_V
R_QB = R_UC + C_WIDTH
R_KB = R_QB + B_QK
R_VA = R_KB + B_QK
R_LR = R_VA + A_KV
R_WIDTH = R_LR + LANES
QK_WIDTH = A_Q + A_KV

VMEM_LIMIT = 48 * 1024 * 1024
PEER_VMEM_LIMIT = 56 * 1024 * 1024

TM_QK = 512
TM_REST = 512
TN_REST = R_WIDTH // 2
TM_MERGE = 256
TP_POOL = 512
TT_TOPK = 256
TT_PEER = 512
IB_PEER = 8


def _params(sem):
    return pltpu.CompilerParams(dimension_semantics=sem, vmem_limit_bytes=VMEM_LIMIT)


def _qk_kernel(x_ref, w_ref, b_ref, c_ref, s1_ref, s2_ref, o_ref):
    y = jnp.dot(x_ref[...], w_ref[...], preferred_element_type=F32) + b_ref[...]
    c = c_ref[...]
    s1 = s1_ref[...]
    s2 = s2_ref[...]
    for j in range(QK_WIDTH // LANES):
        yj = y[:, j * LANES:(j + 1) * LANES]
        up = pltpu.roll(yj, LANES - ROT_DIM // 2, axis=1)
        dn = pltpu.roll(yj, ROT_DIM // 2, axis=1)
        o_ref[:, j * LANES:(j + 1) * LANES] = yj * c + up * s1 + dn * s2


def _proj_qk(xb, w, b, rope_c, rope_s1, rope_s2):
    n_prompt_blocks = SEQ // TM_QK

    def tab_map(i):
        return (jnp.where(i < N_PROMPT // TM_QK, i % n_prompt_blocks, n_prompt_blocks), 0)

    tab_spec = pl.BlockSpec((TM_QK, LANES), tab_map)
    return pl.pallas_call(
        _qk_kernel,
        grid=(N_TOK // TM_QK,),
        in_specs=[
            pl.BlockSpec((TM_QK, D_MODEL), lambda i: (i, 0)),
            pl.BlockSpec((D_MODEL, QK_WIDTH), lambda i: (0, 0)),
            pl.BlockSpec((1, QK_WIDTH), lambda i: (0, 0)),
            tab_spec, tab_spec, tab_spec,
        ],
        out_specs=pl.BlockSpec((TM_QK, QK_WIDTH), lambda i: (i, 0)),
        out_shape=jax.ShapeDtypeStruct((N_TOK, QK_WIDTH), F32),
        compiler_params=_params(("parallel",)),
        name="proj_qk",
    )(xb, w, b, rope_c, rope_s1, rope_s2)


def _mm_bias_kernel(x_ref, w_ref, b_ref, o_ref):
    o_ref[...] = jnp.dot(x_ref[...], w_ref[...], preferred_element_type=F32) + b_ref[...]


def _proj_rest(xb, w, b):
    return pl.pallas_call(
        _mm_bias_kernel,
        grid=(R_WIDTH // TN_REST, N_TOK // TM_REST),
        in_specs=[
            pl.BlockSpec((TM_REST, D_MODEL), lambda j, i: (i, 0)),
            pl.BlockSpec((D_MODEL, TN_REST), lambda j, i: (0, j)),
            pl.BlockSpec((1, TN_REST), lambda j, i: (0, j)),
        ],
        out_specs=pl.BlockSpec((TM_REST, TN_REST), lambda j, i: (i, j)),
        out_shape=jax.ShapeDtypeStruct((N_TOK, R_WIDTH), F32),
        compiler_params=_params(("parallel", "arbitrary")),
        name="proj_rest",
    )(xb, w, b)


def _attend(q, kk, vv, sink_ref, c_min, o_ref, row0):
    tq = q.shape[0]
    r = lax.broadcasted_iota(jnp.int32, (tq, 2 * WINDOW), 0)
    c = lax.broadcasted_iota(jnp.int32, (tq, 2 * WINDOW), 1)
    ok = (c > r) & (c <= r + WINDOW) & (c >= c_min)
    qb = (q * (A_HEAD_DIM ** -0.5)).astype(BF16)
    for h in range(A_HEADS):
        g = h // A_GROUP
        qh = qb[:, h * A_HEAD_DIM:(h + 1) * A_HEAD_DIM]
        kg = kk[:, g * A_HEAD_DIM:(g + 1) * A_HEAD_DIM]
        vg = vv[:, g * A_HEAD_DIM:(g + 1) * A_HEAD_DIM]
        s = lax.dot_general(qh, kg, (((1,), (1,)), ((), ())), preferred_element_type=F32)
        s = jnp.where(ok, s, NEG_INF)
        sink = sink_ref[h]
        m = jnp.maximum(jnp.max(s, axis=1, keepdims=True), sink)
        p = jnp.exp(s - m)
        denom = jnp.sum(p, axis=1, keepdims=True) + jnp.exp(sink - m)
        o = jnp.dot(p.astype(BF16), vg, preferred_element_type=F32) / denom
        o_ref[pl.ds(row0, tq), h * A_HEAD_DIM:(h + 1) * A_HEAD_DIM] = o.astype(o_ref.dtype)


def _attn_prompt_kernel(sink_ref, q_ref, kc_ref, kp_ref, vc_ref, vp_ref, o_ref):
    n = pl.program_id(1)
    kk = jnp.concatenate([kp_ref[...], kc_ref[...]], axis=0).astype(BF16)
    vv = jnp.concatenate([vp_ref[...], vc_ref[...]], axis=0).astype(BF16)
    _attend(q_ref[...], kk, vv, sink_ref, jnp.where(n > 0, 0, WINDOW), o_ref, 0)


def _attn_prompt(qk, rest, sinks):
    nb = SEQ // WINDOW
    kcol = A_Q // A_KV
    vcol = R_VA // A_KV

    def cur(b, n):
        return b * nb + n

    def prev(b, n):
        return b * nb + jnp.maximum(n - 1, 0)

    return pl.pallas_call(
        _attn_prompt_kernel,
        grid=(BATCH, nb),
        in_specs=[
            pl.BlockSpec(memory_space=pltpu.SMEM),
            pl.BlockSpec((WINDOW, A_Q), lambda b, n: (cur(b, n), 0)),
            pl.BlockSpec((WINDOW, A_KV), lambda b, n: (cur(b, n), kcol)),
            pl.BlockSpec((WINDOW, A_KV), lambda b, n: (prev(b, n), kcol)),
            pl.BlockSpec((WINDOW, A_KV), lambda b, n: (cur(b, n), vcol)),
            pl.BlockSpec((WINDOW, A_KV), lambda b, n: (prev(b, n), vcol)),
        ],
        out_specs=pl.BlockSpec((WINDOW, A_Q), lambda b, n: (cur(b, n), 0)),
        out_shape=jax.ShapeDtypeStruct((N_TOK, A_Q), BF16),
        compiler_params=_params(("parallel", "arbitrary")),
        name="attn_prompt",
    )(sinks, qk, qk, qk, rest, rest)


BB_ATTN = 8


def _attn_sample_kernel(sink_ref, q_ref, kn_ref, vn_ref, ks_ref, vs_ref, prompt_out_ref, o_ref):
    del prompt_out_ref
    pad = jnp.zeros((WINDOW - DEC_SEQ, A_KV), F32)
    rows = A_GROUP * DEC_SEQ
    t = lax.broadcasted_iota(jnp.int32, (rows, 2 * WINDOW), 0) % DEC_SEQ
    c = lax.broadcasted_iota(jnp.int32, (rows, 2 * WINDOW), 1)
    ok = (c > t) & (c <= t + WINDOW)
    nt = (((1,), (1,)), ((), ()))

    def body(bb, carry):
        row0 = pl.multiple_of(bb * DEC_SEQ, DEC_SEQ)
        q = q_ref[pl.ds(row0, DEC_SEQ), :] * (A_HEAD_DIM ** -0.5)
        kk = jnp.concatenate([ks_ref[bb], kn_ref[pl.ds(row0, DEC_SEQ), :], pad], axis=0).astype(BF16)
        vv = jnp.concatenate([vs_ref[bb], vn_ref[pl.ds(row0, DEC_SEQ), :], pad], axis=0).astype(BF16)
        for g in range(A_KV_HEADS):
            heads = range(g * A_GROUP, (g + 1) * A_GROUP)
            qg = jnp.concatenate([q[:, h * A_HEAD_DIM:(h + 1) * A_HEAD_DIM] for h in heads], axis=0)
            ds = slice(g * A_HEAD_DIM, (g + 1) * A_HEAD_DIM)
            s = lax.dot_general(qg.astype(BF16), kk[:, ds], nt, preferred_element_type=F32)
            s = jnp.where(ok, s, NEG_INF)
            sink = sink_ref[g * rows:(g + 1) * rows, 0:1]
            m = jnp.maximum(jnp.max(s, axis=1, keepdims=True), sink)
            p = jnp.exp(s - m)
            denom = jnp.sum(p, axis=1, keepdims=True) + jnp.exp(sink - m)
            o = jnp.dot(p.astype(BF16), vv[:, ds], preferred_element_type=F32) / denom
            for k, h in enumerate(heads):
                o_ref[pl.ds(row0, DEC_SEQ), h * A_HEAD_DIM:(h + 1) * A_HEAD_DIM] = (
                    o[k * DEC_SEQ:(k + 1) * DEC_SEQ, :].astype(o_ref.dtype))
        return carry

    lax.fori_loop(0, BB_ATTN, body, 0)


def _attn_sample(qk, rest, sinks, k_state, v_state, prompt_out):
    rows = BB_ATTN * DEC_SEQ
    base = N_PROMPT // rows
    kcol = A_Q // A_KV
    vcol = R_VA // A_KV
    sink_rows = jnp.broadcast_to(jnp.repeat(sinks, DEC_SEQ)[:, None], (A_HEADS * DEC_SEQ, LANES))
    return pl.pallas_call(
        _attn_sample_kernel,
        grid=(DEC_BATCH // BB_ATTN,),
        in_specs=[
            pl.BlockSpec((A_HEADS * DEC_SEQ, LANES), lambda i: (0, 0)),
            pl.BlockSpec((rows, A_Q), lambda i: (base + i, 0)),
            pl.BlockSpec((rows, A_KV), lambda i: (base + i, kcol)),
            pl.BlockSpec((rows, A_KV), lambda i: (base + i, vcol)),
            pl.BlockSpec((BB_ATTN, WINDOW, A_KV), lambda i: (i, 0, 0)),
            pl.BlockSpec((BB_ATTN, WINDOW, A_KV), lambda i: (i, 0, 0)),
            pl.BlockSpec(memory_space=pl.ANY),
        ],
        out_specs=pl.BlockSpec((rows, A_Q), lambda i: (base + i, 0)),
        out_shape=jax.ShapeDtypeStruct((N_TOK, A_Q), BF16),
        input_output_aliases={6: 0},
        compiler_params=_params(("parallel",)),
        name="attn_sample",
    )(sink_rows, qk, qk, rest, k_state, v_state, prompt_out)


def _split3(x):
    hi = x.astype(BF16)
    r1 = x - hi.astype(F32)
    mid = r1.astype(BF16)
    lo = (r1 - mid.astype(F32)).astype(BF16)
    return hi, mid, lo


def _gla_kernel(lr_ref, q_ref, k_ref, v_ref, gb_ref, s0_ref, wa_ref, ba_ref, g_ref, *refs, n_chunks):
    o_ref, sout_ref, st_ref = refs[-3:]
    ci = pl.program_id(1)
    c = q_ref.shape[0]

    @pl.when(ci == 0)
    def _():
        for h in range(B_HEADS):
            st_ref[h] = s0_ref[0, 0, h].T

    z = jnp.dot(lr_ref[...].astype(BF16), wa_ref[...], preferred_element_type=F32) + ba_ref[...]
    log_a = -(jnp.maximum(-z, 0.0) + jnp.log1p(jnp.exp(-jnp.abs(z)))) / B_TAU
    ri = lax.broadcasted_iota(jnp.int32, (c, c), 0)
    cj = lax.broadcasted_iota(jnp.int32, (c, c), 1)
    causal = cj <= ri
    tri = jnp.where(causal, 1.0, 0.0).astype(BF16)
    hi, mid, lo = _split3(log_a)
    b = (jnp.dot(tri, hi, preferred_element_type=F32)
         + jnp.dot(tri, mid, preferred_element_type=F32)
         + jnp.dot(tri, lo, preferred_element_type=F32))
    bl = b[c - 1:c, :]
    q = q_ref[...] * (B_DK ** -0.5)
    k = k_ref[...]
    qd = (q * jnp.exp(b)).astype(BF16)
    kd = (k * jnp.exp(-b)).astype(BF16)
    kl = (k * jnp.exp(bl - b)).astype(BF16)
    ebl = jnp.exp(bl)
    nt = (((1,), (1,)), ((), ()))
    for h in range(B_HEADS):
        ks = slice(h * B_DK, (h + 1) * B_DK)
        vs = slice(h * B_DV, (h + 1) * B_DV)
        vh = v_ref[:, vs]
        st = st_ref[h]
        o = lax.dot_general(qd[:, ks], st.astype(BF16), nt, preferred_element_type=F32)
        att = lax.dot_general(qd[:, ks], kd[:, ks], nt, preferred_element_type=F32)
        att = jnp.where(causal, att, 0.0)
        o = o + jnp.dot(att.astype(BF16), vh.astype(BF16), preferred_element_type=F32)
        st_ref[h] = st * ebl[:, ks] + jnp.dot(vh.T.astype(BF16), kl[:, ks], preferred_element_type=F32)
        o = o * lax.rsqrt(jnp.mean(o * o, axis=1, keepdims=True) + RMS_EPS) * g_ref[h:h + 1, :]
        gate = gb_ref[:, vs]
        o_ref[:, vs] = (o * (gate / (1.0 + jnp.exp(-gate)))).astype(o_ref.dtype)

    @pl.when(ci == n_chunks - 1)
    def _():
        for h in range(B_HEADS):
            sout_ref[0, 0, h] = st_ref[h].T


def _gla(rest, s0, layer, wa, ba, g, *, n_batch, seq, chunk, row_base, fill_into=None, states_into=None):
    n_chunks = seq // chunk
    base = row_base // chunk

    def row(b, ci):
        return base + b * n_chunks + ci

    state_spec = pl.BlockSpec((1, 1, B_HEADS, B_DK, B_DV), lambda b, ci: (layer, b, 0, 0, 0))
    s0_layer = layer if s0.shape[0] > 1 else 0
    s0_spec = pl.BlockSpec((1, 1, B_HEADS, B_DK, B_DV), lambda b, ci: (s0_layer, b, 0, 0, 0))
    extra_args, aliases = [], {}
    for arg, out_idx in ((fill_into, 0), (states_into, 1)):
        if arg is not None:
            aliases[9 + len(extra_args)] = out_idx
            extra_args.append(arg)
    return pl.pallas_call(
        functools.partial(_gla_kernel, n_chunks=n_chunks),
        grid=(n_batch, n_chunks),
        input_output_aliases=aliases,
        in_specs=[
            pl.BlockSpec((chunk, LANES), lambda b, ci: (row(b, ci), R_LR // LANES)),
            pl.BlockSpec((chunk, B_QK), lambda b, ci: (row(b, ci), R_QB // B_QK)),
            pl.BlockSpec((chunk, B_QK), lambda b, ci: (row(b, ci), R_KB // B_QK)),
            pl.BlockSpec((chunk, B_V), lambda b, ci: (row(b, ci), R_VB // B_V)),
            pl.BlockSpec((chunk, B_V), lambda b, ci: (row(b, ci), R_GB // B_V)),
            s0_spec,
            pl.BlockSpec((LANES, B_QK), lambda b, ci: (0, 0)),
            pl.BlockSpec((1, B_QK), lambda b, ci: (0, 0)),
            pl.BlockSpec((B_HEADS, B_DV), lambda b, ci: (0, 0)),
        ] + [pl.BlockSpec(memory_space=pl.ANY)] * len(extra_args),
        out_specs=[
            pl.BlockSpec((chunk, B_V), lambda b, ci: (row(b, ci), 0)),
            state_spec,
        ],
        out_shape=[
            jax.ShapeDtypeStruct((N_TOK, B_V), BF16),
            jax.ShapeDtypeStruct((DEPTH, n_batch, B_HEADS, B_DK, B_DV), F32),
        ],
        scratch_shapes=[pltpu.VMEM((B_HEADS, B_DV, B_DK), F32)],
        compiler_params=_params(("parallel", "arbitrary")),
        name="gla",
    )(rest, rest, rest, rest, rest, s0, wa, ba, g, *extra_args)


HALO = 16


def _pool_kernel(u_ref, prev_ref, w_ref, scale_ref, *refs, from_start):
    o_ref = refs[-1]
    tp = u_ref.shape[0]
    u = u_ref[...]
    prev = prev_ref[...]
    if from_start:
        ti = pl.program_id(1)
        prev = jnp.where(ti > 0, prev, 0.0)
        t0 = ti * tp
    full = jnp.concatenate([prev, u], axis=0)
    for g, w in enumerate(POOL_WINDOWS):
        cs = slice(g * C_GROUP_W, (g + 1) * C_GROUP_W)
        acc = full[:, cs]
        span = 1
        while span < w:
            acc = acc + pltpu.roll(acc, span, axis=0)
            span *= 2
        wsum = acc[HALO:, :]
        if from_start:
            t = t0 + lax.broadcasted_iota(jnp.int32, (tp, C_GROUP_W), 0)
            cnt = jnp.minimum(t + 1, w).astype(F32)
        else:
            cnt = float(w)
        d = wsum / cnt - u[:, cs]
        y = jnp.dot(d.astype(BF16), w_ref[g], preferred_element_type=F32) * scale_ref[:, cs]
        o_ref[:, cs] = y.astype(o_ref.dtype)


def _pool_prompt(rest, w, scale):
    nt = SEQ // TP_POOL
    ucol = R_UC // C_WIDTH

    def halo(b, i):
        return (jnp.maximum((b * SEQ + i * TP_POOL) // HALO - 1, 0), ucol)

    return pl.pallas_call(
        functools.partial(_pool_kernel, from_start=True),
        grid=(BATCH, nt),
        in_specs=[
            pl.BlockSpec((TP_POOL, C_WIDTH), lambda b, i: (b * nt + i, ucol)),
            pl.BlockSpec((HALO, C_WIDTH), halo),
            pl.BlockSpec((C_GROUPS, C_GROUP_W, C_GROUP_W), lambda b, i: (0, 0, 0)),
            pl.BlockSpec((1, C_WIDTH), lambda b, i: (0, 0)),
        ],
        out_specs=pl.BlockSpec((TP_POOL, C_WIDTH), lambda b, i: (b * nt + i, 0)),
        out_shape=jax.ShapeDtypeStruct((N_TOK, C_WIDTH), BF16),
        compiler_params=_params(("parallel", "arbitrary")),
        name="pool_prompt",
    )(rest, rest, w, scale)


def _pool_sample(rest, prev, w, scale, prompt_out):
    ucol = R_UC // C_WIDTH
    base = N_PROMPT // DEC_SEQ
    return pl.pallas_call(
        functools.partial(_pool_kernel, from_start=False),
        grid=(DEC_BATCH,),
        in_specs=[
            pl.BlockSpec((DEC_SEQ, C_WIDTH), lambda b: (base + b, ucol)),
            pl.BlockSpec((HALO, C_WIDTH), lambda b: (b, 0)),
            pl.BlockSpec((C_GROUPS, C_GROUP_W, C_GROUP_W), lambda b: (0, 0, 0)),
            pl.BlockSpec((1, C_WIDTH), lambda b: (0, 0)),
            pl.BlockSpec(memory_space=pl.ANY),
        ],
        out_specs=pl.BlockSpec((DEC_SEQ, C_WIDTH), lambda b: (base + b, 0)),
        out_shape=jax.ShapeDtypeStruct((N_TOK, C_WIDTH), BF16),
        input_output_aliases={4: 0},
        compiler_params=_params(("parallel",)),
        name="pool_sample",
    )(rest, prev, w, scale, prompt_out)


def _layer_norm(x, g, b):
    mu = jnp.mean(x, axis=1, keepdims=True)
    xc = x - mu
    var = jnp.mean(xc * xc, axis=1, keepdims=True)
    return xc * lax.rsqrt(var + LN_EPS) * g + b


def _merge_kernel(x_ref, gates_ref, oa_ref, ob_ref, oc_ref, wa_ref, wb_ref, wc_ref, wo_ref,
                  g_ref, b_ref, o_ref, oT_ref):
    def gate(i):
        z = gates_ref[:, i * D_MODEL:(i + 1) * D_MODEL]
        return 1.0 / (1.0 + jnp.exp(-z))

    merged = (gate(0) * jnp.dot(oa_ref[...], wa_ref[...], preferred_element_type=F32)
              + gate(1) * jnp.dot(ob_ref[...], wb_ref[...], preferred_element_type=F32)
              + gate(2) * jnp.dot(oc_ref[...], wc_ref[...], preferred_element_type=F32))
    mix = jnp.dot(merged.astype(BF16), wo_ref[...], preferred_element_type=F32)
    y = _layer_norm(DN_ALPHA * x_ref[...] + mix, g_ref[...], b_ref[...])
    o_ref[...] = y
    oT_ref[...] = y.T.astype(BF16)


def _merge(x, rest, oa, ob, oc, wa, wb, wc, wo, g, b):
    row = pl.BlockSpec((TM_MERGE, D_MODEL), lambda i: (i, 0))
    wspec = pl.BlockSpec((D_MODEL, D_MODEL), lambda i: (0, 0))
    vec = pl.BlockSpec((1, D_MODEL), lambda i: (0, 0))
    return pl.pallas_call(
        _merge_kernel,
        grid=(N_TOK // TM_MERGE,),
        in_specs=[row, pl.BlockSpec((TM_MERGE, 3 * D_MODEL), lambda i: (i, R_GATES)),
                  row, row, row, wspec, wspec, wspec, wspec, vec, vec],
        out_specs=[row, pl.BlockSpec((D_MODEL, TM_MERGE), lambda i: (0, i))],
        out_shape=[jax.ShapeDtypeStruct((N_TOK, D_MODEL), F32),
                   jax.ShapeDtypeStruct((D_MODEL, N_TOK), BF16)],
        compiler_params=_params(("parallel",)),
        name="merge",
    )(x, rest, oa, ob, oc, wa, wb, wc, wo, g, b)


def _extract_desc(s, n, track_rank=False):
    vals = []
    rank = jnp.full(s.shape, float(n), F32) if track_rank else None
    for r in range(n):
        m = jnp.max(s, axis=0, keepdims=True)
        vals.append(m)
        hit = s == m
        if track_rank:
            rank = jnp.where(hit, float(r), rank)
        s = jnp.where(hit, -jnp.inf, s)
    return vals, rank


def _peer_topk_kernel(xT_ref, wq_ref, sk_ref, cnt_ref, e1_ref, rank_ref, e2_ref, s1_scr, sv_scr):
    tt = xT_ref.shape[1]
    xT = xT_ref[...]
    for hp in range(2 * PEER_HEADS):
        h, second = divmod(hp, 2)
        qT = jnp.dot(wq_ref[hp * PEER_DHALF:(hp + 1) * PEER_DHALF, :], xT, preferred_element_type=F32)
        s = jnp.dot(sk_ref[hp], qT.astype(BF16), preferred_element_type=F32)
        vals, rank = _extract_desc(s, PEER_TOPK, track_rank=bool(second))
        sv_scr[hp] = jnp.concatenate(vals, axis=0)
        if second:
            rank_ref[h] = rank.astype(BF16)
            e2_ref[h] = jnp.exp(s - vals[0]).astype(BF16)
        else:
            s1_scr[h] = s
    row8 = lax.broadcasted_iota(jnp.int32, (8, tt), 0)
    for h in range(PEER_HEADS):
        sv1 = sv_scr[2 * h]
        sv2 = sv_scr[2 * h + 1]
        pieces = [sv1[0:1, :] + sv2]
        for a in range(1, 8):
            nb = PEER_TOPK // (a + 1)
            pieces.append(jnp.where(row8 < nb, sv1[a:a + 1, :] + sv2[0:8, :], -jnp.inf))
        pieces.append(sv2[0:1, :] + sv1[8:16, :])
        cand, _ = _extract_desc(jnp.concatenate(pieces, axis=0), PEER_TOPK)
        top = cand[0]
        z = jnp.zeros_like(top)
        for r in range(PEER_TOPK):
            z = z + jnp.exp(cand[r] - top)
        kth = cand[PEER_TOPK - 1]
        counts = [jnp.sum(jnp.where(p >= kth, 1.0, 0.0), axis=0, keepdims=True) for p in pieces[:8]]
        tail = jnp.where(pieces[8] >= kth, 1.0, 0.0)
        counts += [tail[a:a + 1, :] for a in range(8)]
        s1 = s1_scr[h]
        cnt = jnp.zeros_like(s1)
        for a in range(PEER_TOPK):
            cnt = jnp.where(s1 == sv1[a:a + 1, :], counts[a], cnt)
        e1 = jnp.exp(s1 - (sv1[0:1, :] + jnp.log(z))) * SQRT_HALF
        for c in range(tt // LANES):
            cnt_ref[h, c] = cnt[:, c * LANES:(c + 1) * LANES]
            e1_ref[h, c] = e1[:, c * LANES:(c + 1) * LANES]


def _peer_topk(xT, wqT, sk):
    spec = pl.BlockSpec((PEER_HEADS, N_KEYS, TT_TOPK), lambda t: (0, 0, t))
    row_spec = pl.BlockSpec((PEER_HEADS, TT_TOPK // LANES, N_KEYS, LANES), lambda t: (0, t, 0, 0))
    wide = jax.ShapeDtypeStruct((PEER_HEADS, N_TOK // LANES, N_KEYS, LANES), F32)
    narrow = jax.ShapeDtypeStruct((PEER_HEADS, N_KEYS, N_TOK), BF16)
    return pl.pallas_call(
        _peer_topk_kernel,
        grid=(N_TOK // TT_TOPK,),
        in_specs=[
            pl.BlockSpec((D_MODEL, TT_TOPK), lambda t: (0, t)),
            pl.BlockSpec((PEER_HEADS * PEER_DKEY, D_MODEL), lambda t: (0, 0)),
            pl.BlockSpec((2 * PEER_HEADS, N_KEYS, PEER_DHALF), lambda t: (0, 0, 0)),
        ],
        out_specs=[row_spec, row_spec, spec, spec],
        out_shape=[wide, wide, narrow, narrow],
        scratch_shapes=[pltpu.VMEM((PEER_HEADS, N_KEYS, TT_TOPK), F32),
                        pltpu.VMEM((2 * PEER_HEADS, PEER_TOPK, TT_TOPK), F32)],
        compiler_params=_params(("parallel",)),
        name="peer_topk",
    )(xT, wqT, sk)


SQRT_HALF = 2.0 ** -0.5


def _gelu_unscaled(x):
    t = x * SQRT_HALF
    return t * (1.0 + lax.erf(t))


LC_PEER = 256
MC_PEER = 512


def _bf16_rows(ref, h, r, chunks, n_rows):
    x = jnp.concatenate([ref[h, c, pl.ds(r, 16, stride=0), :] for c in chunks], axis=1)
    packed = x.astype(BF16)
    return jnp.concatenate([packed] * (n_rows // packed.shape[0]), axis=0)


def _peer_main_kernel(xT_ref, u_ref, vt_ref, cnt_ref, e1_ref, rank_ref, e2_ref, yT_ref, *scratch):
    i = pl.program_id(1)
    n_mc = len(scratch) // 2
    s_scrs, wh_scrs = scratch[:n_mc], scratch[n_mc:]

    @pl.when(i == 0)
    def _():
        yT_ref[...] = jnp.zeros_like(yT_ref)

    zero = jnp.zeros((N_KEYS, LC_PEER), BF16)
    for mc in range(n_mc):
        mcols = slice(mc * MC_PEER, (mc + 1) * MC_PEER)
        s_scr, wh_scr = s_scrs[mc], wh_scrs[mc]
        s_scr[...] = jnp.dot(u_ref[0], xT_ref[:, mcols], preferred_element_type=F32)
        for ii in range(IB_PEER):
            rows = slice(ii * N_KEYS, (ii + 1) * N_KEYS)
            for lc in range(MC_PEER // LC_PEER):
                lcols = slice(lc * LC_PEER, (lc + 1) * LC_PEER)
                lane0 = mc * MC_PEER + lc * LC_PEER
                cols = slice(lane0, lane0 + LC_PEER)
                chunks = range(lane0 // LANES, (lane0 + LC_PEER) // LANES)
                w = zero
                for h in range(PEER_HEADS):
                    cnt = _bf16_rows(cnt_ref, h, ii, chunks, N_KEYS)
                    e1 = _bf16_rows(e1_ref, h, ii, chunks, N_KEYS)
                    w = w + jnp.where(rank_ref[h, :, cols] < cnt, e2_ref[h, :, cols] * e1, zero)
                wh_scr[rows, lcols] = w * _gelu_unscaled(s_scr[rows, lcols]).astype(BF16)
        yT_ref[:, mcols] += jnp.dot(vt_ref[0], wh_scr[...], preferred_element_type=F32)


def _peer_main(xT, u, vt, layer, cnt, e1, rank, e2):
    eb = IB_PEER * N_KEYS
    n_mc = TT_PEER // MC_PEER
    row_spec = pl.BlockSpec((PEER_HEADS, TT_PEER // LANES, IB_PEER, LANES), lambda t, i: (0, t, i, 0))
    tab_spec = pl.BlockSpec((PEER_HEADS, N_KEYS, TT_PEER), lambda t, i: (0, 0, t))
    return pl.pallas_call(
        _peer_main_kernel,
        grid=(N_TOK // TT_PEER, N_EXPERTS // eb),
        in_specs=[
            pl.BlockSpec((D_MODEL, TT_PEER), lambda t, i: (0, t)),
            pl.BlockSpec((1, eb, D_MODEL), lambda t, i: (layer, i, 0)),
            pl.BlockSpec((1, D_MODEL, eb), lambda t, i: (layer, 0, i)),
            row_spec, row_spec, tab_spec, tab_spec,
        ],
        out_specs=pl.BlockSpec((D_MODEL, TT_PEER), lambda t, i: (0, t)),
        out_shape=jax.ShapeDtypeStruct((D_MODEL, N_TOK), F32),
        scratch_shapes=([pltpu.VMEM((eb, MC_PEER), F32)] * n_mc + [pltpu.VMEM((eb, MC_PEER), BF16)] * n_mc),
        compiler_params=_params(("parallel", "arbitrary")),
        name="peer_main",
    )(xT, u, vt, cnt, e1, rank, e2)


def _peer_out_kernel(x_ref, yT_ref, g_ref, b_ref, o_ref, ob_ref):
    y = _layer_norm(DN_ALPHA * x_ref[...] + yT_ref[...].T, g_ref[...], b_ref[...])
    o_ref[...] = y
    ob_ref[...] = y.astype(BF16)


def _peer_out(x1, yT, g, b):
    row = pl.BlockSpec((TM_MERGE, D_MODEL), lambda i: (i, 0))
    vec = pl.BlockSpec((1, D_MODEL), lambda i: (0, 0))
    return pl.pallas_call(
        _peer_out_kernel,
        grid=(N_TOK // TM_MERGE,),
        in_specs=[row, pl.BlockSpec((D_MODEL, TM_MERGE), lambda i: (0, i)), vec, vec],
        out_specs=[row, row],
        out_shape=[jax.ShapeDtypeStruct((N_TOK, D_MODEL), F32),
                   jax.ShapeDtypeStruct((N_TOK, D_MODEL), BF16)],
        compiler_params=_params(("parallel",)),
        name="peer_out",
    )(x1, yT, g, b)


def _rope_tables():
    half = ROT_DIM // 2
    pos = jnp.concatenate([jnp.arange(SEQ), PAST_LEN + (jnp.arange(TM_QK) % DEC_SEQ)])
    inv = ROPE_THETA ** (-jnp.arange(half, dtype=F32) / half)
    ang = pos.astype(F32)[:, None] * inv[None, :]
    cos, sin = jnp.cos(ang), jnp.sin(ang)
    n = pos.shape[0]
    one = jnp.ones((n, A_HEAD_DIM - ROT_DIM), F32)
    zero = jnp.zeros((n, A_HEAD_DIM - ROT_DIM), F32)
    zh = jnp.zeros((n, half), F32)
    reps = LANES // A_HEAD_DIM
    c = jnp.tile(jnp.concatenate([cos, cos, one], 1), (1, reps))
    s1 = jnp.tile(jnp.concatenate([-sin, zh, zero], 1), (1, reps))
    s2 = jnp.tile(jnp.concatenate([zh, sin, zero], 1), (1, reps))
    return c, s1, s2


def _split_cols(w):
    cuts = [int(c) for c in np.cumsum(SPLITS)[:-1]]
    return jnp.split(w, cuts, axis=-1)


def _layer(layer, x, xb, k_state, v_state, gla_states, gla_out, pool_state, rope, w_in, b_in, sinks,
           w_alpha, b_alpha, gla_g, w_pool, pool_scale, w_a, w_b, w_c, w_out, ln1_g, ln1_b,
           peer_query, peer_subkeys, peer_u, peer_vt, ln2_g, ln2_b):
    qa_w, ka_w, va_w, qb_w, kb_w, vb_w, lr_w, gb_w, uc_w, gates_w = _split_cols(w_in)
    qa_b, ka_b, va_b, qb_b, kb_b, vb_b, lr_b, gb_b, uc_b, gates_b = _split_cols(b_in[None, :])
    lr_pad = LANES - B_GATE_RANK
    w_qk = jnp.concatenate([qa_w, ka_w], 1).astype(BF16)
    b_qk = jnp.concatenate([qa_b, ka_b], 1)
    w_rest = jnp.concatenate([gates_w, vb_w, gb_w, uc_w, qb_w, kb_w, va_w,
                              jnp.pad(lr_w, ((0, 0), (0, lr_pad)))], 1).astype(BF16)
    b_rest = jnp.concatenate([gates_b, vb_b, gb_b, uc_b, qb_b, kb_b, va_b,
                              jnp.pad(lr_b, ((0, 0), (0, lr_pad)))], 1)

    qk = _proj_qk(xb, w_qk, b_qk, *rope)
    rest = _proj_rest(xb, w_rest, b_rest)

    ks = k_state.reshape(DEC_BATCH, WINDOW, A_KV)
    vs = v_state.reshape(DEC_BATCH, WINDOW, A_KV)
    oa = _attn_sample(qk, rest, sinks, ks, vs, _attn_prompt(qk, rest, sinks))

    wa = jnp.pad(w_alpha, ((0, lr_pad), (0, 0))).astype(BF16)
    ba = b_alpha[None, :]
    gla_p, gla_s = gla_out
    ob, gla_p = _gla(rest, jnp.zeros((1, BATCH, B_HEADS, B_DK, B_DV), F32), layer, wa, ba, gla_g,
                     n_batch=BATCH, seq=SEQ, chunk=GLA_CHUNK, row_base=0, states_into=gla_p)
    ob, gla_s = _gla(rest, gla_states, layer, wa, ba, gla_g, n_batch=DEC_BATCH, seq=DEC_SEQ,
                     chunk=math.gcd(DEC_SEQ, GLA_CHUNK), row_base=N_PROMPT, fill_into=ob, states_into=gla_s)

    wp = w_pool.astype(BF16)
    ps = pool_scale[None, :]
    prev = jnp.pad(pool_state, ((0, 0), (HALO - POOL_STATE, 0), (0, 0))).reshape(DEC_BATCH * HALO, C_WIDTH)
    oc = _pool_sample(rest, prev, wp, ps, _pool_prompt(rest, wp, ps))

    x1, x1T = _merge(x, rest, oa, ob, oc, w_a.astype(BF16), w_b.astype(BF16), w_c.astype(BF16),
                     w_out.astype(BF16), ln1_g[None, :], ln1_b[None, :])

    wqT = peer_query.reshape(D_MODEL, PEER_HEADS * PEER_DKEY).T.astype(BF16)
    sk = peer_subkeys.reshape(2 * PEER_HEADS, N_KEYS, PEER_DHALF).astype(BF16)
    cnt, e1, rank, e2 = _peer_topk(x1T, wqT, sk)
    yT = _peer_main(x1T, peer_u, peer_vt, layer, cnt, e1, rank, e2)
    x2, x2b = _peer_out(x1, yT, ln2_g[None, :], ln2_b[None, :])

    def prompt_tail(t, col0, width, n):
        return jnp.stack([t[(b + 1) * SEQ - n:(b + 1) * SEQ, col0:col0 + width] for b in range(BATCH)])

    def sample_tail(state, t, col0, width, n):
        new = t[N_PROMPT:, col0:col0 + width].reshape(DEC_BATCH, DEC_SEQ, width)
        return jnp.concatenate([state, new], 1)[:, -n:]

    kv_shape = (-1, WINDOW, A_KV_HEADS, A_HEAD_DIM)
    states = (prompt_tail(qk, A_Q, A_KV, WINDOW).reshape(kv_shape),
              prompt_tail(rest, R_VA, A_KV, WINDOW).reshape(kv_shape),
              prompt_tail(rest, R_UC, C_WIDTH, POOL_STATE),
              sample_tail(ks, qk, A_Q, A_KV, WINDOW).reshape(kv_shape),
              sample_tail(vs, rest, R_VA, A_KV, WINDOW).reshape(kv_shape),
              sample_tail(pool_state, rest, R_UC, C_WIDTH, POOL_STATE))
    return x2, x2b, states, (gla_p, gla_s)


def kernel(x_prompt, x_sample, state_win_k, state_win_v, state_gla, state_pool, w_in, b_in, attn_sinks,
           w_alpha, b_alpha, gla_norm_g, w_pool, pool_scale, w_branch_a, w_branch_b, w_branch_c, w_out,
           ln1_g, ln1_b, peer_query, peer_subkeys, peer_u, peer_v, ln2_g, ln2_b):
    x = jnp.concatenate([x_prompt.reshape(N_PROMPT, D_MODEL), x_sample.reshape(N_SAMPLE, D_MODEL)], 0)
    xb = x.astype(BF16)
    rope = _rope_tables()
    peer_ub = peer_u.astype(BF16)
    peer_vtb = jnp.swapaxes(peer_v, 1, 2).astype(BF16)
    per_layer = []
    gla_out = (None, None)
    for l in range(DEPTH):
        x, xb, states, gla_out = _layer(
            l, x, xb, state_win_k[l], state_win_v[l], state_gla, gla_out, state_pool[l], rope,
            w_in[l], b_in[l], attn_sinks[l], w_alpha[l], b_alpha[l], gla_norm_g[l],
            w_pool[l], pool_scale[l], w_branch_a[l], w_branch_b[l], w_branch_c[l], w_out[l],
            ln1_g[l], ln1_b[l], peer_query[l], peer_subkeys[l], peer_ub, peer_vtb, ln2_g[l], ln2_b[l])
        per_layer.append(states)
    pk, pv, pp, sk, sv, sp = [jnp.stack([per_layer[l][i] for l in range(DEPTH)]) for i in range(6)]
    return (x[:N_PROMPT].reshape(BATCH, SEQ, D_MODEL), x[N_PROMPT:].reshape(DEC_BATCH, DEC_SEQ, D_MODEL),
            pk, pv, gla_out[0], pp, sk, sv, gla_out[1], sp)
```

```python
import functools
import math

import jax
import jax.numpy as jnp
import numpy as np
from jax import lax
from jax.experimental import pallas as pl
from jax.experimental.pallas import tpu as pltpu

F32 = jnp.float32
BF16 = jnp.bfloat16

D_MODEL = 1024
BATCH = 8
SEQ = 2048
DEPTH = 2
DEC_BATCH = 128
DEC_SEQ = 8
PAST_LEN = 16384

A_HEADS = 16
A_KV_HEADS = 2
A_HEAD_DIM = 64
A_GROUP = A_HEADS // A_KV_HEADS
WINDOW = 128
ROT_DIM = A_HEAD_DIM // 4
ROPE_THETA = 500000.0
NEG_INF = -1e30
B_HEADS = 4
B_DK = D_MODEL // 2 // B_HEADS
B_DV = D_MODEL // B_HEADS
B_GATE_RANK = 16
B_TAU = 16.0
GLA_CHUNK = 64
POOL_WINDOWS = (2, 4, 8, 16)
C_GROUPS = len(POOL_WINDOWS)
C_GROUP_W = D_MODEL // C_GROUPS
C_WIDTH = C_GROUPS * C_GROUP_W
POOL_STATE = max(POOL_WINDOWS) - 1
PEER_HEADS = 8
N_KEYS = 128
N_EXPERTS = N_KEYS * N_KEYS
PEER_TOPK = 16
PEER_DKEY = 256
PEER_DHALF = PEER_DKEY // 2
DN_ALPHA = (2 * DEPTH) ** 0.25
LN_EPS = 1e-5
RMS_EPS = 1e-6

A_Q = A_HEADS * A_HEAD_DIM
A_KV = A_KV_HEADS * A_HEAD_DIM
B_QK = B_HEADS * B_DK
B_V = B_HEADS * B_DV
SPLITS = (A_Q, A_KV, A_KV, B_QK, B_QK, B_V, B_GATE_RANK, B_V, C_WIDTH, 3 * D_MODEL)

LANES = 128
N_PROMPT = BATCH * SEQ
N_SAMPLE = DEC_BATCH * DEC_SEQ
N_TOK = N_PROMPT + N_SAMPLE

R_GATES = 0
R_VB = 3 * D_MODEL
R_GB = R_VB + B_V
R_UC = R_GB + B_V
R_QB = R_UC + C_WIDTH
R_KB = R_QB + B_QK
R_VA = R_KB + B_QK
R_LR = R_VA + A_KV
R_WIDTH = R_LR + LANES
QK_WIDTH = A_Q + A_KV

VMEM_LIMIT = 48 * 1024 * 1024

TM_QK = 512
TM_REST = 512
TN_REST = R_WIDTH // 2
TM_MERGE = 256
TP_POOL = 512
TT_TOPK = 256
TT_PEER = 1024
IB_PEER = 8


def _params(sem):
    return pltpu.CompilerParams(dimension_semantics=sem, vmem_limit_bytes=VMEM_LIMIT)


def _qk_kernel(x_ref, w_ref, b_ref, c_ref, s1_ref, s2_ref, o_ref):
    y = jnp.dot(x_ref[...], w_ref[...], preferred_element_type=F32) + b_ref[...]
    c = c_ref[...]
    s1 = s1_ref[...]
    s2 = s2_ref[...]
    for j in range(QK_WIDTH // LANES):
        yj = y[:, j * LANES:(j + 1) * LANES]
        up = pltpu.roll(yj, LANES - ROT_DIM // 2, axis=1)
        dn = pltpu.roll(yj, ROT_DIM // 2, axis=1)
        o_ref[:, j * LANES:(j + 1) * LANES] = yj * c + up * s1 + dn * s2


def _proj_qk(xb, w, b, rope_c, rope_s1, rope_s2):
    n_prompt_blocks = SEQ // TM_QK

    def tab_map(i):
        return (jnp.where(i < N_PROMPT // TM_QK, i % n_prompt_blocks, n_prompt_blocks), 0)

    tab_spec = pl.BlockSpec((TM_QK, LANES), tab_map)
    return pl.pallas_call(
        _qk_kernel,
        grid=(N_TOK // TM_QK,),
        in_specs=[
            pl.BlockSpec((TM_QK, D_MODEL), lambda i: (i, 0)),
            pl.BlockSpec((D_MODEL, QK_WIDTH), lambda i: (0, 0)),
            pl.BlockSpec((1, QK_WIDTH), lambda i: (0, 0)),
            tab_spec, tab_spec, tab_spec,
        ],
        out_specs=pl.BlockSpec((TM_QK, QK_WIDTH), lambda i: (i, 0)),
        out_shape=jax.ShapeDtypeStruct((N_TOK, QK_WIDTH), F32),
        compiler_params=_params(("parallel",)),
        name="proj_qk",
    )(xb, w, b, rope_c, rope_s1, rope_s2)


def _mm_bias_kernel(x_ref, w_ref, b_ref, o_ref):
    o_ref[...] = jnp.dot(x_ref[...], w_ref[...], preferred_element_type=F32) + b_ref[...]


def _proj_rest(xb, w, b):
    return pl.pallas_call(
        _mm_bias_kernel,
        grid=(R_WIDTH // TN_REST, N_TOK // TM_REST),
        in_specs=[
            pl.BlockSpec((TM_REST, D_MODEL), lambda j, i: (i, 0)),
            pl.BlockSpec((D_MODEL, TN_REST), lambda j, i: (0, j)),
            pl.BlockSpec((1, TN_REST), lambda j, i: (0, j)),
        ],
        out_specs=pl.BlockSpec((TM_REST, TN_REST), lambda j, i: (i, j)),
        out_shape=jax.ShapeDtypeStruct((N_TOK, R_WIDTH), F32),
        compiler_params=_params(("parallel", "arbitrary")),
        name="proj_rest",
    )(xb, w, b)


def _attend(q, kk, vv, sink_ref, c_min, o_ref, row0):
    tq = q.shape[0]
    r = lax.broadcasted_iota(jnp.int32, (tq, 2 * WINDOW), 0)
    c = lax.broadcasted_iota(jnp.int32, (tq, 2 * WINDOW), 1)
    ok = (c > r) & (c <= r + WINDOW) & (c >= c_min)
    qb = (q * (A_HEAD_DIM ** -0.5)).astype(BF16)
    for h in range(A_HEADS):
        g = h // A_GROUP
        qh = qb[:, h * A_HEAD_DIM:(h + 1) * A_HEAD_DIM]
        kg = kk[:, g * A_HEAD_DIM:(g + 1) * A_HEAD_DIM]
        vg = vv[:, g * A_HEAD_DIM:(g + 1) * A_HEAD_DIM]
        s = lax.dot_general(qh, kg, (((1,), (1,)), ((), ())), preferred_element_type=F32)
        s = jnp.where(ok, s, NEG_INF)
        sink = sink_ref[h]
        m = jnp.maximum(jnp.max(s, axis=1, keepdims=True), sink)
        p = jnp.exp(s - m)
        denom = jnp.sum(p, axis=1, keepdims=True) + jnp.exp(sink - m)
        o = jnp.dot(p.astype(BF16), vg, preferred_element_type=F32) / denom
        o_ref[pl.ds(row0, tq), h * A_HEAD_DIM:(h + 1) * A_HEAD_DIM] = o.astype(o_ref.dtype)


def _attn_prompt_kernel(sink_ref, q_ref, kc_ref, kp_ref, vc_ref, vp_ref, o_ref):
    n = pl.program_id(1)
    kk = jnp.concatenate([kp_ref[...], kc_ref[...]], axis=0).astype(BF16)
    vv = jnp.concatenate([vp_ref[...], vc_ref[...]], axis=0).astype(BF16)
    _attend(q_ref[...], kk, vv, sink_ref, jnp.where(n > 0, 0, WINDOW), o_ref, 0)


def _attn_prompt(qk, rest, sinks):
    nb = SEQ // WINDOW
    kcol = A_Q // A_KV
    vcol = R_VA // A_KV

    def cur(b, n):
        return b * nb + n

    def prev(b, n):
        return b * nb + jnp.maximum(n - 1, 0)

    return pl.pallas_call(
        _attn_prompt_kernel,
        grid=(BATCH, nb),
        in_specs=[
            pl.BlockSpec(memory_space=pltpu.SMEM),
            pl.BlockSpec((WINDOW, A_Q), lambda b, n: (cur(b, n), 0)),
            pl.BlockSpec((WINDOW, A_KV), lambda b, n: (cur(b, n), kcol)),
            pl.BlockSpec((WINDOW, A_KV), lambda b, n: (prev(b, n), kcol)),
            pl.BlockSpec((WINDOW, A_KV), lambda b, n: (cur(b, n), vcol)),
            pl.BlockSpec((WINDOW, A_KV), lambda b, n: (prev(b, n), vcol)),
        ],
        out_specs=pl.BlockSpec((WINDOW, A_Q), lambda b, n: (cur(b, n), 0)),
        out_shape=jax.ShapeDtypeStruct((N_TOK, A_Q), BF16),
        compiler_params=_params(("parallel", "arbitrary")),
        name="attn_prompt",
    )(sinks, qk, qk, qk, rest, rest)


BB_ATTN = 8


def _attn_sample_kernel(sink_ref, q_ref, kn_ref, vn_ref, ks_ref, vs_ref, prompt_out_ref, o_ref):
    del prompt_out_ref
    pad = jnp.zeros((WINDOW - DEC_SEQ, A_KV), F32)
    rows = A_GROUP * DEC_SEQ
    t = lax.broadcasted_iota(jnp.int32, (rows, 2 * WINDOW), 0) % DEC_SEQ
    c = lax.broadcasted_iota(jnp.int32, (rows, 2 * WINDOW), 1)
    ok = (c > t) & (c <= t + WINDOW)
    nt = (((1,), (1,)), ((), ()))

    def body(bb, carry):
        row0 = pl.multiple_of(bb * DEC_SEQ, DEC_SEQ)
        q = q_ref[pl.ds(row0, DEC_SEQ), :] * (A_HEAD_DIM ** -0.5)
        kk = jnp.concatenate([ks_ref[bb], kn_ref[pl.ds(row0, DEC_SEQ), :], pad], axis=0).astype(BF16)
        vv = jnp.concatenate([vs_ref[bb], vn_ref[pl.ds(row0, DEC_SEQ), :], pad], axis=0).astype(BF16)
        for g in range(A_KV_HEADS):
            heads = range(g * A_GROUP, (g + 1) * A_GROUP)
            qg = jnp.concatenate([q[:, h * A_HEAD_DIM:(h + 1) * A_HEAD_DIM] for h in heads], axis=0)
            ds = slice(g * A_HEAD_DIM, (g + 1) * A_HEAD_DIM)
            s = lax.dot_general(qg.astype(BF16), kk[:, ds], nt, preferred_element_type=F32)
            s = jnp.where(ok, s, NEG_INF)
            sink = sink_ref[g * rows:(g + 1) * rows, 0:1]
            m = jnp.maximum(jnp.max(s, axis=1, keepdims=True), sink)
            p = jnp.exp(s - m)
            denom = jnp.sum(p, axis=1, keepdims=True) + jnp.exp(sink - m)
            o = jnp.dot(p.astype(BF16), vv[:, ds], preferred_element_type=F32) / denom
            for k, h in enumerate(heads):
                o_ref[pl.ds(row0, DEC_SEQ), h * A_HEAD_DIM:(h + 1) * A_HEAD_DIM] = (
                    o[k * DEC_SEQ:(k + 1) * DEC_SEQ, :].astype(o_ref.dtype))
        return carry

    lax.fori_loop(0, BB_ATTN, body, 0)


def _attn_sample(qk, rest, sinks, k_state, v_state, prompt_out):
    rows = BB_ATTN * DEC_SEQ
    base = N_PROMPT // rows
    kcol = A_Q // A_KV
    vcol = R_VA // A_KV
    sink_rows = jnp.broadcast_to(jnp.repeat(sinks, DEC_SEQ)[:, None], (A_HEADS * DEC_SEQ, LANES))
    return pl.pallas_call(
        _attn_sample_kernel,
        grid=(DEC_BATCH // BB_ATTN,),
        in_specs=[
            pl.BlockSpec((A_HEADS * DEC_SEQ, LANES), lambda i: (0, 0)),
            pl.BlockSpec((rows, A_Q), lambda i: (base + i, 0)),
            pl.BlockSpec((rows, A_KV), lambda i: (base + i, kcol)),
            pl.BlockSpec((rows, A_KV), lambda i: (base + i, vcol)),
            pl.BlockSpec((BB_ATTN, WINDOW, A_KV), lambda i: (i, 0, 0)),
            pl.BlockSpec((BB_ATTN, WINDOW, A_KV), lambda i: (i, 0, 0)),
            pl.BlockSpec(memory_space=pl.ANY),
        ],
        out_specs=pl.BlockSpec((rows, A_Q), lambda i: (base + i, 0)),
        out_shape=jax.ShapeDtypeStruct((N_TOK, A_Q), BF16),
        input_output_aliases={6: 0},
        compiler_params=_params(("parallel",)),
        name="attn_sample",
    )(sink_rows, qk, qk, rest, k_state, v_state, prompt_out)


def _split3(x):
    hi = x.astype(BF16)
    r1 = x - hi.astype(F32)
    mid = r1.astype(BF16)
    lo = (r1 - mid.astype(F32)).astype(BF16)
    return hi, mid, lo


def _gla_kernel(lr_ref, q_ref, k_ref, v_ref, gb_ref, s0_ref, wa_ref, ba_ref, g_ref, *refs, n_chunks):
    o_ref, sout_ref, st_ref = refs[-3:]
    ci = pl.program_id(1)
    c = q_ref.shape[0]

    @pl.when(ci == 0)
    def _():
        for h in range(B_HEADS):
            st_ref[h] = s0_ref[0, 0, h].T

    z = jnp.dot(lr_ref[...].astype(BF16), wa_ref[...], preferred_element_type=F32) + ba_ref[...]
    log_a = -(jnp.maximum(-z, 0.0) + jnp.log1p(jnp.exp(-jnp.abs(z)))) / B_TAU
    ri = lax.broadcasted_iota(jnp.int32, (c, c), 0)
    cj = lax.broadcasted_iota(jnp.int32, (c, c), 1)
    causal = cj <= ri
    tri = jnp.where(causal, 1.0, 0.0).astype(BF16)
    hi, mid, lo = _split3(log_a)
    b = (jnp.dot(tri, hi, preferred_element_type=F32)
         + jnp.dot(tri, mid, preferred_element_type=F32)
         + jnp.dot(tri, lo, preferred_element_type=F32))
    bl = b[c - 1:c, :]
    q = q_ref[...] * (B_DK ** -0.5)
    k = k_ref[...]
    qd = (q * jnp.exp(b)).astype(BF16)
    kd = (k * jnp.exp(-b)).astype(BF16)
    kl = (k * jnp.exp(bl - b)).astype(BF16)
    ebl = jnp.exp(bl)
    nt = (((1,), (1,)), ((), ()))
    for h in range(B_HEADS):
        ks = slice(h * B_DK, (h + 1) * B_DK)
        vs = slice(h * B_DV, (h + 1) * B_DV)
        vh = v_ref[:, vs]
        st = st_ref[h]
        o = lax.dot_general(qd[:, ks], st.astype(BF16), nt, preferred_element_type=F32)
        att = lax.dot_general(qd[:, ks], kd[:, ks], nt, preferred_element_type=F32)
        att = jnp.where(causal, att, 0.0)
        o = o + jnp.dot(att.astype(BF16), vh.astype(BF16), preferred_element_type=F32)
        st_ref[h] = st * ebl[:, ks] + jnp.dot(vh.T.astype(BF16), kl[:, ks], preferred_element_type=F32)
        o = o * lax.rsqrt(jnp.mean(o * o, axis=1, keepdims=True) + RMS_EPS) * g_ref[h:h + 1, :]
        gate = gb_ref[:, vs]
        o_ref[:, vs] = (o * (gate / (1.0 + jnp.exp(-gate)))).astype(o_ref.dtype)

    @pl.when(ci == n_chunks - 1)
    def _():
        for h in range(B_HEADS):
            sout_ref[0, 0, h] = st_ref[h].T


def _gla(rest, s0, layer, wa, ba, g, *, n_batch, seq, chunk, row_base, fill_into=None, states_into=None):
    n_chunks = seq // chunk
    base = row_base // chunk

    def row(b, ci):
        return base + b * n_chunks + ci

    state_spec = pl.BlockSpec((1, 1, B_HEADS, B_DK, B_DV), lambda b, ci: (layer, b, 0, 0, 0))
    s0_layer = layer if s0.shape[0] > 1 else 0
    s0_spec = pl.BlockSpec((1, 1, B_HEADS, B_DK, B_DV), lambda b, ci: (s0_layer, b, 0, 0, 0))
    extra_args, aliases = [], {}
    for arg, out_idx in ((fill_into, 0), (states_into, 1)):
        if arg is not None:
            aliases[9 + len(extra_args)] = out_idx
            extra_args.append(arg)
    return pl.pallas_call(
        functools.partial(_gla_kernel, n_chunks=n_chunks),
        grid=(n_batch, n_chunks),
        input_output_aliases=aliases,
        in_specs=[
            pl.BlockSpec((chunk, LANES), lambda b, ci: (row(b, ci), R_LR // LANES)),
            pl.BlockSpec((chunk, B_QK), lambda b, ci: (row(b, ci), R_QB // B_QK)),
            pl.BlockSpec((chunk, B_QK), lambda b, ci: (row(b, ci), R_KB // B_QK)),
            pl.BlockSpec((chunk, B_V), lambda b, ci: (row(b, ci), R_VB // B_V)),
            pl.BlockSpec((chunk, B_V), lambda b, ci: (row(b, ci), R_GB // B_V)),
            s0_spec,
            pl.BlockSpec((LANES, B_QK), lambda b, ci: (0, 0)),
            pl.BlockSpec((1, B_QK), lambda b, ci: (0, 0)),
            pl.BlockSpec((B_HEADS, B_DV), lambda b, ci: (0, 0)),
        ] + [pl.BlockSpec(memory_space=pl.ANY)] * len(extra_args),
        out_specs=[
            pl.BlockSpec((chunk, B_V), lambda b, ci: (row(b, ci), 0)),
            state_spec,
        ],
        out_shape=[
            jax.ShapeDtypeStruct((N_TOK, B_V), BF16),
            jax.ShapeDtypeStruct((DEPTH, n_batch, B_HEADS, B_DK, B_DV), F32),
        ],
        scratch_shapes=[pltpu.VMEM((B_HEADS, B_DV, B_DK), F32)],
        compiler_params=_params(("parallel", "arbitrary")),
        name="gla",
    )(rest, rest, rest, rest, rest, s0, wa, ba, g, *extra_args)


HALO = 16


def _pool_kernel(u_ref, prev_ref, w_ref, scale_ref, *refs, from_start):
    o_ref = refs[-1]
    tp = u_ref.shape[0]
    u = u_ref[...]
    prev = prev_ref[...]
    if from_start:
        ti = pl.program_id(1)
        prev = jnp.where(ti > 0, prev, 0.0)
        t0 = ti * tp
    full = jnp.concatenate([prev, u], axis=0)
    for g, w in enumerate(POOL_WINDOWS):
        cs = slice(g * C_GROUP_W, (g + 1) * C_GROUP_W)
        acc = full[:, cs]
        span = 1
        while span < w:
            acc = acc + pltpu.roll(acc, span, axis=0)
            span *= 2
        wsum = acc[HALO:, :]
        if from_start:
            t = t0 + lax.broadcasted_iota(jnp.int32, (tp, C_GROUP_W), 0)
            cnt = jnp.minimum(t + 1, w).astype(F32)
        else:
            cnt = float(w)
        d = wsum / cnt - u[:, cs]
        y = jnp.dot(d.astype(BF16), w_ref[g], preferred_element_type=F32) * scale_ref[:, cs]
        o_ref[:, cs] = y.astype(o_ref.dtype)


def _pool_prompt(rest, w, scale):
    nt = SEQ // TP_POOL
    ucol = R_UC // C_WIDTH

    def halo(b, i):
        return (jnp.maximum((b * SEQ + i * TP_POOL) // HALO - 1, 0), ucol)

    return pl.pallas_call(
        functools.partial(_pool_kernel, from_start=True),
        grid=(BATCH, nt),
        in_specs=[
            pl.BlockSpec((TP_POOL, C_WIDTH), lambda b, i: (b * nt + i, ucol)),
            pl.BlockSpec((HALO, C_WIDTH), halo),
            pl.BlockSpec((C_GROUPS, C_GROUP_W, C_GROUP_W), lambda b, i: (0, 0, 0)),
            pl.BlockSpec((1, C_WIDTH), lambda b, i: (0, 0)),
        ],
        out_specs=pl.BlockSpec((TP_POOL, C_WIDTH), lambda b, i: (b * nt + i, 0)),
        out_shape=jax.ShapeDtypeStruct((N_TOK, C_WIDTH), BF16),
        compiler_params=_params(("parallel", "arbitrary")),
        name="pool_prompt",
    )(rest, rest, w, scale)


def _pool_sample(rest, prev, w, scale, prompt_out):
    ucol = R_UC // C_WIDTH
    base = N_PROMPT // DEC_SEQ
    return pl.pallas_call(
        functools.partial(_pool_kernel, from_start=False),
        grid=(DEC_BATCH,),
        in_specs=[
            pl.BlockSpec((DEC_SEQ, C_WIDTH), lambda b: (base + b, ucol)),
            pl.BlockSpec((HALO, C_WIDTH), lambda b: (b, 0)),
            pl.BlockSpec((C_GROUPS, C_GROUP_W, C_GROUP_W), lambda b: (0, 0, 0)),
            pl.BlockSpec((1, C_WIDTH), lambda b: (0, 0)),
            pl.BlockSpec(memory_space=pl.ANY),
        ],
        out_specs=pl.BlockSpec((DEC_SEQ, C_WIDTH), lambda b: (base + b, 0)),
        out_shape=jax.ShapeDtypeStruct((N_TOK, C_WIDTH), BF16),
        input_output_aliases={4: 0},
        compiler_params=_params(("parallel",)),
        name="pool_sample",
    )(rest, prev, w, scale, prompt_out)


def _layer_norm(x, g, b):
    mu = jnp.mean(x, axis=1, keepdims=True)
    xc = x - mu
    var = jnp.mean(xc * xc, axis=1, keepdims=True)
    return xc * lax.rsqrt(var + LN_EPS) * g + b


def _merge_kernel(x_ref, gates_ref, oa_ref, ob_ref, oc_ref, wa_ref, wb_ref, wc_ref, wo_ref,
                  g_ref, b_ref, o_ref, oT_ref):
    def gate(i):
        z = gates_ref[:, i * D_MODEL:(i + 1) * D_MODEL]
        return 1.0 / (1.0 + jnp.exp(-z))

    merged = (gate(0) * jnp.dot(oa_ref[...], wa_ref[...], preferred_element_type=F32)
              + gate(1) * jnp.dot(ob_ref[...], wb_ref[...], preferred_element_type=F32)
              + gate(2) * jnp.dot(oc_ref[...], wc_ref[...], preferred_element_type=F32))
    mix = jnp.dot(merged.astype(BF16), wo_ref[...], preferred_element_type=F32)
    y = _layer_norm(DN_ALPHA * x_ref[...] + mix, g_ref[...], b_ref[...])
    o_ref[...] = y
    oT_ref[...] = y.T.astype(BF16)


def _merge(x, rest, oa, ob, oc, wa, wb, wc, wo, g, b):
    row = pl.BlockSpec((TM_MERGE, D_MODEL), lambda i: (i, 0))
    wspec = pl.BlockSpec((D_MODEL, D_MODEL), lambda i: (0, 0))
    vec = pl.BlockSpec((1, D_MODEL), lambda i: (0, 0))
    return pl.pallas_call(
        _merge_kernel,
        grid=(N_TOK // TM_MERGE,),
        in_specs=[row, pl.BlockSpec((TM_MERGE, 3 * D_MODEL), lambda i: (i, R_GATES)),
                  row, row, row, wspec, wspec, wspec, wspec, vec, vec],
        out_specs=[row, pl.BlockSpec((D_MODEL, TM_MERGE), lambda i: (0, i))],
        out_shape=[jax.ShapeDtypeStruct((N_TOK, D_MODEL), F32),
                   jax.ShapeDtypeStruct((D_MODEL, N_TOK), BF16)],
        compiler_params=_params(("parallel",)),
        name="merge",
    )(x, rest, oa, ob, oc, wa, wb, wc, wo, g, b)


SUBLANES = 8


def _extract_desc(s, n):
    vals = []
    for _ in range(n):
        m = jnp.max(s, axis=0, keepdims=True)
        vals.append(m)
        s = jnp.where(s == m, -jnp.inf, s)
    return vals


def _merge_sort_pairs(n):
    pairs = []
    p = 1
    while p < n:
        k = p
        while k >= 1:
            for j in range(k % p, n - k, 2 * k):
                for i in range(min(k, n - j - k)):
                    if (i + j) // (2 * p) == (i + j + k) // (2 * p):
                        pairs.append((i + j, i + j + k))
            k //= 2
        p *= 2
    return pairs


def _top_desc(s, n):
    v = [s[SUBLANES * k:SUBLANES * (k + 1), :] for k in range(s.shape[0] // SUBLANES)]
    for a, b in _merge_sort_pairs(len(v)):
        v[a], v[b] = jnp.maximum(v[a], v[b]), jnp.minimum(v[a], v[b])
    vals = []
    for r in range(n):
        m = jnp.max(v[0], axis=0, keepdims=True)
        vals.append(m)
        hit = v[0] == m
        for k in range(n - r - 1):
            v[k] = jnp.where(hit, v[k + 1], v[k])
    return vals


def _rank_of(s, vals):
    rank = jnp.full(s.shape, float(len(vals)), F32)
    for r, val in enumerate(vals):
        rank = jnp.where(s == val, float(r), rank)
    return rank


def _peer_topk_kernel(xT_ref, wq_ref, sk_ref, cnt_ref, e1_ref, rank_ref, e2_ref, s1_scr, sv_scr):
    tt = xT_ref.shape[1]
    xT = xT_ref[...]
    for hp in range(2 * PEER_HEADS):
        h, second = divmod(hp, 2)
        qT = jnp.dot(wq_ref[hp * PEER_DHALF:(hp + 1) * PEER_DHALF, :], xT, preferred_element_type=F32)
        s = jnp.dot(sk_ref[hp], qT.astype(BF16), preferred_element_type=F32)
        vals = _top_desc(s, PEER_TOPK)
        sv_scr[hp] = jnp.concatenate(vals, axis=0)
        if second:
            rank_ref[h] = _rank_of(s, vals).astype(BF16)
            e2_ref[h] = jnp.exp(s - vals[0]).astype(BF16)
        else:
            s1_scr[h] = s
    row8 = lax.broadcasted_iota(jnp.int32, (8, tt), 0)
    for h in range(PEER_HEADS):
        sv1 = sv_scr[2 * h]
        sv2 = sv_scr[2 * h + 1]
        pieces = [sv1[0:1, :] + sv2]
        for a in range(1, 8):
            nb = PEER_TOPK // (a + 1)
            pieces.append(jnp.where(row8 < nb, sv1[a:a + 1, :] + sv2[0:8, :], -jnp.inf))
        pieces.append(sv2[0:1, :] + sv1[8:16, :])
        cand = _extract_desc(jnp.concatenate(pieces, axis=0), PEER_TOPK)
        top = cand[0]
        z = jnp.zeros_like(top)
        for r in range(PEER_TOPK):
            z = z + jnp.exp(cand[r] - top)
        kth = cand[PEER_TOPK - 1]
        counts = [jnp.sum(jnp.where(p >= kth, 1.0, 0.0), axis=0, keepdims=True) for p in pieces[:8]]
        tail = jnp.where(pieces[8] >= kth, 1.0, 0.0)
        counts += [tail[a:a + 1, :] for a in range(8)]
        s1 = s1_scr[h]
        cnt = jnp.zeros_like(s1)
        for a in range(PEER_TOPK):
            cnt = jnp.where(s1 == sv1[a:a + 1, :], counts[a], cnt)
        e1 = jnp.exp(s1 - (sv1[0:1, :] + jnp.log(z))) * SQRT_HALF
        for c in range(tt // LANES):
            cnt_ref[h, c] = cnt[:, c * LANES:(c + 1) * LANES]
            e1_ref[h, c] = e1[:, c * LANES:(c + 1) * LANES]


def _peer_topk(xT, wqT, sk):
    spec = pl.BlockSpec((PEER_HEADS, N_KEYS, TT_TOPK), lambda t: (0, 0, t))
    row_spec = pl.BlockSpec((PEER_HEADS, TT_TOPK // LANES, N_KEYS, LANES), lambda t: (0, t, 0, 0))
    wide = jax.ShapeDtypeStruct((PEER_HEADS, N_TOK // LANES, N_KEYS, LANES), F32)
    narrow = jax.ShapeDtypeStruct((PEER_HEADS, N_KEYS, N_TOK), BF16)
    return pl.pallas_call(
        _peer_topk_kernel,
        grid=(N_TOK // TT_TOPK,),
        in_specs=[
            pl.BlockSpec((D_MODEL, TT_TOPK), lambda t: (0, t)),
            pl.BlockSpec((PEER_HEADS * PEER_DKEY, D_MODEL), lambda t: (0, 0)),
            pl.BlockSpec((2 * PEER_HEADS, N_KEYS, PEER_DHALF), lambda t: (0, 0, 0)),
        ],
        out_specs=[row_spec, row_spec, spec, spec],
        out_shape=[wide, wide, narrow, narrow],
        scratch_shapes=[pltpu.VMEM((PEER_HEADS, N_KEYS, TT_TOPK), F32),
                        pltpu.VMEM((2 * PEER_HEADS, PEER_TOPK, TT_TOPK), F32)],
        compiler_params=_params(("parallel",)),
        name="peer_topk",
    )(xT, wqT, sk)


SQRT_HALF = 2.0 ** -0.5


def _gelu_unscaled(x):
    t = x * SQRT_HALF
    return t * (1.0 + lax.erf(t))


LC_PEER = 256


def _bf16_rows(ref, h, r, chunks, n_rows):
    x = jnp.concatenate([ref[h, c, pl.ds(r, 16, stride=0), :] for c in chunks], axis=1)
    packed = x.astype(BF16)
    return jnp.concatenate([packed] * (n_rows // packed.shape[0]), axis=0)


MM_PIECE = 256


def _peer_main_kernel(xT_ref, u_ref, vt_ref, cnt_ref, e1_ref, rank_ref, e2_ref, yT_ref, s_scr, wh_scr):
    i = pl.program_id(1)
    tt = xT_ref.shape[1]

    @pl.when(i == 0)
    def _():
        yT_ref[...] = jnp.zeros_like(yT_ref)

    zero = jnp.zeros((N_KEYS, LC_PEER), BF16)
    per_piece = MM_PIECE // N_KEYS
    for ii in range(IB_PEER):
        rows = slice(ii * N_KEYS, (ii + 1) * N_KEYS)
        if ii % per_piece == 0:
            piece = slice(ii * N_KEYS, ii * N_KEYS + MM_PIECE)
            s_scr[piece, :] = jnp.dot(u_ref[0, piece, :], xT_ref[...], preferred_element_type=F32)
        for lc in range(tt // LC_PEER):
            cols = slice(lc * LC_PEER, (lc + 1) * LC_PEER)
            chunks = range(lc * LC_PEER // LANES, (lc + 1) * LC_PEER // LANES)
            w = zero
            for h in range(PEER_HEADS):
                cnt = _bf16_rows(cnt_ref, h, ii, chunks, N_KEYS)
                e1 = _bf16_rows(e1_ref, h, ii, chunks, N_KEYS)
                w = w + jnp.where(rank_ref[h, :, cols] < cnt, e2_ref[h, :, cols] * e1, zero)
            wh_scr[rows, cols] = w
    for ii in range(IB_PEER):
        rows = slice(ii * N_KEYS, (ii + 1) * N_KEYS)
        wh_scr[rows, :] = wh_scr[rows, :] * _gelu_unscaled(s_scr[rows, :]).astype(BF16)
    yT_ref[...] += jnp.dot(vt_ref[0], wh_scr[...], preferred_element_type=F32)


def _peer_main(xT, u, vt, layer, cnt, e1, rank, e2):
    eb = IB_PEER * N_KEYS
    row_spec = pl.BlockSpec((PEER_HEADS, TT_PEER // LANES, IB_PEER, LANES), lambda t, i: (0, t, i, 0))
    tab_spec = pl.BlockSpec((PEER_HEADS, N_KEYS, TT_PEER), lambda t, i: (0, 0, t))
    return pl.pallas_call(
        _peer_main_kernel,
        grid=(N_TOK // TT_PEER, N_EXPERTS // eb),
        in_specs=[
            pl.BlockSpec((D_MODEL, TT_PEER), lambda t, i: (0, t)),
            pl.BlockSpec((1, eb, D_MODEL), lambda t, i: (layer, i, 0)),
            pl.BlockSpec((1, D_MODEL, eb), lambda t, i: (layer, 0, i)),
            row_spec, row_spec, tab_spec, tab_spec,
        ],
        out_specs=pl.BlockSpec((D_MODEL, TT_PEER), lambda t, i: (0, t)),
        out_shape=jax.ShapeDtypeStruct((D_MODEL, N_TOK), F32),
        scratch_shapes=[pltpu.VMEM((eb, TT_PEER), F32), pltpu.VMEM((eb, TT_PEER), BF16)],
        compiler_params=_params(("parallel", "arbitrary")),
        name="peer_main",
    )(xT, u, vt, cnt, e1, rank, e2)


def _peer_out_kernel(x_ref, yT_ref, g_ref, b_ref, o_ref, ob_ref):
    y = _layer_norm(DN_ALPHA * x_ref[...] + yT_ref[...].T, g_ref[...], b_ref[...])
    o_ref[...] = y
    ob_ref[...] = y.astype(BF16)


def _peer_out(x1, yT, g, b):
    row = pl.BlockSpec((TM_MERGE, D_MODEL), lambda i: (i, 0))
    vec = pl.BlockSpec((1, D_MODEL), lambda i: (0, 0))
    return pl.pallas_call(
        _peer_out_kernel,
        grid=(N_TOK // TM_MERGE,),
        in_specs=[row, pl.BlockSpec((D_MODEL, TM_MERGE), lambda i: (0, i)), vec, vec],
        out_specs=[row, row],
        out_shape=[jax.ShapeDtypeStruct((N_TOK, D_MODEL), F32),
                   jax.ShapeDtypeStruct((N_TOK, D_MODEL), BF16)],
        compiler_params=_params(("parallel",)),
        name="peer_out",
    )(x1, yT, g, b)


def _rope_tables():
    half = ROT_DIM // 2
    pos = jnp.concatenate([jnp.arange(SEQ), PAST_LEN + (jnp.arange(TM_QK) % DEC_SEQ)])
    inv = ROPE_THETA ** (-jnp.arange(half, dtype=F32) / half)
    ang = pos.astype(F32)[:, None] * inv[None, :]
    cos, sin = jnp.cos(ang), jnp.sin(ang)
    n = pos.shape[0]
    one = jnp.ones((n, A_HEAD_DIM - ROT_DIM), F32)
    zero = jnp.zeros((n, A_HEAD_DIM - ROT_DIM), F32)
    zh = jnp.zeros((n, half), F32)
    reps = LANES // A_HEAD_DIM
    c = jnp.tile(jnp.concatenate([cos, cos, one], 1), (1, reps))
    s1 = jnp.tile(jnp.concatenate([-sin, zh, zero], 1), (1, reps))
    s2 = jnp.tile(jnp.concatenate([zh, sin, zero], 1), (1, reps))
    return c, s1, s2


def _split_cols(w):
    cuts = [int(c) for c in np.cumsum(SPLITS)[:-1]]
    return jnp.split(w, cuts, axis=-1)


def _layer(layer, x, xb, k_state, v_state, gla_states, gla_out, pool_state, rope, w_in, b_in, sinks,
           w_alpha, b_alpha, gla_g, w_pool, pool_scale, w_a, w_b, w_c, w_out, ln1_g, ln1_b,
           peer_query, peer_subkeys, peer_u, peer_vt, ln2_g, ln2_b):
    qa_w, ka_w, va_w, qb_w, kb_w, vb_w, lr_w, gb_w, uc_w, gates_w = _split_cols(w_in)
    qa_b, ka_b, va_b, qb_b, kb_b, vb_b, lr_b, gb_b, uc_b, gates_b = _split_cols(b_in[None, :])
    lr_pad = LANES - B_GATE_RANK
    w_qk = jnp.concatenate([qa_w, ka_w], 1).astype(BF16)
    b_qk = jnp.concatenate([qa_b, ka_b], 1)
    w_rest = jnp.concatenate([gates_w, vb_w, gb_w, uc_w, qb_w, kb_w, va_w,
                              jnp.pad(lr_w, ((0, 0), (0, lr_pad)))], 1).astype(BF16)
    b_rest = jnp.concatenate([gates_b, vb_b, gb_b, uc_b, qb_b, kb_b, va_b,
                              jnp.pad(lr_b, ((0, 0), (0, lr_pad)))], 1)

    qk = _proj_qk(xb, w_qk, b_qk, *rope)
    rest = _proj_rest(xb, w_rest, b_rest)

    ks = k_state.reshape(DEC_BATCH, WINDOW, A_KV)
    vs = v_state.reshape(DEC_BATCH, WINDOW, A_KV)
    oa = _attn_sample(qk, rest, sinks, ks, vs, _attn_prompt(qk, rest, sinks))

    wa = jnp.pad(w_alpha, ((0, lr_pad), (0, 0))).astype(BF16)
    ba = b_alpha[None, :]
    gla_p, gla_s = gla_out
    ob, gla_p = _gla(rest, jnp.zeros((1, BATCH, B_HEADS, B_DK, B_DV), F32), layer, wa, ba, gla_g,
                     n_batch=BATCH, seq=SEQ, chunk=GLA_CHUNK, row_base=0, states_into=gla_p)
    ob, gla_s = _gla(rest, gla_states, layer, wa, ba, gla_g, n_batch=DEC_BATCH, seq=DEC_SEQ,
                     chunk=math.gcd(DEC_SEQ, GLA_CHUNK), row_base=N_PROMPT, fill_into=ob, states_into=gla_s)

    wp = w_pool.astype(BF16)
    ps = pool_scale[None, :]
    prev = jnp.pad(pool_state, ((0, 0), (HALO - POOL_STATE, 0), (0, 0))).reshape(DEC_BATCH * HALO, C_WIDTH)
    oc = _pool_sample(rest, prev, wp, ps, _pool_prompt(rest, wp, ps))

    x1, x1T = _merge(x, rest, oa, ob, oc, w_a.astype(BF16), w_b.astype(BF16), w_c.astype(BF16),
                     w_out.astype(BF16), ln1_g[None, :], ln1_b[None, :])

    wqT = peer_query.reshape(D_MODEL, PEER_HEADS * PEER_DKEY).T.astype(BF16)
    sk = peer_subkeys.reshape(2 * PEER_HEADS, N_KEYS, PEER_DHALF).astype(BF16)
    cnt, e1, rank, e2 = _peer_topk(x1T, wqT, sk)
    yT = _peer_main(x1T, peer_u, peer_vt, layer, cnt, e1, rank, e2)
    x2, x2b = _peer_out(x1, yT, ln2_g[None, :], ln2_b[None, :])

    def prompt_tail(t, col0, width, n):
        return jnp.stack([t[(b + 1) * SEQ - n:(b + 1) * SEQ, col0:col0 + width] for b in range(BATCH)])

    def sample_tail(state, t, col0, width, n):
        new = t[N_PROMPT:, col0:col0 + width].reshape(DEC_BATCH, DEC_SEQ, width)
        return jnp.concatenate([state, new], 1)[:, -n:]

    kv_shape = (-1, WINDOW, A_KV_HEADS, A_HEAD_DIM)
    states = (prompt_tail(qk, A_Q, A_KV, WINDOW).reshape(kv_shape),
              prompt_tail(rest, R_VA, A_KV, WINDOW).reshape(kv_shape),
              prompt_tail(rest, R_UC, C_WIDTH, POOL_STATE),
              sample_tail(ks, qk, A_Q, A_KV, WINDOW).reshape(kv_shape),
              sample_tail(vs, rest, R_VA, A_KV, WINDOW).reshape(kv_shape),
              sample_tail(pool_state, rest, R_UC, C_WIDTH, POOL_STATE))
    return x2, x2b, states, (gla_p, gla_s)


def kernel(x_prompt, x_sample, state_win_k, state_win_v, state_gla, state_pool, w_in, b_in, attn_sinks,
           w_alpha, b_alpha, gla_norm_g, w_pool, pool_scale, w_branch_a, w_branch_b, w_branch_c, w_out,
           ln1_g, ln1_b, peer_query, peer_subkeys, peer_u, peer_v, ln2_g, ln2_b):
    x = jnp.concatenate([x_prompt.reshape(N_PROMPT, D_MODEL), x_sample.reshape(N_SAMPLE, D_MODEL)], 0)
    xb = x.astype(BF16)
    rope = _rope_tables()
    peer_ub = peer_u.astype(BF16)
    peer_vtb = jnp.swapaxes(peer_v, 1, 2).astype(BF16)
    per_layer = []
    gla_out = (None, None)
    for l in range(DEPTH):
        x, xb, states, gla_out = _layer(
            l, x, xb, state_win_k[l], state_win_v[l], state_gla, gla_out, state_pool[l], rope,
            w_in[l], b_in[l], attn_sinks[l], w_alpha[l], b_alpha[l], gla_norm_g[l],
            w_pool[l], pool_scale[l], w_branch_a[l], w_branch_b[l], w_branch_c[l], w_out[l],
            ln1_g[l], ln1_b[l], peer_query[l], peer_subkeys[l], peer_ub, peer_vtb, ln2_g[l], ln2_b[l])
        per_layer.append(states)
    pk, pv, pp, sk, sv, sp = [jnp.stack([per_layer[l][i] for l in range(DEPTH)]) for i in range(6)]
    return (x[:N_PROMPT].reshape(BATCH, SEQ, D_MODEL), x[N_PROMPT:].reshape(DEC_BATCH, DEC_SEQ, D_MODEL),
            pk, pv, gla_out[0], pp, sk, sv, gla_out[1], sp)
```

```python
import functools
import math

import jax
import jax.numpy as jnp
import numpy as np
from jax import lax
from jax.experimental import pallas as pl
from jax.experimental.pallas import tpu as pltpu

F32 = jnp.float32
BF16 = jnp.bfloat16

D_MODEL = 1024
BATCH = 8
SEQ = 2048
DEPTH = 2
DEC_BATCH = 128
DEC_SEQ = 8
PAST_LEN = 16384

A_HEADS = 16
A_KV_HEADS = 2
A_HEAD_DIM = 64
A_GROUP = A_HEADS // A_KV_HEADS
WINDOW = 128
ROT_DIM = A_HEAD_DIM // 4
ROPE_THETA = 500000.0
NEG_INF = -1e30
B_HEADS = 4
B_DK = D_MODEL // 2 // B_HEADS
B_DV = D_MODEL // B_HEADS
B_GATE_RANK = 16
B_TAU = 16.0
GLA_CHUNK = 64
POOL_WINDOWS = (2, 4, 8, 16)
C_GROUPS = len(POOL_WINDOWS)
C_GROUP_W = D_MODEL // C_GROUPS
C_WIDTH = C_GROUPS * C_GROUP_W
POOL_STATE = max(POOL_WINDOWS) - 1
PEER_HEADS = 8
N_KEYS = 128
N_EXPERTS = N_KEYS * N_KEYS
PEER_TOPK = 16
PEER_DKEY = 256
PEER_DHALF = PEER_DKEY // 2
DN_ALPHA = (2 * DEPTH) ** 0.25
LN_EPS = 1e-5
RMS_EPS = 1e-6

A_Q = A_HEADS * A_HEAD_DIM
A_KV = A_KV_HEADS * A_HEAD_DIM
B_QK = B_HEADS * B_DK
B_V = B_HEADS * B_DV
SPLITS = (A_Q, A_KV, A_KV, B_QK, B_QK, B_V, B_GATE_RANK, B_V, C_WIDTH, 3 * D_MODEL)

LANES = 128
N_PROMPT = BATCH * SEQ
N_SAMPLE = DEC_BATCH * DEC_SEQ
N_TOK = N_PROMPT + N_SAMPLE

R_GATES = 0
R_VB = 3 * D_MODEL
R_GB = R_VB + B_V
R_UC = R_GB + B_V
R_QB = R_UC + C_WIDTH
R_KB = R_QB + B_QK
R_VA = R_KB + B_QK
R_LR = R_VA + A_KV
R_WIDTH = R_LR + LANES
QK_WIDTH = A_Q + A_KV

VMEM_LIMIT = 48 * 1024 * 1024

TM_QK = 512
TM_REST = 512
TN_REST = R_WIDTH // 2
TM_MERGE = 256
TP_POOL = 512
GLA_GROUP_PROMPT = 2
GLA_GROUP_SAMPLE = 4
TT_TOPK = 256
TT_PEER = 1024
IB_PEER = 8


def _params(sem):
    return pltpu.CompilerParams(dimension_semantics=sem, vmem_limit_bytes=VMEM_LIMIT)


def _qk_kernel(x_ref, w_ref, b_ref, c_ref, s1_ref, s2_ref, o_ref):
    y = jnp.dot(x_ref[...], w_ref[...], preferred_element_type=F32) + b_ref[...]
    c = c_ref[...]
    s1 = s1_ref[...]
    s2 = s2_ref[...]
    for j in range(QK_WIDTH // LANES):
        yj = y[:, j * LANES:(j + 1) * LANES]
        up = pltpu.roll(yj, LANES - ROT_DIM // 2, axis=1)
        dn = pltpu.roll(yj, ROT_DIM // 2, axis=1)
        o_ref[:, j * LANES:(j + 1) * LANES] = yj * c + up * s1 + dn * s2


def _proj_qk(xb, w, b, rope_c, rope_s1, rope_s2):
    n_prompt_blocks = SEQ // TM_QK

    def tab_map(i):
        return (jnp.where(i < N_PROMPT // TM_QK, i % n_prompt_blocks, n_prompt_blocks), 0)

    tab_spec = pl.BlockSpec((TM_QK, LANES), tab_map)
    return pl.pallas_call(
        _qk_kernel,
        grid=(N_TOK // TM_QK,),
        in_specs=[
            pl.BlockSpec((TM_QK, D_MODEL), lambda i: (i, 0)),
            pl.BlockSpec((D_MODEL, QK_WIDTH), lambda i: (0, 0)),
            pl.BlockSpec((1, QK_WIDTH), lambda i: (0, 0)),
            tab_spec, tab_spec, tab_spec,
        ],
        out_specs=pl.BlockSpec((TM_QK, QK_WIDTH), lambda i: (i, 0)),
        out_shape=jax.ShapeDtypeStruct((N_TOK, QK_WIDTH), F32),
        compiler_params=_params(("parallel",)),
        name="proj_qk",
    )(xb, w, b, rope_c, rope_s1, rope_s2)


def _mm_bias_kernel(x_ref, w_ref, b_ref, o_ref):
    o_ref[...] = jnp.dot(x_ref[...], w_ref[...], preferred_element_type=F32) + b_ref[...]


def _proj_rest(xb, w, b):
    return pl.pallas_call(
        _mm_bias_kernel,
        grid=(R_WIDTH // TN_REST, N_TOK // TM_REST),
        in_specs=[
            pl.BlockSpec((TM_REST, D_MODEL), lambda j, i: (i, 0)),
            pl.BlockSpec((D_MODEL, TN_REST), lambda j, i: (0, j)),
            pl.BlockSpec((1, TN_REST), lambda j, i: (0, j)),
        ],
        out_specs=pl.BlockSpec((TM_REST, TN_REST), lambda j, i: (i, j)),
        out_shape=jax.ShapeDtypeStruct((N_TOK, R_WIDTH), F32),
        compiler_params=_params(("parallel", "arbitrary")),
        name="proj_rest",
    )(xb, w, b)


def _attend(q, kk, vv, sink_ref, c_min, o_ref, row0):
    tq = q.shape[0]
    r = lax.broadcasted_iota(jnp.int32, (tq, 2 * WINDOW), 0)
    c = lax.broadcasted_iota(jnp.int32, (tq, 2 * WINDOW), 1)
    ok = (c > r) & (c <= r + WINDOW) & (c >= c_min)
    qb = (q * (A_HEAD_DIM ** -0.5)).astype(BF16)
    for h in range(A_HEADS):
        g = h // A_GROUP
        qh = qb[:, h * A_HEAD_DIM:(h + 1) * A_HEAD_DIM]
        kg = kk[:, g * A_HEAD_DIM:(g + 1) * A_HEAD_DIM]
        vg = vv[:, g * A_HEAD_DIM:(g + 1) * A_HEAD_DIM]
        s = lax.dot_general(qh, kg, (((1,), (1,)), ((), ())), preferred_element_type=F32)
        s = jnp.where(ok, s, NEG_INF)
        sink = sink_ref[h]
        m = jnp.maximum(jnp.max(s, axis=1, keepdims=True), sink)
        p = jnp.exp(s - m)
        denom = jnp.sum(p, axis=1, keepdims=True) + jnp.exp(sink - m)
        o = jnp.dot(p.astype(BF16), vg, preferred_element_type=F32) / denom
        o_ref[pl.ds(row0, tq), h * A_HEAD_DIM:(h + 1) * A_HEAD_DIM] = o.astype(o_ref.dtype)


def _attn_prompt_kernel(sink_ref, q_ref, kc_ref, kp_ref, vc_ref, vp_ref, o_ref):
    n = pl.program_id(1)
    kk = jnp.concatenate([kp_ref[...], kc_ref[...]], axis=0).astype(BF16)
    vv = jnp.concatenate([vp_ref[...], vc_ref[...]], axis=0).astype(BF16)
    _attend(q_ref[...], kk, vv, sink_ref, jnp.where(n > 0, 0, WINDOW), o_ref, 0)


def _attn_prompt(qk, rest, sinks):
    nb = SEQ // WINDOW
    kcol = A_Q // A_KV
    vcol = R_VA // A_KV

    def cur(b, n):
        return b * nb + n

    def prev(b, n):
        return b * nb + jnp.maximum(n - 1, 0)

    return pl.pallas_call(
        _attn_prompt_kernel,
        grid=(BATCH, nb),
        in_specs=[
            pl.BlockSpec(memory_space=pltpu.SMEM),
            pl.BlockSpec((WINDOW, A_Q), lambda b, n: (cur(b, n), 0)),
            pl.BlockSpec((WINDOW, A_KV), lambda b, n: (cur(b, n), kcol)),
            pl.BlockSpec((WINDOW, A_KV), lambda b, n: (prev(b, n), kcol)),
            pl.BlockSpec((WINDOW, A_KV), lambda b, n: (cur(b, n), vcol)),
            pl.BlockSpec((WINDOW, A_KV), lambda b, n: (prev(b, n), vcol)),
        ],
        out_specs=pl.BlockSpec((WINDOW, A_Q), lambda b, n: (cur(b, n), 0)),
        out_shape=jax.ShapeDtypeStruct((N_PROMPT, A_Q), BF16),
        compiler_params=_params(("parallel", "arbitrary")),
        name="attn_prompt",
    )(sinks, qk, qk, qk, rest, rest)


BB_ATTN = 8


def _attn_sample_kernel(sink_ref, q_ref, kn_ref, vn_ref, ks_ref, vs_ref, o_ref):
    pad = jnp.zeros((WINDOW - DEC_SEQ, A_KV), F32)
    rows = A_GROUP * DEC_SEQ
    t = lax.broadcasted_iota(jnp.int32, (rows, 2 * WINDOW), 0) % DEC_SEQ
    c = lax.broadcasted_iota(jnp.int32, (rows, 2 * WINDOW), 1)
    ok = (c > t) & (c <= t + WINDOW)
    nt = (((1,), (1,)), ((), ()))

    def body(bb, carry):
        row0 = pl.multiple_of(bb * DEC_SEQ, DEC_SEQ)
        q = q_ref[pl.ds(row0, DEC_SEQ), :] * (A_HEAD_DIM ** -0.5)
        kk = jnp.concatenate([ks_ref[bb], kn_ref[pl.ds(row0, DEC_SEQ), :], pad], axis=0).astype(BF16)
        vv = jnp.concatenate([vs_ref[bb], vn_ref[pl.ds(row0, DEC_SEQ), :], pad], axis=0).astype(BF16)
        for g in range(A_KV_HEADS):
            heads = range(g * A_GROUP, (g + 1) * A_GROUP)
            qg = jnp.concatenate([q[:, h * A_HEAD_DIM:(h + 1) * A_HEAD_DIM] for h in heads], axis=0)
            ds = slice(g * A_HEAD_DIM, (g + 1) * A_HEAD_DIM)
            s = lax.dot_general(qg.astype(BF16), kk[:, ds], nt, preferred_element_type=F32)
            s = jnp.where(ok, s, NEG_INF)
            sink = sink_ref[g * rows:(g + 1) * rows, 0:1]
            m = jnp.maximum(jnp.max(s, axis=1, keepdims=True), sink)
            p = jnp.exp(s - m)
            denom = jnp.sum(p, axis=1, keepdims=True) + jnp.exp(sink - m)
            o = jnp.dot(p.astype(BF16), vv[:, ds], preferred_element_type=F32) / denom
            for k, h in enumerate(heads):
                o_ref[pl.ds(row0, DEC_SEQ), h * A_HEAD_DIM:(h + 1) * A_HEAD_DIM] = (
                    o[k * DEC_SEQ:(k + 1) * DEC_SEQ, :].astype(o_ref.dtype))
        return carry

    lax.fori_loop(0, BB_ATTN, body, 0)


def _attn_sample(qk, rest, sinks, k_state, v_state):
    rows = BB_ATTN * DEC_SEQ
    base = N_PROMPT // rows
    kcol = A_Q // A_KV
    vcol = R_VA // A_KV
    sink_rows = jnp.broadcast_to(jnp.repeat(sinks, DEC_SEQ)[:, None], (A_HEADS * DEC_SEQ, LANES))
    return pl.pallas_call(
        _attn_sample_kernel,
        grid=(DEC_BATCH // BB_ATTN,),
        in_specs=[
            pl.BlockSpec((A_HEADS * DEC_SEQ, LANES), lambda i: (0, 0)),
            pl.BlockSpec((rows, A_Q), lambda i: (base + i, 0)),
            pl.BlockSpec((rows, A_KV), lambda i: (base + i, kcol)),
            pl.BlockSpec((rows, A_KV), lambda i: (base + i, vcol)),
            pl.BlockSpec((BB_ATTN, WINDOW, A_KV), lambda i: (i, 0, 0)),
            pl.BlockSpec((BB_ATTN, WINDOW, A_KV), lambda i: (i, 0, 0)),
        ],
        out_specs=pl.BlockSpec((rows, A_Q), lambda i: (i, 0)),
        out_shape=jax.ShapeDtypeStruct((N_SAMPLE, A_Q), BF16),
        compiler_params=_params(("parallel",)),
        name="attn_sample",
    )(sink_rows, qk, qk, rest, k_state, v_state)


def _split3(x):
    hi = x.astype(BF16)
    r1 = x - hi.astype(F32)
    mid = r1.astype(BF16)
    lo = (r1 - mid.astype(F32)).astype(BF16)
    return hi, mid, lo


GLA_INPUTS = ((LANES, R_LR), (B_QK, R_QB), (B_QK, R_KB), (B_V, R_VB), (B_V, R_GB))


def _gla_kernel(*refs, n_chunks, group, chunk, layer, n_in):
    ins, rest_refs = refs[:n_in], refs[n_in:]
    per = n_in // len(GLA_INPUTS)
    s0_ref, wa_ref, ba_ref, g_ref = rest_refs[:4]
    prev_ref = rest_refs[4] if layer else None
    o_ref, sout_ref, st_ref = rest_refs[-3:]
    ci = pl.program_id(1)
    c = chunk

    def rows(inp, g):
        if per == 1:
            return ins[inp][g * c:(g + 1) * c, :]
        return ins[inp * per + g][...]

    @pl.when(ci == 0)
    def _():
        for g in range(group):
            for h in range(B_HEADS):
                st_ref[g, h] = s0_ref[0, g, h].T

    ri = lax.broadcasted_iota(jnp.int32, (c, c), 0)
    cj = lax.broadcasted_iota(jnp.int32, (c, c), 1)
    causal = cj <= ri
    tri = jnp.where(causal, 1.0, 0.0).astype(BF16)
    nt = (((1,), (1,)), ((), ()))
    G = range(group)
    z = [jnp.dot(rows(0, g).astype(BF16), wa_ref[...], preferred_element_type=F32) + ba_ref[...] for g in G]
    log_a = [-(jnp.maximum(-z[g], 0.0) + jnp.log1p(jnp.exp(-jnp.abs(z[g])))) / B_TAU for g in G]
    parts = [_split3(log_a[g]) for g in G]
    b = [jnp.dot(tri, parts[g][0], preferred_element_type=F32)
         + jnp.dot(tri, parts[g][1], preferred_element_type=F32)
         + jnp.dot(tri, parts[g][2], preferred_element_type=F32) for g in G]
    bl = [b[g][c - 1:c, :] for g in G]
    qd = [(rows(1, g) * (B_DK ** -0.5) * jnp.exp(b[g])).astype(BF16) for g in G]
    kd = [(rows(2, g) * jnp.exp(-b[g])).astype(BF16) for g in G]
    kl = [(rows(2, g) * jnp.exp(bl[g] - b[g])).astype(BF16) for g in G]
    ebl = [jnp.exp(bl[g]) for g in G]
    for h in range(B_HEADS):
        ks = slice(h * B_DK, (h + 1) * B_DK)
        vs = slice(h * B_DV, (h + 1) * B_DV)
        vh = [rows(3, g)[:, vs] for g in G]
        st = [st_ref[g, h] for g in G]
        o = [lax.dot_general(qd[g][:, ks], st[g].astype(BF16), nt, preferred_element_type=F32) for g in G]
        att = [lax.dot_general(qd[g][:, ks], kd[g][:, ks], nt, preferred_element_type=F32) for g in G]
        att = [jnp.where(causal, att[g], 0.0).astype(BF16) for g in G]
        o = [o[g] + jnp.dot(att[g], vh[g].astype(BF16), preferred_element_type=F32) for g in G]
        for g in G:
            st_ref[g, h] = st[g] * ebl[g][:, ks] + jnp.dot(vh[g].T.astype(BF16), kl[g][:, ks],
                                                           preferred_element_type=F32)
        o = [o[g] * lax.rsqrt(jnp.mean(o[g] * o[g], axis=1, keepdims=True) + RMS_EPS) * g_ref[h:h + 1, :]
             for g in G]
        for g in G:
            gate = rows(4, g)[:, vs]
            o_ref[g, :, vs] = (o[g] * (gate / (1.0 + jnp.exp(-gate)))).astype(o_ref.dtype)

    @pl.when(ci == n_chunks - 1)
    def _():
        for g in range(group):
            for l in range(layer):
                sout_ref[l, g] = prev_ref[l, g]
            for h in range(B_HEADS):
                sout_ref[layer, g, h] = st_ref[g, h].T


def _gla(rest, s0, layer, prev_states, wa, ba, gain, *, n_batch, seq, chunk, row_base, group):
    n_chunks = seq // chunk
    contiguous = n_chunks == 1
    base = row_base // chunk

    def in_specs_for(width, col):
        if contiguous:
            return [pl.BlockSpec((group * chunk, width), lambda b, ci: (base // group + b, col // width))]
        return [pl.BlockSpec((chunk, width),
                             lambda b, ci, g=g: (base + (b * group + g) * n_chunks + ci, col // width))
                for g in range(group)]

    row_specs = [spec for width, col in GLA_INPUTS for spec in in_specs_for(width, col)]
    s0_layer = layer if s0.shape[0] > 1 else 0
    state_block = (group, B_HEADS, B_DK, B_DV)
    prev_specs = [pl.BlockSpec((layer,) + state_block, lambda b, ci: (0, b, 0, 0, 0))] if layer else []
    prev_args = [prev_states] if layer else []
    return pl.pallas_call(
        functools.partial(_gla_kernel, n_chunks=n_chunks, group=group, chunk=chunk, layer=layer,
                          n_in=len(row_specs)),
        grid=(n_batch // group, n_chunks),
        in_specs=row_specs + [
            pl.BlockSpec((1,) + state_block, lambda b, ci: (s0_layer, b, 0, 0, 0)),
            pl.BlockSpec((LANES, B_QK), lambda b, ci: (0, 0)),
            pl.BlockSpec((1, B_QK), lambda b, ci: (0, 0)),
            pl.BlockSpec((B_HEADS, B_DV), lambda b, ci: (0, 0)),
        ] + prev_specs,
        out_specs=[
            pl.BlockSpec((group, chunk, B_V), lambda b, ci: (b, ci, 0)),
            pl.BlockSpec((layer + 1,) + state_block, lambda b, ci: (0, b, 0, 0, 0)),
        ],
        out_shape=[
            jax.ShapeDtypeStruct((n_batch, seq, B_V), BF16),
            jax.ShapeDtypeStruct((layer + 1, n_batch, B_HEADS, B_DK, B_DV), F32),
        ],
        scratch_shapes=[pltpu.VMEM((group, B_HEADS, B_DV, B_DK), F32)],
        compiler_params=_params(("parallel", "arbitrary")),
        name="gla",
    )(*([rest] * len(row_specs)), s0, wa, ba, gain, *prev_args)


HALO = 16


def _pool_kernel(u_ref, prev_ref, w_ref, scale_ref, o_ref, *, from_start):
    tp = u_ref.shape[0]
    u = u_ref[...]
    prev = prev_ref[...]
    if from_start:
        ti = pl.program_id(1)
        prev = jnp.where(ti > 0, prev, 0.0)
        t0 = ti * tp
    full = jnp.concatenate([prev, u], axis=0)
    for g, w in enumerate(POOL_WINDOWS):
        cs = slice(g * C_GROUP_W, (g + 1) * C_GROUP_W)
        acc = full[:, cs]
        span = 1
        while span < w:
            acc = acc + pltpu.roll(acc, span, axis=0)
            span *= 2
        wsum = acc[HALO:, :]
        if from_start:
            t = t0 + lax.broadcasted_iota(jnp.int32, (tp, C_GROUP_W), 0)
            cnt = jnp.minimum(t + 1, w).astype(F32)
        else:
            cnt = float(w)
        d = wsum / cnt - u[:, cs]
        y = jnp.dot(d.astype(BF16), w_ref[g], preferred_element_type=F32) * scale_ref[:, cs]
        o_ref[:, cs] = y.astype(o_ref.dtype)


def _pool_prompt(rest, w, scale):
    nt = SEQ // TP_POOL
    ucol = R_UC // C_WIDTH

    def halo(b, i):
        return (jnp.maximum((b * SEQ + i * TP_POOL) // HALO - 1, 0), ucol)

    return pl.pallas_call(
        functools.partial(_pool_kernel, from_start=True),
        grid=(BATCH, nt),
        in_specs=[
            pl.BlockSpec((TP_POOL, C_WIDTH), lambda b, i: (b * nt + i, ucol)),
            pl.BlockSpec((HALO, C_WIDTH), halo),
            pl.BlockSpec((C_GROUPS, C_GROUP_W, C_GROUP_W), lambda b, i: (0, 0, 0)),
            pl.BlockSpec((1, C_WIDTH), lambda b, i: (0, 0)),
        ],
        out_specs=pl.BlockSpec((TP_POOL, C_WIDTH), lambda b, i: (b * nt + i, 0)),
        out_shape=jax.ShapeDtypeStruct((N_PROMPT, C_WIDTH), BF16),
        compiler_params=_params(("parallel", "arbitrary")),
        name="pool_prompt",
    )(rest, rest, w, scale)


def _pool_sample(rest, prev, w, scale):
    ucol = R_UC // C_WIDTH
    base = N_PROMPT // DEC_SEQ
    return pl.pallas_call(
        functools.partial(_pool_kernel, from_start=False),
        grid=(DEC_BATCH,),
        in_specs=[
            pl.BlockSpec((DEC_SEQ, C_WIDTH), lambda b: (base + b, ucol)),
            pl.BlockSpec((HALO, C_WIDTH), lambda b: (b, 0)),
            pl.BlockSpec((C_GROUPS, C_GROUP_W, C_GROUP_W), lambda b: (0, 0, 0)),
            pl.BlockSpec((1, C_WIDTH), lambda b: (0, 0)),
        ],
        out_specs=pl.BlockSpec((DEC_SEQ, C_WIDTH), lambda b: (b, 0)),
        out_shape=jax.ShapeDtypeStruct((N_SAMPLE, C_WIDTH), BF16),
        compiler_params=_params(("parallel",)),
        name="pool_sample",
    )(rest, prev, w, scale)


def _layer_norm(x, g, b):
    mu = jnp.mean(x, axis=1, keepdims=True)
    xc = x - mu
    var = jnp.mean(xc * xc, axis=1, keepdims=True)
    return xc * lax.rsqrt(var + LN_EPS) * g + b


def _merge_kernel(x_ref, gates_ref, oa_p_ref, oa_s_ref, ob_p_ref, ob_s_ref, oc_p_ref, oc_s_ref,
                  wa_ref, wb_ref, wc_ref, wo_ref, g_ref, b_ref, o_ref, oT_ref):
    in_sample = pl.program_id(0) >= N_PROMPT // TM_MERGE

    def gate(i):
        z = gates_ref[:, i * D_MODEL:(i + 1) * D_MODEL]
        return 1.0 / (1.0 + jnp.exp(-z))

    def branch(p_ref, s_ref, w_ref):
        o = jnp.where(in_sample, s_ref[...], p_ref[...])
        return jnp.dot(o, w_ref[...], preferred_element_type=F32)

    merged = (gate(0) * branch(oa_p_ref, oa_s_ref, wa_ref)
              + gate(1) * branch(ob_p_ref, ob_s_ref, wb_ref)
              + gate(2) * branch(oc_p_ref, oc_s_ref, wc_ref))
    mix = jnp.dot(merged.astype(BF16), wo_ref[...], preferred_element_type=F32)
    y = _layer_norm(DN_ALPHA * x_ref[...] + mix, g_ref[...], b_ref[...])
    o_ref[...] = y
    oT_ref[...] = y.T.astype(BF16)


def _merge(x, rest, branches, wa, wb, wc, wo, g, b):
    n_p = N_PROMPT // TM_MERGE
    row = pl.BlockSpec((TM_MERGE, D_MODEL), lambda i: (i, 0))
    row_p = pl.BlockSpec((TM_MERGE, D_MODEL), lambda i: (jnp.minimum(i, n_p - 1), 0))
    row_s = pl.BlockSpec((TM_MERGE, D_MODEL), lambda i: (jnp.maximum(i - n_p, 0), 0))
    wspec = pl.BlockSpec((D_MODEL, D_MODEL), lambda i: (0, 0))
    vec = pl.BlockSpec((1, D_MODEL), lambda i: (0, 0))
    (oa_p, oa_s), (ob_p, ob_s), (oc_p, oc_s) = branches
    return pl.pallas_call(
        _merge_kernel,
        grid=(N_TOK // TM_MERGE,),
        in_specs=[row, pl.BlockSpec((TM_MERGE, 3 * D_MODEL), lambda i: (i, R_GATES)),
                  row_p, row_s, row_p, row_s, row_p, row_s, wspec, wspec, wspec, wspec, vec, vec],
        out_specs=[row, pl.BlockSpec((D_MODEL, TM_MERGE), lambda i: (0, i))],
        out_shape=[jax.ShapeDtypeStruct((N_TOK, D_MODEL), F32),
                   jax.ShapeDtypeStruct((D_MODEL, N_TOK), BF16)],
        compiler_params=_params(("parallel",)),
        name="merge",
    )(x, rest, oa_p, oa_s, ob_p, ob_s, oc_p, oc_s, wa, wb, wc, wo, g, b)


SUBLANES = 8


def _extract_desc(s, n):
    vals = []
    for _ in range(n):
        m = jnp.max(s, axis=0, keepdims=True)
        vals.append(m)
        s = jnp.where(s == m, -jnp.inf, s)
    return vals


def _merge_sort_pairs(n):
    pairs = []
    p = 1
    while p < n:
        k = p
        while k >= 1:
            for j in range(k % p, n - k, 2 * k):
                for i in range(min(k, n - j - k)):
                    if (i + j) // (2 * p) == (i + j + k) // (2 * p):
                        pairs.append((i + j, i + j + k))
            k //= 2
        p *= 2
    return pairs


def _top_desc(s, n):
    v = [s[SUBLANES * k:SUBLANES * (k + 1), :] for k in range(s.shape[0] // SUBLANES)]
    for a, b in _merge_sort_pairs(len(v)):
        v[a], v[b] = jnp.maximum(v[a], v[b]), jnp.minimum(v[a], v[b])
    vals = []
    for r in range(n):
        m = jnp.max(v[0], axis=0, keepdims=True)
        vals.append(m)
        hit = v[0] == m
        for k in range(n - r - 1):
            v[k] = jnp.where(hit, v[k + 1], v[k])
    return vals


def _rank_of(s, vals):
    rank = jnp.full(s.shape, float(len(vals)), F32)
    for r, val in enumerate(vals):
        rank = jnp.where(s == val, float(r), rank)
    return rank


def _peer_topk_kernel(xT_ref, wq_ref, sk_ref, cnt_ref, e1_ref, rank_ref, e2_ref, s1_scr, sv_scr):
    tt = xT_ref.shape[1]
    xT = xT_ref[...]
    for hp in range(2 * PEER_HEADS):
        h, second = divmod(hp, 2)
        qT = jnp.dot(wq_ref[hp * PEER_DHALF:(hp + 1) * PEER_DHALF, :], xT, preferred_element_type=F32)
        s = jnp.dot(sk_ref[hp], qT.astype(BF16), preferred_element_type=F32)
        vals = _top_desc(s, PEER_TOPK)
        sv_scr[hp] = jnp.concatenate(vals, axis=0)
        if second:
            rank_ref[h] = _rank_of(s, vals).astype(BF16)
            e2_ref[h] = jnp.exp(s - vals[0]).astype(BF16)
        else:
            s1_scr[h] = s
    row8 = lax.broadcasted_iota(jnp.int32, (8, tt), 0)
    for h in range(PEER_HEADS):
        sv1 = sv_scr[2 * h]
        sv2 = sv_scr[2 * h + 1]
        pieces = [sv1[0:1, :] + sv2]
        for a in range(1, 8):
            nb = PEER_TOPK // (a + 1)
            pieces.append(jnp.where(row8 < nb, sv1[a:a + 1, :] + sv2[0:8, :], -jnp.inf))
        pieces.append(sv2[0:1, :] + sv1[8:16, :])
        cand = _extract_desc(jnp.concatenate(pieces, axis=0), PEER_TOPK)
        top = cand[0]
        z = jnp.zeros_like(top)
        for r in range(PEER_TOPK):
            z = z + jnp.exp(cand[r] - top)
        kth = cand[PEER_TOPK - 1]
        counts = [jnp.sum(jnp.where(p >= kth, 1.0, 0.0), axis=0, keepdims=True) for p in pieces[:8]]
        tail = jnp.where(pieces[8] >= kth, 1.0, 0.0)
        counts += [tail[a:a + 1, :] for a in range(8)]
        s1 = s1_scr[h]
        cnt = jnp.zeros_like(s1)
        for a in range(PEER_TOPK):
            cnt = jnp.where(s1 == sv1[a:a + 1, :], counts[a], cnt)
        e1 = jnp.exp(s1 - (sv1[0:1, :] + jnp.log(z))) * SQRT_HALF
        for c in range(tt // LANES):
            cnt_ref[h, c] = cnt[:, c * LANES:(c + 1) * LANES]
            e1_ref[h, c] = e1[:, c * LANES:(c + 1) * LANES]


def _peer_topk(xT, wqT, sk):
    spec = pl.BlockSpec((PEER_HEADS, N_KEYS, TT_TOPK), lambda t: (0, 0, t))
    row_spec = pl.BlockSpec((PEER_HEADS, TT_TOPK // LANES, N_KEYS, LANES), lambda t: (0, t, 0, 0))
    wide = jax.ShapeDtypeStruct((PEER_HEADS, N_TOK // LANES, N_KEYS, LANES), F32)
    narrow = jax.ShapeDtypeStruct((PEER_HEADS, N_KEYS, N_TOK), BF16)
    return pl.pallas_call(
        _peer_topk_kernel,
        grid=(N_TOK // TT_TOPK,),
        in_specs=[
            pl.BlockSpec((D_MODEL, TT_TOPK), lambda t: (0, t)),
            pl.BlockSpec((PEER_HEADS * PEER_DKEY, D_MODEL), lambda t: (0, 0)),
            pl.BlockSpec((2 * PEER_HEADS, N_KEYS, PEER_DHALF), lambda t: (0, 0, 0)),
        ],
        out_specs=[row_spec, row_spec, spec, spec],
        out_shape=[wide, wide, narrow, narrow],
        scratch_shapes=[pltpu.VMEM((PEER_HEADS, N_KEYS, TT_TOPK), F32),
                        pltpu.VMEM((2 * PEER_HEADS, PEER_TOPK, TT_TOPK), F32)],
        compiler_params=_params(("parallel",)),
        name="peer_topk",
    )(xT, wqT, sk)


SQRT_HALF = 2.0 ** -0.5


def _gelu_unscaled(x):
    t = x * SQRT_HALF
    return t * (1.0 + lax.erf(t))


LC_PEER = 256


def _bf16_rows(ref, h, r, chunks, n_rows):
    x = jnp.concatenate([ref[h, c, pl.ds(r, 16, stride=0), :] for c in chunks], axis=1)
    packed = x.astype(BF16)
    return jnp.concatenate([packed] * (n_rows // packed.shape[0]), axis=0)


MM_PIECE = 256


def _peer_main_kernel(xT_ref, u_ref, vt_ref, cnt_ref, e1_ref, rank_ref, e2_ref, yT_ref, s_scr, wh_scr):
    i = pl.program_id(1)
    tt = xT_ref.shape[1]

    @pl.when(i == 0)
    def _():
        yT_ref[...] = jnp.zeros_like(yT_ref)

    zero = jnp.zeros((N_KEYS, LC_PEER), BF16)
    per_piece = MM_PIECE // N_KEYS
    for ii in range(IB_PEER):
        rows = slice(ii * N_KEYS, (ii + 1) * N_KEYS)
        if ii % per_piece == 0:
            piece = slice(ii * N_KEYS, ii * N_KEYS + MM_PIECE)
            s_scr[piece, :] = jnp.dot(u_ref[0, piece, :], xT_ref[...], preferred_element_type=F32)
        for lc in range(tt // LC_PEER):
            cols = slice(lc * LC_PEER, (lc + 1) * LC_PEER)
            chunks = range(lc * LC_PEER // LANES, (lc + 1) * LC_PEER // LANES)
            w = zero
            for h in range(PEER_HEADS):
                cnt = _bf16_rows(cnt_ref, h, ii, chunks, N_KEYS)
                e1 = _bf16_rows(e1_ref, h, ii, chunks, N_KEYS)
                w = w + jnp.where(rank_ref[h, :, cols] < cnt, e2_ref[h, :, cols] * e1, zero)
            wh_scr[rows, cols] = w
    for ii in range(IB_PEER):
        rows = slice(ii * N_KEYS, (ii + 1) * N_KEYS)
        wh_scr[rows, :] = wh_scr[rows, :] * _gelu_unscaled(s_scr[rows, :]).astype(BF16)
    yT_ref[...] += jnp.dot(vt_ref[0], wh_scr[...], preferred_element_type=F32)


def _peer_main(xT, u, vt, layer, cnt, e1, rank, e2):
    eb = IB_PEER * N_KEYS
    row_spec = pl.BlockSpec((PEER_HEADS, TT_PEER // LANES, IB_PEER, LANES), lambda t, i: (0, t, i, 0))
    tab_spec = pl.BlockSpec((PEER_HEADS, N_KEYS, TT_PEER), lambda t, i: (0, 0, t))
    return pl.pallas_call(
        _peer_main_kernel,
        grid=(N_TOK // TT_PEER, N_EXPERTS // eb),
        in_specs=[
            pl.BlockSpec((D_MODEL, TT_PEER), lambda t, i: (0, t)),
            pl.BlockSpec((1, eb, D_MODEL), lambda t, i: (layer, i, 0)),
            pl.BlockSpec((1, D_MODEL, eb), lambda t, i: (layer, 0, i)),
            row_spec, row_spec, tab_spec, tab_spec,
        ],
        out_specs=pl.BlockSpec((D_MODEL, TT_PEER), lambda t, i: (0, t)),
        out_shape=jax.ShapeDtypeStruct((D_MODEL, N_TOK), F32),
        scratch_shapes=[pltpu.VMEM((eb, TT_PEER), F32), pltpu.VMEM((eb, TT_PEER), BF16)],
        compiler_params=_params(("parallel", "arbitrary")),
        name="peer_main",
    )(xT, u, vt, cnt, e1, rank, e2)


def _peer_out_kernel(x_ref, yT_ref, g_ref, b_ref, o_ref, ob_ref):
    y = _layer_norm(DN_ALPHA * x_ref[...] + yT_ref[...].T, g_ref[...], b_ref[...])
    o_ref[...] = y
    ob_ref[...] = y.astype(BF16)


def _peer_out(x1, yT, g, b):
    row = pl.BlockSpec((TM_MERGE, D_MODEL), lambda i: (i, 0))
    vec = pl.BlockSpec((1, D_MODEL), lambda i: (0, 0))
    return pl.pallas_call(
        _peer_out_kernel,
        grid=(N_TOK // TM_MERGE,),
        in_specs=[row, pl.BlockSpec((D_MODEL, TM_MERGE), lambda i: (0, i)), vec, vec],
        out_specs=[row, row],
        out_shape=[jax.ShapeDtypeStruct((N_TOK, D_MODEL), F32),
                   jax.ShapeDtypeStruct((N_TOK, D_MODEL), BF16)],
        compiler_params=_params(("parallel",)),
        name="peer_out",
    )(x1, yT, g, b)


def _rope_tables():
    half = ROT_DIM // 2
    pos = jnp.concatenate([jnp.arange(SEQ), PAST_LEN + (jnp.arange(TM_QK) % DEC_SEQ)])
    inv = ROPE_THETA ** (-jnp.arange(half, dtype=F32) / half)
    ang = pos.astype(F32)[:, None] * inv[None, :]
    cos, sin = jnp.cos(ang), jnp.sin(ang)
    n = pos.shape[0]
    one = jnp.ones((n, A_HEAD_DIM - ROT_DIM), F32)
    zero = jnp.zeros((n, A_HEAD_DIM - ROT_DIM), F32)
    zh = jnp.zeros((n, half), F32)
    reps = LANES // A_HEAD_DIM
    c = jnp.tile(jnp.concatenate([cos, cos, one], 1), (1, reps))
    s1 = jnp.tile(jnp.concatenate([-sin, zh, zero], 1), (1, reps))
    s2 = jnp.tile(jnp.concatenate([zh, sin, zero], 1), (1, reps))
    return c, s1, s2


def _split_cols(w):
    cuts = [int(c) for c in np.cumsum(SPLITS)[:-1]]
    return jnp.split(w, cuts, axis=-1)


def _layer(layer, x, xb, k_state, v_state, gla_states, gla_out, pool_state, rope, w_in, b_in, sinks,
           w_alpha, b_alpha, gla_g, w_pool, pool_scale, w_a, w_b, w_c, w_out, ln1_g, ln1_b,
           peer_query, peer_subkeys, peer_u, peer_vt, ln2_g, ln2_b):
    qa_w, ka_w, va_w, qb_w, kb_w, vb_w, lr_w, gb_w, uc_w, gates_w = _split_cols(w_in)
    qa_b, ka_b, va_b, qb_b, kb_b, vb_b, lr_b, gb_b, uc_b, gates_b = _split_cols(b_in[None, :])
    lr_pad = LANES - B_GATE_RANK
    w_qk = jnp.concatenate([qa_w, ka_w], 1).astype(BF16)
    b_qk = jnp.concatenate([qa_b, ka_b], 1)
    w_rest = jnp.concatenate([gates_w, vb_w, gb_w, uc_w, qb_w, kb_w, va_w,
                              jnp.pad(lr_w, ((0, 0), (0, lr_pad)))], 1).astype(BF16)
    b_rest = jnp.concatenate([gates_b, vb_b, gb_b, uc_b, qb_b, kb_b, va_b,
                              jnp.pad(lr_b, ((0, 0), (0, lr_pad)))], 1)

    qk = _proj_qk(xb, w_qk, b_qk, *rope)
    rest = _proj_rest(xb, w_rest, b_rest)

    ks = k_state.reshape(DEC_BATCH, WINDOW, A_KV)
    vs = v_state.reshape(DEC_BATCH, WINDOW, A_KV)
    oa = (_attn_prompt(qk, rest, sinks), _attn_sample(qk, rest, sinks, ks, vs))

    wa = jnp.pad(w_alpha, ((0, lr_pad), (0, 0))).astype(BF16)
    ba = b_alpha[None, :]
    gla_p, gla_s = gla_out
    ob_p, gla_p = _gla(rest, jnp.zeros((1, BATCH, B_HEADS, B_DK, B_DV), F32), layer, gla_p, wa, ba, gla_g,
                       n_batch=BATCH, seq=SEQ, chunk=GLA_CHUNK, row_base=0, group=GLA_GROUP_PROMPT)
    ob_s, gla_s = _gla(rest, gla_states, layer, gla_s, wa, ba, gla_g, n_batch=DEC_BATCH, seq=DEC_SEQ,
                       chunk=math.gcd(DEC_SEQ, GLA_CHUNK), row_base=N_PROMPT, group=GLA_GROUP_SAMPLE)
    ob = (ob_p.reshape(N_PROMPT, B_V), ob_s.reshape(N_SAMPLE, B_V))

    wp = w_pool.astype(BF16)
    ps = pool_scale[None, :]
    prev = jnp.pad(pool_state, ((0, 0), (HALO - POOL_STATE, 0), (0, 0))).reshape(DEC_BATCH * HALO, C_WIDTH)
    oc = (_pool_prompt(rest, wp, ps), _pool_sample(rest, prev, wp, ps))

    x1, x1T = _merge(x, rest, (oa, ob, oc), w_a.astype(BF16), w_b.astype(BF16), w_c.astype(BF16),
                     w_out.astype(BF16), ln1_g[None, :], ln1_b[None, :])

    wqT = peer_query.reshape(D_MODEL, PEER_HEADS * PEER_DKEY).T.astype(BF16)
    sk = peer_subkeys.reshape(2 * PEER_HEADS, N_KEYS, PEER_DHALF).astype(BF16)
    cnt, e1, rank, e2 = _peer_topk(x1T, wqT, sk)
    yT = _peer_main(x1T, peer_u, peer_vt, layer, cnt, e1, rank, e2)
    x2, x2b = _peer_out(x1, yT, ln2_g[None, :], ln2_b[None, :])

    def prompt_tail(t, col0, width, n):
        return jnp.stack([t[(b + 1) * SEQ - n:(b + 1) * SEQ, col0:col0 + width] for b in range(BATCH)])

    def sample_tail(state, t, col0, width, n):
        new = t[N_PROMPT:, col0:col0 + width].reshape(DEC_BATCH, DEC_SEQ, width)
        return jnp.concatenate([state, new], 1)[:, -n:]

    kv_shape = (-1, WINDOW, A_KV_HEADS, A_HEAD_DIM)
    states = (prompt_tail(qk, A_Q, A_KV, WINDOW).reshape(kv_shape),
              prompt_tail(rest, R_VA, A_KV, WINDOW).reshape(kv_shape),
              prompt_tail(rest, R_UC, C_WIDTH, POOL_STATE),
              sample_tail(ks, qk, A_Q, A_KV, WINDOW).reshape(kv_shape),
              sample_tail(vs, rest, R_VA, A_KV, WINDOW).reshape(kv_shape),
              sample_tail(pool_state, rest, R_UC, C_WIDTH, POOL_STATE))
    return x2, x2b, states, (gla_p, gla_s)


def kernel(x_prompt, x_sample, state_win_k, state_win_v, state_gla, state_pool, w_in, b_in, attn_sinks,
           w_alpha, b_alpha, gla_norm_g, w_pool, pool_scale, w_branch_a, w_branch_b, w_branch_c, w_out,
           ln1_g, ln1_b, peer_query, peer_subkeys, peer_u, peer_v, ln2_g, ln2_b):
    x = jnp.concatenate([x_prompt.reshape(N_PROMPT, D_MODEL), x_sample.reshape(N_SAMPLE, D_MODEL)], 0)
    xb = x.astype(BF16)
    rope = _rope_tables()
    peer_ub = peer_u.astype(BF16)
    peer_vtb = jnp.swapaxes(peer_v, 1, 2).astype(BF16)
    per_layer = []
    gla_out = (None, None)
    for l in range(DEPTH):
        x, xb, states, gla_out = _layer(
            l, x, xb, state_win_k[l], state_win_v[l], state_gla, gla_out, state_pool[l], rope,
            w_in[l], b_in[l], attn_sinks[l], w_alpha[l], b_alpha[l], gla_norm_g[l],
            w_pool[l], pool_scale[l], w_branch_a[l], w_branch_b[l], w_branch_c[l], w_out[l],
            ln1_g[l], ln1_b[l], peer_query[l], peer_subkeys[l], peer_ub, peer_vtb, ln2_g[l], ln2_b[l])
        per_layer.append(states)
    pk, pv, pp, sk, sv, sp = [jnp.stack([per_layer[l][i] for l in range(DEPTH)]) for i in range(6)]
    return (x[:N_PROMPT].reshape(BATCH, SEQ, D_MODEL), x[N_PROMPT:].reshape(DEC_BATCH, DEC_SEQ, D_MODEL),
            pk, pv, gla_out[0], pp, sk, sv, gla_out[1], sp)
```

```python
import functools
import math

import jax
import jax.numpy as jnp
import numpy as np
from jax import lax
from jax.experimental import pallas as pl
from jax.experimental.pallas import tpu as pltpu

F32 = jnp.float32
BF16 = jnp.bfloat16

D_MODEL = 1024
BATCH = 8
SEQ = 2048
DEPTH = 2
DEC_BATCH = 128
DEC_SEQ = 8
PAST_LEN = 16384

A_HEADS = 16
A_KV_HEADS = 2
A_HEAD_DIM = 64
A_GROUP = A_HEADS // A_KV_HEADS
WINDOW = 128
ROT_DIM = A_HEAD_DIM // 4
ROPE_THETA = 500000.0
NEG_INF = -1e30
B_HEADS = 4
B_DK = D_MODEL // 2 // B_HEADS
B_DV = D_MODEL // B_HEADS
B_GATE_RANK = 16
B_TAU = 16.0
GLA_CHUNK = 64
POOL_WINDOWS = (2, 4, 8, 16)
C_GROUPS = len(POOL_WINDOWS)
C_GROUP_W = D_MODEL // C_GROUPS
C_WIDTH = C_GROUPS * C_GROUP_W
POOL_STATE = max(POOL_WINDOWS) - 1
PEER_HEADS = 8
N_KEYS = 128
N_EXPERTS = N_KEYS * N_KEYS
PEER_TOPK = 16
PEER_DKEY = 256
PEER_DHALF = PEER_DKEY // 2
DN_ALPHA = (2 * DEPTH) ** 0.25
LN_EPS = 1e-5
RMS_EPS = 1e-6

A_Q = A_HEADS * A_HEAD_DIM
A_KV = A_KV_HEADS * A_HEAD_DIM
B_QK = B_HEADS * B_DK
B_V = B_HEADS * B_DV
SPLITS = (A_Q, A_KV, A_KV, B_QK, B_QK, B_V, B_GATE_RANK, B_V, C_WIDTH, 3 * D_MODEL)

LANES = 128
N_PROMPT = BATCH * SEQ
N_SAMPLE = DEC_BATCH * DEC_SEQ
N_TOK = N_PROMPT + N_SAMPLE

R_GATES = 0
R_VB = 3 * D_MODEL
R_GB = R_VB + B_V
R_UC = R_GB + B_V
R_QB = R_UC + C_WIDTH
R_KB = R_QB + B_QK
R_VA = R_KB + B_QK
R_LR = R_VA + A_KV
R_WIDTH = R_LR + LANES
QK_WIDTH = A_Q + A_KV

VMEM_LIMIT = 48 * 1024 * 1024

TM_QK = 512
TM_REST = 512
TN_REST = R_WIDTH // 2
TM_MERGE = 256
TP_POOL = 512
GLA_GROUP_PROMPT = 2
GLA_GROUP_SAMPLE = 4
TT_TOPK = 256
TT_PEER = 1024
IB_PEER = 8


def _params(sem):
    return pltpu.CompilerParams(dimension_semantics=sem, vmem_limit_bytes=VMEM_LIMIT)


def _qk_kernel(x_ref, w_ref, b_ref, c_ref, s1_ref, s2_ref, o_ref):
    y = jnp.dot(x_ref[...], w_ref[...], preferred_element_type=F32) + b_ref[...]
    c = c_ref[...]
    s1 = s1_ref[...]
    s2 = s2_ref[...]
    for j in range(QK_WIDTH // LANES):
        yj = y[:, j * LANES:(j + 1) * LANES]
        up = pltpu.roll(yj, LANES - ROT_DIM // 2, axis=1)
        dn = pltpu.roll(yj, ROT_DIM // 2, axis=1)
        o_ref[:, j * LANES:(j + 1) * LANES] = yj * c + up * s1 + dn * s2


def _proj_qk(xb, w, b, rope_c, rope_s1, rope_s2):
    n_prompt_blocks = SEQ // TM_QK

    def tab_map(i):
        return (jnp.where(i < N_PROMPT // TM_QK, i % n_prompt_blocks, n_prompt_blocks), 0)

    tab_spec = pl.BlockSpec((TM_QK, LANES), tab_map)
    return pl.pallas_call(
        _qk_kernel,
        grid=(N_TOK // TM_QK,),
        in_specs=[
            pl.BlockSpec((TM_QK, D_MODEL), lambda i: (i, 0)),
            pl.BlockSpec((D_MODEL, QK_WIDTH), lambda i: (0, 0)),
            pl.BlockSpec((1, QK_WIDTH), lambda i: (0, 0)),
            tab_spec, tab_spec, tab_spec,
        ],
        out_specs=pl.BlockSpec((TM_QK, QK_WIDTH), lambda i: (i, 0)),
        out_shape=jax.ShapeDtypeStruct((N_TOK, QK_WIDTH), F32),
        compiler_params=_params(("parallel",)),
        name="proj_qk",
    )(xb, w, b, rope_c, rope_s1, rope_s2)


def _mm_bias_kernel(x_ref, w_ref, b_ref, o_ref):
    o_ref[...] = jnp.dot(x_ref[...], w_ref[...], preferred_element_type=F32) + b_ref[...]


def _proj_rest(xb, w, b):
    return pl.pallas_call(
        _mm_bias_kernel,
        grid=(R_WIDTH // TN_REST, N_TOK // TM_REST),
        in_specs=[
            pl.BlockSpec((TM_REST, D_MODEL), lambda j, i: (i, 0)),
            pl.BlockSpec((D_MODEL, TN_REST), lambda j, i: (0, j)),
            pl.BlockSpec((1, TN_REST), lambda j, i: (0, j)),
        ],
        out_specs=pl.BlockSpec((TM_REST, TN_REST), lambda j, i: (i, j)),
        out_shape=jax.ShapeDtypeStruct((N_TOK, R_WIDTH), F32),
        compiler_params=_params(("parallel", "arbitrary")),
        name="proj_rest",
    )(xb, w, b)


HEADS_PER_PASS = 16


def _attend(q, kk, vv, sink_ref, c_min, o_ref, row0):
    tq = q.shape[0]
    r = lax.broadcasted_iota(jnp.int32, (tq, 2 * WINDOW), 0)
    c = lax.broadcasted_iota(jnp.int32, (tq, 2 * WINDOW), 1)
    ok = (c > r) & (c <= r + WINDOW) & (c >= c_min)
    qb = (q * (A_HEAD_DIM ** -0.5)).astype(BF16)
    nt = (((1,), (1,)), ((), ()))
    for h0 in range(0, A_HEADS, HEADS_PER_PASS):
        hs = range(h0, h0 + HEADS_PER_PASS)
        col = {h: slice(h * A_HEAD_DIM, (h + 1) * A_HEAD_DIM) for h in hs}
        kv = {h: slice((h // A_GROUP) * A_HEAD_DIM, (h // A_GROUP + 1) * A_HEAD_DIM) for h in hs}
        s = {h: lax.dot_general(qb[:, col[h]], kk[:, kv[h]], nt, preferred_element_type=F32) for h in hs}
        s = {h: jnp.where(ok, s[h], NEG_INF) for h in hs}
        m = {h: jnp.maximum(jnp.max(s[h], axis=1, keepdims=True), sink_ref[h]) for h in hs}
        p = {h: jnp.exp(s[h] - m[h]) for h in hs}
        denom = {h: jnp.sum(p[h], axis=1, keepdims=True) + jnp.exp(sink_ref[h] - m[h]) for h in hs}
        o = {h: jnp.dot(p[h].astype(BF16), vv[:, kv[h]], preferred_element_type=F32) / denom[h] for h in hs}
        for h in hs:
            o_ref[pl.ds(row0, tq), col[h]] = o[h].astype(o_ref.dtype)


def _attn_prompt_kernel(sink_ref, q_ref, kc_ref, kp_ref, vc_ref, vp_ref, o_ref):
    n = pl.program_id(1)
    kk = jnp.concatenate([kp_ref[...], kc_ref[...]], axis=0).astype(BF16)
    vv = jnp.concatenate([vp_ref[...], vc_ref[...]], axis=0).astype(BF16)
    _attend(q_ref[...], kk, vv, sink_ref, jnp.where(n > 0, 0, WINDOW), o_ref, 0)


def _attn_prompt(qk, rest, sinks):
    nb = SEQ // WINDOW
    kcol = A_Q // A_KV
    vcol = R_VA // A_KV

    def cur(b, n):
        return b * nb + n

    def prev(b, n):
        return b * nb + jnp.maximum(n - 1, 0)

    return pl.pallas_call(
        _attn_prompt_kernel,
        grid=(BATCH, nb),
        in_specs=[
            pl.BlockSpec(memory_space=pltpu.SMEM),
            pl.BlockSpec((WINDOW, A_Q), lambda b, n: (cur(b, n), 0)),
            pl.BlockSpec((WINDOW, A_KV), lambda b, n: (cur(b, n), kcol)),
            pl.BlockSpec((WINDOW, A_KV), lambda b, n: (prev(b, n), kcol)),
            pl.BlockSpec((WINDOW, A_KV), lambda b, n: (cur(b, n), vcol)),
            pl.BlockSpec((WINDOW, A_KV), lambda b, n: (prev(b, n), vcol)),
        ],
        out_specs=pl.BlockSpec((WINDOW, A_Q), lambda b, n: (cur(b, n), 0)),
        out_shape=jax.ShapeDtypeStruct((N_PROMPT, A_Q), BF16),
        compiler_params=_params(("parallel", "arbitrary")),
        name="attn_prompt",
    )(sinks, qk, qk, qk, rest, rest)


BB_ATTN = 8


def _attn_sample_kernel(sink_ref, q_ref, kn_ref, vn_ref, ks_ref, vs_ref, o_ref):
    pad = jnp.zeros((WINDOW - DEC_SEQ, A_KV), F32)
    rows = A_GROUP * DEC_SEQ
    t = lax.broadcasted_iota(jnp.int32, (rows, 2 * WINDOW), 0) % DEC_SEQ
    c = lax.broadcasted_iota(jnp.int32, (rows, 2 * WINDOW), 1)
    ok = (c > t) & (c <= t + WINDOW)
    nt = (((1,), (1,)), ((), ()))

    def body(pair, carry):
        chains = [(e, g) for e in range(2) for g in range(A_KV_HEADS)]
        elem = {e: pair * 2 + e for e in range(2)}
        row0 = {e: pl.multiple_of(elem[e] * DEC_SEQ, DEC_SEQ) for e in range(2)}
        q = {e: q_ref[pl.ds(row0[e], DEC_SEQ), :] * (A_HEAD_DIM ** -0.5) for e in range(2)}
        kk = {e: jnp.concatenate([ks_ref[elem[e]], kn_ref[pl.ds(row0[e], DEC_SEQ), :], pad], axis=0).astype(BF16)
              for e in range(2)}
        vv = {e: jnp.concatenate([vs_ref[elem[e]], vn_ref[pl.ds(row0[e], DEC_SEQ), :], pad], axis=0).astype(BF16)
              for e in range(2)}
        heads = {g: range(g * A_GROUP, (g + 1) * A_GROUP) for g in range(A_KV_HEADS)}
        ds = {g: slice(g * A_HEAD_DIM, (g + 1) * A_HEAD_DIM) for g in range(A_KV_HEADS)}
        sink = {g: sink_ref[g * rows:(g + 1) * rows, 0:1] for g in range(A_KV_HEADS)}
        qg = {(bb, g): jnp.concatenate([q[bb][:, h * A_HEAD_DIM:(h + 1) * A_HEAD_DIM] for h in heads[g]],
                                       axis=0).astype(BF16) for bb, g in chains}
        s = {(bb, g): lax.dot_general(qg[bb, g], kk[bb][:, ds[g]], nt, preferred_element_type=F32)
             for bb, g in chains}
        s = {ch: jnp.where(ok, s[ch], NEG_INF) for ch in chains}
        m = {(bb, g): jnp.maximum(jnp.max(s[bb, g], axis=1, keepdims=True), sink[g]) for bb, g in chains}
        p = {ch: jnp.exp(s[ch] - m[ch]) for ch in chains}
        denom = {(bb, g): jnp.sum(p[bb, g], axis=1, keepdims=True) + jnp.exp(sink[g] - m[bb, g])
                 for bb, g in chains}
        o = {(bb, g): jnp.dot(p[bb, g].astype(BF16), vv[bb][:, ds[g]], preferred_element_type=F32) / denom[bb, g]
             for bb, g in chains}
        for bb, g in chains:
            for k, h in enumerate(heads[g]):
                o_ref[pl.ds(row0[bb], DEC_SEQ), h * A_HEAD_DIM:(h + 1) * A_HEAD_DIM] = (
                    o[bb, g][k * DEC_SEQ:(k + 1) * DEC_SEQ, :].astype(o_ref.dtype))
        return carry

    lax.fori_loop(0, BB_ATTN // 2, body, 0)


def _attn_sample(qk, rest, sinks, k_state, v_state):
    rows = BB_ATTN * DEC_SEQ
    base = N_PROMPT // rows
    kcol = A_Q // A_KV
    vcol = R_VA // A_KV
    sink_rows = jnp.broadcast_to(jnp.repeat(sinks, DEC_SEQ)[:, None], (A_HEADS * DEC_SEQ, LANES))
    return pl.pallas_call(
        _attn_sample_kernel,
        grid=(DEC_BATCH // BB_ATTN,),
        in_specs=[
            pl.BlockSpec((A_HEADS * DEC_SEQ, LANES), lambda i: (0, 0)),
            pl.BlockSpec((rows, A_Q), lambda i: (base + i, 0)),
            pl.BlockSpec((rows, A_KV), lambda i: (base + i, kcol)),
            pl.BlockSpec((rows, A_KV), lambda i: (base + i, vcol)),
            pl.BlockSpec((BB_ATTN, WINDOW, A_KV), lambda i: (i, 0, 0)),
            pl.BlockSpec((BB_ATTN, WINDOW, A_KV), lambda i: (i, 0, 0)),
        ],
        out_specs=pl.BlockSpec((rows, A_Q), lambda i: (i, 0)),
        out_shape=jax.ShapeDtypeStruct((N_SAMPLE, A_Q), BF16),
        compiler_params=_params(("parallel",)),
        name="attn_sample",
    )(sink_rows, qk, qk, rest, k_state, v_state)


def _split3(x):
    hi = x.astype(BF16)
    r1 = x - hi.astype(F32)
    mid = r1.astype(BF16)
    lo = (r1 - mid.astype(F32)).astype(BF16)
    return hi, mid, lo


GLA_INPUTS = ((LANES, R_LR), (B_QK, R_QB), (B_QK, R_KB), (B_V, R_VB), (B_V, R_GB))


def _gla_kernel(*refs, n_chunks, group, chunk, layer, n_in):
    ins, rest_refs = refs[:n_in], refs[n_in:]
    per = n_in // len(GLA_INPUTS)
    s0_ref, wa_ref, ba_ref, g_ref = rest_refs[:4]
    prev_ref = rest_refs[4] if layer else None
    o_ref, sout_ref, st_ref = rest_refs[-3:]
    ci = pl.program_id(1)
    c = chunk

    def rows(inp, g):
        if per == 1:
            return ins[inp][g * c:(g + 1) * c, :]
        return ins[inp * per + g][...]

    @pl.when(ci == 0)
    def _():
        for g in range(group):
            for h in range(B_HEADS):
                st_ref[g, h] = s0_ref[0, g, h].T

    ri = lax.broadcasted_iota(jnp.int32, (c, c), 0)
    cj = lax.broadcasted_iota(jnp.int32, (c, c), 1)
    causal = cj <= ri
    tri = jnp.where(causal, 1.0, 0.0).astype(BF16)
    nt = (((1,), (1,)), ((), ()))
    G = range(group)
    z = [jnp.dot(rows(0, g).astype(BF16), wa_ref[...], preferred_element_type=F32) + ba_ref[...] for g in G]
    log_a = [-(jnp.maximum(-z[g], 0.0) + jnp.log1p(jnp.exp(-jnp.abs(z[g])))) / B_TAU for g in G]
    parts = [_split3(log_a[g]) for g in G]
    b = [jnp.dot(tri, parts[g][0], preferred_element_type=F32)
         + jnp.dot(tri, parts[g][1], preferred_element_type=F32)
         + jnp.dot(tri, parts[g][2], preferred_element_type=F32) for g in G]
    bl = [b[g][c - 1:c, :] for g in G]
    qd = [(rows(1, g) * (B_DK ** -0.5) * jnp.exp(b[g])).astype(BF16) for g in G]
    kd = [(rows(2, g) * jnp.exp(-b[g])).astype(BF16) for g in G]
    kl = [(rows(2, g) * jnp.exp(bl[g] - b[g])).astype(BF16) for g in G]
    ebl = [jnp.exp(bl[g]) for g in G]
    for h in range(B_HEADS):
        ks = slice(h * B_DK, (h + 1) * B_DK)
        vs = slice(h * B_DV, (h + 1) * B_DV)
        vh = [rows(3, g)[:, vs] for g in G]
        st = [st_ref[g, h] for g in G]
        o = [lax.dot_general(qd[g][:, ks], st[g].astype(BF16), nt, preferred_element_type=F32) for g in G]
        att = [lax.dot_general(qd[g][:, ks], kd[g][:, ks], nt, preferred_element_type=F32) for g in G]
        att = [jnp.where(causal, att[g], 0.0).astype(BF16) for g in G]
        o = [o[g] + jnp.dot(att[g], vh[g].astype(BF16), preferred_element_type=F32) for g in G]
        for g in G:
            st_ref[g, h] = st[g] * ebl[g][:, ks] + jnp.dot(vh[g].T.astype(BF16), kl[g][:, ks],
                                                           preferred_element_type=F32)
        o = [o[g] * lax.rsqrt(jnp.mean(o[g] * o[g], axis=1, keepdims=True) + RMS_EPS) * g_ref[h:h + 1, :]
             for g in G]
        for g in G:
            gate = rows(4, g)[:, vs]
            o_ref[g, :, vs] = (o[g] * (gate / (1.0 + jnp.exp(-gate)))).astype(o_ref.dtype)

    @pl.when(ci == n_chunks - 1)
    def _():
        for g in range(group):
            for l in range(layer):
                sout_ref[l, g] = prev_ref[l, g]
            for h in range(B_HEADS):
                sout_ref[layer, g, h] = st_ref[g, h].T


def _gla(rest, s0, layer, prev_states, wa, ba, gain, *, n_batch, seq, chunk, row_base, group):
    n_chunks = seq // chunk
    contiguous = n_chunks == 1
    base = row_base // chunk

    def in_specs_for(width, col):
        if contiguous:
            return [pl.BlockSpec((group * chunk, width), lambda b, ci: (base // group + b, col // width))]
        return [pl.BlockSpec((chunk, width),
                             lambda b, ci, g=g: (base + (b * group + g) * n_chunks + ci, col // width))
                for g in range(group)]

    row_specs = [spec for width, col in GLA_INPUTS for spec in in_specs_for(width, col)]
    s0_layer = layer if s0.shape[0] > 1 else 0
    state_block = (group, B_HEADS, B_DK, B_DV)
    prev_specs = [pl.BlockSpec((layer,) + state_block, lambda b, ci: (0, b, 0, 0, 0))] if layer else []
    prev_args = [prev_states] if layer else []
    return pl.pallas_call(
        functools.partial(_gla_kernel, n_chunks=n_chunks, group=group, chunk=chunk, layer=layer,
                          n_in=len(row_specs)),
        grid=(n_batch // group, n_chunks),
        in_specs=row_specs + [
            pl.BlockSpec((1,) + state_block, lambda b, ci: (s0_layer, b, 0, 0, 0)),
            pl.BlockSpec((LANES, B_QK), lambda b, ci: (0, 0)),
            pl.BlockSpec((1, B_QK), lambda b, ci: (0, 0)),
            pl.BlockSpec((B_HEADS, B_DV), lambda b, ci: (0, 0)),
        ] + prev_specs,
        out_specs=[
            pl.BlockSpec((group, chunk, B_V), lambda b, ci: (b, ci, 0)),
            pl.BlockSpec((layer + 1,) + state_block, lambda b, ci: (0, b, 0, 0, 0)),
        ],
        out_shape=[
            jax.ShapeDtypeStruct((n_batch, seq, B_V), BF16),
            jax.ShapeDtypeStruct((layer + 1, n_batch, B_HEADS, B_DK, B_DV), F32),
        ],
        scratch_shapes=[pltpu.VMEM((group, B_HEADS, B_DV, B_DK), F32)],
        compiler_params=_params(("parallel", "arbitrary")),
        name="gla",
    )(*([rest] * len(row_specs)), s0, wa, ba, gain, *prev_args)


HALO = 16
BB_POOL = 16


def _pool_kernel(u_ref, prev_ref, w_ref, scale_ref, o_ref, *, from_start, n_seq):
    tp = u_ref.shape[0] // n_seq
    if from_start:
        ti = pl.program_id(1)
        t0 = ti * tp
    diffs = [[] for _ in POOL_WINDOWS]
    for sq in range(n_seq):
        u = u_ref[sq * tp:(sq + 1) * tp, :]
        prev = prev_ref[sq * HALO:(sq + 1) * HALO, :]
        if from_start:
            prev = jnp.where(ti > 0, prev, 0.0)
        full = jnp.concatenate([prev, u], axis=0)
        for g, w in enumerate(POOL_WINDOWS):
            cs = slice(g * C_GROUP_W, (g + 1) * C_GROUP_W)
            acc = full[:, cs]
            span = 1
            while span < w:
                acc = acc + pltpu.roll(acc, span, axis=0)
                span *= 2
            wsum = acc[HALO:, :]
            if from_start:
                t = t0 + lax.broadcasted_iota(jnp.int32, (tp, C_GROUP_W), 0)
                cnt = jnp.minimum(t + 1, w).astype(F32)
            else:
                cnt = float(w)
            diffs[g].append(wsum / cnt - u[:, cs])
    for g in range(C_GROUPS):
        cs = slice(g * C_GROUP_W, (g + 1) * C_GROUP_W)
        d = jnp.concatenate(diffs[g], axis=0) if n_seq > 1 else diffs[g][0]
        y = jnp.dot(d.astype(BF16), w_ref[g], preferred_element_type=F32) * scale_ref[:, cs]
        o_ref[:, cs] = y.astype(o_ref.dtype)


def _pool_prompt(rest, w, scale):
    nt = SEQ // TP_POOL
    ucol = R_UC // C_WIDTH

    def halo(b, i):
        return (jnp.maximum((b * SEQ + i * TP_POOL) // HALO - 1, 0), ucol)

    return pl.pallas_call(
        functools.partial(_pool_kernel, from_start=True, n_seq=1),
        grid=(BATCH, nt),
        in_specs=[
            pl.BlockSpec((TP_POOL, C_WIDTH), lambda b, i: (b * nt + i, ucol)),
            pl.BlockSpec((HALO, C_WIDTH), halo),
            pl.BlockSpec((C_GROUPS, C_GROUP_W, C_GROUP_W), lambda b, i: (0, 0, 0)),
            pl.BlockSpec((1, C_WIDTH), lambda b, i: (0, 0)),
        ],
        out_specs=pl.BlockSpec((TP_POOL, C_WIDTH), lambda b, i: (b * nt + i, 0)),
        out_shape=jax.ShapeDtypeStruct((N_PROMPT, C_WIDTH), BF16),
        compiler_params=_params(("parallel", "arbitrary")),
        name="pool_prompt",
    )(rest, rest, w, scale)


def _pool_sample(rest, prev, w, scale):
    ucol = R_UC // C_WIDTH
    rows = BB_POOL * DEC_SEQ
    base = N_PROMPT // rows
    return pl.pallas_call(
        functools.partial(_pool_kernel, from_start=False, n_seq=BB_POOL),
        grid=(DEC_BATCH // BB_POOL,),
        in_specs=[
            pl.BlockSpec((rows, C_WIDTH), lambda b: (base + b, ucol)),
            pl.BlockSpec((BB_POOL * HALO, C_WIDTH), lambda b: (b, 0)),
            pl.BlockSpec((C_GROUPS, C_GROUP_W, C_GROUP_W), lambda b: (0, 0, 0)),
            pl.BlockSpec((1, C_WIDTH), lambda b: (0, 0)),
        ],
        out_specs=pl.BlockSpec((rows, C_WIDTH), lambda b: (b, 0)),
        out_shape=jax.ShapeDtypeStruct((N_SAMPLE, C_WIDTH), BF16),
        compiler_params=_params(("parallel",)),
        name="pool_sample",
    )(rest, prev, w, scale)


def _layer_norm(x, g, b):
    mu = jnp.mean(x, axis=1, keepdims=True)
    xc = x - mu
    var = jnp.mean(xc * xc, axis=1, keepdims=True)
    return xc * lax.rsqrt(var + LN_EPS) * g + b


def _merge_kernel(x_ref, gates_ref, oa_p_ref, oa_s_ref, ob_p_ref, ob_s_ref, oc_p_ref, oc_s_ref,
                  wa_ref, wb_ref, wc_ref, wo_ref, g_ref, b_ref, o_ref, oT_ref):
    in_sample = pl.program_id(0) >= N_PROMPT // TM_MERGE

    def gate(i):
        z = gates_ref[:, i * D_MODEL:(i + 1) * D_MODEL]
        return 1.0 / (1.0 + jnp.exp(-z))

    def branch(p_ref, s_ref, w_ref):
        o = jnp.where(in_sample, s_ref[...], p_ref[...])
        return jnp.dot(o, w_ref[...], preferred_element_type=F32)

    merged = (gate(0) * branch(oa_p_ref, oa_s_ref, wa_ref)
              + gate(1) * branch(ob_p_ref, ob_s_ref, wb_ref)
              + gate(2) * branch(oc_p_ref, oc_s_ref, wc_ref))
    mix = jnp.dot(merged.astype(BF16), wo_ref[...], preferred_element_type=F32)
    y = _layer_norm(DN_ALPHA * x_ref[...] + mix, g_ref[...], b_ref[...])
    o_ref[...] = y
    oT_ref[...] = y.T.astype(BF16)


def _merge(x, rest, branches, wa, wb, wc, wo, g, b):
    n_p = N_PROMPT // TM_MERGE
    row = pl.BlockSpec((TM_MERGE, D_MODEL), lambda i: (i, 0))
    row_p = pl.BlockSpec((TM_MERGE, D_MODEL), lambda i: (jnp.minimum(i, n_p - 1), 0))
    row_s = pl.BlockSpec((TM_MERGE, D_MODEL), lambda i: (jnp.maximum(i - n_p, 0), 0))
    wspec = pl.BlockSpec((D_MODEL, D_MODEL), lambda i: (0, 0))
    vec = pl.BlockSpec((1, D_MODEL), lambda i: (0, 0))
    (oa_p, oa_s), (ob_p, ob_s), (oc_p, oc_s) = branches
    return pl.pallas_call(
        _merge_kernel,
        grid=(N_TOK // TM_MERGE,),
        in_specs=[row, pl.BlockSpec((TM_MERGE, 3 * D_MODEL), lambda i: (i, R_GATES)),
                  row_p, row_s, row_p, row_s, row_p, row_s, wspec, wspec, wspec, wspec, vec, vec],
        out_specs=[row, pl.BlockSpec((D_MODEL, TM_MERGE), lambda i: (0, i))],
        out_shape=[jax.ShapeDtypeStruct((N_TOK, D_MODEL), F32),
                   jax.ShapeDtypeStruct((D_MODEL, N_TOK), BF16)],
        compiler_params=_params(("parallel",)),
        name="merge",
    )(x, rest, oa_p, oa_s, ob_p, ob_s, oc_p, oc_s, wa, wb, wc, wo, g, b)


SUBLANES = 8


def _extract_desc(s, n):
    vals = []
    for _ in range(n):
        m = jnp.max(s, axis=0, keepdims=True)
        vals.append(m)
        s = jnp.where(s == m, -jnp.inf, s)
    return vals


def _merge_sort_pairs(n):
    pairs = []
    p = 1
    while p < n:
        k = p
        while k >= 1:
            for j in range(k % p, n - k, 2 * k):
                for i in range(min(k, n - j - k)):
                    if (i + j) // (2 * p) == (i + j + k) // (2 * p):
                        pairs.append((i + j, i + j + k))
            k //= 2
        p *= 2
    return pairs


def _top_desc(s, n):
    v = [s[SUBLANES * k:SUBLANES * (k + 1), :] for k in range(s.shape[0] // SUBLANES)]
    for a, b in _merge_sort_pairs(len(v)):
        v[a], v[b] = jnp.maximum(v[a], v[b]), jnp.minimum(v[a], v[b])
    vals = []
    for r in range(n):
        m = jnp.max(v[0], axis=0, keepdims=True)
        vals.append(m)
        hit = v[0] == m
        for k in range(n - r - 1):
            v[k] = jnp.where(hit, v[k + 1], v[k])
    return vals


def _rank_of(s, vals):
    rank = jnp.full(s.shape, float(len(vals)), F32)
    for r, val in enumerate(vals):
        rank = jnp.where(s == val, float(r), rank)
    return rank


def _peer_topk_kernel(xT_ref, wq_ref, sk_ref, cnt_ref, e1_ref, rank_ref, e2_ref, s1_scr, sv_scr):
    tt = xT_ref.shape[1]
    xT = xT_ref[...]
    for hp in range(2 * PEER_HEADS):
        h, second = divmod(hp, 2)
        qT = jnp.dot(wq_ref[hp * PEER_DHALF:(hp + 1) * PEER_DHALF, :], xT, preferred_element_type=F32)
        s = jnp.dot(sk_ref[hp], qT.astype(BF16), preferred_element_type=F32)
        vals = _top_desc(s, PEER_TOPK)
        sv_scr[hp] = jnp.concatenate(vals, axis=0)
        if second:
            rank_ref[h] = _rank_of(s, vals).astype(BF16)
            e2_ref[h] = jnp.exp(s - vals[0]).astype(BF16)
        else:
            s1_scr[h] = s
    row8 = lax.broadcasted_iota(jnp.int32, (8, tt), 0)
    for h in range(PEER_HEADS):
        sv1 = sv_scr[2 * h]
        sv2 = sv_scr[2 * h + 1]
        pieces = [sv1[0:1, :] + sv2]
        for a in range(1, 8):
            nb = PEER_TOPK // (a + 1)
            pieces.append(jnp.where(row8 < nb, sv1[a:a + 1, :] + sv2[0:8, :], -jnp.inf))
        pieces.append(sv2[0:1, :] + sv1[8:16, :])
        cand = _extract_desc(jnp.concatenate(pieces, axis=0), PEER_TOPK)
        top = cand[0]
        z = jnp.zeros_like(top)
        for r in range(PEER_TOPK):
            z = z + jnp.exp(cand[r] - top)
        kth = cand[PEER_TOPK - 1]
        counts = [jnp.sum(jnp.where(p >= kth, 1.0, 0.0), axis=0, keepdims=True) for p in pieces[:8]]
        tail = jnp.where(pieces[8] >= kth, 1.0, 0.0)
        counts += [tail[a:a + 1, :] for a in range(8)]
        s1 = s1_scr[h]
        cnt = jnp.zeros_like(s1)
        for a in range(PEER_TOPK):
            cnt = jnp.where(s1 == sv1[a:a + 1, :], counts[a], cnt)
        e1 = jnp.exp(s1 - (sv1[0:1, :] + jnp.log(z))) * SQRT_HALF
        for c in range(tt // LANES):
            cnt_ref[h, c] = cnt[:, c * LANES:(c + 1) * LANES]
            e1_ref[h, c] = e1[:, c * LANES:(c + 1) * LANES]


def _peer_topk(xT, wqT, sk):
    spec = pl.BlockSpec((PEER_HEADS, N_KEYS, TT_TOPK), lambda t: (0, 0, t))
    row_spec = pl.BlockSpec((PEER_HEADS, TT_TOPK // LANES, N_KEYS, LANES), lambda t: (0, t, 0, 0))
    wide = jax.ShapeDtypeStruct((PEER_HEADS, N_TOK // LANES, N_KEYS, LANES), F32)
    narrow = jax.ShapeDtypeStruct((PEER_HEADS, N_KEYS, N_TOK), BF16)
    return pl.pallas_call(
        _peer_topk_kernel,
        grid=(N_TOK // TT_TOPK,),
        in_specs=[
            pl.BlockSpec((D_MODEL, TT_TOPK), lambda t: (0, t)),
            pl.BlockSpec((PEER_HEADS * PEER_DKEY, D_MODEL), lambda t: (0, 0)),
            pl.BlockSpec((2 * PEER_HEADS, N_KEYS, PEER_DHALF), lambda t: (0, 0, 0)),
        ],
        out_specs=[row_spec, row_spec, spec, spec],
        out_shape=[wide, wide, narrow, narrow],
        scratch_shapes=[pltpu.VMEM((PEER_HEADS, N_KEYS, TT_TOPK), F32),
                        pltpu.VMEM((2 * PEER_HEADS, PEER_TOPK, TT_TOPK), F32)],
        compiler_params=_params(("parallel",)),
        name="peer_topk",
    )(xT, wqT, sk)


SQRT_HALF = 2.0 ** -0.5


def _gelu_unscaled(x):
    t = x * SQRT_HALF
    return t * (1.0 + lax.erf(t))


LC_PEER = 256


def _bf16_rows(ref, h, r, chunks, n_rows):
    x = jnp.concatenate([ref[h, c, pl.ds(r, 16, stride=0), :] for c in chunks], axis=1)
    packed = x.astype(BF16)
    return jnp.concatenate([packed] * (n_rows // packed.shape[0]), axis=0)


MM_PIECE = 256


def _peer_main_kernel(xT_ref, u_ref, vt_ref, cnt_ref, e1_ref, rank_ref, e2_ref, yT_ref, s_scr, wh_scr):
    i = pl.program_id(1)
    tt = xT_ref.shape[1]

    @pl.when(i == 0)
    def _():
        yT_ref[...] = jnp.zeros_like(yT_ref)

    zero = jnp.zeros((N_KEYS, LC_PEER), BF16)
    per_piece = MM_PIECE // N_KEYS
    for ii in range(IB_PEER):
        rows = slice(ii * N_KEYS, (ii + 1) * N_KEYS)
        if ii % per_piece == 0:
            piece = slice(ii * N_KEYS, ii * N_KEYS + MM_PIECE)
            s_scr[piece, :] = jnp.dot(u_ref[0, piece, :], xT_ref[...], preferred_element_type=F32)
        for lc in range(tt // LC_PEER):
            cols = slice(lc * LC_PEER, (lc + 1) * LC_PEER)
            chunks = range(lc * LC_PEER // LANES, (lc + 1) * LC_PEER // LANES)
            w = zero
            for h in range(PEER_HEADS):
                cnt = _bf16_rows(cnt_ref, h, ii, chunks, N_KEYS)
                e1 = _bf16_rows(e1_ref, h, ii, chunks, N_KEYS)
                w = w + jnp.where(rank_ref[h, :, cols] < cnt, e2_ref[h, :, cols] * e1, zero)
            wh_scr[rows, cols] = w
    for ii in range(IB_PEER):
        rows = slice(ii * N_KEYS, (ii + 1) * N_KEYS)
        wh_scr[rows, :] = wh_scr[rows, :] * _gelu_unscaled(s_scr[rows, :]).astype(BF16)
    yT_ref[...] += jnp.dot(vt_ref[0], wh_scr[...], preferred_element_type=F32)


def _peer_main(xT, u, vt, layer, cnt, e1, rank, e2):
    eb = IB_PEER * N_KEYS
    row_spec = pl.BlockSpec((PEER_HEADS, TT_PEER // LANES, IB_PEER, LANES), lambda t, i: (0, t, i, 0))
    tab_spec = pl.BlockSpec((PEER_HEADS, N_KEYS, TT_PEER), lambda t, i: (0, 0, t))
    return pl.pallas_call(
        _peer_main_kernel,
        grid=(N_TOK // TT_PEER, N_EXPERTS // eb),
        in_specs=[
            pl.BlockSpec((D_MODEL, TT_PEER), lambda t, i: (0, t)),
            pl.BlockSpec((1, eb, D_MODEL), lambda t, i: (layer, i, 0)),
            pl.BlockSpec((1, D_MODEL, eb), lambda t, i: (layer, 0, i)),
            row_spec, row_spec, tab_spec, tab_spec,
        ],
        out_specs=pl.BlockSpec((D_MODEL, TT_PEER), lambda t, i: (0, t)),
        out_shape=jax.ShapeDtypeStruct((D_MODEL, N_TOK), F32),
        scratch_shapes=[pltpu.VMEM((eb, TT_PEER), F32), pltpu.VMEM((eb, TT_PEER), BF16)],
        compiler_params=_params(("parallel", "arbitrary")),
        name="peer_main",
    )(xT, u, vt, cnt, e1, rank, e2)


def _peer_out_kernel(x_ref, yT_ref, g_ref, b_ref, o_ref, ob_ref):
    y = _layer_norm(DN_ALPHA * x_ref[...] + yT_ref[...].T, g_ref[...], b_ref[...])
    o_ref[...] = y
    ob_ref[...] = y.astype(BF16)


def _peer_out(x1, yT, g, b):
    row = pl.BlockSpec((TM_MERGE, D_MODEL), lambda i: (i, 0))
    vec = pl.BlockSpec((1, D_MODEL), lambda i: (0, 0))
    return pl.pallas_call(
        _peer_out_kernel,
        grid=(N_TOK // TM_MERGE,),
        in_specs=[row, pl.BlockSpec((D_MODEL, TM_MERGE), lambda i: (0, i)), vec, vec],
        out_specs=[row, row],
        out_shape=[jax.ShapeDtypeStruct((N_TOK, D_MODEL), F32),
                   jax.ShapeDtypeStruct((N_TOK, D_MODEL), BF16)],
        compiler_params=_params(("parallel",)),
        name="peer_out",
    )(x1, yT, g, b)


def _rope_tables():
    half = ROT_DIM // 2
    pos = jnp.concatenate([jnp.arange(SEQ), PAST_LEN + (jnp.arange(TM_QK) % DEC_SEQ)])
    inv = ROPE_THETA ** (-jnp.arange(half, dtype=F32) / half)
    ang = pos.astype(F32)[:, None] * inv[None, :]
    cos, sin = jnp.cos(ang), jnp.sin(ang)
    n = pos.shape[0]
    one = jnp.ones((n, A_HEAD_DIM - ROT_DIM), F32)
    zero = jnp.zeros((n, A_HEAD_DIM - ROT_DIM), F32)
    zh = jnp.zeros((n, half), F32)
    reps = LANES // A_HEAD_DIM
    c = jnp.tile(jnp.concatenate([cos, cos, one], 1), (1, reps))
    s1 = jnp.tile(jnp.concatenate([-sin, zh, zero], 1), (1, reps))
    s2 = jnp.tile(jnp.concatenate([zh, sin, zero], 1), (1, reps))
    return c, s1, s2


def _split_cols(w):
    cuts = [int(c) for c in np.cumsum(SPLITS)[:-1]]
    return jnp.split(w, cuts, axis=-1)


def _layer(layer, x, xb, k_state, v_state, gla_states, gla_out, pool_state, rope, w_in, b_in, sinks,
           w_alpha, b_alpha, gla_g, w_pool, pool_scale, w_a, w_b, w_c, w_out, ln1_g, ln1_b,
           peer_query, peer_subkeys, peer_u, peer_vt, ln2_g, ln2_b):
    qa_w, ka_w, va_w, qb_w, kb_w, vb_w, lr_w, gb_w, uc_w, gates_w = _split_cols(w_in)
    qa_b, ka_b, va_b, qb_b, kb_b, vb_b, lr_b, gb_b, uc_b, gates_b = _split_cols(b_in[None, :])
    lr_pad = LANES - B_GATE_RANK
    w_qk = jnp.concatenate([qa_w, ka_w], 1).astype(BF16)
    b_qk = jnp.concatenate([qa_b, ka_b], 1)
    w_rest = jnp.concatenate([gates_w, vb_w, gb_w, uc_w, qb_w, kb_w, va_w,
                              jnp.pad(lr_w, ((0, 0), (0, lr_pad)))], 1).astype(BF16)
    b_rest = jnp.concatenate([gates_b, vb_b, gb_b, uc_b, qb_b, kb_b, va_b,
                              jnp.pad(lr_b, ((0, 0), (0, lr_pad)))], 1)

    qk = _proj_qk(xb, w_qk, b_qk, *rope)
    rest = _proj_rest(xb, w_rest, b_rest)

    ks = k_state.reshape(DEC_BATCH, WINDOW, A_KV)
    vs = v_state.reshape(DEC_BATCH, WINDOW, A_KV)
    oa = (_attn_prompt(qk, rest, sinks), _attn_sample(qk, rest, sinks, ks, vs))

    wa = jnp.pad(w_alpha, ((0, lr_pad), (0, 0))).astype(BF16)
    ba = b_alpha[None, :]
    gla_p, gla_s = gla_out
    ob_p, gla_p = _gla(rest, jnp.zeros((1, BATCH, B_HEADS, B_DK, B_DV), F32), layer, gla_p, wa, ba, gla_g,
                       n_batch=BATCH, seq=SEQ, chunk=GLA_CHUNK, row_base=0, group=GLA_GROUP_PROMPT)
    ob_s, gla_s = _gla(rest, gla_states, layer, gla_s, wa, ba, gla_g, n_batch=DEC_BATCH, seq=DEC_SEQ,
                       chunk=math.gcd(DEC_SEQ, GLA_CHUNK), row_base=N_PROMPT, group=GLA_GROUP_SAMPLE)
    ob = (ob_p.reshape(N_PROMPT, B_V), ob_s.reshape(N_SAMPLE, B_V))

    wp = w_pool.astype(BF16)
    ps = pool_scale[None, :]
    prev = jnp.pad(pool_state, ((0, 0), (HALO - POOL_STATE, 0), (0, 0))).reshape(DEC_BATCH * HALO, C_WIDTH)
    oc = (_pool_prompt(rest, wp, ps), _pool_sample(rest, prev, wp, ps))

    x1, x1T = _merge(x, rest, (oa, ob, oc), w_a.astype(BF16), w_b.astype(BF16), w_c.astype(BF16),
                     w_out.astype(BF16), ln1_g[None, :], ln1_b[None, :])

    wqT = peer_query.reshape(D_MODEL, PEER_HEADS * PEER_DKEY).T.astype(BF16)
    sk = peer_subkeys.reshape(2 * PEER_HEADS, N_KEYS, PEER_DHALF).astype(BF16)
    cnt, e1, rank, e2 = _peer_topk(x1T, wqT, sk)
    yT = _peer_main(x1T, peer_u, peer_vt, layer, cnt, e1, rank, e2)
    x2, x2b = _peer_out(x1, yT, ln2_g[None, :], ln2_b[None, :])

    def prompt_tail(t, col0, width, n):
        return jnp.stack([t[(b + 1) * SEQ - n:(b + 1) * SEQ, col0:col0 + width] for b in range(BATCH)])

    def sample_tail(state, t, col0, width, n):
        new = t[N_PROMPT:, col0:col0 + width].reshape(DEC_BATCH, DEC_SEQ, width)
        return jnp.concatenate([state, new], 1)[:, -n:]

    kv_shape = (-1, WINDOW, A_KV_HEADS, A_HEAD_DIM)
    states = (prompt_tail(qk, A_Q, A_KV, WINDOW).reshape(kv_shape),
              prompt_tail(rest, R_VA, A_KV, WINDOW).reshape(kv_shape),
              prompt_tail(rest, R_UC, C_WIDTH, POOL_STATE),
              sample_tail(ks, qk, A_Q, A_KV, WINDOW).reshape(kv_shape),
              sample_tail(vs, rest, R_VA, A_KV, WINDOW).reshape(kv_shape),
              sample_tail(pool_state, rest, R_UC, C_WIDTH, POOL_STATE))
    return x2, x2b, states, (gla_p, gla_s)


def kernel(x_prompt, x_sample, state_win_k, state_win_v, state_gla, state_pool, w_in, b_in, attn_sinks,
           w_alpha, b_alpha, gla_norm_g, w_pool, pool_scale, w_branch_a, w_branch_b, w_branch_c, w_out,
           ln1_g, ln1_b, peer_query, peer_subkeys, peer_u, peer_v, ln2_g, ln2_b):
    x = jnp.concatenate([x_prompt.reshape(N_PROMPT, D_MODEL), x_sample.reshape(N_SAMPLE, D_MODEL)], 0)
    xb = x.astype(BF16)
    rope = _rope_tables()
    peer_ub = peer_u.astype(BF16)
    peer_vtb = jnp.swapaxes(peer_v, 1, 2).astype(BF16)
    per_layer = []
    gla_out = (None, None)
    for l in range(DEPTH):
        x, xb, states, gla_out = _layer(
            l, x, xb, state_win_k[l], state_win_v[l], state_gla, gla_out, state_pool[l], rope,
            w_in[l], b_in[l], attn_sinks[l], w_alpha[l], b_alpha[l], gla_norm_g[l],
            w_pool[l], pool_scale[l], w_branch_a[l], w_branch_b[l], w_branch_c[l], w_out[l],
            ln1_g[l], ln1_b[l], peer_query[l], peer_subkeys[l], peer_ub, peer_vtb, ln2_g[l], ln2_b[l])
        per_layer.append(states)
    pk, pv, pp, sk, sv, sp = [jnp.stack([per_layer[l][i] for l in range(DEPTH)]) for i in range(6)]
    return (x[:N_PROMPT].reshape(BATCH, SEQ, D_MODEL), x[N_PROMPT:].reshape(DEC_BATCH, DEC_SEQ, D_MODEL),
            pk, pv, gla_out[0], pp, sk, sv, gla_out[1], sp)
```

```python
import functools
import math

import jax
import jax.numpy as jnp
import numpy as np
from jax import lax
from jax.experimental import pallas as pl
from jax.experimental.pallas import tpu as pltpu

F32 = jnp.float32
BF16 = jnp.bfloat16

D_MODEL = 1024
BATCH = 8
SEQ = 2048
DEPTH = 2
DEC_BATCH = 128
DEC_SEQ = 8
PAST_LEN = 16384

A_HEADS = 16
A_KV_HEADS = 2
A_HEAD_DIM = 64
A_GROUP = A_HEADS // A_KV_HEADS
WINDOW = 128
ROT_DIM = A_HEAD_DIM // 4
ROPE_THETA = 500000.0
NEG_INF = -1e30
B_HEADS = 4
B_DK = D_MODEL // 2 // B_HEADS
B_DV = D_MODEL // B_HEADS
B_GATE_RANK = 16
B_TAU = 16.0
GLA_CHUNK = 64
POOL_WINDOWS = (2, 4, 8, 16)
C_GROUPS = len(POOL_WINDOWS)
C_GROUP_W = D_MODEL // C_GROUPS
C_WIDTH = C_GROUPS * C_GROUP_W
POOL_STATE = max(POOL_WINDOWS) - 1
PEER_HEADS = 8
N_KEYS = 128
N_EXPERTS = N_KEYS * N_KEYS
PEER_TOPK = 16
PEER_DKEY = 256
PEER_DHALF = PEER_DKEY // 2
DN_ALPHA = (2 * DEPTH) ** 0.25
LN_EPS = 1e-5
RMS_EPS = 1e-6

A_Q = A_HEADS * A_HEAD_DIM
A_KV = A_KV_HEADS * A_HEAD_DIM
B_QK = B_HEADS * B_DK
B_V = B_HEADS * B_DV
SPLITS = (A_Q, A_KV, A_KV, B_QK, B_QK, B_V, B_GATE_RANK, B_V, C_WIDTH, 3 * D_MODEL)

LANES = 128
N_PROMPT = BATCH * SEQ
N_SAMPLE = DEC_BATCH * DEC_SEQ
N_TOK = N_PROMPT + N_SAMPLE

R_GATES = 0
R_VB = 3 * D_MODEL
R_GB = R_VB + B_V
R_UC = R_GB + B_V
R_QB = R_UC + C_WIDTH
R_KB = R_QB + B_QK
R_VA = R_KB + B_QK
R_LR = R_VA + A_KV
R_WIDTH = R_LR + LANES
QK_WIDTH = A_Q + A_KV

VMEM_LIMIT = 48 * 1024 * 1024

TM_QK = 512
TM_REST = 512
TN_REST = R_WIDTH // 2
TM_MERGE = 256
TP_POOL = 512
GLA_GROUP_PROMPT = 2
GLA_GROUP_SAMPLE = 4
TT_TOPK = 256
TT_PEER = 512
IB_PEER = 16


def _params(sem):
    return pltpu.CompilerParams(dimension_semantics=sem, vmem_limit_bytes=VMEM_LIMIT)


def _qk_kernel(x_ref, w_ref, b_ref, c_ref, s1_ref, s2_ref, o_ref):
    y = jnp.dot(x_ref[...], w_ref[...], preferred_element_type=F32) + b_ref[...]
    c = c_ref[...]
    s1 = s1_ref[...]
    s2 = s2_ref[...]
    for j in range(QK_WIDTH // LANES):
        yj = y[:, j * LANES:(j + 1) * LANES]
        up = pltpu.roll(yj, LANES - ROT_DIM // 2, axis=1)
        dn = pltpu.roll(yj, ROT_DIM // 2, axis=1)
        o_ref[:, j * LANES:(j + 1) * LANES] = yj * c + up * s1 + dn * s2


def _proj_qk(xb, w, b, rope_c, rope_s1, rope_s2):
    n_prompt_blocks = SEQ // TM_QK

    def tab_map(i):
        return (jnp.where(i < N_PROMPT // TM_QK, i % n_prompt_blocks, n_prompt_blocks), 0)

    tab_spec = pl.BlockSpec((TM_QK, LANES), tab_map)
    return pl.pallas_call(
        _qk_kernel,
        grid=(N_TOK // TM_QK,),
        in_specs=[
            pl.BlockSpec((TM_QK, D_MODEL), lambda i: (i, 0)),
            pl.BlockSpec((D_MODEL, QK_WIDTH), lambda i: (0, 0)),
            pl.BlockSpec((1, QK_WIDTH), lambda i: (0, 0)),
            tab_spec, tab_spec, tab_spec,
        ],
        out_specs=pl.BlockSpec((TM_QK, QK_WIDTH), lambda i: (i, 0)),
        out_shape=jax.ShapeDtypeStruct((N_TOK, QK_WIDTH), F32),
        compiler_params=_params(("parallel",)),
        name="proj_qk",
    )(xb, w, b, rope_c, rope_s1, rope_s2)


def _mm_bias_kernel(x_ref, w_ref, b_ref, o_ref):
    o_ref[...] = jnp.dot(x_ref[...], w_ref[...], preferred_element_type=F32) + b_ref[...]


def _proj_rest(xb, w, b):
    return pl.pallas_call(
        _mm_bias_kernel,
        grid=(R_WIDTH // TN_REST, N_TOK // TM_REST),
        in_specs=[
            pl.BlockSpec((TM_REST, D_MODEL), lambda j, i: (i, 0)),
            pl.BlockSpec((D_MODEL, TN_REST), lambda j, i: (0, j)),
            pl.BlockSpec((1, TN_REST), lambda j, i: (0, j)),
        ],
        out_specs=pl.BlockSpec((TM_REST, TN_REST), lambda j, i: (i, j)),
        out_shape=jax.ShapeDtypeStruct((N_TOK, R_WIDTH), F32),
        compiler_params=_params(("parallel", "arbitrary")),
        name="proj_rest",
    )(xb, w, b)


HEADS_PER_PASS = 16


def _attend(q, kk, vv, sink_ref, c_min, o_ref, row0):
    tq = q.shape[0]
    r = lax.broadcasted_iota(jnp.int32, (tq, 2 * WINDOW), 0)
    c = lax.broadcasted_iota(jnp.int32, (tq, 2 * WINDOW), 1)
    ok = (c > r) & (c <= r + WINDOW) & (c >= c_min)
    qb = (q * (A_HEAD_DIM ** -0.5)).astype(BF16)
    nt = (((1,), (1,)), ((), ()))
    for h0 in range(0, A_HEADS, HEADS_PER_PASS):
        hs = range(h0, h0 + HEADS_PER_PASS)
        col = {h: slice(h * A_HEAD_DIM, (h + 1) * A_HEAD_DIM) for h in hs}
        kv = {h: slice((h // A_GROUP) * A_HEAD_DIM, (h // A_GROUP + 1) * A_HEAD_DIM) for h in hs}
        s = {h: lax.dot_general(qb[:, col[h]], kk[:, kv[h]], nt, preferred_element_type=F32) for h in hs}
        s = {h: jnp.where(ok, s[h], NEG_INF) for h in hs}
        m = {h: jnp.maximum(jnp.max(s[h], axis=1, keepdims=True), sink_ref[h]) for h in hs}
        p = {h: jnp.exp(s[h] - m[h]) for h in hs}
        denom = {h: jnp.sum(p[h], axis=1, keepdims=True) + jnp.exp(sink_ref[h] - m[h]) for h in hs}
        o = {h: jnp.dot(p[h].astype(BF16), vv[:, kv[h]], preferred_element_type=F32) / denom[h] for h in hs}
        for h in hs:
            o_ref[pl.ds(row0, tq), col[h]] = o[h].astype(o_ref.dtype)


def _attn_prompt_kernel(sink_ref, q_ref, kc_ref, kp_ref, vc_ref, vp_ref, o_ref):
    n = pl.program_id(1)
    kk = jnp.concatenate([kp_ref[...], kc_ref[...]], axis=0).astype(BF16)
    vv = jnp.concatenate([vp_ref[...], vc_ref[...]], axis=0).astype(BF16)
    _attend(q_ref[...], kk, vv, sink_ref, jnp.where(n > 0, 0, WINDOW), o_ref, 0)


def _attn_prompt(qk, rest, sinks):
    nb = SEQ // WINDOW
    kcol = A_Q // A_KV
    vcol = R_VA // A_KV

    def cur(b, n):
        return b * nb + n

    def prev(b, n):
        return b * nb + jnp.maximum(n - 1, 0)

    return pl.pallas_call(
        _attn_prompt_kernel,
        grid=(BATCH, nb),
        in_specs=[
            pl.BlockSpec(memory_space=pltpu.SMEM),
            pl.BlockSpec((WINDOW, A_Q), lambda b, n: (cur(b, n), 0)),
            pl.BlockSpec((WINDOW, A_KV), lambda b, n: (cur(b, n), kcol)),
            pl.BlockSpec((WINDOW, A_KV), lambda b, n: (prev(b, n), kcol)),
            pl.BlockSpec((WINDOW, A_KV), lambda b, n: (cur(b, n), vcol)),
            pl.BlockSpec((WINDOW, A_KV), lambda b, n: (prev(b, n), vcol)),
        ],
        out_specs=pl.BlockSpec((WINDOW, A_Q), lambda b, n: (cur(b, n), 0)),
        out_shape=jax.ShapeDtypeStruct((N_PROMPT, A_Q), BF16),
        compiler_params=_params(("parallel", "arbitrary")),
        name="attn_prompt",
    )(sinks, qk, qk, qk, rest, rest)


BB_ATTN = 8


def _attn_sample_kernel(sink_ref, q_ref, kn_ref, vn_ref, ks_ref, vs_ref, o_ref):
    pad = jnp.zeros((WINDOW - DEC_SEQ, A_KV), F32)
    rows = A_GROUP * DEC_SEQ
    t = lax.broadcasted_iota(jnp.int32, (rows, 2 * WINDOW), 0) % DEC_SEQ
    c = lax.broadcasted_iota(jnp.int32, (rows, 2 * WINDOW), 1)
    ok = (c > t) & (c <= t + WINDOW)
    nt = (((1,), (1,)), ((), ()))

    def body(pair, carry):
        chains = [(e, g) for e in range(2) for g in range(A_KV_HEADS)]
        elem = {e: pair * 2 + e for e in range(2)}
        row0 = {e: pl.multiple_of(elem[e] * DEC_SEQ, DEC_SEQ) for e in range(2)}
        q = {e: q_ref[pl.ds(row0[e], DEC_SEQ), :] * (A_HEAD_DIM ** -0.5) for e in range(2)}
        kk = {e: jnp.concatenate([ks_ref[elem[e]], kn_ref[pl.ds(row0[e], DEC_SEQ), :], pad], axis=0).astype(BF16)
              for e in range(2)}
        vv = {e: jnp.concatenate([vs_ref[elem[e]], vn_ref[pl.ds(row0[e], DEC_SEQ), :], pad], axis=0).astype(BF16)
              for e in range(2)}
        heads = {g: range(g * A_GROUP, (g + 1) * A_GROUP) for g in range(A_KV_HEADS)}
        ds = {g: slice(g * A_HEAD_DIM, (g + 1) * A_HEAD_DIM) for g in range(A_KV_HEADS)}
        sink = {g: sink_ref[g * rows:(g + 1) * rows, 0:1] for g in range(A_KV_HEADS)}
        qg = {(bb, g): jnp.concatenate([q[bb][:, h * A_HEAD_DIM:(h + 1) * A_HEAD_DIM] for h in heads[g]],
                                       axis=0).astype(BF16) for bb, g in chains}
        s = {(bb, g): lax.dot_general(qg[bb, g], kk[bb][:, ds[g]], nt, preferred_element_type=F32)
             for bb, g in chains}
        s = {ch: jnp.where(ok, s[ch], NEG_INF) for ch in chains}
        m = {(bb, g): jnp.maximum(jnp.max(s[bb, g], axis=1, keepdims=True), sink[g]) for bb, g in chains}
        p = {ch: jnp.exp(s[ch] - m[ch]) for ch in chains}
        denom = {(bb, g): jnp.sum(p[bb, g], axis=1, keepdims=True) + jnp.exp(sink[g] - m[bb, g])
                 for bb, g in chains}
        o = {(bb, g): jnp.dot(p[bb, g].astype(BF16), vv[bb][:, ds[g]], preferred_element_type=F32) / denom[bb, g]
             for bb, g in chains}
        for bb, g in chains:
            for k, h in enumerate(heads[g]):
                o_ref[pl.ds(row0[bb], DEC_SEQ), h * A_HEAD_DIM:(h + 1) * A_HEAD_DIM] = (
                    o[bb, g][k * DEC_SEQ:(k + 1) * DEC_SEQ, :].astype(o_ref.dtype))
        return carry

    lax.fori_loop(0, BB_ATTN // 2, body, 0)


def _attn_sample(qk, rest, sinks, k_state, v_state):
    rows = BB_ATTN * DEC_SEQ
    base = N_PROMPT // rows
    kcol = A_Q // A_KV
    vcol = R_VA // A_KV
    sink_rows = jnp.broadcast_to(jnp.repeat(sinks, DEC_SEQ)[:, None], (A_HEADS * DEC_SEQ, LANES))
    return pl.pallas_call(
        _attn_sample_kernel,
        grid=(DEC_BATCH // BB_ATTN,),
        in_specs=[
            pl.BlockSpec((A_HEADS * DEC_SEQ, LANES), lambda i: (0, 0)),
            pl.BlockSpec((rows, A_Q), lambda i: (base + i, 0)),
            pl.BlockSpec((rows, A_KV), lambda i: (base + i, kcol)),
            pl.BlockSpec((rows, A_KV), lambda i: (base + i, vcol)),
            pl.BlockSpec((BB_ATTN, WINDOW, A_KV), lambda i: (i, 0, 0)),
            pl.BlockSpec((BB_ATTN, WINDOW, A_KV), lambda i: (i, 0, 0)),
        ],
        out_specs=pl.BlockSpec((rows, A_Q), lambda i: (i, 0)),
        out_shape=jax.ShapeDtypeStruct((N_SAMPLE, A_Q), BF16),
        compiler_params=_params(("parallel",)),
        name="attn_sample",
    )(sink_rows, qk, qk, rest, k_state, v_state)


def _split3(x):
    hi = x.astype(BF16)
    r1 = x - hi.astype(F32)
    mid = r1.astype(BF16)
    lo = (r1 - mid.astype(F32)).astype(BF16)
    return hi, mid, lo


GLA_INPUTS = ((LANES, R_LR), (B_QK, R_QB), (B_QK, R_KB), (B_V, R_VB), (B_V, R_GB))


def _gla_kernel(*refs, n_chunks, group, chunk, layer, n_in):
    ins, rest_refs = refs[:n_in], refs[n_in:]
    per = n_in // len(GLA_INPUTS)
    s0_ref, wa_ref, ba_ref, g_ref = rest_refs[:4]
    prev_ref = rest_refs[4] if layer else None
    o_ref, sout_ref, st_ref = rest_refs[-3:]
    ci = pl.program_id(1)
    c = chunk

    def rows(inp, g):
        if per == 1:
            return ins[inp][g * c:(g + 1) * c, :]
        return ins[inp * per + g][...]

    @pl.when(ci == 0)
    def _():
        for g in range(group):
            for h in range(B_HEADS):
                st_ref[g, h] = s0_ref[0, g, h].T

    ri = lax.broadcasted_iota(jnp.int32, (c, c), 0)
    cj = lax.broadcasted_iota(jnp.int32, (c, c), 1)
    causal = cj <= ri
    tri = jnp.where(causal, 1.0, 0.0).astype(BF16)
    nt = (((1,), (1,)), ((), ()))
    G = range(group)
    z = [jnp.dot(rows(0, g).astype(BF16), wa_ref[...], preferred_element_type=F32) + ba_ref[...] for g in G]
    log_a = [-(jnp.maximum(-z[g], 0.0) + jnp.log1p(jnp.exp(-jnp.abs(z[g])))) / B_TAU for g in G]
    parts = [_split3(log_a[g]) for g in G]
    b = [jnp.dot(tri, parts[g][0], preferred_element_type=F32)
         + jnp.dot(tri, parts[g][1], preferred_element_type=F32)
         + jnp.dot(tri, parts[g][2], preferred_element_type=F32) for g in G]
    bl = [b[g][c - 1:c, :] for g in G]
    qd = [(rows(1, g) * (B_DK ** -0.5) * jnp.exp(b[g])).astype(BF16) for g in G]
    kd = [(rows(2, g) * jnp.exp(-b[g])).astype(BF16) for g in G]
    kl = [(rows(2, g) * jnp.exp(bl[g] - b[g])).astype(BF16) for g in G]
    ebl = [jnp.exp(bl[g]) for g in G]
    for h in range(B_HEADS):
        ks = slice(h * B_DK, (h + 1) * B_DK)
        vs = slice(h * B_DV, (h + 1) * B_DV)
        vh = [rows(3, g)[:, vs] for g in G]
        st = [st_ref[g, h] for g in G]
        o = [lax.dot_general(qd[g][:, ks], st[g].astype(BF16), nt, preferred_element_type=F32) for g in G]
        att = [lax.dot_general(qd[g][:, ks], kd[g][:, ks], nt, preferred_element_type=F32) for g in G]
        att = [jnp.where(causal, att[g], 0.0).astype(BF16) for g in G]
        o = [o[g] + jnp.dot(att[g], vh[g].astype(BF16), preferred_element_type=F32) for g in G]
        for g in G:
            st_ref[g, h] = st[g] * ebl[g][:, ks] + jnp.dot(vh[g].T.astype(BF16), kl[g][:, ks],
                                                           preferred_element_type=F32)
        o = [o[g] * lax.rsqrt(jnp.mean(o[g] * o[g], axis=1, keepdims=True) + RMS_EPS) * g_ref[h:h + 1, :]
             for g in G]
        for g in G:
            gate = rows(4, g)[:, vs]
            o_ref[g, :, vs] = (o[g] * (gate / (1.0 + jnp.exp(-gate)))).astype(o_ref.dtype)

    @pl.when(ci == n_chunks - 1)
    def _():
        for g in range(group):
            for l in range(layer):
                sout_ref[l, g] = prev_ref[l, g]
            for h in range(B_HEADS):
                sout_ref[layer, g, h] = st_ref[g, h].T


def _gla(rest, s0, layer, prev_states, wa, ba, gain, *, n_batch, seq, chunk, row_base, group):
    n_chunks = seq // chunk
    contiguous = n_chunks == 1
    base = row_base // chunk

    def in_specs_for(width, col):
        if contiguous:
            return [pl.BlockSpec((group * chunk, width), lambda b, ci: (base // group + b, col // width))]
        return [pl.BlockSpec((chunk, width),
                             lambda b, ci, g=g: (base + (b * group + g) * n_chunks + ci, col // width))
                for g in range(group)]

    row_specs = [spec for width, col in GLA_INPUTS for spec in in_specs_for(width, col)]
    s0_layer = layer if s0.shape[0] > 1 else 0
    state_block = (group, B_HEADS, B_DK, B_DV)
    prev_specs = [pl.BlockSpec((layer,) + state_block, lambda b, ci: (0, b, 0, 0, 0))] if layer else []
    prev_args = [prev_states] if layer else []
    return pl.pallas_call(
        functools.partial(_gla_kernel, n_chunks=n_chunks, group=group, chunk=chunk, layer=layer,
                          n_in=len(row_specs)),
        grid=(n_batch // group, n_chunks),
        in_specs=row_specs + [
            pl.BlockSpec((1,) + state_block, lambda b, ci: (s0_layer, b, 0, 0, 0)),
            pl.BlockSpec((LANES, B_QK), lambda b, ci: (0, 0)),
            pl.BlockSpec((1, B_QK), lambda b, ci: (0, 0)),
            pl.BlockSpec((B_HEADS, B_DV), lambda b, ci: (0, 0)),
        ] + prev_specs,
        out_specs=[
            pl.BlockSpec((group, chunk, B_V), lambda b, ci: (b, ci, 0)),
            pl.BlockSpec((layer + 1,) + state_block, lambda b, ci: (0, b, 0, 0, 0)),
        ],
        out_shape=[
            jax.ShapeDtypeStruct((n_batch, seq, B_V), BF16),
            jax.ShapeDtypeStruct((layer + 1, n_batch, B_HEADS, B_DK, B_DV), F32),
        ],
        scratch_shapes=[pltpu.VMEM((group, B_HEADS, B_DV, B_DK), F32)],
        compiler_params=_params(("parallel", "arbitrary")),
        name="gla",
    )(*([rest] * len(row_specs)), s0, wa, ba, gain, *prev_args)


HALO = 16
BB_POOL = 16


def _pool_kernel(u_ref, prev_ref, w_ref, scale_ref, o_ref, *, from_start, n_seq):
    tp = u_ref.shape[0] // n_seq
    if from_start:
        ti = pl.program_id(1)
        t0 = ti * tp
    diffs = [[] for _ in POOL_WINDOWS]
    for sq in range(n_seq):
        u = u_ref[sq * tp:(sq + 1) * tp, :]
        prev = prev_ref[sq * HALO:(sq + 1) * HALO, :]
        if from_start:
            prev = jnp.where(ti > 0, prev, 0.0)
        full = jnp.concatenate([prev, u], axis=0)
        for g, w in enumerate(POOL_WINDOWS):
            cs = slice(g * C_GROUP_W, (g + 1) * C_GROUP_W)
            acc = full[:, cs]
            span = 1
            while span < w:
                acc = acc + pltpu.roll(acc, span, axis=0)
                span *= 2
            wsum = acc[HALO:, :]
            if from_start:
                t = t0 + lax.broadcasted_iota(jnp.int32, (tp, C_GROUP_W), 0)
                cnt = jnp.minimum(t + 1, w).astype(F32)
            else:
                cnt = float(w)
            diffs[g].append(wsum / cnt - u[:, cs])
    for g in range(C_GROUPS):
        cs = slice(g * C_GROUP_W, (g + 1) * C_GROUP_W)
        d = jnp.concatenate(diffs[g], axis=0) if n_seq > 1 else diffs[g][0]
        y = jnp.dot(d.astype(BF16), w_ref[g], preferred_element_type=F32) * scale_ref[:, cs]
        o_ref[:, cs] = y.astype(o_ref.dtype)


def _pool_prompt(rest, w, scale):
    nt = SEQ // TP_POOL
    ucol = R_UC // C_WIDTH

    def halo(b, i):
        return (jnp.maximum((b * SEQ + i * TP_POOL) // HALO - 1, 0), ucol)

    return pl.pallas_call(
        functools.partial(_pool_kernel, from_start=True, n_seq=1),
        grid=(BATCH, nt),
        in_specs=[
            pl.BlockSpec((TP_POOL, C_WIDTH), lambda b, i: (b * nt + i, ucol)),
            pl.BlockSpec((HALO, C_WIDTH), halo),
            pl.BlockSpec((C_GROUPS, C_GROUP_W, C_GROUP_W), lambda b, i: (0, 0, 0)),
            pl.BlockSpec((1, C_WIDTH), lambda b, i: (0, 0)),
        ],
        out_specs=pl.BlockSpec((TP_POOL, C_WIDTH), lambda b, i: (b * nt + i, 0)),
        out_shape=jax.ShapeDtypeStruct((N_PROMPT, C_WIDTH), BF16),
        compiler_params=_params(("parallel", "arbitrary")),
        name="pool_prompt",
    )(rest, rest, w, scale)


def _pool_sample(rest, prev, w, scale):
    ucol = R_UC // C_WIDTH
    rows = BB_POOL * DEC_SEQ
    base = N_PROMPT // rows
    return pl.pallas_call(
        functools.partial(_pool_kernel, from_start=False, n_seq=BB_POOL),
        grid=(DEC_BATCH // BB_POOL,),
        in_specs=[
            pl.BlockSpec((rows, C_WIDTH), lambda b: (base + b, ucol)),
            pl.BlockSpec((BB_POOL * HALO, C_WIDTH), lambda b: (b, 0)),
            pl.BlockSpec((C_GROUPS, C_GROUP_W, C_GROUP_W), lambda b: (0, 0, 0)),
            pl.BlockSpec((1, C_WIDTH), lambda b: (0, 0)),
        ],
        out_specs=pl.BlockSpec((rows, C_WIDTH), lambda b: (b, 0)),
        out_shape=jax.ShapeDtypeStruct((N_SAMPLE, C_WIDTH), BF16),
        compiler_params=_params(("parallel",)),
        name="pool_sample",
    )(rest, prev, w, scale)


def _layer_norm(x, g, b):
    mu = jnp.mean(x, axis=1, keepdims=True)
    xc = x - mu
    var = jnp.mean(xc * xc, axis=1, keepdims=True)
    return xc * lax.rsqrt(var + LN_EPS) * g + b


def _merge_kernel(x_ref, gates_ref, oa_p_ref, oa_s_ref, ob_p_ref, ob_s_ref, oc_p_ref, oc_s_ref,
                  wa_ref, wb_ref, wc_ref, wo_ref, g_ref, b_ref, o_ref, oT_ref):
    in_sample = pl.program_id(0) >= N_PROMPT // TM_MERGE

    def gate(i):
        z = gates_ref[:, i * D_MODEL:(i + 1) * D_MODEL]
        return 1.0 / (1.0 + jnp.exp(-z))

    def branch(p_ref, s_ref, w_ref):
        o = jnp.where(in_sample, s_ref[...], p_ref[...])
        return jnp.dot(o, w_ref[...], preferred_element_type=F32)

    merged = (gate(0) * branch(oa_p_ref, oa_s_ref, wa_ref)
              + gate(1) * branch(ob_p_ref, ob_s_ref, wb_ref)
              + gate(2) * branch(oc_p_ref, oc_s_ref, wc_ref))
    mix = jnp.dot(merged.astype(BF16), wo_ref[...], preferred_element_type=F32)
    y = _layer_norm(DN_ALPHA * x_ref[...] + mix, g_ref[...], b_ref[...])
    o_ref[...] = y
    oT_ref[...] = y.T.astype(BF16)


def _merge(x, rest, branches, wa, wb, wc, wo, g, b):
    n_p = N_PROMPT // TM_MERGE
    row = pl.BlockSpec((TM_MERGE, D_MODEL), lambda i: (i, 0))
    row_p = pl.BlockSpec((TM_MERGE, D_MODEL), lambda i: (jnp.minimum(i, n_p - 1), 0))
    row_s = pl.BlockSpec((TM_MERGE, D_MODEL), lambda i: (jnp.maximum(i - n_p, 0), 0))
    wspec = pl.BlockSpec((D_MODEL, D_MODEL), lambda i: (0, 0))
    vec = pl.BlockSpec((1, D_MODEL), lambda i: (0, 0))
    (oa_p, oa_s), (ob_p, ob_s), (oc_p, oc_s) = branches
    return pl.pallas_call(
        _merge_kernel,
        grid=(N_TOK // TM_MERGE,),
        in_specs=[row, pl.BlockSpec((TM_MERGE, 3 * D_MODEL), lambda i: (i, R_GATES)),
                  row_p, row_s, row_p, row_s, row_p, row_s, wspec, wspec, wspec, wspec, vec, vec],
        out_specs=[row, pl.BlockSpec((D_MODEL, TM_MERGE), lambda i: (0, i))],
        out_shape=[jax.ShapeDtypeStruct((N_TOK, D_MODEL), F32),
                   jax.ShapeDtypeStruct((D_MODEL, N_TOK), BF16)],
        compiler_params=_params(("parallel",)),
        name="merge",
    )(x, rest, oa_p, oa_s, ob_p, ob_s, oc_p, oc_s, wa, wb, wc, wo, g, b)


SUBLANES = 8


def _extract_desc(s, n):
    vals = []
    for _ in range(n):
        m = jnp.max(s, axis=0, keepdims=True)
        vals.append(m)
        s = jnp.where(s == m, -jnp.inf, s)
    return vals


def _merge_sort_pairs(n):
    pairs = []
    p = 1
    while p < n:
        k = p
        while k >= 1:
            for j in range(k % p, n - k, 2 * k):
                for i in range(min(k, n - j - k)):
                    if (i + j) // (2 * p) == (i + j + k) // (2 * p):
                        pairs.append((i + j, i + j + k))
            k //= 2
        p *= 2
    return pairs


def _top_desc(s, n):
    v = [s[SUBLANES * k:SUBLANES * (k + 1), :] for k in range(s.shape[0] // SUBLANES)]
    for a, b in _merge_sort_pairs(len(v)):
        v[a], v[b] = jnp.maximum(v[a], v[b]), jnp.minimum(v[a], v[b])
    vals = []
    for r in range(n):
        m = jnp.max(v[0], axis=0, keepdims=True)
        vals.append(m)
        hit = v[0] == m
        for k in range(n - r - 1):
            v[k] = jnp.where(hit, v[k + 1], v[k])
    return vals


def _rank_of(s, vals):
    rank = jnp.full(s.shape, float(len(vals)), F32)
    for r, val in enumerate(vals):
        rank = jnp.where(s == val, float(r), rank)
    return rank


def _peer_topk_kernel(xT_ref, wq_ref, sk_ref, cnt_ref, e1_ref, rank_ref, e2_ref, s1_scr, sv_scr):
    tt = xT_ref.shape[1]
    xT = xT_ref[...]
    for hp in range(2 * PEER_HEADS):
        h, second = divmod(hp, 2)
        qT = jnp.dot(wq_ref[hp * PEER_DHALF:(hp + 1) * PEER_DHALF, :], xT, preferred_element_type=F32)
        s = jnp.dot(sk_ref[hp], qT.astype(BF16), preferred_element_type=F32)
        vals = _top_desc(s, PEER_TOPK)
        sv_scr[hp] = jnp.concatenate(vals, axis=0)
        if second:
            rank_ref[h] = _rank_of(s, vals).astype(BF16)
            e2_ref[h] = jnp.exp(s - vals[0]).astype(BF16)
        else:
            s1_scr[h] = s
    row8 = lax.broadcasted_iota(jnp.int32, (8, tt), 0)
    for h in range(PEER_HEADS):
        sv1 = sv_scr[2 * h]
        sv2 = sv_scr[2 * h + 1]
        pieces = [sv1[0:1, :] + sv2]
        for a in range(1, 8):
            nb = PEER_TOPK // (a + 1)
            pieces.append(jnp.where(row8 < nb, sv1[a:a + 1, :] + sv2[0:8, :], -jnp.inf))
        pieces.append(sv2[0:1, :] + sv1[8:16, :])
        cand = _extract_desc(jnp.concatenate(pieces, axis=0), PEER_TOPK)
        top = cand[0]
        z = jnp.zeros_like(top)
        for r in range(PEER_TOPK):
            z = z + jnp.exp(cand[r] - top)
        kth = cand[PEER_TOPK - 1]
        counts = [jnp.sum(jnp.where(p >= kth, 1.0, 0.0), axis=0, keepdims=True) for p in pieces[:8]]
        tail = jnp.where(pieces[8] >= kth, 1.0, 0.0)
        counts += [tail[a:a + 1, :] for a in range(8)]
        s1 = s1_scr[h]
        cnt = jnp.zeros_like(s1)
        for a in range(PEER_TOPK):
            cnt = jnp.where(s1 == sv1[a:a + 1, :], counts[a], cnt)
        e1 = jnp.exp(s1 - (sv1[0:1, :] + jnp.log(z))) * SQRT_HALF
        for c in range(tt // LANES):
            cnt_ref[h, c] = cnt[:, c * LANES:(c + 1) * LANES]
            e1_ref[h, c] = e1[:, c * LANES:(c + 1) * LANES]


def _peer_topk(xT, wqT, sk):
    spec = pl.BlockSpec((PEER_HEADS, N_KEYS, TT_TOPK), lambda t: (0, 0, t))
    row_spec = pl.BlockSpec((PEER_HEADS, TT_TOPK // LANES, N_KEYS, LANES), lambda t: (0, t, 0, 0))
    wide = jax.ShapeDtypeStruct((PEER_HEADS, N_TOK // LANES, N_KEYS, LANES), F32)
    narrow = jax.ShapeDtypeStruct((PEER_HEADS, N_KEYS, N_TOK), BF16)
    return pl.pallas_call(
        _peer_topk_kernel,
        grid=(N_TOK // TT_TOPK,),
        in_specs=[
            pl.BlockSpec((D_MODEL, TT_TOPK), lambda t: (0, t)),
            pl.BlockSpec((PEER_HEADS * PEER_DKEY, D_MODEL), lambda t: (0, 0)),
            pl.BlockSpec((2 * PEER_HEADS, N_KEYS, PEER_DHALF), lambda t: (0, 0, 0)),
        ],
        out_specs=[row_spec, row_spec, spec, spec],
        out_shape=[wide, wide, narrow, narrow],
        scratch_shapes=[pltpu.VMEM((PEER_HEADS, N_KEYS, TT_TOPK), F32),
                        pltpu.VMEM((2 * PEER_HEADS, PEER_TOPK, TT_TOPK), F32)],
        compiler_params=_params(("parallel",)),
        name="peer_topk",
    )(xT, wqT, sk)


SQRT_HALF = 2.0 ** -0.5


def _gelu_unscaled(x):
    t = x * SQRT_HALF
    return t * (1.0 + lax.erf(t))


LC_PEER = 256


def _bf16_rows(ref, h, r, chunks, n_rows):
    x = jnp.concatenate([ref[h, c, pl.ds(r, 16, stride=0), :] for c in chunks], axis=1)
    packed = x.astype(BF16)
    return jnp.concatenate([packed] * (n_rows // packed.shape[0]), axis=0)


MM_PIECE = 256


def _peer_main_kernel(xT_ref, u_ref, vt_ref, cnt_ref, e1_ref, rank_ref, e2_ref, x_ref, g_ref, b_ref,
                      o_ref, ob_ref, yT_ref, s_scr, wh_scr):
    i = pl.program_id(1)
    tt = xT_ref.shape[1]

    @pl.when(i == 0)
    def _():
        yT_ref[...] = jnp.zeros_like(yT_ref)

    zero = jnp.zeros((N_KEYS, LC_PEER), BF16)
    per_piece = MM_PIECE // N_KEYS
    for ii in range(IB_PEER):
        rows = slice(ii * N_KEYS, (ii + 1) * N_KEYS)
        if ii % per_piece == 0:
            piece = slice(ii * N_KEYS, ii * N_KEYS + MM_PIECE)
            s_scr[piece, :] = jnp.dot(u_ref[0, piece, :], xT_ref[...], preferred_element_type=F32)
        for lc in range(tt // LC_PEER):
            cols = slice(lc * LC_PEER, (lc + 1) * LC_PEER)
            chunks = range(lc * LC_PEER // LANES, (lc + 1) * LC_PEER // LANES)
            w = zero
            for h in range(PEER_HEADS):
                cnt = _bf16_rows(cnt_ref, h, ii, chunks, N_KEYS)
                e1 = _bf16_rows(e1_ref, h, ii, chunks, N_KEYS)
                w = w + jnp.where(rank_ref[h, :, cols] < cnt, e2_ref[h, :, cols] * e1, zero)
            wh_scr[rows, cols] = w
    for ii in range(IB_PEER):
        rows = slice(ii * N_KEYS, (ii + 1) * N_KEYS)
        wh_scr[rows, :] = wh_scr[rows, :] * _gelu_unscaled(s_scr[rows, :]).astype(BF16)
    yT_ref[...] += jnp.dot(vt_ref[0], wh_scr[...], preferred_element_type=F32)

    @pl.when(i == pl.num_programs(1) - 1)
    def _():
        y = _layer_norm(DN_ALPHA * x_ref[...] + yT_ref[...].T, g_ref[...], b_ref[...])
        o_ref[...] = y
        ob_ref[...] = y.astype(BF16)


def _peer_main(xT, u, vt, layer, cnt, e1, rank, e2, x1, g, b):
    eb = IB_PEER * N_KEYS
    row_spec = pl.BlockSpec((PEER_HEADS, TT_PEER // LANES, IB_PEER, LANES), lambda t, i: (0, t, i, 0))
    tab_spec = pl.BlockSpec((PEER_HEADS, N_KEYS, TT_PEER), lambda t, i: (0, 0, t))
    tok_spec = pl.BlockSpec((TT_PEER, D_MODEL), lambda t, i: (t, 0))
    vec = pl.BlockSpec((1, D_MODEL), lambda t, i: (0, 0))
    return pl.pallas_call(
        _peer_main_kernel,
        grid=(N_TOK // TT_PEER, N_EXPERTS // eb),
        in_specs=[
            pl.BlockSpec((D_MODEL, TT_PEER), lambda t, i: (0, t)),
            pl.BlockSpec((1, eb, D_MODEL), lambda t, i: (layer, i, 0)),
            pl.BlockSpec((1, D_MODEL, eb), lambda t, i: (layer, 0, i)),
            row_spec, row_spec, tab_spec, tab_spec, tok_spec, vec, vec,
        ],
        out_specs=[tok_spec, tok_spec],
        out_shape=[jax.ShapeDtypeStruct((N_TOK, D_MODEL), F32),
                   jax.ShapeDtypeStruct((N_TOK, D_MODEL), BF16)],
        scratch_shapes=[pltpu.VMEM((D_MODEL, TT_PEER), F32),
                        pltpu.VMEM((eb, TT_PEER), F32), pltpu.VMEM((eb, TT_PEER), BF16)],
        compiler_params=_params(("parallel", "arbitrary")),
        name="peer_main",
    )(xT, u, vt, cnt, e1, rank, e2, x1, g, b)


def _rope_tables():
    half = ROT_DIM // 2
    pos = jnp.concatenate([jnp.arange(SEQ), PAST_LEN + (jnp.arange(TM_QK) % DEC_SEQ)])
    inv = ROPE_THETA ** (-jnp.arange(half, dtype=F32) / half)
    ang = pos.astype(F32)[:, None] * inv[None, :]
    cos, sin = jnp.cos(ang), jnp.sin(ang)
    n = pos.shape[0]
    one = jnp.ones((n, A_HEAD_DIM - ROT_DIM), F32)
    zero = jnp.zeros((n, A_HEAD_DIM - ROT_DIM), F32)
    zh = jnp.zeros((n, half), F32)
    reps = LANES // A_HEAD_DIM
    c = jnp.tile(jnp.concatenate([cos, cos, one], 1), (1, reps))
    s1 = jnp.tile(jnp.concatenate([-sin, zh, zero], 1), (1, reps))
    s2 = jnp.tile(jnp.concatenate([zh, sin, zero], 1), (1, reps))
    return c, s1, s2


def _split_cols(w):
    cuts = [int(c) for c in np.cumsum(SPLITS)[:-1]]
    return jnp.split(w, cuts, axis=-1)


def _layer(layer, x, xb, k_state, v_state, gla_states, gla_out, pool_state, rope, w_in, b_in, sinks,
           w_alpha, b_alpha, gla_g, w_pool, pool_scale, w_a, w_b, w_c, w_out, ln1_g, ln1_b,
           peer_query, peer_subkeys, peer_u, peer_vt, ln2_g, ln2_b):
    qa_w, ka_w, va_w, qb_w, kb_w, vb_w, lr_w, gb_w, uc_w, gates_w = _split_cols(w_in)
    qa_b, ka_b, va_b, qb_b, kb_b, vb_b, lr_b, gb_b, uc_b, gates_b = _split_cols(b_in[None, :])
    lr_pad = LANES - B_GATE_RANK
    w_qk = jnp.concatenate([qa_w, ka_w], 1).astype(BF16)
    b_qk = jnp.concatenate([qa_b, ka_b], 1)
    w_rest = jnp.concatenate([gates_w, vb_w, gb_w, uc_w, qb_w, kb_w, va_w,
                              jnp.pad(lr_w, ((0, 0), (0, lr_pad)))], 1).astype(BF16)
    b_rest = jnp.concatenate([gates_b, vb_b, gb_b, uc_b, qb_b, kb_b, va_b,
                              jnp.pad(lr_b, ((0, 0), (0, lr_pad)))], 1)

    qk = _proj_qk(xb, w_qk, b_qk, *rope)
    rest = _proj_rest(xb, w_rest, b_rest)

    ks = k_state.reshape(DEC_BATCH, WINDOW, A_KV)
    vs = v_state.reshape(DEC_BATCH, WINDOW, A_KV)
    oa = (_attn_prompt(qk, rest, sinks), _attn_sample(qk, rest, sinks, ks, vs))

    wa = jnp.pad(w_alpha, ((0, lr_pad), (0, 0))).astype(BF16)
    ba = b_alpha[None, :]
    gla_p, gla_s = gla_out
    ob_p, gla_p = _gla(rest, jnp.zeros((1, BATCH, B_HEADS, B_DK, B_DV), F32), layer, gla_p, wa, ba, gla_g,
                       n_batch=BATCH, seq=SEQ, chunk=GLA_CHUNK, row_base=0, group=GLA_GROUP_PROMPT)
    ob_s, gla_s = _gla(rest, gla_states, layer, gla_s, wa, ba, gla_g, n_batch=DEC_BATCH, seq=DEC_SEQ,
                       chunk=math.gcd(DEC_SEQ, GLA_CHUNK), row_base=N_PROMPT, group=GLA_GROUP_SAMPLE)
    ob = (ob_p.reshape(N_PROMPT, B_V), ob_s.reshape(N_SAMPLE, B_V))

    wp = w_pool.astype(BF16)
    ps = pool_scale[None, :]
    prev = jnp.pad(pool_state, ((0, 0), (HALO - POOL_STATE, 0), (0, 0))).reshape(DEC_BATCH * HALO, C_WIDTH)
    oc = (_pool_prompt(rest, wp, ps), _pool_sample(rest, prev, wp, ps))

    x1, x1T = _merge(x, rest, (oa, ob, oc), w_a.astype(BF16), w_b.astype(BF16), w_c.astype(BF16),
                     w_out.astype(BF16), ln1_g[None, :], ln1_b[None, :])

    wqT = peer_query.reshape(D_MODEL, PEER_HEADS * PEER_DKEY).T.astype(BF16)
    sk = peer_subkeys.reshape(2 * PEER_HEADS, N_KEYS, PEER_DHALF).astype(BF16)
    cnt, e1, rank, e2 = _peer_topk(x1T, wqT, sk)
    x2, x2b = _peer_main(x1T, peer_u, peer_vt, layer, cnt, e1, rank, e2, x1, ln2_g[None, :], ln2_b[None, :])

    def prompt_tail(t, col0, width, n):
        return jnp.stack([t[(b + 1) * SEQ - n:(b + 1) * SEQ, col0:col0 + width] for b in range(BATCH)])

    def sample_tail(state, t, col0, width, n):
        new = t[N_PROMPT:, col0:col0 + width].reshape(DEC_BATCH, DEC_SEQ, width)
        return jnp.concatenate([state, new], 1)[:, -n:]

    kv_shape = (-1, WINDOW, A_KV_HEADS, A_HEAD_DIM)
    states = (prompt_tail(qk, A_Q, A_KV, WINDOW).reshape(kv_shape),
              prompt_tail(rest, R_VA, A_KV, WINDOW).reshape(kv_shape),
              prompt_tail(rest, R_UC, C_WIDTH, POOL_STATE),
              sample_tail(ks, qk, A_Q, A_KV, WINDOW).reshape(kv_shape),
              sample_tail(vs, rest, R_VA, A_KV, WINDOW).reshape(kv_shape),
              sample_tail(pool_state, rest, R_UC, C_WIDTH, POOL_STATE))
    return x2, x2b, states, (gla_p, gla_s)


def kernel(x_prompt, x_sample, state_win_k, state_win_v, state_gla, state_pool, w_in, b_in, attn_sinks,
           w_alpha, b_alpha, gla_norm_g, w_pool, pool_scale, w_branch_a, w_branch_b, w_branch_c, w_out,
           ln1_g, ln1_b, peer_query, peer_subkeys, peer_u, peer_v, ln2_g, ln2_b):
    x = jnp.concatenate([x_prompt.reshape(N_PROMPT, D_MODEL), x_sample.reshape(N_SAMPLE, D_MODEL)], 0)
    xb = x.astype(BF16)
    rope = _rope_tables()
    peer_ub = peer_u.astype(BF16)
    peer_vtb = jnp.swapaxes(peer_v, 1, 2).astype(BF16)
    per_layer = []
    gla_out = (None, None)
    for l in range(DEPTH):
        x, xb, states, gla_out = _layer(
            l, x, xb, state_win_k[l], state_win_v[l], state_gla, gla_out, state_pool[l], rope,
            w_in[l], b_in[l], attn_sinks[l], w_alpha[l], b_alpha[l], gla_norm_g[l],
            w_pool[l], pool_scale[l], w_branch_a[l], w_branch_b[l], w_branch_c[l], w_out[l],
            ln1_g[l], ln1_b[l], peer_query[l], peer_subkeys[l], peer_ub, peer_vtb, ln2_g[l], ln2_b[l])
        per_layer.append(states)
    pk, pv, pp, sk, sv, sp = [jnp.stack([per_layer[l][i] for l in range(DEPTH)]) for i in range(6)]
    return (x[:N_PROMPT].reshape(BATCH, SEQ, D_MODEL), x[N_PROMPT:].reshape(DEC_BATCH, DEC_SEQ, D_MODEL),
            pk, pv, gla_out[0], pp, sk, sv, gla_out[1], sp)
```

```python
import functools
import math

import jax
import jax.numpy as jnp
import numpy as np
from jax import lax
from jax.experimental import pallas as pl
from jax.experimental.pallas import tpu as pltpu

F32 = jnp.float32
BF16 = jnp.bfloat16

D_MODEL = 1024
BATCH = 8
SEQ = 2048
DEPTH = 2
DEC_BATCH = 128
DEC_SEQ = 8
PAST_LEN = 16384

A_HEADS = 16
A_KV_HEADS = 2
A_HEAD_DIM = 64
A_GROUP = A_HEADS // A_KV_HEADS
WINDOW = 128
ROT_DIM = A_HEAD_DIM // 4
ROPE_THETA = 500000.0
NEG_INF = -1e30
B_HEADS = 4
B_DK = D_MODEL // 2 // B_HEADS
B_DV = D_MODEL // B_HEADS
B_GATE_RANK = 16
B_TAU = 16.0
GLA_CHUNK = 64
POOL_WINDOWS = (2, 4, 8, 16)
C_GROUPS = len(POOL_WINDOWS)
C_GROUP_W = D_MODEL // C_GROUPS
C_WIDTH = C_GROUPS * C_GROUP_W
POOL_STATE = max(POOL_WINDOWS) - 1
PEER_HEADS = 8
N_KEYS = 128
N_EXPERTS = N_KEYS * N_KEYS
PEER_TOPK = 16
PEER_DKEY = 256
PEER_DHALF = PEER_DKEY // 2
DN_ALPHA = (2 * DEPTH) ** 0.25
LN_EPS = 1e-5
RMS_EPS = 1e-6

A_Q = A_HEADS * A_HEAD_DIM
A_KV = A_KV_HEADS * A_HEAD_DIM
B_QK = B_HEADS * B_DK
B_V = B_HEADS * B_DV
SPLITS = (A_Q, A_KV, A_KV, B_QK, B_QK, B_V, B_GATE_RANK, B_V, C_WIDTH, 3 * D_MODEL)

LANES = 128
N_PROMPT = BATCH * SEQ
N_SAMPLE = DEC_BATCH * DEC_SEQ
N_TOK = N_PROMPT + N_SAMPLE

R_GATES = 0
R_VB = 3 * D_MODEL
R_GB = R_VB + B_V
R_UC = R_GB + B_V
R_QB = R_UC + C_WIDTH
R_KB = R_QB + B_QK
R_VA = R_KB + B_QK
R_LR = R_VA + A_KV
R_WIDTH = R_LR + LANES
QK_WIDTH = A_Q + A_KV

VMEM_LIMIT = 48 * 1024 * 1024

TM_QK = 512
TM_REST = 512
TN_REST = R_WIDTH // 2
TM_MERGE = 256
TP_POOL = 512
GLA_GROUP_PROMPT = 4
GLA_GROUP_SAMPLE = 4
TT_TOPK = 256
TT_PEER = 512
IB_PEER = 16


def _params(sem):
    return pltpu.CompilerParams(dimension_semantics=sem, vmem_limit_bytes=VMEM_LIMIT)


def _qk_kernel(x_ref, w_ref, b_ref, c_ref, s1_ref, s2_ref, o_ref):
    y = jnp.dot(x_ref[...], w_ref[...], preferred_element_type=F32) + b_ref[...]
    c = c_ref[...]
    s1 = s1_ref[...]
    s2 = s2_ref[...]
    for j in range(QK_WIDTH // LANES):
        yj = y[:, j * LANES:(j + 1) * LANES]
        up = pltpu.roll(yj, LANES - ROT_DIM // 2, axis=1)
        dn = pltpu.roll(yj, ROT_DIM // 2, axis=1)
        o_ref[:, j * LANES:(j + 1) * LANES] = yj * c + up * s1 + dn * s2


def _proj_qk(xb, w, b, rope_c, rope_s1, rope_s2):
    n_prompt_blocks = SEQ // TM_QK

    def tab_map(i):
        return (jnp.where(i < N_PROMPT // TM_QK, i % n_prompt_blocks, n_prompt_blocks), 0)

    tab_spec = pl.BlockSpec((TM_QK, LANES), tab_map)
    return pl.pallas_call(
        _qk_kernel,
        grid=(N_TOK // TM_QK,),
        in_specs=[
            pl.BlockSpec((TM_QK, D_MODEL), lambda i: (i, 0)),
            pl.BlockSpec((D_MODEL, QK_WIDTH), lambda i: (0, 0)),
            pl.BlockSpec((1, QK_WIDTH), lambda i: (0, 0)),
            tab_spec, tab_spec, tab_spec,
        ],
        out_specs=pl.BlockSpec((TM_QK, QK_WIDTH), lambda i: (i, 0)),
        out_shape=jax.ShapeDtypeStruct((N_TOK, QK_WIDTH), F32),
        compiler_params=_params(("parallel",)),
        name="proj_qk",
    )(xb, w, b, rope_c, rope_s1, rope_s2)


def _mm_bias_kernel(x_ref, w_ref, b_ref, o_ref):
    o_ref[...] = jnp.dot(x_ref[...], w_ref[...], preferred_element_type=F32) + b_ref[...]


def _proj_rest(xb, w, b):
    return pl.pallas_call(
        _mm_bias_kernel,
        grid=(R_WIDTH // TN_REST, N_TOK // TM_REST),
        in_specs=[
            pl.BlockSpec((TM_REST, D_MODEL), lambda j, i: (i, 0)),
            pl.BlockSpec((D_MODEL, TN_REST), lambda j, i: (0, j)),
            pl.BlockSpec((1, TN_REST), lambda j, i: (0, j)),
        ],
        out_specs=pl.BlockSpec((TM_REST, TN_REST), lambda j, i: (i, j)),
        out_shape=jax.ShapeDtypeStruct((N_TOK, R_WIDTH), F32),
        compiler_params=_params(("parallel", "arbitrary")),
        name="proj_rest",
    )(xb, w, b)


HEADS_PER_PASS = 16


def _attend(q, kk, vv, sink_ref, c_min, o_ref, row0):
    tq = q.shape[0]
    r = lax.broadcasted_iota(jnp.int32, (tq, 2 * WINDOW), 0)
    c = lax.broadcasted_iota(jnp.int32, (tq, 2 * WINDOW), 1)
    ok = (c > r) & (c <= r + WINDOW) & (c >= c_min)
    qb = (q * (A_HEAD_DIM ** -0.5)).astype(BF16)
    nt = (((1,), (1,)), ((), ()))
    for h0 in range(0, A_HEADS, HEADS_PER_PASS):
        hs = range(h0, h0 + HEADS_PER_PASS)
        col = {h: slice(h * A_HEAD_DIM, (h + 1) * A_HEAD_DIM) for h in hs}
        kv = {h: slice((h // A_GROUP) * A_HEAD_DIM, (h // A_GROUP + 1) * A_HEAD_DIM) for h in hs}
        s = {h: lax.dot_general(qb[:, col[h]], kk[:, kv[h]], nt, preferred_element_type=F32) for h in hs}
        s = {h: jnp.where(ok, s[h], NEG_INF) for h in hs}
        m = {h: jnp.maximum(jnp.max(s[h], axis=1, keepdims=True), sink_ref[h]) for h in hs}
        p = {h: jnp.exp(s[h] - m[h]) for h in hs}
        denom = {h: jnp.sum(p[h], axis=1, keepdims=True) + jnp.exp(sink_ref[h] - m[h]) for h in hs}
        o = {h: jnp.dot(p[h].astype(BF16), vv[:, kv[h]], preferred_element_type=F32) / denom[h] for h in hs}
        for h in hs:
            o_ref[pl.ds(row0, tq), col[h]] = o[h].astype(o_ref.dtype)


def _attn_prompt_kernel(sink_ref, q_ref, kc_ref, kp_ref, vc_ref, vp_ref, o_ref):
    n = pl.program_id(1)
    kk = jnp.concatenate([kp_ref[...], kc_ref[...]], axis=0).astype(BF16)
    vv = jnp.concatenate([vp_ref[...], vc_ref[...]], axis=0).astype(BF16)
    _attend(q_ref[...], kk, vv, sink_ref, jnp.where(n > 0, 0, WINDOW), o_ref, 0)


def _attn_prompt(qk, rest, sinks):
    nb = SEQ // WINDOW
    kcol = A_Q // A_KV
    vcol = R_VA // A_KV

    def cur(b, n):
        return b * nb + n

    def prev(b, n):
        return b * nb + jnp.maximum(n - 1, 0)

    return pl.pallas_call(
        _attn_prompt_kernel,
        grid=(BATCH, nb),
        in_specs=[
            pl.BlockSpec(memory_space=pltpu.SMEM),
            pl.BlockSpec((WINDOW, A_Q), lambda b, n: (cur(b, n), 0)),
            pl.BlockSpec((WINDOW, A_KV), lambda b, n: (cur(b, n), kcol)),
            pl.BlockSpec((WINDOW, A_KV), lambda b, n: (prev(b, n), kcol)),
            pl.BlockSpec((WINDOW, A_KV), lambda b, n: (cur(b, n), vcol)),
            pl.BlockSpec((WINDOW, A_KV), lambda b, n: (prev(b, n), vcol)),
        ],
        out_specs=pl.BlockSpec((WINDOW, A_Q), lambda b, n: (cur(b, n), 0)),
        out_shape=jax.ShapeDtypeStruct((N_PROMPT, A_Q), BF16),
        compiler_params=_params(("parallel", "arbitrary")),
        name="attn_prompt",
    )(sinks, qk, qk, qk, rest, rest)


BB_ATTN = 8


def _attn_sample_kernel(sink_ref, q_ref, kn_ref, vn_ref, ks_ref, vs_ref, o_ref):
    pad = jnp.zeros((WINDOW - DEC_SEQ, A_KV), F32)
    rows = A_GROUP * DEC_SEQ
    t = lax.broadcasted_iota(jnp.int32, (rows, 2 * WINDOW), 0) % DEC_SEQ
    c = lax.broadcasted_iota(jnp.int32, (rows, 2 * WINDOW), 1)
    ok = (c > t) & (c <= t + WINDOW)
    nt = (((1,), (1,)), ((), ()))

    def body(pair, carry):
        chains = [(e, g) for e in range(2) for g in range(A_KV_HEADS)]
        elem = {e: pair * 2 + e for e in range(2)}
        row0 = {e: pl.multiple_of(elem[e] * DEC_SEQ, DEC_SEQ) for e in range(2)}
        q = {e: q_ref[pl.ds(row0[e], DEC_SEQ), :] * (A_HEAD_DIM ** -0.5) for e in range(2)}
        kk = {e: jnp.concatenate([ks_ref[elem[e]], kn_ref[pl.ds(row0[e], DEC_SEQ), :], pad], axis=0).astype(BF16)
              for e in range(2)}
        vv = {e: jnp.concatenate([vs_ref[elem[e]], vn_ref[pl.ds(row0[e], DEC_SEQ), :], pad], axis=0).astype(BF16)
              for e in range(2)}
        heads = {g: range(g * A_GROUP, (g + 1) * A_GROUP) for g in range(A_KV_HEADS)}
        ds = {g: slice(g * A_HEAD_DIM, (g + 1) * A_HEAD_DIM) for g in range(A_KV_HEADS)}
        sink = {g: sink_ref[g * rows:(g + 1) * rows, 0:1] for g in range(A_KV_HEADS)}
        qg = {(bb, g): jnp.concatenate([q[bb][:, h * A_HEAD_DIM:(h + 1) * A_HEAD_DIM] for h in heads[g]],
                                       axis=0).astype(BF16) for bb, g in chains}
        s = {(bb, g): lax.dot_general(qg[bb, g], kk[bb][:, ds[g]], nt, preferred_element_type=F32)
             for bb, g in chains}
        s = {ch: jnp.where(ok, s[ch], NEG_INF) for ch in chains}
        m = {(bb, g): jnp.maximum(jnp.max(s[bb, g], axis=1, keepdims=True), sink[g]) for bb, g in chains}
        p = {ch: jnp.exp(s[ch] - m[ch]) for ch in chains}
        denom = {(bb, g): jnp.sum(p[bb, g], axis=1, keepdims=True) + jnp.exp(sink[g] - m[bb, g])
                 for bb, g in chains}
        o = {(bb, g): jnp.dot(p[bb, g].astype(BF16), vv[bb][:, ds[g]], preferred_element_type=F32) / denom[bb, g]
             for bb, g in chains}
        for bb, g in chains:
            for k, h in enumerate(heads[g]):
                o_ref[pl.ds(row0[bb], DEC_SEQ), h * A_HEAD_DIM:(h + 1) * A_HEAD_DIM] = (
                    o[bb, g][k * DEC_SEQ:(k + 1) * DEC_SEQ, :].astype(o_ref.dtype))
        return carry

    lax.fori_loop(0, BB_ATTN // 2, body, 0)


def _attn_sample(qk, rest, sinks, k_state, v_state):
    rows = BB_ATTN * DEC_SEQ
    base = N_PROMPT // rows
    kcol = A_Q // A_KV
    vcol = R_VA // A_KV
    sink_rows = jnp.broadcast_to(jnp.repeat(sinks, DEC_SEQ)[:, None], (A_HEADS * DEC_SEQ, LANES))
    return pl.pallas_call(
        _attn_sample_kernel,
        grid=(DEC_BATCH // BB_ATTN,),
        in_specs=[
            pl.BlockSpec((A_HEADS * DEC_SEQ, LANES), lambda i: (0, 0)),
            pl.BlockSpec((rows, A_Q), lambda i: (base + i, 0)),
            pl.BlockSpec((rows, A_KV), lambda i: (base + i, kcol)),
            pl.BlockSpec((rows, A_KV), lambda i: (base + i, vcol)),
            pl.BlockSpec((BB_ATTN, WINDOW, A_KV), lambda i: (i, 0, 0)),
            pl.BlockSpec((BB_ATTN, WINDOW, A_KV), lambda i: (i, 0, 0)),
        ],
        out_specs=pl.BlockSpec((rows, A_Q), lambda i: (i, 0)),
        out_shape=jax.ShapeDtypeStruct((N_SAMPLE, A_Q), BF16),
        compiler_params=_params(("parallel",)),
        name="attn_sample",
    )(sink_rows, qk, qk, rest, k_state, v_state)


def _split3(x):
    hi = x.astype(BF16)
    r1 = x - hi.astype(F32)
    mid = r1.astype(BF16)
    lo = (r1 - mid.astype(F32)).astype(BF16)
    return hi, mid, lo


GLA_INPUTS = ((LANES, R_LR), (B_QK, R_QB), (B_QK, R_KB), (B_V, R_VB), (B_V, R_GB))


def _gla_kernel(*refs, n_chunks, group, chunk, layer, n_in):
    ins, rest_refs = refs[:n_in], refs[n_in:]
    per = n_in // len(GLA_INPUTS)
    s0_ref, wa_ref, ba_ref, g_ref = rest_refs[:4]
    prev_ref = rest_refs[4] if layer else None
    o_ref, sout_ref, st_ref = rest_refs[-3:]
    ci = pl.program_id(1)
    c = chunk

    def rows(inp, g):
        if per == 1:
            return ins[inp][g * c:(g + 1) * c, :]
        return ins[inp * per + g][...]

    single = n_chunks == 1
    if not single:
        @pl.when(ci == 0)
        def _():
            for g in range(group):
                for h in range(B_HEADS):
                    st_ref[g, h] = s0_ref[0, g, h].T

    ri = lax.broadcasted_iota(jnp.int32, (c, c), 0)
    cj = lax.broadcasted_iota(jnp.int32, (c, c), 1)
    causal = cj <= ri
    tri = jnp.where(causal, 1.0, 0.0).astype(BF16)
    nt = (((1,), (1,)), ((), ()))
    G = range(group)
    z = [jnp.dot(rows(0, g).astype(BF16), wa_ref[...], preferred_element_type=F32) + ba_ref[...] for g in G]
    log_a = [-(jnp.maximum(-z[g], 0.0) + jnp.log1p(jnp.exp(-jnp.abs(z[g])))) / B_TAU for g in G]
    parts = [_split3(log_a[g]) for g in G]
    b = [jnp.dot(tri, parts[g][0], preferred_element_type=F32)
         + jnp.dot(tri, parts[g][1], preferred_element_type=F32)
         + jnp.dot(tri, parts[g][2], preferred_element_type=F32) for g in G]
    bl = [b[g][c - 1:c, :] for g in G]
    qd = [(rows(1, g) * (B_DK ** -0.5) * jnp.exp(b[g])).astype(BF16) for g in G]
    kd = [(rows(2, g) * jnp.exp(-b[g])).astype(BF16) for g in G]
    kl = [rows(2, g) * jnp.exp(bl[g] - b[g]) for g in G]
    kl = [kl[g].T if single else kl[g].astype(BF16) for g in G]
    ebl = [jnp.exp(bl[g]) for g in G]
    for h in range(B_HEADS):
        ks = slice(h * B_DK, (h + 1) * B_DK)
        vs = slice(h * B_DV, (h + 1) * B_DV)
        vh = [rows(3, g)[:, vs] for g in G]
        if single:
            st = [s0_ref[0, g, h] for g in G]
            o = [jnp.dot(qd[g][:, ks], st[g].astype(BF16), preferred_element_type=F32) for g in G]
        else:
            st = [st_ref[g, h] for g in G]
            o = [lax.dot_general(qd[g][:, ks], st[g].astype(BF16), nt, preferred_element_type=F32) for g in G]
        att = [lax.dot_general(qd[g][:, ks], kd[g][:, ks], nt, preferred_element_type=F32) for g in G]
        att = [jnp.where(causal, att[g], 0.0).astype(BF16) for g in G]
        o = [o[g] + jnp.dot(att[g], vh[g].astype(BF16), preferred_element_type=F32) for g in G]
        for g in G:
            if single:
                decay = jnp.broadcast_to(ebl[g][:, ks], (SUBLANES, B_DK)).T[:, 0:1]
                for l in range(layer):
                    sout_ref[l, g, h] = prev_ref[l, g, h]
                sout_ref[layer, g, h] = st[g] * decay + jnp.dot(kl[g][ks, :], vh[g],
                                                                  preferred_element_type=F32)
            else:
                st_ref[g, h] = st[g] * ebl[g][:, ks] + jnp.dot(vh[g].T.astype(BF16), kl[g][:, ks],
                                                               preferred_element_type=F32)
        o = [o[g] * lax.rsqrt(jnp.mean(o[g] * o[g], axis=1, keepdims=True) + RMS_EPS) * g_ref[h:h + 1, :]
             for g in G]
        for g in G:
            gate = rows(4, g)[:, vs]
            o_ref[g, :, vs] = (o[g] * (gate / (1.0 + jnp.exp(-gate)))).astype(o_ref.dtype)

    if not single:
        @pl.when(ci == n_chunks - 1)
        def _():
            for g in range(group):
                for l in range(layer):
                    sout_ref[l, g] = prev_ref[l, g]
                for h in range(B_HEADS):
                    sout_ref[layer, g, h] = st_ref[g, h].T


def _gla(rest, s0, layer, prev_states, wa, ba, gain, *, n_batch, seq, chunk, row_base, group):
    n_chunks = seq // chunk
    contiguous = n_chunks == 1
    base = row_base // chunk

    def in_specs_for(width, col):
        if contiguous:
            return [pl.BlockSpec((group * chunk, width), lambda b, ci: (base // group + b, col // width))]
        return [pl.BlockSpec((chunk, width),
                             lambda b, ci, g=g: (base + (b * group + g) * n_chunks + ci, col // width))
                for g in range(group)]

    row_specs = [spec for width, col in GLA_INPUTS for spec in in_specs_for(width, col)]
    s0_layer = layer if s0.shape[0] > 1 else 0
    state_block = (group, B_HEADS, B_DK, B_DV)
    prev_specs = [pl.BlockSpec((layer,) + state_block, lambda b, ci: (0, b, 0, 0, 0))] if layer else []
    prev_args = [prev_states] if layer else []
    return pl.pallas_call(
        functools.partial(_gla_kernel, n_chunks=n_chunks, group=group, chunk=chunk, layer=layer,
                          n_in=len(row_specs)),
        grid=(n_batch // group, n_chunks),
        in_specs=row_specs + [
            pl.BlockSpec((1,) + state_block, lambda b, ci: (s0_layer, b, 0, 0, 0)),
            pl.BlockSpec((LANES, B_QK), lambda b, ci: (0, 0)),
            pl.BlockSpec((1, B_QK), lambda b, ci: (0, 0)),
            pl.BlockSpec((B_HEADS, B_DV), lambda b, ci: (0, 0)),
        ] + prev_specs,
        out_specs=[
            pl.BlockSpec((group, chunk, B_V), lambda b, ci: (b, ci, 0)),
            pl.BlockSpec((layer + 1,) + state_block, lambda b, ci: (0, b, 0, 0, 0)),
        ],
        out_shape=[
            jax.ShapeDtypeStruct((n_batch, seq, B_V), BF16),
            jax.ShapeDtypeStruct((layer + 1, n_batch, B_HEADS, B_DK, B_DV), F32),
        ],
        scratch_shapes=[pltpu.VMEM((group, B_HEADS, B_DV, B_DK), F32)],
        compiler_params=_params(("parallel", "arbitrary")),
        name="gla",
    )(*([rest] * len(row_specs)), s0, wa, ba, gain, *prev_args)


HALO = 16
BB_POOL = 16


def _pool_kernel(u_ref, prev_ref, w_ref, scale_ref, o_ref, *, from_start, n_seq):
    tp = u_ref.shape[0] // n_seq
    if from_start:
        ti = pl.program_id(1)
        t0 = ti * tp
    diffs = [[] for _ in POOL_WINDOWS]
    for sq in range(n_seq):
        u = u_ref[sq * tp:(sq + 1) * tp, :]
        prev = prev_ref[sq * HALO:(sq + 1) * HALO, :]
        if from_start:
            prev = jnp.where(ti > 0, prev, 0.0)
        full = jnp.concatenate([prev, u], axis=0)
        for g, w in enumerate(POOL_WINDOWS):
            cs = slice(g * C_GROUP_W, (g + 1) * C_GROUP_W)
            acc = full[:, cs]
            span = 1
            while span < w:
                acc = acc + pltpu.roll(acc, span, axis=0)
                span *= 2
            wsum = acc[HALO:, :]
            if from_start:
                t = t0 + lax.broadcasted_iota(jnp.int32, (tp, C_GROUP_W), 0)
                cnt = jnp.minimum(t + 1, w).astype(F32)
            else:
                cnt = float(w)
            diffs[g].append(wsum / cnt - u[:, cs])
    for g in range(C_GROUPS):
        cs = slice(g * C_GROUP_W, (g + 1) * C_GROUP_W)
        d = jnp.concatenate(diffs[g], axis=0) if n_seq > 1 else diffs[g][0]
        y = jnp.dot(d.astype(BF16), w_ref[g], preferred_element_type=F32) * scale_ref[:, cs]
        o_ref[:, cs] = y.astype(o_ref.dtype)


def _pool_prompt(rest, w, scale):
    nt = SEQ // TP_POOL
    ucol = R_UC // C_WIDTH

    def halo(b, i):
        return (jnp.maximum((b * SEQ + i * TP_POOL) // HALO - 1, 0), ucol)

    return pl.pallas_call(
        functools.partial(_pool_kernel, from_start=True, n_seq=1),
        grid=(BATCH, nt),
        in_specs=[
            pl.BlockSpec((TP_POOL, C_WIDTH), lambda b, i: (b * nt + i, ucol)),
            pl.BlockSpec((HALO, C_WIDTH), halo),
            pl.BlockSpec((C_GROUPS, C_GROUP_W, C_GROUP_W), lambda b, i: (0, 0, 0)),
            pl.BlockSpec((1, C_WIDTH), lambda b, i: (0, 0)),
        ],
        out_specs=pl.BlockSpec((TP_POOL, C_WIDTH), lambda b, i: (b * nt + i, 0)),
        out_shape=jax.ShapeDtypeStruct((N_PROMPT, C_WIDTH), BF16),
        compiler_params=_params(("parallel", "arbitrary")),
        name="pool_prompt",
    )(rest, rest, w, scale)


def _pool_sample(rest, prev, w, scale):
    ucol = R_UC // C_WIDTH
    rows = BB_POOL * DEC_SEQ
    base = N_PROMPT // rows
    return pl.pallas_call(
        functools.partial(_pool_kernel, from_start=False, n_seq=BB_POOL),
        grid=(DEC_BATCH // BB_POOL,),
        in_specs=[
            pl.BlockSpec((rows, C_WIDTH), lambda b: (base + b, ucol)),
            pl.BlockSpec((BB_POOL * HALO, C_WIDTH), lambda b: (b, 0)),
            pl.BlockSpec((C_GROUPS, C_GROUP_W, C_GROUP_W), lambda b: (0, 0, 0)),
            pl.BlockSpec((1, C_WIDTH), lambda b: (0, 0)),
        ],
        out_specs=pl.BlockSpec((rows, C_WIDTH), lambda b: (b, 0)),
        out_shape=jax.ShapeDtypeStruct((N_SAMPLE, C_WIDTH), BF16),
        compiler_params=_params(("parallel",)),
        name="pool_sample",
    )(rest, prev, w, scale)


def _layer_norm(x, g, b):
    mu = jnp.mean(x, axis=1, keepdims=True)
    xc = x - mu
    var = jnp.mean(xc * xc, axis=1, keepdims=True)
    return xc * lax.rsqrt(var + LN_EPS) * g + b


def _merge_kernel(x_ref, gates_ref, oa_p_ref, oa_s_ref, ob_p_ref, ob_s_ref, oc_p_ref, oc_s_ref,
                  wa_ref, wb_ref, wc_ref, wo_ref, g_ref, b_ref, o_ref, oT_ref):
    in_sample = pl.program_id(0) >= N_PROMPT // TM_MERGE

    def gate(i):
        z = gates_ref[:, i * D_MODEL:(i + 1) * D_MODEL]
        return 1.0 / (1.0 + jnp.exp(-z))

    def branch(p_ref, s_ref, w_ref):
        o = jnp.where(in_sample, s_ref[...], p_ref[...])
        return jnp.dot(o, w_ref[...], preferred_element_type=F32)

    merged = (gate(0) * branch(oa_p_ref, oa_s_ref, wa_ref)
              + gate(1) * branch(ob_p_ref, ob_s_ref, wb_ref)
              + gate(2) * branch(oc_p_ref, oc_s_ref, wc_ref))
    mix = jnp.dot(merged.astype(BF16), wo_ref[...], preferred_element_type=F32)
    y = _layer_norm(DN_ALPHA * x_ref[...] + mix, g_ref[...], b_ref[...])
    o_ref[...] = y
    oT_ref[...] = y.T.astype(BF16)


def _merge(x, rest, branches, wa, wb, wc, wo, g, b):
    n_p = N_PROMPT // TM_MERGE
    row = pl.BlockSpec((TM_MERGE, D_MODEL), lambda i: (i, 0))
    row_p = pl.BlockSpec((TM_MERGE, D_MODEL), lambda i: (jnp.minimum(i, n_p - 1), 0))
    row_s = pl.BlockSpec((TM_MERGE, D_MODEL), lambda i: (jnp.maximum(i - n_p, 0), 0))
    wspec = pl.BlockSpec((D_MODEL, D_MODEL), lambda i: (0, 0))
    vec = pl.BlockSpec((1, D_MODEL), lambda i: (0, 0))
    (oa_p, oa_s), (ob_p, ob_s), (oc_p, oc_s) = branches
    return pl.pallas_call(
        _merge_kernel,
        grid=(N_TOK // TM_MERGE,),
        in_specs=[row, pl.BlockSpec((TM_MERGE, 3 * D_MODEL), lambda i: (i, R_GATES)),
                  row_p, row_s, row_p, row_s, row_p, row_s, wspec, wspec, wspec, wspec, vec, vec],
        out_specs=[row, pl.BlockSpec((D_MODEL, TM_MERGE), lambda i: (0, i))],
        out_shape=[jax.ShapeDtypeStruct((N_TOK, D_MODEL), F32),
                   jax.ShapeDtypeStruct((D_MODEL, N_TOK), BF16)],
        compiler_params=_params(("parallel",)),
        name="merge",
    )(x, rest, oa_p, oa_s, ob_p, ob_s, oc_p, oc_s, wa, wb, wc, wo, g, b)


SUBLANES = 8


def _extract_desc(s, n):
    vals = []
    for _ in range(n):
        m = jnp.max(s, axis=0, keepdims=True)
        vals.append(m)
        s = jnp.where(s == m, -jnp.inf, s)
    return vals


def _merge_sort_pairs(n):
    pairs = []
    p = 1
    while p < n:
        k = p
        while k >= 1:
            for j in range(k % p, n - k, 2 * k):
                for i in range(min(k, n - j - k)):
                    if (i + j) // (2 * p) == (i + j + k) // (2 * p):
                        pairs.append((i + j, i + j + k))
            k //= 2
        p *= 2
    return pairs


def _top_desc(s, n):
    v = [s[SUBLANES * k:SUBLANES * (k + 1), :] for k in range(s.shape[0] // SUBLANES)]
    for a, b in _merge_sort_pairs(len(v)):
        v[a], v[b] = jnp.maximum(v[a], v[b]), jnp.minimum(v[a], v[b])
    vals = []
    for r in range(n):
        m = jnp.max(v[0], axis=0, keepdims=True)
        vals.append(m)
        hit = v[0] == m
        for k in range(n - r - 1):
            v[k] = jnp.where(hit, v[k + 1], v[k])
    return vals


def _rank_of(s, vals):
    rank = jnp.full(s.shape, float(len(vals)), F32)
    for r, val in enumerate(vals):
        rank = jnp.where(s == val, float(r), rank)
    return rank


def _peer_topk_kernel(xT_ref, wq_ref, sk_ref, cnt_ref, e1_ref, rank_ref, e2_ref, s1_scr, sv_scr):
    tt = xT_ref.shape[1]
    xT = xT_ref[...]
    for hp in range(2 * PEER_HEADS):
        h, second = divmod(hp, 2)
        qT = jnp.dot(wq_ref[hp * PEER_DHALF:(hp + 1) * PEER_DHALF, :], xT, preferred_element_type=F32)
        s = jnp.dot(sk_ref[hp], qT.astype(BF16), preferred_element_type=F32)
        vals = _top_desc(s, PEER_TOPK)
        sv_scr[hp] = jnp.concatenate(vals, axis=0)
        if second:
            rank_ref[h] = _rank_of(s, vals).astype(BF16)
            e2_ref[h] = jnp.exp(s - vals[0]).astype(BF16)
        else:
            s1_scr[h] = s
    row8 = lax.broadcasted_iota(jnp.int32, (8, tt), 0)
    for h in range(PEER_HEADS):
        sv1 = sv_scr[2 * h]
        sv2 = sv_scr[2 * h + 1]
        pieces = [sv1[0:1, :] + sv2]
        for a in range(1, 8):
            nb = PEER_TOPK // (a + 1)
            pieces.append(jnp.where(row8 < nb, sv1[a:a + 1, :] + sv2[0:8, :], -jnp.inf))
        pieces.append(sv2[0:1, :] + sv1[8:16, :])
        cand = _extract_desc(jnp.concatenate(pieces, axis=0), PEER_TOPK)
        top = cand[0]
        z = jnp.zeros_like(top)
        for r in range(PEER_TOPK):
            z = z + jnp.exp(cand[r] - top)
        kth = cand[PEER_TOPK - 1]
        counts = [jnp.sum(jnp.where(p >= kth, 1.0, 0.0), axis=0, keepdims=True) for p in pieces[:8]]
        tail = jnp.where(pieces[8] >= kth, 1.0, 0.0)
        counts += [tail[a:a + 1, :] for a in range(8)]
        s1 = s1_scr[h]
        cnt = jnp.zeros_like(s1)
        for a in range(PEER_TOPK):
            cnt = jnp.where(s1 == sv1[a:a + 1, :], counts[a], cnt)
        e1 = jnp.exp(s1 - (sv1[0:1, :] + jnp.log(z))) * SQRT_HALF
        for c in range(tt // LANES):
            cnt_ref[h, c] = cnt[:, c * LANES:(c + 1) * LANES]
            e1_ref[h, c] = e1[:, c * LANES:(c + 1) * LANES]


def _peer_topk(xT, wqT, sk):
    spec = pl.BlockSpec((PEER_HEADS, N_KEYS, TT_TOPK), lambda t: (0, 0, t))
    row_spec = pl.BlockSpec((PEER_HEADS, TT_TOPK // LANES, N_KEYS, LANES), lambda t: (0, t, 0, 0))
    wide = jax.ShapeDtypeStruct((PEER_HEADS, N_TOK // LANES, N_KEYS, LANES), F32)
    narrow = jax.ShapeDtypeStruct((PEER_HEADS, N_KEYS, N_TOK), BF16)
    return pl.pallas_call(
        _peer_topk_kernel,
        grid=(N_TOK // TT_TOPK,),
        in_specs=[
            pl.BlockSpec((D_MODEL, TT_TOPK), lambda t: (0, t)),
            pl.BlockSpec((PEER_HEADS * PEER_DKEY, D_MODEL), lambda t: (0, 0)),
            pl.BlockSpec((2 * PEER_HEADS, N_KEYS, PEER_DHALF), lambda t: (0, 0, 0)),
        ],
        out_specs=[row_spec, row_spec, spec, spec],
        out_shape=[wide, wide, narrow, narrow],
        scratch_shapes=[pltpu.VMEM((PEER_HEADS, N_KEYS, TT_TOPK), F32),
                        pltpu.VMEM((2 * PEER_HEADS, PEER_TOPK, TT_TOPK), F32)],
        compiler_params=_params(("parallel",)),
        name="peer_topk",
    )(xT, wqT, sk)


SQRT_HALF = 2.0 ** -0.5


def _gelu_unscaled(x):
    t = x * SQRT_HALF
    return t * (1.0 + lax.erf(t))


LC_PEER = 256


def _bf16_rows(ref, h, r, chunks, n_rows):
    x = jnp.concatenate([ref[h, c, pl.ds(r, 16, stride=0), :] for c in chunks], axis=1)
    packed = x.astype(BF16)
    return jnp.concatenate([packed] * (n_rows // packed.shape[0]), axis=0)


MM_PIECE = 512


def _peer_main_kernel(xT_ref, u_ref, vt_ref, cnt_ref, e1_ref, rank_ref, e2_ref, x_ref, g_ref, b_ref,
                      o_ref, ob_ref, yT_ref, s_scr, wh_scr):
    i = pl.program_id(1)
    tt = xT_ref.shape[1]

    @pl.when(i == 0)
    def _():
        yT_ref[...] = jnp.zeros_like(yT_ref)

    zero = jnp.zeros((N_KEYS, LC_PEER), BF16)
    per_piece = MM_PIECE // N_KEYS
    for ii in range(IB_PEER):
        rows = slice(ii * N_KEYS, (ii + 1) * N_KEYS)
        if ii % per_piece == 0:
            piece = slice(ii * N_KEYS, ii * N_KEYS + MM_PIECE)
            s_scr[piece, :] = jnp.dot(u_ref[0, piece, :], xT_ref[...], preferred_element_type=F32)
        for lc in range(tt // LC_PEER):
            cols = slice(lc * LC_PEER, (lc + 1) * LC_PEER)
            chunks = range(lc * LC_PEER // LANES, (lc + 1) * LC_PEER // LANES)
            w = zero
            for h in range(PEER_HEADS):
                cnt = _bf16_rows(cnt_ref, h, ii, chunks, N_KEYS)
                e1 = _bf16_rows(e1_ref, h, ii, chunks, N_KEYS)
                w = w + jnp.where(rank_ref[h, :, cols] < cnt, e2_ref[h, :, cols] * e1, zero)
            wh_scr[rows, cols] = w
    for ii in range(IB_PEER):
        rows = slice(ii * N_KEYS, (ii + 1) * N_KEYS)
        wh_scr[rows, :] = wh_scr[rows, :] * _gelu_unscaled(s_scr[rows, :]).astype(BF16)
    yT_ref[...] += jnp.dot(vt_ref[0], wh_scr[...], preferred_element_type=F32)

    @pl.when(i == pl.num_programs(1) - 1)
    def _():
        y = _layer_norm(DN_ALPHA * x_ref[...] + yT_ref[...].T, g_ref[...], b_ref[...])
        o_ref[...] = y
        ob_ref[...] = y.astype(BF16)


def _peer_main(xT, u, vt, layer, cnt, e1, rank, e2, x1, g, b):
    eb = IB_PEER * N_KEYS
    row_spec = pl.BlockSpec((PEER_HEADS, TT_PEER // LANES, IB_PEER, LANES), lambda t, i: (0, t, i, 0))
    tab_spec = pl.BlockSpec((PEER_HEADS, N_KEYS, TT_PEER), lambda t, i: (0, 0, t))
    tok_spec = pl.BlockSpec((TT_PEER, D_MODEL), lambda t, i: (t, 0))
    vec = pl.BlockSpec((1, D_MODEL), lambda t, i: (0, 0))
    return pl.pallas_call(
        _peer_main_kernel,
        grid=(N_TOK // TT_PEER, N_EXPERTS // eb),
        in_specs=[
            pl.BlockSpec((D_MODEL, TT_PEER), lambda t, i: (0, t)),
            pl.BlockSpec((1, eb, D_MODEL), lambda t, i: (layer, i, 0)),
            pl.BlockSpec((1, D_MODEL, eb), lambda t, i: (layer, 0, i)),
            row_spec, row_spec, tab_spec, tab_spec, tok_spec, vec, vec,
        ],
        out_specs=[tok_spec, tok_spec],
        out_shape=[jax.ShapeDtypeStruct((N_TOK, D_MODEL), F32),
                   jax.ShapeDtypeStruct((N_TOK, D_MODEL), BF16)],
        scratch_shapes=[pltpu.VMEM((D_MODEL, TT_PEER), F32),
                        pltpu.VMEM((eb, TT_PEER), F32), pltpu.VMEM((eb, TT_PEER), BF16)],
        compiler_params=_params(("parallel", "arbitrary")),
        name="peer_main",
    )(xT, u, vt, cnt, e1, rank, e2, x1, g, b)


def _rope_tables():
    half = ROT_DIM // 2
    pos = jnp.concatenate([jnp.arange(SEQ), PAST_LEN + (jnp.arange(TM_QK) % DEC_SEQ)])
    inv = ROPE_THETA ** (-jnp.arange(half, dtype=F32) / half)
    ang = pos.astype(F32)[:, None] * inv[None, :]
    cos, sin = jnp.cos(ang), jnp.sin(ang)
    n = pos.shape[0]
    one = jnp.ones((n, A_HEAD_DIM - ROT_DIM), F32)
    zero = jnp.zeros((n, A_HEAD_DIM - ROT_DIM), F32)
    zh = jnp.zeros((n, half), F32)
    reps = LANES // A_HEAD_DIM
    c = jnp.tile(jnp.concatenate([cos, cos, one], 1), (1, reps))
    s1 = jnp.tile(jnp.concatenate([-sin, zh, zero], 1), (1, reps))
    s2 = jnp.tile(jnp.concatenate([zh, sin, zero], 1), (1, reps))
    return c, s1, s2


def _split_cols(w):
    cuts = [int(c) for c in np.cumsum(SPLITS)[:-1]]
    return jnp.split(w, cuts, axis=-1)


def _layer(layer, x, xb, k_state, v_state, gla_states, gla_out, pool_state, rope, w_in, b_in, sinks,
           w_alpha, b_alpha, gla_g, w_pool, pool_scale, w_a, w_b, w_c, w_out, ln1_g, ln1_b,
           peer_query, peer_subkeys, peer_u, peer_vt, ln2_g, ln2_b):
    qa_w, ka_w, va_w, qb_w, kb_w, vb_w, lr_w, gb_w, uc_w, gates_w = _split_cols(w_in)
    qa_b, ka_b, va_b, qb_b, kb_b, vb_b, lr_b, gb_b, uc_b, gates_b = _split_cols(b_in[None, :])
    lr_pad = LANES - B_GATE_RANK
    w_qk = jnp.concatenate([qa_w, ka_w], 1).astype(BF16)
    b_qk = jnp.concatenate([qa_b, ka_b], 1)
    w_rest = jnp.concatenate([gates_w, vb_w, gb_w, uc_w, qb_w, kb_w, va_w,
                              jnp.pad(lr_w, ((0, 0), (0, lr_pad)))], 1).astype(BF16)
    b_rest = jnp.concatenate([gates_b, vb_b, gb_b, uc_b, qb_b, kb_b, va_b,
                              jnp.pad(lr_b, ((0, 0), (0, lr_pad)))], 1)

    qk = _proj_qk(xb, w_qk, b_qk, *rope)
    rest = _proj_rest(xb, w_rest, b_rest)

    ks = k_state.reshape(DEC_BATCH, WINDOW, A_KV)
    vs = v_state.reshape(DEC_BATCH, WINDOW, A_KV)
    oa = (_attn_prompt(qk, rest, sinks), _attn_sample(qk, rest, sinks, ks, vs))

    wa = jnp.pad(w_alpha, ((0, lr_pad), (0, 0))).astype(BF16)
    ba = b_alpha[None, :]
    gla_p, gla_s = gla_out
    ob_p, gla_p = _gla(rest, jnp.zeros((1, BATCH, B_HEADS, B_DK, B_DV), F32), layer, gla_p, wa, ba, gla_g,
                       n_batch=BATCH, seq=SEQ, chunk=GLA_CHUNK, row_base=0, group=GLA_GROUP_PROMPT)
    ob_s, gla_s = _gla(rest, gla_states, layer, gla_s, wa, ba, gla_g, n_batch=DEC_BATCH, seq=DEC_SEQ,
                       chunk=math.gcd(DEC_SEQ, GLA_CHUNK), row_base=N_PROMPT, group=GLA_GROUP_SAMPLE)
    ob = (ob_p.reshape(N_PROMPT, B_V), ob_s.reshape(N_SAMPLE, B_V))

    wp = w_pool.astype(BF16)
    ps = pool_scale[None, :]
    prev = jnp.pad(pool_state, ((0, 0), (HALO - POOL_STATE, 0), (0, 0))).reshape(DEC_BATCH * HALO, C_WIDTH)
    oc = (_pool_prompt(rest, wp, ps), _pool_sample(rest, prev, wp, ps))

    x1, x1T = _merge(x, rest, (oa, ob, oc), w_a.astype(BF16), w_b.astype(BF16), w_c.astype(BF16),
                     w_out.astype(BF16), ln1_g[None, :], ln1_b[None, :])

    wqT = peer_query.reshape(D_MODEL, PEER_HEADS * PEER_DKEY).T.astype(BF16)
    sk = peer_subkeys.reshape(2 * PEER_HEADS, N_KEYS, PEER_DHALF).astype(BF16)
    cnt, e1, rank, e2 = _peer_topk(x1T, wqT, sk)
    x2, x2b = _peer_main(x1T, peer_u, peer_vt, layer, cnt, e1, rank, e2, x1, ln2_g[None, :], ln2_b[None, :])

    def prompt_tail(t, col0, width, n):
        return jnp.stack([t[(b + 1) * SEQ - n:(b + 1) * SEQ, col0:col0 + width] for b in range(BATCH)])

    def sample_tail(state, t, col0, width, n):
        new = t[N_PROMPT:, col0:col0 + width].reshape(DEC_BATCH, DEC_SEQ, width)
        return jnp.concatenate([state, new], 1)[:, -n:]

    kv_shape = (-1, WINDOW, A_KV_HEADS, A_HEAD_DIM)
    states = (prompt_tail(qk, A_Q, A_KV, WINDOW).reshape(kv_shape),
              prompt_tail(rest, R_VA, A_KV, WINDOW).reshape(kv_shape),
              prompt_tail(rest, R_UC, C_WIDTH, POOL_STATE),
              sample_tail(ks, qk, A_Q, A_KV, WINDOW).reshape(kv_shape),
              sample_tail(vs, rest, R_VA, A_KV, WINDOW).reshape(kv_shape),
              sample_tail(pool_state, rest, R_UC, C_WIDTH, POOL_STATE))
    return x2, x2b, states, (gla_p, gla_s)


def kernel(x_prompt, x_sample, state_win_k, state_win_v, state_gla, state_pool, w_in, b_in, attn_sinks,
           w_alpha, b_alpha, gla_norm_g, w_pool, pool_scale, w_branch_a, w_branch_b, w_branch_c, w_out,
           ln1_g, ln1_b, peer_query, peer_subkeys, peer_u, peer_v, ln2_g, ln2_b):
    x = jnp.concatenate([x_prompt.reshape(N_PROMPT, D_MODEL), x_sample.reshape(N_SAMPLE, D_MODEL)], 0)
    xb = x.astype(BF16)
    rope = _rope_tables()
    peer_ub = peer_u.astype(BF16)
    peer_vtb = jnp.swapaxes(peer_v, 1, 2).astype(BF16)
    per_layer = []
    gla_out = (None, None)
    for l in range(DEPTH):
        x, xb, states, gla_out = _layer(
            l, x, xb, state_win_k[l], state_win_v[l], state_gla, gla_out, state_pool[l], rope,
            w_in[l], b_in[l], attn_sinks[l], w_alpha[l], b_alpha[l], gla_norm_g[l],
            w_pool[l], pool_scale[l], w_branch_a[l], w_branch_b[l], w_branch_c[l], w_out[l],
            ln1_g[l], ln1_b[l], peer_query[l], peer_subkeys[l], peer_ub, peer_vtb, ln2_g[l], ln2_b[l])
        per_layer.append(states)
    pk, pv, pp, sk, sv, sp = [jnp.stack([per_layer[l][i] for l in range(DEPTH)]) for i in range(6)]
    return (x[:N_PROMPT].reshape(BATCH, SEQ, D_MODEL), x[N_PROMPT:].reshape(DEC_BATCH, DEC_SEQ, D_MODEL),
            pk, pv, gla_out[0], pp, sk, sv, gla_out[1], sp)
```

```python
import functools
import math

import jax
import jax.numpy as jnp
import numpy as np
from jax import lax
from jax.experimental import pallas as pl
from jax.experimental.pallas import tpu as pltpu

F32 = jnp.float32
BF16 = jnp.bfloat16

D_MODEL = 1024
BATCH = 8
SEQ = 2048
DEPTH = 2
DEC_BATCH = 128
DEC_SEQ = 8
PAST_LEN = 16384

A_HEADS = 16
A_KV_HEADS = 2
A_HEAD_DIM = 64
A_GROUP = A_HEADS // A_KV_HEADS
WINDOW = 128
ROT_DIM = A_HEAD_DIM // 4
ROPE_THETA = 500000.0
NEG_INF = -1e30
B_HEADS = 4
B_DK = D_MODEL // 2 // B_HEADS
B_DV = D_MODEL // B_HEADS
B_GATE_RANK = 16
B_TAU = 16.0
GLA_CHUNK = 64
POOL_WINDOWS = (2, 4, 8, 16)
C_GROUPS = len(POOL_WINDOWS)
C_GROUP_W = D_MODEL // C_GROUPS
C_WIDTH = C_GROUPS * C_GROUP_W
POOL_STATE = max(POOL_WINDOWS) - 1
PEER_HEADS = 8
N_KEYS = 128
N_EXPERTS = N_KEYS * N_KEYS
PEER_TOPK = 16
PEER_DKEY = 256
PEER_DHALF = PEER_DKEY // 2
DN_ALPHA = (2 * DEPTH) ** 0.25
LN_EPS = 1e-5
RMS_EPS = 1e-6

A_Q = A_HEADS * A_HEAD_DIM
A_KV = A_KV_HEADS * A_HEAD_DIM
B_QK = B_HEADS * B_DK
B_V = B_HEADS * B_DV
SPLITS = (A_Q, A_KV, A_KV, B_QK, B_QK, B_V, B_GATE_RANK, B_V, C_WIDTH, 3 * D_MODEL)

LANES = 128
N_PROMPT = BATCH * SEQ
N_SAMPLE = DEC_BATCH * DEC_SEQ
N_TOK = N_PROMPT + N_SAMPLE

R_GATES = 0
R_VB = 3 * D_MODEL
R_GB = R_VB + B_V
R_UC = R_GB + B_V
R_QB = R_UC + C_WIDTH
R_KB = R_QB + B_QK
R_VA = R_KB + B_QK
R_LR = R_VA + A_KV
R_WIDTH = R_LR + LANES
QK_WIDTH = A_Q + A_KV

VMEM_LIMIT = 48 * 1024 * 1024

TM_QK = 512
TM_REST = 512
TN_REST = R_WIDTH // 2
TM_MERGE = 256
TP_POOL = 512
GLA_GROUP_PROMPT = 4
GLA_GROUP_SAMPLE = 4
TT_TOPK = 1024
TT_PEER = 512
IB_PEER = 16


def _params(sem):
    return pltpu.CompilerParams(dimension_semantics=sem, vmem_limit_bytes=VMEM_LIMIT)


def _qk_kernel(x_ref, w_ref, b_ref, c_ref, s1_ref, s2_ref, o_ref):
    y = jnp.dot(x_ref[...], w_ref[...], preferred_element_type=F32) + b_ref[...]
    c = c_ref[...]
    s1 = s1_ref[...]
    s2 = s2_ref[...]
    for j in range(QK_WIDTH // LANES):
        yj = y[:, j * LANES:(j + 1) * LANES]
        up = pltpu.roll(yj, LANES - ROT_DIM // 2, axis=1)
        dn = pltpu.roll(yj, ROT_DIM // 2, axis=1)
        o_ref[:, j * LANES:(j + 1) * LANES] = yj * c + up * s1 + dn * s2


def _proj_qk(xb, w, b, rope_c, rope_s1, rope_s2):
    n_prompt_blocks = SEQ // TM_QK

    def tab_map(i):
        return (jnp.where(i < N_PROMPT // TM_QK, i % n_prompt_blocks, n_prompt_blocks), 0)

    tab_spec = pl.BlockSpec((TM_QK, LANES), tab_map)
    return pl.pallas_call(
        _qk_kernel,
        grid=(N_TOK // TM_QK,),
        in_specs=[
            pl.BlockSpec((TM_QK, D_MODEL), lambda i: (i, 0)),
            pl.BlockSpec((D_MODEL, QK_WIDTH), lambda i: (0, 0)),
            pl.BlockSpec((1, QK_WIDTH), lambda i: (0, 0)),
            tab_spec, tab_spec, tab_spec,
        ],
        out_specs=pl.BlockSpec((TM_QK, QK_WIDTH), lambda i: (i, 0)),
        out_shape=jax.ShapeDtypeStruct((N_TOK, QK_WIDTH), F32),
        compiler_params=_params(("parallel",)),
        name="proj_qk",
    )(xb, w, b, rope_c, rope_s1, rope_s2)


def _mm_bias_kernel(x_ref, w_ref, b_ref, o_ref):
    o_ref[...] = jnp.dot(x_ref[...], w_ref[...], preferred_element_type=F32) + b_ref[...]


def _proj_rest(xb, w, b):
    return pl.pallas_call(
        _mm_bias_kernel,
        grid=(R_WIDTH // TN_REST, N_TOK // TM_REST),
        in_specs=[
            pl.BlockSpec((TM_REST, D_MODEL), lambda j, i: (i, 0)),
            pl.BlockSpec((D_MODEL, TN_REST), lambda j, i: (0, j)),
            pl.BlockSpec((1, TN_REST), lambda j, i: (0, j)),
        ],
        out_specs=pl.BlockSpec((TM_REST, TN_REST), lambda j, i: (i, j)),
        out_shape=jax.ShapeDtypeStruct((N_TOK, R_WIDTH), F32),
        compiler_params=_params(("parallel", "arbitrary")),
        name="proj_rest",
    )(xb, w, b)


HEADS_PER_PASS = 16


def _attend(q, kk, vv, sink_ref, c_min, o_ref, row0):
    tq = q.shape[0]
    r = lax.broadcasted_iota(jnp.int32, (tq, 2 * WINDOW), 0)
    c = lax.broadcasted_iota(jnp.int32, (tq, 2 * WINDOW), 1)
    ok = (c > r) & (c <= r + WINDOW) & (c >= c_min)
    qb = (q * (A_HEAD_DIM ** -0.5)).astype(BF16)
    nt = (((1,), (1,)), ((), ()))
    for h0 in range(0, A_HEADS, HEADS_PER_PASS):
        hs = range(h0, h0 + HEADS_PER_PASS)
        col = {h: slice(h * A_HEAD_DIM, (h + 1) * A_HEAD_DIM) for h in hs}
        kv = {h: slice((h // A_GROUP) * A_HEAD_DIM, (h // A_GROUP + 1) * A_HEAD_DIM) for h in hs}
        s = {h: lax.dot_general(qb[:, col[h]], kk[:, kv[h]], nt, preferred_element_type=F32) for h in hs}
        s = {h: jnp.where(ok, s[h], NEG_INF) for h in hs}
        m = {h: jnp.maximum(jnp.max(s[h], axis=1, keepdims=True), sink_ref[h]) for h in hs}
        p = {h: jnp.exp(s[h] - m[h]) for h in hs}
        denom = {h: jnp.sum(p[h], axis=1, keepdims=True) + jnp.exp(sink_ref[h] - m[h]) for h in hs}
        o = {h: jnp.dot(p[h].astype(BF16), vv[:, kv[h]], preferred_element_type=F32) / denom[h] for h in hs}
        for h in hs:
            o_ref[pl.ds(row0, tq), col[h]] = o[h].astype(o_ref.dtype)


def _attn_prompt_kernel(sink_ref, q_ref, kc_ref, kp_ref, vc_ref, vp_ref, o_ref):
    n = pl.program_id(1)
    kk = jnp.concatenate([kp_ref[...], kc_ref[...]], axis=0).astype(BF16)
    vv = jnp.concatenate([vp_ref[...], vc_ref[...]], axis=0).astype(BF16)
    _attend(q_ref[...], kk, vv, sink_ref, jnp.where(n > 0, 0, WINDOW), o_ref, 0)


def _attn_prompt(qk, rest, sinks):
    nb = SEQ // WINDOW
    kcol = A_Q // A_KV
    vcol = R_VA // A_KV

    def cur(b, n):
        return b * nb + n

    def prev(b, n):
        return b * nb + jnp.maximum(n - 1, 0)

    return pl.pallas_call(
        _attn_prompt_kernel,
        grid=(BATCH, nb),
        in_specs=[
            pl.BlockSpec(memory_space=pltpu.SMEM),
            pl.BlockSpec((WINDOW, A_Q), lambda b, n: (cur(b, n), 0)),
            pl.BlockSpec((WINDOW, A_KV), lambda b, n: (cur(b, n), kcol)),
            pl.BlockSpec((WINDOW, A_KV), lambda b, n: (prev(b, n), kcol)),
            pl.BlockSpec((WINDOW, A_KV), lambda b, n: (cur(b, n), vcol)),
            pl.BlockSpec((WINDOW, A_KV), lambda b, n: (prev(b, n), vcol)),
        ],
        out_specs=pl.BlockSpec((WINDOW, A_Q), lambda b, n: (cur(b, n), 0)),
        out_shape=jax.ShapeDtypeStruct((N_PROMPT, A_Q), BF16),
        compiler_params=_params(("parallel", "arbitrary")),
        name="attn_prompt",
    )(sinks, qk, qk, qk, rest, rest)


BB_ATTN = 8


def _attn_sample_kernel(sink_ref, q_ref, kn_ref, vn_ref, ks_ref, vs_ref, o_ref):
    pad = jnp.zeros((WINDOW - DEC_SEQ, A_KV), F32)
    rows = A_GROUP * DEC_SEQ
    t = lax.broadcasted_iota(jnp.int32, (rows, 2 * WINDOW), 0) % DEC_SEQ
    c = lax.broadcasted_iota(jnp.int32, (rows, 2 * WINDOW), 1)
    ok = (c > t) & (c <= t + WINDOW)
    nt = (((1,), (1,)), ((), ()))

    def body(pair, carry):
        chains = [(e, g) for e in range(2) for g in range(A_KV_HEADS)]
        elem = {e: pair * 2 + e for e in range(2)}
        row0 = {e: pl.multiple_of(elem[e] * DEC_SEQ, DEC_SEQ) for e in range(2)}
        q = {e: q_ref[pl.ds(row0[e], DEC_SEQ), :] * (A_HEAD_DIM ** -0.5) for e in range(2)}
        kk = {e: jnp.concatenate([ks_ref[elem[e]], kn_ref[pl.ds(row0[e], DEC_SEQ), :], pad], axis=0).astype(BF16)
              for e in range(2)}
        vv = {e: jnp.concatenate([vs_ref[elem[e]], vn_ref[pl.ds(row0[e], DEC_SEQ), :], pad], axis=0).astype(BF16)
              for e in range(2)}
        heads = {g: range(g * A_GROUP, (g + 1) * A_GROUP) for g in range(A_KV_HEADS)}
        ds = {g: slice(g * A_HEAD_DIM, (g + 1) * A_HEAD_DIM) for g in range(A_KV_HEADS)}
        sink = {g: sink_ref[g * rows:(g + 1) * rows, 0:1] for g in range(A_KV_HEADS)}
        qg = {(bb, g): jnp.concatenate([q[bb][:, h * A_HEAD_DIM:(h + 1) * A_HEAD_DIM] for h in heads[g]],
                                       axis=0).astype(BF16) for bb, g in chains}
        s = {(bb, g): lax.dot_general(qg[bb, g], kk[bb][:, ds[g]], nt, preferred_element_type=F32)
             for bb, g in chains}
        s = {ch: jnp.where(ok, s[ch], NEG_INF) for ch in chains}
        m = {(bb, g): jnp.maximum(jnp.max(s[bb, g], axis=1, keepdims=True), sink[g]) for bb, g in chains}
        p = {ch: jnp.exp(s[ch] - m[ch]) for ch in chains}
        denom = {(bb, g): jnp.sum(p[bb, g], axis=1, keepdims=True) + jnp.exp(sink[g] - m[bb, g])
                 for bb, g in chains}
        o = {(bb, g): jnp.dot(p[bb, g].astype(BF16), vv[bb][:, ds[g]], preferred_element_type=F32) / denom[bb, g]
             for bb, g in chains}
        for bb, g in chains:
            for k, h in enumerate(heads[g]):
                o_ref[pl.ds(row0[bb], DEC_SEQ), h * A_HEAD_DIM:(h + 1) * A_HEAD_DIM] = (
                    o[bb, g][k * DEC_SEQ:(k + 1) * DEC_SEQ, :].astype(o_ref.dtype))
        return carry

    lax.fori_loop(0, BB_ATTN // 2, body, 0)


def _attn_sample(qk, rest, sinks, k_state, v_state):
    rows = BB_ATTN * DEC_SEQ
    base = N_PROMPT // rows
    kcol = A_Q // A_KV
    vcol = R_VA // A_KV
    sink_rows = jnp.broadcast_to(jnp.repeat(sinks, DEC_SEQ)[:, None], (A_HEADS * DEC_SEQ, LANES))
    return pl.pallas_call(
        _attn_sample_kernel,
        grid=(DEC_BATCH // BB_ATTN,),
        in_specs=[
            pl.BlockSpec((A_HEADS * DEC_SEQ, LANES), lambda i: (0, 0)),
            pl.BlockSpec((rows, A_Q), lambda i: (base + i, 0)),
            pl.BlockSpec((rows, A_KV), lambda i: (base + i, kcol)),
            pl.BlockSpec((rows, A_KV), lambda i: (base + i, vcol)),
            pl.BlockSpec((BB_ATTN, WINDOW, A_KV), lambda i: (i, 0, 0)),
            pl.BlockSpec((BB_ATTN, WINDOW, A_KV), lambda i: (i, 0, 0)),
        ],
        out_specs=pl.BlockSpec((rows, A_Q), lambda i: (i, 0)),
        out_shape=jax.ShapeDtypeStruct((N_SAMPLE, A_Q), BF16),
        compiler_params=_params(("parallel",)),
        name="attn_sample",
    )(sink_rows, qk, qk, rest, k_state, v_state)


def _split3(x):
    hi = x.astype(BF16)
    r1 = x - hi.astype(F32)
    mid = r1.astype(BF16)
    lo = (r1 - mid.astype(F32)).astype(BF16)
    return hi, mid, lo


GLA_INPUTS = ((LANES, R_LR), (B_QK, R_QB), (B_QK, R_KB), (B_V, R_VB), (B_V, R_GB))


def _gla_kernel(*refs, n_chunks, group, chunk, layer, n_in):
    ins, rest_refs = refs[:n_in], refs[n_in:]
    per = n_in // len(GLA_INPUTS)
    s0_ref, wa_ref, ba_ref, g_ref = rest_refs[:4]
    prev_ref = rest_refs[4] if layer else None
    o_ref, sout_ref, st_ref = rest_refs[-3:]
    ci = pl.program_id(1)
    c = chunk

    def rows(inp, g):
        if per == 1:
            return ins[inp][g * c:(g + 1) * c, :]
        return ins[inp * per + g][...]

    single = n_chunks == 1
    if not single:
        @pl.when(ci == 0)
        def _():
            for g in range(group):
                for h in range(B_HEADS):
                    st_ref[g, h] = s0_ref[0, g, h].T

    ri = lax.broadcasted_iota(jnp.int32, (c, c), 0)
    cj = lax.broadcasted_iota(jnp.int32, (c, c), 1)
    causal = cj <= ri
    tri = jnp.where(causal, 1.0, 0.0).astype(BF16)
    nt = (((1,), (1,)), ((), ()))
    G = range(group)
    z = [jnp.dot(rows(0, g).astype(BF16), wa_ref[...], preferred_element_type=F32) + ba_ref[...] for g in G]
    log_a = [-(jnp.maximum(-z[g], 0.0) + jnp.log1p(jnp.exp(-jnp.abs(z[g])))) / B_TAU for g in G]
    parts = [_split3(log_a[g]) for g in G]
    b = [jnp.dot(tri, parts[g][0], preferred_element_type=F32)
         + jnp.dot(tri, parts[g][1], preferred_element_type=F32)
         + jnp.dot(tri, parts[g][2], preferred_element_type=F32) for g in G]
    bl = [b[g][c - 1:c, :] for g in G]
    qd = [(rows(1, g) * (B_DK ** -0.5) * jnp.exp(b[g])).astype(BF16) for g in G]
    kd = [(rows(2, g) * jnp.exp(-b[g])).astype(BF16) for g in G]
    kl = [rows(2, g) * jnp.exp(bl[g] - b[g]) for g in G]
    kl = [kl[g].T if single else kl[g].astype(BF16) for g in G]
    ebl = [jnp.exp(bl[g]) for g in G]
    for h in range(B_HEADS):
        ks = slice(h * B_DK, (h + 1) * B_DK)
        vs = slice(h * B_DV, (h + 1) * B_DV)
        vh = [rows(3, g)[:, vs] for g in G]
        if single:
            st = [s0_ref[0, g, h] for g in G]
            o = [jnp.dot(qd[g][:, ks], st[g].astype(BF16), preferred_element_type=F32) for g in G]
        else:
            st = [st_ref[g, h] for g in G]
            o = [lax.dot_general(qd[g][:, ks], st[g].astype(BF16), nt, preferred_element_type=F32) for g in G]
        att = [lax.dot_general(qd[g][:, ks], kd[g][:, ks], nt, preferred_element_type=F32) for g in G]
        att = [jnp.where(causal, att[g], 0.0).astype(BF16) for g in G]
        o = [o[g] + jnp.dot(att[g], vh[g].astype(BF16), preferred_element_type=F32) for g in G]
        for g in G:
            if single:
                decay = jnp.broadcast_to(ebl[g][:, ks], (SUBLANES, B_DK)).T[:, 0:1]
                for l in range(layer):
                    sout_ref[l, g, h] = prev_ref[l, g, h]
                sout_ref[layer, g, h] = st[g] * decay + jnp.dot(kl[g][ks, :], vh[g],
                                                                  preferred_element_type=F32)
            else:
                st_ref[g, h] = st[g] * ebl[g][:, ks] + jnp.dot(vh[g].T.astype(BF16), kl[g][:, ks],
                                                               preferred_element_type=F32)
        o = [o[g] * lax.rsqrt(jnp.mean(o[g] * o[g], axis=1, keepdims=True) + RMS_EPS) * g_ref[h:h + 1, :]
             for g in G]
        for g in G:
            gate = rows(4, g)[:, vs]
            o_ref[g, :, vs] = (o[g] * (gate / (1.0 + jnp.exp(-gate)))).astype(o_ref.dtype)

    if not single:
        @pl.when(ci == n_chunks - 1)
        def _():
            for g in range(group):
                for l in range(layer):
                    sout_ref[l, g] = prev_ref[l, g]
                for h in range(B_HEADS):
                    sout_ref[layer, g, h] = st_ref[g, h].T


def _gla(rest, s0, layer, prev_states, wa, ba, gain, *, n_batch, seq, chunk, row_base, group):
    n_chunks = seq // chunk
    contiguous = n_chunks == 1
    base = row_base // chunk

    def in_specs_for(width, col):
        if contiguous:
            return [pl.BlockSpec((group * chunk, width), lambda b, ci: (base // group + b, col // width))]
        return [pl.BlockSpec((chunk, width),
                             lambda b, ci, g=g: (base + (b * group + g) * n_chunks + ci, col // width))
                for g in range(group)]

    row_specs = [spec for width, col in GLA_INPUTS for spec in in_specs_for(width, col)]
    s0_layer = layer if s0.shape[0] > 1 else 0
    state_block = (group, B_HEADS, B_DK, B_DV)
    prev_specs = [pl.BlockSpec((layer,) + state_block, lambda b, ci: (0, b, 0, 0, 0))] if layer else []
    prev_args = [prev_states] if layer else []
    return pl.pallas_call(
        functools.partial(_gla_kernel, n_chunks=n_chunks, group=group, chunk=chunk, layer=layer,
                          n_in=len(row_specs)),
        grid=(n_batch // group, n_chunks),
        in_specs=row_specs + [
            pl.BlockSpec((1,) + state_block, lambda b, ci: (s0_layer, b, 0, 0, 0)),
            pl.BlockSpec((LANES, B_QK), lambda b, ci: (0, 0)),
            pl.BlockSpec((1, B_QK), lambda b, ci: (0, 0)),
            pl.BlockSpec((B_HEADS, B_DV), lambda b, ci: (0, 0)),
        ] + prev_specs,
        out_specs=[
            pl.BlockSpec((group, chunk, B_V), lambda b, ci: (b, ci, 0)),
            pl.BlockSpec((layer + 1,) + state_block, lambda b, ci: (0, b, 0, 0, 0)),
        ],
        out_shape=[
            jax.ShapeDtypeStruct((n_batch, seq, B_V), BF16),
            jax.ShapeDtypeStruct((layer + 1, n_batch, B_HEADS, B_DK, B_DV), F32),
        ],
        scratch_shapes=[pltpu.VMEM((group, B_HEADS, B_DV, B_DK), F32)],
        compiler_params=_params(("parallel", "arbitrary")),
        name="gla",
    )(*([rest] * len(row_specs)), s0, wa, ba, gain, *prev_args)


HALO = 16
BB_POOL = 16


def _pool_kernel(u_ref, prev_ref, w_ref, scale_ref, o_ref, *, from_start, n_seq):
    tp = u_ref.shape[0] // n_seq
    if from_start:
        ti = pl.program_id(1)
        t0 = ti * tp
    diffs = [[] for _ in POOL_WINDOWS]
    for sq in range(n_seq):
        u = u_ref[sq * tp:(sq + 1) * tp, :]
        prev = prev_ref[sq * HALO:(sq + 1) * HALO, :]
        if from_start:
            prev = jnp.where(ti > 0, prev, 0.0)
        full = jnp.concatenate([prev, u], axis=0)
        for g, w in enumerate(POOL_WINDOWS):
            cs = slice(g * C_GROUP_W, (g + 1) * C_GROUP_W)
            acc = full[:, cs]
            span = 1
            while span < w:
                acc = acc + pltpu.roll(acc, span, axis=0)
                span *= 2
            wsum = acc[HALO:, :]
            if from_start:
                t = t0 + lax.broadcasted_iota(jnp.int32, (tp, C_GROUP_W), 0)
                cnt = jnp.minimum(t + 1, w).astype(F32)
            else:
                cnt = float(w)
            diffs[g].append(wsum / cnt - u[:, cs])
    for g in range(C_GROUPS):
        cs = slice(g * C_GROUP_W, (g + 1) * C_GROUP_W)
        d = jnp.concatenate(diffs[g], axis=0) if n_seq > 1 else diffs[g][0]
        y = jnp.dot(d.astype(BF16), w_ref[g], preferred_element_type=F32) * scale_ref[:, cs]
        o_ref[:, cs] = y.astype(o_ref.dtype)


def _pool_prompt(rest, w, scale):
    nt = SEQ // TP_POOL
    ucol = R_UC // C_WIDTH

    def halo(b, i):
        return (jnp.maximum((b * SEQ + i * TP_POOL) // HALO - 1, 0), ucol)

    return pl.pallas_call(
        functools.partial(_pool_kernel, from_start=True, n_seq=1),
        grid=(BATCH, nt),
        in_specs=[
            pl.BlockSpec((TP_POOL, C_WIDTH), lambda b, i: (b * nt + i, ucol)),
            pl.BlockSpec((HALO, C_WIDTH), halo),
            pl.BlockSpec((C_GROUPS, C_GROUP_W, C_GROUP_W), lambda b, i: (0, 0, 0)),
            pl.BlockSpec((1, C_WIDTH), lambda b, i: (0, 0)),
        ],
        out_specs=pl.BlockSpec((TP_POOL, C_WIDTH), lambda b, i: (b * nt + i, 0)),
        out_shape=jax.ShapeDtypeStruct((N_PROMPT, C_WIDTH), BF16),
        compiler_params=_params(("parallel", "arbitrary")),
        name="pool_prompt",
    )(rest, rest, w, scale)


def _pool_sample(rest, prev, w, scale):
    ucol = R_UC // C_WIDTH
    rows = BB_POOL * DEC_SEQ
    base = N_PROMPT // rows
    return pl.pallas_call(
        functools.partial(_pool_kernel, from_start=False, n_seq=BB_POOL),
        grid=(DEC_BATCH // BB_POOL,),
        in_specs=[
            pl.BlockSpec((rows, C_WIDTH), lambda b: (base + b, ucol)),
            pl.BlockSpec((BB_POOL * HALO, C_WIDTH), lambda b: (b, 0)),
            pl.BlockSpec((C_GROUPS, C_GROUP_W, C_GROUP_W), lambda b: (0, 0, 0)),
            pl.BlockSpec((1, C_WIDTH), lambda b: (0, 0)),
        ],
        out_specs=pl.BlockSpec((rows, C_WIDTH), lambda b: (b, 0)),
        out_shape=jax.ShapeDtypeStruct((N_SAMPLE, C_WIDTH), BF16),
        compiler_params=_params(("parallel",)),
        name="pool_sample",
    )(rest, prev, w, scale)


def _layer_norm(x, g, b):
    mu = jnp.mean(x, axis=1, keepdims=True)
    xc = x - mu
    var = jnp.mean(xc * xc, axis=1, keepdims=True)
    return xc * lax.rsqrt(var + LN_EPS) * g + b


def _merge_kernel(x_ref, gates_ref, oa_p_ref, oa_s_ref, ob_p_ref, ob_s_ref, oc_p_ref, oc_s_ref,
                  wa_ref, wb_ref, wc_ref, wo_ref, g_ref, b_ref, o_ref, oT_ref):
    in_sample = pl.program_id(0) >= N_PROMPT // TM_MERGE

    def gate(i):
        z = gates_ref[:, i * D_MODEL:(i + 1) * D_MODEL]
        return 1.0 / (1.0 + jnp.exp(-z))

    def branch(p_ref, s_ref, w_ref):
        o = jnp.where(in_sample, s_ref[...], p_ref[...])
        return jnp.dot(o, w_ref[...], preferred_element_type=F32)

    merged = (gate(0) * branch(oa_p_ref, oa_s_ref, wa_ref)
              + gate(1) * branch(ob_p_ref, ob_s_ref, wb_ref)
              + gate(2) * branch(oc_p_ref, oc_s_ref, wc_ref))
    mix = jnp.dot(merged.astype(BF16), wo_ref[...], preferred_element_type=F32)
    y = _layer_norm(DN_ALPHA * x_ref[...] + mix, g_ref[...], b_ref[...])
    o_ref[...] = y
    oT_ref[...] = y.T.astype(BF16)


def _merge(x, rest, branches, wa, wb, wc, wo, g, b):
    n_p = N_PROMPT // TM_MERGE
    row = pl.BlockSpec((TM_MERGE, D_MODEL), lambda i: (i, 0))
    row_p = pl.BlockSpec((TM_MERGE, D_MODEL), lambda i: (jnp.minimum(i, n_p - 1), 0))
    row_s = pl.BlockSpec((TM_MERGE, D_MODEL), lambda i: (jnp.maximum(i - n_p, 0), 0))
    wspec = pl.BlockSpec((D_MODEL, D_MODEL), lambda i: (0, 0))
    vec = pl.BlockSpec((1, D_MODEL), lambda i: (0, 0))
    (oa_p, oa_s), (ob_p, ob_s), (oc_p, oc_s) = branches
    return pl.pallas_call(
        _merge_kernel,
        grid=(N_TOK // TM_MERGE,),
        in_specs=[row, pl.BlockSpec((TM_MERGE, 3 * D_MODEL), lambda i: (i, R_GATES)),
                  row_p, row_s, row_p, row_s, row_p, row_s, wspec, wspec, wspec, wspec, vec, vec],
        out_specs=[row, pl.BlockSpec((D_MODEL, TM_MERGE), lambda i: (0, i))],
        out_shape=[jax.ShapeDtypeStruct((N_TOK, D_MODEL), F32),
                   jax.ShapeDtypeStruct((D_MODEL, N_TOK), BF16)],
        compiler_params=_params(("parallel",)),
        name="merge",
    )(x, rest, oa_p, oa_s, ob_p, ob_s, oc_p, oc_s, wa, wb, wc, wo, g, b)


SUBLANES = 8


def _extract_desc(s, n):
    vals = []
    for _ in range(n):
        m = jnp.max(s, axis=0, keepdims=True)
        vals.append(m)
        s = jnp.where(s == m, -jnp.inf, s)
    return vals


def _merge_sort_pairs(n):
    pairs = []
    p = 1
    while p < n:
        k = p
        while k >= 1:
            for j in range(k % p, n - k, 2 * k):
                for i in range(min(k, n - j - k)):
                    if (i + j) // (2 * p) == (i + j + k) // (2 * p):
                        pairs.append((i + j, i + j + k))
            k //= 2
        p *= 2
    return pairs


def _top_desc(s, n):
    v = [s[SUBLANES * k:SUBLANES * (k + 1), :] for k in range(s.shape[0] // SUBLANES)]
    for a, b in _merge_sort_pairs(len(v)):
        v[a], v[b] = jnp.maximum(v[a], v[b]), jnp.minimum(v[a], v[b])
    vals = []
    for r in range(n):
        m = jnp.max(v[0], axis=0, keepdims=True)
        vals.append(m)
        hit = v[0] == m
        for k in range(n - r - 1):
            v[k] = jnp.where(hit, v[k + 1], v[k])
    return vals


def _rank_of(s, vals):
    rank = jnp.full(s.shape, float(len(vals)), F32)
    for r, val in enumerate(vals):
        rank = jnp.where(s == val, float(r), rank)
    return rank


def _peer_topk_kernel(xT_ref, wq_ref, sk_ref, cnt_ref, e1_ref, rank_ref, e2_ref, s1_scr, sv_scr):
    tt = xT_ref.shape[1]
    xT = xT_ref[...]
    for hp in range(2 * PEER_HEADS):
        h, second = divmod(hp, 2)
        qT = jnp.dot(wq_ref[hp * PEER_DHALF:(hp + 1) * PEER_DHALF, :], xT, preferred_element_type=F32)
        s = jnp.dot(sk_ref[hp], qT.astype(BF16), preferred_element_type=F32)
        vals = _top_desc(s, PEER_TOPK)
        sv_scr[hp] = jnp.concatenate(vals, axis=0)
        if second:
            rank_ref[h] = _rank_of(s, vals).astype(BF16)
            e2_ref[h] = jnp.exp(s - vals[0]).astype(BF16)
        else:
            s1_scr[h] = s
    row8 = lax.broadcasted_iota(jnp.int32, (8, tt), 0)
    for h in range(PEER_HEADS):
        sv1 = sv_scr[2 * h]
        sv2 = sv_scr[2 * h + 1]
        pieces = [sv1[0:1, :] + sv2]
        for a in range(1, 8):
            nb = PEER_TOPK // (a + 1)
            pieces.append(jnp.where(row8 < nb, sv1[a:a + 1, :] + sv2[0:8, :], -jnp.inf))
        pieces.append(sv2[0:1, :] + sv1[8:16, :])
        cand = _extract_desc(jnp.concatenate(pieces, axis=0), PEER_TOPK)
        top = cand[0]
        z = jnp.zeros_like(top)
        for r in range(PEER_TOPK):
            z = z + jnp.exp(cand[r] - top)
        kth = cand[PEER_TOPK - 1]
        counts = [jnp.sum(jnp.where(p >= kth, 1.0, 0.0), axis=0, keepdims=True) for p in pieces[:8]]
        tail = jnp.where(pieces[8] >= kth, 1.0, 0.0)
        counts += [tail[a:a + 1, :] for a in range(8)]
        s1 = s1_scr[h]
        cnt = jnp.zeros_like(s1)
        for a in range(PEER_TOPK):
            cnt = jnp.where(s1 == sv1[a:a + 1, :], counts[a], cnt)
        e1 = jnp.exp(s1 - (sv1[0:1, :] + jnp.log(z))) * SQRT_HALF
        for c in range(tt // LANES):
            cnt_ref[h, c] = cnt[:, c * LANES:(c + 1) * LANES]
            e1_ref[h, c] = e1[:, c * LANES:(c + 1) * LANES]


def _peer_topk(xT, wqT, sk):
    spec = pl.BlockSpec((PEER_HEADS, N_KEYS, TT_TOPK), lambda t: (0, 0, t))
    row_spec = pl.BlockSpec((PEER_HEADS, TT_TOPK // LANES, N_KEYS, LANES), lambda t: (0, t, 0, 0))
    wide = jax.ShapeDtypeStruct((PEER_HEADS, N_TOK // LANES, N_KEYS, LANES), F32)
    narrow = jax.ShapeDtypeStruct((PEER_HEADS, N_KEYS, N_TOK), BF16)
    return pl.pallas_call(
        _peer_topk_kernel,
        grid=(N_TOK // TT_TOPK,),
        in_specs=[
            pl.BlockSpec((D_MODEL, TT_TOPK), lambda t: (0, t)),
            pl.BlockSpec((PEER_HEADS * PEER_DKEY, D_MODEL), lambda t: (0, 0)),
            pl.BlockSpec((2 * PEER_HEADS, N_KEYS, PEER_DHALF), lambda t: (0, 0, 0)),
        ],
        out_specs=[row_spec, row_spec, spec, spec],
        out_shape=[wide, wide, narrow, narrow],
        scratch_shapes=[pltpu.VMEM((PEER_HEADS, N_KEYS, TT_TOPK), F32),
                        pltpu.VMEM((2 * PEER_HEADS, PEER_TOPK, TT_TOPK), F32)],
        compiler_params=_params(("parallel",)),
        name="peer_topk",
    )(xT, wqT, sk)


SQRT_HALF = 2.0 ** -0.5


def _gelu_unscaled(x):
    t = x * SQRT_HALF
    return t * (1.0 + lax.erf(t))


LC_PEER = 256


def _bf16_rows(ref, h, r, chunks, n_rows):
    x = jnp.concatenate([ref[h, c, pl.ds(r, 16, stride=0), :] for c in chunks], axis=1)
    packed = x.astype(BF16)
    return jnp.concatenate([packed] * (n_rows // packed.shape[0]), axis=0)


MM_PIECE = 512


def _peer_main_kernel(xT_ref, u_ref, vt_ref, cnt_ref, e1_ref, rank_ref, e2_ref, x_ref, g_ref, b_ref,
                      o_ref, ob_ref, yT_ref, s_scr, wh_scr):
    i = pl.program_id(1)
    tt = xT_ref.shape[1]

    @pl.when(i == 0)
    def _():
        yT_ref[...] = jnp.zeros_like(yT_ref)

    zero = jnp.zeros((N_KEYS, LC_PEER), BF16)
    per_piece = MM_PIECE // N_KEYS
    for ii in range(IB_PEER):
        rows = slice(ii * N_KEYS, (ii + 1) * N_KEYS)
        if ii % per_piece == 0:
            piece = slice(ii * N_KEYS, ii * N_KEYS + MM_PIECE)
            s_scr[piece, :] = jnp.dot(u_ref[0, piece, :], xT_ref[...], preferred_element_type=F32)
        for lc in range(tt // LC_PEER):
            cols = slice(lc * LC_PEER, (lc + 1) * LC_PEER)
            chunks = range(lc * LC_PEER // LANES, (lc + 1) * LC_PEER // LANES)
            w = zero
            for h in range(PEER_HEADS):
                cnt = _bf16_rows(cnt_ref, h, ii, chunks, N_KEYS)
                e1 = _bf16_rows(e1_ref, h, ii, chunks, N_KEYS)
                w = w + jnp.where(rank_ref[h, :, cols] < cnt, e2_ref[h, :, cols] * e1, zero)
            wh_scr[rows, cols] = w
    for ii in range(IB_PEER):
        rows = slice(ii * N_KEYS, (ii + 1) * N_KEYS)
        wh_scr[rows, :] = wh_scr[rows, :] * _gelu_unscaled(s_scr[rows, :]).astype(BF16)
    yT_ref[...] += jnp.dot(vt_ref[0], wh_scr[...], preferred_element_type=F32)

    @pl.when(i == pl.num_programs(1) - 1)
    def _():
        y = _layer_norm(DN_ALPHA * x_ref[...] + yT_ref[...].T, g_ref[...], b_ref[...])
        o_ref[...] = y
        ob_ref[...] = y.astype(BF16)


def _peer_main(xT, u, vt, layer, cnt, e1, rank, e2, x1, g, b):
    eb = IB_PEER * N_KEYS
    row_spec = pl.BlockSpec((PEER_HEADS, TT_PEER // LANES, IB_PEER, LANES), lambda t, i: (0, t, i, 0))
    tab_spec = pl.BlockSpec((PEER_HEADS, N_KEYS, TT_PEER), lambda t, i: (0, 0, t))
    tok_spec = pl.BlockSpec((TT_PEER, D_MODEL), lambda t, i: (t, 0))
    vec = pl.BlockSpec((1, D_MODEL), lambda t, i: (0, 0))
    return pl.pallas_call(
        _peer_main_kernel,
        grid=(N_TOK // TT_PEER, N_EXPERTS // eb),
        in_specs=[
            pl.BlockSpec((D_MODEL, TT_PEER), lambda t, i: (0, t)),
            pl.BlockSpec((1, eb, D_MODEL), lambda t, i: (layer, i, 0)),
            pl.BlockSpec((1, D_MODEL, eb), lambda t, i: (layer, 0, i)),
            row_spec, row_spec, tab_spec, tab_spec, tok_spec, vec, vec,
        ],
        out_specs=[tok_spec, tok_spec],
        out_shape=[jax.ShapeDtypeStruct((N_TOK, D_MODEL), F32),
                   jax.ShapeDtypeStruct((N_TOK, D_MODEL), BF16)],
        scratch_shapes=[pltpu.VMEM((D_MODEL, TT_PEER), F32),
                        pltpu.VMEM((eb, TT_PEER), F32), pltpu.VMEM((eb, TT_PEER), BF16)],
        compiler_params=_params(("parallel", "arbitrary")),
        name="peer_main",
    )(xT, u, vt, cnt, e1, rank, e2, x1, g, b)


def _rope_tables():
    half = ROT_DIM // 2
    pos = jnp.concatenate([jnp.arange(SEQ), PAST_LEN + (jnp.arange(TM_QK) % DEC_SEQ)])
    inv = ROPE_THETA ** (-jnp.arange(half, dtype=F32) / half)
    ang = pos.astype(F32)[:, None] * inv[None, :]
    cos, sin = jnp.cos(ang), jnp.sin(ang)
    n = pos.shape[0]
    one = jnp.ones((n, A_HEAD_DIM - ROT_DIM), F32)
    zero = jnp.zeros((n, A_HEAD_DIM - ROT_DIM), F32)
    zh = jnp.zeros((n, half), F32)
    reps = LANES // A_HEAD_DIM
    c = jnp.tile(jnp.concatenate([cos, cos, one], 1), (1, reps))
    s1 = jnp.tile(jnp.concatenate([-sin, zh, zero], 1), (1, reps))
    s2 = jnp.tile(jnp.concatenate([zh, sin, zero], 1), (1, reps))
    return c, s1, s2


def _split_cols(w):
    cuts = [int(c) for c in np.cumsum(SPLITS)[:-1]]
    return jnp.split(w, cuts, axis=-1)


def _layer(layer, x, xb, k_state, v_state, gla_states, gla_out, pool_state, rope, w_in, b_in, sinks,
           w_alpha, b_alpha, gla_g, w_pool, pool_scale, w_a, w_b, w_c, w_out, ln1_g, ln1_b,
           peer_query, peer_subkeys, peer_u, peer_vt, ln2_g, ln2_b):
    qa_w, ka_w, va_w, qb_w, kb_w, vb_w, lr_w, gb_w, uc_w, gates_w = _split_cols(w_in)
    qa_b, ka_b, va_b, qb_b, kb_b, vb_b, lr_b, gb_b, uc_b, gates_b = _split_cols(b_in[None, :])
    lr_pad = LANES - B_GATE_RANK
    w_qk = jnp.concatenate([qa_w, ka_w], 1).astype(BF16)
    b_qk = jnp.concatenate([qa_b, ka_b], 1)
    w_rest = jnp.concatenate([gates_w, vb_w, gb_w, uc_w, qb_w, kb_w, va_w,
                              jnp.pad(lr_w, ((0, 0), (0, lr_pad)))], 1).astype(BF16)
    b_rest = jnp.concatenate([gates_b, vb_b, gb_b, uc_b, qb_b, kb_b, va_b,
                              jnp.pad(lr_b, ((0, 0), (0, lr_pad)))], 1)

    qk = _proj_qk(xb, w_qk, b_qk, *rope)
    rest = _proj_rest(xb, w_rest, b_rest)

    ks = k_state.reshape(DEC_BATCH, WINDOW, A_KV)
    vs = v_state.reshape(DEC_BATCH, WINDOW, A_KV)
    oa = (_attn_prompt(qk, rest, sinks), _attn_sample(qk, rest, sinks, ks, vs))

    wa = jnp.pad(w_alpha, ((0, lr_pad), (0, 0))).astype(BF16)
    ba = b_alpha[None, :]
    gla_p, gla_s = gla_out
    ob_p, gla_p = _gla(rest, jnp.zeros((1, BATCH, B_HEADS, B_DK, B_DV), F32), layer, gla_p, wa, ba, gla_g,
                       n_batch=BATCH, seq=SEQ, chunk=GLA_CHUNK, row_base=0, group=GLA_GROUP_PROMPT)
    ob_s, gla_s = _gla(rest, gla_states, layer, gla_s, wa, ba, gla_g, n_batch=DEC_BATCH, seq=DEC_SEQ,
                       chunk=math.gcd(DEC_SEQ, GLA_CHUNK), row_base=N_PROMPT, group=GLA_GROUP_SAMPLE)
    ob = (ob_p.reshape(N_PROMPT, B_V), ob_s.reshape(N_SAMPLE, B_V))

    wp = w_pool.astype(BF16)
    ps = pool_scale[None, :]
    prev = jnp.pad(pool_state, ((0, 0), (HALO - POOL_STATE, 0), (0, 0))).reshape(DEC_BATCH * HALO, C_WIDTH)
    oc = (_pool_prompt(rest, wp, ps), _pool_sample(rest, prev, wp, ps))

    x1, x1T = _merge(x, rest, (oa, ob, oc), w_a.astype(BF16), w_b.astype(BF16), w_c.astype(BF16),
                     w_out.astype(BF16), ln1_g[None, :], ln1_b[None, :])

    wqT = peer_query.reshape(D_MODEL, PEER_HEADS * PEER_DKEY).T.astype(BF16)
    sk = peer_subkeys.reshape(2 * PEER_HEADS, N_KEYS, PEER_DHALF).astype(BF16)
    cnt, e1, rank, e2 = _peer_topk(x1T, wqT, sk)
    x2, x2b = _peer_main(x1T, peer_u, peer_vt, layer, cnt, e1, rank, e2, x1, ln2_g[None, :], ln2_b[None, :])

    def prompt_tail(t, col0, width, n):
        return jnp.stack([t[(b + 1) * SEQ - n:(b + 1) * SEQ, col0:col0 + width] for b in range(BATCH)])

    def sample_tail(state, t, col0, width, n):
        new = t[N_PROMPT:, col0:col0 + width].reshape(DEC_BATCH, DEC_SEQ, width)
        return jnp.concatenate([state, new], 1)[:, -n:]

    kv_shape = (-1, WINDOW, A_KV_HEADS, A_HEAD_DIM)
    states = (prompt_tail(qk, A_Q, A_KV, WINDOW).reshape(kv_shape),
              prompt_tail(rest, R_VA, A_KV, WINDOW).reshape(kv_shape),
              prompt_tail(rest, R_UC, C_WIDTH, POOL_STATE),
              sample_tail(ks, qk, A_Q, A_KV, WINDOW).reshape(kv_shape),
              sample_tail(vs, rest, R_VA, A_KV, WINDOW).reshape(kv_shape),
              sample_tail(pool_state, rest, R_UC, C_WIDTH, POOL_STATE))
    return x2, x2b, states, (gla_p, gla_s)


def kernel(x_prompt, x_sample, state_win_k, state_win_v, state_gla, state_pool, w_in, b_in, attn_sinks,
           w_alpha, b_alpha, gla_norm_g, w_pool, pool_scale, w_branch_a, w_branch_b, w_branch_c, w_out,
           ln1_g, ln1_b, peer_query, peer_subkeys, peer_u, peer_v, ln2_g, ln2_b):
    x = jnp.concatenate([x_prompt.reshape(N_PROMPT, D_MODEL), x_sample.reshape(N_SAMPLE, D_MODEL)], 0)
    xb = x.astype(BF16)
    rope = _rope_tables()
    peer_ub = peer_u.astype(BF16)
    peer_vtb = jnp.swapaxes(peer_v, 1, 2).astype(BF16)
    per_layer = []
    gla_out = (None, None)
    for l in range(DEPTH):
        x, xb, states, gla_out = _layer(
            l, x, xb, state_win_k[l], state_win_v[l], state_gla, gla_out, state_pool[l], rope,
            w_in[l], b_in[l], attn_sinks[l], w_alpha[l], b_alpha[l], gla_norm_g[l],
            w_pool[l], pool_scale[l], w_branch_a[l], w_branch_b[l], w_branch_c[l], w_out[l],
            ln1_g[l], ln1_b[l], peer_query[l], peer_subkeys[l], peer_ub, peer_vtb, ln2_g[l], ln2_b[l])
        per_layer.append(states)
    pk, pv, pp, sk, sv, sp = [jnp.stack([per_layer[l][i] for l in range(DEPTH)]) for i in range(6)]
    return (x[:N_PROMPT].reshape(BATCH, SEQ, D_MODEL), x[N_PROMPT:].reshape(DEC_BATCH, DEC_SEQ, D_MODEL),
            pk, pv, gla_out[0], pp, sk, sv, gla_out[1], sp)
```

```python
import functools
import math

import jax
import jax.numpy as jnp
import numpy as np
from jax import lax
from jax.experimental import pallas as pl
from jax.experimental.pallas import tpu as pltpu

F32 = jnp.float32
BF16 = jnp.bfloat16

D_MODEL = 1024
BATCH = 8
SEQ = 2048
DEPTH = 2
DEC_BATCH = 128
DEC_SEQ = 8
PAST_LEN = 16384

A_HEADS = 16
A_KV_HEADS = 2
A_HEAD_DIM = 64
A_GROUP = A_HEADS // A_KV_HEADS
WINDOW = 128
ROT_DIM = A_HEAD_DIM // 4
ROPE_THETA = 500000.0
NEG_INF = -1e30
B_HEADS = 4
B_DK = D_MODEL // 2 // B_HEADS
B_DV = D_MODEL // B_HEADS
B_GATE_RANK = 16
B_TAU = 16.0
GLA_CHUNK = 64
POOL_WINDOWS = (2, 4, 8, 16)
C_GROUPS = len(POOL_WINDOWS)
C_GROUP_W = D_MODEL // C_GROUPS
C_WIDTH = C_GROUPS * C_GROUP_W
POOL_STATE = max(POOL_WINDOWS) - 1
PEER_HEADS = 8
N_KEYS = 128
N_EXPERTS = N_KEYS * N_KEYS
PEER_TOPK = 16
PEER_DKEY = 256
PEER_DHALF = PEER_DKEY // 2
DN_ALPHA = (2 * DEPTH) ** 0.25
LN_EPS = 1e-5
RMS_EPS = 1e-6

A_Q = A_HEADS * A_HEAD_DIM
A_KV = A_KV_HEADS * A_HEAD_DIM
B_QK = B_HEADS * B_DK
B_V = B_HEADS * B_DV
SPLITS = (A_Q, A_KV, A_KV, B_QK, B_QK, B_V, B_GATE_RANK, B_V, C_WIDTH, 3 * D_MODEL)

LANES = 128
SUBLANES = 8
N_PROMPT = BATCH * SEQ
N_SAMPLE = DEC_BATCH * DEC_SEQ
N_TOK = N_PROMPT + N_SAMPLE

R_GATES = 0
R_VB = 3 * D_MODEL
R_GB = R_VB + B_V
R_UC = R_GB + B_V
R_QB = R_UC + C_WIDTH
R_KB = R_QB + B_QK
R_VA = R_KB + B_QK
R_LR = R_VA + A_KV
R_WIDTH = R_LR + LANES
QK_WIDTH = A_Q + A_KV

VMEM_LIMIT = 48 * 1024 * 1024

TM_QK = 512
TM_REST = 512
TN_REST = R_WIDTH // 2
TM_MERGE = 256
TP_POOL = 512
GLA_GROUP_PROMPT = 4
GLA_GROUP_SAMPLE = 4
TT_TOPK = 1024
TT_PEER = 512
IB_PEER = 16


def _params(sem):
    return pltpu.CompilerParams(dimension_semantics=sem, vmem_limit_bytes=VMEM_LIMIT)


def _qk_kernel(x_ref, w_ref, b_ref, c_ref, s1_ref, s2_ref, o_ref):
    y = jnp.dot(x_ref[...], w_ref[...], preferred_element_type=F32) + b_ref[...]
    c = c_ref[...]
    s1 = s1_ref[...]
    s2 = s2_ref[...]
    for j in range(QK_WIDTH // LANES):
        yj = y[:, j * LANES:(j + 1) * LANES]
        up = pltpu.roll(yj, LANES - ROT_DIM // 2, axis=1)
        dn = pltpu.roll(yj, ROT_DIM // 2, axis=1)
        o_ref[:, j * LANES:(j + 1) * LANES] = yj * c + up * s1 + dn * s2


def _proj_qk(xb, w, b, rope_c, rope_s1, rope_s2):
    n_prompt_blocks = SEQ // TM_QK

    def tab_map(i):
        return (jnp.where(i < N_PROMPT // TM_QK, i % n_prompt_blocks, n_prompt_blocks), 0)

    tab_spec = pl.BlockSpec((TM_QK, LANES), tab_map)
    return pl.pallas_call(
        _qk_kernel,
        grid=(N_TOK // TM_QK,),
        in_specs=[
            pl.BlockSpec((TM_QK, D_MODEL), lambda i: (i, 0)),
            pl.BlockSpec((D_MODEL, QK_WIDTH), lambda i: (0, 0)),
            pl.BlockSpec((1, QK_WIDTH), lambda i: (0, 0)),
            tab_spec, tab_spec, tab_spec,
        ],
        out_specs=pl.BlockSpec((TM_QK, QK_WIDTH), lambda i: (i, 0)),
        out_shape=jax.ShapeDtypeStruct((N_TOK, QK_WIDTH), F32),
        compiler_params=_params(("parallel",)),
        name="proj_qk",
    )(xb, w, b, rope_c, rope_s1, rope_s2)


def _mm_bias_kernel(x_ref, w_ref, b_ref, o_ref):
    o_ref[...] = jnp.dot(x_ref[...], w_ref[...], preferred_element_type=F32) + b_ref[...]


def _proj_rest(xb, w, b):
    return pl.pallas_call(
        _mm_bias_kernel,
        grid=(R_WIDTH // TN_REST, N_TOK // TM_REST),
        in_specs=[
            pl.BlockSpec((TM_REST, D_MODEL), lambda j, i: (i, 0)),
            pl.BlockSpec((D_MODEL, TN_REST), lambda j, i: (0, j)),
            pl.BlockSpec((1, TN_REST), lambda j, i: (0, j)),
        ],
        out_specs=pl.BlockSpec((TM_REST, TN_REST), lambda j, i: (i, j)),
        out_shape=jax.ShapeDtypeStruct((N_TOK, R_WIDTH), F32),
        compiler_params=_params(("parallel", "arbitrary")),
        name="proj_rest",
    )(xb, w, b)


HEADS_PER_PASS = 16


def _attend(q, kk, vv, sink_ref, c_min, o_ref, row0):
    tq = q.shape[0]
    r = lax.broadcasted_iota(jnp.int32, (tq, 2 * WINDOW), 0)
    c = lax.broadcasted_iota(jnp.int32, (tq, 2 * WINDOW), 1)
    ok = (c > r) & (c <= r + WINDOW) & (c >= c_min)
    qb = (q * (A_HEAD_DIM ** -0.5)).astype(BF16)
    nt = (((1,), (1,)), ((), ()))
    for h0 in range(0, A_HEADS, HEADS_PER_PASS):
        hs = range(h0, h0 + HEADS_PER_PASS)
        col = {h: slice(h * A_HEAD_DIM, (h + 1) * A_HEAD_DIM) for h in hs}
        kv = {h: slice((h // A_GROUP) * A_HEAD_DIM, (h // A_GROUP + 1) * A_HEAD_DIM) for h in hs}
        s = {h: lax.dot_general(qb[:, col[h]], kk[:, kv[h]], nt, preferred_element_type=F32) for h in hs}
        s = {h: jnp.where(ok, s[h], NEG_INF) for h in hs}
        m = {h: jnp.maximum(jnp.max(s[h], axis=1, keepdims=True), sink_ref[h]) for h in hs}
        p = {h: jnp.exp(s[h] - m[h]) for h in hs}
        denom = {h: jnp.sum(p[h], axis=1, keepdims=True) + jnp.exp(sink_ref[h] - m[h]) for h in hs}
        o = {h: jnp.dot(p[h].astype(BF16), vv[:, kv[h]], preferred_element_type=F32) / denom[h] for h in hs}
        for h in hs:
            o_ref[pl.ds(row0, tq), col[h]] = o[h].astype(o_ref.dtype)


def _attn_prompt_kernel(sink_ref, q_ref, kc_ref, kp_ref, vc_ref, vp_ref, o_ref):
    n = pl.program_id(1)
    kk = jnp.concatenate([kp_ref[...], kc_ref[...]], axis=0).astype(BF16)
    vv = jnp.concatenate([vp_ref[...], vc_ref[...]], axis=0).astype(BF16)
    _attend(q_ref[...], kk, vv, sink_ref, jnp.where(n > 0, 0, WINDOW), o_ref, 0)


def _attn_prompt(qk, rest, sinks):
    nb = SEQ // WINDOW
    kcol = A_Q // A_KV
    vcol = R_VA // A_KV

    def cur(b, n):
        return b * nb + n

    def prev(b, n):
        return b * nb + jnp.maximum(n - 1, 0)

    return pl.pallas_call(
        _attn_prompt_kernel,
        grid=(BATCH, nb),
        in_specs=[
            pl.BlockSpec(memory_space=pltpu.SMEM),
            pl.BlockSpec((WINDOW, A_Q), lambda b, n: (cur(b, n), 0)),
            pl.BlockSpec((WINDOW, A_KV), lambda b, n: (cur(b, n), kcol)),
            pl.BlockSpec((WINDOW, A_KV), lambda b, n: (prev(b, n), kcol)),
            pl.BlockSpec((WINDOW, A_KV), lambda b, n: (cur(b, n), vcol)),
            pl.BlockSpec((WINDOW, A_KV), lambda b, n: (prev(b, n), vcol)),
        ],
        out_specs=pl.BlockSpec((WINDOW, A_Q), lambda b, n: (cur(b, n), 0)),
        out_shape=jax.ShapeDtypeStruct((N_PROMPT, A_Q), BF16),
        compiler_params=_params(("parallel", "arbitrary")),
        name="attn_prompt",
    )(sinks, qk, qk, qk, rest, rest)


BB_ATTN = 8


def _attn_sample_kernel(sink_ref, q_ref, kn_ref, vn_ref, ks_ref, vs_ref, o_ref):
    pad = jnp.zeros((WINDOW - DEC_SEQ, A_KV), F32)
    rows = A_GROUP * DEC_SEQ
    t = lax.broadcasted_iota(jnp.int32, (rows, 2 * WINDOW), 0) % DEC_SEQ
    c = lax.broadcasted_iota(jnp.int32, (rows, 2 * WINDOW), 1)
    ok = (c > t) & (c <= t + WINDOW)
    nt = (((1,), (1,)), ((), ()))

    def body(pair, carry):
        chains = [(e, g) for e in range(2) for g in range(A_KV_HEADS)]
        elem = {e: pair * 2 + e for e in range(2)}
        row0 = {e: pl.multiple_of(elem[e] * DEC_SEQ, DEC_SEQ) for e in range(2)}
        q = {e: q_ref[pl.ds(row0[e], DEC_SEQ), :] * (A_HEAD_DIM ** -0.5) for e in range(2)}
        kk = {e: jnp.concatenate([ks_ref[elem[e]], kn_ref[pl.ds(row0[e], DEC_SEQ), :], pad], axis=0).astype(BF16)
              for e in range(2)}
        vv = {e: jnp.concatenate([vs_ref[elem[e]], vn_ref[pl.ds(row0[e], DEC_SEQ), :], pad], axis=0).astype(BF16)
              for e in range(2)}
        heads = {g: range(g * A_GROUP, (g + 1) * A_GROUP) for g in range(A_KV_HEADS)}
        ds = {g: slice(g * A_HEAD_DIM, (g + 1) * A_HEAD_DIM) for g in range(A_KV_HEADS)}
        sink = {g: sink_ref[g * rows:(g + 1) * rows, 0:1] for g in range(A_KV_HEADS)}
        qg = {(bb, g): jnp.concatenate([q[bb][:, h * A_HEAD_DIM:(h + 1) * A_HEAD_DIM] for h in heads[g]],
                                       axis=0).astype(BF16) for bb, g in chains}
        s = {(bb, g): lax.dot_general(qg[bb, g], kk[bb][:, ds[g]], nt, preferred_element_type=F32)
             for bb, g in chains}
        s = {ch: jnp.where(ok, s[ch], NEG_INF) for ch in chains}
        m = {(bb, g): jnp.maximum(jnp.max(s[bb, g], axis=1, keepdims=True), sink[g]) for bb, g in chains}
        p = {ch: jnp.exp(s[ch] - m[ch]) for ch in chains}
        denom = {(bb, g): jnp.sum(p[bb, g], axis=1, keepdims=True) + jnp.exp(sink[g] - m[bb, g])
                 for bb, g in chains}
        o = {(bb, g): jnp.dot(p[bb, g].astype(BF16), vv[bb][:, ds[g]], preferred_element_type=F32) / denom[bb, g]
             for bb, g in chains}
        for bb, g in chains:
            for k, h in enumerate(heads[g]):
                o_ref[pl.ds(row0[bb], DEC_SEQ), h * A_HEAD_DIM:(h + 1) * A_HEAD_DIM] = (
                    o[bb, g][k * DEC_SEQ:(k + 1) * DEC_SEQ, :].astype(o_ref.dtype))
        return carry

    lax.fori_loop(0, BB_ATTN // 2, body, 0)


def _attn_sample(qk, rest, sinks, k_state, v_state):
    rows = BB_ATTN * DEC_SEQ
    base = N_PROMPT // rows
    kcol = A_Q // A_KV
    vcol = R_VA // A_KV
    sink_rows = jnp.broadcast_to(jnp.repeat(sinks, DEC_SEQ)[:, None], (A_HEADS * DEC_SEQ, LANES))
    return pl.pallas_call(
        _attn_sample_kernel,
        grid=(DEC_BATCH // BB_ATTN,),
        in_specs=[
            pl.BlockSpec((A_HEADS * DEC_SEQ, LANES), lambda i: (0, 0)),
            pl.BlockSpec((rows, A_Q), lambda i: (base + i, 0)),
            pl.BlockSpec((rows, A_KV), lambda i: (base + i, kcol)),
            pl.BlockSpec((rows, A_KV), lambda i: (base + i, vcol)),
            pl.BlockSpec((BB_ATTN, WINDOW, A_KV), lambda i: (i, 0, 0)),
            pl.BlockSpec((BB_ATTN, WINDOW, A_KV), lambda i: (i, 0, 0)),
        ],
        out_specs=pl.BlockSpec((rows, A_Q), lambda i: (i, 0)),
        out_shape=jax.ShapeDtypeStruct((N_SAMPLE, A_Q), BF16),
        compiler_params=_params(("parallel",)),
        name="attn_sample",
    )(sink_rows, qk, qk, rest, k_state, v_state)


def _split3(x):
    hi = x.astype(BF16)
    r1 = x - hi.astype(F32)
    mid = r1.astype(BF16)
    lo = (r1 - mid.astype(F32)).astype(BF16)
    return hi, mid, lo


GLA_INPUTS = ((LANES, R_LR), (B_QK, R_QB), (B_QK, R_KB), (B_V, R_VB), (B_V, R_GB))


def _gla_kernel(*refs, n_chunks, group, chunk, layer, n_in):
    ins, rest_refs = refs[:n_in], refs[n_in:]
    per = n_in // len(GLA_INPUTS)
    s0_ref, wa_ref, ba_ref, g_ref = rest_refs[:4]
    prev_ref = rest_refs[4] if layer else None
    o_ref, sout_ref, st_ref = rest_refs[-3:]
    ci = pl.program_id(1)
    c = chunk

    def rows(inp, g):
        if per == 1:
            return ins[inp][g * c:(g + 1) * c, :]
        return ins[inp * per + g][...]

    single = n_chunks == 1
    if not single:
        @pl.when(ci == 0)
        def _():
            for g in range(group):
                for h in range(B_HEADS):
                    st_ref[g, h] = s0_ref[0, g, h].T

    ri = lax.broadcasted_iota(jnp.int32, (c, c), 0)
    cj = lax.broadcasted_iota(jnp.int32, (c, c), 1)
    causal = cj <= ri
    tri = jnp.where(causal, 1.0, 0.0).astype(BF16)
    nt = (((1,), (1,)), ((), ()))
    G = range(group)
    z = [jnp.dot(rows(0, g).astype(BF16), wa_ref[...], preferred_element_type=F32) + ba_ref[...] for g in G]
    log_a = [-(jnp.maximum(-z[g], 0.0) + jnp.log1p(jnp.exp(-jnp.abs(z[g])))) / B_TAU for g in G]
    parts = [_split3(log_a[g]) for g in G]
    b = [jnp.dot(tri, parts[g][0], preferred_element_type=F32)
         + jnp.dot(tri, parts[g][1], preferred_element_type=F32)
         + jnp.dot(tri, parts[g][2], preferred_element_type=F32) for g in G]
    bl = [b[g][c - 1:c, :] for g in G]
    qd = [(rows(1, g) * (B_DK ** -0.5) * jnp.exp(b[g])).astype(BF16) for g in G]
    kd = [(rows(2, g) * jnp.exp(-b[g])).astype(BF16) for g in G]
    kl = [rows(2, g) * jnp.exp(bl[g] - b[g]) for g in G]
    kl = [kl[g].T if single else kl[g].astype(BF16) for g in G]
    ebl = [jnp.exp(bl[g]) for g in G]
    for h in range(B_HEADS):
        ks = slice(h * B_DK, (h + 1) * B_DK)
        vs = slice(h * B_DV, (h + 1) * B_DV)
        vh = [rows(3, g)[:, vs] for g in G]
        if single:
            st = [s0_ref[0, g, h] for g in G]
            o = [jnp.dot(qd[g][:, ks], st[g].astype(BF16), preferred_element_type=F32) for g in G]
        else:
            st = [st_ref[g, h] for g in G]
            o = [lax.dot_general(qd[g][:, ks], st[g].astype(BF16), nt, preferred_element_type=F32) for g in G]
        att = [lax.dot_general(qd[g][:, ks], kd[g][:, ks], nt, preferred_element_type=F32) for g in G]
        att = [jnp.where(causal, att[g], 0.0).astype(BF16) for g in G]
        o = [o[g] + jnp.dot(att[g], vh[g].astype(BF16), preferred_element_type=F32) for g in G]
        for g in G:
            if single:
                decay = jnp.broadcast_to(ebl[g][:, ks], (SUBLANES, B_DK)).T[:, 0:1]
                for l in range(layer):
                    sout_ref[l, g, h] = prev_ref[l, g, h]
                sout_ref[layer, g, h] = st[g] * decay + jnp.dot(kl[g][ks, :], vh[g],
                                                                  preferred_element_type=F32)
            else:
                st_ref[g, h] = st[g] * ebl[g][:, ks] + jnp.dot(vh[g].T.astype(BF16), kl[g][:, ks],
                                                               preferred_element_type=F32)
        o = [o[g] * lax.rsqrt(jnp.mean(o[g] * o[g], axis=1, keepdims=True) + RMS_EPS) * g_ref[h:h + 1, :]
             for g in G]
        for g in G:
            gate = rows(4, g)[:, vs]
            o_ref[g, :, vs] = (o[g] * (gate / (1.0 + jnp.exp(-gate)))).astype(o_ref.dtype)

    if not single:
        @pl.when(ci == n_chunks - 1)
        def _():
            for g in range(group):
                for l in range(layer):
                    sout_ref[l, g] = prev_ref[l, g]
                for h in range(B_HEADS):
                    sout_ref[layer, g, h] = st_ref[g, h].T


def _gla(rest, s0, layer, prev_states, wa, ba, gain, *, n_batch, seq, chunk, row_base, group):
    n_chunks = seq // chunk
    contiguous = n_chunks == 1
    base = row_base // chunk

    def in_specs_for(width, col):
        if contiguous:
            return [pl.BlockSpec((group * chunk, width), lambda b, ci: (base // group + b, col // width))]
        return [pl.BlockSpec((chunk, width),
                             lambda b, ci, g=g: (base + (b * group + g) * n_chunks + ci, col // width))
                for g in range(group)]

    row_specs = [spec for width, col in GLA_INPUTS for spec in in_specs_for(width, col)]
    s0_layer = layer if s0.shape[0] > 1 else 0
    state_block = (group, B_HEADS, B_DK, B_DV)
    prev_specs = [pl.BlockSpec((layer,) + state_block, lambda b, ci: (0, b, 0, 0, 0))] if layer else []
    prev_args = [prev_states] if layer else []
    return pl.pallas_call(
        functools.partial(_gla_kernel, n_chunks=n_chunks, group=group, chunk=chunk, layer=layer,
                          n_in=len(row_specs)),
        grid=(n_batch // group, n_chunks),
        in_specs=row_specs + [
            pl.BlockSpec((1,) + state_block, lambda b, ci: (s0_layer, b, 0, 0, 0)),
            pl.BlockSpec((LANES, B_QK), lambda b, ci: (0, 0)),
            pl.BlockSpec((1, B_QK), lambda b, ci: (0, 0)),
            pl.BlockSpec((B_HEADS, B_DV), lambda b, ci: (0, 0)),
        ] + prev_specs,
        out_specs=[
            pl.BlockSpec((group, chunk, B_V), lambda b, ci: (b, ci, 0)),
            pl.BlockSpec((layer + 1,) + state_block, lambda b, ci: (0, b, 0, 0, 0)),
        ],
        out_shape=[
            jax.ShapeDtypeStruct((n_batch, seq, B_V), BF16),
            jax.ShapeDtypeStruct((layer + 1, n_batch, B_HEADS, B_DK, B_DV), F32),
        ],
        scratch_shapes=[pltpu.VMEM((group, B_HEADS, B_DV, B_DK), F32)],
        compiler_params=_params(("parallel", "arbitrary")),
        name="gla",
    )(*([rest] * len(row_specs)), s0, wa, ba, gain, *prev_args)


HALO = 16
BB_POOL = 16


def _pool_kernel(u_ref, prev_ref, w_ref, scale_ref, o_ref, *, from_start, n_seq):
    tp = u_ref.shape[0] // n_seq
    if from_start:
        ti = pl.program_id(1)
        t0 = ti * tp
    diffs = [[] for _ in POOL_WINDOWS]
    for sq in range(n_seq):
        u = u_ref[sq * tp:(sq + 1) * tp, :]
        prev = prev_ref[sq * HALO:(sq + 1) * HALO, :]
        if from_start:
            prev = jnp.where(ti > 0, prev, 0.0)
        full = jnp.concatenate([prev, u], axis=0)
        for g, w in enumerate(POOL_WINDOWS):
            cs = slice(g * C_GROUP_W, (g + 1) * C_GROUP_W)
            acc = full[:, cs]
            span = 1
            while span < w:
                acc = acc + pltpu.roll(acc, span, axis=0)
                span *= 2
            wsum = acc[HALO:, :]
            if from_start:
                t = t0 + lax.broadcasted_iota(jnp.int32, (tp, C_GROUP_W), 0)
                cnt = jnp.minimum(t + 1, w).astype(F32)
            else:
                cnt = float(w)
            diffs[g].append(wsum / cnt - u[:, cs])
    for g in range(C_GROUPS):
        cs = slice(g * C_GROUP_W, (g + 1) * C_GROUP_W)
        d = jnp.concatenate(diffs[g], axis=0) if n_seq > 1 else diffs[g][0]
        y = jnp.dot(d.astype(BF16), w_ref[g], preferred_element_type=F32) * scale_ref[:, cs]
        o_ref[:, cs] = y.astype(o_ref.dtype)


def _pool_prompt(rest, w, scale):
    nt = SEQ // TP_POOL
    ucol = R_UC // C_WIDTH

    def halo(b, i):
        return (jnp.maximum((b * SEQ + i * TP_POOL) // HALO - 1, 0), ucol)

    return pl.pallas_call(
        functools.partial(_pool_kernel, from_start=True, n_seq=1),
        grid=(BATCH, nt),
        in_specs=[
            pl.BlockSpec((TP_POOL, C_WIDTH), lambda b, i: (b * nt + i, ucol)),
            pl.BlockSpec((HALO, C_WIDTH), halo),
            pl.BlockSpec((C_GROUPS, C_GROUP_W, C_GROUP_W), lambda b, i: (0, 0, 0)),
            pl.BlockSpec((1, C_WIDTH), lambda b, i: (0, 0)),
        ],
        out_specs=pl.BlockSpec((TP_POOL, C_WIDTH), lambda b, i: (b * nt + i, 0)),
        out_shape=jax.ShapeDtypeStruct((N_PROMPT, C_WIDTH), BF16),
        compiler_params=_params(("parallel", "arbitrary")),
        name="pool_prompt",
    )(rest, rest, w, scale)


def _pool_sample(rest, prev, w, scale):
    ucol = R_UC // C_WIDTH
    rows = BB_POOL * DEC_SEQ
    base = N_PROMPT // rows
    return pl.pallas_call(
        functools.partial(_pool_kernel, from_start=False, n_seq=BB_POOL),
        grid=(DEC_BATCH // BB_POOL,),
        in_specs=[
            pl.BlockSpec((rows, C_WIDTH), lambda b: (base + b, ucol)),
            pl.BlockSpec((BB_POOL * HALO, C_WIDTH), lambda b: (b, 0)),
            pl.BlockSpec((C_GROUPS, C_GROUP_W, C_GROUP_W), lambda b: (0, 0, 0)),
            pl.BlockSpec((1, C_WIDTH), lambda b: (0, 0)),
        ],
        out_specs=pl.BlockSpec((rows, C_WIDTH), lambda b: (b, 0)),
        out_shape=jax.ShapeDtypeStruct((N_SAMPLE, C_WIDTH), BF16),
        compiler_params=_params(("parallel",)),
        name="pool_sample",
    )(rest, prev, w, scale)


def _layer_norm(x, g, b):
    mu = jnp.mean(x, axis=1, keepdims=True)
    xc = x - mu
    var = jnp.mean(xc * xc, axis=1, keepdims=True)
    return xc * lax.rsqrt(var + LN_EPS) * g + b


def _merge_kernel(x_ref, gates_ref, oa_p_ref, oa_s_ref, ob_p_ref, ob_s_ref, oc_p_ref, oc_s_ref,
                  wa_ref, wb_ref, wc_ref, wo_ref, g_ref, b_ref, o_ref, oT_ref):
    in_sample = pl.program_id(0) >= N_PROMPT // TM_MERGE

    def gate(i):
        z = gates_ref[:, i * D_MODEL:(i + 1) * D_MODEL]
        return 1.0 / (1.0 + jnp.exp(-z))

    def branch(p_ref, s_ref, w_ref):
        o = jnp.where(in_sample, s_ref[...], p_ref[...])
        return jnp.dot(o, w_ref[...], preferred_element_type=F32)

    merged = (gate(0) * branch(oa_p_ref, oa_s_ref, wa_ref)
              + gate(1) * branch(ob_p_ref, ob_s_ref, wb_ref)
              + gate(2) * branch(oc_p_ref, oc_s_ref, wc_ref))
    mix = jnp.dot(merged.astype(BF16), wo_ref[...], preferred_element_type=F32)
    y = _layer_norm(DN_ALPHA * x_ref[...] + mix, g_ref[...], b_ref[...])
    o_ref[...] = y
    oT_ref[...] = y.T.astype(BF16)


def _merge(x, rest, branches, wa, wb, wc, wo, g, b):
    n_p = N_PROMPT // TM_MERGE
    row = pl.BlockSpec((TM_MERGE, D_MODEL), lambda i: (i, 0))
    row_p = pl.BlockSpec((TM_MERGE, D_MODEL), lambda i: (jnp.minimum(i, n_p - 1), 0))
    row_s = pl.BlockSpec((TM_MERGE, D_MODEL), lambda i: (jnp.maximum(i - n_p, 0), 0))
    wspec = pl.BlockSpec((D_MODEL, D_MODEL), lambda i: (0, 0))
    vec = pl.BlockSpec((1, D_MODEL), lambda i: (0, 0))
    (oa_p, oa_s), (ob_p, ob_s), (oc_p, oc_s) = branches
    return pl.pallas_call(
        _merge_kernel,
        grid=(N_TOK // TM_MERGE,),
        in_specs=[row, pl.BlockSpec((TM_MERGE, 3 * D_MODEL), lambda i: (i, R_GATES)),
                  row_p, row_s, row_p, row_s, row_p, row_s, wspec, wspec, wspec, wspec, vec, vec],
        out_specs=[row, pl.BlockSpec((D_MODEL, TM_MERGE), lambda i: (0, i))],
        out_shape=[jax.ShapeDtypeStruct((N_TOK, D_MODEL), F32),
                   jax.ShapeDtypeStruct((D_MODEL, N_TOK), BF16)],
        compiler_params=_params(("parallel",)),
        name="merge",
    )(x, rest, oa_p, oa_s, ob_p, ob_s, oc_p, oc_s, wa, wb, wc, wo, g, b)


def _extract_desc(s, n):
    vals = []
    for _ in range(n):
        m = jnp.max(s, axis=0, keepdims=True)
        vals.append(m)
        s = jnp.where(s == m, -jnp.inf, s)
    return vals


def _merge_sort_pairs(n):
    pairs = []
    p = 1
    while p < n:
        k = p
        while k >= 1:
            for j in range(k % p, n - k, 2 * k):
                for i in range(min(k, n - j - k)):
                    if (i + j) // (2 * p) == (i + j + k) // (2 * p):
                        pairs.append((i + j, i + j + k))
            k //= 2
        p *= 2
    return pairs


def _top_desc(s, n):
    v = [s[SUBLANES * k:SUBLANES * (k + 1), :] for k in range(s.shape[0] // SUBLANES)]
    for a, b in _merge_sort_pairs(len(v)):
        v[a], v[b] = jnp.maximum(v[a], v[b]), jnp.minimum(v[a], v[b])
    vals = []
    for r in range(n):
        m = jnp.max(v[0], axis=0, keepdims=True)
        vals.append(m)
        hit = v[0] == m
        for k in range(n - r - 1):
            v[k] = jnp.where(hit, v[k + 1], v[k])
    return vals


def _rank_of(s, vals):
    rank = jnp.full(s.shape, float(len(vals)), F32)
    for r, val in enumerate(vals):
        rank = jnp.where(s == val, float(r), rank)
    return rank


def _peer_topk_kernel(xT_ref, wq_ref, sk_ref, cnt_ref, e1_ref, rank_ref, e2_ref, s1_scr, sv_scr):
    tt = xT_ref.shape[1]
    xT = xT_ref[...]
    for hp in range(2 * PEER_HEADS):
        h, second = divmod(hp, 2)
        qT = jnp.dot(wq_ref[hp * PEER_DHALF:(hp + 1) * PEER_DHALF, :], xT, preferred_element_type=F32)
        s = jnp.dot(sk_ref[hp], qT.astype(BF16), preferred_element_type=F32)
        vals = _top_desc(s, PEER_TOPK)
        sv_scr[hp] = jnp.concatenate(vals, axis=0)
        if second:
            rank_ref[h] = _rank_of(s, vals).astype(BF16)
            e2_ref[h] = jnp.exp(s - vals[0]).astype(BF16)
        else:
            s1_scr[h] = s
    row8 = lax.broadcasted_iota(jnp.int32, (8, tt), 0)
    for h in range(PEER_HEADS):
        sv1 = sv_scr[2 * h]
        sv2 = sv_scr[2 * h + 1]
        pieces = [sv1[0:1, :] + sv2]
        for a in range(1, 8):
            nb = PEER_TOPK // (a + 1)
            pieces.append(jnp.where(row8 < nb, sv1[a:a + 1, :] + sv2[0:8, :], -jnp.inf))
        pieces.append(sv2[0:1, :] + sv1[8:16, :])
        cand = _extract_desc(jnp.concatenate(pieces, axis=0), PEER_TOPK)
        top = cand[0]
        z = jnp.zeros_like(top)
        for r in range(PEER_TOPK):
            z = z + jnp.exp(cand[r] - top)
        kth = cand[PEER_TOPK - 1]
        counts = [jnp.sum(jnp.where(p >= kth, 1.0, 0.0), axis=0, keepdims=True) for p in pieces[:8]]
        tail = jnp.where(pieces[8] >= kth, 1.0, 0.0)
        counts += [tail[a:a + 1, :] for a in range(8)]
        s1 = s1_scr[h]
        cnt = jnp.zeros_like(s1)
        for a in range(PEER_TOPK):
            cnt = jnp.where(s1 == sv1[a:a + 1, :], counts[a], cnt)
        e1 = jnp.exp(s1 - (sv1[0:1, :] + jnp.log(z))) * SQRT_HALF
        for c in range(tt // LANES):
            cnt_ref[h, c] = cnt[:, c * LANES:(c + 1) * LANES]
            e1_ref[h, c] = e1[:, c * LANES:(c + 1) * LANES]


def _peer_topk(xT, wqT, sk):
    spec = pl.BlockSpec((PEER_HEADS, N_KEYS, TT_TOPK), lambda t: (0, 0, t))
    row_spec = pl.BlockSpec((PEER_HEADS, TT_TOPK // LANES, N_KEYS, LANES), lambda t: (0, t, 0, 0))
    wide = jax.ShapeDtypeStruct((PEER_HEADS, N_TOK // LANES, N_KEYS, LANES), F32)
    narrow = jax.ShapeDtypeStruct((PEER_HEADS, N_KEYS, N_TOK), BF16)
    return pl.pallas_call(
        _peer_topk_kernel,
        grid=(N_TOK // TT_TOPK,),
        in_specs=[
            pl.BlockSpec((D_MODEL, TT_TOPK), lambda t: (0, t)),
            pl.BlockSpec((PEER_HEADS * PEER_DKEY, D_MODEL), lambda t: (0, 0)),
            pl.BlockSpec((2 * PEER_HEADS, N_KEYS, PEER_DHALF), lambda t: (0, 0, 0)),
        ],
        out_specs=[row_spec, row_spec, spec, spec],
        out_shape=[wide, wide, narrow, narrow],
        scratch_shapes=[pltpu.VMEM((PEER_HEADS, N_KEYS, TT_TOPK), F32),
                        pltpu.VMEM((2 * PEER_HEADS, PEER_TOPK, TT_TOPK), F32)],
        compiler_params=_params(("parallel",)),
        name="peer_topk",
    )(xT, wqT, sk)


SQRT_HALF = 2.0 ** -0.5


def _gelu_unscaled(x):
    t = x * SQRT_HALF
    return t * (1.0 + lax.erf(t))


LC_PEER = 256


def _bf16_rows(ref, h, r, chunks, n_rows):
    x = jnp.concatenate([ref[h, c, pl.ds(r, 2 * SUBLANES, stride=0), :] for c in chunks], axis=1)
    packed = x.astype(BF16)
    return jnp.concatenate([packed] * (n_rows // packed.shape[0]), axis=0)


MM_PIECE = 512


def _peer_main_kernel(xT_ref, u_ref, vt_ref, cnt_ref, e1_ref, rank_ref, e2_ref, x_ref, g_ref, b_ref,
                      o_ref, ob_ref, yT_ref, s_scr, wh_scr):
    i = pl.program_id(1)
    tt = xT_ref.shape[1]

    @pl.when(i == 0)
    def _():
        yT_ref[...] = jnp.zeros_like(yT_ref)

    zero = jnp.zeros((N_KEYS, LC_PEER), BF16)
    per_piece = MM_PIECE // N_KEYS
    for ii in range(IB_PEER):
        rows = slice(ii * N_KEYS, (ii + 1) * N_KEYS)
        if ii % per_piece == 0:
            piece = slice(ii * N_KEYS, ii * N_KEYS + MM_PIECE)
            s_scr[piece, :] = jnp.dot(u_ref[0, piece, :], xT_ref[...], preferred_element_type=F32)
        for lc in range(tt // LC_PEER):
            cols = slice(lc * LC_PEER, (lc + 1) * LC_PEER)
            chunks = range(lc * LC_PEER // LANES, (lc + 1) * LC_PEER // LANES)
            w = zero
            for h in range(PEER_HEADS):
                cnt = _bf16_rows(cnt_ref, h, ii, chunks, N_KEYS)
                e1 = _bf16_rows(e1_ref, h, ii, chunks, N_KEYS)
                w = w + jnp.where(rank_ref[h, :, cols] < cnt, e2_ref[h, :, cols] * e1, zero)
            wh_scr[rows, cols] = w
    for ii in range(IB_PEER):
        rows = slice(ii * N_KEYS, (ii + 1) * N_KEYS)
        wh_scr[rows, :] = wh_scr[rows, :] * _gelu_unscaled(s_scr[rows, :]).astype(BF16)
    yT_ref[...] += jnp.dot(vt_ref[0], wh_scr[...], preferred_element_type=F32)

    @pl.when(i == pl.num_programs(1) - 1)
    def _():
        y = _layer_norm(DN_ALPHA * x_ref[...] + yT_ref[...].T, g_ref[...], b_ref[...])
        o_ref[...] = y
        ob_ref[...] = y.astype(BF16)


def _peer_main(xT, u, vt, layer, cnt, e1, rank, e2, x1, g, b):
    eb = IB_PEER * N_KEYS
    row_spec = pl.BlockSpec((PEER_HEADS, TT_PEER // LANES, IB_PEER, LANES), lambda t, i: (0, t, i, 0))
    tab_spec = pl.BlockSpec((PEER_HEADS, N_KEYS, TT_PEER), lambda t, i: (0, 0, t))
    tok_spec = pl.BlockSpec((TT_PEER, D_MODEL), lambda t, i: (t, 0))
    vec = pl.BlockSpec((1, D_MODEL), lambda t, i: (0, 0))
    return pl.pallas_call(
        _peer_main_kernel,
        grid=(N_TOK // TT_PEER, N_EXPERTS // eb),
        in_specs=[
            pl.BlockSpec((D_MODEL, TT_PEER), lambda t, i: (0, t)),
            pl.BlockSpec((1, eb, D_MODEL), lambda t, i: (layer, i, 0)),
            pl.BlockSpec((1, D_MODEL, eb), lambda t, i: (layer, 0, i)),
            row_spec, row_spec, tab_spec, tab_spec, tok_spec, vec, vec,
        ],
        out_specs=[tok_spec, tok_spec],
        out_shape=[jax.ShapeDtypeStruct((N_TOK, D_MODEL), F32),
                   jax.ShapeDtypeStruct((N_TOK, D_MODEL), BF16)],
        scratch_shapes=[pltpu.VMEM((D_MODEL, TT_PEER), F32),
                        pltpu.VMEM((eb, TT_PEER), F32), pltpu.VMEM((eb, TT_PEER), BF16)],
        compiler_params=_params(("parallel", "arbitrary")),
        name="peer_main",
    )(xT, u, vt, cnt, e1, rank, e2, x1, g, b)


def _rope_tables():
    half = ROT_DIM // 2
    pos = jnp.concatenate([jnp.arange(SEQ), PAST_LEN + (jnp.arange(TM_QK) % DEC_SEQ)])
    inv = ROPE_THETA ** (-jnp.arange(half, dtype=F32) / half)
    ang = pos.astype(F32)[:, None] * inv[None, :]
    cos, sin = jnp.cos(ang), jnp.sin(ang)
    n = pos.shape[0]
    one = jnp.ones((n, A_HEAD_DIM - ROT_DIM), F32)
    zero = jnp.zeros((n, A_HEAD_DIM - ROT_DIM), F32)
    zh = jnp.zeros((n, half), F32)
    reps = LANES // A_HEAD_DIM
    c = jnp.tile(jnp.concatenate([cos, cos, one], 1), (1, reps))
    s1 = jnp.tile(jnp.concatenate([-sin, zh, zero], 1), (1, reps))
    s2 = jnp.tile(jnp.concatenate([zh, sin, zero], 1), (1, reps))
    return c, s1, s2


def _split_cols(w):
    cuts = [int(c) for c in np.cumsum(SPLITS)[:-1]]
    return jnp.split(w, cuts, axis=-1)


def _layer(layer, x, xb, k_state, v_state, gla_states, gla_out, pool_state, rope, w_in, b_in, sinks,
           w_alpha, b_alpha, gla_g, w_pool, pool_scale, w_a, w_b, w_c, w_out, ln1_g, ln1_b,
           peer_query, peer_subkeys, peer_u, peer_vt, ln2_g, ln2_b):
    qa_w, ka_w, va_w, qb_w, kb_w, vb_w, lr_w, gb_w, uc_w, gates_w = _split_cols(w_in)
    qa_b, ka_b, va_b, qb_b, kb_b, vb_b, lr_b, gb_b, uc_b, gates_b = _split_cols(b_in[None, :])
    lr_pad = LANES - B_GATE_RANK
    w_qk = jnp.concatenate([qa_w, ka_w], 1).astype(BF16)
    b_qk = jnp.concatenate([qa_b, ka_b], 1)
    w_rest = jnp.concatenate([gates_w, vb_w, gb_w, uc_w, qb_w, kb_w, va_w,
                              jnp.pad(lr_w, ((0, 0), (0, lr_pad)))], 1).astype(BF16)
    b_rest = jnp.concatenate([gates_b, vb_b, gb_b, uc_b, qb_b, kb_b, va_b,
                              jnp.pad(lr_b, ((0, 0), (0, lr_pad)))], 1)

    qk = _proj_qk(xb, w_qk, b_qk, *rope)
    rest = _proj_rest(xb, w_rest, b_rest)

    ks = k_state.reshape(DEC_BATCH, WINDOW, A_KV)
    vs = v_state.reshape(DEC_BATCH, WINDOW, A_KV)
    oa = (_attn_prompt(qk, rest, sinks), _attn_sample(qk, rest, sinks, ks, vs))

    wa = jnp.pad(w_alpha, ((0, lr_pad), (0, 0))).astype(BF16)
    ba = b_alpha[None, :]
    gla_p, gla_s = gla_out
    ob_p, gla_p = _gla(rest, jnp.zeros((1, BATCH, B_HEADS, B_DK, B_DV), F32), layer, gla_p, wa, ba, gla_g,
                       n_batch=BATCH, seq=SEQ, chunk=GLA_CHUNK, row_base=0, group=GLA_GROUP_PROMPT)
    ob_s, gla_s = _gla(rest, gla_states, layer, gla_s, wa, ba, gla_g, n_batch=DEC_BATCH, seq=DEC_SEQ,
                       chunk=math.gcd(DEC_SEQ, GLA_CHUNK), row_base=N_PROMPT, group=GLA_GROUP_SAMPLE)
    ob = (ob_p.reshape(N_PROMPT, B_V), ob_s.reshape(N_SAMPLE, B_V))

    wp = w_pool.astype(BF16)
    ps = pool_scale[None, :]
    prev = jnp.pad(pool_state, ((0, 0), (HALO - POOL_STATE, 0), (0, 0))).reshape(DEC_BATCH * HALO, C_WIDTH)
    oc = (_pool_prompt(rest, wp, ps), _pool_sample(rest, prev, wp, ps))

    x1, x1T = _merge(x, rest, (oa, ob, oc), w_a.astype(BF16), w_b.astype(BF16), w_c.astype(BF16),
                     w_out.astype(BF16), ln1_g[None, :], ln1_b[None, :])

    wqT = peer_query.reshape(D_MODEL, PEER_HEADS * PEER_DKEY).T.astype(BF16)
    sk = peer_subkeys.reshape(2 * PEER_HEADS, N_KEYS, PEER_DHALF).astype(BF16)
    cnt, e1, rank, e2 = _peer_topk(x1T, wqT, sk)
    x2, x2b = _peer_main(x1T, peer_u, peer_vt, layer, cnt, e1, rank, e2, x1, ln2_g[None, :], ln2_b[None, :])

    def prompt_tail(t, col0, width, n):
        return jnp.stack([t[(b + 1) * SEQ - n:(b + 1) * SEQ, col0:col0 + width] for b in range(BATCH)])

    def sample_tail(state, t, col0, width, n):
        new = t[N_PROMPT:, col0:col0 + width].reshape(DEC_BATCH, DEC_SEQ, width)
        return jnp.concatenate([state, new], 1)[:, -n:]

    kv_shape = (-1, WINDOW, A_KV_HEADS, A_HEAD_DIM)
    states = (prompt_tail(qk, A_Q, A_KV, WINDOW).reshape(kv_shape),
              prompt_tail(rest, R_VA, A_KV, WINDOW).reshape(kv_shape),
              prompt_tail(rest, R_UC, C_WIDTH, POOL_STATE),
              sample_tail(ks, qk, A_Q, A_KV, WINDOW).reshape(kv_shape),
              sample_tail(vs, rest, R_VA, A_KV, WINDOW).reshape(kv_shape),
              sample_tail(pool_state, rest, R_UC, C_WIDTH, POOL_STATE))
    return x2, x2b, states, (gla_p, gla_s)


def kernel(x_prompt, x_sample, state_win_k, state_win_v, state_gla, state_pool, w_in, b_in, attn_sinks,
           w_alpha, b_alpha, gla_norm_g, w_pool, pool_scale, w_branch_a, w_branch_b, w_branch_c, w_out,
           ln1_g, ln1_b, peer_query, peer_subkeys, peer_u, peer_v, ln2_g, ln2_b):
    x = jnp.concatenate([x_prompt.reshape(N_PROMPT, D_MODEL), x_sample.reshape(N_SAMPLE, D_MODEL)], 0)
    xb = x.astype(BF16)
    rope = _rope_tables()
    peer_ub = peer_u.astype(BF16)
    peer_vtb = jnp.swapaxes(peer_v, 1, 2).astype(BF16)
    per_layer = []
    gla_out = (None, None)
    for l in range(DEPTH):
        x, xb, states, gla_out = _layer(
            l, x, xb, state_win_k[l], state_win_v[l], state_gla, gla_out, state_pool[l], rope,
            w_in[l], b_in[l], attn_sinks[l], w_alpha[l], b_alpha[l], gla_norm_g[l],
            w_pool[l], pool_scale[l], w_branch_a[l], w_branch_b[l], w_branch_c[l], w_out[l],
            ln1_g[l], ln1_b[l], peer_query[l], peer_subkeys[l], peer_ub, peer_vtb, ln2_g[l], ln2_b[l])
        per_layer.append(states)
    pk, pv, pp, sk, sv, sp = [jnp.stack([per_layer[l][i] for l in range(DEPTH)]) for i in range(6)]
    return (x[:N_PROMPT].reshape(BATCH, SEQ, D_MODEL), x[N_PROMPT:].reshape(DEC_BATCH, DEC_SEQ, D_MODEL),
            pk, pv, gla_out[0], pp, sk, sv, gla_out[1], sp)
```

```python
import functools
import math

import jax
import jax.numpy as jnp
import numpy as np
from jax import lax
from jax.experimental import pallas as pl
from jax.experimental.pallas import tpu as pltpu

F32 = jnp.float32
BF16 = jnp.bfloat16

D_MODEL = 1024
BATCH = 8
SEQ = 2048
DEPTH = 2
DEC_BATCH = 128
DEC_SEQ = 8
PAST_LEN = 16384

A_HEADS = 16
A_KV_HEADS = 2
A_HEAD_DIM = 64
A_GROUP = A_HEADS // A_KV_HEADS
WINDOW = 128
ROT_DIM = A_HEAD_DIM // 4
ROPE_THETA = 500000.0
NEG_INF = -1e30
B_HEADS = 4
B_DK = D_MODEL // 2 // B_HEADS
B_DV = D_MODEL // B_HEADS
B_GATE_RANK = 16
B_TAU = 16.0
GLA_CHUNK = 64
POOL_WINDOWS = (2, 4, 8, 16)
C_GROUPS = len(POOL_WINDOWS)
C_GROUP_W = D_MODEL // C_GROUPS
C_WIDTH = C_GROUPS * C_GROUP_W
POOL_STATE = max(POOL_WINDOWS) - 1
PEER_HEADS = 8
N_KEYS = 128
N_EXPERTS = N_KEYS * N_KEYS
PEER_TOPK = 16
PEER_DKEY = 256
PEER_DHALF = PEER_DKEY // 2
DN_ALPHA = (2 * DEPTH) ** 0.25
LN_EPS = 1e-5
RMS_EPS = 1e-6

A_Q = A_HEADS * A_HEAD_DIM
A_KV = A_KV_HEADS * A_HEAD_DIM
B_QK = B_HEADS * B_DK
B_V = B_HEADS * B_DV
SPLITS = (A_Q, A_KV, A_KV, B_QK, B_QK, B_V, B_GATE_RANK, B_V, C_WIDTH, 3 * D_MODEL)

LANES = 128
SUBLANES = 8
N_PROMPT = BATCH * SEQ
N_SAMPLE = DEC_BATCH * DEC_SEQ
N_TOK = N_PROMPT + N_SAMPLE

R_GATES = 0
R_VB = 3 * D_MODEL
R_GB = R_VB + B_V
R_UC = R_GB + B_V
R_QB = R_UC + C_WIDTH
R_KB = R_QB + B_QK
R_VA = R_KB + B_QK
R_LR = R_VA + A_KV
R_WIDTH = R_LR + LANES
QK_WIDTH = A_Q + A_KV

VMEM_LIMIT = 48 * 1024 * 1024

TM_QK = 512
TM_REST = 512
TN_REST = R_WIDTH // 2
TM_MERGE = 256
TP_POOL = 512
GLA_GROUP_PROMPT = 4
GLA_GROUP_SAMPLE = 4
TT_TOPK = 1024
TT_PEER = 512
IB_PEER = 16


def _params(sem):
    return pltpu.CompilerParams(dimension_semantics=sem, vmem_limit_bytes=VMEM_LIMIT)


def _qk_kernel(x_ref, w_ref, b_ref, c_ref, s1_ref, s2_ref, o_ref):
    y = jnp.dot(x_ref[...], w_ref[...], preferred_element_type=F32) + b_ref[...]
    c = c_ref[...]
    s1 = s1_ref[...]
    s2 = s2_ref[...]
    for j in range(QK_WIDTH // LANES):
        yj = y[:, j * LANES:(j + 1) * LANES]
        up = pltpu.roll(yj, LANES - ROT_DIM // 2, axis=1)
        dn = pltpu.roll(yj, ROT_DIM // 2, axis=1)
        o_ref[:, j * LANES:(j + 1) * LANES] = yj * c + up * s1 + dn * s2


def _proj_qk(xb, w, b, rope_c, rope_s1, rope_s2):
    n_prompt_blocks = SEQ // TM_QK

    def tab_map(i):
        return (jnp.where(i < N_PROMPT // TM_QK, i % n_prompt_blocks, n_prompt_blocks), 0)

    tab_spec = pl.BlockSpec((TM_QK, LANES), tab_map)
    return pl.pallas_call(
        _qk_kernel,
        grid=(N_TOK // TM_QK,),
        in_specs=[
            pl.BlockSpec((TM_QK, D_MODEL), lambda i: (i, 0)),
            pl.BlockSpec((D_MODEL, QK_WIDTH), lambda i: (0, 0)),
            pl.BlockSpec((1, QK_WIDTH), lambda i: (0, 0)),
            tab_spec, tab_spec, tab_spec,
        ],
        out_specs=pl.BlockSpec((TM_QK, QK_WIDTH), lambda i: (i, 0)),
        out_shape=jax.ShapeDtypeStruct((N_TOK, QK_WIDTH), F32),
        compiler_params=_params(("parallel",)),
        name="proj_qk",
    )(xb, w, b, rope_c, rope_s1, rope_s2)


def _mm_bias_kernel(x_ref, w_ref, b_ref, o_ref):
    o_ref[...] = jnp.dot(x_ref[...], w_ref[...], preferred_element_type=F32) + b_ref[...]


def _proj_rest(xb, w, b):
    return pl.pallas_call(
        _mm_bias_kernel,
        grid=(R_WIDTH // TN_REST, N_TOK // TM_REST),
        in_specs=[
            pl.BlockSpec((TM_REST, D_MODEL), lambda j, i: (i, 0)),
            pl.BlockSpec((D_MODEL, TN_REST), lambda j, i: (0, j)),
            pl.BlockSpec((1, TN_REST), lambda j, i: (0, j)),
        ],
        out_specs=pl.BlockSpec((TM_REST, TN_REST), lambda j, i: (i, j)),
        out_shape=jax.ShapeDtypeStruct((N_TOK, R_WIDTH), F32),
        compiler_params=_params(("parallel", "arbitrary")),
        name="proj_rest",
    )(xb, w, b)


HEADS_PER_PASS = 16


def _attend(q, kk, vv, sink_ref, c_min, o_ref, row0):
    tq = q.shape[0]
    r = lax.broadcasted_iota(jnp.int32, (tq, 2 * WINDOW), 0)
    c = lax.broadcasted_iota(jnp.int32, (tq, 2 * WINDOW), 1)
    ok = (c > r) & (c <= r + WINDOW) & (c >= c_min)
    qb = (q * (A_HEAD_DIM ** -0.5)).astype(BF16)
    nt = (((1,), (1,)), ((), ()))
    for h0 in range(0, A_HEADS, HEADS_PER_PASS):
        hs = range(h0, h0 + HEADS_PER_PASS)
        col = {h: slice(h * A_HEAD_DIM, (h + 1) * A_HEAD_DIM) for h in hs}
        kv = {h: slice((h // A_GROUP) * A_HEAD_DIM, (h // A_GROUP + 1) * A_HEAD_DIM) for h in hs}
        s = {h: lax.dot_general(qb[:, col[h]], kk[:, kv[h]], nt, preferred_element_type=F32) for h in hs}
        s = {h: jnp.where(ok, s[h], NEG_INF) for h in hs}
        m = {h: jnp.maximum(jnp.max(s[h], axis=1, keepdims=True), sink_ref[h]) for h in hs}
        p = {h: jnp.exp(s[h] - m[h]) for h in hs}
        denom = {h: jnp.sum(p[h], axis=1, keepdims=True) + jnp.exp(sink_ref[h] - m[h]) for h in hs}
        o = {h: jnp.dot(p[h].astype(BF16), vv[:, kv[h]], preferred_element_type=F32) / denom[h] for h in hs}
        for h in hs:
            o_ref[pl.ds(row0, tq), col[h]] = o[h].astype(o_ref.dtype)


def _attn_prompt_kernel(sink_ref, q_ref, kc_ref, kp_ref, vc_ref, vp_ref, o_ref):
    n = pl.program_id(1)
    kk = jnp.concatenate([kp_ref[...], kc_ref[...]], axis=0).astype(BF16)
    vv = jnp.concatenate([vp_ref[...], vc_ref[...]], axis=0).astype(BF16)
    _attend(q_ref[...], kk, vv, sink_ref, jnp.where(n > 0, 0, WINDOW), o_ref, 0)


def _attn_prompt(qk, rest, sinks):
    nb = SEQ // WINDOW
    kcol = A_Q // A_KV
    vcol = R_VA // A_KV

    def cur(b, n):
        return b * nb + n

    def prev(b, n):
        return b * nb + jnp.maximum(n - 1, 0)

    return pl.pallas_call(
        _attn_prompt_kernel,
        grid=(BATCH, nb),
        in_specs=[
            pl.BlockSpec(memory_space=pltpu.SMEM),
            pl.BlockSpec((WINDOW, A_Q), lambda b, n: (cur(b, n), 0)),
            pl.BlockSpec((WINDOW, A_KV), lambda b, n: (cur(b, n), kcol)),
            pl.BlockSpec((WINDOW, A_KV), lambda b, n: (prev(b, n), kcol)),
            pl.BlockSpec((WINDOW, A_KV), lambda b, n: (cur(b, n), vcol)),
            pl.BlockSpec((WINDOW, A_KV), lambda b, n: (prev(b, n), vcol)),
        ],
        out_specs=pl.BlockSpec((WINDOW, A_Q), lambda b, n: (cur(b, n), 0)),
        out_shape=jax.ShapeDtypeStruct((N_PROMPT, A_Q), BF16),
        compiler_params=_params(("parallel", "arbitrary")),
        name="attn_prompt",
    )(sinks, qk, qk, qk, rest, rest)


BB_ATTN = 8


def _attn_sample_kernel(sink_ref, q_ref, kn_ref, vn_ref, ks_ref, vs_ref, o_ref):
    pad = jnp.zeros((WINDOW - DEC_SEQ, A_KV), F32)
    rows = A_GROUP * DEC_SEQ
    t = lax.broadcasted_iota(jnp.int32, (rows, 2 * WINDOW), 0) % DEC_SEQ
    c = lax.broadcasted_iota(jnp.int32, (rows, 2 * WINDOW), 1)
    ok = (c > t) & (c <= t + WINDOW)
    nt = (((1,), (1,)), ((), ()))

    def body(pair, carry):
        chains = [(e, g) for e in range(2) for g in range(A_KV_HEADS)]
        elem = {e: pair * 2 + e for e in range(2)}
        row0 = {e: pl.multiple_of(elem[e] * DEC_SEQ, DEC_SEQ) for e in range(2)}
        q = {e: q_ref[pl.ds(row0[e], DEC_SEQ), :] * (A_HEAD_DIM ** -0.5) for e in range(2)}
        kk = {e: jnp.concatenate([ks_ref[elem[e]], kn_ref[pl.ds(row0[e], DEC_SEQ), :], pad], axis=0).astype(BF16)
              for e in range(2)}
        vv = {e: jnp.concatenate([vs_ref[elem[e]], vn_ref[pl.ds(row0[e], DEC_SEQ), :], pad], axis=0).astype(BF16)
              for e in range(2)}
        heads = {g: range(g * A_GROUP, (g + 1) * A_GROUP) for g in range(A_KV_HEADS)}
        ds = {g: slice(g * A_HEAD_DIM, (g + 1) * A_HEAD_DIM) for g in range(A_KV_HEADS)}
        sink = {g: sink_ref[g * rows:(g + 1) * rows, 0:1] for g in range(A_KV_HEADS)}
        qg = {(bb, g): jnp.concatenate([q[bb][:, h * A_HEAD_DIM:(h + 1) * A_HEAD_DIM] for h in heads[g]],
                                       axis=0).astype(BF16) for bb, g in chains}
        s = {(bb, g): lax.dot_general(qg[bb, g], kk[bb][:, ds[g]], nt, preferred_element_type=F32)
             for bb, g in chains}
        s = {ch: jnp.where(ok, s[ch], NEG_INF) for ch in chains}
        m = {(bb, g): jnp.maximum(jnp.max(s[bb, g], axis=1, keepdims=True), sink[g]) for bb, g in chains}
        p = {ch: jnp.exp(s[ch] - m[ch]) for ch in chains}
        denom = {(bb, g): jnp.sum(p[bb, g], axis=1, keepdims=True) + jnp.exp(sink[g] - m[bb, g])
                 for bb, g in chains}
        o = {(bb, g): jnp.dot(p[bb, g].astype(BF16), vv[bb][:, ds[g]], preferred_element_type=F32) / denom[bb, g]
             for bb, g in chains}
        for bb, g in chains:
            for k, h in enumerate(heads[g]):
                o_ref[pl.ds(row0[bb], DEC_SEQ), h * A_HEAD_DIM:(h + 1) * A_HEAD_DIM] = (
                    o[bb, g][k * DEC_SEQ:(k + 1) * DEC_SEQ, :].astype(o_ref.dtype))
        return carry

    lax.fori_loop(0, BB_ATTN // 2, body, 0)


def _attn_sample(qk, rest, sinks, k_state, v_state):
    rows = BB_ATTN * DEC_SEQ
    base = N_PROMPT // rows
    kcol = A_Q // A_KV
    vcol = R_VA // A_KV
    sink_rows = jnp.broadcast_to(jnp.repeat(sinks, DEC_SEQ)[:, None], (A_HEADS * DEC_SEQ, LANES))
    return pl.pallas_call(
        _attn_sample_kernel,
        grid=(DEC_BATCH // BB_ATTN,),
        in_specs=[
            pl.BlockSpec((A_HEADS * DEC_SEQ, LANES), lambda i: (0, 0)),
            pl.BlockSpec((rows, A_Q), lambda i: (base + i, 0)),
            pl.BlockSpec((rows, A_KV), lambda i: (base + i, kcol)),
            pl.BlockSpec((rows, A_KV), lambda i: (base + i, vcol)),
            pl.BlockSpec((BB_ATTN, WINDOW, A_KV), lambda i: (i, 0, 0)),
            pl.BlockSpec((BB_ATTN, WINDOW, A_KV), lambda i: (i, 0, 0)),
        ],
        out_specs=pl.BlockSpec((rows, A_Q), lambda i: (i, 0)),
        out_shape=jax.ShapeDtypeStruct((N_SAMPLE, A_Q), BF16),
        compiler_params=_params(("parallel",)),
        name="attn_sample",
    )(sink_rows, qk, qk, rest, k_state, v_state)


def _split3(x):
    hi = x.astype(BF16)
    r1 = x - hi.astype(F32)
    mid = r1.astype(BF16)
    lo = (r1 - mid.astype(F32)).astype(BF16)
    return hi, mid, lo


GLA_INPUTS = ((LANES, R_LR), (B_QK, R_QB), (B_QK, R_KB), (B_V, R_VB), (B_V, R_GB))


def _gla_kernel(*refs, n_chunks, group, chunk, layer, n_in):
    ins, rest_refs = refs[:n_in], refs[n_in:]
    per = n_in // len(GLA_INPUTS)
    s0_ref, wa_ref, ba_ref, g_ref = rest_refs[:4]
    prev_ref = rest_refs[4] if layer else None
    o_ref, sout_ref, st_ref = rest_refs[-3:]
    ci = pl.program_id(1)
    c = chunk

    def rows(inp, g):
        if per == 1:
            return ins[inp][g * c:(g + 1) * c, :]
        return ins[inp * per + g][...]

    single = n_chunks == 1
    if not single:
        @pl.when(ci == 0)
        def _():
            for g in range(group):
                for h in range(B_HEADS):
                    st_ref[g, h] = s0_ref[0, g, h].T

    ri = lax.broadcasted_iota(jnp.int32, (c, c), 0)
    cj = lax.broadcasted_iota(jnp.int32, (c, c), 1)
    causal = cj <= ri
    tri = jnp.where(causal, 1.0, 0.0).astype(BF16)
    nt = (((1,), (1,)), ((), ()))
    G = range(group)
    z = [jnp.dot(rows(0, g).astype(BF16), wa_ref[...], preferred_element_type=F32) + ba_ref[...] for g in G]
    log_a = [-(jnp.maximum(-z[g], 0.0) + jnp.log1p(jnp.exp(-jnp.abs(z[g])))) / B_TAU for g in G]
    parts = [_split3(log_a[g]) for g in G]
    b = [jnp.dot(tri, parts[g][0], preferred_element_type=F32)
         + jnp.dot(tri, parts[g][1], preferred_element_type=F32)
         + jnp.dot(tri, parts[g][2], preferred_element_type=F32) for g in G]
    bl = [b[g][c - 1:c, :] for g in G]
    qd = [(rows(1, g) * (B_DK ** -0.5) * jnp.exp(b[g])).astype(BF16) for g in G]
    kd = [(rows(2, g) * jnp.exp(-b[g])).astype(BF16) for g in G]
    kl = [rows(2, g) * jnp.exp(bl[g] - b[g]) for g in G]
    kl = [kl[g].T if single else kl[g].astype(BF16) for g in G]
    ebl = [jnp.exp(bl[g]) for g in G]
    for h in range(B_HEADS):
        ks = slice(h * B_DK, (h + 1) * B_DK)
        vs = slice(h * B_DV, (h + 1) * B_DV)
        vh = [rows(3, g)[:, vs] for g in G]
        if single:
            st = [s0_ref[0, g, h] for g in G]
            o = [jnp.dot(qd[g][:, ks], st[g].astype(BF16), preferred_element_type=F32) for g in G]
        else:
            st = [st_ref[g, h] for g in G]
            o = [lax.dot_general(qd[g][:, ks], st[g].astype(BF16), nt, preferred_element_type=F32) for g in G]
        att = [lax.dot_general(qd[g][:, ks], kd[g][:, ks], nt, preferred_element_type=F32) for g in G]
        att = [jnp.where(causal, att[g], 0.0).astype(BF16) for g in G]
        o = [o[g] + jnp.dot(att[g], vh[g].astype(BF16), preferred_element_type=F32) for g in G]
        for g in G:
            if single:
                decay = jnp.broadcast_to(ebl[g][:, ks], (SUBLANES, B_DK)).T[:, 0:1]
                for l in range(layer):
                    sout_ref[l, g, h] = prev_ref[l, g, h]
                sout_ref[layer, g, h] = st[g] * decay + jnp.dot(kl[g][ks, :], vh[g],
                                                                  preferred_element_type=F32)
            else:
                st_ref[g, h] = st[g] * ebl[g][:, ks] + jnp.dot(vh[g].T.astype(BF16), kl[g][:, ks],
                                                               preferred_element_type=F32)
        o = [o[g] * lax.rsqrt(jnp.mean(o[g] * o[g], axis=1, keepdims=True) + RMS_EPS) * g_ref[h:h + 1, :]
             for g in G]
        for g in G:
            gate = rows(4, g)[:, vs]
            o_ref[g, :, vs] = (o[g] * (gate / (1.0 + jnp.exp(-gate)))).astype(o_ref.dtype)

    if not single:
        @pl.when(ci == n_chunks - 1)
        def _():
            for g in range(group):
                for l in range(layer):
                    sout_ref[l, g] = prev_ref[l, g]
                for h in range(B_HEADS):
                    sout_ref[layer, g, h] = st_ref[g, h].T


def _gla(rest, s0, layer, prev_states, wa, ba, gain, *, n_batch, seq, chunk, row_base, group):
    n_chunks = seq // chunk
    contiguous = n_chunks == 1
    base = row_base // chunk

    def in_specs_for(width, col):
        if contiguous:
            return [pl.BlockSpec((group * chunk, width), lambda b, ci: (base // group + b, col // width))]
        return [pl.BlockSpec((chunk, width),
                             lambda b, ci, g=g: (base + (b * group + g) * n_chunks + ci, col // width))
                for g in range(group)]

    row_specs = [spec for width, col in GLA_INPUTS for spec in in_specs_for(width, col)]
    s0_layer = layer if s0.shape[0] > 1 else 0
    state_block = (group, B_HEADS, B_DK, B_DV)
    prev_specs = [pl.BlockSpec((layer,) + state_block, lambda b, ci: (0, b, 0, 0, 0))] if layer else []
    prev_args = [prev_states] if layer else []
    return pl.pallas_call(
        functools.partial(_gla_kernel, n_chunks=n_chunks, group=group, chunk=chunk, layer=layer,
                          n_in=len(row_specs)),
        grid=(n_batch // group, n_chunks),
        in_specs=row_specs + [
            pl.BlockSpec((1,) + state_block, lambda b, ci: (s0_layer, b, 0, 0, 0)),
            pl.BlockSpec((LANES, B_QK), lambda b, ci: (0, 0)),
            pl.BlockSpec((1, B_QK), lambda b, ci: (0, 0)),
            pl.BlockSpec((B_HEADS, B_DV), lambda b, ci: (0, 0)),
        ] + prev_specs,
        out_specs=[
            pl.BlockSpec((group, chunk, B_V), lambda b, ci: (b, ci, 0)),
            pl.BlockSpec((layer + 1,) + state_block, lambda b, ci: (0, b, 0, 0, 0)),
        ],
        out_shape=[
            jax.ShapeDtypeStruct((n_batch, seq, B_V), BF16),
            jax.ShapeDtypeStruct((layer + 1, n_batch, B_HEADS, B_DK, B_DV), F32),
        ],
        scratch_shapes=[pltpu.VMEM((group, B_HEADS, B_DV, B_DK), F32)],
        compiler_params=_params(("parallel", "arbitrary")),
        name="gla",
    )(*([rest] * len(row_specs)), s0, wa, ba, gain, *prev_args)


HALO = 16
BB_POOL = 16


def _pool_kernel(u_ref, prev_ref, w_ref, scale_ref, o_ref, *, from_start, n_seq):
    tp = u_ref.shape[0] // n_seq
    if from_start:
        ti = pl.program_id(1)
        t0 = ti * tp
    diffs = [[] for _ in POOL_WINDOWS]
    for sq in range(n_seq):
        u = u_ref[sq * tp:(sq + 1) * tp, :]
        prev = prev_ref[sq * HALO:(sq + 1) * HALO, :]
        if from_start:
            prev = jnp.where(ti > 0, prev, 0.0)
        full = jnp.concatenate([prev, u], axis=0)
        for g, w in enumerate(POOL_WINDOWS):
            cs = slice(g * C_GROUP_W, (g + 1) * C_GROUP_W)
            acc = full[:, cs]
            span = 1
            while span < w:
                acc = acc + pltpu.roll(acc, span, axis=0)
                span *= 2
            wsum = acc[HALO:, :]
            if from_start:
                t = t0 + lax.broadcasted_iota(jnp.int32, (tp, C_GROUP_W), 0)
                cnt = jnp.minimum(t + 1, w).astype(F32)
            else:
                cnt = float(w)
            diffs[g].append(wsum / cnt - u[:, cs])
    for g in range(C_GROUPS):
        cs = slice(g * C_GROUP_W, (g + 1) * C_GROUP_W)
        d = jnp.concatenate(diffs[g], axis=0) if n_seq > 1 else diffs[g][0]
        y = jnp.dot(d.astype(BF16), w_ref[g], preferred_element_type=F32) * scale_ref[:, cs]
        o_ref[:, cs] = y.astype(o_ref.dtype)


def _pool_prompt(rest, w, scale):
    nt = SEQ // TP_POOL
    ucol = R_UC // C_WIDTH

    def halo(b, i):
        return (jnp.maximum((b * SEQ + i * TP_POOL) // HALO - 1, 0), ucol)

    return pl.pallas_call(
        functools.partial(_pool_kernel, from_start=True, n_seq=1),
        grid=(BATCH, nt),
        in_specs=[
            pl.BlockSpec((TP_POOL, C_WIDTH), lambda b, i: (b * nt + i, ucol)),
            pl.BlockSpec((HALO, C_WIDTH), halo),
            pl.BlockSpec((C_GROUPS, C_GROUP_W, C_GROUP_W), lambda b, i: (0, 0, 0)),
            pl.BlockSpec((1, C_WIDTH), lambda b, i: (0, 0)),
        ],
        out_specs=pl.BlockSpec((TP_POOL, C_WIDTH), lambda b, i: (b * nt + i, 0)),
        out_shape=jax.ShapeDtypeStruct((N_PROMPT, C_WIDTH), BF16),
        compiler_params=_params(("parallel", "arbitrary")),
        name="pool_prompt",
    )(rest, rest, w, scale)


def _pool_sample(rest, prev, w, scale):
    ucol = R_UC // C_WIDTH
    rows = BB_POOL * DEC_SEQ
    base = N_PROMPT // rows
    return pl.pallas_call(
        functools.partial(_pool_kernel, from_start=False, n_seq=BB_POOL),
        grid=(DEC_BATCH // BB_POOL,),
        in_specs=[
            pl.BlockSpec((rows, C_WIDTH), lambda b: (base + b, ucol)),
            pl.BlockSpec((BB_POOL * HALO, C_WIDTH), lambda b: (b, 0)),
            pl.BlockSpec((C_GROUPS, C_GROUP_W, C_GROUP_W), lambda b: (0, 0, 0)),
            pl.BlockSpec((1, C_WIDTH), lambda b: (0, 0)),
        ],
        out_specs=pl.BlockSpec((rows, C_WIDTH), lambda b: (b, 0)),
        out_shape=jax.ShapeDtypeStruct((N_SAMPLE, C_WIDTH), BF16),
        compiler_params=_params(("parallel",)),
        name="pool_sample",
    )(rest, prev, w, scale)


def _layer_norm(x, g, b):
    mu = jnp.mean(x, axis=1, keepdims=True)
    xc = x - mu
    var = jnp.mean(xc * xc, axis=1, keepdims=True)
    return xc * lax.rsqrt(var + LN_EPS) * g + b


def _merge_kernel(*refs):
    x_refs = refs[:len(refs) - 15]
    (gates_ref, oa_p_ref, oa_s_ref, ob_p_ref, ob_s_ref, oc_p_ref, oc_s_ref,
     wa_ref, wb_ref, wc_ref, wo_ref, g_ref, b_ref, o_ref, oT_ref) = refs[len(x_refs):]
    in_sample = pl.program_id(0) >= N_PROMPT // TM_MERGE
    x = x_refs[0][...] if len(x_refs) == 1 else jnp.where(in_sample, x_refs[1][...], x_refs[0][...])

    def gate(i):
        z = gates_ref[:, i * D_MODEL:(i + 1) * D_MODEL]
        return 1.0 / (1.0 + jnp.exp(-z))

    def branch(p_ref, s_ref, w_ref):
        o = jnp.where(in_sample, s_ref[...], p_ref[...])
        return jnp.dot(o, w_ref[...], preferred_element_type=F32)

    merged = (gate(0) * branch(oa_p_ref, oa_s_ref, wa_ref)
              + gate(1) * branch(ob_p_ref, ob_s_ref, wb_ref)
              + gate(2) * branch(oc_p_ref, oc_s_ref, wc_ref))
    mix = jnp.dot(merged.astype(BF16), wo_ref[...], preferred_element_type=F32)
    y = _layer_norm(DN_ALPHA * x + mix, g_ref[...], b_ref[...])
    o_ref[...] = y
    oT_ref[...] = y.T.astype(BF16)


def _merge(x, rest, branches, wa, wb, wc, wo, g, b):
    n_p = N_PROMPT // TM_MERGE
    row = pl.BlockSpec((TM_MERGE, D_MODEL), lambda i: (i, 0))
    row_p = pl.BlockSpec((TM_MERGE, D_MODEL), lambda i: (jnp.minimum(i, n_p - 1), 0))
    row_s = pl.BlockSpec((TM_MERGE, D_MODEL), lambda i: (jnp.maximum(i - n_p, 0), 0))
    wspec = pl.BlockSpec((D_MODEL, D_MODEL), lambda i: (0, 0))
    vec = pl.BlockSpec((1, D_MODEL), lambda i: (0, 0))
    (oa_p, oa_s), (ob_p, ob_s), (oc_p, oc_s) = branches
    x_parts = x if isinstance(x, tuple) else (x,)
    x_specs = [row_p, row_s] if isinstance(x, tuple) else [row]
    return pl.pallas_call(
        _merge_kernel,
        grid=(N_TOK // TM_MERGE,),
        in_specs=x_specs + [pl.BlockSpec((TM_MERGE, 3 * D_MODEL), lambda i: (i, R_GATES)),
                            row_p, row_s, row_p, row_s, row_p, row_s, wspec, wspec, wspec, wspec, vec, vec],
        out_specs=[row, pl.BlockSpec((D_MODEL, TM_MERGE), lambda i: (0, i))],
        out_shape=[jax.ShapeDtypeStruct((N_TOK, D_MODEL), F32),
                   jax.ShapeDtypeStruct((D_MODEL, N_TOK), BF16)],
        compiler_params=_params(("parallel",)),
        name="merge",
    )(*x_parts, rest, oa_p, oa_s, ob_p, ob_s, oc_p, oc_s, wa, wb, wc, wo, g, b)


def _extract_desc(s, n):
    vals = []
    for _ in range(n):
        m = jnp.max(s, axis=0, keepdims=True)
        vals.append(m)
        s = jnp.where(s == m, -jnp.inf, s)
    return vals


def _merge_sort_pairs(n):
    pairs = []
    p = 1
    while p < n:
        k = p
        while k >= 1:
            for j in range(k % p, n - k, 2 * k):
                for i in range(min(k, n - j - k)):
                    if (i + j) // (2 * p) == (i + j + k) // (2 * p):
                        pairs.append((i + j, i + j + k))
            k //= 2
        p *= 2
    return pairs


def _top_desc(s, n):
    v = [s[SUBLANES * k:SUBLANES * (k + 1), :] for k in range(s.shape[0] // SUBLANES)]
    for a, b in _merge_sort_pairs(len(v)):
        v[a], v[b] = jnp.maximum(v[a], v[b]), jnp.minimum(v[a], v[b])
    vals = []
    for r in range(n):
        m = jnp.max(v[0], axis=0, keepdims=True)
        vals.append(m)
        hit = v[0] == m
        for k in range(n - r - 1):
            v[k] = jnp.where(hit, v[k + 1], v[k])
    return vals


def _rank_of(s, vals):
    rank = jnp.full(s.shape, float(len(vals)), F32)
    for r, val in enumerate(vals):
        rank = jnp.where(s == val, float(r), rank)
    return rank


def _peer_topk_kernel(xT_ref, wq_ref, sk_ref, cnt_ref, e1_ref, rank_ref, e2_ref, s1_scr, sv_scr):
    tt = xT_ref.shape[1]
    xT = xT_ref[...]
    for hp in range(2 * PEER_HEADS):
        h, second = divmod(hp, 2)
        qT = jnp.dot(wq_ref[hp * PEER_DHALF:(hp + 1) * PEER_DHALF, :], xT, preferred_element_type=F32)
        s = jnp.dot(sk_ref[hp], qT.astype(BF16), preferred_element_type=F32)
        vals = _top_desc(s, PEER_TOPK)
        sv_scr[hp] = jnp.concatenate(vals, axis=0)
        if second:
            rank_ref[h] = _rank_of(s, vals).astype(BF16)
            e2_ref[h] = jnp.exp(s - vals[0]).astype(BF16)
        else:
            s1_scr[h] = s
    row8 = lax.broadcasted_iota(jnp.int32, (8, tt), 0)
    for h in range(PEER_HEADS):
        sv1 = sv_scr[2 * h]
        sv2 = sv_scr[2 * h + 1]
        pieces = [sv1[0:1, :] + sv2]
        for a in range(1, 8):
            nb = PEER_TOPK // (a + 1)
            pieces.append(jnp.where(row8 < nb, sv1[a:a + 1, :] + sv2[0:8, :], -jnp.inf))
        pieces.append(sv2[0:1, :] + sv1[8:16, :])
        cand = _extract_desc(jnp.concatenate(pieces, axis=0), PEER_TOPK)
        top = cand[0]
        z = jnp.zeros_like(top)
        for r in range(PEER_TOPK):
            z = z + jnp.exp(cand[r] - top)
        kth = cand[PEER_TOPK - 1]
        counts = [jnp.sum(jnp.where(p >= kth, 1.0, 0.0), axis=0, keepdims=True) for p in pieces[:8]]
        tail = jnp.where(pieces[8] >= kth, 1.0, 0.0)
        counts += [tail[a:a + 1, :] for a in range(8)]
        s1 = s1_scr[h]
        cnt = jnp.zeros_like(s1)
        for a in range(PEER_TOPK):
            cnt = jnp.where(s1 == sv1[a:a + 1, :], counts[a], cnt)
        e1 = jnp.exp(s1 - (sv1[0:1, :] + jnp.log(z))) * SQRT_HALF
        for c in range(tt // LANES):
            cnt_ref[h, c] = cnt[:, c * LANES:(c + 1) * LANES]
            e1_ref[h, c] = e1[:, c * LANES:(c + 1) * LANES]


def _peer_topk(xT, wqT, sk):
    spec = pl.BlockSpec((PEER_HEADS, N_KEYS, TT_TOPK), lambda t: (0, 0, t))
    row_spec = pl.BlockSpec((PEER_HEADS, TT_TOPK // LANES, N_KEYS, LANES), lambda t: (0, t, 0, 0))
    wide = jax.ShapeDtypeStruct((PEER_HEADS, N_TOK // LANES, N_KEYS, LANES), F32)
    narrow = jax.ShapeDtypeStruct((PEER_HEADS, N_KEYS, N_TOK), BF16)
    return pl.pallas_call(
        _peer_topk_kernel,
        grid=(N_TOK // TT_TOPK,),
        in_specs=[
            pl.BlockSpec((D_MODEL, TT_TOPK), lambda t: (0, t)),
            pl.BlockSpec((PEER_HEADS * PEER_DKEY, D_MODEL), lambda t: (0, 0)),
            pl.BlockSpec((2 * PEER_HEADS, N_KEYS, PEER_DHALF), lambda t: (0, 0, 0)),
        ],
        out_specs=[row_spec, row_spec, spec, spec],
        out_shape=[wide, wide, narrow, narrow],
        scratch_shapes=[pltpu.VMEM((PEER_HEADS, N_KEYS, TT_TOPK), F32),
                        pltpu.VMEM((2 * PEER_HEADS, PEER_TOPK, TT_TOPK), F32)],
        compiler_params=_params(("parallel",)),
        name="peer_topk",
    )(xT, wqT, sk)


SQRT_HALF = 2.0 ** -0.5


def _gelu_unscaled(x):
    t = x * SQRT_HALF
    return t * (1.0 + lax.erf(t))


LC_PEER = 256


def _bf16_rows(ref, h, r, chunks, n_rows):
    x = jnp.concatenate([ref[h, c, pl.ds(r, 2 * SUBLANES, stride=0), :] for c in chunks], axis=1)
    packed = x.astype(BF16)
    return jnp.concatenate([packed] * (n_rows // packed.shape[0]), axis=0)


MM_PIECE = 512


def _peer_main_kernel(xT_ref, u_ref, vt_ref, cnt_ref, e1_ref, rank_ref, e2_ref, x_ref, g_ref, b_ref,
                      o_ref, ob_ref, yT_ref, s_scr, wh_scr):
    i = pl.program_id(1)
    tt = xT_ref.shape[1]

    @pl.when(i == 0)
    def _():
        yT_ref[...] = jnp.zeros_like(yT_ref)

    zero = jnp.zeros((N_KEYS, LC_PEER), BF16)
    per_piece = MM_PIECE // N_KEYS
    for ii in range(IB_PEER):
        rows = slice(ii * N_KEYS, (ii + 1) * N_KEYS)
        if ii % per_piece == 0:
            piece = slice(ii * N_KEYS, ii * N_KEYS + MM_PIECE)
            s_scr[piece, :] = jnp.dot(u_ref[0, piece, :], xT_ref[...], preferred_element_type=F32)
        for lc in range(tt // LC_PEER):
            cols = slice(lc * LC_PEER, (lc + 1) * LC_PEER)
            chunks = range(lc * LC_PEER // LANES, (lc + 1) * LC_PEER // LANES)
            w = zero
            for h in range(PEER_HEADS):
                cnt = _bf16_rows(cnt_ref, h, ii, chunks, N_KEYS)
                e1 = _bf16_rows(e1_ref, h, ii, chunks, N_KEYS)
                w = w + jnp.where(rank_ref[h, :, cols] < cnt, e2_ref[h, :, cols] * e1, zero)
            wh_scr[rows, cols] = w
    for ii in range(IB_PEER):
        rows = slice(ii * N_KEYS, (ii + 1) * N_KEYS)
        wh_scr[rows, :] = wh_scr[rows, :] * _gelu_unscaled(s_scr[rows, :]).astype(BF16)
    yT_ref[...] += jnp.dot(vt_ref[0], wh_scr[...], preferred_element_type=F32)

    @pl.when(i == pl.num_programs(1) - 1)
    def _():
        y = _layer_norm(DN_ALPHA * x_ref[...] + yT_ref[...].T, g_ref[...], b_ref[...])
        o_ref[...] = y
        ob_ref[...] = y.astype(BF16)


def _peer_main(xT, u, vt, layer, cnt, e1, rank, e2, x1, g, b):
    eb = IB_PEER * N_KEYS
    row_spec = pl.BlockSpec((PEER_HEADS, TT_PEER // LANES, IB_PEER, LANES), lambda t, i: (0, t, i, 0))
    tab_spec = pl.BlockSpec((PEER_HEADS, N_KEYS, TT_PEER), lambda t, i: (0, 0, t))
    tok_spec = pl.BlockSpec((TT_PEER, D_MODEL), lambda t, i: (t, 0))
    vec = pl.BlockSpec((1, D_MODEL), lambda t, i: (0, 0))
    return pl.pallas_call(
        _peer_main_kernel,
        grid=(N_TOK // TT_PEER, N_EXPERTS // eb),
        in_specs=[
            pl.BlockSpec((D_MODEL, TT_PEER), lambda t, i: (0, t)),
            pl.BlockSpec((1, eb, D_MODEL), lambda t, i: (layer, i, 0)),
            pl.BlockSpec((1, D_MODEL, eb), lambda t, i: (layer, 0, i)),
            row_spec, row_spec, tab_spec, tab_spec, tok_spec, vec, vec,
        ],
        out_specs=[tok_spec, tok_spec],
        out_shape=[jax.ShapeDtypeStruct((N_TOK, D_MODEL), F32),
                   jax.ShapeDtypeStruct((N_TOK, D_MODEL), BF16)],
        scratch_shapes=[pltpu.VMEM((D_MODEL, TT_PEER), F32),
                        pltpu.VMEM((eb, TT_PEER), F32), pltpu.VMEM((eb, TT_PEER), BF16)],
        compiler_params=_params(("parallel", "arbitrary")),
        name="peer_main",
    )(xT, u, vt, cnt, e1, rank, e2, x1, g, b)


def _rope_tables():
    half = ROT_DIM // 2
    pos = jnp.concatenate([jnp.arange(SEQ), PAST_LEN + (jnp.arange(TM_QK) % DEC_SEQ)])
    inv = ROPE_THETA ** (-jnp.arange(half, dtype=F32) / half)
    ang = pos.astype(F32)[:, None] * inv[None, :]
    cos, sin = jnp.cos(ang), jnp.sin(ang)
    n = pos.shape[0]
    one = jnp.ones((n, A_HEAD_DIM - ROT_DIM), F32)
    zero = jnp.zeros((n, A_HEAD_DIM - ROT_DIM), F32)
    zh = jnp.zeros((n, half), F32)
    reps = LANES // A_HEAD_DIM
    c = jnp.tile(jnp.concatenate([cos, cos, one], 1), (1, reps))
    s1 = jnp.tile(jnp.concatenate([-sin, zh, zero], 1), (1, reps))
    s2 = jnp.tile(jnp.concatenate([zh, sin, zero], 1), (1, reps))
    return c, s1, s2


def _split_cols(w):
    cuts = [int(c) for c in np.cumsum(SPLITS)[:-1]]
    return jnp.split(w, cuts, axis=-1)


def _layer(layer, x, xb, k_state, v_state, gla_states, gla_out, pool_state, rope, w_in, b_in, sinks,
           w_alpha, b_alpha, gla_g, w_pool, pool_scale, w_a, w_b, w_c, w_out, ln1_g, ln1_b,
           peer_query, peer_subkeys, peer_u, peer_vt, ln2_g, ln2_b):
    qa_w, ka_w, va_w, qb_w, kb_w, vb_w, lr_w, gb_w, uc_w, gates_w = _split_cols(w_in)
    qa_b, ka_b, va_b, qb_b, kb_b, vb_b, lr_b, gb_b, uc_b, gates_b = _split_cols(b_in[None, :])
    lr_pad = LANES - B_GATE_RANK
    w_qk = jnp.concatenate([qa_w, ka_w], 1).astype(BF16)
    b_qk = jnp.concatenate([qa_b, ka_b], 1)
    w_rest = jnp.concatenate([gates_w, vb_w, gb_w, uc_w, qb_w, kb_w, va_w,
                              jnp.pad(lr_w, ((0, 0), (0, lr_pad)))], 1).astype(BF16)
    b_rest = jnp.concatenate([gates_b, vb_b, gb_b, uc_b, qb_b, kb_b, va_b,
                              jnp.pad(lr_b, ((0, 0), (0, lr_pad)))], 1)

    qk = _proj_qk(xb, w_qk, b_qk, *rope)
    rest = _proj_rest(xb, w_rest, b_rest)

    ks = k_state.reshape(DEC_BATCH, WINDOW, A_KV)
    vs = v_state.reshape(DEC_BATCH, WINDOW, A_KV)
    oa = (_attn_prompt(qk, rest, sinks), _attn_sample(qk, rest, sinks, ks, vs))

    wa = jnp.pad(w_alpha, ((0, lr_pad), (0, 0))).astype(BF16)
    ba = b_alpha[None, :]
    gla_p, gla_s = gla_out
    ob_p, gla_p = _gla(rest, jnp.zeros((1, BATCH, B_HEADS, B_DK, B_DV), F32), layer, gla_p, wa, ba, gla_g,
                       n_batch=BATCH, seq=SEQ, chunk=GLA_CHUNK, row_base=0, group=GLA_GROUP_PROMPT)
    ob_s, gla_s = _gla(rest, gla_states, layer, gla_s, wa, ba, gla_g, n_batch=DEC_BATCH, seq=DEC_SEQ,
                       chunk=math.gcd(DEC_SEQ, GLA_CHUNK), row_base=N_PROMPT, group=GLA_GROUP_SAMPLE)
    ob = (ob_p.reshape(N_PROMPT, B_V), ob_s.reshape(N_SAMPLE, B_V))

    wp = w_pool.astype(BF16)
    ps = pool_scale[None, :]
    prev = jnp.pad(pool_state, ((0, 0), (HALO - POOL_STATE, 0), (0, 0))).reshape(DEC_BATCH * HALO, C_WIDTH)
    oc = (_pool_prompt(rest, wp, ps), _pool_sample(rest, prev, wp, ps))

    x1, x1T = _merge(x, rest, (oa, ob, oc), w_a.astype(BF16), w_b.astype(BF16), w_c.astype(BF16),
                     w_out.astype(BF16), ln1_g[None, :], ln1_b[None, :])

    wqT = peer_query.reshape(D_MODEL, PEER_HEADS * PEER_DKEY).T.astype(BF16)
    sk = peer_subkeys.reshape(2 * PEER_HEADS, N_KEYS, PEER_DHALF).astype(BF16)
    cnt, e1, rank, e2 = _peer_topk(x1T, wqT, sk)
    x2, x2b = _peer_main(x1T, peer_u, peer_vt, layer, cnt, e1, rank, e2, x1, ln2_g[None, :], ln2_b[None, :])

    def prompt_tail(t, col0, width, n):
        return jnp.stack([t[(b + 1) * SEQ - n:(b + 1) * SEQ, col0:col0 + width] for b in range(BATCH)])

    def sample_tail(state, t, col0, width, n):
        new = t[N_PROMPT:, col0:col0 + width].reshape(DEC_BATCH, DEC_SEQ, width)
        return jnp.concatenate([state, new], 1)[:, -n:]

    kv_shape = (-1, WINDOW, A_KV_HEADS, A_HEAD_DIM)
    states = (prompt_tail(qk, A_Q, A_KV, WINDOW).reshape(kv_shape),
              prompt_tail(rest, R_VA, A_KV, WINDOW).reshape(kv_shape),
              prompt_tail(rest, R_UC, C_WIDTH, POOL_STATE),
              sample_tail(ks, qk, A_Q, A_KV, WINDOW).reshape(kv_shape),
              sample_tail(vs, rest, R_VA, A_KV, WINDOW).reshape(kv_shape),
              sample_tail(pool_state, rest, R_UC, C_WIDTH, POOL_STATE))
    return x2, x2b, states, (gla_p, gla_s)


def kernel(x_prompt, x_sample, state_win_k, state_win_v, state_gla, state_pool, w_in, b_in, attn_sinks,
           w_alpha, b_alpha, gla_norm_g, w_pool, pool_scale, w_branch_a, w_branch_b, w_branch_c, w_out,
           ln1_g, ln1_b, peer_query, peer_subkeys, peer_u, peer_v, ln2_g, ln2_b):
    x = (x_prompt.reshape(N_PROMPT, D_MODEL), x_sample.reshape(N_SAMPLE, D_MODEL))
    xb = jnp.concatenate([x[0].astype(BF16), x[1].astype(BF16)], 0)
    rope = _rope_tables()
    peer_ub = peer_u.astype(BF16)
    peer_vtb = jnp.swapaxes(peer_v, 1, 2).astype(BF16)
    per_layer = []
    gla_out = (None, None)
    for l in range(DEPTH):
        x, xb, states, gla_out = _layer(
            l, x, xb, state_win_k[l], state_win_v[l], state_gla, gla_out, state_pool[l], rope,
            w_in[l], b_in[l], attn_sinks[l], w_alpha[l], b_alpha[l], gla_norm_g[l],
            w_pool[l], pool_scale[l], w_branch_a[l], w_branch_b[l], w_branch_c[l], w_out[l],
            ln1_g[l], ln1_b[l], peer_query[l], peer_subkeys[l], peer_ub, peer_vtb, ln2_g[l], ln2_b[l])
        per_layer.append(states)
    pk, pv, pp, sk, sv, sp = [jnp.stack([per_layer[l][i] for l in range(DEPTH)]) for i in range(6)]
    return (x[:N_PROMPT].reshape(BATCH, SEQ, D_MODEL), x[N_PROMPT:].reshape(DEC_BATCH, DEC_SEQ, D_MODEL),
            pk, pv, gla_out[0], pp, sk, sv, gla_out[1], sp)
```

```python
import functools
import math

import jax
import jax.numpy as jnp
import numpy as np
from jax import lax
from jax.experimental import pallas as pl
from jax.experimental.pallas import tpu as pltpu

F32 = jnp.float32
BF16 = jnp.bfloat16

D_MODEL = 1024
BATCH = 8
SEQ = 2048
DEPTH = 2
DEC_BATCH = 128
DEC_SEQ = 8
PAST_LEN = 16384

A_HEADS = 16
A_KV_HEADS = 2
A_HEAD_DIM = 64
A_GROUP = A_HEADS // A_KV_HEADS
WINDOW = 128
ROT_DIM = A_HEAD_DIM // 4
ROPE_THETA = 500000.0
NEG_INF = -1e30
B_HEADS = 4
B_DK = D_MODEL // 2 // B_HEADS
B_DV = D_MODEL // B_HEADS
B_GATE_RANK = 16
B_TAU = 16.0
GLA_CHUNK = 64
POOL_WINDOWS = (2, 4, 8, 16)
C_GROUPS = len(POOL_WINDOWS)
C_GROUP_W = D_MODEL // C_GROUPS
C_WIDTH = C_GROUPS * C_GROUP_W
POOL_STATE = max(POOL_WINDOWS) - 1
PEER_HEADS = 8
N_KEYS = 128
N_EXPERTS = N_KEYS * N_KEYS
PEER_TOPK = 16
PEER_DKEY = 256
PEER_DHALF = PEER_DKEY // 2
DN_ALPHA = (2 * DEPTH) ** 0.25
LN_EPS = 1e-5
RMS_EPS = 1e-6

A_Q = A_HEADS * A_HEAD_DIM
A_KV = A_KV_HEADS * A_HEAD_DIM
B_QK = B_HEADS * B_DK
B_V = B_HEADS * B_DV
SPLITS = (A_Q, A_KV, A_KV, B_QK, B_QK, B_V, B_GATE_RANK, B_V, C_WIDTH, 3 * D_MODEL)

LANES = 128
SUBLANES = 8
N_PROMPT = BATCH * SEQ
N_SAMPLE = DEC_BATCH * DEC_SEQ
N_TOK = N_PROMPT + N_SAMPLE

R_GATES = 0
R_VB = 3 * D_MODEL
R_GB = R_VB + B_V
R_UC = R_GB + B_V
R_QB = R_UC + C_WIDTH
R_KB = R_QB + B_QK
R_VA = R_KB + B_QK
R_LR = R_VA + A_KV
R_WIDTH = R_LR + LANES
QK_WIDTH = A_Q + A_KV

VMEM_LIMIT = 48 * 1024 * 1024

TM_QK = 512
TM_REST = 512
TN_REST = R_WIDTH // 2
TM_MERGE = 256
TP_POOL = 512
GLA_GROUP_PROMPT = 4
GLA_GROUP_SAMPLE = 4
TT_TOPK = 1024
TT_PEER = 512
IB_PEER = 16


def _params(sem):
    return pltpu.CompilerParams(dimension_semantics=sem, vmem_limit_bytes=VMEM_LIMIT)


def _qk_kernel(x_ref, w_ref, b_ref, c_ref, s1_ref, s2_ref, o_ref):
    y = jnp.dot(x_ref[...], w_ref[...], preferred_element_type=F32) + b_ref[...]
    c = c_ref[...]
    s1 = s1_ref[...]
    s2 = s2_ref[...]
    for j in range(QK_WIDTH // LANES):
        yj = y[:, j * LANES:(j + 1) * LANES]
        up = pltpu.roll(yj, LANES - ROT_DIM // 2, axis=1)
        dn = pltpu.roll(yj, ROT_DIM // 2, axis=1)
        o_ref[:, j * LANES:(j + 1) * LANES] = yj * c + up * s1 + dn * s2


def _proj_qk(xb, w, b, rope_c, rope_s1, rope_s2):
    n_prompt_blocks = SEQ // TM_QK

    def tab_map(i):
        return (jnp.where(i < N_PROMPT // TM_QK, i % n_prompt_blocks, n_prompt_blocks), 0)

    tab_spec = pl.BlockSpec((TM_QK, LANES), tab_map)
    return pl.pallas_call(
        _qk_kernel,
        grid=(N_TOK // TM_QK,),
        in_specs=[
            pl.BlockSpec((TM_QK, D_MODEL), lambda i: (i, 0)),
            pl.BlockSpec((D_MODEL, QK_WIDTH), lambda i: (0, 0)),
            pl.BlockSpec((1, QK_WIDTH), lambda i: (0, 0)),
            tab_spec, tab_spec, tab_spec,
        ],
        out_specs=pl.BlockSpec((TM_QK, QK_WIDTH), lambda i: (i, 0)),
        out_shape=jax.ShapeDtypeStruct((N_TOK, QK_WIDTH), F32),
        compiler_params=_params(("parallel",)),
        name="proj_qk",
    )(xb, w, b, rope_c, rope_s1, rope_s2)


def _mm_bias_kernel(x_ref, w_ref, b_ref, o_ref):
    o_ref[...] = jnp.dot(x_ref[...], w_ref[...], preferred_element_type=F32) + b_ref[...]


def _proj_rest(xb, w, b):
    return pl.pallas_call(
        _mm_bias_kernel,
        grid=(R_WIDTH // TN_REST, N_TOK // TM_REST),
        in_specs=[
            pl.BlockSpec((TM_REST, D_MODEL), lambda j, i: (i, 0)),
            pl.BlockSpec((D_MODEL, TN_REST), lambda j, i: (0, j)),
            pl.BlockSpec((1, TN_REST), lambda j, i: (0, j)),
        ],
        out_specs=pl.BlockSpec((TM_REST, TN_REST), lambda j, i: (i, j)),
        out_shape=jax.ShapeDtypeStruct((N_TOK, R_WIDTH), F32),
        compiler_params=_params(("parallel", "arbitrary")),
        name="proj_rest",
    )(xb, w, b)


HEADS_PER_PASS = 16


def _attend(q, kk, vv, sink_ref, c_min, o_ref, row0):
    tq = q.shape[0]
    r = lax.broadcasted_iota(jnp.int32, (tq, 2 * WINDOW), 0)
    c = lax.broadcasted_iota(jnp.int32, (tq, 2 * WINDOW), 1)
    ok = (c > r) & (c <= r + WINDOW) & (c >= c_min)
    qb = (q * (A_HEAD_DIM ** -0.5)).astype(BF16)
    nt = (((1,), (1,)), ((), ()))
    for h0 in range(0, A_HEADS, HEADS_PER_PASS):
        hs = range(h0, h0 + HEADS_PER_PASS)
        col = {h: slice(h * A_HEAD_DIM, (h + 1) * A_HEAD_DIM) for h in hs}
        kv = {h: slice((h // A_GROUP) * A_HEAD_DIM, (h // A_GROUP + 1) * A_HEAD_DIM) for h in hs}
        s = {h: lax.dot_general(qb[:, col[h]], kk[:, kv[h]], nt, preferred_element_type=F32) for h in hs}
        s = {h: jnp.where(ok, s[h], NEG_INF) for h in hs}
        m = {h: jnp.maximum(jnp.max(s[h], axis=1, keepdims=True), sink_ref[h]) for h in hs}
        p = {h: jnp.exp(s[h] - m[h]) for h in hs}
        denom = {h: jnp.sum(p[h], axis=1, keepdims=True) + jnp.exp(sink_ref[h] - m[h]) for h in hs}
        o = {h: jnp.dot(p[h].astype(BF16), vv[:, kv[h]], preferred_element_type=F32) / denom[h] for h in hs}
        for h in hs:
            o_ref[pl.ds(row0, tq), col[h]] = o[h].astype(o_ref.dtype)


def _attn_prompt_kernel(sink_ref, q_ref, kc_ref, kp_ref, vc_ref, vp_ref, o_ref):
    n = pl.program_id(1)
    kk = jnp.concatenate([kp_ref[...], kc_ref[...]], axis=0).astype(BF16)
    vv = jnp.concatenate([vp_ref[...], vc_ref[...]], axis=0).astype(BF16)
    _attend(q_ref[...], kk, vv, sink_ref, jnp.where(n > 0, 0, WINDOW), o_ref, 0)


def _attn_prompt(qk, rest, sinks):
    nb = SEQ // WINDOW
    kcol = A_Q // A_KV
    vcol = R_VA // A_KV

    def cur(b, n):
        return b * nb + n

    def prev(b, n):
        return b * nb + jnp.maximum(n - 1, 0)

    return pl.pallas_call(
        _attn_prompt_kernel,
        grid=(BATCH, nb),
        in_specs=[
            pl.BlockSpec(memory_space=pltpu.SMEM),
            pl.BlockSpec((WINDOW, A_Q), lambda b, n: (cur(b, n), 0)),
            pl.BlockSpec((WINDOW, A_KV), lambda b, n: (cur(b, n), kcol)),
            pl.BlockSpec((WINDOW, A_KV), lambda b, n: (prev(b, n), kcol)),
            pl.BlockSpec((WINDOW, A_KV), lambda b, n: (cur(b, n), vcol)),
            pl.BlockSpec((WINDOW, A_KV), lambda b, n: (prev(b, n), vcol)),
        ],
        out_specs=pl.BlockSpec((WINDOW, A_Q), lambda b, n: (cur(b, n), 0)),
        out_shape=jax.ShapeDtypeStruct((N_PROMPT, A_Q), BF16),
        compiler_params=_params(("parallel", "arbitrary")),
        name="attn_prompt",
    )(sinks, qk, qk, qk, rest, rest)


BB_ATTN = 8


def _attn_sample_kernel(sink_ref, q_ref, kn_ref, vn_ref, ks_ref, vs_ref, o_ref):
    pad = jnp.zeros((WINDOW - DEC_SEQ, A_KV), F32)
    rows = A_GROUP * DEC_SEQ
    t = lax.broadcasted_iota(jnp.int32, (rows, 2 * WINDOW), 0) % DEC_SEQ
    c = lax.broadcasted_iota(jnp.int32, (rows, 2 * WINDOW), 1)
    ok = (c > t) & (c <= t + WINDOW)
    nt = (((1,), (1,)), ((), ()))

    def body(pair, carry):
        chains = [(e, g) for e in range(2) for g in range(A_KV_HEADS)]
        elem = {e: pair * 2 + e for e in range(2)}
        row0 = {e: pl.multiple_of(elem[e] * DEC_SEQ, DEC_SEQ) for e in range(2)}
        q = {e: q_ref[pl.ds(row0[e], DEC_SEQ), :] * (A_HEAD_DIM ** -0.5) for e in range(2)}
        kk = {e: jnp.concatenate([ks_ref[elem[e]], kn_ref[pl.ds(row0[e], DEC_SEQ), :], pad], axis=0).astype(BF16)
              for e in range(2)}
        vv = {e: jnp.concatenate([vs_ref[elem[e]], vn_ref[pl.ds(row0[e], DEC_SEQ), :], pad], axis=0).astype(BF16)
              for e in range(2)}
        heads = {g: range(g * A_GROUP, (g + 1) * A_GROUP) for g in range(A_KV_HEADS)}
        ds = {g: slice(g * A_HEAD_DIM, (g + 1) * A_HEAD_DIM) for g in range(A_KV_HEADS)}
        sink = {g: sink_ref[g * rows:(g + 1) * rows, 0:1] for g in range(A_KV_HEADS)}
        qg = {(bb, g): jnp.concatenate([q[bb][:, h * A_HEAD_DIM:(h + 1) * A_HEAD_DIM] for h in heads[g]],
                                       axis=0).astype(BF16) for bb, g in chains}
        s = {(bb, g): lax.dot_general(qg[bb, g], kk[bb][:, ds[g]], nt, preferred_element_type=F32)
             for bb, g in chains}
        s = {ch: jnp.where(ok, s[ch], NEG_INF) for ch in chains}
        m = {(bb, g): jnp.maximum(jnp.max(s[bb, g], axis=1, keepdims=True), sink[g]) for bb, g in chains}
        p = {ch: jnp.exp(s[ch] - m[ch]) for ch in chains}
        denom = {(bb, g): jnp.sum(p[bb, g], axis=1, keepdims=True) + jnp.exp(sink[g] - m[bb, g])
                 for bb, g in chains}
        o = {(bb, g): jnp.dot(p[bb, g].astype(BF16), vv[bb][:, ds[g]], preferred_element_type=F32) / denom[bb, g]
             for bb, g in chains}
        for bb, g in chains:
            for k, h in enumerate(heads[g]):
                o_ref[pl.ds(row0[bb], DEC_SEQ), h * A_HEAD_DIM:(h + 1) * A_HEAD_DIM] = (
                    o[bb, g][k * DEC_SEQ:(k + 1) * DEC_SEQ, :].astype(o_ref.dtype))
        return carry

    lax.fori_loop(0, BB_ATTN // 2, body, 0)


def _attn_sample(qk, rest, sinks, k_state, v_state):
    rows = BB_ATTN * DEC_SEQ
    base = N_PROMPT // rows
    kcol = A_Q // A_KV
    vcol = R_VA // A_KV
    sink_rows = jnp.broadcast_to(jnp.repeat(sinks, DEC_SEQ)[:, None], (A_HEADS * DEC_SEQ, LANES))
    return pl.pallas_call(
        _attn_sample_kernel,
        grid=(DEC_BATCH // BB_ATTN,),
        in_specs=[
            pl.BlockSpec((A_HEADS * DEC_SEQ, LANES), lambda i: (0, 0)),
            pl.BlockSpec((rows, A_Q), lambda i: (base + i, 0)),
            pl.BlockSpec((rows, A_KV), lambda i: (base + i, kcol)),
            pl.BlockSpec((rows, A_KV), lambda i: (base + i, vcol)),
            pl.BlockSpec((BB_ATTN, WINDOW, A_KV), lambda i: (i, 0, 0)),
            pl.BlockSpec((BB_ATTN, WINDOW, A_KV), lambda i: (i, 0, 0)),
        ],
        out_specs=pl.BlockSpec((rows, A_Q), lambda i: (i, 0)),
        out_shape=jax.ShapeDtypeStruct((N_SAMPLE, A_Q), BF16),
        compiler_params=_params(("parallel",)),
        name="attn_sample",
    )(sink_rows, qk, qk, rest, k_state, v_state)


def _split3(x):
    hi = x.astype(BF16)
    r1 = x - hi.astype(F32)
    mid = r1.astype(BF16)
    lo = (r1 - mid.astype(F32)).astype(BF16)
    return hi, mid, lo


GLA_INPUTS = ((LANES, R_LR), (B_QK, R_QB), (B_QK, R_KB), (B_V, R_VB), (B_V, R_GB))


def _gla_kernel(*refs, n_chunks, group, chunk, layer, n_in):
    ins, rest_refs = refs[:n_in], refs[n_in:]
    per = n_in // len(GLA_INPUTS)
    s0_ref, wa_ref, ba_ref, g_ref = rest_refs[:4]
    prev_ref = rest_refs[4] if layer else None
    o_ref, sout_ref, st_ref = rest_refs[-3:]
    ci = pl.program_id(1)
    c = chunk

    def rows(inp, g):
        if per == 1:
            return ins[inp][g * c:(g + 1) * c, :]
        return ins[inp * per + g][...]

    single = n_chunks == 1
    if not single:
        @pl.when(ci == 0)
        def _():
            for g in range(group):
                for h in range(B_HEADS):
                    st_ref[g, h] = s0_ref[0, g, h].T

    ri = lax.broadcasted_iota(jnp.int32, (c, c), 0)
    cj = lax.broadcasted_iota(jnp.int32, (c, c), 1)
    causal = cj <= ri
    tri = jnp.where(causal, 1.0, 0.0).astype(BF16)
    nt = (((1,), (1,)), ((), ()))
    G = range(group)
    z = [jnp.dot(rows(0, g).astype(BF16), wa_ref[...], preferred_element_type=F32) + ba_ref[...] for g in G]
    log_a = [-(jnp.maximum(-z[g], 0.0) + jnp.log1p(jnp.exp(-jnp.abs(z[g])))) / B_TAU for g in G]
    parts = [_split3(log_a[g]) for g in G]
    b = [jnp.dot(tri, parts[g][0], preferred_element_type=F32)
         + jnp.dot(tri, parts[g][1], preferred_element_type=F32)
         + jnp.dot(tri, parts[g][2], preferred_element_type=F32) for g in G]
    bl = [b[g][c - 1:c, :] for g in G]
    qd = [(rows(1, g) * (B_DK ** -0.5) * jnp.exp(b[g])).astype(BF16) for g in G]
    kd = [(rows(2, g) * jnp.exp(-b[g])).astype(BF16) for g in G]
    kl = [rows(2, g) * jnp.exp(bl[g] - b[g]) for g in G]
    kl = [kl[g].T if single else kl[g].astype(BF16) for g in G]
    ebl = [jnp.exp(bl[g]) for g in G]
    for h in range(B_HEADS):
        ks = slice(h * B_DK, (h + 1) * B_DK)
        vs = slice(h * B_DV, (h + 1) * B_DV)
        vh = [rows(3, g)[:, vs] for g in G]
        if single:
            st = [s0_ref[0, g, h] for g in G]
            o = [jnp.dot(qd[g][:, ks], st[g].astype(BF16), preferred_element_type=F32) for g in G]
        else:
            st = [st_ref[g, h] for g in G]
            o = [lax.dot_general(qd[g][:, ks], st[g].astype(BF16), nt, preferred_element_type=F32) for g in G]
        att = [lax.dot_general(qd[g][:, ks], kd[g][:, ks], nt, preferred_element_type=F32) for g in G]
        att = [jnp.where(causal, att[g], 0.0).astype(BF16) for g in G]
        o = [o[g] + jnp.dot(att[g], vh[g].astype(BF16), preferred_element_type=F32) for g in G]
        for g in G:
            if single:
                decay = jnp.broadcast_to(ebl[g][:, ks], (SUBLANES, B_DK)).T[:, 0:1]
                for l in range(layer):
                    sout_ref[l, g, h] = prev_ref[l, g, h]
                sout_ref[layer, g, h] = st[g] * decay + jnp.dot(kl[g][ks, :], vh[g],
                                                                  preferred_element_type=F32)
            else:
                st_ref[g, h] = st[g] * ebl[g][:, ks] + jnp.dot(vh[g].T.astype(BF16), kl[g][:, ks],
                                                               preferred_element_type=F32)
        o = [o[g] * lax.rsqrt(jnp.mean(o[g] * o[g], axis=1, keepdims=True) + RMS_EPS) * g_ref[h:h + 1, :]
             for g in G]
        for g in G:
            gate = rows(4, g)[:, vs]
            o_ref[g, :, vs] = (o[g] * (gate / (1.0 + jnp.exp(-gate)))).astype(o_ref.dtype)

    if not single:
        @pl.when(ci == n_chunks - 1)
        def _():
            for g in range(group):
                for l in range(layer):
                    sout_ref[l, g] = prev_ref[l, g]
                for h in range(B_HEADS):
                    sout_ref[layer, g, h] = st_ref[g, h].T


def _gla(rest, s0, layer, prev_states, wa, ba, gain, *, n_batch, seq, chunk, row_base, group):
    n_chunks = seq // chunk
    contiguous = n_chunks == 1
    base = row_base // chunk

    def in_specs_for(width, col):
        if contiguous:
            return [pl.BlockSpec((group * chunk, width), lambda b, ci: (base // group + b, col // width))]
        return [pl.BlockSpec((chunk, width),
                             lambda b, ci, g=g: (base + (b * group + g) * n_chunks + ci, col // width))
                for g in range(group)]

    row_specs = [spec for width, col in GLA_INPUTS for spec in in_specs_for(width, col)]
    s0_layer = layer if s0.shape[0] > 1 else 0
    state_block = (group, B_HEADS, B_DK, B_DV)
    prev_specs = [pl.BlockSpec((layer,) + state_block, lambda b, ci: (0, b, 0, 0, 0))] if layer else []
    prev_args = [prev_states] if layer else []
    return pl.pallas_call(
        functools.partial(_gla_kernel, n_chunks=n_chunks, group=group, chunk=chunk, layer=layer,
                          n_in=len(row_specs)),
        grid=(n_batch // group, n_chunks),
        in_specs=row_specs + [
            pl.BlockSpec((1,) + state_block, lambda b, ci: (s0_layer, b, 0, 0, 0)),
            pl.BlockSpec((LANES, B_QK), lambda b, ci: (0, 0)),
            pl.BlockSpec((1, B_QK), lambda b, ci: (0, 0)),
            pl.BlockSpec((B_HEADS, B_DV), lambda b, ci: (0, 0)),
        ] + prev_specs,
        out_specs=[
            pl.BlockSpec((group, chunk, B_V), lambda b, ci: (b, ci, 0)),
            pl.BlockSpec((layer + 1,) + state_block, lambda b, ci: (0, b, 0, 0, 0)),
        ],
        out_shape=[
            jax.ShapeDtypeStruct((n_batch, seq, B_V), BF16),
            jax.ShapeDtypeStruct((layer + 1, n_batch, B_HEADS, B_DK, B_DV), F32),
        ],
        scratch_shapes=[pltpu.VMEM((group, B_HEADS, B_DV, B_DK), F32)],
        compiler_params=_params(("parallel", "arbitrary")),
        name="gla",
    )(*([rest] * len(row_specs)), s0, wa, ba, gain, *prev_args)


HALO = 16
BB_POOL = 16


def _pool_kernel(u_ref, prev_ref, w_ref, scale_ref, o_ref, *, from_start, n_seq):
    tp = u_ref.shape[0] // n_seq
    if from_start:
        ti = pl.program_id(1)
        t0 = ti * tp
    diffs = [[] for _ in POOL_WINDOWS]
    for sq in range(n_seq):
        u = u_ref[sq * tp:(sq + 1) * tp, :]
        prev = prev_ref[sq * HALO:(sq + 1) * HALO, :]
        if from_start:
            prev = jnp.where(ti > 0, prev, 0.0)
        full = jnp.concatenate([prev, u], axis=0)
        for g, w in enumerate(POOL_WINDOWS):
            cs = slice(g * C_GROUP_W, (g + 1) * C_GROUP_W)
            acc = full[:, cs]
            span = 1
            while span < w:
                acc = acc + pltpu.roll(acc, span, axis=0)
                span *= 2
            wsum = acc[HALO:, :]
            if from_start:
                t = t0 + lax.broadcasted_iota(jnp.int32, (tp, C_GROUP_W), 0)
                cnt = jnp.minimum(t + 1, w).astype(F32)
            else:
                cnt = float(w)
            diffs[g].append(wsum / cnt - u[:, cs])
    for g in range(C_GROUPS):
        cs = slice(g * C_GROUP_W, (g + 1) * C_GROUP_W)
        d = jnp.concatenate(diffs[g], axis=0) if n_seq > 1 else diffs[g][0]
        y = jnp.dot(d.astype(BF16), w_ref[g], preferred_element_type=F32) * scale_ref[:, cs]
        o_ref[:, cs] = y.astype(o_ref.dtype)


def _pool_prompt(rest, w, scale):
    nt = SEQ // TP_POOL
    ucol = R_UC // C_WIDTH

    def halo(b, i):
        return (jnp.maximum((b * SEQ + i * TP_POOL) // HALO - 1, 0), ucol)

    return pl.pallas_call(
        functools.partial(_pool_kernel, from_start=True, n_seq=1),
        grid=(BATCH, nt),
        in_specs=[
            pl.BlockSpec((TP_POOL, C_WIDTH), lambda b, i: (b * nt + i, ucol)),
            pl.BlockSpec((HALO, C_WIDTH), halo),
            pl.BlockSpec((C_GROUPS, C_GROUP_W, C_GROUP_W), lambda b, i: (0, 0, 0)),
            pl.BlockSpec((1, C_WIDTH), lambda b, i: (0, 0)),
        ],
        out_specs=pl.BlockSpec((TP_POOL, C_WIDTH), lambda b, i: (b * nt + i, 0)),
        out_shape=jax.ShapeDtypeStruct((N_PROMPT, C_WIDTH), BF16),
        compiler_params=_params(("parallel", "arbitrary")),
        name="pool_prompt",
    )(rest, rest, w, scale)


def _pool_sample(rest, prev, w, scale):
    ucol = R_UC // C_WIDTH
    rows = BB_POOL * DEC_SEQ
    base = N_PROMPT // rows
    return pl.pallas_call(
        functools.partial(_pool_kernel, from_start=False, n_seq=BB_POOL),
        grid=(DEC_BATCH // BB_POOL,),
        in_specs=[
            pl.BlockSpec((rows, C_WIDTH), lambda b: (base + b, ucol)),
            pl.BlockSpec((BB_POOL * HALO, C_WIDTH), lambda b: (b, 0)),
            pl.BlockSpec((C_GROUPS, C_GROUP_W, C_GROUP_W), lambda b: (0, 0, 0)),
            pl.BlockSpec((1, C_WIDTH), lambda b: (0, 0)),
        ],
        out_specs=pl.BlockSpec((rows, C_WIDTH), lambda b: (b, 0)),
        out_shape=jax.ShapeDtypeStruct((N_SAMPLE, C_WIDTH), BF16),
        compiler_params=_params(("parallel",)),
        name="pool_sample",
    )(rest, prev, w, scale)


def _layer_norm(x, g, b):
    mu = jnp.mean(x, axis=1, keepdims=True)
    xc = x - mu
    var = jnp.mean(xc * xc, axis=1, keepdims=True)
    return xc * lax.rsqrt(var + LN_EPS) * g + b


def _merge_kernel(x_ref, gates_ref, oa_p_ref, oa_s_ref, ob_p_ref, ob_s_ref, oc_p_ref, oc_s_ref,
                  wa_ref, wb_ref, wc_ref, wo_ref, g_ref, b_ref, o_ref, oT_ref):
    in_sample = pl.program_id(0) >= N_PROMPT // TM_MERGE

    def gate(i):
        z = gates_ref[:, i * D_MODEL:(i + 1) * D_MODEL]
        return 1.0 / (1.0 + jnp.exp(-z))

    def branch(p_ref, s_ref, w_ref):
        o = jnp.where(in_sample, s_ref[...], p_ref[...])
        return jnp.dot(o, w_ref[...], preferred_element_type=F32)

    merged = (gate(0) * branch(oa_p_ref, oa_s_ref, wa_ref)
              + gate(1) * branch(ob_p_ref, ob_s_ref, wb_ref)
              + gate(2) * branch(oc_p_ref, oc_s_ref, wc_ref))
    mix = jnp.dot(merged.astype(BF16), wo_ref[...], preferred_element_type=F32)
    y = _layer_norm(DN_ALPHA * x_ref[...] + mix, g_ref[...], b_ref[...])
    o_ref[...] = y
    oT_ref[...] = y.T.astype(BF16)


def _merge(x, rest, branches, wa, wb, wc, wo, g, b):
    n_p = N_PROMPT // TM_MERGE
    row = pl.BlockSpec((TM_MERGE, D_MODEL), lambda i: (i, 0))
    row_p = pl.BlockSpec((TM_MERGE, D_MODEL), lambda i: (jnp.minimum(i, n_p - 1), 0))
    row_s = pl.BlockSpec((TM_MERGE, D_MODEL), lambda i: (jnp.maximum(i - n_p, 0), 0))
    wspec = pl.BlockSpec((D_MODEL, D_MODEL), lambda i: (0, 0))
    vec = pl.BlockSpec((1, D_MODEL), lambda i: (0, 0))
    (oa_p, oa_s), (ob_p, ob_s), (oc_p, oc_s) = branches
    return pl.pallas_call(
        _merge_kernel,
        grid=(N_TOK // TM_MERGE,),
        in_specs=[row, pl.BlockSpec((TM_MERGE, 3 * D_MODEL), lambda i: (i, R_GATES)),
                  row_p, row_s, row_p, row_s, row_p, row_s, wspec, wspec, wspec, wspec, vec, vec],
        out_specs=[row, pl.BlockSpec((D_MODEL, TM_MERGE), lambda i: (0, i))],
        out_shape=[jax.ShapeDtypeStruct((N_TOK, D_MODEL), F32),
                   jax.ShapeDtypeStruct((D_MODEL, N_TOK), BF16)],
        compiler_params=_params(("parallel",)),
        name="merge",
    )(x, rest, oa_p, oa_s, ob_p, ob_s, oc_p, oc_s, wa, wb, wc, wo, g, b)


def _merge_sort_pairs(n):
    size = 1
    while size < n:
        size *= 2
    pairs = []
    p = 1
    while p < size:
        k = p
        while k >= 1:
            for j in range(k % p, size - k, 2 * k):
                for i in range(min(k, size - j - k)):
                    if (i + j) // (2 * p) == (i + j + k) // (2 * p):
                        pairs.append((i + j, i + j + k))
            k //= 2
        p *= 2
    return [(a, b) for a, b in pairs if b < n]


def _top_desc(s, n):
    v = [s[SUBLANES * k:SUBLANES * (k + 1), :] for k in range(s.shape[0] // SUBLANES)]
    depth = len(v)
    for a, b in _merge_sort_pairs(depth):
        v[a], v[b] = jnp.maximum(v[a], v[b]), jnp.minimum(v[a], v[b])
    vals = []
    for r in range(n):
        m = jnp.max(v[0], axis=0, keepdims=True)
        vals.append(m)
        hit = v[0] == m
        needed = n - r - 1
        for k in range(min(depth - 1, needed)):
            v[k] = jnp.where(hit, v[k + 1], v[k])
        if needed >= depth:
            v[depth - 1] = jnp.where(hit, -jnp.inf, v[depth - 1])
    return vals


def _rank_of(s, vals):
    rank = jnp.full(s.shape, float(len(vals)), F32)
    for r, val in enumerate(vals):
        rank = jnp.where(s == val, float(r), rank)
    return rank


def _peer_topk_kernel(xT_ref, wq_ref, sk_ref, cnt_ref, e1_ref, rank_ref, e2_ref, s1_scr, sv_scr):
    tt = xT_ref.shape[1]
    xT = xT_ref[...]
    for hp in range(2 * PEER_HEADS):
        h, second = divmod(hp, 2)
        qT = jnp.dot(wq_ref[hp * PEER_DHALF:(hp + 1) * PEER_DHALF, :], xT, preferred_element_type=F32)
        s = jnp.dot(sk_ref[hp], qT.astype(BF16), preferred_element_type=F32)
        vals = _top_desc(s, PEER_TOPK)
        sv_scr[hp] = jnp.concatenate(vals, axis=0)
        if second:
            rank_ref[h] = _rank_of(s, vals).astype(BF16)
            e2_ref[h] = jnp.exp(s - vals[0]).astype(BF16)
        else:
            s1_scr[h] = s
    row8 = lax.broadcasted_iota(jnp.int32, (8, tt), 0)
    for h in range(PEER_HEADS):
        sv1 = sv_scr[2 * h]
        sv2 = sv_scr[2 * h + 1]
        pieces = [sv1[0:1, :] + sv2]
        for a in range(1, 8):
            nb = PEER_TOPK // (a + 1)
            pieces.append(jnp.where(row8 < nb, sv1[a:a + 1, :] + sv2[0:8, :], -jnp.inf))
        pieces.append(sv2[0:1, :] + sv1[8:16, :])
        cand = _top_desc(jnp.concatenate(pieces, axis=0), PEER_TOPK)
        top = cand[0]
        z = jnp.zeros_like(top)
        for r in range(PEER_TOPK):
            z = z + jnp.exp(cand[r] - top)
        kth = cand[PEER_TOPK - 1]
        counts = [jnp.sum(jnp.where(p >= kth, 1.0, 0.0), axis=0, keepdims=True) for p in pieces[:8]]
        tail = jnp.where(pieces[8] >= kth, 1.0, 0.0)
        counts += [tail[a:a + 1, :] for a in range(8)]
        s1 = s1_scr[h]
        cnt = jnp.zeros_like(s1)
        for a in range(PEER_TOPK):
            cnt = jnp.where(s1 == sv1[a:a + 1, :], counts[a], cnt)
        e1 = jnp.exp(s1 - (sv1[0:1, :] + jnp.log(z))) * SQRT_HALF
        for c in range(tt // LANES):
            cnt_ref[h, c] = cnt[:, c * LANES:(c + 1) * LANES]
            e1_ref[h, c] = e1[:, c * LANES:(c + 1) * LANES]


def _peer_topk(xT, wqT, sk):
    spec = pl.BlockSpec((PEER_HEADS, N_KEYS, TT_TOPK), lambda t: (0, 0, t))
    row_spec = pl.BlockSpec((PEER_HEADS, TT_TOPK // LANES, N_KEYS, LANES), lambda t: (0, t, 0, 0))
    wide = jax.ShapeDtypeStruct((PEER_HEADS, N_TOK // LANES, N_KEYS, LANES), F32)
    narrow = jax.ShapeDtypeStruct((PEER_HEADS, N_KEYS, N_TOK), BF16)
    return pl.pallas_call(
        _peer_topk_kernel,
        grid=(N_TOK // TT_TOPK,),
        in_specs=[
            pl.BlockSpec((D_MODEL, TT_TOPK), lambda t: (0, t)),
            pl.BlockSpec((PEER_HEADS * PEER_DKEY, D_MODEL), lambda t: (0, 0)),
            pl.BlockSpec((2 * PEER_HEADS, N_KEYS, PEER_DHALF), lambda t: (0, 0, 0)),
        ],
        out_specs=[row_spec, row_spec, spec, spec],
        out_shape=[wide, wide, narrow, narrow],
        scratch_shapes=[pltpu.VMEM((PEER_HEADS, N_KEYS, TT_TOPK), F32),
                        pltpu.VMEM((2 * PEER_HEADS, PEER_TOPK, TT_TOPK), F32)],
        compiler_params=_params(("parallel",)),
        name="peer_topk",
    )(xT, wqT, sk)


SQRT_HALF = 2.0 ** -0.5


def _gelu_unscaled(x):
    t = x * SQRT_HALF
    return t * (1.0 + lax.erf(t))


LC_PEER = 256


def _bf16_rows(ref, h, r, chunks, n_rows):
    x = jnp.concatenate([ref[h, c, pl.ds(r, 2 * SUBLANES, stride=0), :] for c in chunks], axis=1)
    packed = x.astype(BF16)
    return jnp.concatenate([packed] * (n_rows // packed.shape[0]), axis=0)


MM_PIECE = 512


def _peer_main_kernel(xT_ref, u_ref, vt_ref, cnt_ref, e1_ref, rank_ref, e2_ref, x_ref, g_ref, b_ref,
                      o_ref, ob_ref, yT_ref, s_scr, wh_scr):
    i = pl.program_id(1)
    tt = xT_ref.shape[1]

    @pl.when(i == 0)
    def _():
        yT_ref[...] = jnp.zeros_like(yT_ref)

    zero = jnp.zeros((N_KEYS, LC_PEER), BF16)
    per_piece = MM_PIECE // N_KEYS
    for ii in range(IB_PEER):
        rows = slice(ii * N_KEYS, (ii + 1) * N_KEYS)
        if ii % per_piece == 0:
            piece = slice(ii * N_KEYS, ii * N_KEYS + MM_PIECE)
            s_scr[piece, :] = jnp.dot(u_ref[0, piece, :], xT_ref[...], preferred_element_type=F32)
        for lc in range(tt // LC_PEER):
            cols = slice(lc * LC_PEER, (lc + 1) * LC_PEER)
            chunks = range(lc * LC_PEER // LANES, (lc + 1) * LC_PEER // LANES)
            w = zero
            for h in range(PEER_HEADS):
                cnt = _bf16_rows(cnt_ref, h, ii, chunks, N_KEYS)
                e1 = _bf16_rows(e1_ref, h, ii, chunks, N_KEYS)
                w = w + jnp.where(rank_ref[h, :, cols] < cnt, e2_ref[h, :, cols] * e1, zero)
            wh_scr[rows, cols] = w
    for ii in range(IB_PEER):
        rows = slice(ii * N_KEYS, (ii + 1) * N_KEYS)
        wh_scr[rows, :] = wh_scr[rows, :] * _gelu_unscaled(s_scr[rows, :]).astype(BF16)
    yT_ref[...] += jnp.dot(vt_ref[0], wh_scr[...], preferred_element_type=F32)

    @pl.when(i == pl.num_programs(1) - 1)
    def _():
        y = _layer_norm(DN_ALPHA * x_ref[...] + yT_ref[...].T, g_ref[...], b_ref[...])
        o_ref[...] = y
        ob_ref[...] = y.astype(BF16)


def _peer_main(xT, u, vt, layer, cnt, e1, rank, e2, x1, g, b):
    eb = IB_PEER * N_KEYS
    row_spec = pl.BlockSpec((PEER_HEADS, TT_PEER // LANES, IB_PEER, LANES), lambda t, i: (0, t, i, 0))
    tab_spec = pl.BlockSpec((PEER_HEADS, N_KEYS, TT_PEER), lambda t, i: (0, 0, t))
    tok_spec = pl.BlockSpec((TT_PEER, D_MODEL), lambda t, i: (t, 0))
    vec = pl.BlockSpec((1, D_MODEL), lambda t, i: (0, 0))
    return pl.pallas_call(
        _peer_main_kernel,
        grid=(N_TOK // TT_PEER, N_EXPERTS // eb),
        in_specs=[
            pl.BlockSpec((D_MODEL, TT_PEER), lambda t, i: (0, t)),
            pl.BlockSpec((1, eb, D_MODEL), lambda t, i: (layer, i, 0)),
            pl.BlockSpec((1, D_MODEL, eb), lambda t, i: (layer, 0, i)),
            row_spec, row_spec, tab_spec, tab_spec, tok_spec, vec, vec,
        ],
        out_specs=[tok_spec, tok_spec],
        out_shape=[jax.ShapeDtypeStruct((N_TOK, D_MODEL), F32),
                   jax.ShapeDtypeStruct((N_TOK, D_MODEL), BF16)],
        scratch_shapes=[pltpu.VMEM((D_MODEL, TT_PEER), F32),
                        pltpu.VMEM((eb, TT_PEER), F32), pltpu.VMEM((eb, TT_PEER), BF16)],
        compiler_params=_params(("parallel", "arbitrary")),
        name="peer_main",
    )(xT, u, vt, cnt, e1, rank, e2, x1, g, b)


def _rope_tables():
    half = ROT_DIM // 2
    pos = jnp.concatenate([jnp.arange(SEQ), PAST_LEN + (jnp.arange(TM_QK) % DEC_SEQ)])
    inv = ROPE_THETA ** (-jnp.arange(half, dtype=F32) / half)
    ang = pos.astype(F32)[:, None] * inv[None, :]
    cos, sin = jnp.cos(ang), jnp.sin(ang)
    n = pos.shape[0]
    one = jnp.ones((n, A_HEAD_DIM - ROT_DIM), F32)
    zero = jnp.zeros((n, A_HEAD_DIM - ROT_DIM), F32)
    zh = jnp.zeros((n, half), F32)
    reps = LANES // A_HEAD_DIM
    c = jnp.tile(jnp.concatenate([cos, cos, one], 1), (1, reps))
    s1 = jnp.tile(jnp.concatenate([-sin, zh, zero], 1), (1, reps))
    s2 = jnp.tile(jnp.concatenate([zh, sin, zero], 1), (1, reps))
    return c, s1, s2


def _split_cols(w):
    cuts = [int(c) for c in np.cumsum(SPLITS)[:-1]]
    return jnp.split(w, cuts, axis=-1)


def _layer(layer, x, xb, k_state, v_state, gla_states, gla_out, pool_state, rope, w_in, b_in, sinks,
           w_alpha, b_alpha, gla_g, w_pool, pool_scale, w_a, w_b, w_c, w_out, ln1_g, ln1_b,
           peer_query, peer_subkeys, peer_u, peer_vt, ln2_g, ln2_b):
    qa_w, ka_w, va_w, qb_w, kb_w, vb_w, lr_w, gb_w, uc_w, gates_w = _split_cols(w_in)
    qa_b, ka_b, va_b, qb_b, kb_b, vb_b, lr_b, gb_b, uc_b, gates_b = _split_cols(b_in[None, :])
    lr_pad = LANES - B_GATE_RANK
    w_qk = jnp.concatenate([qa_w, ka_w], 1).astype(BF16)
    b_qk = jnp.concatenate([qa_b, ka_b], 1)
    w_rest = jnp.concatenate([gates_w, vb_w, gb_w, uc_w, qb_w, kb_w, va_w,
                              jnp.pad(lr_w, ((0, 0), (0, lr_pad)))], 1).astype(BF16)
    b_rest = jnp.concatenate([gates_b, vb_b, gb_b, uc_b, qb_b, kb_b, va_b,
                              jnp.pad(lr_b, ((0, 0), (0, lr_pad)))], 1)

    qk = _proj_qk(xb, w_qk, b_qk, *rope)
    rest = _proj_rest(xb, w_rest, b_rest)

    ks = k_state.reshape(DEC_BATCH, WINDOW, A_KV)
    vs = v_state.reshape(DEC_BATCH, WINDOW, A_KV)
    oa = (_attn_prompt(qk, rest, sinks), _attn_sample(qk, rest, sinks, ks, vs))

    wa = jnp.pad(w_alpha, ((0, lr_pad), (0, 0))).astype(BF16)
    ba = b_alpha[None, :]
    gla_p, gla_s = gla_out
    ob_p, gla_p = _gla(rest, jnp.zeros((1, BATCH, B_HEADS, B_DK, B_DV), F32), layer, gla_p, wa, ba, gla_g,
                       n_batch=BATCH, seq=SEQ, chunk=GLA_CHUNK, row_base=0, group=GLA_GROUP_PROMPT)
    ob_s, gla_s = _gla(rest, gla_states, layer, gla_s, wa, ba, gla_g, n_batch=DEC_BATCH, seq=DEC_SEQ,
                       chunk=math.gcd(DEC_SEQ, GLA_CHUNK), row_base=N_PROMPT, group=GLA_GROUP_SAMPLE)
    ob = (ob_p.reshape(N_PROMPT, B_V), ob_s.reshape(N_SAMPLE, B_V))

    wp = w_pool.astype(BF16)
    ps = pool_scale[None, :]
    prev = jnp.pad(pool_state, ((0, 0), (HALO - POOL_STATE, 0), (0, 0))).reshape(DEC_BATCH * HALO, C_WIDTH)
    oc = (_pool_prompt(rest, wp, ps), _pool_sample(rest, prev, wp, ps))

    x1, x1T = _merge(x, rest, (oa, ob, oc), w_a.astype(BF16), w_b.astype(BF16), w_c.astype(BF16),
                     w_out.astype(BF16), ln1_g[None, :], ln1_b[None, :])

    wqT = peer_query.reshape(D_MODEL, PEER_HEADS * PEER_DKEY).T.astype(BF16)
    sk = peer_subkeys.reshape(2 * PEER_HEADS, N_KEYS, PEER_DHALF).astype(BF16)
    cnt, e1, rank, e2 = _peer_topk(x1T, wqT, sk)
    x2, x2b = _peer_main(x1T, peer_u, peer_vt, layer, cnt, e1, rank, e2, x1, ln2_g[None, :], ln2_b[None, :])

    def prompt_tail(t, col0, width, n):
        return jnp.stack([t[(b + 1) * SEQ - n:(b + 1) * SEQ, col0:col0 + width] for b in range(BATCH)])

    def sample_tail(state, t, col0, width, n):
        new = t[N_PROMPT:, col0:col0 + width].reshape(DEC_BATCH, DEC_SEQ, width)
        return jnp.concatenate([state, new], 1)[:, -n:]

    kv_shape = (-1, WINDOW, A_KV_HEADS, A_HEAD_DIM)
    states = (prompt_tail(qk, A_Q, A_KV, WINDOW).reshape(kv_shape),
              prompt_tail(rest, R_VA, A_KV, WINDOW).reshape(kv_shape),
              prompt_tail(rest, R_UC, C_WIDTH, POOL_STATE),
              sample_tail(ks, qk, A_Q, A_KV, WINDOW).reshape(kv_shape),
              sample_tail(vs, rest, R_VA, A_KV, WINDOW).reshape(kv_shape),
              sample_tail(pool_state, rest, R_UC, C_WIDTH, POOL_STATE))
    return x2, x2b, states, (gla_p, gla_s)


def kernel(x_prompt, x_sample, state_win_k, state_win_v, state_gla, state_pool, w_in, b_in, attn_sinks,
           w_alpha, b_alpha, gla_norm_g, w_pool, pool_scale, w_branch_a, w_branch_b, w_branch_c, w_out,
           ln1_g, ln1_b, peer_query, peer_subkeys, peer_u, peer_v, ln2_g, ln2_b):
    x = jnp.concatenate([x_prompt.reshape(N_PROMPT, D_MODEL), x_sample.reshape(N_SAMPLE, D_MODEL)], 0)
    xb = x.astype(BF16)
    rope = _rope_tables()
    peer_ub = peer_u.astype(BF16)
    peer_vtb = jnp.swapaxes(peer_v, 1, 2).astype(BF16)
    per_layer = []
    gla_out = (None, None)
    for l in range(DEPTH):
        x, xb, states, gla_out = _layer(
            l, x, xb, state_win_k[l], state_win_v[l], state_gla, gla_out, state_pool[l], rope,
            w_in[l], b_in[l], attn_sinks[l], w_alpha[l], b_alpha[l], gla_norm_g[l],
            w_pool[l], pool_scale[l], w_branch_a[l], w_branch_b[l], w_branch_c[l], w_out[l],
            ln1_g[l], ln1_b[l], peer_query[l], peer_subkeys[l], peer_ub, peer_vtb, ln2_g[l], ln2_b[l])
        per_layer.append(states)
    pk, pv, pp, sk, sv, sp = [jnp.stack([per_layer[l][i] for l in range(DEPTH)]) for i in range(6)]
    return (x[:N_PROMPT].reshape(BATCH, SEQ, D_MODEL), x[N_PROMPT:].reshape(DEC_BATCH, DEC_SEQ, D_MODEL),
            pk, pv, gla_out[0], pp, sk, sv, gla_out[1], sp)
```

```python
import functools
import math

import jax
import jax.numpy as jnp
import numpy as np
from jax import lax
from jax.experimental import pallas as pl
from jax.experimental.pallas import tpu as pltpu

F32 = jnp.float32
BF16 = jnp.bfloat16

D_MODEL = 1024
BATCH = 8
SEQ = 2048
DEPTH = 2
DEC_BATCH = 128
DEC_SEQ = 8
PAST_LEN = 16384

A_HEADS = 16
A_KV_HEADS = 2
A_HEAD_DIM = 64
A_GROUP = A_HEADS // A_KV_HEADS
WINDOW = 128
ROT_DIM = A_HEAD_DIM // 4
ROPE_THETA = 500000.0
NEG_INF = -1e30
B_HEADS = 4
B_DK = D_MODEL // 2 // B_HEADS
B_DV = D_MODEL // B_HEADS
B_GATE_RANK = 16
B_TAU = 16.0
GLA_CHUNK = 64
POOL_WINDOWS = (2, 4, 8, 16)
C_GROUPS = len(POOL_WINDOWS)
C_GROUP_W = D_MODEL // C_GROUPS
C_WIDTH = C_GROUPS * C_GROUP_W
POOL_STATE = max(POOL_WINDOWS) - 1
PEER_HEADS = 8
N_KEYS = 128
N_EXPERTS = N_KEYS * N_KEYS
PEER_TOPK = 16
PEER_DKEY = 256
PEER_DHALF = PEER_DKEY // 2
DN_ALPHA = (2 * DEPTH) ** 0.25
LN_EPS = 1e-5
RMS_EPS = 1e-6

A_Q = A_HEADS * A_HEAD_DIM
A_KV = A_KV_HEADS * A_HEAD_DIM
B_QK = B_HEADS * B_DK
B_V = B_HEADS * B_DV
SPLITS = (A_Q, A_KV, A_KV, B_QK, B_QK, B_V, B_GATE_RANK, B_V, C_WIDTH, 3 * D_MODEL)

LANES = 128
SUBLANES = 8
N_PROMPT = BATCH * SEQ
N_SAMPLE = DEC_BATCH * DEC_SEQ
N_TOK = N_PROMPT + N_SAMPLE

R_GATES = 0
R_VB = 3 * D_MODEL
R_GB = R_VB + B_V
R_UC = R_GB + B_V
R_QB = R_UC + C_WIDTH
R_KB = R_QB + B_QK
R_VA = R_KB + B_QK
R_LR = R_VA + A_KV
R_WIDTH = R_LR + LANES
QK_WIDTH = A_Q + A_KV

VMEM_LIMIT = 48 * 1024 * 1024

TM_QK = 512
TM_REST = 512
TN_REST = R_WIDTH // 2
TM_MERGE = 256
TP_POOL = 512
GLA_GROUP_PROMPT = 4
GLA_GROUP_SAMPLE = 4
TT_TOPK = 512
TT_PEER = 512
IB_PEER = 16


def _params(sem):
    return pltpu.CompilerParams(dimension_semantics=sem, vmem_limit_bytes=VMEM_LIMIT)


def _qk_kernel(x_ref, w_ref, b_ref, c_ref, s1_ref, s2_ref, o_ref):
    y = jnp.dot(x_ref[...], w_ref[...], preferred_element_type=F32) + b_ref[...]
    c = c_ref[...]
    s1 = s1_ref[...]
    s2 = s2_ref[...]
    for j in range(QK_WIDTH // LANES):
        yj = y[:, j * LANES:(j + 1) * LANES]
        up = pltpu.roll(yj, LANES - ROT_DIM // 2, axis=1)
        dn = pltpu.roll(yj, ROT_DIM // 2, axis=1)
        o_ref[:, j * LANES:(j + 1) * LANES] = yj * c + up * s1 + dn * s2


def _proj_qk(xb, w, b, rope_c, rope_s1, rope_s2):
    n_prompt_blocks = SEQ // TM_QK

    def tab_map(i):
        return (jnp.where(i < N_PROMPT // TM_QK, i % n_prompt_blocks, n_prompt_blocks), 0)

    tab_spec = pl.BlockSpec((TM_QK, LANES), tab_map)
    return pl.pallas_call(
        _qk_kernel,
        grid=(N_TOK // TM_QK,),
        in_specs=[
            pl.BlockSpec((TM_QK, D_MODEL), lambda i: (i, 0)),
            pl.BlockSpec((D_MODEL, QK_WIDTH), lambda i: (0, 0)),
            pl.BlockSpec((1, QK_WIDTH), lambda i: (0, 0)),
            tab_spec, tab_spec, tab_spec,
        ],
        out_specs=pl.BlockSpec((TM_QK, QK_WIDTH), lambda i: (i, 0)),
        out_shape=jax.ShapeDtypeStruct((N_TOK, QK_WIDTH), F32),
        compiler_params=_params(("parallel",)),
        name="proj_qk",
    )(xb, w, b, rope_c, rope_s1, rope_s2)


def _mm_bias_kernel(x_ref, w_ref, b_ref, o_ref):
    o_ref[...] = jnp.dot(x_ref[...], w_ref[...], preferred_element_type=F32) + b_ref[...]


def _proj_rest(xb, w, b):
    return pl.pallas_call(
        _mm_bias_kernel,
        grid=(R_WIDTH // TN_REST, N_TOK // TM_REST),
        in_specs=[
            pl.BlockSpec((TM_REST, D_MODEL), lambda j, i: (i, 0)),
            pl.BlockSpec((D_MODEL, TN_REST), lambda j, i: (0, j)),
            pl.BlockSpec((1, TN_REST), lambda j, i: (0, j)),
        ],
        out_specs=pl.BlockSpec((TM_REST, TN_REST), lambda j, i: (i, j)),
        out_shape=jax.ShapeDtypeStruct((N_TOK, R_WIDTH), F32),
        compiler_params=_params(("parallel", "arbitrary")),
        name="proj_rest",
    )(xb, w, b)


HEADS_PER_PASS = 16


def _attend(q, kk, vv, sink_ref, c_min, o_ref, row0):
    tq = q.shape[0]
    r = lax.broadcasted_iota(jnp.int32, (tq, 2 * WINDOW), 0)
    c = lax.broadcasted_iota(jnp.int32, (tq, 2 * WINDOW), 1)
    ok = (c > r) & (c <= r + WINDOW) & (c >= c_min)
    qb = (q * (A_HEAD_DIM ** -0.5)).astype(BF16)
    nt = (((1,), (1,)), ((), ()))
    for h0 in range(0, A_HEADS, HEADS_PER_PASS):
        hs = range(h0, h0 + HEADS_PER_PASS)
        col = {h: slice(h * A_HEAD_DIM, (h + 1) * A_HEAD_DIM) for h in hs}
        kv = {h: slice((h // A_GROUP) * A_HEAD_DIM, (h // A_GROUP + 1) * A_HEAD_DIM) for h in hs}
        s = {h: lax.dot_general(qb[:, col[h]], kk[:, kv[h]], nt, preferred_element_type=F32) for h in hs}
        s = {h: jnp.where(ok, s[h], NEG_INF) for h in hs}
        m = {h: jnp.maximum(jnp.max(s[h], axis=1, keepdims=True), sink_ref[h]) for h in hs}
        p = {h: jnp.exp(s[h] - m[h]) for h in hs}
        denom = {h: jnp.sum(p[h], axis=1, keepdims=True) + jnp.exp(sink_ref[h] - m[h]) for h in hs}
        o = {h: jnp.dot(p[h].astype(BF16), vv[:, kv[h]], preferred_element_type=F32) / denom[h] for h in hs}
        for h in hs:
            o_ref[pl.ds(row0, tq), col[h]] = o[h].astype(o_ref.dtype)


def _attn_prompt_kernel(sink_ref, q_ref, kc_ref, kp_ref, vc_ref, vp_ref, o_ref):
    n = pl.program_id(1)
    kk = jnp.concatenate([kp_ref[...], kc_ref[...]], axis=0).astype(BF16)
    vv = jnp.concatenate([vp_ref[...], vc_ref[...]], axis=0).astype(BF16)
    _attend(q_ref[...], kk, vv, sink_ref, jnp.where(n > 0, 0, WINDOW), o_ref, 0)


def _attn_prompt(qk, rest, sinks):
    nb = SEQ // WINDOW
    kcol = A_Q // A_KV
    vcol = R_VA // A_KV

    def cur(b, n):
        return b * nb + n

    def prev(b, n):
        return b * nb + jnp.maximum(n - 1, 0)

    return pl.pallas_call(
        _attn_prompt_kernel,
        grid=(BATCH, nb),
        in_specs=[
            pl.BlockSpec(memory_space=pltpu.SMEM),
            pl.BlockSpec((WINDOW, A_Q), lambda b, n: (cur(b, n), 0)),
            pl.BlockSpec((WINDOW, A_KV), lambda b, n: (cur(b, n), kcol)),
            pl.BlockSpec((WINDOW, A_KV), lambda b, n: (prev(b, n), kcol)),
            pl.BlockSpec((WINDOW, A_KV), lambda b, n: (cur(b, n), vcol)),
            pl.BlockSpec((WINDOW, A_KV), lambda b, n: (prev(b, n), vcol)),
        ],
        out_specs=pl.BlockSpec((WINDOW, A_Q), lambda b, n: (cur(b, n), 0)),
        out_shape=jax.ShapeDtypeStruct((N_PROMPT, A_Q), BF16),
        compiler_params=_params(("parallel", "arbitrary")),
        name="attn_prompt",
    )(sinks, qk, qk, qk, rest, rest)


BB_ATTN = 8


def _attn_sample_kernel(sink_ref, q_ref, kn_ref, vn_ref, ks_ref, vs_ref, o_ref):
    pad = jnp.zeros((WINDOW - DEC_SEQ, A_KV), F32)
    rows = A_GROUP * DEC_SEQ
    t = lax.broadcasted_iota(jnp.int32, (rows, 2 * WINDOW), 0) % DEC_SEQ
    c = lax.broadcasted_iota(jnp.int32, (rows, 2 * WINDOW), 1)
    ok = (c > t) & (c <= t + WINDOW)
    nt = (((1,), (1,)), ((), ()))

    def body(pair, carry):
        chains = [(e, g) for e in range(2) for g in range(A_KV_HEADS)]
        elem = {e: pair * 2 + e for e in range(2)}
        row0 = {e: pl.multiple_of(elem[e] * DEC_SEQ, DEC_SEQ) for e in range(2)}
        q = {e: q_ref[pl.ds(row0[e], DEC_SEQ), :] * (A_HEAD_DIM ** -0.5) for e in range(2)}
        kk = {e: jnp.concatenate([ks_ref[elem[e]], kn_ref[pl.ds(row0[e], DEC_SEQ), :], pad], axis=0).astype(BF16)
              for e in range(2)}
        vv = {e: jnp.concatenate([vs_ref[elem[e]], vn_ref[pl.ds(row0[e], DEC_SEQ), :], pad], axis=0).astype(BF16)
              for e in range(2)}
        heads = {g: range(g * A_GROUP, (g + 1) * A_GROUP) for g in range(A_KV_HEADS)}
        ds = {g: slice(g * A_HEAD_DIM, (g + 1) * A_HEAD_DIM) for g in range(A_KV_HEADS)}
        sink = {g: sink_ref[g * rows:(g + 1) * rows, 0:1] for g in range(A_KV_HEADS)}
        qg = {(bb, g): jnp.concatenate([q[bb][:, h * A_HEAD_DIM:(h + 1) * A_HEAD_DIM] for h in heads[g]],
                                       axis=0).astype(BF16) for bb, g in chains}
        s = {(bb, g): lax.dot_general(qg[bb, g], kk[bb][:, ds[g]], nt, preferred_element_type=F32)
             for bb, g in chains}
        s = {ch: jnp.where(ok, s[ch], NEG_INF) for ch in chains}
        m = {(bb, g): jnp.maximum(jnp.max(s[bb, g], axis=1, keepdims=True), sink[g]) for bb, g in chains}
        p = {ch: jnp.exp(s[ch] - m[ch]) for ch in chains}
        denom = {(bb, g): jnp.sum(p[bb, g], axis=1, keepdims=True) + jnp.exp(sink[g] - m[bb, g])
                 for bb, g in chains}
        o = {(bb, g): jnp.dot(p[bb, g].astype(BF16), vv[bb][:, ds[g]], preferred_element_type=F32) / denom[bb, g]
             for bb, g in chains}
        for bb, g in chains:
            for k, h in enumerate(heads[g]):
                o_ref[pl.ds(row0[bb], DEC_SEQ), h * A_HEAD_DIM:(h + 1) * A_HEAD_DIM] = (
                    o[bb, g][k * DEC_SEQ:(k + 1) * DEC_SEQ, :].astype(o_ref.dtype))
        return carry

    lax.fori_loop(0, BB_ATTN // 2, body, 0)


def _attn_sample(qk, rest, sinks, k_state, v_state):
    rows = BB_ATTN * DEC_SEQ
    base = N_PROMPT // rows
    kcol = A_Q // A_KV
    vcol = R_VA // A_KV
    sink_rows = jnp.broadcast_to(jnp.repeat(sinks, DEC_SEQ)[:, None], (A_HEADS * DEC_SEQ, LANES))
    return pl.pallas_call(
        _attn_sample_kernel,
        grid=(DEC_BATCH // BB_ATTN,),
        in_specs=[
            pl.BlockSpec((A_HEADS * DEC_SEQ, LANES), lambda i: (0, 0)),
            pl.BlockSpec((rows, A_Q), lambda i: (base + i, 0)),
            pl.BlockSpec((rows, A_KV), lambda i: (base + i, kcol)),
            pl.BlockSpec((rows, A_KV), lambda i: (base + i, vcol)),
            pl.BlockSpec((BB_ATTN, WINDOW, A_KV), lambda i: (i, 0, 0)),
            pl.BlockSpec((BB_ATTN, WINDOW, A_KV), lambda i: (i, 0, 0)),
        ],
        out_specs=pl.BlockSpec((rows, A_Q), lambda i: (i, 0)),
        out_shape=jax.ShapeDtypeStruct((N_SAMPLE, A_Q), BF16),
        compiler_params=_params(("parallel",)),
        name="attn_sample",
    )(sink_rows, qk, qk, rest, k_state, v_state)


def _split3(x):
    hi = x.astype(BF16)
    r1 = x - hi.astype(F32)
    mid = r1.astype(BF16)
    lo = (r1 - mid.astype(F32)).astype(BF16)
    return hi, mid, lo


GLA_INPUTS = ((LANES, R_LR), (B_QK, R_QB), (B_QK, R_KB), (B_V, R_VB), (B_V, R_GB))


def _gla_kernel(*refs, n_chunks, group, chunk, layer, n_in):
    ins, rest_refs = refs[:n_in], refs[n_in:]
    per = n_in // len(GLA_INPUTS)
    s0_ref, wa_ref, ba_ref, g_ref = rest_refs[:4]
    prev_ref = rest_refs[4] if layer else None
    o_ref, sout_ref, st_ref = rest_refs[-3:]
    ci = pl.program_id(1)
    c = chunk

    def rows(inp, g):
        if per == 1:
            return ins[inp][g * c:(g + 1) * c, :]
        return ins[inp * per + g][...]

    single = n_chunks == 1
    if not single:
        @pl.when(ci == 0)
        def _():
            for g in range(group):
                for h in range(B_HEADS):
                    st_ref[g, h] = s0_ref[0, g, h].T

    ri = lax.broadcasted_iota(jnp.int32, (c, c), 0)
    cj = lax.broadcasted_iota(jnp.int32, (c, c), 1)
    causal = cj <= ri
    tri = jnp.where(causal, 1.0, 0.0).astype(BF16)
    nt = (((1,), (1,)), ((), ()))
    G = range(group)
    z = [jnp.dot(rows(0, g).astype(BF16), wa_ref[...], preferred_element_type=F32) + ba_ref[...] for g in G]
    log_a = [-(jnp.maximum(-z[g], 0.0) + jnp.log1p(jnp.exp(-jnp.abs(z[g])))) / B_TAU for g in G]
    parts = [_split3(log_a[g]) for g in G]
    b = [jnp.dot(tri, parts[g][0], preferred_element_type=F32)
         + jnp.dot(tri, parts[g][1], preferred_element_type=F32)
         + jnp.dot(tri, parts[g][2], preferred_element_type=F32) for g in G]
    bl = [b[g][c - 1:c, :] for g in G]
    qd = [(rows(1, g) * (B_DK ** -0.5) * jnp.exp(b[g])).astype(BF16) for g in G]
    kd = [(rows(2, g) * jnp.exp(-b[g])).astype(BF16) for g in G]
    kl = [rows(2, g) * jnp.exp(bl[g] - b[g]) for g in G]
    kl = [kl[g].T if single else kl[g].astype(BF16) for g in G]
    ebl = [jnp.exp(bl[g]) for g in G]
    for h in range(B_HEADS):
        ks = slice(h * B_DK, (h + 1) * B_DK)
        vs = slice(h * B_DV, (h + 1) * B_DV)
        vh = [rows(3, g)[:, vs] for g in G]
        if single:
            st = [s0_ref[0, g, h] for g in G]
            o = [jnp.dot(qd[g][:, ks], st[g].astype(BF16), preferred_element_type=F32) for g in G]
        else:
            st = [st_ref[g, h] for g in G]
            o = [lax.dot_general(qd[g][:, ks], st[g].astype(BF16), nt, preferred_element_type=F32) for g in G]
        att = [lax.dot_general(qd[g][:, ks], kd[g][:, ks], nt, preferred_element_type=F32) for g in G]
        att = [jnp.where(causal, att[g], 0.0).astype(BF16) for g in G]
        o = [o[g] + jnp.dot(att[g], vh[g].astype(BF16), preferred_element_type=F32) for g in G]
        for g in G:
            if single:
                decay = jnp.broadcast_to(ebl[g][:, ks], (SUBLANES, B_DK)).T[:, 0:1]
                for l in range(layer):
                    sout_ref[l, g, h] = prev_ref[l, g, h]
                sout_ref[layer, g, h] = st[g] * decay + jnp.dot(kl[g][ks, :], vh[g],
                                                                  preferred_element_type=F32)
            else:
                st_ref[g, h] = st[g] * ebl[g][:, ks] + jnp.dot(vh[g].T.astype(BF16), kl[g][:, ks],
                                                               preferred_element_type=F32)
        o = [o[g] * lax.rsqrt(jnp.mean(o[g] * o[g], axis=1, keepdims=True) + RMS_EPS) * g_ref[h:h + 1, :]
             for g in G]
        for g in G:
            gate = rows(4, g)[:, vs]
            o_ref[g, :, vs] = (o[g] * (gate / (1.0 + jnp.exp(-gate)))).astype(o_ref.dtype)

    if not single:
        @pl.when(ci == n_chunks - 1)
        def _():
            for g in range(group):
                for l in range(layer):
                    sout_ref[l, g] = prev_ref[l, g]
                for h in range(B_HEADS):
                    sout_ref[layer, g, h] = st_ref[g, h].T


def _gla(rest, s0, layer, prev_states, wa, ba, gain, *, n_batch, seq, chunk, row_base, group):
    n_chunks = seq // chunk
    contiguous = n_chunks == 1
    base = row_base // chunk

    def in_specs_for(width, col):
        if contiguous:
            return [pl.BlockSpec((group * chunk, width), lambda b, ci: (base // group + b, col // width))]
        return [pl.BlockSpec((chunk, width),
                             lambda b, ci, g=g: (base + (b * group + g) * n_chunks + ci, col // width))
                for g in range(group)]

    row_specs = [spec for width, col in GLA_INPUTS for spec in in_specs_for(width, col)]
    s0_layer = layer if s0.shape[0] > 1 else 0
    state_block = (group, B_HEADS, B_DK, B_DV)
    prev_specs = [pl.BlockSpec((layer,) + state_block, lambda b, ci: (0, b, 0, 0, 0))] if layer else []
    prev_args = [prev_states] if layer else []
    return pl.pallas_call(
        functools.partial(_gla_kernel, n_chunks=n_chunks, group=group, chunk=chunk, layer=layer,
                          n_in=len(row_specs)),
        grid=(n_batch // group, n_chunks),
        in_specs=row_specs + [
            pl.BlockSpec((1,) + state_block, lambda b, ci: (s0_layer, b, 0, 0, 0)),
            pl.BlockSpec((LANES, B_QK), lambda b, ci: (0, 0)),
            pl.BlockSpec((1, B_QK), lambda b, ci: (0, 0)),
            pl.BlockSpec((B_HEADS, B_DV), lambda b, ci: (0, 0)),
        ] + prev_specs,
        out_specs=[
            pl.BlockSpec((group, chunk, B_V), lambda b, ci: (b, ci, 0)),
            pl.BlockSpec((layer + 1,) + state_block, lambda b, ci: (0, b, 0, 0, 0)),
        ],
        out_shape=[
            jax.ShapeDtypeStruct((n_batch, seq, B_V), BF16),
            jax.ShapeDtypeStruct((layer + 1, n_batch, B_HEADS, B_DK, B_DV), F32),
        ],
        scratch_shapes=[pltpu.VMEM((group, B_HEADS, B_DV, B_DK), F32)],
        compiler_params=_params(("parallel", "arbitrary")),
        name="gla",
    )(*([rest] * len(row_specs)), s0, wa, ba, gain, *prev_args)


HALO = 16
BB_POOL = 16


def _pool_kernel(u_ref, prev_ref, w_ref, scale_ref, o_ref, *, from_start, n_seq):
    tp = u_ref.shape[0] // n_seq
    if from_start:
        ti = pl.program_id(1)
        t0 = ti * tp
    diffs = [[] for _ in POOL_WINDOWS]
    for sq in range(n_seq):
        u = u_ref[sq * tp:(sq + 1) * tp, :]
        prev = prev_ref[sq * HALO:(sq + 1) * HALO, :]
        if from_start:
            prev = jnp.where(ti > 0, prev, 0.0)
        full = jnp.concatenate([prev, u], axis=0)
        for g, w in enumerate(POOL_WINDOWS):
            cs = slice(g * C_GROUP_W, (g + 1) * C_GROUP_W)
            acc = full[:, cs]
            span = 1
            while span < w:
                acc = acc + pltpu.roll(acc, span, axis=0)
                span *= 2
            wsum = acc[HALO:, :]
            if from_start:
                t = t0 + lax.broadcasted_iota(jnp.int32, (tp, C_GROUP_W), 0)
                cnt = jnp.minimum(t + 1, w).astype(F32)
            else:
                cnt = float(w)
            diffs[g].append(wsum / cnt - u[:, cs])
    for g in range(C_GROUPS):
        cs = slice(g * C_GROUP_W, (g + 1) * C_GROUP_W)
        d = jnp.concatenate(diffs[g], axis=0) if n_seq > 1 else diffs[g][0]
        y = jnp.dot(d.astype(BF16), w_ref[g], preferred_element_type=F32) * scale_ref[:, cs]
        o_ref[:, cs] = y.astype(o_ref.dtype)


def _pool_prompt(rest, w, scale):
    nt = SEQ // TP_POOL
    ucol = R_UC // C_WIDTH

    def halo(b, i):
        return (jnp.maximum((b * SEQ + i * TP_POOL) // HALO - 1, 0), ucol)

    return pl.pallas_call(
        functools.partial(_pool_kernel, from_start=True, n_seq=1),
        grid=(BATCH, nt),
        in_specs=[
            pl.BlockSpec((TP_POOL, C_WIDTH), lambda b, i: (b * nt + i, ucol)),
            pl.BlockSpec((HALO, C_WIDTH), halo),
            pl.BlockSpec((C_GROUPS, C_GROUP_W, C_GROUP_W), lambda b, i: (0, 0, 0)),
            pl.BlockSpec((1, C_WIDTH), lambda b, i: (0, 0)),
        ],
        out_specs=pl.BlockSpec((TP_POOL, C_WIDTH), lambda b, i: (b * nt + i, 0)),
        out_shape=jax.ShapeDtypeStruct((N_PROMPT, C_WIDTH), BF16),
        compiler_params=_params(("parallel", "arbitrary")),
        name="pool_prompt",
    )(rest, rest, w, scale)


def _pool_sample(rest, prev, w, scale):
    ucol = R_UC // C_WIDTH
    rows = BB_POOL * DEC_SEQ
    base = N_PROMPT // rows
    return pl.pallas_call(
        functools.partial(_pool_kernel, from_start=False, n_seq=BB_POOL),
        grid=(DEC_BATCH // BB_POOL,),
        in_specs=[
            pl.BlockSpec((rows, C_WIDTH), lambda b: (base + b, ucol)),
            pl.BlockSpec((BB_POOL * HALO, C_WIDTH), lambda b: (b, 0)),
            pl.BlockSpec((C_GROUPS, C_GROUP_W, C_GROUP_W), lambda b: (0, 0, 0)),
            pl.BlockSpec((1, C_WIDTH), lambda b: (0, 0)),
        ],
        out_specs=pl.BlockSpec((rows, C_WIDTH), lambda b: (b, 0)),
        out_shape=jax.ShapeDtypeStruct((N_SAMPLE, C_WIDTH), BF16),
        compiler_params=_params(("parallel",)),
        name="pool_sample",
    )(rest, prev, w, scale)


def _layer_norm(x, g, b):
    mu = jnp.mean(x, axis=1, keepdims=True)
    xc = x - mu
    var = jnp.mean(xc * xc, axis=1, keepdims=True)
    return xc * lax.rsqrt(var + LN_EPS) * g + b


def _merge_kernel(x_ref, gates_ref, oa_p_ref, oa_s_ref, ob_p_ref, ob_s_ref, oc_p_ref, oc_s_ref,
                  wa_ref, wb_ref, wc_ref, wo_ref, g_ref, b_ref, o_ref, oT_ref):
    in_sample = pl.program_id(0) >= N_PROMPT // TM_MERGE

    def gate(i):
        z = gates_ref[:, i * D_MODEL:(i + 1) * D_MODEL]
        return 1.0 / (1.0 + jnp.exp(-z))

    def branch(p_ref, s_ref, w_ref):
        o = jnp.where(in_sample, s_ref[...], p_ref[...])
        return jnp.dot(o, w_ref[...], preferred_element_type=F32)

    merged = (gate(0) * branch(oa_p_ref, oa_s_ref, wa_ref)
              + gate(1) * branch(ob_p_ref, ob_s_ref, wb_ref)
              + gate(2) * branch(oc_p_ref, oc_s_ref, wc_ref))
    mix = jnp.dot(merged.astype(BF16), wo_ref[...], preferred_element_type=F32)
    y = _layer_norm(DN_ALPHA * x_ref[...] + mix, g_ref[...], b_ref[...])
    o_ref[...] = y
    oT_ref[...] = y.T.astype(BF16)


def _merge(x, rest, branches, wa, wb, wc, wo, g, b):
    n_p = N_PROMPT // TM_MERGE
    row = pl.BlockSpec((TM_MERGE, D_MODEL), lambda i: (i, 0))
    row_p = pl.BlockSpec((TM_MERGE, D_MODEL), lambda i: (jnp.minimum(i, n_p - 1), 0))
    row_s = pl.BlockSpec((TM_MERGE, D_MODEL), lambda i: (jnp.maximum(i - n_p, 0), 0))
    wspec = pl.BlockSpec((D_MODEL, D_MODEL), lambda i: (0, 0))
    vec = pl.BlockSpec((1, D_MODEL), lambda i: (0, 0))
    (oa_p, oa_s), (ob_p, ob_s), (oc_p, oc_s) = branches
    return pl.pallas_call(
        _merge_kernel,
        grid=(N_TOK // TM_MERGE,),
        in_specs=[row, pl.BlockSpec((TM_MERGE, 3 * D_MODEL), lambda i: (i, R_GATES)),
                  row_p, row_s, row_p, row_s, row_p, row_s, wspec, wspec, wspec, wspec, vec, vec],
        out_specs=[row, pl.BlockSpec((D_MODEL, TM_MERGE), lambda i: (0, i))],
        out_shape=[jax.ShapeDtypeStruct((N_TOK, D_MODEL), F32),
                   jax.ShapeDtypeStruct((D_MODEL, N_TOK), BF16)],
        compiler_params=_params(("parallel",)),
        name="merge",
    )(x, rest, oa_p, oa_s, ob_p, ob_s, oc_p, oc_s, wa, wb, wc, wo, g, b)


def _merge_sort_pairs(n):
    size = 1
    while size < n:
        size *= 2
    pairs = []
    p = 1
    while p < size:
        k = p
        while k >= 1:
            for j in range(k % p, size - k, 2 * k):
                for i in range(min(k, size - j - k)):
                    if (i + j) // (2 * p) == (i + j + k) // (2 * p):
                        pairs.append((i + j, i + j + k))
            k //= 2
        p *= 2
    return [(a, b) for a, b in pairs if b < n]


def _top_desc(s, n):
    v = [s[SUBLANES * k:SUBLANES * (k + 1), :] for k in range(s.shape[0] // SUBLANES)]
    depth = len(v)
    for a, b in _merge_sort_pairs(depth):
        v[a], v[b] = jnp.maximum(v[a], v[b]), jnp.minimum(v[a], v[b])
    vals = []
    for r in range(n):
        m = jnp.max(v[0], axis=0, keepdims=True)
        vals.append(m)
        hit = v[0] == m
        needed = n - r - 1
        for k in range(min(depth - 1, needed)):
            v[k] = jnp.where(hit, v[k + 1], v[k])
        if needed >= depth:
            v[depth - 1] = jnp.where(hit, -jnp.inf, v[depth - 1])
    return vals


def _rank_of(s, vals):
    rank = jnp.full(s.shape, float(len(vals)), F32)
    for r, val in enumerate(vals):
        rank = jnp.where(s == val, float(r), rank)
    return rank


def _peer_topk_kernel(xT_ref, wq_ref, sk_ref, cnt_ref, e1_ref, rank_ref, e2_ref, s1_scr, sv_scr):
    tt = xT_ref.shape[1]
    xT = xT_ref[...]
    for hp in range(2 * PEER_HEADS):
        h, second = divmod(hp, 2)
        qT = jnp.dot(wq_ref[hp * PEER_DHALF:(hp + 1) * PEER_DHALF, :], xT, preferred_element_type=F32)
        s = jnp.dot(sk_ref[hp], qT.astype(BF16), preferred_element_type=F32)
        vals = _top_desc(s, PEER_TOPK)
        sv_scr[hp] = jnp.concatenate(vals, axis=0)
        if second:
            rank_ref[h] = _rank_of(s, vals).astype(BF16)
            e2_ref[h] = jnp.exp(s - vals[0]).astype(BF16)
        else:
            s1_scr[h] = s
    row8 = lax.broadcasted_iota(jnp.int32, (8, tt), 0)
    for h in range(PEER_HEADS):
        sv1 = sv_scr[2 * h]
        sv2 = sv_scr[2 * h + 1]
        pieces = [sv1[0:1, :] + sv2]
        for a in range(1, 8):
            nb = PEER_TOPK // (a + 1)
            pieces.append(jnp.where(row8 < nb, sv1[a:a + 1, :] + sv2[0:8, :], -jnp.inf))
        pieces.append(sv2[0:1, :] + sv1[8:16, :])
        cand = _top_desc(jnp.concatenate(pieces, axis=0), PEER_TOPK)
        top = cand[0]
        z = jnp.zeros_like(top)
        for r in range(PEER_TOPK):
            z = z + jnp.exp(cand[r] - top)
        kth = cand[PEER_TOPK - 1]
        counts = [jnp.sum(jnp.where(p >= kth, 1.0, 0.0), axis=0, keepdims=True) for p in pieces[:8]]
        tail = jnp.where(pieces[8] >= kth, 1.0, 0.0)
        counts += [tail[a:a + 1, :] for a in range(8)]
        s1 = s1_scr[h]
        cnt = jnp.zeros_like(s1)
        for a in range(PEER_TOPK):
            cnt = jnp.where(s1 == sv1[a:a + 1, :], counts[a], cnt)
        e1 = jnp.exp(s1 - (sv1[0:1, :] + jnp.log(z))) * SQRT_HALF
        for c in range(tt // LANES):
            cnt_ref[h, c] = cnt[:, c * LANES:(c + 1) * LANES]
            e1_ref[h, c] = e1[:, c * LANES:(c + 1) * LANES]


def _peer_topk(xT, wqT, sk):
    spec = pl.BlockSpec((PEER_HEADS, N_KEYS, TT_TOPK), lambda t: (0, 0, t))
    row_spec = pl.BlockSpec((PEER_HEADS, TT_TOPK // LANES, N_KEYS, LANES), lambda t: (0, t, 0, 0))
    wide = jax.ShapeDtypeStruct((PEER_HEADS, N_TOK // LANES, N_KEYS, LANES), F32)
    narrow = jax.ShapeDtypeStruct((PEER_HEADS, N_KEYS, N_TOK), BF16)
    return pl.pallas_call(
        _peer_topk_kernel,
        grid=(N_TOK // TT_TOPK,),
        in_specs=[
            pl.BlockSpec((D_MODEL, TT_TOPK), lambda t: (0, t)),
            pl.BlockSpec((PEER_HEADS * PEER_DKEY, D_MODEL), lambda t: (0, 0)),
            pl.BlockSpec((2 * PEER_HEADS, N_KEYS, PEER_DHALF), lambda t: (0, 0, 0)),
        ],
        out_specs=[row_spec, row_spec, spec, spec],
        out_shape=[wide, wide, narrow, narrow],
        scratch_shapes=[pltpu.VMEM((PEER_HEADS, N_KEYS, TT_TOPK), F32),
                        pltpu.VMEM((2 * PEER_HEADS, PEER_TOPK, TT_TOPK), F32)],
        compiler_params=_params(("parallel",)),
        name="peer_topk",
    )(xT, wqT, sk)


SQRT_HALF = 2.0 ** -0.5


def _gelu_unscaled(x):
    t = x * SQRT_HALF
    return t * (1.0 + lax.erf(t))


LC_PEER = 256


def _bf16_rows(ref, h, r, chunks, n_rows):
    x = jnp.concatenate([ref[h, c, pl.ds(r, 2 * SUBLANES, stride=0), :] for c in chunks], axis=1)
    packed = x.astype(BF16)
    return jnp.concatenate([packed] * (n_rows // packed.shape[0]), axis=0)


MM_PIECE = 512


def _peer_main_kernel(xT_ref, u_ref, vt_ref, cnt_ref, e1_ref, rank_ref, e2_ref, x_ref, g_ref, b_ref,
                      o_ref, ob_ref, yT_ref, s_scr, wh_scr):
    i = pl.program_id(1)
    tt = xT_ref.shape[1]

    @pl.when(i == 0)
    def _():
        yT_ref[...] = jnp.zeros_like(yT_ref)

    zero = jnp.zeros((N_KEYS, LC_PEER), BF16)
    per_piece = MM_PIECE // N_KEYS
    for ii in range(IB_PEER):
        rows = slice(ii * N_KEYS, (ii + 1) * N_KEYS)
        if ii % per_piece == 0:
            piece = slice(ii * N_KEYS, ii * N_KEYS + MM_PIECE)
            s_scr[piece, :] = jnp.dot(u_ref[0, piece, :], xT_ref[...], preferred_element_type=F32)
        for lc in range(tt // LC_PEER):
            cols = slice(lc * LC_PEER, (lc + 1) * LC_PEER)
            chunks = range(lc * LC_PEER // LANES, (lc + 1) * LC_PEER // LANES)
            w = zero
            for h in range(PEER_HEADS):
                cnt = _bf16_rows(cnt_ref, h, ii, chunks, N_KEYS)
                e1 = _bf16_rows(e1_ref, h, ii, chunks, N_KEYS)
                w = w + jnp.where(rank_ref[h, :, cols] < cnt, e2_ref[h, :, cols] * e1, zero)
            wh_scr[rows, cols] = w
    for ii in range(IB_PEER):
        rows = slice(ii * N_KEYS, (ii + 1) * N_KEYS)
        wh_scr[rows, :] = wh_scr[rows, :] * _gelu_unscaled(s_scr[rows, :]).astype(BF16)
    yT_ref[...] += jnp.dot(vt_ref[0], wh_scr[...], preferred_element_type=F32)

    @pl.when(i == pl.num_programs(1) - 1)
    def _():
        y = _layer_norm(DN_ALPHA * x_ref[...] + yT_ref[...].T, g_ref[...], b_ref[...])
        o_ref[...] = y
        ob_ref[...] = y.astype(BF16)


def _peer_main(xT, u, vt, layer, cnt, e1, rank, e2, x1, g, b):
    eb = IB_PEER * N_KEYS
    row_spec = pl.BlockSpec((PEER_HEADS, TT_PEER // LANES, IB_PEER, LANES), lambda t, i: (0, t, i, 0))
    tab_spec = pl.BlockSpec((PEER_HEADS, N_KEYS, TT_PEER), lambda t, i: (0, 0, t))
    tok_spec = pl.BlockSpec((TT_PEER, D_MODEL), lambda t, i: (t, 0))
    vec = pl.BlockSpec((1, D_MODEL), lambda t, i: (0, 0))
    return pl.pallas_call(
        _peer_main_kernel,
        grid=(N_TOK // TT_PEER, N_EXPERTS // eb),
        in_specs=[
            pl.BlockSpec((D_MODEL, TT_PEER), lambda t, i: (0, t)),
            pl.BlockSpec((1, eb, D_MODEL), lambda t, i: (layer, i, 0)),
            pl.BlockSpec((1, D_MODEL, eb), lambda t, i: (layer, 0, i)),
            row_spec, row_spec, tab_spec, tab_spec, tok_spec, vec, vec,
        ],
        out_specs=[tok_spec, tok_spec],
        out_shape=[jax.ShapeDtypeStruct((N_TOK, D_MODEL), F32),
                   jax.ShapeDtypeStruct((N_TOK, D_MODEL), BF16)],
        scratch_shapes=[pltpu.VMEM((D_MODEL, TT_PEER), F32),
                        pltpu.VMEM((eb, TT_PEER), F32), pltpu.VMEM((eb, TT_PEER), BF16)],
        compiler_params=_params(("parallel", "arbitrary")),
        name="peer_main",
    )(xT, u, vt, cnt, e1, rank, e2, x1, g, b)


def _rope_tables():
    half = ROT_DIM // 2
    pos = jnp.concatenate([jnp.arange(SEQ), PAST_LEN + (jnp.arange(TM_QK) % DEC_SEQ)])
    inv = ROPE_THETA ** (-jnp.arange(half, dtype=F32) / half)
    ang = pos.astype(F32)[:, None] * inv[None, :]
    cos, sin = jnp.cos(ang), jnp.sin(ang)
    n = pos.shape[0]
    one = jnp.ones((n, A_HEAD_DIM - ROT_DIM), F32)
    zero = jnp.zeros((n, A_HEAD_DIM - ROT_DIM), F32)
    zh = jnp.zeros((n, half), F32)
    reps = LANES // A_HEAD_DIM
    c = jnp.tile(jnp.concatenate([cos, cos, one], 1), (1, reps))
    s1 = jnp.tile(jnp.concatenate([-sin, zh, zero], 1), (1, reps))
    s2 = jnp.tile(jnp.concatenate([zh, sin, zero], 1), (1, reps))
    return c, s1, s2


def _split_cols(w):
    cuts = [int(c) for c in np.cumsum(SPLITS)[:-1]]
    return jnp.split(w, cuts, axis=-1)


def _layer(layer, x, xb, k_state, v_state, gla_states, gla_out, pool_state, rope, w_in, b_in, sinks,
           w_alpha, b_alpha, gla_g, w_pool, pool_scale, w_a, w_b, w_c, w_out, ln1_g, ln1_b,
           peer_query, peer_subkeys, peer_u, peer_vt, ln2_g, ln2_b):
    qa_w, ka_w, va_w, qb_w, kb_w, vb_w, lr_w, gb_w, uc_w, gates_w = _split_cols(w_in)
    qa_b, ka_b, va_b, qb_b, kb_b, vb_b, lr_b, gb_b, uc_b, gates_b = _split_cols(b_in[None, :])
    lr_pad = LANES - B_GATE_RANK
    w_qk = jnp.concatenate([qa_w, ka_w], 1).astype(BF16)
    b_qk = jnp.concatenate([qa_b, ka_b], 1)
    w_rest = jnp.concatenate([gates_w, vb_w, gb_w, uc_w, qb_w, kb_w, va_w,
                              jnp.pad(lr_w, ((0, 0), (0, lr_pad)))], 1).astype(BF16)
    b_rest = jnp.concatenate([gates_b, vb_b, gb_b, uc_b, qb_b, kb_b, va_b,
                              jnp.pad(lr_b, ((0, 0), (0, lr_pad)))], 1)

    qk = _proj_qk(xb, w_qk, b_qk, *rope)
    rest = _proj_rest(xb, w_rest, b_rest)

    ks = k_state.reshape(DEC_BATCH, WINDOW, A_KV)
    vs = v_state.reshape(DEC_BATCH, WINDOW, A_KV)
    oa = (_attn_prompt(qk, rest, sinks), _attn_sample(qk, rest, sinks, ks, vs))

    wa = jnp.pad(w_alpha, ((0, lr_pad), (0, 0))).astype(BF16)
    ba = b_alpha[None, :]
    gla_p, gla_s = gla_out
    ob_p, gla_p = _gla(rest, jnp.zeros((1, BATCH, B_HEADS, B_DK, B_DV), F32), layer, gla_p, wa, ba, gla_g,
                       n_batch=BATCH, seq=SEQ, chunk=GLA_CHUNK, row_base=0, group=GLA_GROUP_PROMPT)
    ob_s, gla_s = _gla(rest, gla_states, layer, gla_s, wa, ba, gla_g, n_batch=DEC_BATCH, seq=DEC_SEQ,
                       chunk=math.gcd(DEC_SEQ, GLA_CHUNK), row_base=N_PROMPT, group=GLA_GROUP_SAMPLE)
    ob = (ob_p.reshape(N_PROMPT, B_V), ob_s.reshape(N_SAMPLE, B_V))

    wp = w_pool.astype(BF16)
    ps = pool_scale[None, :]
    prev = jnp.pad(pool_state, ((0, 0), (HALO - POOL_STATE, 0), (0, 0))).reshape(DEC_BATCH * HALO, C_WIDTH)
    oc = (_pool_prompt(rest, wp, ps), _pool_sample(rest, prev, wp, ps))

    x1, x1T = _merge(x, rest, (oa, ob, oc), w_a.astype(BF16), w_b.astype(BF16), w_c.astype(BF16),
                     w_out.astype(BF16), ln1_g[None, :], ln1_b[None, :])

    wqT = peer_query.reshape(D_MODEL, PEER_HEADS * PEER_DKEY).T.astype(BF16)
    sk = peer_subkeys.reshape(2 * PEER_HEADS, N_KEYS, PEER_DHALF).astype(BF16)
    cnt, e1, rank, e2 = _peer_topk(x1T, wqT, sk)
    x2, x2b = _peer_main(x1T, peer_u, peer_vt, layer, cnt, e1, rank, e2, x1, ln2_g[None, :], ln2_b[None, :])

    def prompt_tail(t, col0, width, n):
        return jnp.stack([t[(b + 1) * SEQ - n:(b + 1) * SEQ, col0:col0 + width] for b in range(BATCH)])

    def sample_tail(state, t, col0, width, n):
        new = t[N_PROMPT:, col0:col0 + width].reshape(DEC_BATCH, DEC_SEQ, width)
        return jnp.concatenate([state, new], 1)[:, -n:]

    kv_shape = (-1, WINDOW, A_KV_HEADS, A_HEAD_DIM)
    states = (prompt_tail(qk, A_Q, A_KV, WINDOW).reshape(kv_shape),
              prompt_tail(rest, R_VA, A_KV, WINDOW).reshape(kv_shape),
              prompt_tail(rest, R_UC, C_WIDTH, POOL_STATE),
              sample_tail(ks, qk, A_Q, A_KV, WINDOW).reshape(kv_shape),
              sample_tail(vs, rest, R_VA, A_KV, WINDOW).reshape(kv_shape),
              sample_tail(pool_state, rest, R_UC, C_WIDTH, POOL_STATE))
    return x2, x2b, states, (gla_p, gla_s)


def kernel(x_prompt, x_sample, state_win_k, state_win_v, state_gla, state_pool, w_in, b_in, attn_sinks,
           w_alpha, b_alpha, gla_norm_g, w_pool, pool_scale, w_branch_a, w_branch_b, w_branch_c, w_out,
           ln1_g, ln1_b, peer_query, peer_subkeys, peer_u, peer_v, ln2_g, ln2_b):
    x = jnp.concatenate([x_prompt.reshape(N_PROMPT, D_MODEL), x_sample.reshape(N_SAMPLE, D_MODEL)], 0)
    xb = x.astype(BF16)
    rope = _rope_tables()
    peer_ub = peer_u.astype(BF16)
    peer_vtb = jnp.swapaxes(peer_v, 1, 2).astype(BF16)
    per_layer = []
    gla_out = (None, None)
    for l in range(DEPTH):
        x, xb, states, gla_out = _layer(
            l, x, xb, state_win_k[l], state_win_v[l], state_gla, gla_out, state_pool[l], rope,
            w_in[l], b_in[l], attn_sinks[l], w_alpha[l], b_alpha[l], gla_norm_g[l],
            w_pool[l], pool_scale[l], w_branch_a[l], w_branch_b[l], w_branch_c[l], w_out[l],
            ln1_g[l], ln1_b[l], peer_query[l], peer_subkeys[l], peer_ub, peer_vtb, ln2_g[l], ln2_b[l])
        per_layer.append(states)
    pk, pv, pp, sk, sv, sp = [jnp.stack([per_layer[l][i] for l in range(DEPTH)]) for i in range(6)]
    return (x[:N_PROMPT].reshape(BATCH, SEQ, D_MODEL), x[N_PROMPT:].reshape(DEC_BATCH, DEC_SEQ, D_MODEL),
            pk, pv, gla_out[0], pp, sk, sv, gla_out[1], sp)
```

```python
import functools
import math

import jax
import jax.numpy as jnp
import numpy as np
from jax import lax
from jax.experimental import pallas as pl
from jax.experimental.pallas import tpu as pltpu

F32 = jnp.float32
BF16 = jnp.bfloat16

D_MODEL = 1024
BATCH = 8
SEQ = 2048
DEPTH = 2
DEC_BATCH = 128
DEC_SEQ = 8
PAST_LEN = 16384

A_HEADS = 16
A_KV_HEADS = 2
A_HEAD_DIM = 64
A_GROUP = A_HEADS // A_KV_HEADS
WINDOW = 128
ROT_DIM = A_HEAD_DIM // 4
ROPE_THETA = 500000.0
NEG_INF = -1e30
B_HEADS = 4
B_DK = D_MODEL // 2 // B_HEADS
B_DV = D_MODEL // B_HEADS
B_GATE_RANK = 16
B_TAU = 16.0
GLA_CHUNK = 64
POOL_WINDOWS = (2, 4, 8, 16)
C_GROUPS = len(POOL_WINDOWS)
C_GROUP_W = D_MODEL // C_GROUPS
C_WIDTH = C_GROUPS * C_GROUP_W
POOL_STATE = max(POOL_WINDOWS) - 1
PEER_HEADS = 8
N_KEYS = 128
N_EXPERTS = N_KEYS * N_KEYS
PEER_TOPK = 16
PEER_DKEY = 256
PEER_DHALF = PEER_DKEY // 2
DN_ALPHA = (2 * DEPTH) ** 0.25
LN_EPS = 1e-5
RMS_EPS = 1e-6

A_Q = A_HEADS * A_HEAD_DIM
A_KV = A_KV_HEADS * A_HEAD_DIM
B_QK = B_HEADS * B_DK
B_V = B_HEADS * B_DV
SPLITS = (A_Q, A_KV, A_KV, B_QK, B_QK, B_V, B_GATE_RANK, B_V, C_WIDTH, 3 * D_MODEL)

LANES = 128
SUBLANES = 8
N_PROMPT = BATCH * SEQ
N_SAMPLE = DEC_BATCH * DEC_SEQ
N_TOK = N_PROMPT + N_SAMPLE

R_GATES = 0
R_VB = 3 * D_MODEL
R_GB = R_VB + B_V
R_UC = R_GB + B_V
R_QB = R_UC + C_WIDTH
R_KB = R_QB + B_QK
R_VA = R_KB + B_QK
R_LR = R_VA + A_KV
R_WIDTH = R_LR + LANES
QK_WIDTH = A_Q + A_KV

VMEM_LIMIT = 48 * 1024 * 1024

TM_QK = 512
TM_REST = 512
TN_REST = R_WIDTH // 2
TM_MERGE = 256
TP_POOL = 512
GLA_GROUP_PROMPT = 4
GLA_GROUP_SAMPLE = 4
TT_TOPK = 512
TT_PEER = 512
IB_PEER = 16


def _params(sem):
    return pltpu.CompilerParams(dimension_semantics=sem, vmem_limit_bytes=VMEM_LIMIT)


def _qk_kernel(x_ref, w_ref, b_ref, c_ref, s1_ref, s2_ref, o_ref):
    y = jnp.dot(x_ref[...], w_ref[...], preferred_element_type=F32) + b_ref[...]
    c = c_ref[...]
    s1 = s1_ref[...]
    s2 = s2_ref[...]
    for j in range(QK_WIDTH // LANES):
        yj = y[:, j * LANES:(j + 1) * LANES]
        up = pltpu.roll(yj, LANES - ROT_DIM // 2, axis=1)
        dn = pltpu.roll(yj, ROT_DIM // 2, axis=1)
        o_ref[:, j * LANES:(j + 1) * LANES] = yj * c + up * s1 + dn * s2


def _proj_qk(xb, w, b, rope_c, rope_s1, rope_s2):
    n_prompt_blocks = SEQ // TM_QK

    def tab_map(i):
        return (jnp.where(i < N_PROMPT // TM_QK, i % n_prompt_blocks, n_prompt_blocks), 0)

    tab_spec = pl.BlockSpec((TM_QK, LANES), tab_map)
    return pl.pallas_call(
        _qk_kernel,
        grid=(N_TOK // TM_QK,),
        in_specs=[
            pl.BlockSpec((TM_QK, D_MODEL), lambda i: (i, 0)),
            pl.BlockSpec((D_MODEL, QK_WIDTH), lambda i: (0, 0)),
            pl.BlockSpec((1, QK_WIDTH), lambda i: (0, 0)),
            tab_spec, tab_spec, tab_spec,
        ],
        out_specs=pl.BlockSpec((TM_QK, QK_WIDTH), lambda i: (i, 0)),
        out_shape=jax.ShapeDtypeStruct((N_TOK, QK_WIDTH), F32),
        compiler_params=_params(("parallel",)),
        name="proj_qk",
    )(xb, w, b, rope_c, rope_s1, rope_s2)


def _mm_bias_kernel(x_ref, w_ref, b_ref, o_ref):
    o_ref[...] = jnp.dot(x_ref[...], w_ref[...], preferred_element_type=F32) + b_ref[...]


def _proj_rest(xb, w, b):
    return pl.pallas_call(
        _mm_bias_kernel,
        grid=(R_WIDTH // TN_REST, N_TOK // TM_REST),
        in_specs=[
            pl.BlockSpec((TM_REST, D_MODEL), lambda j, i: (i, 0)),
            pl.BlockSpec((D_MODEL, TN_REST), lambda j, i: (0, j)),
            pl.BlockSpec((1, TN_REST), lambda j, i: (0, j)),
        ],
        out_specs=pl.BlockSpec((TM_REST, TN_REST), lambda j, i: (i, j)),
        out_shape=jax.ShapeDtypeStruct((N_TOK, R_WIDTH), F32),
        compiler_params=_params(("parallel", "arbitrary")),
        name="proj_rest",
    )(xb, w, b)


HEADS_PER_PASS = 16


def _attend(q, kk, vv, sink_ref, c_min, o_ref):
    tq = q.shape[0]
    r = lax.broadcasted_iota(jnp.int32, (tq, 2 * WINDOW), 0)
    c = lax.broadcasted_iota(jnp.int32, (tq, 2 * WINDOW), 1)
    ok = (c > r) & (c <= r + WINDOW) & (c >= c_min)
    qb = (q * (A_HEAD_DIM ** -0.5)).astype(BF16)
    nt = (((1,), (1,)), ((), ()))
    for h0 in range(0, A_HEADS, HEADS_PER_PASS):
        hs = range(h0, h0 + HEADS_PER_PASS)
        col = {h: slice(h * A_HEAD_DIM, (h + 1) * A_HEAD_DIM) for h in hs}
        kv = {h: slice((h // A_GROUP) * A_HEAD_DIM, (h // A_GROUP + 1) * A_HEAD_DIM) for h in hs}
        s = {h: lax.dot_general(qb[:, col[h]], kk[:, kv[h]], nt, preferred_element_type=F32) for h in hs}
        s = {h: jnp.where(ok, s[h], NEG_INF) for h in hs}
        m = {h: jnp.maximum(jnp.max(s[h], axis=1, keepdims=True), sink_ref[h]) for h in hs}
        p = {h: jnp.exp(s[h] - m[h]) for h in hs}
        denom = {h: jnp.sum(p[h], axis=1, keepdims=True) + jnp.exp(sink_ref[h] - m[h]) for h in hs}
        o = {h: jnp.dot(p[h].astype(BF16), vv[:, kv[h]], preferred_element_type=F32) / denom[h] for h in hs}
        for h in hs:
            o_ref[:, col[h]] = o[h].astype(o_ref.dtype)


def _attn_prompt_kernel(sink_ref, q_ref, kc_ref, kp_ref, vc_ref, vp_ref, o_ref):
    n = pl.program_id(1)
    kk = jnp.concatenate([kp_ref[...], kc_ref[...]], axis=0).astype(BF16)
    vv = jnp.concatenate([vp_ref[...], vc_ref[...]], axis=0).astype(BF16)
    _attend(q_ref[...], kk, vv, sink_ref, jnp.where(n > 0, 0, WINDOW), o_ref)


def _attn_prompt(qk, rest, sinks):
    nb = SEQ // WINDOW
    kcol = A_Q // A_KV
    vcol = R_VA // A_KV

    def cur(b, n):
        return b * nb + n

    def prev(b, n):
        return b * nb + jnp.maximum(n - 1, 0)

    return pl.pallas_call(
        _attn_prompt_kernel,
        grid=(BATCH, nb),
        in_specs=[
            pl.BlockSpec(memory_space=pltpu.SMEM),
            pl.BlockSpec((WINDOW, A_Q), lambda b, n: (cur(b, n), 0)),
            pl.BlockSpec((WINDOW, A_KV), lambda b, n: (cur(b, n), kcol)),
            pl.BlockSpec((WINDOW, A_KV), lambda b, n: (prev(b, n), kcol)),
            pl.BlockSpec((WINDOW, A_KV), lambda b, n: (cur(b, n), vcol)),
            pl.BlockSpec((WINDOW, A_KV), lambda b, n: (prev(b, n), vcol)),
        ],
        out_specs=pl.BlockSpec((WINDOW, A_Q), lambda b, n: (cur(b, n), 0)),
        out_shape=jax.ShapeDtypeStruct((N_PROMPT, A_Q), BF16),
        compiler_params=_params(("parallel", "arbitrary")),
        name="attn_prompt",
    )(sinks, qk, qk, qk, rest, rest)


BB_ATTN = 8


def _attn_sample_kernel(sink_ref, q_ref, kn_ref, vn_ref, ks_ref, vs_ref, o_ref):
    pad = jnp.zeros((WINDOW - DEC_SEQ, A_KV), F32)
    rows = A_GROUP * DEC_SEQ
    t = lax.broadcasted_iota(jnp.int32, (rows, 2 * WINDOW), 0) % DEC_SEQ
    c = lax.broadcasted_iota(jnp.int32, (rows, 2 * WINDOW), 1)
    ok = (c > t) & (c <= t + WINDOW)
    nt = (((1,), (1,)), ((), ()))

    def body(pair, carry):
        chains = [(e, g) for e in range(2) for g in range(A_KV_HEADS)]
        elem = {e: pair * 2 + e for e in range(2)}
        row0 = {e: pl.multiple_of(elem[e] * DEC_SEQ, DEC_SEQ) for e in range(2)}
        q = {e: q_ref[pl.ds(row0[e], DEC_SEQ), :] * (A_HEAD_DIM ** -0.5) for e in range(2)}
        kk = {e: jnp.concatenate([ks_ref[elem[e]], kn_ref[pl.ds(row0[e], DEC_SEQ), :], pad], axis=0).astype(BF16)
              for e in range(2)}
        vv = {e: jnp.concatenate([vs_ref[elem[e]], vn_ref[pl.ds(row0[e], DEC_SEQ), :], pad], axis=0).astype(BF16)
              for e in range(2)}
        heads = {g: range(g * A_GROUP, (g + 1) * A_GROUP) for g in range(A_KV_HEADS)}
        ds = {g: slice(g * A_HEAD_DIM, (g + 1) * A_HEAD_DIM) for g in range(A_KV_HEADS)}
        sink = {g: sink_ref[g * rows:(g + 1) * rows, 0:1] for g in range(A_KV_HEADS)}
        qg = {(bb, g): jnp.concatenate([q[bb][:, h * A_HEAD_DIM:(h + 1) * A_HEAD_DIM] for h in heads[g]],
                                       axis=0).astype(BF16) for bb, g in chains}
        s = {(bb, g): lax.dot_general(qg[bb, g], kk[bb][:, ds[g]], nt, preferred_element_type=F32)
             for bb, g in chains}
        s = {ch: jnp.where(ok, s[ch], NEG_INF) for ch in chains}
        m = {(bb, g): jnp.maximum(jnp.max(s[bb, g], axis=1, keepdims=True), sink[g]) for bb, g in chains}
        p = {ch: jnp.exp(s[ch] - m[ch]) for ch in chains}
        denom = {(bb, g): jnp.sum(p[bb, g], axis=1, keepdims=True) + jnp.exp(sink[g] - m[bb, g])
                 for bb, g in chains}
        o = {(bb, g): jnp.dot(p[bb, g].astype(BF16), vv[bb][:, ds[g]], preferred_element_type=F32) / denom[bb, g]
             for bb, g in chains}
        for bb, g in chains:
            for k, h in enumerate(heads[g]):
                o_ref[pl.ds(row0[bb], DEC_SEQ), h * A_HEAD_DIM:(h + 1) * A_HEAD_DIM] = (
                    o[bb, g][k * DEC_SEQ:(k + 1) * DEC_SEQ, :].astype(o_ref.dtype))
        return carry

    lax.fori_loop(0, BB_ATTN // 2, body, 0)


def _attn_sample(qk, rest, sinks, k_state, v_state):
    rows = BB_ATTN * DEC_SEQ
    base = N_PROMPT // rows
    kcol = A_Q // A_KV
    vcol = R_VA // A_KV
    sink_rows = jnp.broadcast_to(jnp.repeat(sinks, DEC_SEQ)[:, None], (A_HEADS * DEC_SEQ, LANES))
    return pl.pallas_call(
        _attn_sample_kernel,
        grid=(DEC_BATCH // BB_ATTN,),
        in_specs=[
            pl.BlockSpec((A_HEADS * DEC_SEQ, LANES), lambda i: (0, 0)),
            pl.BlockSpec((rows, A_Q), lambda i: (base + i, 0)),
            pl.BlockSpec((rows, A_KV), lambda i: (base + i, kcol)),
            pl.BlockSpec((rows, A_KV), lambda i: (base + i, vcol)),
            pl.BlockSpec((BB_ATTN, WINDOW, A_KV), lambda i: (i, 0, 0)),
            pl.BlockSpec((BB_ATTN, WINDOW, A_KV), lambda i: (i, 0, 0)),
        ],
        out_specs=pl.BlockSpec((rows, A_Q), lambda i: (i, 0)),
        out_shape=jax.ShapeDtypeStruct((N_SAMPLE, A_Q), BF16),
        compiler_params=_params(("parallel",)),
        name="attn_sample",
    )(sink_rows, qk, qk, rest, k_state, v_state)


def _split3(x):
    hi = x.astype(BF16)
    r1 = x - hi.astype(F32)
    mid = r1.astype(BF16)
    lo = (r1 - mid.astype(F32)).astype(BF16)
    return hi, mid, lo


GLA_INPUTS = ((LANES, R_LR), (B_QK, R_QB), (B_QK, R_KB), (B_V, R_VB), (B_V, R_GB))


def _gla_kernel(*refs, n_chunks, group, chunk, layer, n_in):
    ins, rest_refs = refs[:n_in], refs[n_in:]
    per = n_in // len(GLA_INPUTS)
    s0_ref, wa_ref, ba_ref, g_ref = rest_refs[:4]
    prev_ref = rest_refs[4] if layer else None
    o_ref, sout_ref, st_ref = rest_refs[-3:]
    ci = pl.program_id(1)
    c = chunk

    def rows(inp, g):
        if per == 1:
            return ins[inp][g * c:(g + 1) * c, :]
        return ins[inp * per + g][...]

    single = n_chunks == 1
    if not single:
        @pl.when(ci == 0)
        def _():
            for g in range(group):
                for h in range(B_HEADS):
                    st_ref[g, h] = s0_ref[0, g, h].T

    ri = lax.broadcasted_iota(jnp.int32, (c, c), 0)
    cj = lax.broadcasted_iota(jnp.int32, (c, c), 1)
    causal = cj <= ri
    tri = jnp.where(causal, 1.0, 0.0).astype(BF16)
    nt = (((1,), (1,)), ((), ()))
    G = range(group)
    z = [jnp.dot(rows(0, g).astype(BF16), wa_ref[...], preferred_element_type=F32) + ba_ref[...] for g in G]
    log_a = [-(jnp.maximum(-z[g], 0.0) + jnp.log1p(jnp.exp(-jnp.abs(z[g])))) / B_TAU for g in G]
    parts = [_split3(log_a[g]) for g in G]
    b = [jnp.dot(tri, parts[g][0], preferred_element_type=F32)
         + jnp.dot(tri, parts[g][1], preferred_element_type=F32)
         + jnp.dot(tri, parts[g][2], preferred_element_type=F32) for g in G]
    bl = [b[g][c - 1:c, :] for g in G]
    qd = [(rows(1, g) * (B_DK ** -0.5) * jnp.exp(b[g])).astype(BF16) for g in G]
    kd = [(rows(2, g) * jnp.exp(-b[g])).astype(BF16) for g in G]
    kl = [rows(2, g) * jnp.exp(bl[g] - b[g]) for g in G]
    kl = [kl[g].T if single else kl[g].astype(BF16) for g in G]
    ebl = [jnp.exp(bl[g]) for g in G]
    for h in range(B_HEADS):
        ks = slice(h * B_DK, (h + 1) * B_DK)
        vs = slice(h * B_DV, (h + 1) * B_DV)
        vh = [rows(3, g)[:, vs] for g in G]
        if single:
            st = [s0_ref[0, g, h] for g in G]
            o = [jnp.dot(qd[g][:, ks], st[g].astype(BF16), preferred_element_type=F32) for g in G]
        else:
            st = [st_ref[g, h] for g in G]
            o = [lax.dot_general(qd[g][:, ks], st[g].astype(BF16), nt, preferred_element_type=F32) for g in G]
        att = [lax.dot_general(qd[g][:, ks], kd[g][:, ks], nt, preferred_element_type=F32) for g in G]
        att = [jnp.where(causal, att[g], 0.0).astype(BF16) for g in G]
        o = [o[g] + jnp.dot(att[g], vh[g].astype(BF16), preferred_element_type=F32) for g in G]
        for g in G:
            if single:
                decay = jnp.broadcast_to(ebl[g][:, ks], (SUBLANES, B_DK)).T[:, 0:1]
                for l in range(layer):
                    sout_ref[l, g, h] = prev_ref[l, g, h]
                sout_ref[layer, g, h] = st[g] * decay + jnp.dot(kl[g][ks, :], vh[g],
                                                                  preferred_element_type=F32)
            else:
                st_ref[g, h] = st[g] * ebl[g][:, ks] + jnp.dot(vh[g].T.astype(BF16), kl[g][:, ks],
                                                               preferred_element_type=F32)
        o = [o[g] * lax.rsqrt(jnp.mean(o[g] * o[g], axis=1, keepdims=True) + RMS_EPS) * g_ref[h:h + 1, :]
             for g in G]
        for g in G:
            gate = rows(4, g)[:, vs]
            o_ref[g, :, vs] = (o[g] * (gate / (1.0 + jnp.exp(-gate)))).astype(o_ref.dtype)

    if not single:
        @pl.when(ci == n_chunks - 1)
        def _():
            for g in range(group):
                for l in range(layer):
                    sout_ref[l, g] = prev_ref[l, g]
                for h in range(B_HEADS):
                    sout_ref[layer, g, h] = st_ref[g, h].T


def _gla(rest, s0, layer, prev_states, wa, ba, gain, *, n_batch, seq, chunk, row_base, group):
    n_chunks = seq // chunk
    contiguous = n_chunks == 1
    base = row_base // chunk

    def in_specs_for(width, col):
        if contiguous:
            return [pl.BlockSpec((group * chunk, width), lambda b, ci: (base // group + b, col // width))]
        return [pl.BlockSpec((chunk, width),
                             lambda b, ci, g=g: (base + (b * group + g) * n_chunks + ci, col // width))
                for g in range(group)]

    row_specs = [spec for width, col in GLA_INPUTS for spec in in_specs_for(width, col)]
    s0_layer = layer if s0.shape[0] > 1 else 0
    state_block = (group, B_HEADS, B_DK, B_DV)
    prev_specs = [pl.BlockSpec((layer,) + state_block, lambda b, ci: (0, b, 0, 0, 0))] if layer else []
    prev_args = [prev_states] if layer else []
    return pl.pallas_call(
        functools.partial(_gla_kernel, n_chunks=n_chunks, group=group, chunk=chunk, layer=layer,
                          n_in=len(row_specs)),
        grid=(n_batch // group, n_chunks),
        in_specs=row_specs + [
            pl.BlockSpec((1,) + state_block, lambda b, ci: (s0_layer, b, 0, 0, 0)),
            pl.BlockSpec((LANES, B_QK), lambda b, ci: (0, 0)),
            pl.BlockSpec((1, B_QK), lambda b, ci: (0, 0)),
            pl.BlockSpec((B_HEADS, B_DV), lambda b, ci: (0, 0)),
        ] + prev_specs,
        out_specs=[
            pl.BlockSpec((group, chunk, B_V), lambda b, ci: (b, ci, 0)),
            pl.BlockSpec((layer + 1,) + state_block, lambda b, ci: (0, b, 0, 0, 0)),
        ],
        out_shape=[
            jax.ShapeDtypeStruct((n_batch, seq, B_V), BF16),
            jax.ShapeDtypeStruct((layer + 1, n_batch, B_HEADS, B_DK, B_DV), F32),
        ],
        scratch_shapes=[pltpu.VMEM((group, B_HEADS, B_DV, B_DK), F32)],
        compiler_params=_params(("parallel", "arbitrary")),
        name="gla",
    )(*([rest] * len(row_specs)), s0, wa, ba, gain, *prev_args)


HALO = 16
BB_POOL = 16


def _pool_kernel(u_ref, prev_ref, w_ref, scale_ref, o_ref, *, from_start, n_seq):
    tp = u_ref.shape[0] // n_seq
    if from_start:
        ti = pl.program_id(1)
        t0 = ti * tp
    diffs = [[] for _ in POOL_WINDOWS]
    for sq in range(n_seq):
        u = u_ref[sq * tp:(sq + 1) * tp, :]
        prev = prev_ref[sq * HALO:(sq + 1) * HALO, :]
        if from_start:
            prev = jnp.where(ti > 0, prev, 0.0)
        full = jnp.concatenate([prev, u], axis=0)
        for g, w in enumerate(POOL_WINDOWS):
            cs = slice(g * C_GROUP_W, (g + 1) * C_GROUP_W)
            acc = full[:, cs]
            span = 1
            while span < w:
                acc = acc + pltpu.roll(acc, span, axis=0)
                span *= 2
            wsum = acc[HALO:, :]
            if from_start:
                t = t0 + lax.broadcasted_iota(jnp.int32, (tp, C_GROUP_W), 0)
                cnt = jnp.minimum(t + 1, w).astype(F32)
            else:
                cnt = float(w)
            diffs[g].append(wsum / cnt - u[:, cs])
    for g in range(C_GROUPS):
        cs = slice(g * C_GROUP_W, (g + 1) * C_GROUP_W)
        d = jnp.concatenate(diffs[g], axis=0) if n_seq > 1 else diffs[g][0]
        y = jnp.dot(d.astype(BF16), w_ref[g], preferred_element_type=F32) * scale_ref[:, cs]
        o_ref[:, cs] = y.astype(o_ref.dtype)


def _pool_prompt(rest, w, scale):
    nt = SEQ // TP_POOL
    ucol = R_UC // C_WIDTH

    def halo(b, i):
        return (jnp.maximum((b * SEQ + i * TP_POOL) // HALO - 1, 0), ucol)

    return pl.pallas_call(
        functools.partial(_pool_kernel, from_start=True, n_seq=1),
        grid=(BATCH, nt),
        in_specs=[
            pl.BlockSpec((TP_POOL, C_WIDTH), lambda b, i: (b * nt + i, ucol)),
            pl.BlockSpec((HALO, C_WIDTH), halo),
            pl.BlockSpec((C_GROUPS, C_GROUP_W, C_GROUP_W), lambda b, i: (0, 0, 0)),
            pl.BlockSpec((1, C_WIDTH), lambda b, i: (0, 0)),
        ],
        out_specs=pl.BlockSpec((TP_POOL, C_WIDTH), lambda b, i: (b * nt + i, 0)),
        out_shape=jax.ShapeDtypeStruct((N_PROMPT, C_WIDTH), BF16),
        compiler_params=_params(("parallel", "arbitrary")),
        name="pool_prompt",
    )(rest, rest, w, scale)


def _pool_sample(rest, prev, w, scale):
    ucol = R_UC // C_WIDTH
    rows = BB_POOL * DEC_SEQ
    base = N_PROMPT // rows
    return pl.pallas_call(
        functools.partial(_pool_kernel, from_start=False, n_seq=BB_POOL),
        grid=(DEC_BATCH // BB_POOL,),
        in_specs=[
            pl.BlockSpec((rows, C_WIDTH), lambda b: (base + b, ucol)),
            pl.BlockSpec((BB_POOL * HALO, C_WIDTH), lambda b: (b, 0)),
            pl.BlockSpec((C_GROUPS, C_GROUP_W, C_GROUP_W), lambda b: (0, 0, 0)),
            pl.BlockSpec((1, C_WIDTH), lambda b: (0, 0)),
        ],
        out_specs=pl.BlockSpec((rows, C_WIDTH), lambda b: (b, 0)),
        out_shape=jax.ShapeDtypeStruct((N_SAMPLE, C_WIDTH), BF16),
        compiler_params=_params(("parallel",)),
        name="pool_sample",
    )(rest, prev, w, scale)


def _layer_norm(x, g, b):
    mu = jnp.mean(x, axis=1, keepdims=True)
    xc = x - mu
    var = jnp.mean(xc * xc, axis=1, keepdims=True)
    return xc * lax.rsqrt(var + LN_EPS) * g + b


def _merge_kernel(x_ref, gates_ref, oa_p_ref, oa_s_ref, ob_p_ref, ob_s_ref, oc_p_ref, oc_s_ref,
                  wa_ref, wb_ref, wc_ref, wo_ref, g_ref, b_ref, o_ref, oT_ref):
    in_sample = pl.program_id(0) >= N_PROMPT // TM_MERGE

    def gate(i):
        z = gates_ref[:, i * D_MODEL:(i + 1) * D_MODEL]
        return 1.0 / (1.0 + jnp.exp(-z))

    def branch(p_ref, s_ref, w_ref):
        o = jnp.where(in_sample, s_ref[...], p_ref[...])
        return jnp.dot(o, w_ref[...], preferred_element_type=F32)

    merged = (gate(0) * branch(oa_p_ref, oa_s_ref, wa_ref)
              + gate(1) * branch(ob_p_ref, ob_s_ref, wb_ref)
              + gate(2) * branch(oc_p_ref, oc_s_ref, wc_ref))
    mix = jnp.dot(merged.astype(BF16), wo_ref[...], preferred_element_type=F32)
    y = _layer_norm(DN_ALPHA * x_ref[...] + mix, g_ref[...], b_ref[...])
    o_ref[...] = y
    oT_ref[...] = y.T.astype(BF16)


def _merge(x, rest, branches, wa, wb, wc, wo, g, b):
    n_p = N_PROMPT // TM_MERGE
    row = pl.BlockSpec((TM_MERGE, D_MODEL), lambda i: (i, 0))
    row_p = pl.BlockSpec((TM_MERGE, D_MODEL), lambda i: (jnp.minimum(i, n_p - 1), 0))
    row_s = pl.BlockSpec((TM_MERGE, D_MODEL), lambda i: (jnp.maximum(i - n_p, 0), 0))
    wspec = pl.BlockSpec((D_MODEL, D_MODEL), lambda i: (0, 0))
    vec = pl.BlockSpec((1, D_MODEL), lambda i: (0, 0))
    (oa_p, oa_s), (ob_p, ob_s), (oc_p, oc_s) = branches
    return pl.pallas_call(
        _merge_kernel,
        grid=(N_TOK // TM_MERGE,),
        in_specs=[row, pl.BlockSpec((TM_MERGE, 3 * D_MODEL), lambda i: (i, R_GATES)),
                  row_p, row_s, row_p, row_s, row_p, row_s, wspec, wspec, wspec, wspec, vec, vec],
        out_specs=[row, pl.BlockSpec((D_MODEL, TM_MERGE), lambda i: (0, i))],
        out_shape=[jax.ShapeDtypeStruct((N_TOK, D_MODEL), F32),
                   jax.ShapeDtypeStruct((D_MODEL, N_TOK), BF16)],
        compiler_params=_params(("parallel",)),
        name="merge",
    )(x, rest, oa_p, oa_s, ob_p, ob_s, oc_p, oc_s, wa, wb, wc, wo, g, b)


def _merge_sort_pairs(n):
    size = 1
    while size < n:
        size *= 2
    pairs = []
    p = 1
    while p < size:
        k = p
        while k >= 1:
            for j in range(k % p, size - k, 2 * k):
                for i in range(min(k, size - j - k)):
                    if (i + j) // (2 * p) == (i + j + k) // (2 * p):
                        pairs.append((i + j, i + j + k))
            k //= 2
        p *= 2
    return [(a, b) for a, b in pairs if b < n]


def _top_desc(s, n):
    v = [s[SUBLANES * k:SUBLANES * (k + 1), :] for k in range(s.shape[0] // SUBLANES)]
    depth = len(v)
    for a, b in _merge_sort_pairs(depth):
        v[a], v[b] = jnp.maximum(v[a], v[b]), jnp.minimum(v[a], v[b])
    vals = []
    for r in range(n):
        m = jnp.max(v[0], axis=0, keepdims=True)
        vals.append(m)
        hit = v[0] == m
        needed = n - r - 1
        for k in range(min(depth - 1, needed)):
            v[k] = jnp.where(hit, v[k + 1], v[k])
        if needed >= depth:
            v[depth - 1] = jnp.where(hit, -jnp.inf, v[depth - 1])
    return vals


def _rank_of(s, vals):
    rank = jnp.full(s.shape, float(len(vals)), F32)
    for r, val in enumerate(vals):
        rank = jnp.where(s == val, float(r), rank)
    return rank


def _peer_topk_kernel(xT_ref, wq_ref, sk_ref, cnt_ref, e1_ref, rank_ref, e2_ref, s1_scr, sv_scr):
    tt = xT_ref.shape[1]
    xT = xT_ref[...]
    for hp in range(2 * PEER_HEADS):
        h, second = divmod(hp, 2)
        qT = jnp.dot(wq_ref[hp * PEER_DHALF:(hp + 1) * PEER_DHALF, :], xT, preferred_element_type=F32)
        s = jnp.dot(sk_ref[hp], qT.astype(BF16), preferred_element_type=F32)
        vals = _top_desc(s, PEER_TOPK)
        sv_scr[hp] = jnp.concatenate(vals, axis=0)
        if second:
            rank_ref[h] = _rank_of(s, vals).astype(BF16)
            e2_ref[h] = jnp.exp(s - vals[0]).astype(BF16)
        else:
            s1_scr[h] = s
    row8 = lax.broadcasted_iota(jnp.int32, (8, tt), 0)
    for h in range(PEER_HEADS):
        sv1 = sv_scr[2 * h]
        sv2 = sv_scr[2 * h + 1]
        pieces = [sv1[0:1, :] + sv2]
        for a in range(1, 8):
            nb = PEER_TOPK // (a + 1)
            pieces.append(jnp.where(row8 < nb, sv1[a:a + 1, :] + sv2[0:8, :], -jnp.inf))
        pieces.append(sv2[0:1, :] + sv1[8:16, :])
        cand = _top_desc(jnp.concatenate(pieces, axis=0), PEER_TOPK)
        top = cand[0]
        z = jnp.zeros_like(top)
        for r in range(PEER_TOPK):
            z = z + jnp.exp(cand[r] - top)
        kth = cand[PEER_TOPK - 1]
        counts = [jnp.sum(jnp.where(p >= kth, 1.0, 0.0), axis=0, keepdims=True) for p in pieces[:8]]
        tail = jnp.where(pieces[8] >= kth, 1.0, 0.0)
        counts += [tail[a:a + 1, :] for a in range(8)]
        s1 = s1_scr[h]
        cnt = jnp.zeros_like(s1)
        for a in range(PEER_TOPK):
            cnt = jnp.where(s1 == sv1[a:a + 1, :], counts[a], cnt)
        e1 = jnp.exp(s1 - (sv1[0:1, :] + jnp.log(z))) * SQRT_HALF
        for c in range(tt // LANES):
            cnt_ref[h, c] = cnt[:, c * LANES:(c + 1) * LANES]
            e1_ref[h, c] = e1[:, c * LANES:(c + 1) * LANES]


def _peer_topk(xT, wqT, sk):
    spec = pl.BlockSpec((PEER_HEADS, N_KEYS, TT_TOPK), lambda t: (0, 0, t))
    row_spec = pl.BlockSpec((PEER_HEADS, TT_TOPK // LANES, N_KEYS, LANES), lambda t: (0, t, 0, 0))
    wide = jax.ShapeDtypeStruct((PEER_HEADS, N_TOK // LANES, N_KEYS, LANES), F32)
    narrow = jax.ShapeDtypeStruct((PEER_HEADS, N_KEYS, N_TOK), BF16)
    return pl.pallas_call(
        _peer_topk_kernel,
        grid=(N_TOK // TT_TOPK,),
        in_specs=[
            pl.BlockSpec((D_MODEL, TT_TOPK), lambda t: (0, t)),
            pl.BlockSpec((PEER_HEADS * PEER_DKEY, D_MODEL), lambda t: (0, 0)),
            pl.BlockSpec((2 * PEER_HEADS, N_KEYS, PEER_DHALF), lambda t: (0, 0, 0)),
        ],
        out_specs=[row_spec, row_spec, spec, spec],
        out_shape=[wide, wide, narrow, narrow],
        scratch_shapes=[pltpu.VMEM((PEER_HEADS, N_KEYS, TT_TOPK), F32),
                        pltpu.VMEM((2 * PEER_HEADS, PEER_TOPK, TT_TOPK), F32)],
        compiler_params=_params(("parallel",)),
        name="peer_topk",
    )(xT, wqT, sk)


SQRT_HALF = 2.0 ** -0.5


def _gelu_unscaled(x):
    t = x * SQRT_HALF
    return t * (1.0 + lax.erf(t))


LC_PEER = 256


def _bf16_rows(ref, h, r, chunks, n_rows):
    x = jnp.concatenate([ref[h, c, pl.ds(r, 2 * SUBLANES, stride=0), :] for c in chunks], axis=1)
    packed = x.astype(BF16)
    return jnp.concatenate([packed] * (n_rows // packed.shape[0]), axis=0)


MM_PIECE = 512


def _peer_main_kernel(xT_ref, u_ref, vt_ref, cnt_ref, e1_ref, rank_ref, e2_ref, x_ref, g_ref, b_ref,
                      o_ref, ob_ref, yT_ref, s_scr, wh_scr):
    i = pl.program_id(1)
    tt = xT_ref.shape[1]

    @pl.when(i == 0)
    def _():
        yT_ref[...] = jnp.zeros_like(yT_ref)

    zero = jnp.zeros((N_KEYS, LC_PEER), BF16)
    per_piece = MM_PIECE // N_KEYS
    for ii in range(IB_PEER):
        rows = slice(ii * N_KEYS, (ii + 1) * N_KEYS)
        if ii % per_piece == 0:
            piece = slice(ii * N_KEYS, ii * N_KEYS + MM_PIECE)
            s_scr[piece, :] = jnp.dot(u_ref[0, piece, :], xT_ref[...], preferred_element_type=F32)
        for lc in range(tt // LC_PEER):
            cols = slice(lc * LC_PEER, (lc + 1) * LC_PEER)
            chunks = range(lc * LC_PEER // LANES, (lc + 1) * LC_PEER // LANES)
            w = zero
            for h in range(PEER_HEADS):
                cnt = _bf16_rows(cnt_ref, h, ii, chunks, N_KEYS)
                e1 = _bf16_rows(e1_ref, h, ii, chunks, N_KEYS)
                w = w + jnp.where(rank_ref[h, :, cols] < cnt, e2_ref[h, :, cols] * e1, zero)
            wh_scr[rows, cols] = w
    for ii in range(IB_PEER):
        rows = slice(ii * N_KEYS, (ii + 1) * N_KEYS)
        wh_scr[rows, :] = wh_scr[rows, :] * _gelu_unscaled(s_scr[rows, :]).astype(BF16)
    yT_ref[...] += jnp.dot(vt_ref[0], wh_scr[...], preferred_element_type=F32)

    @pl.when(i == pl.num_programs(1) - 1)
    def _():
        y = _layer_norm(DN_ALPHA * x_ref[...] + yT_ref[...].T, g_ref[...], b_ref[...])
        o_ref[...] = y
        ob_ref[...] = y.astype(BF16)


def _peer_main(xT, u, vt, layer, cnt, e1, rank, e2, x1, g, b):
    eb = IB_PEER * N_KEYS
    row_spec = pl.BlockSpec((PEER_HEADS, TT_PEER // LANES, IB_PEER, LANES), lambda t, i: (0, t, i, 0))
    tab_spec = pl.BlockSpec((PEER_HEADS, N_KEYS, TT_PEER), lambda t, i: (0, 0, t))
    tok_spec = pl.BlockSpec((TT_PEER, D_MODEL), lambda t, i: (t, 0))
    vec = pl.BlockSpec((1, D_MODEL), lambda t, i: (0, 0))
    return pl.pallas_call(
        _peer_main_kernel,
        grid=(N_TOK // TT_PEER, N_EXPERTS // eb),
        in_specs=[
            pl.BlockSpec((D_MODEL, TT_PEER), lambda t, i: (0, t)),
            pl.BlockSpec((1, eb, D_MODEL), lambda t, i: (layer, i, 0)),
            pl.BlockSpec((1, D_MODEL, eb), lambda t, i: (layer, 0, i)),
            row_spec, row_spec, tab_spec, tab_spec, tok_spec, vec, vec,
        ],
        out_specs=[tok_spec, tok_spec],
        out_shape=[jax.ShapeDtypeStruct((N_TOK, D_MODEL), F32),
                   jax.ShapeDtypeStruct((N_TOK, D_MODEL), BF16)],
        scratch_shapes=[pltpu.VMEM((D_MODEL, TT_PEER), F32),
                        pltpu.VMEM((eb, TT_PEER), F32), pltpu.VMEM((eb, TT_PEER), BF16)],
        compiler_params=_params(("parallel", "arbitrary")),
        name="peer_main",
    )(xT, u, vt, cnt, e1, rank, e2, x1, g, b)


def _rope_tables():
    half = ROT_DIM // 2
    pos = jnp.concatenate([jnp.arange(SEQ), PAST_LEN + (jnp.arange(TM_QK) % DEC_SEQ)])
    inv = ROPE_THETA ** (-jnp.arange(half, dtype=F32) / half)
    ang = pos.astype(F32)[:, None] * inv[None, :]
    cos, sin = jnp.cos(ang), jnp.sin(ang)
    n = pos.shape[0]
    one = jnp.ones((n, A_HEAD_DIM - ROT_DIM), F32)
    zero = jnp.zeros((n, A_HEAD_DIM - ROT_DIM), F32)
    zh = jnp.zeros((n, half), F32)
    reps = LANES // A_HEAD_DIM
    c = jnp.tile(jnp.concatenate([cos, cos, one], 1), (1, reps))
    s1 = jnp.tile(jnp.concatenate([-sin, zh, zero], 1), (1, reps))
    s2 = jnp.tile(jnp.concatenate([zh, sin, zero], 1), (1, reps))
    return c, s1, s2


def _split_cols(w):
    cuts = [int(c) for c in np.cumsum(SPLITS)[:-1]]
    return jnp.split(w, cuts, axis=-1)


def _layer(layer, x, xb, k_state, v_state, gla_states, gla_out, pool_state, rope, w_in, b_in, sinks,
           w_alpha, b_alpha, gla_g, w_pool, pool_scale, w_a, w_b, w_c, w_out, ln1_g, ln1_b,
           peer_query, peer_subkeys, peer_u, peer_vt, ln2_g, ln2_b):
    qa_w, ka_w, va_w, qb_w, kb_w, vb_w, lr_w, gb_w, uc_w, gates_w = _split_cols(w_in)
    qa_b, ka_b, va_b, qb_b, kb_b, vb_b, lr_b, gb_b, uc_b, gates_b = _split_cols(b_in[None, :])
    lr_pad = LANES - B_GATE_RANK
    w_qk = jnp.concatenate([qa_w, ka_w], 1).astype(BF16)
    b_qk = jnp.concatenate([qa_b, ka_b], 1)
    w_rest = jnp.concatenate([gates_w, vb_w, gb_w, uc_w, qb_w, kb_w, va_w,
                              jnp.pad(lr_w, ((0, 0), (0, lr_pad)))], 1).astype(BF16)
    b_rest = jnp.concatenate([gates_b, vb_b, gb_b, uc_b, qb_b, kb_b, va_b,
                              jnp.pad(lr_b, ((0, 0), (0, lr_pad)))], 1)

    qk = _proj_qk(xb, w_qk, b_qk, *rope)
    rest = _proj_rest(xb, w_rest, b_rest)

    ks = k_state.reshape(DEC_BATCH, WINDOW, A_KV)
    vs = v_state.reshape(DEC_BATCH, WINDOW, A_KV)
    oa = (_attn_prompt(qk, rest, sinks), _attn_sample(qk, rest, sinks, ks, vs))

    wa = jnp.pad(w_alpha, ((0, lr_pad), (0, 0))).astype(BF16)
    ba = b_alpha[None, :]
    gla_p, gla_s = gla_out
    ob_p, gla_p = _gla(rest, jnp.zeros((1, BATCH, B_HEADS, B_DK, B_DV), F32), layer, gla_p, wa, ba, gla_g,
                       n_batch=BATCH, seq=SEQ, chunk=GLA_CHUNK, row_base=0, group=GLA_GROUP_PROMPT)
    ob_s, gla_s = _gla(rest, gla_states, layer, gla_s, wa, ba, gla_g, n_batch=DEC_BATCH, seq=DEC_SEQ,
                       chunk=math.gcd(DEC_SEQ, GLA_CHUNK), row_base=N_PROMPT, group=GLA_GROUP_SAMPLE)
    ob = (ob_p.reshape(N_PROMPT, B_V), ob_s.reshape(N_SAMPLE, B_V))

    wp = w_pool.astype(BF16)
    ps = pool_scale[None, :]
    prev = jnp.pad(pool_state, ((0, 0), (HALO - POOL_STATE, 0), (0, 0))).reshape(DEC_BATCH * HALO, C_WIDTH)
    oc = (_pool_prompt(rest, wp, ps), _pool_sample(rest, prev, wp, ps))

    x1, x1T = _merge(x, rest, (oa, ob, oc), w_a.astype(BF16), w_b.astype(BF16), w_c.astype(BF16),
                     w_out.astype(BF16), ln1_g[None, :], ln1_b[None, :])

    wqT = peer_query.reshape(D_MODEL, PEER_HEADS * PEER_DKEY).T.astype(BF16)
    sk = peer_subkeys.reshape(2 * PEER_HEADS, N_KEYS, PEER_DHALF).astype(BF16)
    cnt, e1, rank, e2 = _peer_topk(x1T, wqT, sk)
    x2, x2b = _peer_main(x1T, peer_u, peer_vt, layer, cnt, e1, rank, e2, x1, ln2_g[None, :], ln2_b[None, :])

    def prompt_tail(t, col0, width, n):
        return jnp.stack([t[(b + 1) * SEQ - n:(b + 1) * SEQ, col0:col0 + width] for b in range(BATCH)])

    def sample_tail(state, t, col0, width, n):
        new = t[N_PROMPT:, col0:col0 + width].reshape(DEC_BATCH, DEC_SEQ, width)
        return jnp.concatenate([state, new], 1)[:, -n:]

    kv_shape = (-1, WINDOW, A_KV_HEADS, A_HEAD_DIM)
    states = (prompt_tail(qk, A_Q, A_KV, WINDOW).reshape(kv_shape),
              prompt_tail(rest, R_VA, A_KV, WINDOW).reshape(kv_shape),
              prompt_tail(rest, R_UC, C_WIDTH, POOL_STATE),
              sample_tail(ks, qk, A_Q, A_KV, WINDOW).reshape(kv_shape),
              sample_tail(vs, rest, R_VA, A_KV, WINDOW).reshape(kv_shape),
              sample_tail(pool_state, rest, R_UC, C_WIDTH, POOL_STATE))
    return x2, x2b, states, (gla_p, gla_s)


def kernel(x_prompt, x_sample, state_win_k, state_win_v, state_gla, state_pool, w_in, b_in, attn_sinks,
           w_alpha, b_alpha, gla_norm_g, w_pool, pool_scale, w_branch_a, w_branch_b, w_branch_c, w_out,
           ln1_g, ln1_b, peer_query, peer_subkeys, peer_u, peer_v, ln2_g, ln2_b):
    x = jnp.concatenate([x_prompt.reshape(N_PROMPT, D_MODEL), x_sample.reshape(N_SAMPLE, D_MODEL)], 0)
    xb = x.astype(BF16)
    rope = _rope_tables()
    peer_ub = peer_u.astype(BF16)
    peer_vtb = jnp.swapaxes(peer_v, 1, 2).astype(BF16)
    per_layer = []
    gla_out = (None, None)
    for l in range(DEPTH):
        x, xb, states, gla_out = _layer(
            l, x, xb, state_win_k[l], state_win_v[l], state_gla, gla_out, state_pool[l], rope,
            w_in[l], b_in[l], attn_sinks[l], w_alpha[l], b_alpha[l], gla_norm_g[l],
            w_pool[l], pool_scale[l], w_branch_a[l], w_branch_b[l], w_branch_c[l], w_out[l],
            ln1_g[l], ln1_b[l], peer_query[l], peer_subkeys[l], peer_ub, peer_vtb, ln2_g[l], ln2_b[l])
        per_layer.append(states)
    pk, pv, pp, sk, sv, sp = [jnp.stack([per_layer[l][i] for l in range(DEPTH)]) for i in range(6)]
    return (x[:N_PROMPT].reshape(BATCH, SEQ, D_MODEL), x[N_PROMPT:].reshape(DEC_BATCH, DEC_SEQ, D_MODEL),
            pk, pv, gla_out[0], pp, sk, sv, gla_out[1], sp)
```

```python
import functools
import math

import jax
import jax.numpy as jnp
import numpy as np
from jax import lax
from jax.experimental import pallas as pl
from jax.experimental.pallas import tpu as pltpu

F32 = jnp.float32
BF16 = jnp.bfloat16

D_MODEL = 1024
BATCH = 8
SEQ = 2048
DEPTH = 2
DEC_BATCH = 128
DEC_SEQ = 8
PAST_LEN = 16384

A_HEADS = 16
A_KV_HEADS = 2
A_HEAD_DIM = 64
A_GROUP = A_HEADS // A_KV_HEADS
WINDOW = 128
ROT_DIM = A_HEAD_DIM // 4
ROPE_THETA = 500000.0
NEG_INF = -1e30
B_HEADS = 4
B_DK = D_MODEL // 2 // B_HEADS
B_DV = D_MODEL // B_HEADS
B_GATE_RANK = 16
B_TAU = 16.0
GLA_CHUNK = 64
POOL_WINDOWS = (2, 4, 8, 16)
C_GROUPS = len(POOL_WINDOWS)
C_GROUP_W = D_MODEL // C_GROUPS
C_WIDTH = C_GROUPS * C_GROUP_W
POOL_STATE = max(POOL_WINDOWS) - 1
PEER_HEADS = 8
N_KEYS = 128
N_EXPERTS = N_KEYS * N_KEYS
PEER_TOPK = 16
PEER_DKEY = 256
PEER_DHALF = PEER_DKEY // 2
DN_ALPHA = (2 * DEPTH) ** 0.25
LN_EPS = 1e-5
RMS_EPS = 1e-6

A_Q = A_HEADS * A_HEAD_DIM
A_KV = A_KV_HEADS * A_HEAD_DIM
B_QK = B_HEADS * B_DK
B_V = B_HEADS * B_DV
SPLITS = (A_Q, A_KV, A_KV, B_QK, B_QK, B_V, B_GATE_RANK, B_V, C_WIDTH, 3 * D_MODEL)

LANES = 128
SUBLANES = 8
N_PROMPT = BATCH * SEQ
N_SAMPLE = DEC_BATCH * DEC_SEQ
N_TOK = N_PROMPT + N_SAMPLE

R_GATES = 0
R_VB = 3 * D_MODEL
R_GB = R_VB + B_V
R_UC = R_GB + B_V
R_QB = R_UC + C_WIDTH
R_KB = R_QB + B_QK
R_VA = R_KB + B_QK
R_LR = R_VA + A_KV
R_WIDTH = R_LR + LANES
QK_WIDTH = A_Q + A_KV

VMEM_LIMIT = 48 * 1024 * 1024

TM_QK = 512
TM_REST = 512
TN_REST = R_WIDTH // 2
TM_MERGE = 256
TP_POOL = 512
GLA_GROUP_PROMPT = 4
GLA_GROUP_SAMPLE = 4
TT_TOPK = 512
TT_PEER = 512
IB_PEER = 16


def _params(sem):
    return pltpu.CompilerParams(dimension_semantics=sem, vmem_limit_bytes=VMEM_LIMIT)


def _qk_kernel(x_ref, w_ref, b_ref, c_ref, s1_ref, s2_ref, o_ref):
    y = jnp.dot(x_ref[...], w_ref[...], preferred_element_type=F32) + b_ref[...]
    c = c_ref[...]
    s1 = s1_ref[...]
    s2 = s2_ref[...]
    for j in range(QK_WIDTH // LANES):
        yj = y[:, j * LANES:(j + 1) * LANES]
        up = pltpu.roll(yj, LANES - ROT_DIM // 2, axis=1)
        dn = pltpu.roll(yj, ROT_DIM // 2, axis=1)
        o_ref[:, j * LANES:(j + 1) * LANES] = yj * c + up * s1 + dn * s2


def _proj_qk(xb, w, b, rope_c, rope_s1, rope_s2):
    n_prompt_blocks = SEQ // TM_QK

    def tab_map(i):
        return (jnp.where(i < N_PROMPT // TM_QK, i % n_prompt_blocks, n_prompt_blocks), 0)

    tab_spec = pl.BlockSpec((TM_QK, LANES), tab_map)
    return pl.pallas_call(
        _qk_kernel,
        grid=(N_TOK // TM_QK,),
        in_specs=[
            pl.BlockSpec((TM_QK, D_MODEL), lambda i: (i, 0)),
            pl.BlockSpec((D_MODEL, QK_WIDTH), lambda i: (0, 0)),
            pl.BlockSpec((1, QK_WIDTH), lambda i: (0, 0)),
            tab_spec, tab_spec, tab_spec,
        ],
        out_specs=pl.BlockSpec((TM_QK, QK_WIDTH), lambda i: (i, 0)),
        out_shape=jax.ShapeDtypeStruct((N_TOK, QK_WIDTH), F32),
        compiler_params=_params(("parallel",)),
        name="proj_qk",
    )(xb, w, b, rope_c, rope_s1, rope_s2)


def _mm_bias_kernel(x_ref, w_ref, b_ref, o_ref):
    o_ref[...] = jnp.dot(x_ref[...], w_ref[...], preferred_element_type=F32) + b_ref[...]


def _proj_rest(xb, w, b):
    return pl.pallas_call(
        _mm_bias_kernel,
        grid=(R_WIDTH // TN_REST, N_TOK // TM_REST),
        in_specs=[
            pl.BlockSpec((TM_REST, D_MODEL), lambda j, i: (i, 0)),
            pl.BlockSpec((D_MODEL, TN_REST), lambda j, i: (0, j)),
            pl.BlockSpec((1, TN_REST), lambda j, i: (0, j)),
        ],
        out_specs=pl.BlockSpec((TM_REST, TN_REST), lambda j, i: (i, j)),
        out_shape=jax.ShapeDtypeStruct((N_TOK, R_WIDTH), F32),
        compiler_params=_params(("parallel", "arbitrary")),
        name="proj_rest",
    )(xb, w, b)


HEADS_PER_PASS = 16


def _attend(q, kk, vv, sink_ref, c_min, o_ref):
    tq = q.shape[0]
    r = lax.broadcasted_iota(jnp.int32, (tq, 2 * WINDOW), 0)
    c = lax.broadcasted_iota(jnp.int32, (tq, 2 * WINDOW), 1)
    ok = (c > r) & (c <= r + WINDOW) & (c >= c_min)
    qb = (q * (A_HEAD_DIM ** -0.5)).astype(BF16)
    nt = (((1,), (1,)), ((), ()))
    for h0 in range(0, A_HEADS, HEADS_PER_PASS):
        hs = range(h0, h0 + HEADS_PER_PASS)
        col = {h: slice(h * A_HEAD_DIM, (h + 1) * A_HEAD_DIM) for h in hs}
        kv = {h: slice((h // A_GROUP) * A_HEAD_DIM, (h // A_GROUP + 1) * A_HEAD_DIM) for h in hs}
        s = {h: lax.dot_general(qb[:, col[h]], kk[:, kv[h]], nt, preferred_element_type=F32) for h in hs}
        s = {h: jnp.where(ok, s[h], NEG_INF) for h in hs}
        m = {h: jnp.maximum(jnp.max(s[h], axis=1, keepdims=True), sink_ref[h]) for h in hs}
        p = {h: jnp.exp(s[h] - m[h]) for h in hs}
        denom = {h: jnp.sum(p[h], axis=1, keepdims=True) + jnp.exp(sink_ref[h] - m[h]) for h in hs}
        o = {h: jnp.dot(p[h].astype(BF16), vv[:, kv[h]], preferred_element_type=F32) / denom[h] for h in hs}
        for h in hs:
            o_ref[:, col[h]] = o[h].astype(o_ref.dtype)


def _attn_prompt_kernel(sink_ref, q_ref, kc_ref, kp_ref, vc_ref, vp_ref, o_ref):
    n = pl.program_id(1)
    kk = jnp.concatenate([kp_ref[...], kc_ref[...]], axis=0).astype(BF16)
    vv = jnp.concatenate([vp_ref[...], vc_ref[...]], axis=0).astype(BF16)
    _attend(q_ref[...], kk, vv, sink_ref, jnp.where(n > 0, 0, WINDOW), o_ref)


def _attn_prompt(qk, rest, sinks):
    nb = SEQ // WINDOW
    kcol = A_Q // A_KV
    vcol = R_VA // A_KV

    def cur(b, n):
        return b * nb + n

    def prev(b, n):
        return b * nb + jnp.maximum(n - 1, 0)

    return pl.pallas_call(
        _attn_prompt_kernel,
        grid=(BATCH, nb),
        in_specs=[
            pl.BlockSpec(memory_space=pltpu.SMEM),
            pl.BlockSpec((WINDOW, A_Q), lambda b, n: (cur(b, n), 0)),
            pl.BlockSpec((WINDOW, A_KV), lambda b, n: (cur(b, n), kcol)),
            pl.BlockSpec((WINDOW, A_KV), lambda b, n: (prev(b, n), kcol)),
            pl.BlockSpec((WINDOW, A_KV), lambda b, n: (cur(b, n), vcol)),
            pl.BlockSpec((WINDOW, A_KV), lambda b, n: (prev(b, n), vcol)),
        ],
        out_specs=pl.BlockSpec((WINDOW, A_Q), lambda b, n: (cur(b, n), 0)),
        out_shape=jax.ShapeDtypeStruct((N_PROMPT, A_Q), BF16),
        compiler_params=_params(("parallel", "arbitrary")),
        name="attn_prompt",
    )(sinks, qk, qk, qk, rest, rest)


BB_ATTN = 8


def _attn_sample_kernel(sink_ref, q_ref, kn_ref, vn_ref, ks_ref, vs_ref, o_ref):
    pad = jnp.zeros((WINDOW - DEC_SEQ, A_KV), F32)
    rows = A_GROUP * DEC_SEQ
    t = lax.broadcasted_iota(jnp.int32, (rows, 2 * WINDOW), 0) % DEC_SEQ
    c = lax.broadcasted_iota(jnp.int32, (rows, 2 * WINDOW), 1)
    ok = (c > t) & (c <= t + WINDOW)
    nt = (((1,), (1,)), ((), ()))

    def body(pair, carry):
        chains = [(e, g) for e in range(2) for g in range(A_KV_HEADS)]
        elem = {e: pair * 2 + e for e in range(2)}
        row0 = {e: pl.multiple_of(elem[e] * DEC_SEQ, DEC_SEQ) for e in range(2)}
        q = {e: q_ref[pl.ds(row0[e], DEC_SEQ), :] * (A_HEAD_DIM ** -0.5) for e in range(2)}
        kk = {e: jnp.concatenate([ks_ref[elem[e]], kn_ref[pl.ds(row0[e], DEC_SEQ), :], pad], axis=0).astype(BF16)
              for e in range(2)}
        vv = {e: jnp.concatenate([vs_ref[elem[e]], vn_ref[pl.ds(row0[e], DEC_SEQ), :], pad], axis=0).astype(BF16)
              for e in range(2)}
        heads = {g: range(g * A_GROUP, (g + 1) * A_GROUP) for g in range(A_KV_HEADS)}
        ds = {g: slice(g * A_HEAD_DIM, (g + 1) * A_HEAD_DIM) for g in range(A_KV_HEADS)}
        sink = {g: sink_ref[g * rows:(g + 1) * rows, 0:1] for g in range(A_KV_HEADS)}
        qg = {(bb, g): jnp.concatenate([q[bb][:, h * A_HEAD_DIM:(h + 1) * A_HEAD_DIM] for h in heads[g]],
                                       axis=0).astype(BF16) for bb, g in chains}
        s = {(bb, g): lax.dot_general(qg[bb, g], kk[bb][:, ds[g]], nt, preferred_element_type=F32)
             for bb, g in chains}
        s = {ch: jnp.where(ok, s[ch], NEG_INF) for ch in chains}
        m = {(bb, g): jnp.maximum(jnp.max(s[bb, g], axis=1, keepdims=True), sink[g]) for bb, g in chains}
        p = {ch: jnp.exp(s[ch] - m[ch]) for ch in chains}
        denom = {(bb, g): jnp.sum(p[bb, g], axis=1, keepdims=True) + jnp.exp(sink[g] - m[bb, g])
                 for bb, g in chains}
        o = {(bb, g): jnp.dot(p[bb, g].astype(BF16), vv[bb][:, ds[g]], preferred_element_type=F32) / denom[bb, g]
             for bb, g in chains}
        for bb, g in chains:
            for k, h in enumerate(heads[g]):
                o_ref[pl.ds(row0[bb], DEC_SEQ), h * A_HEAD_DIM:(h + 1) * A_HEAD_DIM] = (
                    o[bb, g][k * DEC_SEQ:(k + 1) * DEC_SEQ, :].astype(o_ref.dtype))
        return carry

    lax.fori_loop(0, BB_ATTN // 2, body, 0)


def _attn_sample(qk, rest, sinks, k_state, v_state):
    rows = BB_ATTN * DEC_SEQ
    base = N_PROMPT // rows
    kcol = A_Q // A_KV
    vcol = R_VA // A_KV
    sink_rows = jnp.broadcast_to(jnp.repeat(sinks, DEC_SEQ)[:, None], (A_HEADS * DEC_SEQ, LANES))
    return pl.pallas_call(
        _attn_sample_kernel,
        grid=(DEC_BATCH // BB_ATTN,),
        in_specs=[
            pl.BlockSpec((A_HEADS * DEC_SEQ, LANES), lambda i: (0, 0)),
            pl.BlockSpec((rows, A_Q), lambda i: (base + i, 0)),
            pl.BlockSpec((rows, A_KV), lambda i: (base + i, kcol)),
            pl.BlockSpec((rows, A_KV), lambda i: (base + i, vcol)),
            pl.BlockSpec((BB_ATTN, WINDOW, A_KV), lambda i: (i, 0, 0)),
            pl.BlockSpec((BB_ATTN, WINDOW, A_KV), lambda i: (i, 0, 0)),
        ],
        out_specs=pl.BlockSpec((rows, A_Q), lambda i: (i, 0)),
        out_shape=jax.ShapeDtypeStruct((N_SAMPLE, A_Q), BF16),
        compiler_params=_params(("parallel",)),
        name="attn_sample",
    )(sink_rows, qk, qk, rest, k_state, v_state)


def _split3(x):
    hi = x.astype(BF16)
    r1 = x - hi.astype(F32)
    mid = r1.astype(BF16)
    lo = (r1 - mid.astype(F32)).astype(BF16)
    return hi, mid, lo


GLA_INPUTS = ((LANES, R_LR), (B_QK, R_QB), (B_QK, R_KB), (B_V, R_VB), (B_V, R_GB))


def _gla_kernel(*refs, n_chunks, group, chunk, layer, n_in):
    ins, rest_refs = refs[:n_in], refs[n_in:]
    per = n_in // len(GLA_INPUTS)
    s0_ref, wa_ref, ba_ref, g_ref = rest_refs[:4]
    prev_ref = rest_refs[4] if layer else None
    o_ref, sout_ref, st_ref = rest_refs[-3:]
    ci = pl.program_id(1)
    c = chunk

    def rows(inp, g):
        if per == 1:
            return ins[inp][g * c:(g + 1) * c, :]
        return ins[inp * per + g][...]

    single = n_chunks == 1
    if not single:
        @pl.when(ci == 0)
        def _():
            for g in range(group):
                for h in range(B_HEADS):
                    st_ref[g, h] = s0_ref[0, g, h].T

    ri = lax.broadcasted_iota(jnp.int32, (c, c), 0)
    cj = lax.broadcasted_iota(jnp.int32, (c, c), 1)
    causal = cj <= ri
    tri = jnp.where(causal, 1.0, 0.0).astype(BF16)
    nt = (((1,), (1,)), ((), ()))
    G = range(group)
    z = [jnp.dot(rows(0, g).astype(BF16), wa_ref[...], preferred_element_type=F32) + ba_ref[...] for g in G]
    log_a = [-(jnp.maximum(-z[g], 0.0) + jnp.log1p(jnp.exp(-jnp.abs(z[g])))) / B_TAU for g in G]
    parts = [_split3(log_a[g]) for g in G]
    b = [jnp.dot(tri, parts[g][0], preferred_element_type=F32)
         + jnp.dot(tri, parts[g][1], preferred_element_type=F32)
         + jnp.dot(tri, parts[g][2], preferred_element_type=F32) for g in G]
    bl = [b[g][c - 1:c, :] for g in G]
    qd = [(rows(1, g) * (B_DK ** -0.5) * jnp.exp(b[g])).astype(BF16) for g in G]
    kd = [(rows(2, g) * jnp.exp(-b[g])).astype(BF16) for g in G]
    kl = [rows(2, g) * jnp.exp(bl[g] - b[g]) for g in G]
    kl = [kl[g].T if single else kl[g].astype(BF16) for g in G]
    ebl = [jnp.exp(bl[g]) for g in G]
    for h in range(B_HEADS):
        ks = slice(h * B_DK, (h + 1) * B_DK)
        vs = slice(h * B_DV, (h + 1) * B_DV)
        vh = [rows(3, g)[:, vs] for g in G]
        if single:
            st = [s0_ref[0, g, h] for g in G]
            o = [jnp.dot(qd[g][:, ks], st[g].astype(BF16), preferred_element_type=F32) for g in G]
        else:
            st = [st_ref[g, h] for g in G]
            o = [lax.dot_general(qd[g][:, ks], st[g].astype(BF16), nt, preferred_element_type=F32) for g in G]
        att = [lax.dot_general(qd[g][:, ks], kd[g][:, ks], nt, preferred_element_type=F32) for g in G]
        att = [jnp.where(causal, att[g], 0.0).astype(BF16) for g in G]
        o = [o[g] + jnp.dot(att[g], vh[g].astype(BF16), preferred_element_type=F32) for g in G]
        for g in G:
            if single:
                decay = jnp.broadcast_to(ebl[g][:, ks], (SUBLANES, B_DK)).T[:, 0:1]
                for l in range(layer):
                    sout_ref[l, g, h] = prev_ref[l, g, h]
                sout_ref[layer, g, h] = st[g] * decay + jnp.dot(kl[g][ks, :], vh[g],
                                                                  preferred_element_type=F32)
            else:
                st_ref[g, h] = st[g] * ebl[g][:, ks] + jnp.dot(vh[g].T.astype(BF16), kl[g][:, ks],
                                                               preferred_element_type=F32)
        o = [o[g] * lax.rsqrt(jnp.mean(o[g] * o[g], axis=1, keepdims=True) + RMS_EPS) * g_ref[h:h + 1, :]
             for g in G]
        for g in G:
            gate = rows(4, g)[:, vs]
            o_ref[g, :, vs] = (o[g] * (gate / (1.0 + jnp.exp(-gate)))).astype(o_ref.dtype)

    if not single:
        @pl.when(ci == n_chunks - 1)
        def _():
            for g in range(group):
                for l in range(layer):
                    sout_ref[l, g] = prev_ref[l, g]
                for h in range(B_HEADS):
                    sout_ref[layer, g, h] = st_ref[g, h].T


def _gla(rest, s0, layer, prev_states, wa, ba, gain, *, n_batch, seq, chunk, row_base, group):
    n_chunks = seq // chunk
    contiguous = n_chunks == 1
    base = row_base // chunk

    def in_specs_for(width, col):
        if contiguous:
            return [pl.BlockSpec((group * chunk, width), lambda b, ci: (base // group + b, col // width))]
        return [pl.BlockSpec((chunk, width),
                             lambda b, ci, g=g: (base + (b * group + g) * n_chunks + ci, col // width))
                for g in range(group)]

    row_specs = [spec for width, col in GLA_INPUTS for spec in in_specs_for(width, col)]
    s0_layer = layer if s0.shape[0] > 1 else 0
    state_block = (group, B_HEADS, B_DK, B_DV)
    prev_specs = [pl.BlockSpec((layer,) + state_block, lambda b, ci: (0, b, 0, 0, 0))] if layer else []
    prev_args = [prev_states] if layer else []
    return pl.pallas_call(
        functools.partial(_gla_kernel, n_chunks=n_chunks, group=group, chunk=chunk, layer=layer,
                          n_in=len(row_specs)),
        grid=(n_batch // group, n_chunks),
        in_specs=row_specs + [
            pl.BlockSpec((1,) + state_block, lambda b, ci: (s0_layer, b, 0, 0, 0)),
            pl.BlockSpec((LANES, B_QK), lambda b, ci: (0, 0)),
            pl.BlockSpec((1, B_QK), lambda b, ci: (0, 0)),
            pl.BlockSpec((B_HEADS, B_DV), lambda b, ci: (0, 0)),
        ] + prev_specs,
        out_specs=[
            pl.BlockSpec((group, chunk, B_V), lambda b, ci: (b, ci, 0)),
            pl.BlockSpec((layer + 1,) + state_block, lambda b, ci: (0, b, 0, 0, 0)),
        ],
        out_shape=[
            jax.ShapeDtypeStruct((n_batch, seq, B_V), BF16),
            jax.ShapeDtypeStruct((layer + 1, n_batch, B_HEADS, B_DK, B_DV), F32),
        ],
        scratch_shapes=[pltpu.VMEM((group, B_HEADS, B_DV, B_DK), F32)],
        compiler_params=_params(("parallel", "arbitrary")),
        name="gla",
    )(*([rest] * len(row_specs)), s0, wa, ba, gain, *prev_args)


HALO = 16
BB_POOL = 16


def _pool_kernel(u_ref, prev_ref, w_ref, scale_ref, o_ref, *, from_start, n_seq):
    tp = u_ref.shape[0] // n_seq
    if from_start:
        ti = pl.program_id(1)
        t0 = ti * tp
    diffs = [[] for _ in POOL_WINDOWS]
    for sq in range(n_seq):
        u = u_ref[sq * tp:(sq + 1) * tp, :]
        prev = prev_ref[sq * HALO:(sq + 1) * HALO, :]
        if from_start:
            prev = jnp.where(ti > 0, prev, 0.0)
        full = jnp.concatenate([prev, u], axis=0)
        for g, w in enumerate(POOL_WINDOWS):
            cs = slice(g * C_GROUP_W, (g + 1) * C_GROUP_W)
            acc = full[:, cs]
            span = 1
            while span < w:
                acc = acc + pltpu.roll(acc, span, axis=0)
                span *= 2
            wsum = acc[HALO:, :]
            if from_start:
                t = t0 + lax.broadcasted_iota(jnp.int32, (tp, C_GROUP_W), 0)
                cnt = jnp.minimum(t + 1, w).astype(F32)
            else:
                cnt = float(w)
            diffs[g].append(wsum / cnt - u[:, cs])
    for g in range(C_GROUPS):
        cs = slice(g * C_GROUP_W, (g + 1) * C_GROUP_W)
        d = jnp.concatenate(diffs[g], axis=0) if n_seq > 1 else diffs[g][0]
        y = jnp.dot(d.astype(BF16), w_ref[g], preferred_element_type=F32) * scale_ref[:, cs]
        o_ref[:, cs] = y.astype(o_ref.dtype)


def _pool_prompt(rest, w, scale):
    nt = SEQ // TP_POOL
    ucol = R_UC // C_WIDTH

    def halo(b, i):
        return (jnp.maximum((b * SEQ + i * TP_POOL) // HALO - 1, 0), ucol)

    return pl.pallas_call(
        functools.partial(_pool_kernel, from_start=True, n_seq=1),
        grid=(BATCH, nt),
        in_specs=[
            pl.BlockSpec((TP_POOL, C_WIDTH), lambda b, i: (b * nt + i, ucol)),
            pl.BlockSpec((HALO, C_WIDTH), halo),
            pl.BlockSpec((C_GROUPS, C_GROUP_W, C_GROUP_W), lambda b, i: (0, 0, 0)),
            pl.BlockSpec((1, C_WIDTH), lambda b, i: (0, 0)),
        ],
        out_specs=pl.BlockSpec((TP_POOL, C_WIDTH), lambda b, i: (b * nt + i, 0)),
        out_shape=jax.ShapeDtypeStruct((N_PROMPT, C_WIDTH), BF16),
        compiler_params=_params(("parallel", "arbitrary")),
        name="pool_prompt",
    )(rest, rest, w, scale)


def _pool_sample(rest, prev, w, scale):
    ucol = R_UC // C_WIDTH
    rows = BB_POOL * DEC_SEQ
    base = N_PROMPT // rows
    return pl.pallas_call(
        functools.partial(_pool_kernel, from_start=False, n_seq=BB_POOL),
        grid=(DEC_BATCH // BB_POOL,),
        in_specs=[
            pl.BlockSpec((rows, C_WIDTH), lambda b: (base + b, ucol)),
            pl.BlockSpec((BB_POOL * HALO, C_WIDTH), lambda b: (b, 0)),
            pl.BlockSpec((C_GROUPS, C_GROUP_W, C_GROUP_W), lambda b: (0, 0, 0)),
            pl.BlockSpec((1, C_WIDTH), lambda b: (0, 0)),
        ],
        out_specs=pl.BlockSpec((rows, C_WIDTH), lambda b: (b, 0)),
        out_shape=jax.ShapeDtypeStruct((N_SAMPLE, C_WIDTH), BF16),
        compiler_params=_params(("parallel",)),
        name="pool_sample",
    )(rest, prev, w, scale)


def _layer_norm(x, g, b):
    mu = jnp.mean(x, axis=1, keepdims=True)
    xc = x - mu
    var = jnp.mean(xc * xc, axis=1, keepdims=True)
    return xc * lax.rsqrt(var + LN_EPS) * g + b


def _merge_kernel(x_ref, gates_ref, oa_p_ref, oa_s_ref, ob_p_ref, ob_s_ref, oc_p_ref, oc_s_ref,
                  wa_ref, wb_ref, wc_ref, wo_ref, g_ref, b_ref, o_ref, oT_ref):
    in_sample = pl.program_id(0) >= N_PROMPT // TM_MERGE

    def gate(i):
        z = gates_ref[:, i * D_MODEL:(i + 1) * D_MODEL]
        return 1.0 / (1.0 + jnp.exp(-z))

    def branch(p_ref, s_ref, w_ref):
        o = jnp.where(in_sample, s_ref[...], p_ref[...])
        return jnp.dot(o, w_ref[...], preferred_element_type=F32)

    merged = (gate(0) * branch(oa_p_ref, oa_s_ref, wa_ref)
              + gate(1) * branch(ob_p_ref, ob_s_ref, wb_ref)
              + gate(2) * branch(oc_p_ref, oc_s_ref, wc_ref))
    mix = jnp.dot(merged.astype(BF16), wo_ref[...], preferred_element_type=F32)
    y = _layer_norm(DN_ALPHA * x_ref[...] + mix, g_ref[...], b_ref[...])
    o_ref[...] = y
    oT_ref[...] = y.T.astype(BF16)


def _merge(x, rest, branches, wa, wb, wc, wo, g, b):
    n_p = N_PROMPT // TM_MERGE
    row = pl.BlockSpec((TM_MERGE, D_MODEL), lambda i: (i, 0))
    row_p = pl.BlockSpec((TM_MERGE, D_MODEL), lambda i: (jnp.minimum(i, n_p - 1), 0))
    row_s = pl.BlockSpec((TM_MERGE, D_MODEL), lambda i: (jnp.maximum(i - n_p, 0), 0))
    wspec = pl.BlockSpec((D_MODEL, D_MODEL), lambda i: (0, 0))
    vec = pl.BlockSpec((1, D_MODEL), lambda i: (0, 0))
    (oa_p, oa_s), (ob_p, ob_s), (oc_p, oc_s) = branches
    return pl.pallas_call(
        _merge_kernel,
        grid=(N_TOK // TM_MERGE,),
        in_specs=[row, pl.BlockSpec((TM_MERGE, 3 * D_MODEL), lambda i: (i, R_GATES)),
                  row_p, row_s, row_p, row_s, row_p, row_s, wspec, wspec, wspec, wspec, vec, vec],
        out_specs=[row, pl.BlockSpec((D_MODEL, TM_MERGE), lambda i: (0, i))],
        out_shape=[jax.ShapeDtypeStruct((N_TOK, D_MODEL), F32),
                   jax.ShapeDtypeStruct((D_MODEL, N_TOK), BF16)],
        compiler_params=_params(("parallel",)),
        name="merge",
    )(x, rest, oa_p, oa_s, ob_p, ob_s, oc_p, oc_s, wa, wb, wc, wo, g, b)


def _merge_sort_pairs(n):
    size = 1
    while size < n:
        size *= 2
    pairs = []
    p = 1
    while p < size:
        k = p
        while k >= 1:
            for j in range(k % p, size - k, 2 * k):
                for i in range(min(k, size - j - k)):
                    if (i + j) // (2 * p) == (i + j + k) // (2 * p):
                        pairs.append((i + j, i + j + k))
            k //= 2
        p *= 2
    return [(a, b) for a, b in pairs if b < n]


def _top_desc(s, n):
    v = [s[SUBLANES * k:SUBLANES * (k + 1), :] for k in range(s.shape[0] // SUBLANES)]
    depth = len(v)
    for a, b in _merge_sort_pairs(depth):
        v[a], v[b] = jnp.maximum(v[a], v[b]), jnp.minimum(v[a], v[b])
    vals = []
    for r in range(n):
        m = jnp.max(v[0], axis=0, keepdims=True)
        vals.append(m)
        hit = v[0] == m
        needed = n - r - 1
        for k in range(min(depth - 1, needed)):
            v[k] = jnp.where(hit, v[k + 1], v[k])
        if needed >= depth:
            v[depth - 1] = jnp.where(hit, -jnp.inf, v[depth - 1])
    return vals


def _rank_of(s, vals):
    rank = jnp.full(s.shape, float(len(vals)), F32)
    for r, val in enumerate(vals):
        rank = jnp.where(s == val, float(r), rank)
    return rank


def _peer_topk_kernel(xT_ref, wq_ref, sk_ref, cnt_ref, e1_ref, rank_ref, e2_ref):
    tt = xT_ref.shape[1]
    row8 = lax.broadcasted_iota(jnp.int32, (SUBLANES, tt), 0)

    def scores(hp):
        qT = jnp.dot(wq_ref[hp * PEER_DHALF:(hp + 1) * PEER_DHALF, :], xT_ref[...], preferred_element_type=F32)
        return jnp.dot(sk_ref[hp], qT.astype(BF16), preferred_element_type=F32)

    for h in range(PEER_HEADS):
        s1, s2 = scores(2 * h), scores(2 * h + 1)
        vals1 = _top_desc(s1, PEER_TOPK)
        vals2 = _top_desc(s2, PEER_TOPK)
        rank_ref[h] = _rank_of(s2, vals2).astype(BF16)
        e2_ref[h] = jnp.exp(s2 - vals2[0]).astype(BF16)
        sv1 = jnp.concatenate(vals1, axis=0)
        sv2 = jnp.concatenate(vals2, axis=0)
        pieces = [sv1[0:1, :] + sv2]
        for a in range(1, 8):
            nb = PEER_TOPK // (a + 1)
            pieces.append(jnp.where(row8 < nb, sv1[a:a + 1, :] + sv2[0:8, :], -jnp.inf))
        pieces.append(sv2[0:1, :] + sv1[8:16, :])
        cand = _top_desc(jnp.concatenate(pieces, axis=0), PEER_TOPK)
        top = cand[0]
        z = jnp.zeros_like(top)
        for r in range(PEER_TOPK):
            z = z + jnp.exp(cand[r] - top)
        kth = cand[PEER_TOPK - 1]
        counts = [jnp.sum(jnp.where(p >= kth, 1.0, 0.0), axis=0, keepdims=True) for p in pieces[:8]]
        tail = jnp.where(pieces[8] >= kth, 1.0, 0.0)
        counts += [tail[a:a + 1, :] for a in range(8)]
        cnt = jnp.zeros_like(s1)
        for a in range(PEER_TOPK):
            cnt = jnp.where(s1 == sv1[a:a + 1, :], counts[a], cnt)
        e1 = jnp.exp(s1 - (sv1[0:1, :] + jnp.log(z))) * SQRT_HALF
        for c in range(tt // LANES):
            cnt_ref[h, c] = cnt[:, c * LANES:(c + 1) * LANES]
            e1_ref[h, c] = e1[:, c * LANES:(c + 1) * LANES]


def _peer_topk(xT, wqT, sk):
    spec = pl.BlockSpec((PEER_HEADS, N_KEYS, TT_TOPK), lambda t: (0, 0, t))
    row_spec = pl.BlockSpec((PEER_HEADS, TT_TOPK // LANES, N_KEYS, LANES), lambda t: (0, t, 0, 0))
    wide = jax.ShapeDtypeStruct((PEER_HEADS, N_TOK // LANES, N_KEYS, LANES), F32)
    narrow = jax.ShapeDtypeStruct((PEER_HEADS, N_KEYS, N_TOK), BF16)
    return pl.pallas_call(
        _peer_topk_kernel,
        grid=(N_TOK // TT_TOPK,),
        in_specs=[
            pl.BlockSpec((D_MODEL, TT_TOPK), lambda t: (0, t)),
            pl.BlockSpec((PEER_HEADS * PEER_DKEY, D_MODEL), lambda t: (0, 0)),
            pl.BlockSpec((2 * PEER_HEADS, N_KEYS, PEER_DHALF), lambda t: (0, 0, 0)),
        ],
        out_specs=[row_spec, row_spec, spec, spec],
        out_shape=[wide, wide, narrow, narrow],
        compiler_params=_params(("parallel",)),
        name="peer_topk",
    )(xT, wqT, sk)


SQRT_HALF = 2.0 ** -0.5


def _gelu_unscaled(x):
    t = x * SQRT_HALF
    return t * (1.0 + lax.erf(t))


LC_PEER = 256


def _bf16_rows(ref, h, r, chunks, n_rows):
    x = jnp.concatenate([ref[h, c, pl.ds(r, 2 * SUBLANES, stride=0), :] for c in chunks], axis=1)
    packed = x.astype(BF16)
    return jnp.concatenate([packed] * (n_rows // packed.shape[0]), axis=0)


MM_PIECE = 512


def _peer_main_kernel(xT_ref, u_ref, vt_ref, cnt_ref, e1_ref, rank_ref, e2_ref, x_ref, g_ref, b_ref,
                      o_ref, ob_ref, yT_ref, s_scr, wh_scr):
    i = pl.program_id(1)
    tt = xT_ref.shape[1]

    @pl.when(i == 0)
    def _():
        yT_ref[...] = jnp.zeros_like(yT_ref)

    zero = jnp.zeros((N_KEYS, LC_PEER), BF16)
    per_piece = MM_PIECE // N_KEYS
    for ii in range(IB_PEER):
        rows = slice(ii * N_KEYS, (ii + 1) * N_KEYS)
        if ii % per_piece == 0:
            piece = slice(ii * N_KEYS, ii * N_KEYS + MM_PIECE)
            s_scr[piece, :] = jnp.dot(u_ref[0, piece, :], xT_ref[...], preferred_element_type=F32)
        for lc in range(tt // LC_PEER):
            cols = slice(lc * LC_PEER, (lc + 1) * LC_PEER)
            chunks = range(lc * LC_PEER // LANES, (lc + 1) * LC_PEER // LANES)
            w = zero
            for h in range(PEER_HEADS):
                cnt = _bf16_rows(cnt_ref, h, ii, chunks, N_KEYS)
                e1 = _bf16_rows(e1_ref, h, ii, chunks, N_KEYS)
                w = w + jnp.where(rank_ref[h, :, cols] < cnt, e2_ref[h, :, cols] * e1, zero)
            wh_scr[rows, cols] = w
    for ii in range(IB_PEER):
        rows = slice(ii * N_KEYS, (ii + 1) * N_KEYS)
        wh_scr[rows, :] = wh_scr[rows, :] * _gelu_unscaled(s_scr[rows, :]).astype(BF16)
    yT_ref[...] += jnp.dot(vt_ref[0], wh_scr[...], preferred_element_type=F32)

    @pl.when(i == pl.num_programs(1) - 1)
    def _():
        y = _layer_norm(DN_ALPHA * x_ref[...] + yT_ref[...].T, g_ref[...], b_ref[...])
        o_ref[...] = y
        ob_ref[...] = y.astype(BF16)


def _peer_main(xT, u, vt, layer, cnt, e1, rank, e2, x1, g, b):
    eb = IB_PEER * N_KEYS
    row_spec = pl.BlockSpec((PEER_HEADS, TT_PEER // LANES, IB_PEER, LANES), lambda t, i: (0, t, i, 0))
    tab_spec = pl.BlockSpec((PEER_HEADS, N_KEYS, TT_PEER), lambda t, i: (0, 0, t))
    tok_spec = pl.BlockSpec((TT_PEER, D_MODEL), lambda t, i: (t, 0))
    vec = pl.BlockSpec((1, D_MODEL), lambda t, i: (0, 0))
    return pl.pallas_call(
        _peer_main_kernel,
        grid=(N_TOK // TT_PEER, N_EXPERTS // eb),
        in_specs=[
            pl.BlockSpec((D_MODEL, TT_PEER), lambda t, i: (0, t)),
            pl.BlockSpec((1, eb, D_MODEL), lambda t, i: (layer, i, 0)),
            pl.BlockSpec((1, D_MODEL, eb), lambda t, i: (layer, 0, i)),
            row_spec, row_spec, tab_spec, tab_spec, tok_spec, vec, vec,
        ],
        out_specs=[tok_spec, tok_spec],
        out_shape=[jax.ShapeDtypeStruct((N_TOK, D_MODEL), F32),
                   jax.ShapeDtypeStruct((N_TOK, D_MODEL), BF16)],
        scratch_shapes=[pltpu.VMEM((D_MODEL, TT_PEER), F32),
                        pltpu.VMEM((eb, TT_PEER), F32), pltpu.VMEM((eb, TT_PEER), BF16)],
        compiler_params=_params(("parallel", "arbitrary")),
        name="peer_main",
    )(xT, u, vt, cnt, e1, rank, e2, x1, g, b)


def _rope_tables():
    half = ROT_DIM // 2
    pos = jnp.concatenate([jnp.arange(SEQ), PAST_LEN + (jnp.arange(TM_QK) % DEC_SEQ)])
    inv = ROPE_THETA ** (-jnp.arange(half, dtype=F32) / half)
    ang = pos.astype(F32)[:, None] * inv[None, :]
    cos, sin = jnp.cos(ang), jnp.sin(ang)
    n = pos.shape[0]
    one = jnp.ones((n, A_HEAD_DIM - ROT_DIM), F32)
    zero = jnp.zeros((n, A_HEAD_DIM - ROT_DIM), F32)
    zh = jnp.zeros((n, half), F32)
    reps = LANES // A_HEAD_DIM
    c = jnp.tile(jnp.concatenate([cos, cos, one], 1), (1, reps))
    s1 = jnp.tile(jnp.concatenate([-sin, zh, zero], 1), (1, reps))
    s2 = jnp.tile(jnp.concatenate([zh, sin, zero], 1), (1, reps))
    return c, s1, s2


def _split_cols(w):
    cuts = [int(c) for c in np.cumsum(SPLITS)[:-1]]
    return jnp.split(w, cuts, axis=-1)


def _layer(layer, x, xb, k_state, v_state, gla_states, gla_out, pool_state, rope, w_in, b_in, sinks,
           w_alpha, b_alpha, gla_g, w_pool, pool_scale, w_a, w_b, w_c, w_out, ln1_g, ln1_b,
           peer_query, peer_subkeys, peer_u, peer_vt, ln2_g, ln2_b):
    qa_w, ka_w, va_w, qb_w, kb_w, vb_w, lr_w, gb_w, uc_w, gates_w = _split_cols(w_in)
    qa_b, ka_b, va_b, qb_b, kb_b, vb_b, lr_b, gb_b, uc_b, gates_b = _split_cols(b_in[None, :])
    lr_pad = LANES - B_GATE_RANK
    w_qk = jnp.concatenate([qa_w, ka_w], 1).astype(BF16)
    b_qk = jnp.concatenate([qa_b, ka_b], 1)
    w_rest = jnp.concatenate([gates_w, vb_w, gb_w, uc_w, qb_w, kb_w, va_w,
                              jnp.pad(lr_w, ((0, 0), (0, lr_pad)))], 1).astype(BF16)
    b_rest = jnp.concatenate([gates_b, vb_b, gb_b, uc_b, qb_b, kb_b, va_b,
                              jnp.pad(lr_b, ((0, 0), (0, lr_pad)))], 1)

    qk = _proj_qk(xb, w_qk, b_qk, *rope)
    rest = _proj_rest(xb, w_rest, b_rest)

    ks = k_state.reshape(DEC_BATCH, WINDOW, A_KV)
    vs = v_state.reshape(DEC_BATCH, WINDOW, A_KV)
    oa = (_attn_prompt(qk, rest, sinks), _attn_sample(qk, rest, sinks, ks, vs))

    wa = jnp.pad(w_alpha, ((0, lr_pad), (0, 0))).astype(BF16)
    ba = b_alpha[None, :]
    gla_p, gla_s = gla_out
    ob_p, gla_p = _gla(rest, jnp.zeros((1, BATCH, B_HEADS, B_DK, B_DV), F32), layer, gla_p, wa, ba, gla_g,
                       n_batch=BATCH, seq=SEQ, chunk=GLA_CHUNK, row_base=0, group=GLA_GROUP_PROMPT)
    ob_s, gla_s = _gla(rest, gla_states, layer, gla_s, wa, ba, gla_g, n_batch=DEC_BATCH, seq=DEC_SEQ,
                       chunk=math.gcd(DEC_SEQ, GLA_CHUNK), row_base=N_PROMPT, group=GLA_GROUP_SAMPLE)
    ob = (ob_p.reshape(N_PROMPT, B_V), ob_s.reshape(N_SAMPLE, B_V))

    wp = w_pool.astype(BF16)
    ps = pool_scale[None, :]
    prev = jnp.pad(pool_state, ((0, 0), (HALO - POOL_STATE, 0), (0, 0))).reshape(DEC_BATCH * HALO, C_WIDTH)
    oc = (_pool_prompt(rest, wp, ps), _pool_sample(rest, prev, wp, ps))

    x1, x1T = _merge(x, rest, (oa, ob, oc), w_a.astype(BF16), w_b.astype(BF16), w_c.astype(BF16),
                     w_out.astype(BF16), ln1_g[None, :], ln1_b[None, :])

    wqT = peer_query.reshape(D_MODEL, PEER_HEADS * PEER_DKEY).T.astype(BF16)
    sk = peer_subkeys.reshape(2 * PEER_HEADS, N_KEYS, PEER_DHALF).astype(BF16)
    cnt, e1, rank, e2 = _peer_topk(x1T, wqT, sk)
    x2, x2b = _peer_main(x1T, peer_u, peer_vt, layer, cnt, e1, rank, e2, x1, ln2_g[None, :], ln2_b[None, :])

    def prompt_tail(t, col0, width, n):
        return jnp.stack([t[(b + 1) * SEQ - n:(b + 1) * SEQ, col0:col0 + width] for b in range(BATCH)])

    def sample_tail(state, t, col0, width, n):
        new = t[N_PROMPT:, col0:col0 + width].reshape(DEC_BATCH, DEC_SEQ, width)
        return jnp.concatenate([state, new], 1)[:, -n:]

    kv_shape = (-1, WINDOW, A_KV_HEADS, A_HEAD_DIM)
    states = (prompt_tail(qk, A_Q, A_KV, WINDOW).reshape(kv_shape),
              prompt_tail(rest, R_VA, A_KV, WINDOW).reshape(kv_shape),
              prompt_tail(rest, R_UC, C_WIDTH, POOL_STATE),
              sample_tail(ks, qk, A_Q, A_KV, WINDOW).reshape(kv_shape),
              sample_tail(vs, rest, R_VA, A_KV, WINDOW).reshape(kv_shape),
              sample_tail(pool_state, rest, R_UC, C_WIDTH, POOL_STATE))
    return x2, x2b, states, (gla_p, gla_s)


def kernel(x_prompt, x_sample, state_win_k, state_win_v, state_gla, state_pool, w_in, b_in, attn_sinks,
           w_alpha, b_alpha, gla_norm_g, w_pool, pool_scale, w_branch_a, w_branch_b, w_branch_c, w_out,
           ln1_g, ln1_b, peer_query, peer_subkeys, peer_u, peer_v, ln2_g, ln2_b):
    x = jnp.concatenate([x_prompt.reshape(N_PROMPT, D_MODEL), x_sample.reshape(N_SAMPLE, D_MODEL)], 0)
    xb = x.astype(BF16)
    rope = _rope_tables()
    peer_ub = peer_u.astype(BF16)
    peer_vtb = jnp.swapaxes(peer_v, 1, 2).astype(BF16)
    per_layer = []
    gla_out = (None, None)
    for l in range(DEPTH):
        x, xb, states, gla_out = _layer(
            l, x, xb, state_win_k[l], state_win_v[l], state_gla, gla_out, state_pool[l], rope,
            w_in[l], b_in[l], attn_sinks[l], w_alpha[l], b_alpha[l], gla_norm_g[l],
            w_pool[l], pool_scale[l], w_branch_a[l], w_branch_b[l], w_branch_c[l], w_out[l],
            ln1_g[l], ln1_b[l], peer_query[l], peer_subkeys[l], peer_ub, peer_vtb, ln2_g[l], ln2_b[l])
        per_layer.append(states)
    pk, pv, pp, sk, sv, sp = [jnp.stack([per_layer[l][i] for l in range(DEPTH)]) for i in range(6)]
    return (x[:N_PROMPT].reshape(BATCH, SEQ, D_MODEL), x[N_PROMPT:].reshape(DEC_BATCH, DEC_SEQ, D_MODEL),
            pk, pv, gla_out[0], pp, sk, sv, gla_out[1], sp)
```

```python
import functools
import math

import jax
import jax.numpy as jnp
import numpy as np
from jax import lax
from jax.experimental import pallas as pl
from jax.experimental.pallas import tpu as pltpu

F32 = jnp.float32
BF16 = jnp.bfloat16

D_MODEL = 1024
BATCH = 8
SEQ = 2048
DEPTH = 2
DEC_BATCH = 128
DEC_SEQ = 8
PAST_LEN = 16384

A_HEADS = 16
A_KV_HEADS = 2
A_HEAD_DIM = 64
A_GROUP = A_HEADS // A_KV_HEADS
WINDOW = 128
ROT_DIM = A_HEAD_DIM // 4
ROPE_THETA = 500000.0
NEG_INF = -1e30
B_HEADS = 4
B_DK = D_MODEL // 2 // B_HEADS
B_DV = D_MODEL // B_HEADS
B_GATE_RANK = 16
B_TAU = 16.0
GLA_CHUNK = 64
POOL_WINDOWS = (2, 4, 8, 16)
C_GROUPS = len(POOL_WINDOWS)
C_GROUP_W = D_MODEL // C_GROUPS
C_WIDTH = C_GROUPS * C_GROUP_W
POOL_STATE = max(POOL_WINDOWS) - 1
PEER_HEADS = 8
N_KEYS = 128
N_EXPERTS = N_KEYS * N_KEYS
PEER_TOPK = 16
PEER_DKEY = 256
PEER_DHALF = PEER_DKEY // 2
DN_ALPHA = (2 * DEPTH) ** 0.25
LN_EPS = 1e-5
RMS_EPS = 1e-6

A_Q = A_HEADS * A_HEAD_DIM
A_KV = A_KV_HEADS * A_HEAD_DIM
B_QK = B_HEADS * B_DK
B_V = B_HEADS * B_DV
SPLITS = (A_Q, A_KV, A_KV, B_QK, B_QK, B_V, B_GATE_RANK, B_V, C_WIDTH, 3 * D_MODEL)

LANES = 128
SUBLANES = 8
N_PROMPT = BATCH * SEQ
N_SAMPLE = DEC_BATCH * DEC_SEQ
N_TOK = N_PROMPT + N_SAMPLE

R_GATES = 0
R_VB = 3 * D_MODEL
R_GB = R_VB + B_V
R_UC = R_GB + B_V
R_QB = R_UC + C_WIDTH
R_KB = R_QB + B_QK
R_VA = R_KB + B_QK
R_LR = R_VA + A_KV
R_WIDTH = R_LR + LANES
QK_WIDTH = A_Q + A_KV

VMEM_LIMIT = 48 * 1024 * 1024

TM_QK = 512
TM_REST = 512
TN_REST = R_WIDTH // 2
TM_MERGE = 256
TP_POOL = 512
GLA_GROUP_PROMPT = 4
GLA_GROUP_SAMPLE = 4
TT_TOPK = 512
TT_PEER = 512
IB_PEER = 16


def _params(sem):
    return pltpu.CompilerParams(dimension_semantics=sem, vmem_limit_bytes=VMEM_LIMIT)


def _qk_kernel(x_ref, w_ref, b_ref, c_ref, s1_ref, s2_ref, o_ref):
    y = jnp.dot(x_ref[...], w_ref[...], preferred_element_type=F32) + b_ref[...]
    c = c_ref[...]
    s1 = s1_ref[...]
    s2 = s2_ref[...]
    for j in range(QK_WIDTH // LANES):
        yj = y[:, j * LANES:(j + 1) * LANES]
        up = pltpu.roll(yj, LANES - ROT_DIM // 2, axis=1)
        dn = pltpu.roll(yj, ROT_DIM // 2, axis=1)
        o_ref[:, j * LANES:(j + 1) * LANES] = yj * c + up * s1 + dn * s2


def _proj_qk(xb, w, b, rope_c, rope_s1, rope_s2):
    n_prompt_blocks = SEQ // TM_QK

    def tab_map(i):
        return (jnp.where(i < N_PROMPT // TM_QK, i % n_prompt_blocks, n_prompt_blocks), 0)

    tab_spec = pl.BlockSpec((TM_QK, LANES), tab_map)
    return pl.pallas_call(
        _qk_kernel,
        grid=(N_TOK // TM_QK,),
        in_specs=[
            pl.BlockSpec((TM_QK, D_MODEL), lambda i: (i, 0)),
            pl.BlockSpec((D_MODEL, QK_WIDTH), lambda i: (0, 0)),
            pl.BlockSpec((1, QK_WIDTH), lambda i: (0, 0)),
            tab_spec, tab_spec, tab_spec,
        ],
        out_specs=pl.BlockSpec((TM_QK, QK_WIDTH), lambda i: (i, 0)),
        out_shape=jax.ShapeDtypeStruct((N_TOK, QK_WIDTH), F32),
        compiler_params=_params(("parallel",)),
        name="proj_qk",
    )(xb, w, b, rope_c, rope_s1, rope_s2)


def _mm_bias_kernel(x_ref, w_ref, b_ref, o_ref):
    o_ref[...] = jnp.dot(x_ref[...], w_ref[...], preferred_element_type=F32) + b_ref[...]


def _proj_rest(xb, w, b):
    return pl.pallas_call(
        _mm_bias_kernel,
        grid=(R_WIDTH // TN_REST, N_TOK // TM_REST),
        in_specs=[
            pl.BlockSpec((TM_REST, D_MODEL), lambda j, i: (i, 0)),
            pl.BlockSpec((D_MODEL, TN_REST), lambda j, i: (0, j)),
            pl.BlockSpec((1, TN_REST), lambda j, i: (0, j)),
        ],
        out_specs=pl.BlockSpec((TM_REST, TN_REST), lambda j, i: (i, j)),
        out_shape=jax.ShapeDtypeStruct((N_TOK, R_WIDTH), F32),
        compiler_params=_params(("parallel", "arbitrary")),
        name="proj_rest",
    )(xb, w, b)


HEADS_PER_PASS = 16


def _attend(q, kk, vv, sink_ref, c_min, o_ref):
    tq = q.shape[0]
    r = lax.broadcasted_iota(jnp.int32, (tq, 2 * WINDOW), 0)
    c = lax.broadcasted_iota(jnp.int32, (tq, 2 * WINDOW), 1)
    ok = (c > r) & (c <= r + WINDOW) & (c >= c_min)
    qb = (q * (A_HEAD_DIM ** -0.5)).astype(BF16)
    nt = (((1,), (1,)), ((), ()))
    for h0 in range(0, A_HEADS, HEADS_PER_PASS):
        hs = range(h0, h0 + HEADS_PER_PASS)
        col = {h: slice(h * A_HEAD_DIM, (h + 1) * A_HEAD_DIM) for h in hs}
        kv = {h: slice((h // A_GROUP) * A_HEAD_DIM, (h // A_GROUP + 1) * A_HEAD_DIM) for h in hs}
        s = {h: lax.dot_general(qb[:, col[h]], kk[:, kv[h]], nt, preferred_element_type=F32) for h in hs}
        s = {h: jnp.where(ok, s[h], NEG_INF) for h in hs}
        m = {h: jnp.maximum(jnp.max(s[h], axis=1, keepdims=True), sink_ref[h]) for h in hs}
        p = {h: jnp.exp(s[h] - m[h]) for h in hs}
        denom = {h: jnp.sum(p[h], axis=1, keepdims=True) + jnp.exp(sink_ref[h] - m[h]) for h in hs}
        o = {h: jnp.dot(p[h].astype(BF16), vv[:, kv[h]], preferred_element_type=F32) / denom[h] for h in hs}
        for h in hs:
            o_ref[:, col[h]] = o[h].astype(o_ref.dtype)


def _attn_prompt_kernel(sink_ref, q_ref, kc_ref, kp_ref, vc_ref, vp_ref, o_ref):
    n = pl.program_id(1)
    kk = jnp.concatenate([kp_ref[...], kc_ref[...]], axis=0).astype(BF16)
    vv = jnp.concatenate([vp_ref[...], vc_ref[...]], axis=0).astype(BF16)
    _attend(q_ref[...], kk, vv, sink_ref, jnp.where(n > 0, 0, WINDOW), o_ref)


def _attn_prompt(qk, rest, sinks):
    nb = SEQ // WINDOW
    kcol = A_Q // A_KV
    vcol = R_VA // A_KV

    def cur(b, n):
        return b * nb + n

    def prev(b, n):
        return b * nb + jnp.maximum(n - 1, 0)

    return pl.pallas_call(
        _attn_prompt_kernel,
        grid=(BATCH, nb),
        in_specs=[
            pl.BlockSpec(memory_space=pltpu.SMEM),
            pl.BlockSpec((WINDOW, A_Q), lambda b, n: (cur(b, n), 0)),
            pl.BlockSpec((WINDOW, A_KV), lambda b, n: (cur(b, n), kcol)),
            pl.BlockSpec((WINDOW, A_KV), lambda b, n: (prev(b, n), kcol)),
            pl.BlockSpec((WINDOW, A_KV), lambda b, n: (cur(b, n), vcol)),
            pl.BlockSpec((WINDOW, A_KV), lambda b, n: (prev(b, n), vcol)),
        ],
        out_specs=pl.BlockSpec((WINDOW, A_Q), lambda b, n: (cur(b, n), 0)),
        out_shape=jax.ShapeDtypeStruct((N_PROMPT, A_Q), BF16),
        compiler_params=_params(("parallel", "arbitrary")),
        name="attn_prompt",
    )(sinks, qk, qk, qk, rest, rest)


BB_ATTN = 8


def _attn_sample_kernel(sink_ref, q_ref, kn_ref, vn_ref, ks_ref, vs_ref, o_ref):
    pad = jnp.zeros((WINDOW - DEC_SEQ, A_KV), F32)
    rows = A_GROUP * DEC_SEQ
    t = lax.broadcasted_iota(jnp.int32, (rows, 2 * WINDOW), 0) % DEC_SEQ
    c = lax.broadcasted_iota(jnp.int32, (rows, 2 * WINDOW), 1)
    ok = (c > t) & (c <= t + WINDOW)
    nt = (((1,), (1,)), ((), ()))

    def body(pair, carry):
        chains = [(e, g) for e in range(2) for g in range(A_KV_HEADS)]
        elem = {e: pair * 2 + e for e in range(2)}
        row0 = {e: pl.multiple_of(elem[e] * DEC_SEQ, DEC_SEQ) for e in range(2)}
        q = {e: q_ref[pl.ds(row0[e], DEC_SEQ), :] * (A_HEAD_DIM ** -0.5) for e in range(2)}
        kk = {e: jnp.concatenate([ks_ref[elem[e]], kn_ref[pl.ds(row0[e], DEC_SEQ), :], pad], axis=0).astype(BF16)
              for e in range(2)}
        vv = {e: jnp.concatenate([vs_ref[elem[e]], vn_ref[pl.ds(row0[e], DEC_SEQ), :], pad], axis=0).astype(BF16)
              for e in range(2)}
        heads = {g: range(g * A_GROUP, (g + 1) * A_GROUP) for g in range(A_KV_HEADS)}
        ds = {g: slice(g * A_HEAD_DIM, (g + 1) * A_HEAD_DIM) for g in range(A_KV_HEADS)}
        sink = {g: sink_ref[g * rows:(g + 1) * rows, 0:1] for g in range(A_KV_HEADS)}
        qg = {(bb, g): jnp.concatenate([q[bb][:, h * A_HEAD_DIM:(h + 1) * A_HEAD_DIM] for h in heads[g]],
                                       axis=0).astype(BF16) for bb, g in chains}
        s = {(bb, g): lax.dot_general(qg[bb, g], kk[bb][:, ds[g]], nt, preferred_element_type=F32)
             for bb, g in chains}
        s = {ch: jnp.where(ok, s[ch], NEG_INF) for ch in chains}
        m = {(bb, g): jnp.maximum(jnp.max(s[bb, g], axis=1, keepdims=True), sink[g]) for bb, g in chains}
        p = {ch: jnp.exp(s[ch] - m[ch]) for ch in chains}
        denom = {(bb, g): jnp.sum(p[bb, g], axis=1, keepdims=True) + jnp.exp(sink[g] - m[bb, g])
                 for bb, g in chains}
        o = {(bb, g): jnp.dot(p[bb, g].astype(BF16), vv[bb][:, ds[g]], preferred_element_type=F32) / denom[bb, g]
             for bb, g in chains}
        for bb, g in chains:
            for k, h in enumerate(heads[g]):
                o_ref[pl.ds(row0[bb], DEC_SEQ), h * A_HEAD_DIM:(h + 1) * A_HEAD_DIM] = (
                    o[bb, g][k * DEC_SEQ:(k + 1) * DEC_SEQ, :].astype(o_ref.dtype))
        return carry

    lax.fori_loop(0, BB_ATTN // 2, body, 0)


def _attn_sample(qk, rest, sinks, k_state, v_state):
    rows = BB_ATTN * DEC_SEQ
    base = N_PROMPT // rows
    kcol = A_Q // A_KV
    vcol = R_VA // A_KV
    sink_rows = jnp.broadcast_to(jnp.repeat(sinks, DEC_SEQ)[:, None], (A_HEADS * DEC_SEQ, LANES))
    return pl.pallas_call(
        _attn_sample_kernel,
        grid=(DEC_BATCH // BB_ATTN,),
        in_specs=[
            pl.BlockSpec((A_HEADS * DEC_SEQ, LANES), lambda i: (0, 0)),
            pl.BlockSpec((rows, A_Q), lambda i: (base + i, 0)),
            pl.BlockSpec((rows, A_KV), lambda i: (base + i, kcol)),
            pl.BlockSpec((rows, A_KV), lambda i: (base + i, vcol)),
            pl.BlockSpec((BB_ATTN, WINDOW, A_KV), lambda i: (i, 0, 0)),
            pl.BlockSpec((BB_ATTN, WINDOW, A_KV), lambda i: (i, 0, 0)),
        ],
        out_specs=pl.BlockSpec((rows, A_Q), lambda i: (i, 0)),
        out_shape=jax.ShapeDtypeStruct((N_SAMPLE, A_Q), BF16),
        compiler_params=_params(("parallel",)),
        name="attn_sample",
    )(sink_rows, qk, qk, rest, k_state, v_state)


def _split3(x):
    hi = x.astype(BF16)
    r1 = x - hi.astype(F32)
    mid = r1.astype(BF16)
    lo = (r1 - mid.astype(F32)).astype(BF16)
    return hi, mid, lo


GLA_INPUTS = ((LANES, R_LR), (B_QK, R_QB), (B_QK, R_KB), (B_V, R_VB), (B_V, R_GB))


def _gla_kernel(*refs, n_chunks, group, chunk, layer, n_in):
    ins, rest_refs = refs[:n_in], refs[n_in:]
    per = n_in // len(GLA_INPUTS)
    s0_ref, wa_ref, ba_ref, g_ref = rest_refs[:4]
    prev_ref = rest_refs[4] if layer else None
    o_ref, sout_ref, st_ref = rest_refs[-3:]
    ci = pl.program_id(1)
    c = chunk

    def rows(inp, g):
        if per == 1:
            return ins[inp][g * c:(g + 1) * c, :]
        return ins[inp * per + g][...]

    single = n_chunks == 1
    if not single:
        @pl.when(ci == 0)
        def _():
            for g in range(group):
                for h in range(B_HEADS):
                    st_ref[g, h] = s0_ref[0, g, h].T

    ri = lax.broadcasted_iota(jnp.int32, (c, c), 0)
    cj = lax.broadcasted_iota(jnp.int32, (c, c), 1)
    causal = cj <= ri
    tri = jnp.where(causal, 1.0, 0.0).astype(BF16)
    nt = (((1,), (1,)), ((), ()))
    G = range(group)
    z = [jnp.dot(rows(0, g).astype(BF16), wa_ref[...], preferred_element_type=F32) + ba_ref[...] for g in G]
    log_a = [-(jnp.maximum(-z[g], 0.0) + jnp.log1p(jnp.exp(-jnp.abs(z[g])))) / B_TAU for g in G]
    parts = [_split3(log_a[g]) for g in G]
    b = [jnp.dot(tri, parts[g][0], preferred_element_type=F32)
         + jnp.dot(tri, parts[g][1], preferred_element_type=F32)
         + jnp.dot(tri, parts[g][2], preferred_element_type=F32) for g in G]
    bl = [b[g][c - 1:c, :] for g in G]
    qd = [(rows(1, g) * (B_DK ** -0.5) * jnp.exp(b[g])).astype(BF16) for g in G]
    kd = [(rows(2, g) * jnp.exp(-b[g])).astype(BF16) for g in G]
    kl = [rows(2, g) * jnp.exp(bl[g] - b[g]) for g in G]
    kl = [kl[g].T if single else kl[g].astype(BF16) for g in G]
    ebl = [jnp.exp(bl[g]) for g in G]
    for h in range(B_HEADS):
        ks = slice(h * B_DK, (h + 1) * B_DK)
        vs = slice(h * B_DV, (h + 1) * B_DV)
        vh = [rows(3, g)[:, vs] for g in G]
        if single:
            st = [s0_ref[0, g, h] for g in G]
            o = [jnp.dot(qd[g][:, ks], st[g].astype(BF16), preferred_element_type=F32) for g in G]
        else:
            st = [st_ref[g, h] for g in G]
            o = [lax.dot_general(qd[g][:, ks], st[g].astype(BF16), nt, preferred_element_type=F32) for g in G]
        att = [lax.dot_general(qd[g][:, ks], kd[g][:, ks], nt, preferred_element_type=F32) for g in G]
        att = [jnp.where(causal, att[g], 0.0).astype(BF16) for g in G]
        o = [o[g] + jnp.dot(att[g], vh[g].astype(BF16), preferred_element_type=F32) for g in G]
        for g in G:
            if single:
                decay = jnp.broadcast_to(ebl[g][:, ks], (SUBLANES, B_DK)).T[:, 0:1]
                for l in range(layer):
                    sout_ref[l, g, h] = prev_ref[l, g, h]
                sout_ref[layer, g, h] = st[g] * decay + jnp.dot(kl[g][ks, :], vh[g],
                                                                  preferred_element_type=F32)
            else:
                st_ref[g, h] = st[g] * ebl[g][:, ks] + jnp.dot(vh[g].T.astype(BF16), kl[g][:, ks],
                                                               preferred_element_type=F32)
        o = [o[g] * lax.rsqrt(jnp.mean(o[g] * o[g], axis=1, keepdims=True) + RMS_EPS) * g_ref[h:h + 1, :]
             for g in G]
        for g in G:
            gate = rows(4, g)[:, vs]
            o_ref[g, :, vs] = (o[g] * (gate / (1.0 + jnp.exp(-gate)))).astype(o_ref.dtype)

    if not single:
        @pl.when(ci == n_chunks - 1)
        def _():
            for g in range(group):
                for l in range(layer):
                    sout_ref[l, g] = prev_ref[l, g]
                for h in range(B_HEADS):
                    sout_ref[layer, g, h] = st_ref[g, h].T


def _gla(rest, s0, layer, prev_states, wa, ba, gain, *, n_batch, seq, chunk, row_base, group):
    n_chunks = seq // chunk
    contiguous = n_chunks == 1
    base = row_base // chunk

    def in_specs_for(width, col):
        if contiguous:
            return [pl.BlockSpec((group * chunk, width), lambda b, ci: (base // group + b, col // width))]
        return [pl.BlockSpec((chunk, width),
                             lambda b, ci, g=g: (base + (b * group + g) * n_chunks + ci, col // width))
                for g in range(group)]

    row_specs = [spec for width, col in GLA_INPUTS for spec in in_specs_for(width, col)]
    s0_layer = layer if s0.shape[0] > 1 else 0
    state_block = (group, B_HEADS, B_DK, B_DV)
    prev_specs = [pl.BlockSpec((layer,) + state_block, lambda b, ci: (0, b, 0, 0, 0))] if layer else []
    prev_args = [prev_states] if layer else []
    return pl.pallas_call(
        functools.partial(_gla_kernel, n_chunks=n_chunks, group=group, chunk=chunk, layer=layer,
                          n_in=len(row_specs)),
        grid=(n_batch // group, n_chunks),
        in_specs=row_specs + [
            pl.BlockSpec((1,) + state_block, lambda b, ci: (s0_layer, b, 0, 0, 0)),
            pl.BlockSpec((LANES, B_QK), lambda b, ci: (0, 0)),
            pl.BlockSpec((1, B_QK), lambda b, ci: (0, 0)),
            pl.BlockSpec((B_HEADS, B_DV), lambda b, ci: (0, 0)),
        ] + prev_specs,
        out_specs=[
            pl.BlockSpec((group, chunk, B_V), lambda b, ci: (b, ci, 0)),
            pl.BlockSpec((layer + 1,) + state_block, lambda b, ci: (0, b, 0, 0, 0)),
        ],
        out_shape=[
            jax.ShapeDtypeStruct((n_batch, seq, B_V), BF16),
            jax.ShapeDtypeStruct((layer + 1, n_batch, B_HEADS, B_DK, B_DV), F32),
        ],
        scratch_shapes=[pltpu.VMEM((group, B_HEADS, B_DV, B_DK), F32)],
        compiler_params=_params(("parallel", "arbitrary")),
        name="gla",
    )(*([rest] * len(row_specs)), s0, wa, ba, gain, *prev_args)


HALO = 16
BB_POOL = 16


def _pool_kernel(u_ref, prev_ref, w_ref, scale_ref, o_ref, *, from_start, n_seq):
    tp = u_ref.shape[0] // n_seq
    if from_start:
        ti = pl.program_id(1)
        t0 = ti * tp
    diffs = [[] for _ in POOL_WINDOWS]
    for sq in range(n_seq):
        u = u_ref[sq * tp:(sq + 1) * tp, :]
        prev = prev_ref[sq * HALO:(sq + 1) * HALO, :]
        if from_start:
            prev = jnp.where(ti > 0, prev, 0.0)
        full = jnp.concatenate([prev, u], axis=0)
        for g, w in enumerate(POOL_WINDOWS):
            cs = slice(g * C_GROUP_W, (g + 1) * C_GROUP_W)
            acc = full[:, cs]
            span = 1
            while span < w:
                acc = acc + pltpu.roll(acc, span, axis=0)
                span *= 2
            wsum = acc[HALO:, :]
            if from_start:
                t = t0 + lax.broadcasted_iota(jnp.int32, (tp, C_GROUP_W), 0)
                cnt = jnp.minimum(t + 1, w).astype(F32)
            else:
                cnt = float(w)
            diffs[g].append(wsum / cnt - u[:, cs])
    for g in range(C_GROUPS):
        cs = slice(g * C_GROUP_W, (g + 1) * C_GROUP_W)
        d = jnp.concatenate(diffs[g], axis=0) if n_seq > 1 else diffs[g][0]
        y = jnp.dot(d.astype(BF16), w_ref[g], preferred_element_type=F32) * scale_ref[:, cs]
        o_ref[:, cs] = y.astype(o_ref.dtype)


def _pool_prompt(rest, w, scale):
    nt = SEQ // TP_POOL
    ucol = R_UC // C_WIDTH

    def halo(b, i):
        return (jnp.maximum((b * SEQ + i * TP_POOL) // HALO - 1, 0), ucol)

    return pl.pallas_call(
        functools.partial(_pool_kernel, from_start=True, n_seq=1),
        grid=(BATCH, nt),
        in_specs=[
            pl.BlockSpec((TP_POOL, C_WIDTH), lambda b, i: (b * nt + i, ucol)),
            pl.BlockSpec((HALO, C_WIDTH), halo),
            pl.BlockSpec((C_GROUPS, C_GROUP_W, C_GROUP_W), lambda b, i: (0, 0, 0)),
            pl.BlockSpec((1, C_WIDTH), lambda b, i: (0, 0)),
        ],
        out_specs=pl.BlockSpec((TP_POOL, C_WIDTH), lambda b, i: (b * nt + i, 0)),
        out_shape=jax.ShapeDtypeStruct((N_PROMPT, C_WIDTH), BF16),
        compiler_params=_params(("parallel", "arbitrary")),
        name="pool_prompt",
    )(rest, rest, w, scale)


def _pool_sample(rest, prev, w, scale):
    ucol = R_UC // C_WIDTH
    rows = BB_POOL * DEC_SEQ
    base = N_PROMPT // rows
    return pl.pallas_call(
        functools.partial(_pool_kernel, from_start=False, n_seq=BB_POOL),
        grid=(DEC_BATCH // BB_POOL,),
        in_specs=[
            pl.BlockSpec((rows, C_WIDTH), lambda b: (base + b, ucol)),
            pl.BlockSpec((BB_POOL * HALO, C_WIDTH), lambda b: (b, 0)),
            pl.BlockSpec((C_GROUPS, C_GROUP_W, C_GROUP_W), lambda b: (0, 0, 0)),
            pl.BlockSpec((1, C_WIDTH), lambda b: (0, 0)),
        ],
        out_specs=pl.BlockSpec((rows, C_WIDTH), lambda b: (b, 0)),
        out_shape=jax.ShapeDtypeStruct((N_SAMPLE, C_WIDTH), BF16),
        compiler_params=_params(("parallel",)),
        name="pool_sample",
    )(rest, prev, w, scale)


def _layer_norm(x, g, b):
    mu = jnp.mean(x, axis=1, keepdims=True)
    xc = x - mu
    var = jnp.mean(xc * xc, axis=1, keepdims=True)
    return xc * lax.rsqrt(var + LN_EPS) * g + b


def _merge_kernel(x_ref, gates_ref, oa_p_ref, oa_s_ref, ob_p_ref, ob_s_ref, oc_p_ref, oc_s_ref,
                  wa_ref, wb_ref, wc_ref, wo_ref, g_ref, b_ref, o_ref, oT_ref):
    in_sample = pl.program_id(0) >= N_PROMPT // TM_MERGE

    def gate(i):
        z = gates_ref[:, i * D_MODEL:(i + 1) * D_MODEL]
        return 1.0 / (1.0 + jnp.exp(-z))

    def branch(p_ref, s_ref, w_ref):
        o = jnp.where(in_sample, s_ref[...], p_ref[...])
        return jnp.dot(o, w_ref[...], preferred_element_type=F32)

    merged = (gate(0) * branch(oa_p_ref, oa_s_ref, wa_ref)
              + gate(1) * branch(ob_p_ref, ob_s_ref, wb_ref)
              + gate(2) * branch(oc_p_ref, oc_s_ref, wc_ref))
    mix = jnp.dot(merged.astype(BF16), wo_ref[...], preferred_element_type=F32)
    y = _layer_norm(DN_ALPHA * x_ref[...] + mix, g_ref[...], b_ref[...])
    o_ref[...] = y
    oT_ref[...] = y.T.astype(BF16)


def _merge(x, rest, branches, wa, wb, wc, wo, g, b):
    n_p = N_PROMPT // TM_MERGE
    row = pl.BlockSpec((TM_MERGE, D_MODEL), lambda i: (i, 0))
    row_p = pl.BlockSpec((TM_MERGE, D_MODEL), lambda i: (jnp.minimum(i, n_p - 1), 0))
    row_s = pl.BlockSpec((TM_MERGE, D_MODEL), lambda i: (jnp.maximum(i - n_p, 0), 0))
    wspec = pl.BlockSpec((D_MODEL, D_MODEL), lambda i: (0, 0))
    vec = pl.BlockSpec((1, D_MODEL), lambda i: (0, 0))
    (oa_p, oa_s), (ob_p, ob_s), (oc_p, oc_s) = branches
    return pl.pallas_call(
        _merge_kernel,
        grid=(N_TOK // TM_MERGE,),
        in_specs=[row, pl.BlockSpec((TM_MERGE, 3 * D_MODEL), lambda i: (i, R_GATES)),
                  row_p, row_s, row_p, row_s, row_p, row_s, wspec, wspec, wspec, wspec, vec, vec],
        out_specs=[row, pl.BlockSpec((D_MODEL, TM_MERGE), lambda i: (0, i))],
        out_shape=[jax.ShapeDtypeStruct((N_TOK, D_MODEL), F32),
                   jax.ShapeDtypeStruct((D_MODEL, N_TOK), BF16)],
        compiler_params=_params(("parallel",)),
        name="merge",
    )(x, rest, oa_p, oa_s, ob_p, ob_s, oc_p, oc_s, wa, wb, wc, wo, g, b)


def _merge_sort_pairs(n):
    size = 1
    while size < n:
        size *= 2
    pairs = []
    p = 1
    while p < size:
        k = p
        while k >= 1:
            for j in range(k % p, size - k, 2 * k):
                for i in range(min(k, size - j - k)):
                    if (i + j) // (2 * p) == (i + j + k) // (2 * p):
                        pairs.append((i + j, i + j + k))
            k //= 2
        p *= 2
    return [(a, b) for a, b in pairs if b < n]


def _top_desc(s, n):
    v = [s[SUBLANES * k:SUBLANES * (k + 1), :] for k in range(s.shape[0] // SUBLANES)]
    depth = len(v)
    for a, b in _merge_sort_pairs(depth):
        v[a], v[b] = jnp.maximum(v[a], v[b]), jnp.minimum(v[a], v[b])
    vals = []
    for r in range(n):
        m = jnp.max(v[0], axis=0, keepdims=True)
        vals.append(m)
        hit = v[0] == m
        needed = n - r - 1
        for k in range(min(depth - 1, needed)):
            v[k] = jnp.where(hit, v[k + 1], v[k])
        if needed >= depth:
            v[depth - 1] = jnp.where(hit, -jnp.inf, v[depth - 1])
    return vals


def _rank_of(s, vals):
    rank = jnp.full(s.shape, float(len(vals)), F32)
    for r, val in enumerate(vals):
        rank = jnp.where(s == val, float(r), rank)
    return rank


def _peer_topk_kernel(xT_ref, wq_ref, sk_ref, cnt_ref, e1_ref, rank_ref, e2_ref):
    tt = xT_ref.shape[1]
    row8 = lax.broadcasted_iota(jnp.int32, (SUBLANES, tt), 0)

    def scores(hp):
        qT = jnp.dot(wq_ref[hp * PEER_DHALF:(hp + 1) * PEER_DHALF, :], xT_ref[...], preferred_element_type=F32)
        return jnp.dot(sk_ref[hp], qT.astype(BF16), preferred_element_type=F32)

    for h in range(PEER_HEADS):
        s1, s2 = scores(2 * h), scores(2 * h + 1)
        vals1 = _top_desc(s1, PEER_TOPK)
        vals2 = _top_desc(s2, PEER_TOPK)
        rank_ref[h] = _rank_of(s2, vals2).astype(BF16)
        e2_ref[h] = jnp.exp(s2 - vals2[0]).astype(BF16)
        sv1 = jnp.concatenate(vals1, axis=0)
        sv2 = jnp.concatenate(vals2, axis=0)
        pieces = [sv1[0:1, :] + sv2]
        for a in range(1, 8):
            nb = PEER_TOPK // (a + 1)
            pieces.append(jnp.where(row8 < nb, sv1[a:a + 1, :] + sv2[0:8, :], -jnp.inf))
        pieces.append(sv2[0:1, :] + sv1[8:16, :])
        cand = _top_desc(jnp.concatenate(pieces, axis=0), PEER_TOPK)
        top = cand[0]
        z = jnp.zeros_like(top)
        for r in range(PEER_TOPK):
            z = z + jnp.exp(cand[r] - top)
        kth = cand[PEER_TOPK - 1]
        counts = [jnp.sum(jnp.where(p >= kth, 1.0, 0.0), axis=0, keepdims=True) for p in pieces[:8]]
        in_top = s1 >= sv1[PEER_TOPK - 1:PEER_TOPK, :]
        cnt = jnp.where(in_top, jnp.where(s1 + sv2[0:1, :] >= kth, 1.0, 0.0), 0.0)
        for a in range(SUBLANES):
            cnt = jnp.where(s1 == sv1[a:a + 1, :], counts[a], cnt)
        e1 = jnp.exp(s1 - (sv1[0:1, :] + jnp.log(z))) * SQRT_HALF
        for c in range(tt // LANES):
            cnt_ref[h, c] = cnt[:, c * LANES:(c + 1) * LANES]
            e1_ref[h, c] = e1[:, c * LANES:(c + 1) * LANES]


def _peer_topk(xT, wqT, sk):
    spec = pl.BlockSpec((PEER_HEADS, N_KEYS, TT_TOPK), lambda t: (0, 0, t))
    row_spec = pl.BlockSpec((PEER_HEADS, TT_TOPK // LANES, N_KEYS, LANES), lambda t: (0, t, 0, 0))
    wide = jax.ShapeDtypeStruct((PEER_HEADS, N_TOK // LANES, N_KEYS, LANES), F32)
    narrow = jax.ShapeDtypeStruct((PEER_HEADS, N_KEYS, N_TOK), BF16)
    return pl.pallas_call(
        _peer_topk_kernel,
        grid=(N_TOK // TT_TOPK,),
        in_specs=[
            pl.BlockSpec((D_MODEL, TT_TOPK), lambda t: (0, t)),
            pl.BlockSpec((PEER_HEADS * PEER_DKEY, D_MODEL), lambda t: (0, 0)),
            pl.BlockSpec((2 * PEER_HEADS, N_KEYS, PEER_DHALF), lambda t: (0, 0, 0)),
        ],
        out_specs=[row_spec, row_spec, spec, spec],
        out_shape=[wide, wide, narrow, narrow],
        compiler_params=_params(("parallel",)),
        name="peer_topk",
    )(xT, wqT, sk)


SQRT_HALF = 2.0 ** -0.5


def _gelu_unscaled(x):
    t = x * SQRT_HALF
    return t * (1.0 + lax.erf(t))


LC_PEER = 256


def _bf16_rows(ref, h, r, chunks, n_rows):
    x = jnp.concatenate([ref[h, c, pl.ds(r, 2 * SUBLANES, stride=0), :] for c in chunks], axis=1)
    packed = x.astype(BF16)
    return jnp.concatenate([packed] * (n_rows // packed.shape[0]), axis=0)


MM_PIECE = 512


def _peer_main_kernel(xT_ref, u_ref, vt_ref, cnt_ref, e1_ref, rank_ref, e2_ref, x_ref, g_ref, b_ref,
                      o_ref, ob_ref, yT_ref, s_scr, wh_scr):
    i = pl.program_id(1)
    tt = xT_ref.shape[1]

    @pl.when(i == 0)
    def _():
        yT_ref[...] = jnp.zeros_like(yT_ref)

    zero = jnp.zeros((N_KEYS, LC_PEER), BF16)
    per_piece = MM_PIECE // N_KEYS
    for ii in range(IB_PEER):
        rows = slice(ii * N_KEYS, (ii + 1) * N_KEYS)
        if ii % per_piece == 0:
            piece = slice(ii * N_KEYS, ii * N_KEYS + MM_PIECE)
            s_scr[piece, :] = jnp.dot(u_ref[0, piece, :], xT_ref[...], preferred_element_type=F32)
        for lc in range(tt // LC_PEER):
            cols = slice(lc * LC_PEER, (lc + 1) * LC_PEER)
            chunks = range(lc * LC_PEER // LANES, (lc + 1) * LC_PEER // LANES)
            w = zero
            for h in range(PEER_HEADS):
                cnt = _bf16_rows(cnt_ref, h, ii, chunks, N_KEYS)
                e1 = _bf16_rows(e1_ref, h, ii, chunks, N_KEYS)
                w = w + jnp.where(rank_ref[h, :, cols] < cnt, e2_ref[h, :, cols] * e1, zero)
            wh_scr[rows, cols] = w
    for ii in range(IB_PEER):
        rows = slice(ii * N_KEYS, (ii + 1) * N_KEYS)
        wh_scr[rows, :] = wh_scr[rows, :] * _gelu_unscaled(s_scr[rows, :]).astype(BF16)
    yT_ref[...] += jnp.dot(vt_ref[0], wh_scr[...], preferred_element_type=F32)

    @pl.when(i == pl.num_programs(1) - 1)
    def _():
        y = _layer_norm(DN_ALPHA * x_ref[...] + yT_ref[...].T, g_ref[...], b_ref[...])
        o_ref[...] = y
        ob_ref[...] = y.astype(BF16)


def _peer_main(xT, u, vt, layer, cnt, e1, rank, e2, x1, g, b):
    eb = IB_PEER * N_KEYS
    row_spec = pl.BlockSpec((PEER_HEADS, TT_PEER // LANES, IB_PEER, LANES), lambda t, i: (0, t, i, 0))
    tab_spec = pl.BlockSpec((PEER_HEADS, N_KEYS, TT_PEER), lambda t, i: (0, 0, t))
    tok_spec = pl.BlockSpec((TT_PEER, D_MODEL), lambda t, i: (t, 0))
    vec = pl.BlockSpec((1, D_MODEL), lambda t, i: (0, 0))
    return pl.pallas_call(
        _peer_main_kernel,
        grid=(N_TOK // TT_PEER, N_EXPERTS // eb),
        in_specs=[
            pl.BlockSpec((D_MODEL, TT_PEER), lambda t, i: (0, t)),
            pl.BlockSpec((1, eb, D_MODEL), lambda t, i: (layer, i, 0)),
            pl.BlockSpec((1, D_MODEL, eb), lambda t, i: (layer, 0, i)),
            row_spec, row_spec, tab_spec, tab_spec, tok_spec, vec, vec,
        ],
        out_specs=[tok_spec, tok_spec],
        out_shape=[jax.ShapeDtypeStruct((N_TOK, D_MODEL), F32),
                   jax.ShapeDtypeStruct((N_TOK, D_MODEL), BF16)],
        scratch_shapes=[pltpu.VMEM((D_MODEL, TT_PEER), F32),
                        pltpu.VMEM((eb, TT_PEER), F32), pltpu.VMEM((eb, TT_PEER), BF16)],
        compiler_params=_params(("parallel", "arbitrary")),
        name="peer_main",
    )(xT, u, vt, cnt, e1, rank, e2, x1, g, b)


def _rope_tables():
    half = ROT_DIM // 2
    pos = jnp.concatenate([jnp.arange(SEQ), PAST_LEN + (jnp.arange(TM_QK) % DEC_SEQ)])
    inv = ROPE_THETA ** (-jnp.arange(half, dtype=F32) / half)
    ang = pos.astype(F32)[:, None] * inv[None, :]
    cos, sin = jnp.cos(ang), jnp.sin(ang)
    n = pos.shape[0]
    one = jnp.ones((n, A_HEAD_DIM - ROT_DIM), F32)
    zero = jnp.zeros((n, A_HEAD_DIM - ROT_DIM), F32)
    zh = jnp.zeros((n, half), F32)
    reps = LANES // A_HEAD_DIM
    c = jnp.tile(jnp.concatenate([cos, cos, one], 1), (1, reps))
    s1 = jnp.tile(jnp.concatenate([-sin, zh, zero], 1), (1, reps))
    s2 = jnp.tile(jnp.concatenate([zh, sin, zero], 1), (1, reps))
    return c, s1, s2


def _split_cols(w):
    cuts = [int(c) for c in np.cumsum(SPLITS)[:-1]]
    return jnp.split(w, cuts, axis=-1)


def _layer(layer, x, xb, k_state, v_state, gla_states, gla_out, pool_state, rope, w_in, b_in, sinks,
           w_alpha, b_alpha, gla_g, w_pool, pool_scale, w_a, w_b, w_c, w_out, ln1_g, ln1_b,
           peer_query, peer_subkeys, peer_u, peer_vt, ln2_g, ln2_b):
    qa_w, ka_w, va_w, qb_w, kb_w, vb_w, lr_w, gb_w, uc_w, gates_w = _split_cols(w_in)
    qa_b, ka_b, va_b, qb_b, kb_b, vb_b, lr_b, gb_b, uc_b, gates_b = _split_cols(b_in[None, :])
    lr_pad = LANES - B_GATE_RANK
    w_qk = jnp.concatenate([qa_w, ka_w], 1).astype(BF16)
    b_qk = jnp.concatenate([qa_b, ka_b], 1)
    w_rest = jnp.concatenate([gates_w, vb_w, gb_w, uc_w, qb_w, kb_w, va_w,
                              jnp.pad(lr_w, ((0, 0), (0, lr_pad)))], 1).astype(BF16)
    b_rest = jnp.concatenate([gates_b, vb_b, gb_b, uc_b, qb_b, kb_b, va_b,
                              jnp.pad(lr_b, ((0, 0), (0, lr_pad)))], 1)

    qk = _proj_qk(xb, w_qk, b_qk, *rope)
    rest = _proj_rest(xb, w_rest, b_rest)

    ks = k_state.reshape(DEC_BATCH, WINDOW, A_KV)
    vs = v_state.reshape(DEC_BATCH, WINDOW, A_KV)
    oa = (_attn_prompt(qk, rest, sinks), _attn_sample(qk, rest, sinks, ks, vs))

    wa = jnp.pad(w_alpha, ((0, lr_pad), (0, 0))).astype(BF16)
    ba = b_alpha[None, :]
    gla_p, gla_s = gla_out
    ob_p, gla_p = _gla(rest, jnp.zeros((1, BATCH, B_HEADS, B_DK, B_DV), F32), layer, gla_p, wa, ba, gla_g,
                       n_batch=BATCH, seq=SEQ, chunk=GLA_CHUNK, row_base=0, group=GLA_GROUP_PROMPT)
    ob_s, gla_s = _gla(rest, gla_states, layer, gla_s, wa, ba, gla_g, n_batch=DEC_BATCH, seq=DEC_SEQ,
                       chunk=math.gcd(DEC_SEQ, GLA_CHUNK), row_base=N_PROMPT, group=GLA_GROUP_SAMPLE)
    ob = (ob_p.reshape(N_PROMPT, B_V), ob_s.reshape(N_SAMPLE, B_V))

    wp = w_pool.astype(BF16)
    ps = pool_scale[None, :]
    prev = jnp.pad(pool_state, ((0, 0), (HALO - POOL_STATE, 0), (0, 0))).reshape(DEC_BATCH * HALO, C_WIDTH)
    oc = (_pool_prompt(rest, wp, ps), _pool_sample(rest, prev, wp, ps))

    x1, x1T = _merge(x, rest, (oa, ob, oc), w_a.astype(BF16), w_b.astype(BF16), w_c.astype(BF16),
                     w_out.astype(BF16), ln1_g[None, :], ln1_b[None, :])

    wqT = peer_query.reshape(D_MODEL, PEER_HEADS * PEER_DKEY).T.astype(BF16)
    sk = peer_subkeys.reshape(2 * PEER_HEADS, N_KEYS, PEER_DHALF).astype(BF16)
    cnt, e1, rank, e2 = _peer_topk(x1T, wqT, sk)
    x2, x2b = _peer_main(x1T, peer_u, peer_vt, layer, cnt, e1, rank, e2, x1, ln2_g[None, :], ln2_b[None, :])

    def prompt_tail(t, col0, width, n):
        return jnp.stack([t[(b + 1) * SEQ - n:(b + 1) * SEQ, col0:col0 + width] for b in range(BATCH)])

    def sample_tail(state, t, col0, width, n):
        new = t[N_PROMPT:, col0:col0 + width].reshape(DEC_BATCH, DEC_SEQ, width)
        return jnp.concatenate([state, new], 1)[:, -n:]

    kv_shape = (-1, WINDOW, A_KV_HEADS, A_HEAD_DIM)
    states = (prompt_tail(qk, A_Q, A_KV, WINDOW).reshape(kv_shape),
              prompt_tail(rest, R_VA, A_KV, WINDOW).reshape(kv_shape),
              prompt_tail(rest, R_UC, C_WIDTH, POOL_STATE),
              sample_tail(ks, qk, A_Q, A_KV, WINDOW).reshape(kv_shape),
              sample_tail(vs, rest, R_VA, A_KV, WINDOW).reshape(kv_shape),
              sample_tail(pool_state, rest, R_UC, C_WIDTH, POOL_STATE))
    return x2, x2b, states, (gla_p, gla_s)


def kernel(x_prompt, x_sample, state_win_k, state_win_v, state_gla, state_pool, w_in, b_in, attn_sinks,
           w_alpha, b_alpha, gla_norm_g, w_pool, pool_scale, w_branch_a, w_branch_b, w_branch_c, w_out,
           ln1_g, ln1_b, peer_query, peer_subkeys, peer_u, peer_v, ln2_g, ln2_b):
    x = jnp.concatenate([x_prompt.reshape(N_PROMPT, D_MODEL), x_sample.reshape(N_SAMPLE, D_MODEL)], 0)
    xb = x.astype(BF16)
    rope = _rope_tables()
    peer_ub = peer_u.astype(BF16)
    peer_vtb = jnp.swapaxes(peer_v, 1, 2).astype(BF16)
    per_layer = []
    gla_out = (None, None)
    for l in range(DEPTH):
        x, xb, states, gla_out = _layer(
            l, x, xb, state_win_k[l], state_win_v[l], state_gla, gla_out, state_pool[l], rope,
            w_in[l], b_in[l], attn_sinks[l], w_alpha[l], b_alpha[l], gla_norm_g[l],
            w_pool[l], pool_scale[l], w_branch_a[l], w_branch_b[l], w_branch_c[l], w_out[l],
            ln1_g[l], ln1_b[l], peer_query[l], peer_subkeys[l], peer_ub, peer_vtb, ln2_g[l], ln2_b[l])
        per_layer.append(states)
    pk, pv, pp, sk, sv, sp = [jnp.stack([per_layer[l][i] for l in range(DEPTH)]) for i in range(6)]
    return (x[:N_PROMPT].reshape(BATCH, SEQ, D_MODEL), x[N_PROMPT:].reshape(DEC_BATCH, DEC_SEQ, D_MODEL),
            pk, pv, gla_out[0], pp, sk, sv, gla_out[1], sp)
```

```python
import functools
import math

import jax
import jax.numpy as jnp
import numpy as np
from jax import lax
from jax.experimental import pallas as pl
from jax.experimental.pallas import tpu as pltpu

F32 = jnp.float32
BF16 = jnp.bfloat16

D_MODEL = 1024
BATCH = 8
SEQ = 2048
DEPTH = 2
DEC_BATCH = 128
DEC_SEQ = 8
PAST_LEN = 16384

A_HEADS = 16
A_KV_HEADS = 2
A_HEAD_DIM = 64
A_GROUP = A_HEADS // A_KV_HEADS
WINDOW = 128
ROT_DIM = A_HEAD_DIM // 4
ROPE_THETA = 500000.0
NEG_INF = -1e30
B_HEADS = 4
B_DK = D_MODEL // 2 // B_HEADS
B_DV = D_MODEL // B_HEADS
B_GATE_RANK = 16
B_TAU = 16.0
GLA_CHUNK = 64
POOL_WINDOWS = (2, 4, 8, 16)
C_GROUPS = len(POOL_WINDOWS)
C_GROUP_W = D_MODEL // C_GROUPS
C_WIDTH = C_GROUPS * C_GROUP_W
POOL_STATE = max(POOL_WINDOWS) - 1
PEER_HEADS = 8
N_KEYS = 128
N_EXPERTS = N_KEYS * N_KEYS
PEER_TOPK = 16
PEER_DKEY = 256
PEER_DHALF = PEER_DKEY // 2
DN_ALPHA = (2 * DEPTH) ** 0.25
LN_EPS = 1e-5
RMS_EPS = 1e-6

A_Q = A_HEADS * A_HEAD_DIM
A_KV = A_KV_HEADS * A_HEAD_DIM
B_QK = B_HEADS * B_DK
B_V = B_HEADS * B_DV
SPLITS = (A_Q, A_KV, A_KV, B_QK, B_QK, B_V, B_GATE_RANK, B_V, C_WIDTH, 3 * D_MODEL)

LANES = 128
SUBLANES = 8
N_PROMPT = BATCH * SEQ
N_SAMPLE = DEC_BATCH * DEC_SEQ
N_TOK = N_PROMPT + N_SAMPLE

R_GATES = 0
R_VB = 3 * D_MODEL
R_GB = R_VB + B_V
R_UC = R_GB + B_V
R_QB = R_UC + C_WIDTH
R_KB = R_QB + B_QK
R_VA = R_KB + B_QK
R_LR = R_VA + A_KV
R_WIDTH = R_LR + LANES
QK_WIDTH = A_Q + A_KV

VMEM_LIMIT = 48 * 1024 * 1024

TM_QK = 512
TM_REST = 512
TN_REST = R_WIDTH // 2
TM_MERGE = 256
TP_POOL = 512
GLA_GROUP_PROMPT = 4
GLA_GROUP_SAMPLE = 4
TT_TOPK = 512
TT_PEER = 512
IB_PEER = 16


def _params(sem):
    return pltpu.CompilerParams(dimension_semantics=sem, vmem_limit_bytes=VMEM_LIMIT)


def _qk_kernel(x_ref, w_ref, b_ref, c_ref, s1_ref, s2_ref, o_ref):
    y = jnp.dot(x_ref[...], w_ref[...], preferred_element_type=F32) + b_ref[...]
    c = c_ref[...]
    s1 = s1_ref[...]
    s2 = s2_ref[...]
    for j in range(QK_WIDTH // LANES):
        yj = y[:, j * LANES:(j + 1) * LANES]
        up = pltpu.roll(yj, LANES - ROT_DIM // 2, axis=1)
        dn = pltpu.roll(yj, ROT_DIM // 2, axis=1)
        o_ref[:, j * LANES:(j + 1) * LANES] = yj * c + up * s1 + dn * s2


def _proj_qk(xb, w, b, rope_c, rope_s1, rope_s2):
    n_prompt_blocks = SEQ // TM_QK

    def tab_map(i):
        return (jnp.where(i < N_PROMPT // TM_QK, i % n_prompt_blocks, n_prompt_blocks), 0)

    tab_spec = pl.BlockSpec((TM_QK, LANES), tab_map)
    return pl.pallas_call(
        _qk_kernel,
        grid=(N_TOK // TM_QK,),
        in_specs=[
            pl.BlockSpec((TM_QK, D_MODEL), lambda i: (i, 0)),
            pl.BlockSpec((D_MODEL, QK_WIDTH), lambda i: (0, 0)),
            pl.BlockSpec((1, QK_WIDTH), lambda i: (0, 0)),
            tab_spec, tab_spec, tab_spec,
        ],
        out_specs=pl.BlockSpec((TM_QK, QK_WIDTH), lambda i: (i, 0)),
        out_shape=jax.ShapeDtypeStruct((N_TOK, QK_WIDTH), F32),
        compiler_params=_params(("parallel",)),
        name="proj_qk",
    )(xb, w, b, rope_c, rope_s1, rope_s2)


def _mm_bias_kernel(x_ref, w_ref, b_ref, o_ref):
    o_ref[...] = jnp.dot(x_ref[...], w_ref[...], preferred_element_type=F32) + b_ref[...]


def _proj_rest(xb, w, b):
    return pl.pallas_call(
        _mm_bias_kernel,
        grid=(R_WIDTH // TN_REST, N_TOK // TM_REST),
        in_specs=[
            pl.BlockSpec((TM_REST, D_MODEL), lambda j, i: (i, 0)),
            pl.BlockSpec((D_MODEL, TN_REST), lambda j, i: (0, j)),
            pl.BlockSpec((1, TN_REST), lambda j, i: (0, j)),
        ],
        out_specs=pl.BlockSpec((TM_REST, TN_REST), lambda j, i: (i, j)),
        out_shape=jax.ShapeDtypeStruct((N_TOK, R_WIDTH), F32),
        compiler_params=_params(("parallel", "arbitrary")),
        name="proj_rest",
    )(xb, w, b)


HEADS_PER_PASS = 16


def _attend(q, kk, vv, sink_ref, c_min, o_ref):
    tq = q.shape[0]
    r = lax.broadcasted_iota(jnp.int32, (tq, 2 * WINDOW), 0)
    c = lax.broadcasted_iota(jnp.int32, (tq, 2 * WINDOW), 1)
    ok = (c > r) & (c <= r + WINDOW) & (c >= c_min)
    qb = (q * (A_HEAD_DIM ** -0.5)).astype(BF16)
    nt = (((1,), (1,)), ((), ()))
    for h0 in range(0, A_HEADS, HEADS_PER_PASS):
        hs = range(h0, h0 + HEADS_PER_PASS)
        col = {h: slice(h * A_HEAD_DIM, (h + 1) * A_HEAD_DIM) for h in hs}
        kv = {h: slice((h // A_GROUP) * A_HEAD_DIM, (h // A_GROUP + 1) * A_HEAD_DIM) for h in hs}
        s = {h: lax.dot_general(qb[:, col[h]], kk[:, kv[h]], nt, preferred_element_type=F32) for h in hs}
        s = {h: jnp.where(ok, s[h], NEG_INF) for h in hs}
        m = {h: jnp.maximum(jnp.max(s[h], axis=1, keepdims=True), sink_ref[h]) for h in hs}
        p = {h: jnp.exp(s[h] - m[h]) for h in hs}
        denom = {h: jnp.sum(p[h], axis=1, keepdims=True) + jnp.exp(sink_ref[h] - m[h]) for h in hs}
        o = {h: jnp.dot(p[h].astype(BF16), vv[:, kv[h]], preferred_element_type=F32) / denom[h] for h in hs}
        for h in hs:
            o_ref[:, col[h]] = o[h].astype(o_ref.dtype)


def _attn_prompt_kernel(sink_ref, q_ref, kc_ref, kp_ref, vc_ref, vp_ref, o_ref):
    n = pl.program_id(1)
    kk = jnp.concatenate([kp_ref[...], kc_ref[...]], axis=0).astype(BF16)
    vv = jnp.concatenate([vp_ref[...], vc_ref[...]], axis=0).astype(BF16)
    _attend(q_ref[...], kk, vv, sink_ref, jnp.where(n > 0, 0, WINDOW), o_ref)


def _attn_prompt(qk, rest, sinks):
    nb = SEQ // WINDOW
    kcol = A_Q // A_KV
    vcol = R_VA // A_KV

    def cur(b, n):
        return b * nb + n

    def prev(b, n):
        return b * nb + jnp.maximum(n - 1, 0)

    return pl.pallas_call(
        _attn_prompt_kernel,
        grid=(BATCH, nb),
        in_specs=[
            pl.BlockSpec(memory_space=pltpu.SMEM),
            pl.BlockSpec((WINDOW, A_Q), lambda b, n: (cur(b, n), 0)),
            pl.BlockSpec((WINDOW, A_KV), lambda b, n: (cur(b, n), kcol)),
            pl.BlockSpec((WINDOW, A_KV), lambda b, n: (prev(b, n), kcol)),
            pl.BlockSpec((WINDOW, A_KV), lambda b, n: (cur(b, n), vcol)),
            pl.BlockSpec((WINDOW, A_KV), lambda b, n: (prev(b, n), vcol)),
        ],
        out_specs=pl.BlockSpec((WINDOW, A_Q), lambda b, n: (cur(b, n), 0)),
        out_shape=jax.ShapeDtypeStruct((N_PROMPT, A_Q), BF16),
        compiler_params=_params(("parallel", "arbitrary")),
        name="attn_prompt",
    )(sinks, qk, qk, qk, rest, rest)


BB_ATTN = 8


def _attn_sample_kernel(sink_ref, q_ref, kn_ref, vn_ref, ks_ref, vs_ref, o_ref):
    pad = jnp.zeros((WINDOW - DEC_SEQ, A_KV), F32)
    rows = A_GROUP * DEC_SEQ
    t = lax.broadcasted_iota(jnp.int32, (rows, 2 * WINDOW), 0) % DEC_SEQ
    c = lax.broadcasted_iota(jnp.int32, (rows, 2 * WINDOW), 1)
    ok = (c > t) & (c <= t + WINDOW)
    nt = (((1,), (1,)), ((), ()))

    def body(pair, carry):
        chains = [(e, g) for e in range(2) for g in range(A_KV_HEADS)]
        elem = {e: pair * 2 + e for e in range(2)}
        row0 = {e: pl.multiple_of(elem[e] * DEC_SEQ, DEC_SEQ) for e in range(2)}
        q = {e: q_ref[pl.ds(row0[e], DEC_SEQ), :] * (A_HEAD_DIM ** -0.5) for e in range(2)}
        kk = {e: jnp.concatenate([ks_ref[elem[e]], kn_ref[pl.ds(row0[e], DEC_SEQ), :], pad], axis=0).astype(BF16)
              for e in range(2)}
        vv = {e: jnp.concatenate([vs_ref[elem[e]], vn_ref[pl.ds(row0[e], DEC_SEQ), :], pad], axis=0).astype(BF16)
              for e in range(2)}
        heads = {g: range(g * A_GROUP, (g + 1) * A_GROUP) for g in range(A_KV_HEADS)}
        ds = {g: slice(g * A_HEAD_DIM, (g + 1) * A_HEAD_DIM) for g in range(A_KV_HEADS)}
        sink = {g: sink_ref[g * rows:(g + 1) * rows, 0:1] for g in range(A_KV_HEADS)}
        qg = {(bb, g): jnp.concatenate([q[bb][:, h * A_HEAD_DIM:(h + 1) * A_HEAD_DIM] for h in heads[g]],
                                       axis=0).astype(BF16) for bb, g in chains}
        s = {(bb, g): lax.dot_general(qg[bb, g], kk[bb][:, ds[g]], nt, preferred_element_type=F32)
             for bb, g in chains}
        s = {ch: jnp.where(ok, s[ch], NEG_INF) for ch in chains}
        m = {(bb, g): jnp.maximum(jnp.max(s[bb, g], axis=1, keepdims=True), sink[g]) for bb, g in chains}
        p = {ch: jnp.exp(s[ch] - m[ch]) for ch in chains}
        denom = {(bb, g): jnp.sum(p[bb, g], axis=1, keepdims=True) + jnp.exp(sink[g] - m[bb, g])
                 for bb, g in chains}
        o = {(bb, g): jnp.dot(p[bb, g].astype(BF16), vv[bb][:, ds[g]], preferred_element_type=F32) / denom[bb, g]
             for bb, g in chains}
        for bb, g in chains:
            for k, h in enumerate(heads[g]):
                o_ref[pl.ds(row0[bb], DEC_SEQ), h * A_HEAD_DIM:(h + 1) * A_HEAD_DIM] = (
                    o[bb, g][k * DEC_SEQ:(k + 1) * DEC_SEQ, :].astype(o_ref.dtype))
        return carry

    lax.fori_loop(0, BB_ATTN // 2, body, 0)


def _attn_sample(qk, rest, sinks, k_state, v_state):
    rows = BB_ATTN * DEC_SEQ
    base = N_PROMPT // rows
    kcol = A_Q // A_KV
    vcol = R_VA // A_KV
    sink_rows = jnp.broadcast_to(jnp.repeat(sinks, DEC_SEQ)[:, None], (A_HEADS * DEC_SEQ, LANES))
    return pl.pallas_call(
        _attn_sample_kernel,
        grid=(DEC_BATCH // BB_ATTN,),
        in_specs=[
            pl.BlockSpec((A_HEADS * DEC_SEQ, LANES), lambda i: (0, 0)),
            pl.BlockSpec((rows, A_Q), lambda i: (base + i, 0)),
            pl.BlockSpec((rows, A_KV), lambda i: (base + i, kcol)),
            pl.BlockSpec((rows, A_KV), lambda i: (base + i, vcol)),
            pl.BlockSpec((BB_ATTN, WINDOW, A_KV), lambda i: (i, 0, 0)),
            pl.BlockSpec((BB_ATTN, WINDOW, A_KV), lambda i: (i, 0, 0)),
        ],
        out_specs=pl.BlockSpec((rows, A_Q), lambda i: (i, 0)),
        out_shape=jax.ShapeDtypeStruct((N_SAMPLE, A_Q), BF16),
        compiler_params=_params(("parallel",)),
        name="attn_sample",
    )(sink_rows, qk, qk, rest, k_state, v_state)


def _split3(x):
    hi = x.astype(BF16)
    r1 = x - hi.astype(F32)
    mid = r1.astype(BF16)
    lo = (r1 - mid.astype(F32)).astype(BF16)
    return hi, mid, lo


GLA_INPUTS = ((LANES, R_LR), (B_QK, R_QB), (B_QK, R_KB), (B_V, R_VB), (B_V, R_GB))


def _gla_kernel(*refs, n_chunks, group, chunk, layer, n_in):
    ins, rest_refs = refs[:n_in], refs[n_in:]
    per = n_in // len(GLA_INPUTS)
    s0_ref, wa_ref, ba_ref, g_ref = rest_refs[:4]
    prev_ref = rest_refs[4] if layer else None
    o_ref, sout_ref, st_ref = rest_refs[-3:]
    ci = pl.program_id(1)
    c = chunk

    def rows(inp, g):
        if per == 1:
            return ins[inp][g * c:(g + 1) * c, :]
        return ins[inp * per + g][...]

    single = n_chunks == 1
    if not single:
        @pl.when(ci == 0)
        def _():
            for g in range(group):
                for h in range(B_HEADS):
                    st_ref[g, h] = s0_ref[0, g, h].T

    ri = lax.broadcasted_iota(jnp.int32, (c, c), 0)
    cj = lax.broadcasted_iota(jnp.int32, (c, c), 1)
    causal = cj <= ri
    tri = jnp.where(causal, 1.0, 0.0).astype(BF16)
    nt = (((1,), (1,)), ((), ()))
    G = range(group)
    z = [jnp.dot(rows(0, g).astype(BF16), wa_ref[...], preferred_element_type=F32) + ba_ref[...] for g in G]
    log_a = [-(jnp.maximum(-z[g], 0.0) + jnp.log1p(jnp.exp(-jnp.abs(z[g])))) / B_TAU for g in G]
    parts = [_split3(log_a[g]) for g in G]
    b = [jnp.dot(tri, parts[g][0], preferred_element_type=F32)
         + jnp.dot(tri, parts[g][1], preferred_element_type=F32)
         + jnp.dot(tri, parts[g][2], preferred_element_type=F32) for g in G]
    bl = [b[g][c - 1:c, :] for g in G]
    qd = [(rows(1, g) * (B_DK ** -0.5) * jnp.exp(b[g])).astype(BF16) for g in G]
    kd = [(rows(2, g) * jnp.exp(-b[g])).astype(BF16) for g in G]
    kl = [rows(2, g) * jnp.exp(bl[g] - b[g]) for g in G]
    kl = [kl[g].T if single else kl[g].astype(BF16) for g in G]
    ebl = [jnp.exp(bl[g]) for g in G]
    for h in range(B_HEADS):
        ks = slice(h * B_DK, (h + 1) * B_DK)
        vs = slice(h * B_DV, (h + 1) * B_DV)
        vh = [rows(3, g)[:, vs] for g in G]
        if single:
            st = [s0_ref[0, g, h] for g in G]
            o = [jnp.dot(qd[g][:, ks], st[g].astype(BF16), preferred_element_type=F32) for g in G]
        else:
            st = [st_ref[g, h] for g in G]
            o = [lax.dot_general(qd[g][:, ks], st[g].astype(BF16), nt, preferred_element_type=F32) for g in G]
        att = [lax.dot_general(qd[g][:, ks], kd[g][:, ks], nt, preferred_element_type=F32) for g in G]
        att = [jnp.where(causal, att[g], 0.0).astype(BF16) for g in G]
        o = [o[g] + jnp.dot(att[g], vh[g].astype(BF16), preferred_element_type=F32) for g in G]
        for g in G:
            if single:
                decay = jnp.broadcast_to(ebl[g][:, ks], (SUBLANES, B_DK)).T[:, 0:1]
                for l in range(layer):
                    sout_ref[l, g, h] = prev_ref[l, g, h]
                sout_ref[layer, g, h] = st[g] * decay + jnp.dot(kl[g][ks, :], vh[g],
                                                                  preferred_element_type=F32)
            else:
                st_ref[g, h] = st[g] * ebl[g][:, ks] + jnp.dot(vh[g].T.astype(BF16), kl[g][:, ks],
                                                               preferred_element_type=F32)
        o = [o[g] * lax.rsqrt(jnp.mean(o[g] * o[g], axis=1, keepdims=True) + RMS_EPS) * g_ref[h:h + 1, :]
             for g in G]
        for g in G:
            gate = rows(4, g)[:, vs]
            o_ref[g, :, vs] = (o[g] * (gate / (1.0 + jnp.exp(-gate)))).astype(o_ref.dtype)

    if not single:
        @pl.when(ci == n_chunks - 1)
        def _():
            for g in range(group):
                for l in range(layer):
                    sout_ref[l, g] = prev_ref[l, g]
                for h in range(B_HEADS):
                    sout_ref[layer, g, h] = st_ref[g, h].T


def _gla(rest, s0, layer, prev_states, wa, ba, gain, *, n_batch, seq, chunk, row_base, group):
    n_chunks = seq // chunk
    contiguous = n_chunks == 1
    base = row_base // chunk

    def in_specs_for(width, col):
        if contiguous:
            return [pl.BlockSpec((group * chunk, width), lambda b, ci: (base // group + b, col // width))]
        return [pl.BlockSpec((chunk, width),
                             lambda b, ci, g=g: (base + (b * group + g) * n_chunks + ci, col // width))
                for g in range(group)]

    row_specs = [spec for width, col in GLA_INPUTS for spec in in_specs_for(width, col)]
    s0_layer = layer if s0.shape[0] > 1 else 0
    state_block = (group, B_HEADS, B_DK, B_DV)
    prev_specs = [pl.BlockSpec((layer,) + state_block, lambda b, ci: (0, b, 0, 0, 0))] if layer else []
    prev_args = [prev_states] if layer else []
    return pl.pallas_call(
        functools.partial(_gla_kernel, n_chunks=n_chunks, group=group, chunk=chunk, layer=layer,
                          n_in=len(row_specs)),
        grid=(n_batch // group, n_chunks),
        in_specs=row_specs + [
            pl.BlockSpec((1,) + state_block, lambda b, ci: (s0_layer, b, 0, 0, 0)),
            pl.BlockSpec((LANES, B_QK), lambda b, ci: (0, 0)),
            pl.BlockSpec((1, B_QK), lambda b, ci: (0, 0)),
            pl.BlockSpec((B_HEADS, B_DV), lambda b, ci: (0, 0)),
        ] + prev_specs,
        out_specs=[
            pl.BlockSpec((group, chunk, B_V), lambda b, ci: (b, ci, 0)),
            pl.BlockSpec((layer + 1,) + state_block, lambda b, ci: (0, b, 0, 0, 0)),
        ],
        out_shape=[
            jax.ShapeDtypeStruct((n_batch, seq, B_V), BF16),
            jax.ShapeDtypeStruct((layer + 1, n_batch, B_HEADS, B_DK, B_DV), F32),
        ],
        scratch_shapes=[pltpu.VMEM((group, B_HEADS, B_DV, B_DK), F32)],
        compiler_params=_params(("parallel", "arbitrary")),
        name="gla",
    )(*([rest] * len(row_specs)), s0, wa, ba, gain, *prev_args)


HALO = 16
BB_POOL = 16


def _pool_kernel(u_ref, prev_ref, w_ref, scale_ref, o_ref, *, from_start, n_seq):
    tp = u_ref.shape[0] // n_seq
    if from_start:
        ti = pl.program_id(1)
        t0 = ti * tp
    diffs = [[] for _ in POOL_WINDOWS]
    for sq in range(n_seq):
        u = u_ref[sq * tp:(sq + 1) * tp, :]
        prev = prev_ref[sq * HALO:(sq + 1) * HALO, :]
        if from_start:
            prev = jnp.where(ti > 0, prev, 0.0)
        full = jnp.concatenate([prev, u], axis=0)
        for g, w in enumerate(POOL_WINDOWS):
            cs = slice(g * C_GROUP_W, (g + 1) * C_GROUP_W)
            acc = full[:, cs]
            span = 1
            while span < w:
                acc = acc + pltpu.roll(acc, span, axis=0)
                span *= 2
            wsum = acc[HALO:, :]
            if from_start:
                t = t0 + lax.broadcasted_iota(jnp.int32, (tp, C_GROUP_W), 0)
                cnt = jnp.minimum(t + 1, w).astype(F32)
            else:
                cnt = float(w)
            diffs[g].append(wsum / cnt - u[:, cs])
    for g in range(C_GROUPS):
        cs = slice(g * C_GROUP_W, (g + 1) * C_GROUP_W)
        d = jnp.concatenate(diffs[g], axis=0) if n_seq > 1 else diffs[g][0]
        y = jnp.dot(d.astype(BF16), w_ref[g], preferred_element_type=F32) * scale_ref[:, cs]
        o_ref[:, cs] = y.astype(o_ref.dtype)


def _pool_prompt(rest, w, scale):
    nt = SEQ // TP_POOL
    ucol = R_UC // C_WIDTH

    def halo(b, i):
        return (jnp.maximum((b * SEQ + i * TP_POOL) // HALO - 1, 0), ucol)

    return pl.pallas_call(
        functools.partial(_pool_kernel, from_start=True, n_seq=1),
        grid=(BATCH, nt),
        in_specs=[
            pl.BlockSpec((TP_POOL, C_WIDTH), lambda b, i: (b * nt + i, ucol)),
            pl.BlockSpec((HALO, C_WIDTH), halo),
            pl.BlockSpec((C_GROUPS, C_GROUP_W, C_GROUP_W), lambda b, i: (0, 0, 0)),
            pl.BlockSpec((1, C_WIDTH), lambda b, i: (0, 0)),
        ],
        out_specs=pl.BlockSpec((TP_POOL, C_WIDTH), lambda b, i: (b * nt + i, 0)),
        out_shape=jax.ShapeDtypeStruct((N_PROMPT, C_WIDTH), BF16),
        compiler_params=_params(("parallel", "arbitrary")),
        name="pool_prompt",
    )(rest, rest, w, scale)


def _pool_sample(rest, prev, w, scale):
    ucol = R_UC // C_WIDTH
    rows = BB_POOL * DEC_SEQ
    base = N_PROMPT // rows
    return pl.pallas_call(
        functools.partial(_pool_kernel, from_start=False, n_seq=BB_POOL),
        grid=(DEC_BATCH // BB_POOL,),
        in_specs=[
            pl.BlockSpec((rows, C_WIDTH), lambda b: (base + b, ucol)),
            pl.BlockSpec((BB_POOL * HALO, C_WIDTH), lambda b: (b, 0)),
            pl.BlockSpec((C_GROUPS, C_GROUP_W, C_GROUP_W), lambda b: (0, 0, 0)),
            pl.BlockSpec((1, C_WIDTH), lambda b: (0, 0)),
        ],
        out_specs=pl.BlockSpec((rows, C_WIDTH), lambda b: (b, 0)),
        out_shape=jax.ShapeDtypeStruct((N_SAMPLE, C_WIDTH), BF16),
        compiler_params=_params(("parallel",)),
        name="pool_sample",
    )(rest, prev, w, scale)


def _layer_norm(x, g, b):
    mu = jnp.mean(x, axis=1, keepdims=True)
    xc = x - mu
    var = jnp.mean(xc * xc, axis=1, keepdims=True)
    return xc * lax.rsqrt(var + LN_EPS) * g + b


def _merge_kernel(x_ref, gates_ref, oa_p_ref, oa_s_ref, ob_p_ref, ob_s_ref, oc_p_ref, oc_s_ref,
                  wa_ref, wb_ref, wc_ref, wo_ref, g_ref, b_ref, o_ref, oT_ref):
    in_sample = pl.program_id(0) >= N_PROMPT // TM_MERGE

    def gate(i):
        z = gates_ref[:, i * D_MODEL:(i + 1) * D_MODEL]
        return 1.0 / (1.0 + jnp.exp(-z))

    def branch(p_ref, s_ref, w_ref):
        o = jnp.where(in_sample, s_ref[...], p_ref[...])
        return jnp.dot(o, w_ref[...], preferred_element_type=F32)

    merged = (gate(0) * branch(oa_p_ref, oa_s_ref, wa_ref)
              + gate(1) * branch(ob_p_ref, ob_s_ref, wb_ref)
              + gate(2) * branch(oc_p_ref, oc_s_ref, wc_ref))
    mix = jnp.dot(merged.astype(BF16), wo_ref[...], preferred_element_type=F32)
    y = _layer_norm(DN_ALPHA * x_ref[...] + mix, g_ref[...], b_ref[...])
    o_ref[...] = y
    oT_ref[...] = y.T.astype(BF16)


def _merge(x, rest, branches, wa, wb, wc, wo, g, b):
    n_p = N_PROMPT // TM_MERGE
    row = pl.BlockSpec((TM_MERGE, D_MODEL), lambda i: (i, 0))
    row_p = pl.BlockSpec((TM_MERGE, D_MODEL), lambda i: (jnp.minimum(i, n_p - 1), 0))
    row_s = pl.BlockSpec((TM_MERGE, D_MODEL), lambda i: (jnp.maximum(i - n_p, 0), 0))
    wspec = pl.BlockSpec((D_MODEL, D_MODEL), lambda i: (0, 0))
    vec = pl.BlockSpec((1, D_MODEL), lambda i: (0, 0))
    (oa_p, oa_s), (ob_p, ob_s), (oc_p, oc_s) = branches
    return pl.pallas_call(
        _merge_kernel,
        grid=(N_TOK // TM_MERGE,),
        in_specs=[row, pl.BlockSpec((TM_MERGE, 3 * D_MODEL), lambda i: (i, R_GATES)),
                  row_p, row_s, row_p, row_s, row_p, row_s, wspec, wspec, wspec, wspec, vec, vec],
        out_specs=[row, pl.BlockSpec((D_MODEL, TM_MERGE), lambda i: (0, i))],
        out_shape=[jax.ShapeDtypeStruct((N_TOK, D_MODEL), F32),
                   jax.ShapeDtypeStruct((D_MODEL, N_TOK), BF16)],
        compiler_params=_params(("parallel",)),
        name="merge",
    )(x, rest, oa_p, oa_s, ob_p, ob_s, oc_p, oc_s, wa, wb, wc, wo, g, b)


def _merge_sort_pairs(n):
    size = 1
    while size < n:
        size *= 2
    pairs = []
    p = 1
    while p < size:
        k = p
        while k >= 1:
            for j in range(k % p, size - k, 2 * k):
                for i in range(min(k, size - j - k)):
                    if (i + j) // (2 * p) == (i + j + k) // (2 * p):
                        pairs.append((i + j, i + j + k))
            k //= 2
        p *= 2
    return [(a, b) for a, b in pairs if b < n]


def _top_desc(s, n):
    v = [s[SUBLANES * k:SUBLANES * (k + 1), :] for k in range(s.shape[0] // SUBLANES)]
    depth = len(v)
    for a, b in _merge_sort_pairs(depth):
        v[a], v[b] = jnp.maximum(v[a], v[b]), jnp.minimum(v[a], v[b])
    vals = []
    for r in range(n):
        m = jnp.max(v[0], axis=0, keepdims=True)
        vals.append(m)
        hit = v[0] == m
        needed = n - r - 1
        for k in range(min(depth - 1, needed)):
            v[k] = jnp.where(hit, v[k + 1], v[k])
        if needed >= depth:
            v[depth - 1] = jnp.where(hit, -jnp.inf, v[depth - 1])
    return vals


def _peer_topk_kernel(xT_ref, wq_ref, sk_ref, cnt_ref, e1_ref, rank_ref, e2_ref):
    tt = xT_ref.shape[1]
    row8 = lax.broadcasted_iota(jnp.int32, (SUBLANES, tt), 0)

    def scores(hp):
        qT = jnp.dot(wq_ref[hp * PEER_DHALF:(hp + 1) * PEER_DHALF, :], xT_ref[...], preferred_element_type=F32)
        return jnp.dot(sk_ref[hp], qT.astype(BF16), preferred_element_type=F32)

    for h in range(PEER_HEADS):
        s1, s2 = scores(2 * h), scores(2 * h + 1)
        vals1 = _top_desc(s1, PEER_TOPK)
        vals2 = _top_desc(s2, PEER_TOPK)
        e2_ref[h] = jnp.exp(s2 - vals2[0]).astype(BF16)
        sv1 = jnp.concatenate(vals1, axis=0)
        sv2 = jnp.concatenate(vals2, axis=0)
        pieces = [sv1[0:1, :] + sv2]
        for a in range(1, 8):
            nb = PEER_TOPK // (a + 1)
            pieces.append(jnp.where(row8 < nb, sv1[a:a + 1, :] + sv2[0:8, :], -jnp.inf))
        pieces.append(sv2[0:1, :] + sv1[8:16, :])
        cand = _top_desc(jnp.concatenate(pieces, axis=0), PEER_TOPK)
        top = cand[0]
        z = jnp.zeros_like(top)
        for r in range(PEER_TOPK):
            z = z + jnp.exp(cand[r] - top)
        kth = cand[PEER_TOPK - 1]
        counts = [jnp.sum(jnp.where(p >= kth, 1.0, 0.0), axis=0, keepdims=True) for p in pieces[:8]]
        last = PEER_TOPK - 1
        cnt = jnp.where(s1 >= sv1[last:last + 1, :], jnp.where(s1 + sv2[0:1, :] >= kth, 1.0, 0.0), 0.0)
        rank = jnp.where(s2 >= sv2[last:last + 1, :],
                         jnp.where(sv1[0:1, :] + s2 >= kth, float(SUBLANES), float(PEER_TOPK)), float(PEER_TOPK))
        for a in range(SUBLANES):
            cnt = jnp.where(s1 == sv1[a:a + 1, :], counts[a], cnt)
            rank = jnp.where(s2 == sv2[a:a + 1, :], float(a), rank)
        rank_ref[h] = rank.astype(BF16)
        e1 = jnp.exp(s1 - (sv1[0:1, :] + jnp.log(z))) * SQRT_HALF
        for c in range(tt // LANES):
            cnt_ref[h, c] = cnt[:, c * LANES:(c + 1) * LANES]
            e1_ref[h, c] = e1[:, c * LANES:(c + 1) * LANES]


def _peer_topk(xT, wqT, sk):
    spec = pl.BlockSpec((PEER_HEADS, N_KEYS, TT_TOPK), lambda t: (0, 0, t))
    row_spec = pl.BlockSpec((PEER_HEADS, TT_TOPK // LANES, N_KEYS, LANES), lambda t: (0, t, 0, 0))
    wide = jax.ShapeDtypeStruct((PEER_HEADS, N_TOK // LANES, N_KEYS, LANES), F32)
    narrow = jax.ShapeDtypeStruct((PEER_HEADS, N_KEYS, N_TOK), BF16)
    return pl.pallas_call(
        _peer_topk_kernel,
        grid=(N_TOK // TT_TOPK,),
        in_specs=[
            pl.BlockSpec((D_MODEL, TT_TOPK), lambda t: (0, t)),
            pl.BlockSpec((PEER_HEADS * PEER_DKEY, D_MODEL), lambda t: (0, 0)),
            pl.BlockSpec((2 * PEER_HEADS, N_KEYS, PEER_DHALF), lambda t: (0, 0, 0)),
        ],
        out_specs=[row_spec, row_spec, spec, spec],
        out_shape=[wide, wide, narrow, narrow],
        compiler_params=_params(("parallel",)),
        name="peer_topk",
    )(xT, wqT, sk)


SQRT_HALF = 2.0 ** -0.5


def _gelu_unscaled(x):
    t = x * SQRT_HALF
    return t * (1.0 + lax.erf(t))


LC_PEER = 256


def _bf16_rows(ref, h, r, chunks, n_rows):
    x = jnp.concatenate([ref[h, c, pl.ds(r, 2 * SUBLANES, stride=0), :] for c in chunks], axis=1)
    packed = x.astype(BF16)
    return jnp.concatenate([packed] * (n_rows // packed.shape[0]), axis=0)


MM_PIECE = 512


def _peer_main_kernel(xT_ref, u_ref, vt_ref, cnt_ref, e1_ref, rank_ref, e2_ref, x_ref, g_ref, b_ref,
                      o_ref, ob_ref, yT_ref, s_scr, wh_scr):
    i = pl.program_id(1)
    tt = xT_ref.shape[1]

    @pl.when(i == 0)
    def _():
        yT_ref[...] = jnp.zeros_like(yT_ref)

    zero = jnp.zeros((N_KEYS, LC_PEER), BF16)
    per_piece = MM_PIECE // N_KEYS
    for ii in range(IB_PEER):
        rows = slice(ii * N_KEYS, (ii + 1) * N_KEYS)
        if ii % per_piece == 0:
            piece = slice(ii * N_KEYS, ii * N_KEYS + MM_PIECE)
            s_scr[piece, :] = jnp.dot(u_ref[0, piece, :], xT_ref[...], preferred_element_type=F32)
        for lc in range(tt // LC_PEER):
            cols = slice(lc * LC_PEER, (lc + 1) * LC_PEER)
            chunks = range(lc * LC_PEER // LANES, (lc + 1) * LC_PEER // LANES)
            w = zero
            for h in range(PEER_HEADS):
                cnt = _bf16_rows(cnt_ref, h, ii, chunks, N_KEYS)
                e1 = _bf16_rows(e1_ref, h, ii, chunks, N_KEYS)
                w = w + jnp.where(rank_ref[h, :, cols] < cnt, e2_ref[h, :, cols] * e1, zero)
            wh_scr[rows, cols] = w
    for ii in range(IB_PEER):
        rows = slice(ii * N_KEYS, (ii + 1) * N_KEYS)
        wh_scr[rows, :] = wh_scr[rows, :] * _gelu_unscaled(s_scr[rows, :]).astype(BF16)
    yT_ref[...] += jnp.dot(vt_ref[0], wh_scr[...], preferred_element_type=F32)

    @pl.when(i == pl.num_programs(1) - 1)
    def _():
        y = _layer_norm(DN_ALPHA * x_ref[...] + yT_ref[...].T, g_ref[...], b_ref[...])
        o_ref[...] = y
        ob_ref[...] = y.astype(BF16)


def _peer_main(xT, u, vt, layer, cnt, e1, rank, e2, x1, g, b):
    eb = IB_PEER * N_KEYS
    row_spec = pl.BlockSpec((PEER_HEADS, TT_PEER // LANES, IB_PEER, LANES), lambda t, i: (0, t, i, 0))
    tab_spec = pl.BlockSpec((PEER_HEADS, N_KEYS, TT_PEER), lambda t, i: (0, 0, t))
    tok_spec = pl.BlockSpec((TT_PEER, D_MODEL), lambda t, i: (t, 0))
    vec = pl.BlockSpec((1, D_MODEL), lambda t, i: (0, 0))
    return pl.pallas_call(
        _peer_main_kernel,
        grid=(N_TOK // TT_PEER, N_EXPERTS // eb),
        in_specs=[
            pl.BlockSpec((D_MODEL, TT_PEER), lambda t, i: (0, t)),
            pl.BlockSpec((1, eb, D_MODEL), lambda t, i: (layer, i, 0)),
            pl.BlockSpec((1, D_MODEL, eb), lambda t, i: (layer, 0, i)),
            row_spec, row_spec, tab_spec, tab_spec, tok_spec, vec, vec,
        ],
        out_specs=[tok_spec, tok_spec],
        out_shape=[jax.ShapeDtypeStruct((N_TOK, D_MODEL), F32),
                   jax.ShapeDtypeStruct((N_TOK, D_MODEL), BF16)],
        scratch_shapes=[pltpu.VMEM((D_MODEL, TT_PEER), F32),
                        pltpu.VMEM((eb, TT_PEER), F32), pltpu.VMEM((eb, TT_PEER), BF16)],
        compiler_params=_params(("parallel", "arbitrary")),
        name="peer_main",
    )(xT, u, vt, cnt, e1, rank, e2, x1, g, b)


def _rope_tables():
    half = ROT_DIM // 2
    pos = jnp.concatenate([jnp.arange(SEQ), PAST_LEN + (jnp.arange(TM_QK) % DEC_SEQ)])
    inv = ROPE_THETA ** (-jnp.arange(half, dtype=F32) / half)
    ang = pos.astype(F32)[:, None] * inv[None, :]
    cos, sin = jnp.cos(ang), jnp.sin(ang)
    n = pos.shape[0]
    one = jnp.ones((n, A_HEAD_DIM - ROT_DIM), F32)
    zero = jnp.zeros((n, A_HEAD_DIM - ROT_DIM), F32)
    zh = jnp.zeros((n, half), F32)
    reps = LANES // A_HEAD_DIM
    c = jnp.tile(jnp.concatenate([cos, cos, one], 1), (1, reps))
    s1 = jnp.tile(jnp.concatenate([-sin, zh, zero], 1), (1, reps))
    s2 = jnp.tile(jnp.concatenate([zh, sin, zero], 1), (1, reps))
    return c, s1, s2


def _split_cols(w):
    cuts = [int(c) for c in np.cumsum(SPLITS)[:-1]]
    return jnp.split(w, cuts, axis=-1)


def _layer(layer, x, xb, k_state, v_state, gla_states, gla_out, pool_state, rope, w_in, b_in, sinks,
           w_alpha, b_alpha, gla_g, w_pool, pool_scale, w_a, w_b, w_c, w_out, ln1_g, ln1_b,
           peer_query, peer_subkeys, peer_u, peer_vt, ln2_g, ln2_b):
    qa_w, ka_w, va_w, qb_w, kb_w, vb_w, lr_w, gb_w, uc_w, gates_w = _split_cols(w_in)
    qa_b, ka_b, va_b, qb_b, kb_b, vb_b, lr_b, gb_b, uc_b, gates_b = _split_cols(b_in[None, :])
    lr_pad = LANES - B_GATE_RANK
    w_qk = jnp.concatenate([qa_w, ka_w], 1).astype(BF16)
    b_qk = jnp.concatenate([qa_b, ka_b], 1)
    w_rest = jnp.concatenate([gates_w, vb_w, gb_w, uc_w, qb_w, kb_w, va_w,
                              jnp.pad(lr_w, ((0, 0), (0, lr_pad)))], 1).astype(BF16)
    b_rest = jnp.concatenate([gates_b, vb_b, gb_b, uc_b, qb_b, kb_b, va_b,
                              jnp.pad(lr_b, ((0, 0), (0, lr_pad)))], 1)

    qk = _proj_qk(xb, w_qk, b_qk, *rope)
    rest = _proj_rest(xb, w_rest, b_rest)

    ks = k_state.reshape(DEC_BATCH, WINDOW, A_KV)
    vs = v_state.reshape(DEC_BATCH, WINDOW, A_KV)
    oa = (_attn_prompt(qk, rest, sinks), _attn_sample(qk, rest, sinks, ks, vs))

    wa = jnp.pad(w_alpha, ((0, lr_pad), (0, 0))).astype(BF16)
    ba = b_alpha[None, :]
    gla_p, gla_s = gla_out
    ob_p, gla_p = _gla(rest, jnp.zeros((1, BATCH, B_HEADS, B_DK, B_DV), F32), layer, gla_p, wa, ba, gla_g,
                       n_batch=BATCH, seq=SEQ, chunk=GLA_CHUNK, row_base=0, group=GLA_GROUP_PROMPT)
    ob_s, gla_s = _gla(rest, gla_states, layer, gla_s, wa, ba, gla_g, n_batch=DEC_BATCH, seq=DEC_SEQ,
                       chunk=math.gcd(DEC_SEQ, GLA_CHUNK), row_base=N_PROMPT, group=GLA_GROUP_SAMPLE)
    ob = (ob_p.reshape(N_PROMPT, B_V), ob_s.reshape(N_SAMPLE, B_V))

    wp = w_pool.astype(BF16)
    ps = pool_scale[None, :]
    prev = jnp.pad(pool_state, ((0, 0), (HALO - POOL_STATE, 0), (0, 0))).reshape(DEC_BATCH * HALO, C_WIDTH)
    oc = (_pool_prompt(rest, wp, ps), _pool_sample(rest, prev, wp, ps))

    x1, x1T = _merge(x, rest, (oa, ob, oc), w_a.astype(BF16), w_b.astype(BF16), w_c.astype(BF16),
                     w_out.astype(BF16), ln1_g[None, :], ln1_b[None, :])

    wqT = peer_query.reshape(D_MODEL, PEER_HEADS * PEER_DKEY).T.astype(BF16)
    sk = peer_subkeys.reshape(2 * PEER_HEADS, N_KEYS, PEER_DHALF).astype(BF16)
    cnt, e1, rank, e2 = _peer_topk(x1T, wqT, sk)
    x2, x2b = _peer_main(x1T, peer_u, peer_vt, layer, cnt, e1, rank, e2, x1, ln2_g[None, :], ln2_b[None, :])

    def prompt_tail(t, col0, width, n):
        return jnp.stack([t[(b + 1) * SEQ - n:(b + 1) * SEQ, col0:col0 + width] for b in range(BATCH)])

    def sample_tail(state, t, col0, width, n):
        new = t[N_PROMPT:, col0:col0 + width].reshape(DEC_BATCH, DEC_SEQ, width)
        return jnp.concatenate([state, new], 1)[:, -n:]

    kv_shape = (-1, WINDOW, A_KV_HEADS, A_HEAD_DIM)
    states = (prompt_tail(qk, A_Q, A_KV, WINDOW).reshape(kv_shape),
              prompt_tail(rest, R_VA, A_KV, WINDOW).reshape(kv_shape),
              prompt_tail(rest, R_UC, C_WIDTH, POOL_STATE),
              sample_tail(ks, qk, A_Q, A_KV, WINDOW).reshape(kv_shape),
              sample_tail(vs, rest, R_VA, A_KV, WINDOW).reshape(kv_shape),
              sample_tail(pool_state, rest, R_UC, C_WIDTH, POOL_STATE))
    return x2, x2b, states, (gla_p, gla_s)


def kernel(x_prompt, x_sample, state_win_k, state_win_v, state_gla, state_pool, w_in, b_in, attn_sinks,
           w_alpha, b_alpha, gla_norm_g, w_pool, pool_scale, w_branch_a, w_branch_b, w_branch_c, w_out,
           ln1_g, ln1_b, peer_query, peer_subkeys, peer_u, peer_v, ln2_g, ln2_b):
    x = jnp.concatenate([x_prompt.reshape(N_PROMPT, D_MODEL), x_sample.reshape(N_SAMPLE, D_MODEL)], 0)
    xb = x.astype(BF16)
    rope = _rope_tables()
    peer_ub = peer_u.astype(BF16)
    peer_vtb = jnp.swapaxes(peer_v, 1, 2).astype(BF16)
    per_layer = []
    gla_out = (None, None)
    for l in range(DEPTH):
        x, xb, states, gla_out = _layer(
            l, x, xb, state_win_k[l], state_win_v[l], state_gla, gla_out, state_pool[l], rope,
            w_in[l], b_in[l], attn_sinks[l], w_alpha[l], b_alpha[l], gla_norm_g[l],
            w_pool[l], pool_scale[l], w_branch_a[l], w_branch_b[l], w_branch_c[l], w_out[l],
            ln1_g[l], ln1_b[l], peer_query[l], peer_subkeys[l], peer_ub, peer_vtb, ln2_g[l], ln2_b[l])
        per_layer.append(states)
    pk, pv, pp, sk, sv, sp = [jnp.stack([per_layer[l][i] for l in range(DEPTH)]) for i in range(6)]
    return (x[:N_PROMPT].reshape(BATCH, SEQ, D_MODEL), x[N_PROMPT:].reshape(DEC_BATCH, DEC_SEQ, D_MODEL),
            pk, pv, gla_out[0], pp, sk, sv, gla_out[1], sp)
```

```python
import functools
import math

import jax
import jax.numpy as jnp
import numpy as np
from jax import lax
from jax.experimental import pallas as pl
from jax.experimental.pallas import tpu as pltpu

F32 = jnp.float32
BF16 = jnp.bfloat16

D_MODEL = 1024
BATCH = 8
SEQ = 2048
DEPTH = 2
DEC_BATCH = 128
DEC_SEQ = 8
PAST_LEN = 16384

A_HEADS = 16
A_KV_HEADS = 2
A_HEAD_DIM = 64
A_GROUP = A_HEADS // A_KV_HEADS
WINDOW = 128
ROT_DIM = A_HEAD_DIM // 4
ROPE_THETA = 500000.0
NEG_INF = -1e30
B_HEADS = 4
B_DK = D_MODEL // 2 // B_HEADS
B_DV = D_MODEL // B_HEADS
B_GATE_RANK = 16
B_TAU = 16.0
GLA_CHUNK = 64
POOL_WINDOWS = (2, 4, 8, 16)
C_GROUPS = len(POOL_WINDOWS)
C_GROUP_W = D_MODEL // C_GROUPS
C_WIDTH = C_GROUPS * C_GROUP_W
POOL_STATE = max(POOL_WINDOWS) - 1
PEER_HEADS = 8
N_KEYS = 128
N_EXPERTS = N_KEYS * N_KEYS
PEER_TOPK = 16
PEER_DKEY = 256
PEER_DHALF = PEER_DKEY // 2
DN_ALPHA = (2 * DEPTH) ** 0.25
LN_EPS = 1e-5
RMS_EPS = 1e-6

A_Q = A_HEADS * A_HEAD_DIM
A_KV = A_KV_HEADS * A_HEAD_DIM
B_QK = B_HEADS * B_DK
B_V = B_HEADS * B_DV
SPLITS = (A_Q, A_KV, A_KV, B_QK, B_QK, B_V, B_GATE_RANK, B_V, C_WIDTH, 3 * D_MODEL)

LANES = 128
SUBLANES = 8
N_PROMPT = BATCH * SEQ
N_SAMPLE = DEC_BATCH * DEC_SEQ
N_TOK = N_PROMPT + N_SAMPLE

R_GATES = 0
R_VB = 3 * D_MODEL
R_GB = R_VB + B_V
R_UC = R_GB + B_V
R_QB = R_UC + C_WIDTH
R_KB = R_QB + B_QK
R_VA = R_KB + B_QK
R_LR = R_VA + A_KV
R_WIDTH = R_LR + LANES
QK_WIDTH = A_Q + A_KV

VMEM_LIMIT = 48 * 1024 * 1024

TM_QK = 512
TM_REST = 512
TN_REST = R_WIDTH // 2
TM_MERGE = 256
TP_POOL = 512
GLA_GROUP_PROMPT = 4
GLA_GROUP_SAMPLE = 4
TT_TOPK = 512
TT_PEER = 512
IB_PEER = 16


def _params(sem):
    return pltpu.CompilerParams(dimension_semantics=sem, vmem_limit_bytes=VMEM_LIMIT)


def _qk_kernel(x_ref, w_ref, b_ref, c_ref, s1_ref, s2_ref, o_ref):
    y = jnp.dot(x_ref[...], w_ref[...], preferred_element_type=F32) + b_ref[...]
    c = c_ref[...]
    s1 = s1_ref[...]
    s2 = s2_ref[...]
    for j in range(QK_WIDTH // LANES):
        yj = y[:, j * LANES:(j + 1) * LANES]
        up = pltpu.roll(yj, LANES - ROT_DIM // 2, axis=1)
        dn = pltpu.roll(yj, ROT_DIM // 2, axis=1)
        o_ref[:, j * LANES:(j + 1) * LANES] = yj * c + up * s1 + dn * s2


def _proj_qk(xb, w, b, rope_c, rope_s1, rope_s2):
    n_prompt_blocks = SEQ // TM_QK

    def tab_map(i):
        return (jnp.where(i < N_PROMPT // TM_QK, i % n_prompt_blocks, n_prompt_blocks), 0)

    tab_spec = pl.BlockSpec((TM_QK, LANES), tab_map)
    return pl.pallas_call(
        _qk_kernel,
        grid=(N_TOK // TM_QK,),
        in_specs=[
            pl.BlockSpec((TM_QK, D_MODEL), lambda i: (i, 0)),
            pl.BlockSpec((D_MODEL, QK_WIDTH), lambda i: (0, 0)),
            pl.BlockSpec((1, QK_WIDTH), lambda i: (0, 0)),
            tab_spec, tab_spec, tab_spec,
        ],
        out_specs=pl.BlockSpec((TM_QK, QK_WIDTH), lambda i: (i, 0)),
        out_shape=jax.ShapeDtypeStruct((N_TOK, QK_WIDTH), F32),
        compiler_params=_params(("parallel",)),
        name="proj_qk",
    )(xb, w, b, rope_c, rope_s1, rope_s2)


def _mm_bias_kernel(x_ref, w_ref, b_ref, o_ref):
    o_ref[...] = jnp.dot(x_ref[...], w_ref[...], preferred_element_type=F32) + b_ref[...]


def _proj_rest(xb, w, b):
    return pl.pallas_call(
        _mm_bias_kernel,
        grid=(R_WIDTH // TN_REST, N_TOK // TM_REST),
        in_specs=[
            pl.BlockSpec((TM_REST, D_MODEL), lambda j, i: (i, 0)),
            pl.BlockSpec((D_MODEL, TN_REST), lambda j, i: (0, j)),
            pl.BlockSpec((1, TN_REST), lambda j, i: (0, j)),
        ],
        out_specs=pl.BlockSpec((TM_REST, TN_REST), lambda j, i: (i, j)),
        out_shape=jax.ShapeDtypeStruct((N_TOK, R_WIDTH), F32),
        compiler_params=_params(("parallel", "arbitrary")),
        name="proj_rest",
    )(xb, w, b)


HEADS_PER_PASS = 16


def _attend(q, kk, vv, sink_ref, c_min, o_ref):
    tq = q.shape[0]
    r = lax.broadcasted_iota(jnp.int32, (tq, 2 * WINDOW), 0)
    c = lax.broadcasted_iota(jnp.int32, (tq, 2 * WINDOW), 1)
    ok = (c > r) & (c <= r + WINDOW) & (c >= c_min)
    qb = (q * (A_HEAD_DIM ** -0.5)).astype(BF16)
    nt = (((1,), (1,)), ((), ()))
    for h0 in range(0, A_HEADS, HEADS_PER_PASS):
        hs = range(h0, h0 + HEADS_PER_PASS)
        col = {h: slice(h * A_HEAD_DIM, (h + 1) * A_HEAD_DIM) for h in hs}
        kv = {h: slice((h // A_GROUP) * A_HEAD_DIM, (h // A_GROUP + 1) * A_HEAD_DIM) for h in hs}
        s = {h: lax.dot_general(qb[:, col[h]], kk[:, kv[h]], nt, preferred_element_type=F32) for h in hs}
        s = {h: jnp.where(ok, s[h], NEG_INF) for h in hs}
        m = {h: jnp.maximum(jnp.max(s[h], axis=1, keepdims=True), sink_ref[h]) for h in hs}
        p = {h: jnp.exp(s[h] - m[h]) for h in hs}
        denom = {h: jnp.sum(p[h], axis=1, keepdims=True) + jnp.exp(sink_ref[h] - m[h]) for h in hs}
        o = {h: jnp.dot(p[h].astype(BF16), vv[:, kv[h]], preferred_element_type=F32) / denom[h] for h in hs}
        for h in hs:
            o_ref[:, col[h]] = o[h].astype(o_ref.dtype)


def _attn_prompt_kernel(sink_ref, q_ref, kc_ref, kp_ref, vc_ref, vp_ref, o_ref):
    n = pl.program_id(1)
    kk = jnp.concatenate([kp_ref[...], kc_ref[...]], axis=0).astype(BF16)
    vv = jnp.concatenate([vp_ref[...], vc_ref[...]], axis=0).astype(BF16)
    _attend(q_ref[...], kk, vv, sink_ref, jnp.where(n > 0, 0, WINDOW), o_ref)


def _attn_prompt(qk, rest, sinks):
    nb = SEQ // WINDOW
    kcol = A_Q // A_KV
    vcol = R_VA // A_KV

    def cur(b, n):
        return b * nb + n

    def prev(b, n):
        return b * nb + jnp.maximum(n - 1, 0)

    return pl.pallas_call(
        _attn_prompt_kernel,
        grid=(BATCH, nb),
        in_specs=[
            pl.BlockSpec(memory_space=pltpu.SMEM),
            pl.BlockSpec((WINDOW, A_Q), lambda b, n: (cur(b, n), 0)),
            pl.BlockSpec((WINDOW, A_KV), lambda b, n: (cur(b, n), kcol)),
            pl.BlockSpec((WINDOW, A_KV), lambda b, n: (prev(b, n), kcol)),
            pl.BlockSpec((WINDOW, A_KV), lambda b, n: (cur(b, n), vcol)),
            pl.BlockSpec((WINDOW, A_KV), lambda b, n: (prev(b, n), vcol)),
        ],
        out_specs=pl.BlockSpec((WINDOW, A_Q), lambda b, n: (cur(b, n), 0)),
        out_shape=jax.ShapeDtypeStruct((N_PROMPT, A_Q), BF16),
        compiler_params=_params(("parallel", "arbitrary")),
        name="attn_prompt",
    )(sinks, qk, qk, qk, rest, rest)


BB_ATTN = 8


def _attn_sample_kernel(sink_ref, q_ref, kn_ref, vn_ref, ks_ref, vs_ref, o_ref):
    pad = jnp.zeros((WINDOW - DEC_SEQ, A_KV), F32)
    rows = A_GROUP * DEC_SEQ
    t = lax.broadcasted_iota(jnp.int32, (rows, 2 * WINDOW), 0) % DEC_SEQ
    c = lax.broadcasted_iota(jnp.int32, (rows, 2 * WINDOW), 1)
    ok = (c > t) & (c <= t + WINDOW)
    nt = (((1,), (1,)), ((), ()))

    def body(pair, carry):
        chains = [(e, g) for e in range(2) for g in range(A_KV_HEADS)]
        elem = {e: pair * 2 + e for e in range(2)}
        row0 = {e: pl.multiple_of(elem[e] * DEC_SEQ, DEC_SEQ) for e in range(2)}
        q = {e: q_ref[pl.ds(row0[e], DEC_SEQ), :] * (A_HEAD_DIM ** -0.5) for e in range(2)}
        kk = {e: jnp.concatenate([ks_ref[elem[e]], kn_ref[pl.ds(row0[e], DEC_SEQ), :], pad], axis=0).astype(BF16)
              for e in range(2)}
        vv = {e: jnp.concatenate([vs_ref[elem[e]], vn_ref[pl.ds(row0[e], DEC_SEQ), :], pad], axis=0).astype(BF16)
              for e in range(2)}
        heads = {g: range(g * A_GROUP, (g + 1) * A_GROUP) for g in range(A_KV_HEADS)}
        ds = {g: slice(g * A_HEAD_DIM, (g + 1) * A_HEAD_DIM) for g in range(A_KV_HEADS)}
        sink = {g: sink_ref[g * rows:(g + 1) * rows, 0:1] for g in range(A_KV_HEADS)}
        qg = {(bb, g): jnp.concatenate([q[bb][:, h * A_HEAD_DIM:(h + 1) * A_HEAD_DIM] for h in heads[g]],
                                       axis=0).astype(BF16) for bb, g in chains}
        s = {(bb, g): lax.dot_general(qg[bb, g], kk[bb][:, ds[g]], nt, preferred_element_type=F32)
             for bb, g in chains}
        s = {ch: jnp.where(ok, s[ch], NEG_INF) for ch in chains}
        m = {(bb, g): jnp.maximum(jnp.max(s[bb, g], axis=1, keepdims=True), sink[g]) for bb, g in chains}
        p = {ch: jnp.exp(s[ch] - m[ch]) for ch in chains}
        denom = {(bb, g): jnp.sum(p[bb, g], axis=1, keepdims=True) + jnp.exp(sink[g] - m[bb, g])
                 for bb, g in chains}
        o = {(bb, g): jnp.dot(p[bb, g].astype(BF16), vv[bb][:, ds[g]], preferred_element_type=F32) / denom[bb, g]
             for bb, g in chains}
        for bb, g in chains:
            for k, h in enumerate(heads[g]):
                o_ref[pl.ds(row0[bb], DEC_SEQ), h * A_HEAD_DIM:(h + 1) * A_HEAD_DIM] = (
                    o[bb, g][k * DEC_SEQ:(k + 1) * DEC_SEQ, :].astype(o_ref.dtype))
        return carry

    lax.fori_loop(0, BB_ATTN // 2, body, 0)


def _attn_sample(qk, rest, sinks, k_state, v_state):
    rows = BB_ATTN * DEC_SEQ
    base = N_PROMPT // rows
    kcol = A_Q // A_KV
    vcol = R_VA // A_KV
    sink_rows = jnp.broadcast_to(jnp.repeat(sinks, DEC_SEQ)[:, None], (A_HEADS * DEC_SEQ, LANES))
    return pl.pallas_call(
        _attn_sample_kernel,
        grid=(DEC_BATCH // BB_ATTN,),
        in_specs=[
            pl.BlockSpec((A_HEADS * DEC_SEQ, LANES), lambda i: (0, 0)),
            pl.BlockSpec((rows, A_Q), lambda i: (base + i, 0)),
            pl.BlockSpec((rows, A_KV), lambda i: (base + i, kcol)),
            pl.BlockSpec((rows, A_KV), lambda i: (base + i, vcol)),
            pl.BlockSpec((BB_ATTN, WINDOW, A_KV), lambda i: (i, 0, 0)),
            pl.BlockSpec((BB_ATTN, WINDOW, A_KV), lambda i: (i, 0, 0)),
        ],
        out_specs=pl.BlockSpec((rows, A_Q), lambda i: (i, 0)),
        out_shape=jax.ShapeDtypeStruct((N_SAMPLE, A_Q), BF16),
        compiler_params=_params(("parallel",)),
        name="attn_sample",
    )(sink_rows, qk, qk, rest, k_state, v_state)


def _split3(x):
    hi = x.astype(BF16)
    r1 = x - hi.astype(F32)
    mid = r1.astype(BF16)
    lo = (r1 - mid.astype(F32)).astype(BF16)
    return hi, mid, lo


GLA_INPUTS = ((LANES, R_LR), (B_QK, R_QB), (B_QK, R_KB), (B_V, R_VB), (B_V, R_GB))


def _gla_kernel(*refs, n_chunks, group, chunk, layer, n_in):
    ins, rest_refs = refs[:n_in], refs[n_in:]
    per = n_in // len(GLA_INPUTS)
    s0_ref, wa_ref, ba_ref, g_ref = rest_refs[:4]
    prev_ref = rest_refs[4] if layer else None
    o_ref, sout_ref, st_ref = rest_refs[-3:]
    ci = pl.program_id(1)
    c = chunk

    def rows(inp, g):
        if per == 1:
            return ins[inp][g * c:(g + 1) * c, :]
        return ins[inp * per + g][...]

    single = n_chunks == 1
    if not single:
        @pl.when(ci == 0)
        def _():
            for g in range(group):
                for h in range(B_HEADS):
                    st_ref[g, h] = s0_ref[0, g, h].T

    ri = lax.broadcasted_iota(jnp.int32, (c, c), 0)
    cj = lax.broadcasted_iota(jnp.int32, (c, c), 1)
    causal = cj <= ri
    tri = jnp.where(causal, 1.0, 0.0).astype(BF16)
    nt = (((1,), (1,)), ((), ()))
    G = range(group)
    z = [jnp.dot(rows(0, g).astype(BF16), wa_ref[...], preferred_element_type=F32) + ba_ref[...] for g in G]
    log_a = [-(jnp.maximum(-z[g], 0.0) + jnp.log1p(jnp.exp(-jnp.abs(z[g])))) / B_TAU for g in G]
    parts = [_split3(log_a[g]) for g in G]
    b = [jnp.dot(tri, parts[g][0], preferred_element_type=F32)
         + jnp.dot(tri, parts[g][1], preferred_element_type=F32)
         + jnp.dot(tri, parts[g][2], preferred_element_type=F32) for g in G]
    bl = [b[g][c - 1:c, :] for g in G]
    qd = [(rows(1, g) * (B_DK ** -0.5) * jnp.exp(b[g])).astype(BF16) for g in G]
    kd = [(rows(2, g) * jnp.exp(-b[g])).astype(BF16) for g in G]
    kl = [rows(2, g) * jnp.exp(bl[g] - b[g]) for g in G]
    kl = [kl[g].T if single else kl[g].astype(BF16) for g in G]
    ebl = [jnp.exp(bl[g]) for g in G]
    for h in range(B_HEADS):
        ks = slice(h * B_DK, (h + 1) * B_DK)
        vs = slice(h * B_DV, (h + 1) * B_DV)
        vh = [rows(3, g)[:, vs] for g in G]
        if single:
            st = [s0_ref[0, g, h] for g in G]
            o = [jnp.dot(qd[g][:, ks], st[g].astype(BF16), preferred_element_type=F32) for g in G]
        else:
            st = [st_ref[g, h] for g in G]
            o = [lax.dot_general(qd[g][:, ks], st[g].astype(BF16), nt, preferred_element_type=F32) for g in G]
        att = [lax.dot_general(qd[g][:, ks], kd[g][:, ks], nt, preferred_element_type=F32) for g in G]
        att = [jnp.where(causal, att[g], 0.0).astype(BF16) for g in G]
        o = [o[g] + jnp.dot(att[g], vh[g].astype(BF16), preferred_element_type=F32) for g in G]
        for g in G:
            if single:
                decay = jnp.broadcast_to(ebl[g][:, ks], (SUBLANES, B_DK)).T[:, 0:1]
                for l in range(layer):
                    sout_ref[l, g, h] = prev_ref[l, g, h]
                sout_ref[layer, g, h] = st[g] * decay + jnp.dot(kl[g][ks, :], vh[g],
                                                                  preferred_element_type=F32)
            else:
                st_ref[g, h] = st[g] * ebl[g][:, ks] + jnp.dot(vh[g].T.astype(BF16), kl[g][:, ks],
                                                               preferred_element_type=F32)
        o = [o[g] * lax.rsqrt(jnp.mean(o[g] * o[g], axis=1, keepdims=True) + RMS_EPS) * g_ref[h:h + 1, :]
             for g in G]
        for g in G:
            gate = rows(4, g)[:, vs]
            o_ref[g, :, vs] = (o[g] * (gate / (1.0 + jnp.exp(-gate)))).astype(o_ref.dtype)

    if not single:
        @pl.when(ci == n_chunks - 1)
        def _():
            for g in range(group):
                for l in range(layer):
                    sout_ref[l, g] = prev_ref[l, g]
                for h in range(B_HEADS):
                    sout_ref[layer, g, h] = st_ref[g, h].T


def _gla(rest, s0, layer, prev_states, wa, ba, gain, *, n_batch, seq, chunk, row_base, group):
    n_chunks = seq // chunk
    contiguous = n_chunks == 1
    base = row_base // chunk

    def in_specs_for(width, col):
        if contiguous:
            return [pl.BlockSpec((group * chunk, width), lambda b, ci: (base // group + b, col // width))]
        return [pl.BlockSpec((chunk, width),
                             lambda b, ci, g=g: (base + (b * group + g) * n_chunks + ci, col // width))
                for g in range(group)]

    row_specs = [spec for width, col in GLA_INPUTS for spec in in_specs_for(width, col)]
    s0_layer = layer if s0.shape[0] > 1 else 0
    state_block = (group, B_HEADS, B_DK, B_DV)
    prev_specs = [pl.BlockSpec((layer,) + state_block, lambda b, ci: (0, b, 0, 0, 0))] if layer else []
    prev_args = [prev_states] if layer else []
    return pl.pallas_call(
        functools.partial(_gla_kernel, n_chunks=n_chunks, group=group, chunk=chunk, layer=layer,
                          n_in=len(row_specs)),
        grid=(n_batch // group, n_chunks),
        in_specs=row_specs + [
            pl.BlockSpec((1,) + state_block, lambda b, ci: (s0_layer, b, 0, 0, 0)),
            pl.BlockSpec((LANES, B_QK), lambda b, ci: (0, 0)),
            pl.BlockSpec((1, B_QK), lambda b, ci: (0, 0)),
            pl.BlockSpec((B_HEADS, B_DV), lambda b, ci: (0, 0)),
        ] + prev_specs,
        out_specs=[
            pl.BlockSpec((group, chunk, B_V), lambda b, ci: (b, ci, 0)),
            pl.BlockSpec((layer + 1,) + state_block, lambda b, ci: (0, b, 0, 0, 0)),
        ],
        out_shape=[
            jax.ShapeDtypeStruct((n_batch, seq, B_V), BF16),
            jax.ShapeDtypeStruct((layer + 1, n_batch, B_HEADS, B_DK, B_DV), F32),
        ],
        scratch_shapes=[pltpu.VMEM((group, B_HEADS, B_DV, B_DK), F32)],
        compiler_params=_params(("parallel", "arbitrary")),
        name="gla",
    )(*([rest] * len(row_specs)), s0, wa, ba, gain, *prev_args)


HALO = 16
BB_POOL = 16


def _pool_kernel(u_ref, prev_ref, w_ref, scale_ref, o_ref, *, from_start, n_seq):
    tp = u_ref.shape[0] // n_seq
    if from_start:
        ti = pl.program_id(1)
        t0 = ti * tp
    diffs = [[] for _ in POOL_WINDOWS]
    for sq in range(n_seq):
        u = u_ref[sq * tp:(sq + 1) * tp, :]
        prev = prev_ref[sq * HALO:(sq + 1) * HALO, :]
        if from_start:
            prev = jnp.where(ti > 0, prev, 0.0)
        full = jnp.concatenate([prev, u], axis=0)
        for g, w in enumerate(POOL_WINDOWS):
            cs = slice(g * C_GROUP_W, (g + 1) * C_GROUP_W)
            acc = full[:, cs]
            span = 1
            while span < w:
                acc = acc + pltpu.roll(acc, span, axis=0)
                span *= 2
            wsum = acc[HALO:, :]
            if from_start:
                t = t0 + lax.broadcasted_iota(jnp.int32, (tp, C_GROUP_W), 0)
                cnt = jnp.minimum(t + 1, w).astype(F32)
            else:
                cnt = float(w)
            diffs[g].append(wsum / cnt - u[:, cs])
    for g in range(C_GROUPS):
        cs = slice(g * C_GROUP_W, (g + 1) * C_GROUP_W)
        d = jnp.concatenate(diffs[g], axis=0) if n_seq > 1 else diffs[g][0]
        y = jnp.dot(d.astype(BF16), w_ref[g], preferred_element_type=F32) * scale_ref[:, cs]
        o_ref[:, cs] = y.astype(o_ref.dtype)


def _pool_prompt(rest, w, scale):
    nt = SEQ // TP_POOL
    ucol = R_UC // C_WIDTH

    def halo(b, i):
        return (jnp.maximum((b * SEQ + i * TP_POOL) // HALO - 1, 0), ucol)

    return pl.pallas_call(
        functools.partial(_pool_kernel, from_start=True, n_seq=1),
        grid=(BATCH, nt),
        in_specs=[
            pl.BlockSpec((TP_POOL, C_WIDTH), lambda b, i: (b * nt + i, ucol)),
            pl.BlockSpec((HALO, C_WIDTH), halo),
            pl.BlockSpec((C_GROUPS, C_GROUP_W, C_GROUP_W), lambda b, i: (0, 0, 0)),
            pl.BlockSpec((1, C_WIDTH), lambda b, i: (0, 0)),
        ],
        out_specs=pl.BlockSpec((TP_POOL, C_WIDTH), lambda b, i: (b * nt + i, 0)),
        out_shape=jax.ShapeDtypeStruct((N_PROMPT, C_WIDTH), BF16),
        compiler_params=_params(("parallel", "arbitrary")),
        name="pool_prompt",
    )(rest, rest, w, scale)


def _pool_sample(rest, prev, w, scale):
    ucol = R_UC // C_WIDTH
    rows = BB_POOL * DEC_SEQ
    base = N_PROMPT // rows
    return pl.pallas_call(
        functools.partial(_pool_kernel, from_start=False, n_seq=BB_POOL),
        grid=(DEC_BATCH // BB_POOL,),
        in_specs=[
            pl.BlockSpec((rows, C_WIDTH), lambda b: (base + b, ucol)),
            pl.BlockSpec((BB_POOL * HALO, C_WIDTH), lambda b: (b, 0)),
            pl.BlockSpec((C_GROUPS, C_GROUP_W, C_GROUP_W), lambda b: (0, 0, 0)),
            pl.BlockSpec((1, C_WIDTH), lambda b: (0, 0)),
        ],
        out_specs=pl.BlockSpec((rows, C_WIDTH), lambda b: (b, 0)),
        out_shape=jax.ShapeDtypeStruct((N_SAMPLE, C_WIDTH), BF16),
        compiler_params=_params(("parallel",)),
        name="pool_sample",
    )(rest, prev, w, scale)


def _layer_norm(x, g, b):
    mu = jnp.mean(x, axis=1, keepdims=True)
    xc = x - mu
    var = jnp.mean(xc * xc, axis=1, keepdims=True)
    return xc * lax.rsqrt(var + LN_EPS) * g + b


def _merge_kernel(x_ref, gates_ref, oa_p_ref, oa_s_ref, ob_p_ref, ob_s_ref, oc_p_ref, oc_s_ref,
                  wa_ref, wb_ref, wc_ref, wo_ref, g_ref, b_ref, o_ref, oT_ref):
    in_sample = pl.program_id(0) >= N_PROMPT // TM_MERGE

    def gate(i):
        z = gates_ref[:, i * D_MODEL:(i + 1) * D_MODEL]
        return 1.0 / (1.0 + jnp.exp(-z))

    def branch(p_ref, s_ref, w_ref):
        o = jnp.where(in_sample, s_ref[...], p_ref[...])
        return jnp.dot(o, w_ref[...], preferred_element_type=F32)

    merged = (gate(0) * branch(oa_p_ref, oa_s_ref, wa_ref)
              + gate(1) * branch(ob_p_ref, ob_s_ref, wb_ref)
              + gate(2) * branch(oc_p_ref, oc_s_ref, wc_ref))
    mix = jnp.dot(merged.astype(BF16), wo_ref[...], preferred_element_type=F32)
    y = _layer_norm(DN_ALPHA * x_ref[...] + mix, g_ref[...], b_ref[...])
    o_ref[...] = y
    oT_ref[...] = y.T.astype(BF16)


def _merge(x, rest, branches, wa, wb, wc, wo, g, b):
    n_p = N_PROMPT // TM_MERGE
    row = pl.BlockSpec((TM_MERGE, D_MODEL), lambda i: (i, 0))
    row_p = pl.BlockSpec((TM_MERGE, D_MODEL), lambda i: (jnp.minimum(i, n_p - 1), 0))
    row_s = pl.BlockSpec((TM_MERGE, D_MODEL), lambda i: (jnp.maximum(i - n_p, 0), 0))
    wspec = pl.BlockSpec((D_MODEL, D_MODEL), lambda i: (0, 0))
    vec = pl.BlockSpec((1, D_MODEL), lambda i: (0, 0))
    (oa_p, oa_s), (ob_p, ob_s), (oc_p, oc_s) = branches
    return pl.pallas_call(
        _merge_kernel,
        grid=(N_TOK // TM_MERGE,),
        in_specs=[row, pl.BlockSpec((TM_MERGE, 3 * D_MODEL), lambda i: (i, R_GATES)),
                  row_p, row_s, row_p, row_s, row_p, row_s, wspec, wspec, wspec, wspec, vec, vec],
        out_specs=[row, pl.BlockSpec((D_MODEL, TM_MERGE), lambda i: (0, i))],
        out_shape=[jax.ShapeDtypeStruct((N_TOK, D_MODEL), F32),
                   jax.ShapeDtypeStruct((D_MODEL, N_TOK), BF16)],
        compiler_params=_params(("parallel",)),
        name="merge",
    )(x, rest, oa_p, oa_s, ob_p, ob_s, oc_p, oc_s, wa, wb, wc, wo, g, b)


def _merge_sort_pairs(n):
    size = 1
    while size < n:
        size *= 2
    pairs = []
    p = 1
    while p < size:
        k = p
        while k >= 1:
            for j in range(k % p, size - k, 2 * k):
                for i in range(min(k, size - j - k)):
                    if (i + j) // (2 * p) == (i + j + k) // (2 * p):
                        pairs.append((i + j, i + j + k))
            k //= 2
        p *= 2
    return [(a, b) for a, b in pairs if b < n]


def _top_desc(s, n):
    v = [s[SUBLANES * k:SUBLANES * (k + 1), :] for k in range(s.shape[0] // SUBLANES)]
    depth = len(v)
    for a, b in _merge_sort_pairs(depth):
        v[a], v[b] = jnp.maximum(v[a], v[b]), jnp.minimum(v[a], v[b])
    vals = []
    for r in range(n):
        m = jnp.max(v[0], axis=0, keepdims=True)
        vals.append(m)
        hit = v[0] == m
        needed = n - r - 1
        for k in range(min(depth - 1, needed)):
            v[k] = jnp.where(hit, v[k + 1], v[k])
        if needed >= depth:
            v[depth - 1] = jnp.where(hit, -jnp.inf, v[depth - 1])
    return vals


def _peer_topk_kernel(xT_ref, wq_ref, sk_ref, cnt_ref, e1_ref, rank_ref, e2_ref):
    tt = xT_ref.shape[1]
    row8 = lax.broadcasted_iota(jnp.int32, (SUBLANES, tt), 0)

    def scores(hp):
        qT = jnp.dot(wq_ref[hp * PEER_DHALF:(hp + 1) * PEER_DHALF, :], xT_ref[...], preferred_element_type=F32)
        return jnp.dot(sk_ref[hp], qT.astype(BF16), preferred_element_type=F32)

    for h in range(PEER_HEADS):
        s1, s2 = scores(2 * h), scores(2 * h + 1)
        vals1 = _top_desc(s1, PEER_TOPK)
        vals2 = _top_desc(s2, PEER_TOPK)
        e2_ref[h] = jnp.exp(s2 - vals2[0]).astype(BF16)
        sv1 = jnp.concatenate(vals1, axis=0)
        sv2 = jnp.concatenate(vals2, axis=0)
        pieces = [sv1[0:1, :] + sv2]
        for a in range(1, SUBLANES):
            nb = PEER_TOPK // (a + 1)
            pieces.append(jnp.where(row8 < nb, sv1[a:a + 1, :] + sv2[0:SUBLANES, :], -jnp.inf))
        pieces.append(sv2[0:1, :] + sv1[SUBLANES:PEER_TOPK, :])
        cand = _top_desc(jnp.concatenate(pieces, axis=0), PEER_TOPK)
        top = cand[0]
        z = jnp.zeros_like(top)
        for r in range(PEER_TOPK):
            z = z + jnp.exp(cand[r] - top)
        kth = cand[PEER_TOPK - 1]
        counts = [jnp.sum(jnp.where(p >= kth, 1.0, 0.0), axis=0, keepdims=True) for p in pieces[:SUBLANES]]
        last = PEER_TOPK - 1
        cnt = jnp.where(s1 >= sv1[last:last + 1, :], jnp.where(s1 + sv2[0:1, :] >= kth, 1.0, 0.0), 0.0)
        rank = jnp.where(s2 >= sv2[last:last + 1, :],
                         jnp.where(sv1[0:1, :] + s2 >= kth, float(SUBLANES), float(PEER_TOPK)), float(PEER_TOPK))
        for a in range(SUBLANES):
            cnt = jnp.where(s1 == sv1[a:a + 1, :], counts[a], cnt)
            rank = jnp.where(s2 == sv2[a:a + 1, :], float(a), rank)
        rank_ref[h] = rank.astype(BF16)
        e1 = jnp.exp(s1 - (sv1[0:1, :] + jnp.log(z))) * SQRT_HALF
        for c in range(tt // LANES):
            cnt_ref[h, c] = cnt[:, c * LANES:(c + 1) * LANES]
            e1_ref[h, c] = e1[:, c * LANES:(c + 1) * LANES]


def _peer_topk(xT, wqT, sk):
    spec = pl.BlockSpec((PEER_HEADS, N_KEYS, TT_TOPK), lambda t: (0, 0, t))
    row_spec = pl.BlockSpec((PEER_HEADS, TT_TOPK // LANES, N_KEYS, LANES), lambda t: (0, t, 0, 0))
    wide = jax.ShapeDtypeStruct((PEER_HEADS, N_TOK // LANES, N_KEYS, LANES), F32)
    narrow = jax.ShapeDtypeStruct((PEER_HEADS, N_KEYS, N_TOK), BF16)
    return pl.pallas_call(
        _peer_topk_kernel,
        grid=(N_TOK // TT_TOPK,),
        in_specs=[
            pl.BlockSpec((D_MODEL, TT_TOPK), lambda t: (0, t)),
            pl.BlockSpec((PEER_HEADS * PEER_DKEY, D_MODEL), lambda t: (0, 0)),
            pl.BlockSpec((2 * PEER_HEADS, N_KEYS, PEER_DHALF), lambda t: (0, 0, 0)),
        ],
        out_specs=[row_spec, row_spec, spec, spec],
        out_shape=[wide, wide, narrow, narrow],
        compiler_params=_params(("parallel",)),
        name="peer_topk",
    )(xT, wqT, sk)


SQRT_HALF = 2.0 ** -0.5


def _gelu_unscaled(x):
    t = x * SQRT_HALF
    return t * (1.0 + lax.erf(t))


LC_PEER = 256


def _bf16_rows(ref, h, r, chunks, n_rows):
    x = jnp.concatenate([ref[h, c, pl.ds(r, 2 * SUBLANES, stride=0), :] for c in chunks], axis=1)
    packed = x.astype(BF16)
    return jnp.concatenate([packed] * (n_rows // packed.shape[0]), axis=0)


MM_PIECE = 512


def _peer_main_kernel(xT_ref, u_ref, vt_ref, cnt_ref, e1_ref, rank_ref, e2_ref, x_ref, g_ref, b_ref,
                      o_ref, ob_ref, yT_ref, s_scr, wh_scr):
    i = pl.program_id(1)
    tt = xT_ref.shape[1]

    @pl.when(i == 0)
    def _():
        yT_ref[...] = jnp.zeros_like(yT_ref)

    zero = jnp.zeros((N_KEYS, LC_PEER), BF16)
    per_piece = MM_PIECE // N_KEYS
    for ii in range(IB_PEER):
        rows = slice(ii * N_KEYS, (ii + 1) * N_KEYS)
        if ii % per_piece == 0:
            piece = slice(ii * N_KEYS, ii * N_KEYS + MM_PIECE)
            s_scr[piece, :] = jnp.dot(u_ref[0, piece, :], xT_ref[...], preferred_element_type=F32)
        for lc in range(tt // LC_PEER):
            cols = slice(lc * LC_PEER, (lc + 1) * LC_PEER)
            chunks = range(lc * LC_PEER // LANES, (lc + 1) * LC_PEER // LANES)
            w = zero
            for h in range(PEER_HEADS):
                cnt = _bf16_rows(cnt_ref, h, ii, chunks, N_KEYS)
                e1 = _bf16_rows(e1_ref, h, ii, chunks, N_KEYS)
                w = w + jnp.where(rank_ref[h, :, cols] < cnt, e2_ref[h, :, cols] * e1, zero)
            wh_scr[rows, cols] = w
    for ii in range(IB_PEER):
        rows = slice(ii * N_KEYS, (ii + 1) * N_KEYS)
        wh_scr[rows, :] = wh_scr[rows, :] * _gelu_unscaled(s_scr[rows, :]).astype(BF16)
    yT_ref[...] += jnp.dot(vt_ref[0], wh_scr[...], preferred_element_type=F32)

    @pl.when(i == pl.num_programs(1) - 1)
    def _():
        y = _layer_norm(DN_ALPHA * x_ref[...] + yT_ref[...].T, g_ref[...], b_ref[...])
        o_ref[...] = y
        ob_ref[...] = y.astype(BF16)


def _peer_main(xT, u, vt, layer, cnt, e1, rank, e2, x1, g, b):
    eb = IB_PEER * N_KEYS
    row_spec = pl.BlockSpec((PEER_HEADS, TT_PEER // LANES, IB_PEER, LANES), lambda t, i: (0, t, i, 0))
    tab_spec = pl.BlockSpec((PEER_HEADS, N_KEYS, TT_PEER), lambda t, i: (0, 0, t))
    tok_spec = pl.BlockSpec((TT_PEER, D_MODEL), lambda t, i: (t, 0))
    vec = pl.BlockSpec((1, D_MODEL), lambda t, i: (0, 0))
    return pl.pallas_call(
        _peer_main_kernel,
        grid=(N_TOK // TT_PEER, N_EXPERTS // eb),
        in_specs=[
            pl.BlockSpec((D_MODEL, TT_PEER), lambda t, i: (0, t)),
            pl.BlockSpec((1, eb, D_MODEL), lambda t, i: (layer, i, 0)),
            pl.BlockSpec((1, D_MODEL, eb), lambda t, i: (layer, 0, i)),
            row_spec, row_spec, tab_spec, tab_spec, tok_spec, vec, vec,
        ],
        out_specs=[tok_spec, tok_spec],
        out_shape=[jax.ShapeDtypeStruct((N_TOK, D_MODEL), F32),
                   jax.ShapeDtypeStruct((N_TOK, D_MODEL), BF16)],
        scratch_shapes=[pltpu.VMEM((D_MODEL, TT_PEER), F32),
                        pltpu.VMEM((eb, TT_PEER), F32), pltpu.VMEM((eb, TT_PEER), BF16)],
        compiler_params=_params(("parallel", "arbitrary")),
        name="peer_main",
    )(xT, u, vt, cnt, e1, rank, e2, x1, g, b)


def _rope_tables():
    half = ROT_DIM // 2
    pos = jnp.concatenate([jnp.arange(SEQ), PAST_LEN + (jnp.arange(TM_QK) % DEC_SEQ)])
    inv = ROPE_THETA ** (-jnp.arange(half, dtype=F32) / half)
    ang = pos.astype(F32)[:, None] * inv[None, :]
    cos, sin = jnp.cos(ang), jnp.sin(ang)
    n = pos.shape[0]
    one = jnp.ones((n, A_HEAD_DIM - ROT_DIM), F32)
    zero = jnp.zeros((n, A_HEAD_DIM - ROT_DIM), F32)
    zh = jnp.zeros((n, half), F32)
    reps = LANES // A_HEAD_DIM
    c = jnp.tile(jnp.concatenate([cos, cos, one], 1), (1, reps))
    s1 = jnp.tile(jnp.concatenate([-sin, zh, zero], 1), (1, reps))
    s2 = jnp.tile(jnp.concatenate([zh, sin, zero], 1), (1, reps))
    return c, s1, s2


def _split_cols(w):
    cuts = [int(c) for c in np.cumsum(SPLITS)[:-1]]
    return jnp.split(w, cuts, axis=-1)


def _layer(layer, x, xb, k_state, v_state, gla_states, gla_out, pool_state, rope, w_in, b_in, sinks,
           w_alpha, b_alpha, gla_g, w_pool, pool_scale, w_a, w_b, w_c, w_out, ln1_g, ln1_b,
           peer_query, peer_subkeys, peer_u, peer_vt, ln2_g, ln2_b):
    qa_w, ka_w, va_w, qb_w, kb_w, vb_w, lr_w, gb_w, uc_w, gates_w = _split_cols(w_in)
    qa_b, ka_b, va_b, qb_b, kb_b, vb_b, lr_b, gb_b, uc_b, gates_b = _split_cols(b_in[None, :])
    lr_pad = LANES - B_GATE_RANK
    w_qk = jnp.concatenate([qa_w, ka_w], 1).astype(BF16)
    b_qk = jnp.concatenate([qa_b, ka_b], 1)
    w_rest = jnp.concatenate([gates_w, vb_w, gb_w, uc_w, qb_w, kb_w, va_w,
                              jnp.pad(lr_w, ((0, 0), (0, lr_pad)))], 1).astype(BF16)
    b_rest = jnp.concatenate([gates_b, vb_b, gb_b, uc_b, qb_b, kb_b, va_b,
                              jnp.pad(lr_b, ((0, 0), (0, lr_pad)))], 1)

    qk = _proj_qk(xb, w_qk, b_qk, *rope)
    rest = _proj_rest(xb, w_rest, b_rest)

    ks = k_state.reshape(DEC_BATCH, WINDOW, A_KV)
    vs = v_state.reshape(DEC_BATCH, WINDOW, A_KV)
    oa = (_attn_prompt(qk, rest, sinks), _attn_sample(qk, rest, sinks, ks, vs))

    wa = jnp.pad(w_alpha, ((0, lr_pad), (0, 0))).astype(BF16)
    ba = b_alpha[None, :]
    gla_p, gla_s = gla_out
    ob_p, gla_p = _gla(rest, jnp.zeros((1, BATCH, B_HEADS, B_DK, B_DV), F32), layer, gla_p, wa, ba, gla_g,
                       n_batch=BATCH, seq=SEQ, chunk=GLA_CHUNK, row_base=0, group=GLA_GROUP_PROMPT)
    ob_s, gla_s = _gla(rest, gla_states, layer, gla_s, wa, ba, gla_g, n_batch=DEC_BATCH, seq=DEC_SEQ,
                       chunk=math.gcd(DEC_SEQ, GLA_CHUNK), row_base=N_PROMPT, group=GLA_GROUP_SAMPLE)
    ob = (ob_p.reshape(N_PROMPT, B_V), ob_s.reshape(N_SAMPLE, B_V))

    wp = w_pool.astype(BF16)
    ps = pool_scale[None, :]
    prev = jnp.pad(pool_state, ((0, 0), (HALO - POOL_STATE, 0), (0, 0))).reshape(DEC_BATCH * HALO, C_WIDTH)
    oc = (_pool_prompt(rest, wp, ps), _pool_sample(rest, prev, wp, ps))

    x1, x1T = _merge(x, rest, (oa, ob, oc), w_a.astype(BF16), w_b.astype(BF16), w_c.astype(BF16),
                     w_out.astype(BF16), ln1_g[None, :], ln1_b[None, :])

    wqT = peer_query.reshape(D_MODEL, PEER_HEADS * PEER_DKEY).T.astype(BF16)
    sk = peer_subkeys.reshape(2 * PEER_HEADS, N_KEYS, PEER_DHALF).astype(BF16)
    cnt, e1, rank, e2 = _peer_topk(x1T, wqT, sk)
    x2, x2b = _peer_main(x1T, peer_u, peer_vt, layer, cnt, e1, rank, e2, x1, ln2_g[None, :], ln2_b[None, :])

    def prompt_tail(t, col0, width, n):
        return jnp.stack([t[(b + 1) * SEQ - n:(b + 1) * SEQ, col0:col0 + width] for b in range(BATCH)])

    def sample_tail(state, t, col0, width, n):
        new = t[N_PROMPT:, col0:col0 + width].reshape(DEC_BATCH, DEC_SEQ, width)
        return jnp.concatenate([state, new], 1)[:, -n:]

    kv_shape = (-1, WINDOW, A_KV_HEADS, A_HEAD_DIM)
    states = (prompt_tail(qk, A_Q, A_KV, WINDOW).reshape(kv_shape),
              prompt_tail(rest, R_VA, A_KV, WINDOW).reshape(kv_shape),
              prompt_tail(rest, R_UC, C_WIDTH, POOL_STATE),
              sample_tail(ks, qk, A_Q, A_KV, WINDOW).reshape(kv_shape),
              sample_tail(vs, rest, R_VA, A_KV, WINDOW).reshape(kv_shape),
              sample_tail(pool_state, rest, R_UC, C_WIDTH, POOL_STATE))
    return x2, x2b, states, (gla_p, gla_s)


def kernel(x_prompt, x_sample, state_win_k, state_win_v, state_gla, state_pool, w_in, b_in, attn_sinks,
           w_alpha, b_alpha, gla_norm_g, w_pool, pool_scale, w_branch_a, w_branch_b, w_branch_c, w_out,
           ln1_g, ln1_b, peer_query, peer_subkeys, peer_u, peer_v, ln2_g, ln2_b):
    x = jnp.concatenate([x_prompt.reshape(N_PROMPT, D_MODEL), x_sample.reshape(N_SAMPLE, D_MODEL)], 0)
    xb = x.astype(BF16)
    rope = _rope_tables()
    peer_ub = peer_u.astype(BF16)
    peer_vtb = jnp.swapaxes(peer_v, 1, 2).astype(BF16)
    per_layer = []
    gla_out = (None, None)
    for l in range(DEPTH):
        x, xb, states, gla_out = _layer(
            l, x, xb, state_win_k[l], state_win_v[l], state_gla, gla_out, state_pool[l], rope,
            w_in[l], b_in[l], attn_sinks[l], w_alpha[l], b_alpha[l], gla_norm_g[l],
            w_pool[l], pool_scale[l], w_branch_a[l], w_branch_b[l], w_branch_c[l], w_out[l],
            ln1_g[l], ln1_b[l], peer_query[l], peer_subkeys[l], peer_ub, peer_vtb, ln2_g[l], ln2_b[l])
        per_layer.append(states)
    pk, pv, pp, sk, sv, sp = [jnp.stack([per_layer[l][i] for l in range(DEPTH)]) for i in range(6)]
    return (x[:N_PROMPT].reshape(BATCH, SEQ, D_MODEL), x[N_PROMPT:].reshape(DEC_BATCH, DEC_SEQ, D_MODEL),
            pk, pv, gla_out[0], pp, sk, sv, gla_out[1], sp)
```

```python
import functools
import math

import jax
import jax.numpy as jnp
import numpy as np
from jax import lax
from jax.experimental import pallas as pl
from jax.experimental.pallas import tpu as pltpu

F32 = jnp.float32
BF16 = jnp.bfloat16

D_MODEL = 1024
BATCH = 8
SEQ = 2048
DEPTH = 2
DEC_BATCH = 128
DEC_SEQ = 8
PAST_LEN = 16384

A_HEADS = 16
A_KV_HEADS = 2
A_HEAD_DIM = 64
A_GROUP = A_HEADS // A_KV_HEADS
WINDOW = 128
ROT_DIM = A_HEAD_DIM // 4
ROPE_THETA = 500000.0
NEG_INF = -1e30
B_HEADS = 4
B_DK = D_MODEL // 2 // B_HEADS
B_DV = D_MODEL // B_HEADS
B_GATE_RANK = 16
B_TAU = 16.0
GLA_CHUNK = 64
POOL_WINDOWS = (2, 4, 8, 16)
C_GROUPS = len(POOL_WINDOWS)
C_GROUP_W = D_MODEL // C_GROUPS
C_WIDTH = C_GROUPS * C_GROUP_W
POOL_STATE = max(POOL_WINDOWS) - 1
PEER_HEADS = 8
N_KEYS = 128
N_EXPERTS = N_KEYS * N_KEYS
PEER_TOPK = 16
PEER_DKEY = 256
PEER_DHALF = PEER_DKEY // 2
DN_ALPHA = (2 * DEPTH) ** 0.25
LN_EPS = 1e-5
RMS_EPS = 1e-6

A_Q = A_HEADS * A_HEAD_DIM
A_KV = A_KV_HEADS * A_HEAD_DIM
B_QK = B_HEADS * B_DK
B_V = B_HEADS * B_DV
SPLITS = (A_Q, A_KV, A_KV, B_QK, B_QK, B_V, B_GATE_RANK, B_V, C_WIDTH, 3 * D_MODEL)

LANES = 128
SUBLANES = 8
N_PROMPT = BATCH * SEQ
N_SAMPLE = DEC_BATCH * DEC_SEQ
N_TOK = N_PROMPT + N_SAMPLE

R_GATES = 0
R_VB = 3 * D_MODEL
R_GB = R_VB + B_V
R_UC = R_GB + B_V
R_QB = R_UC + C_WIDTH
R_KB = R_QB + B_QK
R_VA = R_KB + B_QK
R_LR = R_VA + A_KV
R_WIDTH = R_LR + LANES
QK_WIDTH = A_Q + A_KV

VMEM_LIMIT = 48 * 1024 * 1024

TM_QK = 512
TM_REST = 512
TN_REST = R_WIDTH // 2
TM_MERGE = 256
TP_POOL = 512
GLA_GROUP_PROMPT = 4
GLA_GROUP_SAMPLE = 4
TT_TOPK = 512
TT_PEER = 512
IB_PEER = 16


def _params(sem):
    return pltpu.CompilerParams(dimension_semantics=sem, vmem_limit_bytes=VMEM_LIMIT)


def _qk_kernel(x_ref, w_ref, b_ref, c_ref, s1_ref, s2_ref, o_ref):
    y = jnp.dot(x_ref[...], w_ref[...], preferred_element_type=F32) + b_ref[...]
    c = c_ref[...]
    s1 = s1_ref[...]
    s2 = s2_ref[...]
    for j in range(QK_WIDTH // LANES):
        yj = y[:, j * LANES:(j + 1) * LANES]
        up = pltpu.roll(yj, LANES - ROT_DIM // 2, axis=1)
        dn = pltpu.roll(yj, ROT_DIM // 2, axis=1)
        o_ref[:, j * LANES:(j + 1) * LANES] = yj * c + up * s1 + dn * s2


def _proj_qk(xb, w, b, rope_c, rope_s1, rope_s2):
    n_prompt_blocks = SEQ // TM_QK

    def tab_map(i):
        return (jnp.where(i < N_PROMPT // TM_QK, i % n_prompt_blocks, n_prompt_blocks), 0)

    tab_spec = pl.BlockSpec((TM_QK, LANES), tab_map)
    return pl.pallas_call(
        _qk_kernel,
        grid=(N_TOK // TM_QK,),
        in_specs=[
            pl.BlockSpec((TM_QK, D_MODEL), lambda i: (i, 0)),
            pl.BlockSpec((D_MODEL, QK_WIDTH), lambda i: (0, 0)),
            pl.BlockSpec((1, QK_WIDTH), lambda i: (0, 0)),
            tab_spec, tab_spec, tab_spec,
        ],
        out_specs=pl.BlockSpec((TM_QK, QK_WIDTH), lambda i: (i, 0)),
        out_shape=jax.ShapeDtypeStruct((N_TOK, QK_WIDTH), F32),
        compiler_params=_params(("parallel",)),
        name="proj_qk",
    )(xb, w, b, rope_c, rope_s1, rope_s2)


def _mm_bias_kernel(x_ref, w_ref, b_ref, o_ref):
    o_ref[...] = jnp.dot(x_ref[...], w_ref[...], preferred_element_type=F32) + b_ref[...]


def _proj_rest(xb, w, b):
    return pl.pallas_call(
        _mm_bias_kernel,
        grid=(R_WIDTH // TN_REST, N_TOK // TM_REST),
        in_specs=[
            pl.BlockSpec((TM_REST, D_MODEL), lambda j, i: (i, 0)),
            pl.BlockSpec((D_MODEL, TN_REST), lambda j, i: (0, j)),
            pl.BlockSpec((1, TN_REST), lambda j, i: (0, j)),
        ],
        out_specs=pl.BlockSpec((TM_REST, TN_REST), lambda j, i: (i, j)),
        out_shape=jax.ShapeDtypeStruct((N_TOK, R_WIDTH), F32),
        compiler_params=_params(("parallel", "arbitrary")),
        name="proj_rest",
    )(xb, w, b)


HEADS_PER_PASS = 16


def _attend(q, kk, vv, sink_ref, c_min, o_ref):
    tq = q.shape[0]
    r = lax.broadcasted_iota(jnp.int32, (tq, 2 * WINDOW), 0)
    c = lax.broadcasted_iota(jnp.int32, (tq, 2 * WINDOW), 1)
    ok = (c > r) & (c <= r + WINDOW) & (c >= c_min)
    qb = (q * (A_HEAD_DIM ** -0.5)).astype(BF16)
    nt = (((1,), (1,)), ((), ()))
    for h0 in range(0, A_HEADS, HEADS_PER_PASS):
        hs = range(h0, h0 + HEADS_PER_PASS)
        col = {h: slice(h * A_HEAD_DIM, (h + 1) * A_HEAD_DIM) for h in hs}
        kv = {h: slice((h // A_GROUP) * A_HEAD_DIM, (h // A_GROUP + 1) * A_HEAD_DIM) for h in hs}
        s = {h: lax.dot_general(qb[:, col[h]], kk[:, kv[h]], nt, preferred_element_type=F32) for h in hs}
        s = {h: jnp.where(ok, s[h], NEG_INF) for h in hs}
        m = {h: jnp.maximum(jnp.max(s[h], axis=1, keepdims=True), sink_ref[h]) for h in hs}
        p = {h: jnp.exp(s[h] - m[h]) for h in hs}
        denom = {h: jnp.sum(p[h], axis=1, keepdims=True) + jnp.exp(sink_ref[h] - m[h]) for h in hs}
        o = {h: jnp.dot(p[h].astype(BF16), vv[:, kv[h]], preferred_element_type=F32) / denom[h] for h in hs}
        for h in hs:
            o_ref[:, col[h]] = o[h].astype(o_ref.dtype)


def _attn_prompt_kernel(sink_ref, q_ref, kc_ref, kp_ref, vc_ref, vp_ref, o_ref):
    n = pl.program_id(1)
    kk = jnp.concatenate([kp_ref[...], kc_ref[...]], axis=0).astype(BF16)
    vv = jnp.concatenate([vp_ref[...], vc_ref[...]], axis=0).astype(BF16)
    _attend(q_ref[...], kk, vv, sink_ref, jnp.where(n > 0, 0, WINDOW), o_ref)


def _attn_prompt(qk, rest, sinks):
    nb = SEQ // WINDOW
    kcol = A_Q // A_KV
    vcol = R_VA // A_KV

    def cur(b, n):
        return b * nb + n

    def prev(b, n):
        return b * nb + jnp.maximum(n - 1, 0)

    return pl.pallas_call(
        _attn_prompt_kernel,
        grid=(BATCH, nb),
        in_specs=[
            pl.BlockSpec(memory_space=pltpu.SMEM),
            pl.BlockSpec((WINDOW, A_Q), lambda b, n: (cur(b, n), 0)),
            pl.BlockSpec((WINDOW, A_KV), lambda b, n: (cur(b, n), kcol)),
            pl.BlockSpec((WINDOW, A_KV), lambda b, n: (prev(b, n), kcol)),
            pl.BlockSpec((WINDOW, A_KV), lambda b, n: (cur(b, n), vcol)),
            pl.BlockSpec((WINDOW, A_KV), lambda b, n: (prev(b, n), vcol)),
        ],
        out_specs=pl.BlockSpec((WINDOW, A_Q), lambda b, n: (cur(b, n), 0)),
        out_shape=jax.ShapeDtypeStruct((N_PROMPT, A_Q), BF16),
        compiler_params=_params(("parallel", "arbitrary")),
        name="attn_prompt",
    )(sinks, qk, qk, qk, rest, rest)


BB_ATTN = 8


def _attn_sample_kernel(sink_ref, q_ref, kn_ref, vn_ref, ks_ref, vs_ref, o_ref):
    pad = jnp.zeros((WINDOW - DEC_SEQ, A_KV), F32)
    rows = A_GROUP * DEC_SEQ
    t = lax.broadcasted_iota(jnp.int32, (rows, 2 * WINDOW), 0) % DEC_SEQ
    c = lax.broadcasted_iota(jnp.int32, (rows, 2 * WINDOW), 1)
    ok = (c > t) & (c <= t + WINDOW)
    nt = (((1,), (1,)), ((), ()))

    def body(pair, carry):
        chains = [(e, g) for e in range(2) for g in range(A_KV_HEADS)]
        elem = {e: pair * 2 + e for e in range(2)}
        row0 = {e: pl.multiple_of(elem[e] * DEC_SEQ, DEC_SEQ) for e in range(2)}
        q = {e: q_ref[pl.ds(row0[e], DEC_SEQ), :] * (A_HEAD_DIM ** -0.5) for e in range(2)}
        kk = {e: jnp.concatenate([ks_ref[elem[e]], kn_ref[pl.ds(row0[e], DEC_SEQ), :], pad], axis=0).astype(BF16)
              for e in range(2)}
        vv = {e: jnp.concatenate([vs_ref[elem[e]], vn_ref[pl.ds(row0[e], DEC_SEQ), :], pad], axis=0).astype(BF16)
              for e in range(2)}
        heads = {g: range(g * A_GROUP, (g + 1) * A_GROUP) for g in range(A_KV_HEADS)}
        ds = {g: slice(g * A_HEAD_DIM, (g + 1) * A_HEAD_DIM) for g in range(A_KV_HEADS)}
        sink = {g: sink_ref[g * rows:(g + 1) * rows, 0:1] for g in range(A_KV_HEADS)}
        qg = {(bb, g): jnp.concatenate([q[bb][:, h * A_HEAD_DIM:(h + 1) * A_HEAD_DIM] for h in heads[g]],
                                       axis=0).astype(BF16) for bb, g in chains}
        s = {(bb, g): lax.dot_general(qg[bb, g], kk[bb][:, ds[g]], nt, preferred_element_type=F32)
             for bb, g in chains}
        s = {ch: jnp.where(ok, s[ch], NEG_INF) for ch in chains}
        m = {(bb, g): jnp.maximum(jnp.max(s[bb, g], axis=1, keepdims=True), sink[g]) for bb, g in chains}
        p = {ch: jnp.exp(s[ch] - m[ch]) for ch in chains}
        denom = {(bb, g): jnp.sum(p[bb, g], axis=1, keepdims=True) + jnp.exp(sink[g] - m[bb, g])
                 for bb, g in chains}
        o = {(bb, g): jnp.dot(p[bb, g].astype(BF16), vv[bb][:, ds[g]], preferred_element_type=F32) / denom[bb, g]
             for bb, g in chains}
        for bb, g in chains:
            for k, h in enumerate(heads[g]):
                o_ref[pl.ds(row0[bb], DEC_SEQ), h * A_HEAD_DIM:(h + 1) * A_HEAD_DIM] = (
                    o[bb, g][k * DEC_SEQ:(k + 1) * DEC_SEQ, :].astype(o_ref.dtype))
        return carry

    lax.fori_loop(0, BB_ATTN // 2, body, 0)


def _attn_sample(qk, rest, sinks, k_state, v_state):
    rows = BB_ATTN * DEC_SEQ
    base = N_PROMPT // rows
    kcol = A_Q // A_KV
    vcol = R_VA // A_KV
    sink_rows = jnp.broadcast_to(jnp.repeat(sinks, DEC_SEQ)[:, None], (A_HEADS * DEC_SEQ, LANES))
    return pl.pallas_call(
        _attn_sample_kernel,
        grid=(DEC_BATCH // BB_ATTN,),
        in_specs=[
            pl.BlockSpec((A_HEADS * DEC_SEQ, LANES), lambda i: (0, 0)),
            pl.BlockSpec((rows, A_Q), lambda i: (base + i, 0)),
            pl.BlockSpec((rows, A_KV), lambda i: (base + i, kcol)),
            pl.BlockSpec((rows, A_KV), lambda i: (base + i, vcol)),
            pl.BlockSpec((BB_ATTN, WINDOW, A_KV), lambda i: (i, 0, 0)),
            pl.BlockSpec((BB_ATTN, WINDOW, A_KV), lambda i: (i, 0, 0)),
        ],
        out_specs=pl.BlockSpec((rows, A_Q), lambda i: (i, 0)),
        out_shape=jax.ShapeDtypeStruct((N_SAMPLE, A_Q), BF16),
        compiler_params=_params(("parallel",)),
        name="attn_sample",
    )(sink_rows, qk, qk, rest, k_state, v_state)


def _split3(x):
    hi = x.astype(BF16)
    r1 = x - hi.astype(F32)
    mid = r1.astype(BF16)
    lo = (r1 - mid.astype(F32)).astype(BF16)
    return hi, mid, lo


GLA_INPUTS = ((LANES, R_LR), (B_QK, R_QB), (B_QK, R_KB), (B_V, R_VB), (B_V, R_GB))


def _gla_kernel(*refs, n_chunks, group, chunk, layer, n_in):
    ins, rest_refs = refs[:n_in], refs[n_in:]
    per = n_in // len(GLA_INPUTS)
    s0_ref, wa_ref, ba_ref, g_ref = rest_refs[:4]
    prev_ref = rest_refs[4] if layer else None
    o_ref, sout_ref, st_ref = rest_refs[-3:]
    ci = pl.program_id(1)
    c = chunk

    def rows(inp, g):
        if per == 1:
            return ins[inp][g * c:(g + 1) * c, :]
        return ins[inp * per + g][...]

    single = n_chunks == 1
    if not single:
        @pl.when(ci == 0)
        def _():
            for g in range(group):
                for h in range(B_HEADS):
                    st_ref[g, h] = s0_ref[0, g, h].T

    ri = lax.broadcasted_iota(jnp.int32, (c, c), 0)
    cj = lax.broadcasted_iota(jnp.int32, (c, c), 1)
    causal = cj <= ri
    tri = jnp.where(causal, 1.0, 0.0).astype(BF16)
    nt = (((1,), (1,)), ((), ()))
    G = range(group)
    z = [jnp.dot(rows(0, g).astype(BF16), wa_ref[...], preferred_element_type=F32) + ba_ref[...] for g in G]
    log_a = [-(jnp.maximum(-z[g], 0.0) + jnp.log1p(jnp.exp(-jnp.abs(z[g])))) / B_TAU for g in G]
    parts = [_split3(log_a[g]) for g in G]
    b = [jnp.dot(tri, parts[g][0], preferred_element_type=F32)
         + jnp.dot(tri, parts[g][1], preferred_element_type=F32)
         + jnp.dot(tri, parts[g][2], preferred_element_type=F32) for g in G]
    bl = [b[g][c - 1:c, :] for g in G]
    qd = [(rows(1, g) * (B_DK ** -0.5) * jnp.exp(b[g])).astype(BF16) for g in G]
    kd = [(rows(2, g) * jnp.exp(-b[g])).astype(BF16) for g in G]
    kl = [rows(2, g) * jnp.exp(bl[g] - b[g]) for g in G]
    kl = [kl[g].T if single else kl[g].astype(BF16) for g in G]
    ebl = [jnp.exp(bl[g]) for g in G]
    for h in range(B_HEADS):
        ks = slice(h * B_DK, (h + 1) * B_DK)
        vs = slice(h * B_DV, (h + 1) * B_DV)
        vh = [rows(3, g)[:, vs] for g in G]
        if single:
            st = [s0_ref[0, g, h] for g in G]
            o = [jnp.dot(qd[g][:, ks], st[g].astype(BF16), preferred_element_type=F32) for g in G]
        else:
            st = [st_ref[g, h] for g in G]
            o = [lax.dot_general(qd[g][:, ks], st[g].astype(BF16), nt, preferred_element_type=F32) for g in G]
        att = [lax.dot_general(qd[g][:, ks], kd[g][:, ks], nt, preferred_element_type=F32) for g in G]
        att = [jnp.where(causal, att[g], 0.0).astype(BF16) for g in G]
        o = [o[g] + jnp.dot(att[g], vh[g].astype(BF16), preferred_element_type=F32) for g in G]
        for g in G:
            if single:
                decay = jnp.broadcast_to(ebl[g][:, ks], (SUBLANES, B_DK)).T[:, 0:1]
                for l in range(layer):
                    sout_ref[l, g, h] = prev_ref[l, g, h]
                sout_ref[layer, g, h] = st[g] * decay + jnp.dot(kl[g][ks, :], vh[g],
                                                                  preferred_element_type=F32)
            else:
                st_ref[g, h] = st[g] * ebl[g][:, ks] + jnp.dot(vh[g].T.astype(BF16), kl[g][:, ks],
                                                               preferred_element_type=F32)
        o = [o[g] * lax.rsqrt(jnp.mean(o[g] * o[g], axis=1, keepdims=True) + RMS_EPS) * g_ref[h:h + 1, :]
             for g in G]
        for g in G:
            gate = rows(4, g)[:, vs]
            o_ref[g, :, vs] = (o[g] * (gate / (1.0 + jnp.exp(-gate)))).astype(o_ref.dtype)

    if not single:
        @pl.when(ci == n_chunks - 1)
        def _():
            for g in range(group):
                for l in range(layer):
                    sout_ref[l, g] = prev_ref[l, g]
                for h in range(B_HEADS):
                    sout_ref[layer, g, h] = st_ref[g, h].T


def _gla(rest, s0, layer, prev_states, wa, ba, gain, *, n_batch, seq, chunk, row_base, group):
    n_chunks = seq // chunk
    contiguous = n_chunks == 1
    base = row_base // chunk

    def in_specs_for(width, col):
        if contiguous:
            return [pl.BlockSpec((group * chunk, width), lambda b, ci: (base // group + b, col // width))]
        return [pl.BlockSpec((chunk, width),
                             lambda b, ci, g=g: (base + (b * group + g) * n_chunks + ci, col // width))
                for g in range(group)]

    row_specs = [spec for width, col in GLA_INPUTS for spec in in_specs_for(width, col)]
    s0_layer = layer if s0.shape[0] > 1 else 0
    state_block = (group, B_HEADS, B_DK, B_DV)
    prev_specs = [pl.BlockSpec((layer,) + state_block, lambda b, ci: (0, b, 0, 0, 0))] if layer else []
    prev_args = [prev_states] if layer else []
    return pl.pallas_call(
        functools.partial(_gla_kernel, n_chunks=n_chunks, group=group, chunk=chunk, layer=layer,
                          n_in=len(row_specs)),
        grid=(n_batch // group, n_chunks),
        in_specs=row_specs + [
            pl.BlockSpec((1,) + state_block, lambda b, ci: (s0_layer, b, 0, 0, 0)),
            pl.BlockSpec((LANES, B_QK), lambda b, ci: (0, 0)),
            pl.BlockSpec((1, B_QK), lambda b, ci: (0, 0)),
            pl.BlockSpec((B_HEADS, B_DV), lambda b, ci: (0, 0)),
        ] + prev_specs,
        out_specs=[
            pl.BlockSpec((group, chunk, B_V), lambda b, ci: (b, ci, 0)),
            pl.BlockSpec((layer + 1,) + state_block, lambda b, ci: (0, b, 0, 0, 0)),
        ],
        out_shape=[
            jax.ShapeDtypeStruct((n_batch, seq, B_V), BF16),
            jax.ShapeDtypeStruct((layer + 1, n_batch, B_HEADS, B_DK, B_DV), F32),
        ],
        scratch_shapes=[pltpu.VMEM((group, B_HEADS, B_DV, B_DK), F32)],
        compiler_params=_params(("parallel", "arbitrary")),
        name="gla",
    )(*([rest] * len(row_specs)), s0, wa, ba, gain, *prev_args)


HALO = 16
BB_POOL = 16


def _pool_kernel(u_ref, prev_ref, w_ref, scale_ref, o_ref, *, from_start, n_seq):
    tp = u_ref.shape[0] // n_seq
    if from_start:
        ti = pl.program_id(1)
        t0 = ti * tp
    diffs = [[] for _ in POOL_WINDOWS]
    for sq in range(n_seq):
        u = u_ref[sq * tp:(sq + 1) * tp, :]
        prev = prev_ref[sq * HALO:(sq + 1) * HALO, :]
        if from_start:
            prev = jnp.where(ti > 0, prev, 0.0)
        full = jnp.concatenate([prev, u], axis=0)
        for g, w in enumerate(POOL_WINDOWS):
            cs = slice(g * C_GROUP_W, (g + 1) * C_GROUP_W)
            acc = full[:, cs]
            span = 1
            while span < w:
                acc = acc + pltpu.roll(acc, span, axis=0)
                span *= 2
            wsum = acc[HALO:, :]
            if from_start:
                t = t0 + lax.broadcasted_iota(jnp.int32, (tp, C_GROUP_W), 0)
                cnt = jnp.minimum(t + 1, w).astype(F32)
            else:
                cnt = float(w)
            diffs[g].append(wsum / cnt - u[:, cs])
    for g in range(C_GROUPS):
        cs = slice(g * C_GROUP_W, (g + 1) * C_GROUP_W)
        d = jnp.concatenate(diffs[g], axis=0) if n_seq > 1 else diffs[g][0]
        y = jnp.dot(d.astype(BF16), w_ref[g], preferred_element_type=F32) * scale_ref[:, cs]
        o_ref[:, cs] = y.astype(o_ref.dtype)


def _pool_prompt(rest, w, scale):
    nt = SEQ // TP_POOL
    ucol = R_UC // C_WIDTH

    def halo(b, i):
        return (jnp.maximum((b * SEQ + i * TP_POOL) // HALO - 1, 0), ucol)

    return pl.pallas_call(
        functools.partial(_pool_kernel, from_start=True, n_seq=1),
        grid=(BATCH, nt),
        in_specs=[
            pl.BlockSpec((TP_POOL, C_WIDTH), lambda b, i: (b * nt + i, ucol)),
            pl.BlockSpec((HALO, C_WIDTH), halo),
            pl.BlockSpec((C_GROUPS, C_GROUP_W, C_GROUP_W), lambda b, i: (0, 0, 0)),
            pl.BlockSpec((1, C_WIDTH), lambda b, i: (0, 0)),
        ],
        out_specs=pl.BlockSpec((TP_POOL, C_WIDTH), lambda b, i: (b * nt + i, 0)),
        out_shape=jax.ShapeDtypeStruct((N_PROMPT, C_WIDTH), BF16),
        compiler_params=_params(("parallel", "arbitrary")),
        name="pool_prompt",
    )(rest, rest, w, scale)


def _pool_sample(rest, prev, w, scale):
    ucol = R_UC // C_WIDTH
    rows = BB_POOL * DEC_SEQ
    base = N_PROMPT // rows
    return pl.pallas_call(
        functools.partial(_pool_kernel, from_start=False, n_seq=BB_POOL),
        grid=(DEC_BATCH // BB_POOL,),
        in_specs=[
            pl.BlockSpec((rows, C_WIDTH), lambda b: (base + b, ucol)),
            pl.BlockSpec((BB_POOL * HALO, C_WIDTH), lambda b: (b, 0)),
            pl.BlockSpec((C_GROUPS, C_GROUP_W, C_GROUP_W), lambda b: (0, 0, 0)),
            pl.BlockSpec((1, C_WIDTH), lambda b: (0, 0)),
        ],
        out_specs=pl.BlockSpec((rows, C_WIDTH), lambda b: (b, 0)),
        out_shape=jax.ShapeDtypeStruct((N_SAMPLE, C_WIDTH), BF16),
        compiler_params=_params(("parallel",)),
        name="pool_sample",
    )(rest, prev, w, scale)


def _layer_norm(x, g, b):
    mu = jnp.mean(x, axis=1, keepdims=True)
    xc = x - mu
    var = jnp.mean(xc * xc, axis=1, keepdims=True)
    return xc * lax.rsqrt(var + LN_EPS) * g + b


def _merge_kernel(x_ref, gates_ref, oa_p_ref, oa_s_ref, ob_p_ref, ob_s_ref, oc_p_ref, oc_s_ref,
                  wa_ref, wb_ref, wc_ref, wo_ref, g_ref, b_ref, o_ref, oT_ref):
    in_sample = pl.program_id(0) >= N_PROMPT // TM_MERGE

    def gate(i):
        z = gates_ref[:, i * D_MODEL:(i + 1) * D_MODEL]
        return 1.0 / (1.0 + jnp.exp(-z))

    def branch(p_ref, s_ref, w_ref):
        o = jnp.where(in_sample, s_ref[...], p_ref[...])
        return jnp.dot(o, w_ref[...], preferred_element_type=F32)

    merged = (gate(0) * branch(oa_p_ref, oa_s_ref, wa_ref)
              + gate(1) * branch(ob_p_ref, ob_s_ref, wb_ref)
              + gate(2) * branch(oc_p_ref, oc_s_ref, wc_ref))
    mix = jnp.dot(merged.astype(BF16), wo_ref[...], preferred_element_type=F32)
    y = _layer_norm(DN_ALPHA * x_ref[...] + mix, g_ref[...], b_ref[...])
    o_ref[...] = y
    oT_ref[...] = y.T.astype(BF16)


def _merge(x, rest, branches, wa, wb, wc, wo, g, b):
    n_p = N_PROMPT // TM_MERGE
    row = pl.BlockSpec((TM_MERGE, D_MODEL), lambda i: (i, 0))
    row_p = pl.BlockSpec((TM_MERGE, D_MODEL), lambda i: (jnp.minimum(i, n_p - 1), 0))
    row_s = pl.BlockSpec((TM_MERGE, D_MODEL), lambda i: (jnp.maximum(i - n_p, 0), 0))
    wspec = pl.BlockSpec((D_MODEL, D_MODEL), lambda i: (0, 0))
    vec = pl.BlockSpec((1, D_MODEL), lambda i: (0, 0))
    (oa_p, oa_s), (ob_p, ob_s), (oc_p, oc_s) = branches
    return pl.pallas_call(
        _merge_kernel,
        grid=(N_TOK // TM_MERGE,),
        in_specs=[row, pl.BlockSpec((TM_MERGE, 3 * D_MODEL), lambda i: (i, R_GATES)),
                  row_p, row_s, row_p, row_s, row_p, row_s, wspec, wspec, wspec, wspec, vec, vec],
        out_specs=[row, pl.BlockSpec((D_MODEL, TM_MERGE), lambda i: (0, i))],
        out_shape=[jax.ShapeDtypeStruct((N_TOK, D_MODEL), F32),
                   jax.ShapeDtypeStruct((D_MODEL, N_TOK), BF16)],
        compiler_params=_params(("parallel",)),
        name="merge",
    )(x, rest, oa_p, oa_s, ob_p, ob_s, oc_p, oc_s, wa, wb, wc, wo, g, b)


def _merge_sort_pairs(n):
    size = 1
    while size < n:
        size *= 2
    pairs = []
    p = 1
    while p < size:
        k = p
        while k >= 1:
            for j in range(k % p, size - k, 2 * k):
                for i in range(min(k, size - j - k)):
                    if (i + j) // (2 * p) == (i + j + k) // (2 * p):
                        pairs.append((i + j, i + j + k))
            k //= 2
        p *= 2
    return [(a, b) for a, b in pairs if b < n]


def _top_desc(s, n):
    v = [s[SUBLANES * k:SUBLANES * (k + 1), :] for k in range(s.shape[0] // SUBLANES)]
    depth = len(v)
    for a, b in _merge_sort_pairs(depth):
        v[a], v[b] = jnp.maximum(v[a], v[b]), jnp.minimum(v[a], v[b])
    vals = []
    for r in range(n):
        m = jnp.max(v[0], axis=0, keepdims=True)
        vals.append(m)
        hit = v[0] == m
        needed = n - r - 1
        for k in range(min(depth - 1, needed)):
            v[k] = jnp.where(hit, v[k + 1], v[k])
        if needed >= depth:
            v[depth - 1] = jnp.where(hit, -jnp.inf, v[depth - 1])
    return vals


def _peer_topk_kernel(xT_ref, wq_ref, sk_ref, cnt_ref, e1_ref, rank_ref, e2_ref):
    tt = xT_ref.shape[1]
    row8 = lax.broadcasted_iota(jnp.int32, (SUBLANES, tt), 0)

    def scores(hp):
        qT = jnp.dot(wq_ref[hp * PEER_DHALF:(hp + 1) * PEER_DHALF, :], xT_ref[...], preferred_element_type=F32)
        return jnp.dot(sk_ref[hp], qT.astype(BF16), preferred_element_type=F32)

    for h in range(PEER_HEADS):
        s1, s2 = scores(2 * h), scores(2 * h + 1)
        vals1 = _top_desc(s1, PEER_TOPK)
        vals2 = _top_desc(s2, PEER_TOPK)
        e2_ref[h] = jnp.exp(s2 - vals2[0]).astype(BF16)
        sv1 = jnp.concatenate(vals1, axis=0)
        sv2 = jnp.concatenate(vals2, axis=0)
        pieces = [sv1[0:1, :] + sv2]
        for a in range(1, SUBLANES):
            nb = PEER_TOPK // (a + 1)
            pieces.append(jnp.where(row8 < nb, sv1[a:a + 1, :] + sv2[0:SUBLANES, :], -jnp.inf))
        pieces.append(sv2[0:1, :] + sv1[SUBLANES:PEER_TOPK, :])
        cand = _top_desc(jnp.concatenate(pieces, axis=0), PEER_TOPK)
        top = cand[0]
        z = jnp.zeros_like(top)
        for r in range(PEER_TOPK):
            z = z + jnp.exp(cand[r] - top)
        kth = cand[PEER_TOPK - 1]
        counts = [jnp.sum(jnp.where(p >= kth, 1.0, 0.0), axis=0, keepdims=True) for p in pieces[:SUBLANES]]
        last = PEER_TOPK - 1
        cnt = jnp.where(s1 >= sv1[last:last + 1, :], jnp.where(s1 + sv2[0:1, :] >= kth, 1.0, 0.0), 0.0)
        rank = jnp.where(s2 >= sv2[last:last + 1, :],
                         jnp.where(sv1[0:1, :] + s2 >= kth, float(SUBLANES), float(PEER_TOPK)), float(PEER_TOPK))
        for a in range(SUBLANES):
            cnt = jnp.where(s1 == sv1[a:a + 1, :], counts[a], cnt)
            rank = jnp.where(s2 == sv2[a:a + 1, :], float(a), rank)
        rank_ref[h] = rank.astype(BF16)
        e1 = jnp.exp(s1 - (sv1[0:1, :] + jnp.log(z))) * SQRT_HALF
        for c in range(tt // LANES):
            cnt_ref[h, c] = cnt[:, c * LANES:(c + 1) * LANES]
            e1_ref[h, c] = e1[:, c * LANES:(c + 1) * LANES]


def _peer_topk(xT, wqT, sk):
    spec = pl.BlockSpec((PEER_HEADS, N_KEYS, TT_TOPK), lambda t: (0, 0, t))
    row_spec = pl.BlockSpec((PEER_HEADS, TT_TOPK // LANES, N_KEYS, LANES), lambda t: (0, t, 0, 0))
    wide = jax.ShapeDtypeStruct((PEER_HEADS, N_TOK // LANES, N_KEYS, LANES), F32)
    narrow = jax.ShapeDtypeStruct((PEER_HEADS, N_KEYS, N_TOK), BF16)
    return pl.pallas_call(
        _peer_topk_kernel,
        grid=(N_TOK // TT_TOPK,),
        in_specs=[
            pl.BlockSpec((D_MODEL, TT_TOPK), lambda t: (0, t)),
            pl.BlockSpec((PEER_HEADS * PEER_DKEY, D_MODEL), lambda t: (0, 0)),
            pl.BlockSpec((2 * PEER_HEADS, N_KEYS, PEER_DHALF), lambda t: (0, 0, 0)),
        ],
        out_specs=[row_spec, row_spec, spec, spec],
        out_shape=[wide, wide, narrow, narrow],
        compiler_params=_params(("parallel",)),
        name="peer_topk",
    )(xT, wqT, sk)


SQRT_HALF = 2.0 ** -0.5


def _gelu_unscaled(x):
    t = x * SQRT_HALF
    return t * (1.0 + lax.erf(t))


LC_PEER = 256


def _bf16_rows(ref, h, r, chunks, n_rows):
    x = jnp.concatenate([ref[h, c, pl.ds(r, 2 * SUBLANES, stride=0), :] for c in chunks], axis=1)
    packed = x.astype(BF16)
    return jnp.concatenate([packed] * (n_rows // packed.shape[0]), axis=0)


MM_PIECE = 512


def _peer_main_kernel(xT_ref, u_ref, vt_ref, cnt_ref, e1_ref, rank_ref, e2_ref, x_ref, g_ref, b_ref,
                      o_ref, ob_ref, yT_ref, s_scr, wh_scr):
    i = pl.program_id(1)
    tt = xT_ref.shape[1]

    @pl.when(i == 0)
    def _():
        yT_ref[...] = jnp.zeros_like(yT_ref)

    zero = jnp.zeros((N_KEYS, LC_PEER), BF16)
    per_piece = MM_PIECE // N_KEYS
    for ii in range(IB_PEER):
        rows = slice(ii * N_KEYS, (ii + 1) * N_KEYS)
        if ii % per_piece == 0:
            piece = slice(ii * N_KEYS, ii * N_KEYS + MM_PIECE)
            s_scr[piece, :] = jnp.dot(u_ref[0, piece, :], xT_ref[...], preferred_element_type=F32)
        for lc in range(tt // LC_PEER):
            cols = slice(lc * LC_PEER, (lc + 1) * LC_PEER)
            chunks = range(lc * LC_PEER // LANES, (lc + 1) * LC_PEER // LANES)
            w = zero
            for h in range(PEER_HEADS):
                cnt = _bf16_rows(cnt_ref, h, ii, chunks, N_KEYS)
                e1 = _bf16_rows(e1_ref, h, ii, chunks, N_KEYS)
                w = w + jnp.where(rank_ref[h, :, cols] < cnt, e2_ref[h, :, cols] * e1, zero)
            wh_scr[rows, cols] = w
    acc = None
    for p in range(IB_PEER // per_piece):
        for ii in range(p * per_piece, (p + 1) * per_piece):
            rows = slice(ii * N_KEYS, (ii + 1) * N_KEYS)
            wh_scr[rows, :] = wh_scr[rows, :] * _gelu_unscaled(s_scr[rows, :]).astype(BF16)
        piece = slice(p * MM_PIECE, (p + 1) * MM_PIECE)
        part = jnp.dot(vt_ref[0, :, piece], wh_scr[piece, :], preferred_element_type=F32)
        acc = part if acc is None else acc + part
    yT_ref[...] += acc

    @pl.when(i == pl.num_programs(1) - 1)
    def _():
        y = _layer_norm(DN_ALPHA * x_ref[...] + yT_ref[...].T, g_ref[...], b_ref[...])
        o_ref[...] = y
        ob_ref[...] = y.astype(BF16)


def _peer_main(xT, u, vt, layer, cnt, e1, rank, e2, x1, g, b):
    eb = IB_PEER * N_KEYS
    row_spec = pl.BlockSpec((PEER_HEADS, TT_PEER // LANES, IB_PEER, LANES), lambda t, i: (0, t, i, 0))
    tab_spec = pl.BlockSpec((PEER_HEADS, N_KEYS, TT_PEER), lambda t, i: (0, 0, t))
    tok_spec = pl.BlockSpec((TT_PEER, D_MODEL), lambda t, i: (t, 0))
    vec = pl.BlockSpec((1, D_MODEL), lambda t, i: (0, 0))
    return pl.pallas_call(
        _peer_main_kernel,
        grid=(N_TOK // TT_PEER, N_EXPERTS // eb),
        in_specs=[
            pl.BlockSpec((D_MODEL, TT_PEER), lambda t, i: (0, t)),
            pl.BlockSpec((1, eb, D_MODEL), lambda t, i: (layer, i, 0)),
            pl.BlockSpec((1, D_MODEL, eb), lambda t, i: (layer, 0, i)),
            row_spec, row_spec, tab_spec, tab_spec, tok_spec, vec, vec,
        ],
        out_specs=[tok_spec, tok_spec],
        out_shape=[jax.ShapeDtypeStruct((N_TOK, D_MODEL), F32),
                   jax.ShapeDtypeStruct((N_TOK, D_MODEL), BF16)],
        scratch_shapes=[pltpu.VMEM((D_MODEL, TT_PEER), F32),
                        pltpu.VMEM((eb, TT_PEER), F32), pltpu.VMEM((eb, TT_PEER), BF16)],
        compiler_params=_params(("parallel", "arbitrary")),
        name="peer_main",
    )(xT, u, vt, cnt, e1, rank, e2, x1, g, b)


def _rope_tables():
    half = ROT_DIM // 2
    pos = jnp.concatenate([jnp.arange(SEQ), PAST_LEN + (jnp.arange(TM_QK) % DEC_SEQ)])
    inv = ROPE_THETA ** (-jnp.arange(half, dtype=F32) / half)
    ang = pos.astype(F32)[:, None] * inv[None, :]
    cos, sin = jnp.cos(ang), jnp.sin(ang)
    n = pos.shape[0]
    one = jnp.ones((n, A_HEAD_DIM - ROT_DIM), F32)
    zero = jnp.zeros((n, A_HEAD_DIM - ROT_DIM), F32)
    zh = jnp.zeros((n, half), F32)
    reps = LANES // A_HEAD_DIM
    c = jnp.tile(jnp.concatenate([cos, cos, one], 1), (1, reps))
    s1 = jnp.tile(jnp.concatenate([-sin, zh, zero], 1), (1, reps))
    s2 = jnp.tile(jnp.concatenate([zh, sin, zero], 1), (1, reps))
    return c, s1, s2


def _split_cols(w):
    cuts = [int(c) for c in np.cumsum(SPLITS)[:-1]]
    return jnp.split(w, cuts, axis=-1)


def _layer(layer, x, xb, k_state, v_state, gla_states, gla_out, pool_state, rope, w_in, b_in, sinks,
           w_alpha, b_alpha, gla_g, w_pool, pool_scale, w_a, w_b, w_c, w_out, ln1_g, ln1_b,
           peer_query, peer_subkeys, peer_u, peer_vt, ln2_g, ln2_b):
    qa_w, ka_w, va_w, qb_w, kb_w, vb_w, lr_w, gb_w, uc_w, gates_w = _split_cols(w_in)
    qa_b, ka_b, va_b, qb_b, kb_b, vb_b, lr_b, gb_b, uc_b, gates_b = _split_cols(b_in[None, :])
    lr_pad = LANES - B_GATE_RANK
    w_qk = jnp.concatenate([qa_w, ka_w], 1).astype(BF16)
    b_qk = jnp.concatenate([qa_b, ka_b], 1)
    w_rest = jnp.concatenate([gates_w, vb_w, gb_w, uc_w, qb_w, kb_w, va_w,
                              jnp.pad(lr_w, ((0, 0), (0, lr_pad)))], 1).astype(BF16)
    b_rest = jnp.concatenate([gates_b, vb_b, gb_b, uc_b, qb_b, kb_b, va_b,
                              jnp.pad(lr_b, ((0, 0), (0, lr_pad)))], 1)

    qk = _proj_qk(xb, w_qk, b_qk, *rope)
    rest = _proj_rest(xb, w_rest, b_rest)

    ks = k_state.reshape(DEC_BATCH, WINDOW, A_KV)
    vs = v_state.reshape(DEC_BATCH, WINDOW, A_KV)
    oa = (_attn_prompt(qk, rest, sinks), _attn_sample(qk, rest, sinks, ks, vs))

    wa = jnp.pad(w_alpha, ((0, lr_pad), (0, 0))).astype(BF16)
    ba = b_alpha[None, :]
    gla_p, gla_s = gla_out
    ob_p, gla_p = _gla(rest, jnp.zeros((1, BATCH, B_HEADS, B_DK, B_DV), F32), layer, gla_p, wa, ba, gla_g,
                       n_batch=BATCH, seq=SEQ, chunk=GLA_CHUNK, row_base=0, group=GLA_GROUP_PROMPT)
    ob_s, gla_s = _gla(rest, gla_states, layer, gla_s, wa, ba, gla_g, n_batch=DEC_BATCH, seq=DEC_SEQ,
                       chunk=math.gcd(DEC_SEQ, GLA_CHUNK), row_base=N_PROMPT, group=GLA_GROUP_SAMPLE)
    ob = (ob_p.reshape(N_PROMPT, B_V), ob_s.reshape(N_SAMPLE, B_V))

    wp = w_pool.astype(BF16)
    ps = pool_scale[None, :]
    prev = jnp.pad(pool_state, ((0, 0), (HALO - POOL_STATE, 0), (0, 0))).reshape(DEC_BATCH * HALO, C_WIDTH)
    oc = (_pool_prompt(rest, wp, ps), _pool_sample(rest, prev, wp, ps))

    x1, x1T = _merge(x, rest, (oa, ob, oc), w_a.astype(BF16), w_b.astype(BF16), w_c.astype(BF16),
                     w_out.astype(BF16), ln1_g[None, :], ln1_b[None, :])

    wqT = peer_query.reshape(D_MODEL, PEER_HEADS * PEER_DKEY).T.astype(BF16)
    sk = peer_subkeys.reshape(2 * PEER_HEADS, N_KEYS, PEER_DHALF).astype(BF16)
    cnt, e1, rank, e2 = _peer_topk(x1T, wqT, sk)
    x2, x2b = _peer_main(x1T, peer_u, peer_vt, layer, cnt, e1, rank, e2, x1, ln2_g[None, :], ln2_b[None, :])

    def prompt_tail(t, col0, width, n):
        return jnp.stack([t[(b + 1) * SEQ - n:(b + 1) * SEQ, col0:col0 + width] for b in range(BATCH)])

    def sample_tail(state, t, col0, width, n):
        new = t[N_PROMPT:, col0:col0 + width].reshape(DEC_BATCH, DEC_SEQ, width)
        return jnp.concatenate([state, new], 1)[:, -n:]

    kv_shape = (-1, WINDOW, A_KV_HEADS, A_HEAD_DIM)
    states = (prompt_tail(qk, A_Q, A_KV, WINDOW).reshape(kv_shape),
              prompt_tail(rest, R_VA, A_KV, WINDOW).reshape(kv_shape),
              prompt_tail(rest, R_UC, C_WIDTH, POOL_STATE),
              sample_tail(ks, qk, A_Q, A_KV, WINDOW).reshape(kv_shape),
              sample_tail(vs, rest, R_VA, A_KV, WINDOW).reshape(kv_shape),
              sample_tail(pool_state, rest, R_UC, C_WIDTH, POOL_STATE))
    return x2, x2b, states, (gla_p, gla_s)


def kernel(x_prompt, x_sample, state_win_k, state_win_v, state_gla, state_pool, w_in, b_in, attn_sinks,
           w_alpha, b_alpha, gla_norm_g, w_pool, pool_scale, w_branch_a, w_branch_b, w_branch_c, w_out,
           ln1_g, ln1_b, peer_query, peer_subkeys, peer_u, peer_v, ln2_g, ln2_b):
    x = jnp.concatenate([x_prompt.reshape(N_PROMPT, D_MODEL), x_sample.reshape(N_SAMPLE, D_MODEL)], 0)
    xb = x.astype(BF16)
    rope = _rope_tables()
    peer_ub = peer_u.astype(BF16)
    peer_vtb = jnp.swapaxes(peer_v, 1, 2).astype(BF16)
    per_layer = []
    gla_out = (None, None)
    for l in range(DEPTH):
        x, xb, states, gla_out = _layer(
            l, x, xb, state_win_k[l], state_win_v[l], state_gla, gla_out, state_pool[l], rope,
            w_in[l], b_in[l], attn_sinks[l], w_alpha[l], b_alpha[l], gla_norm_g[l],
            w_pool[l], pool_scale[l], w_branch_a[l], w_branch_b[l], w_branch_c[l], w_out[l],
            ln1_g[l], ln1_b[l], peer_query[l], peer_subkeys[l], peer_ub, peer_vtb, ln2_g[l], ln2_b[l])
        per_layer.append(states)
    pk, pv, pp, sk, sv, sp = [jnp.stack([per_layer[l][i] for l in range(DEPTH)]) for i in range(6)]
    return (x[:N_PROMPT].reshape(BATCH, SEQ, D_MODEL), x[N_PROMPT:].reshape(DEC_BATCH, DEC_SEQ, D_MODEL),
            pk, pv, gla_out[0], pp, sk, sv, gla_out[1], sp)
```

```python
import functools
import math

import jax
import jax.numpy as jnp
import numpy as np
from jax import lax
from jax.experimental import pallas as pl
from jax.experimental.pallas import tpu as pltpu

F32 = jnp.float32
BF16 = jnp.bfloat16

D_MODEL = 1024
BATCH = 8
SEQ = 2048
DEPTH = 2
DEC_BATCH = 128
DEC_SEQ = 8
PAST_LEN = 16384

A_HEADS = 16
A_KV_HEADS = 2
A_HEAD_DIM = 64
A_GROUP = A_HEADS // A_KV_HEADS
WINDOW = 128
ROT_DIM = A_HEAD_DIM // 4
ROPE_THETA = 500000.0
NEG_INF = -1e30
B_HEADS = 4
B_DK = D_MODEL // 2 // B_HEADS
B_DV = D_MODEL // B_HEADS
B_GATE_RANK = 16
B_TAU = 16.0
GLA_CHUNK = 64
POOL_WINDOWS = (2, 4, 8, 16)
C_GROUPS = len(POOL_WINDOWS)
C_GROUP_W = D_MODEL // C_GROUPS
C_WIDTH = C_GROUPS * C_GROUP_W
POOL_STATE = max(POOL_WINDOWS) - 1
PEER_HEADS = 8
N_KEYS = 128
N_EXPERTS = N_KEYS * N_KEYS
PEER_TOPK = 16
PEER_DKEY = 256
PEER_DHALF = PEER_DKEY // 2
DN_ALPHA = (2 * DEPTH) ** 0.25
LN_EPS = 1e-5
RMS_EPS = 1e-6

A_Q = A_HEADS * A_HEAD_DIM
A_KV = A_KV_HEADS * A_HEAD_DIM
B_QK = B_HEADS * B_DK
B_V = B_HEADS * B_DV
SPLITS = (A_Q, A_KV, A_KV, B_QK, B_QK, B_V, B_GATE_RANK, B_V, C_WIDTH, 3 * D_MODEL)

LANES = 128
SUBLANES = 8
N_PROMPT = BATCH * SEQ
N_SAMPLE = DEC_BATCH * DEC_SEQ
N_TOK = N_PROMPT + N_SAMPLE

R_GATES = 0
R_VB = 3 * D_MODEL
R_GB = R_VB + B_V
R_UC = R_GB + B_V
R_QB = R_UC + C_WIDTH
R_KB = R_QB + B_QK
R_VA = R_KB + B_QK
R_LR = R_VA + A_KV
R_WIDTH = R_LR + LANES
QK_WIDTH = A_Q + A_KV

VMEM_LIMIT = 48 * 1024 * 1024

TM_QK = 512
TM_REST = 512
TN_REST = R_WIDTH // 2
TM_MERGE = 256
TP_POOL = 512
GLA_GROUP_PROMPT = 4
GLA_GROUP_SAMPLE = 4
TT_TOPK = 512
TT_PEER = 512
IB_PEER = 16


def _params(sem):
    return pltpu.CompilerParams(dimension_semantics=sem, vmem_limit_bytes=VMEM_LIMIT)


def _qk_kernel(x_ref, w_ref, b_ref, c_ref, s1_ref, s2_ref, o_ref):
    y = jnp.dot(x_ref[...], w_ref[...], preferred_element_type=F32) + b_ref[...]
    c = c_ref[...]
    s1 = s1_ref[...]
    s2 = s2_ref[...]
    for j in range(QK_WIDTH // LANES):
        yj = y[:, j * LANES:(j + 1) * LANES]
        up = pltpu.roll(yj, LANES - ROT_DIM // 2, axis=1)
        dn = pltpu.roll(yj, ROT_DIM // 2, axis=1)
        o_ref[:, j * LANES:(j + 1) * LANES] = yj * c + up * s1 + dn * s2


def _proj_qk(xb, w, b, rope_c, rope_s1, rope_s2):
    n_prompt_blocks = SEQ // TM_QK

    def tab_map(i):
        return (jnp.where(i < N_PROMPT // TM_QK, i % n_prompt_blocks, n_prompt_blocks), 0)

    tab_spec = pl.BlockSpec((TM_QK, LANES), tab_map)
    return pl.pallas_call(
        _qk_kernel,
        grid=(N_TOK // TM_QK,),
        in_specs=[
            pl.BlockSpec((TM_QK, D_MODEL), lambda i: (i, 0)),
            pl.BlockSpec((D_MODEL, QK_WIDTH), lambda i: (0, 0)),
            pl.BlockSpec((1, QK_WIDTH), lambda i: (0, 0)),
            tab_spec, tab_spec, tab_spec,
        ],
        out_specs=pl.BlockSpec((TM_QK, QK_WIDTH), lambda i: (i, 0)),
        out_shape=jax.ShapeDtypeStruct((N_TOK, QK_WIDTH), F32),
        compiler_params=_params(("parallel",)),
        name="proj_qk",
    )(xb, w, b, rope_c, rope_s1, rope_s2)


def _mm_bias_kernel(x_ref, w_ref, b_ref, o_ref):
    o_ref[...] = jnp.dot(x_ref[...], w_ref[...], preferred_element_type=F32) + b_ref[...]


def _proj_rest(xb, w, b):
    return pl.pallas_call(
        _mm_bias_kernel,
        grid=(R_WIDTH // TN_REST, N_TOK // TM_REST),
        in_specs=[
            pl.BlockSpec((TM_REST, D_MODEL), lambda j, i: (i, 0)),
            pl.BlockSpec((D_MODEL, TN_REST), lambda j, i: (0, j)),
            pl.BlockSpec((1, TN_REST), lambda j, i: (0, j)),
        ],
        out_specs=pl.BlockSpec((TM_REST, TN_REST), lambda j, i: (i, j)),
        out_shape=jax.ShapeDtypeStruct((N_TOK, R_WIDTH), F32),
        compiler_params=_params(("parallel", "arbitrary")),
        name="proj_rest",
    )(xb, w, b)


HEADS_PER_PASS = 16


def _attend(q, kk, vv, sink_ref, c_min, o_ref, row0):
    tq = q.shape[0]
    r = lax.broadcasted_iota(jnp.int32, (tq, 2 * WINDOW), 0)
    c = lax.broadcasted_iota(jnp.int32, (tq, 2 * WINDOW), 1)
    ok = (c > r) & (c <= r + WINDOW) & (c >= c_min)
    qb = (q * (A_HEAD_DIM ** -0.5)).astype(BF16)
    nt = (((1,), (1,)), ((), ()))
    for h0 in range(0, A_HEADS, HEADS_PER_PASS):
        hs = range(h0, h0 + HEADS_PER_PASS)
        col = {h: slice(h * A_HEAD_DIM, (h + 1) * A_HEAD_DIM) for h in hs}
        kv = {h: slice((h // A_GROUP) * A_HEAD_DIM, (h // A_GROUP + 1) * A_HEAD_DIM) for h in hs}
        s = {h: lax.dot_general(qb[:, col[h]], kk[:, kv[h]], nt, preferred_element_type=F32) for h in hs}
        s = {h: jnp.where(ok, s[h], NEG_INF) for h in hs}
        m = {h: jnp.maximum(jnp.max(s[h], axis=1, keepdims=True), sink_ref[h]) for h in hs}
        p = {h: jnp.exp(s[h] - m[h]) for h in hs}
        denom = {h: jnp.sum(p[h], axis=1, keepdims=True) + jnp.exp(sink_ref[h] - m[h]) for h in hs}
        o = {h: jnp.dot(p[h].astype(BF16), vv[:, kv[h]], preferred_element_type=F32) / denom[h] for h in hs}
        for h in hs:
            o_ref[row0:row0 + tq, col[h]] = o[h].astype(o_ref.dtype)


def _attn_prompt_kernel(sink_ref, q_ref, kc_ref, kp_ref, vc_ref, vp_ref, o_ref):
    n = pl.program_id(1)
    k_all = jnp.concatenate([kp_ref[...], kc_ref[...]], axis=0).astype(BF16)
    v_all = jnp.concatenate([vp_ref[...], vc_ref[...]], axis=0).astype(BF16)
    for j in range(QB_ATTN):
        keys = slice(j * WINDOW, (j + 2) * WINDOW)
        c_min = jnp.where(n > 0, 0, WINDOW) if j == 0 else 0
        _attend(q_ref[j * WINDOW:(j + 1) * WINDOW, :], k_all[keys, :], v_all[keys, :], sink_ref, c_min,
                o_ref, j * WINDOW)


QB_ATTN = 4


def _attn_prompt(qk, rest, sinks):
    nb = SEQ // WINDOW
    kcol = A_Q // A_KV
    vcol = R_VA // A_KV
    rows = QB_ATTN * WINDOW

    def cur(b, n):
        return b * (nb // QB_ATTN) + n

    def prev(b, n):
        return b * nb + jnp.maximum(n * QB_ATTN - 1, 0)

    return pl.pallas_call(
        _attn_prompt_kernel,
        grid=(BATCH, nb // QB_ATTN),
        in_specs=[
            pl.BlockSpec(memory_space=pltpu.SMEM),
            pl.BlockSpec((rows, A_Q), lambda b, n: (cur(b, n), 0)),
            pl.BlockSpec((rows, A_KV), lambda b, n: (cur(b, n), kcol)),
            pl.BlockSpec((WINDOW, A_KV), lambda b, n: (prev(b, n), kcol)),
            pl.BlockSpec((rows, A_KV), lambda b, n: (cur(b, n), vcol)),
            pl.BlockSpec((WINDOW, A_KV), lambda b, n: (prev(b, n), vcol)),
        ],
        out_specs=pl.BlockSpec((rows, A_Q), lambda b, n: (cur(b, n), 0)),
        out_shape=jax.ShapeDtypeStruct((N_PROMPT, A_Q), BF16),
        compiler_params=_params(("parallel", "arbitrary")),
        name="attn_prompt",
    )(sinks, qk, qk, qk, rest, rest)


BB_ATTN = 8


def _attn_sample_kernel(sink_ref, q_ref, kn_ref, vn_ref, ks_ref, vs_ref, o_ref):
    pad = jnp.zeros((WINDOW - DEC_SEQ, A_KV), F32)
    rows = A_GROUP * DEC_SEQ
    t = lax.broadcasted_iota(jnp.int32, (rows, 2 * WINDOW), 0) % DEC_SEQ
    c = lax.broadcasted_iota(jnp.int32, (rows, 2 * WINDOW), 1)
    ok = (c > t) & (c <= t + WINDOW)
    nt = (((1,), (1,)), ((), ()))

    def body(pair, carry):
        chains = [(e, g) for e in range(2) for g in range(A_KV_HEADS)]
        elem = {e: pair * 2 + e for e in range(2)}
        row0 = {e: pl.multiple_of(elem[e] * DEC_SEQ, DEC_SEQ) for e in range(2)}
        q = {e: q_ref[pl.ds(row0[e], DEC_SEQ), :] * (A_HEAD_DIM ** -0.5) for e in range(2)}
        kk = {e: jnp.concatenate([ks_ref[elem[e]], kn_ref[pl.ds(row0[e], DEC_SEQ), :], pad], axis=0).astype(BF16)
              for e in range(2)}
        vv = {e: jnp.concatenate([vs_ref[elem[e]], vn_ref[pl.ds(row0[e], DEC_SEQ), :], pad], axis=0).astype(BF16)
              for e in range(2)}
        heads = {g: range(g * A_GROUP, (g + 1) * A_GROUP) for g in range(A_KV_HEADS)}
        ds = {g: slice(g * A_HEAD_DIM, (g + 1) * A_HEAD_DIM) for g in range(A_KV_HEADS)}
        sink = {g: sink_ref[g * rows:(g + 1) * rows, 0:1] for g in range(A_KV_HEADS)}
        qg = {(bb, g): jnp.concatenate([q[bb][:, h * A_HEAD_DIM:(h + 1) * A_HEAD_DIM] for h in heads[g]],
                                       axis=0).astype(BF16) for bb, g in chains}
        s = {(bb, g): lax.dot_general(qg[bb, g], kk[bb][:, ds[g]], nt, preferred_element_type=F32)
             for bb, g in chains}
        s = {ch: jnp.where(ok, s[ch], NEG_INF) for ch in chains}
        m = {(bb, g): jnp.maximum(jnp.max(s[bb, g], axis=1, keepdims=True), sink[g]) for bb, g in chains}
        p = {ch: jnp.exp(s[ch] - m[ch]) for ch in chains}
        denom = {(bb, g): jnp.sum(p[bb, g], axis=1, keepdims=True) + jnp.exp(sink[g] - m[bb, g])
                 for bb, g in chains}
        o = {(bb, g): jnp.dot(p[bb, g].astype(BF16), vv[bb][:, ds[g]], preferred_element_type=F32) / denom[bb, g]
             for bb, g in chains}
        for bb, g in chains:
            for k, h in enumerate(heads[g]):
                o_ref[pl.ds(row0[bb], DEC_SEQ), h * A_HEAD_DIM:(h + 1) * A_HEAD_DIM] = (
                    o[bb, g][k * DEC_SEQ:(k + 1) * DEC_SEQ, :].astype(o_ref.dtype))
        return carry

    lax.fori_loop(0, BB_ATTN // 2, body, 0)


def _attn_sample(qk, rest, sinks, k_state, v_state):
    rows = BB_ATTN * DEC_SEQ
    base = N_PROMPT // rows
    kcol = A_Q // A_KV
    vcol = R_VA // A_KV
    sink_rows = jnp.broadcast_to(jnp.repeat(sinks, DEC_SEQ)[:, None], (A_HEADS * DEC_SEQ, LANES))
    return pl.pallas_call(
        _attn_sample_kernel,
        grid=(DEC_BATCH // BB_ATTN,),
        in_specs=[
            pl.BlockSpec((A_HEADS * DEC_SEQ, LANES), lambda i: (0, 0)),
            pl.BlockSpec((rows, A_Q), lambda i: (base + i, 0)),
            pl.BlockSpec((rows, A_KV), lambda i: (base + i, kcol)),
            pl.BlockSpec((rows, A_KV), lambda i: (base + i, vcol)),
            pl.BlockSpec((BB_ATTN, WINDOW, A_KV), lambda i: (i, 0, 0)),
            pl.BlockSpec((BB_ATTN, WINDOW, A_KV), lambda i: (i, 0, 0)),
        ],
        out_specs=pl.BlockSpec((rows, A_Q), lambda i: (i, 0)),
        out_shape=jax.ShapeDtypeStruct((N_SAMPLE, A_Q), BF16),
        compiler_params=_params(("parallel",)),
        name="attn_sample",
    )(sink_rows, qk, qk, rest, k_state, v_state)


def _split3(x):
    hi = x.astype(BF16)
    r1 = x - hi.astype(F32)
    mid = r1.astype(BF16)
    lo = (r1 - mid.astype(F32)).astype(BF16)
    return hi, mid, lo


GLA_INPUTS = ((LANES, R_LR), (B_QK, R_QB), (B_QK, R_KB), (B_V, R_VB), (B_V, R_GB))


def _gla_kernel(*refs, n_chunks, group, chunk, layer, n_in):
    ins, rest_refs = refs[:n_in], refs[n_in:]
    per = n_in // len(GLA_INPUTS)
    s0_ref, wa_ref, ba_ref, g_ref = rest_refs[:4]
    prev_ref = rest_refs[4] if layer else None
    o_ref, sout_ref, st_ref = rest_refs[-3:]
    ci = pl.program_id(1)
    c = chunk

    def rows(inp, g):
        if per == 1:
            return ins[inp][g * c:(g + 1) * c, :]
        return ins[inp * per + g][...]

    single = n_chunks == 1
    if not single:
        @pl.when(ci == 0)
        def _():
            for g in range(group):
                for h in range(B_HEADS):
                    st_ref[g, h] = s0_ref[0, g, h].T

    ri = lax.broadcasted_iota(jnp.int32, (c, c), 0)
    cj = lax.broadcasted_iota(jnp.int32, (c, c), 1)
    causal = cj <= ri
    tri = jnp.where(causal, 1.0, 0.0).astype(BF16)
    nt = (((1,), (1,)), ((), ()))
    G = range(group)
    z = [jnp.dot(rows(0, g).astype(BF16), wa_ref[...], preferred_element_type=F32) + ba_ref[...] for g in G]
    log_a = [-(jnp.maximum(-z[g], 0.0) + jnp.log1p(jnp.exp(-jnp.abs(z[g])))) / B_TAU for g in G]
    parts = [_split3(log_a[g]) for g in G]
    b = [jnp.dot(tri, parts[g][0], preferred_element_type=F32)
         + jnp.dot(tri, parts[g][1], preferred_element_type=F32)
         + jnp.dot(tri, parts[g][2], preferred_element_type=F32) for g in G]
    bl = [b[g][c - 1:c, :] for g in G]
    qd = [(rows(1, g) * (B_DK ** -0.5) * jnp.exp(b[g])).astype(BF16) for g in G]
    kd = [(rows(2, g) * jnp.exp(-b[g])).astype(BF16) for g in G]
    kl = [rows(2, g) * jnp.exp(bl[g] - b[g]) for g in G]
    kl = [kl[g].T if single else kl[g].astype(BF16) for g in G]
    ebl = [jnp.exp(bl[g]) for g in G]
    for h in range(B_HEADS):
        ks = slice(h * B_DK, (h + 1) * B_DK)
        vs = slice(h * B_DV, (h + 1) * B_DV)
        vh = [rows(3, g)[:, vs] for g in G]
        if single:
            st = [s0_ref[0, g, h] for g in G]
            o = [jnp.dot(qd[g][:, ks], st[g].astype(BF16), preferred_element_type=F32) for g in G]
        else:
            st = [st_ref[g, h] for g in G]
            o = [lax.dot_general(qd[g][:, ks], st[g].astype(BF16), nt, preferred_element_type=F32) for g in G]
        att = [lax.dot_general(qd[g][:, ks], kd[g][:, ks], nt, preferred_element_type=F32) for g in G]
        att = [jnp.where(causal, att[g], 0.0).astype(BF16) for g in G]
        o = [o[g] + jnp.dot(att[g], vh[g].astype(BF16), preferred_element_type=F32) for g in G]
        for g in G:
            if single:
                decay = jnp.broadcast_to(ebl[g][:, ks], (SUBLANES, B_DK)).T[:, 0:1]
                for l in range(layer):
                    sout_ref[l, g, h] = prev_ref[l, g, h]
                sout_ref[layer, g, h] = st[g] * decay + jnp.dot(kl[g][ks, :], vh[g],
                                                                  preferred_element_type=F32)
            else:
                st_ref[g, h] = st[g] * ebl[g][:, ks] + jnp.dot(vh[g].T.astype(BF16), kl[g][:, ks],
                                                               preferred_element_type=F32)
        o = [o[g] * lax.rsqrt(jnp.mean(o[g] * o[g], axis=1, keepdims=True) + RMS_EPS) * g_ref[h:h + 1, :]
             for g in G]
        for g in G:
            gate = rows(4, g)[:, vs]
            o_ref[g, :, vs] = (o[g] * (gate / (1.0 + jnp.exp(-gate)))).astype(o_ref.dtype)

    if not single:
        @pl.when(ci == n_chunks - 1)
        def _():
            for g in range(group):
                for l in range(layer):
                    sout_ref[l, g] = prev_ref[l, g]
                for h in range(B_HEADS):
                    sout_ref[layer, g, h] = st_ref[g, h].T


def _gla(rest, s0, layer, prev_states, wa, ba, gain, *, n_batch, seq, chunk, row_base, group):
    n_chunks = seq // chunk
    contiguous = n_chunks == 1
    base = row_base // chunk

    def in_specs_for(width, col):
        if contiguous:
            return [pl.BlockSpec((group * chunk, width), lambda b, ci: (base // group + b, col // width))]
        return [pl.BlockSpec((chunk, width),
                             lambda b, ci, g=g: (base + (b * group + g) * n_chunks + ci, col // width))
                for g in range(group)]

    row_specs = [spec for width, col in GLA_INPUTS for spec in in_specs_for(width, col)]
    s0_layer = layer if s0.shape[0] > 1 else 0
    state_block = (group, B_HEADS, B_DK, B_DV)
    prev_specs = [pl.BlockSpec((layer,) + state_block, lambda b, ci: (0, b, 0, 0, 0))] if layer else []
    prev_args = [prev_states] if layer else []
    return pl.pallas_call(
        functools.partial(_gla_kernel, n_chunks=n_chunks, group=group, chunk=chunk, layer=layer,
                          n_in=len(row_specs)),
        grid=(n_batch // group, n_chunks),
        in_specs=row_specs + [
            pl.BlockSpec((1,) + state_block, lambda b, ci: (s0_layer, b, 0, 0, 0)),
            pl.BlockSpec((LANES, B_QK), lambda b, ci: (0, 0)),
            pl.BlockSpec((1, B_QK), lambda b, ci: (0, 0)),
            pl.BlockSpec((B_HEADS, B_DV), lambda b, ci: (0, 0)),
        ] + prev_specs,
        out_specs=[
            pl.BlockSpec((group, chunk, B_V), lambda b, ci: (b, ci, 0)),
            pl.BlockSpec((layer + 1,) + state_block, lambda b, ci: (0, b, 0, 0, 0)),
        ],
        out_shape=[
            jax.ShapeDtypeStruct((n_batch, seq, B_V), BF16),
            jax.ShapeDtypeStruct((layer + 1, n_batch, B_HEADS, B_DK, B_DV), F32),
        ],
        scratch_shapes=[pltpu.VMEM((group, B_HEADS, B_DV, B_DK), F32)],
        compiler_params=_params(("parallel", "arbitrary")),
        name="gla",
    )(*([rest] * len(row_specs)), s0, wa, ba, gain, *prev_args)


HALO = 16
BB_POOL = 16


def _pool_kernel(u_ref, prev_ref, w_ref, scale_ref, o_ref, *, from_start, n_seq):
    tp = u_ref.shape[0] // n_seq
    if from_start:
        ti = pl.program_id(1)
        t0 = ti * tp
    diffs = [[] for _ in POOL_WINDOWS]
    for sq in range(n_seq):
        u = u_ref[sq * tp:(sq + 1) * tp, :]
        prev = prev_ref[sq * HALO:(sq + 1) * HALO, :]
        if from_start:
            prev = jnp.where(ti > 0, prev, 0.0)
        full = jnp.concatenate([prev, u], axis=0)
        for g, w in enumerate(POOL_WINDOWS):
            cs = slice(g * C_GROUP_W, (g + 1) * C_GROUP_W)
            acc = full[:, cs]
            span = 1
            while span < w:
                acc = acc + pltpu.roll(acc, span, axis=0)
                span *= 2
            wsum = acc[HALO:, :]
            if from_start:
                t = t0 + lax.broadcasted_iota(jnp.int32, (tp, C_GROUP_W), 0)
                cnt = jnp.minimum(t + 1, w).astype(F32)
            else:
                cnt = float(w)
            diffs[g].append(wsum / cnt - u[:, cs])
    for g in range(C_GROUPS):
        cs = slice(g * C_GROUP_W, (g + 1) * C_GROUP_W)
        d = jnp.concatenate(diffs[g], axis=0) if n_seq > 1 else diffs[g][0]
        y = jnp.dot(d.astype(BF16), w_ref[g], preferred_element_type=F32) * scale_ref[:, cs]
        o_ref[:, cs] = y.astype(o_ref.dtype)


def _pool_prompt(rest, w, scale):
    nt = SEQ // TP_POOL
    ucol = R_UC // C_WIDTH

    def halo(b, i):
        return (jnp.maximum((b * SEQ + i * TP_POOL) // HALO - 1, 0), ucol)

    return pl.pallas_call(
        functools.partial(_pool_kernel, from_start=True, n_seq=1),
        grid=(BATCH, nt),
        in_specs=[
            pl.BlockSpec((TP_POOL, C_WIDTH), lambda b, i: (b * nt + i, ucol)),
            pl.BlockSpec((HALO, C_WIDTH), halo),
            pl.BlockSpec((C_GROUPS, C_GROUP_W, C_GROUP_W), lambda b, i: (0, 0, 0)),
            pl.BlockSpec((1, C_WIDTH), lambda b, i: (0, 0)),
        ],
        out_specs=pl.BlockSpec((TP_POOL, C_WIDTH), lambda b, i: (b * nt + i, 0)),
        out_shape=jax.ShapeDtypeStruct((N_PROMPT, C_WIDTH), BF16),
        compiler_params=_params(("parallel", "arbitrary")),
        name="pool_prompt",
    )(rest, rest, w, scale)


def _pool_sample(rest, prev, w, scale):
    ucol = R_UC // C_WIDTH
    rows = BB_POOL * DEC_SEQ
    base = N_PROMPT // rows
    return pl.pallas_call(
        functools.partial(_pool_kernel, from_start=False, n_seq=BB_POOL),
        grid=(DEC_BATCH // BB_POOL,),
        in_specs=[
            pl.BlockSpec((rows, C_WIDTH), lambda b: (base + b, ucol)),
            pl.BlockSpec((BB_POOL * HALO, C_WIDTH), lambda b: (b, 0)),
            pl.BlockSpec((C_GROUPS, C_GROUP_W, C_GROUP_W), lambda b: (0, 0, 0)),
            pl.BlockSpec((1, C_WIDTH), lambda b: (0, 0)),
        ],
        out_specs=pl.BlockSpec((rows, C_WIDTH), lambda b: (b, 0)),
        out_shape=jax.ShapeDtypeStruct((N_SAMPLE, C_WIDTH), BF16),
        compiler_params=_params(("parallel",)),
        name="pool_sample",
    )(rest, prev, w, scale)


def _layer_norm(x, g, b):
    mu = jnp.mean(x, axis=1, keepdims=True)
    xc = x - mu
    var = jnp.mean(xc * xc, axis=1, keepdims=True)
    return xc * lax.rsqrt(var + LN_EPS) * g + b


def _merge_kernel(x_ref, gates_ref, oa_p_ref, oa_s_ref, ob_p_ref, ob_s_ref, oc_p_ref, oc_s_ref,
                  wa_ref, wb_ref, wc_ref, wo_ref, g_ref, b_ref, o_ref, oT_ref):
    in_sample = pl.program_id(0) >= N_PROMPT // TM_MERGE

    def gate(i):
        z = gates_ref[:, i * D_MODEL:(i + 1) * D_MODEL]
        return 1.0 / (1.0 + jnp.exp(-z))

    def branch(p_ref, s_ref, w_ref):
        o = jnp.where(in_sample, s_ref[...], p_ref[...])
        return jnp.dot(o, w_ref[...], preferred_element_type=F32)

    merged = (gate(0) * branch(oa_p_ref, oa_s_ref, wa_ref)
              + gate(1) * branch(ob_p_ref, ob_s_ref, wb_ref)
              + gate(2) * branch(oc_p_ref, oc_s_ref, wc_ref))
    mix = jnp.dot(merged.astype(BF16), wo_ref[...], preferred_element_type=F32)
    y = _layer_norm(DN_ALPHA * x_ref[...] + mix, g_ref[...], b_ref[...])
    o_ref[...] = y
    oT_ref[...] = y.T.astype(BF16)


def _merge(x, rest, branches, wa, wb, wc, wo, g, b):
    n_p = N_PROMPT // TM_MERGE
    row = pl.BlockSpec((TM_MERGE, D_MODEL), lambda i: (i, 0))
    row_p = pl.BlockSpec((TM_MERGE, D_MODEL), lambda i: (jnp.minimum(i, n_p - 1), 0))
    row_s = pl.BlockSpec((TM_MERGE, D_MODEL), lambda i: (jnp.maximum(i - n_p, 0), 0))
    wspec = pl.BlockSpec((D_MODEL, D_MODEL), lambda i: (0, 0))
    vec = pl.BlockSpec((1, D_MODEL), lambda i: (0, 0))
    (oa_p, oa_s), (ob_p, ob_s), (oc_p, oc_s) = branches
    return pl.pallas_call(
        _merge_kernel,
        grid=(N_TOK // TM_MERGE,),
        in_specs=[row, pl.BlockSpec((TM_MERGE, 3 * D_MODEL), lambda i: (i, R_GATES)),
                  row_p, row_s, row_p, row_s, row_p, row_s, wspec, wspec, wspec, wspec, vec, vec],
        out_specs=[row, pl.BlockSpec((D_MODEL, TM_MERGE), lambda i: (0, i))],
        out_shape=[jax.ShapeDtypeStruct((N_TOK, D_MODEL), F32),
                   jax.ShapeDtypeStruct((D_MODEL, N_TOK), BF16)],
        compiler_params=_params(("parallel",)),
        name="merge",
    )(x, rest, oa_p, oa_s, ob_p, ob_s, oc_p, oc_s, wa, wb, wc, wo, g, b)


def _merge_sort_pairs(n):
    size = 1
    while size < n:
        size *= 2
    pairs = []
    p = 1
    while p < size:
        k = p
        while k >= 1:
            for j in range(k % p, size - k, 2 * k):
                for i in range(min(k, size - j - k)):
                    if (i + j) // (2 * p) == (i + j + k) // (2 * p):
                        pairs.append((i + j, i + j + k))
            k //= 2
        p *= 2
    return [(a, b) for a, b in pairs if b < n]


def _top_desc(s, n):
    v = [s[SUBLANES * k:SUBLANES * (k + 1), :] for k in range(s.shape[0] // SUBLANES)]
    depth = len(v)
    for a, b in _merge_sort_pairs(depth):
        v[a], v[b] = jnp.maximum(v[a], v[b]), jnp.minimum(v[a], v[b])
    vals = []
    for r in range(n):
        m = jnp.max(v[0], axis=0, keepdims=True)
        vals.append(m)
        hit = v[0] == m
        needed = n - r - 1
        for k in range(min(depth - 1, needed)):
            v[k] = jnp.where(hit, v[k + 1], v[k])
        if needed >= depth:
            v[depth - 1] = jnp.where(hit, -jnp.inf, v[depth - 1])
    return vals


def _peer_topk_kernel(xT_ref, wq_ref, sk_ref, cnt_ref, e1_ref, rank_ref, e2_ref):
    tt = xT_ref.shape[1]
    row8 = lax.broadcasted_iota(jnp.int32, (SUBLANES, tt), 0)

    def scores(hp):
        qT = jnp.dot(wq_ref[hp * PEER_DHALF:(hp + 1) * PEER_DHALF, :], xT_ref[...], preferred_element_type=F32)
        return jnp.dot(sk_ref[hp], qT.astype(BF16), preferred_element_type=F32)

    for h in range(PEER_HEADS):
        s1, s2 = scores(2 * h), scores(2 * h + 1)
        vals1 = _top_desc(s1, PEER_TOPK)
        vals2 = _top_desc(s2, PEER_TOPK)
        e2_ref[h] = jnp.exp(s2 - vals2[0]).astype(BF16)
        sv1 = jnp.concatenate(vals1, axis=0)
        sv2 = jnp.concatenate(vals2, axis=0)
        pieces = [sv1[0:1, :] + sv2]
        for a in range(1, SUBLANES):
            nb = PEER_TOPK // (a + 1)
            pieces.append(jnp.where(row8 < nb, sv1[a:a + 1, :] + sv2[0:SUBLANES, :], -jnp.inf))
        pieces.append(sv2[0:1, :] + sv1[SUBLANES:PEER_TOPK, :])
        cand = _top_desc(jnp.concatenate(pieces, axis=0), PEER_TOPK)
        top = cand[0]
        z = jnp.zeros_like(top)
        for r in range(PEER_TOPK):
            z = z + jnp.exp(cand[r] - top)
        kth = cand[PEER_TOPK - 1]
        counts = [jnp.sum(jnp.where(p >= kth, 1.0, 0.0), axis=0, keepdims=True) for p in pieces[:SUBLANES]]
        last = PEER_TOPK - 1
        cnt = jnp.where(s1 >= sv1[last:last + 1, :], jnp.where(s1 + sv2[0:1, :] >= kth, 1.0, 0.0), 0.0)
        rank = jnp.where(s2 >= sv2[last:last + 1, :],
                         jnp.where(sv1[0:1, :] + s2 >= kth, float(SUBLANES), float(PEER_TOPK)), float(PEER_TOPK))
        for a in range(SUBLANES):
            cnt = jnp.where(s1 == sv1[a:a + 1, :], counts[a], cnt)
            rank = jnp.where(s2 == sv2[a:a + 1, :], float(a), rank)
        rank_ref[h] = rank.astype(BF16)
        e1 = jnp.exp(s1 - (sv1[0:1, :] + jnp.log(z))) * SQRT_HALF
        for c in range(tt // LANES):
            cnt_ref[h, c] = cnt[:, c * LANES:(c + 1) * LANES]
            e1_ref[h, c] = e1[:, c * LANES:(c + 1) * LANES]


def _peer_topk(xT, wqT, sk):
    spec = pl.BlockSpec((PEER_HEADS, N_KEYS, TT_TOPK), lambda t: (0, 0, t))
    row_spec = pl.BlockSpec((PEER_HEADS, TT_TOPK // LANES, N_KEYS, LANES), lambda t: (0, t, 0, 0))
    wide = jax.ShapeDtypeStruct((PEER_HEADS, N_TOK // LANES, N_KEYS, LANES), F32)
    narrow = jax.ShapeDtypeStruct((PEER_HEADS, N_KEYS, N_TOK), BF16)
    return pl.pallas_call(
        _peer_topk_kernel,
        grid=(N_TOK // TT_TOPK,),
        in_specs=[
            pl.BlockSpec((D_MODEL, TT_TOPK), lambda t: (0, t)),
            pl.BlockSpec((PEER_HEADS * PEER_DKEY, D_MODEL), lambda t: (0, 0)),
            pl.BlockSpec((2 * PEER_HEADS, N_KEYS, PEER_DHALF), lambda t: (0, 0, 0)),
        ],
        out_specs=[row_spec, row_spec, spec, spec],
        out_shape=[wide, wide, narrow, narrow],
        compiler_params=_params(("parallel",)),
        name="peer_topk",
    )(xT, wqT, sk)


SQRT_HALF = 2.0 ** -0.5


def _gelu_unscaled(x):
    t = x * SQRT_HALF
    return t * (1.0 + lax.erf(t))


LC_PEER = 256


def _bf16_rows(ref, h, r, chunks, n_rows):
    x = jnp.concatenate([ref[h, c, pl.ds(r, 2 * SUBLANES, stride=0), :] for c in chunks], axis=1)
    packed = x.astype(BF16)
    return jnp.concatenate([packed] * (n_rows // packed.shape[0]), axis=0)


MM_PIECE = 512


def _peer_main_kernel(xT_ref, u_ref, vt_ref, cnt_ref, e1_ref, rank_ref, e2_ref, x_ref, g_ref, b_ref,
                      o_ref, ob_ref, yT_ref, s_scr, wh_scr):
    i = pl.program_id(1)
    tt = xT_ref.shape[1]

    @pl.when(i == 0)
    def _():
        yT_ref[...] = jnp.zeros_like(yT_ref)

    zero = jnp.zeros((N_KEYS, LC_PEER), BF16)
    per_piece = MM_PIECE // N_KEYS
    for ii in range(IB_PEER):
        rows = slice(ii * N_KEYS, (ii + 1) * N_KEYS)
        if ii % per_piece == 0:
            piece = slice(ii * N_KEYS, ii * N_KEYS + MM_PIECE)
            s_scr[piece, :] = jnp.dot(u_ref[0, piece, :], xT_ref[...], preferred_element_type=F32)
        for lc in range(tt // LC_PEER):
            cols = slice(lc * LC_PEER, (lc + 1) * LC_PEER)
            chunks = range(lc * LC_PEER // LANES, (lc + 1) * LC_PEER // LANES)
            w = zero
            for h in range(PEER_HEADS):
                cnt = _bf16_rows(cnt_ref, h, ii, chunks, N_KEYS)
                e1 = _bf16_rows(e1_ref, h, ii, chunks, N_KEYS)
                w = w + jnp.where(rank_ref[h, :, cols] < cnt, e2_ref[h, :, cols] * e1, zero)
            wh_scr[rows, cols] = w
    for ii in range(IB_PEER):
        rows = slice(ii * N_KEYS, (ii + 1) * N_KEYS)
        wh_scr[rows, :] = wh_scr[rows, :] * _gelu_unscaled(s_scr[rows, :]).astype(BF16)
    yT_ref[...] += jnp.dot(vt_ref[0], wh_scr[...], preferred_element_type=F32)

    @pl.when(i == pl.num_programs(1) - 1)
    def _():
        y = _layer_norm(DN_ALPHA * x_ref[...] + yT_ref[...].T, g_ref[...], b_ref[...])
        o_ref[...] = y
        ob_ref[...] = y.astype(BF16)


def _peer_main(xT, u, vt, layer, cnt, e1, rank, e2, x1, g, b):
    eb = IB_PEER * N_KEYS
    row_spec = pl.BlockSpec((PEER_HEADS, TT_PEER // LANES, IB_PEER, LANES), lambda t, i: (0, t, i, 0))
    tab_spec = pl.BlockSpec((PEER_HEADS, N_KEYS, TT_PEER), lambda t, i: (0, 0, t))
    tok_spec = pl.BlockSpec((TT_PEER, D_MODEL), lambda t, i: (t, 0))
    vec = pl.BlockSpec((1, D_MODEL), lambda t, i: (0, 0))
    return pl.pallas_call(
        _peer_main_kernel,
        grid=(N_TOK // TT_PEER, N_EXPERTS // eb),
        in_specs=[
            pl.BlockSpec((D_MODEL, TT_PEER), lambda t, i: (0, t)),
            pl.BlockSpec((1, eb, D_MODEL), lambda t, i: (layer, i, 0)),
            pl.BlockSpec((1, D_MODEL, eb), lambda t, i: (layer, 0, i)),
            row_spec, row_spec, tab_spec, tab_spec, tok_spec, vec, vec,
        ],
        out_specs=[tok_spec, tok_spec],
        out_shape=[jax.ShapeDtypeStruct((N_TOK, D_MODEL), F32),
                   jax.ShapeDtypeStruct((N_TOK, D_MODEL), BF16)],
        scratch_shapes=[pltpu.VMEM((D_MODEL, TT_PEER), F32),
                        pltpu.VMEM((eb, TT_PEER), F32), pltpu.VMEM((eb, TT_PEER), BF16)],
        compiler_params=_params(("parallel", "arbitrary")),
        name="peer_main",
    )(xT, u, vt, cnt, e1, rank, e2, x1, g, b)


def _rope_tables():
    half = ROT_DIM // 2
    pos = jnp.concatenate([jnp.arange(SEQ), PAST_LEN + (jnp.arange(TM_QK) % DEC_SEQ)])
    inv = ROPE_THETA ** (-jnp.arange(half, dtype=F32) / half)
    ang = pos.astype(F32)[:, None] * inv[None, :]
    cos, sin = jnp.cos(ang), jnp.sin(ang)
    n = pos.shape[0]
    one = jnp.ones((n, A_HEAD_DIM - ROT_DIM), F32)
    zero = jnp.zeros((n, A_HEAD_DIM - ROT_DIM), F32)
    zh = jnp.zeros((n, half), F32)
    reps = LANES // A_HEAD_DIM
    c = jnp.tile(jnp.concatenate([cos, cos, one], 1), (1, reps))
    s1 = jnp.tile(jnp.concatenate([-sin, zh, zero], 1), (1, reps))
    s2 = jnp.tile(jnp.concatenate([zh, sin, zero], 1), (1, reps))
    return c, s1, s2


def _split_cols(w):
    cuts = [int(c) for c in np.cumsum(SPLITS)[:-1]]
    return jnp.split(w, cuts, axis=-1)


def _layer(layer, x, xb, k_state, v_state, gla_states, gla_out, pool_state, rope, w_in, b_in, sinks,
           w_alpha, b_alpha, gla_g, w_pool, pool_scale, w_a, w_b, w_c, w_out, ln1_g, ln1_b,
           peer_query, peer_subkeys, peer_u, peer_vt, ln2_g, ln2_b):
    qa_w, ka_w, va_w, qb_w, kb_w, vb_w, lr_w, gb_w, uc_w, gates_w = _split_cols(w_in)
    qa_b, ka_b, va_b, qb_b, kb_b, vb_b, lr_b, gb_b, uc_b, gates_b = _split_cols(b_in[None, :])
    lr_pad = LANES - B_GATE_RANK
    w_qk = jnp.concatenate([qa_w, ka_w], 1).astype(BF16)
    b_qk = jnp.concatenate([qa_b, ka_b], 1)
    w_rest = jnp.concatenate([gates_w, vb_w, gb_w, uc_w, qb_w, kb_w, va_w,
                              jnp.pad(lr_w, ((0, 0), (0, lr_pad)))], 1).astype(BF16)
    b_rest = jnp.concatenate([gates_b, vb_b, gb_b, uc_b, qb_b, kb_b, va_b,
                              jnp.pad(lr_b, ((0, 0), (0, lr_pad)))], 1)

    qk = _proj_qk(xb, w_qk, b_qk, *rope)
    rest = _proj_rest(xb, w_rest, b_rest)

    ks = k_state.reshape(DEC_BATCH, WINDOW, A_KV)
    vs = v_state.reshape(DEC_BATCH, WINDOW, A_KV)
    oa = (_attn_prompt(qk, rest, sinks), _attn_sample(qk, rest, sinks, ks, vs))

    wa = jnp.pad(w_alpha, ((0, lr_pad), (0, 0))).astype(BF16)
    ba = b_alpha[None, :]
    gla_p, gla_s = gla_out
    ob_p, gla_p = _gla(rest, jnp.zeros((1, BATCH, B_HEADS, B_DK, B_DV), F32), layer, gla_p, wa, ba, gla_g,
                       n_batch=BATCH, seq=SEQ, chunk=GLA_CHUNK, row_base=0, group=GLA_GROUP_PROMPT)
    ob_s, gla_s = _gla(rest, gla_states, layer, gla_s, wa, ba, gla_g, n_batch=DEC_BATCH, seq=DEC_SEQ,
                       chunk=math.gcd(DEC_SEQ, GLA_CHUNK), row_base=N_PROMPT, group=GLA_GROUP_SAMPLE)
    ob = (ob_p.reshape(N_PROMPT, B_V), ob_s.reshape(N_SAMPLE, B_V))

    wp = w_pool.astype(BF16)
    ps = pool_scale[None, :]
    prev = jnp.pad(pool_state, ((0, 0), (HALO - POOL_STATE, 0), (0, 0))).reshape(DEC_BATCH * HALO, C_WIDTH)
    oc = (_pool_prompt(rest, wp, ps), _pool_sample(rest, prev, wp, ps))

    x1, x1T = _merge(x, rest, (oa, ob, oc), w_a.astype(BF16), w_b.astype(BF16), w_c.astype(BF16),
                     w_out.astype(BF16), ln1_g[None, :], ln1_b[None, :])

    wqT = peer_query.reshape(D_MODEL, PEER_HEADS * PEER_DKEY).T.astype(BF16)
    sk = peer_subkeys.reshape(2 * PEER_HEADS, N_KEYS, PEER_DHALF).astype(BF16)
    cnt, e1, rank, e2 = _peer_topk(x1T, wqT, sk)
    x2, x2b = _peer_main(x1T, peer_u, peer_vt, layer, cnt, e1, rank, e2, x1, ln2_g[None, :], ln2_b[None, :])

    def prompt_tail(t, col0, width, n):
        return jnp.stack([t[(b + 1) * SEQ - n:(b + 1) * SEQ, col0:col0 + width] for b in range(BATCH)])

    def sample_tail(state, t, col0, width, n):
        new = t[N_PROMPT:, col0:col0 + width].reshape(DEC_BATCH, DEC_SEQ, width)
        return jnp.concatenate([state, new], 1)[:, -n:]

    kv_shape = (-1, WINDOW, A_KV_HEADS, A_HEAD_DIM)
    states = (prompt_tail(qk, A_Q, A_KV, WINDOW).reshape(kv_shape),
              prompt_tail(rest, R_VA, A_KV, WINDOW).reshape(kv_shape),
              prompt_tail(rest, R_UC, C_WIDTH, POOL_STATE),
              sample_tail(ks, qk, A_Q, A_KV, WINDOW).reshape(kv_shape),
              sample_tail(vs, rest, R_VA, A_KV, WINDOW).reshape(kv_shape),
              sample_tail(pool_state, rest, R_UC, C_WIDTH, POOL_STATE))
    return x2, x2b, states, (gla_p, gla_s)


def kernel(x_prompt, x_sample, state_win_k, state_win_v, state_gla, state_pool, w_in, b_in, attn_sinks,
           w_alpha, b_alpha, gla_norm_g, w_pool, pool_scale, w_branch_a, w_branch_b, w_branch_c, w_out,
           ln1_g, ln1_b, peer_query, peer_subkeys, peer_u, peer_v, ln2_g, ln2_b):
    x = jnp.concatenate([x_prompt.reshape(N_PROMPT, D_MODEL), x_sample.reshape(N_SAMPLE, D_MODEL)], 0)
    xb = x.astype(BF16)
    rope = _rope_tables()
    peer_ub = peer_u.astype(BF16)
    peer_vtb = jnp.swapaxes(peer_v, 1, 2).astype(BF16)
    per_layer = []
    gla_out = (None, None)
    for l in range(DEPTH):
        x, xb, states, gla_out = _layer(
            l, x, xb, state_win_k[l], state_win_v[l], state_gla, gla_out, state_pool[l], rope,
            w_in[l], b_in[l], attn_sinks[l], w_alpha[l], b_alpha[l], gla_norm_g[l],
            w_pool[l], pool_scale[l], w_branch_a[l], w_branch_b[l], w_branch_c[l], w_out[l],
            ln1_g[l], ln1_b[l], peer_query[l], peer_subkeys[l], peer_ub, peer_vtb, ln2_g[l], ln2_b[l])
        per_layer.append(states)
    pk, pv, pp, sk, sv, sp = [jnp.stack([per_layer[l][i] for l in range(DEPTH)]) for i in range(6)]
    return (x[:N_PROMPT].reshape(BATCH, SEQ, D_MODEL), x[N_PROMPT:].reshape(DEC_BATCH, DEC_SEQ, D_MODEL),
            pk, pv, gla_out[0], pp, sk, sv, gla_out[1], sp)
```
